```python
import jax, jax.numpy as jnp
from jax import lax
import numpy as np

D_MODEL = 1024
BATCH = 8
SEQ = 8192
DEPTH = 2

D_FF = 2816
D_MIX = D_MODEL
POOL_CH = D_MIX // 2
POOL_WINDOWS = (2, 4, 8, 16)
POOL_GROUPS = len(POOL_WINDOWS)
POOL_GC = POOL_CH // POOL_GROUPS
CONV_CH = D_MIX - POOL_CH
CONV_WIDTH = 31
AB_IN = POOL_CH + 2 * CONV_CH
SGU_CH = D_MIX
SGU_HEADS = 8
SGU_HC = SGU_CH // SGU_HEADS
CHUNK = 128
N_EVEN = (DEPTH + 1) // 2
N_ODD = DEPTH // 2
EPS = 1e-6

kernel_name = "hybrid_pool_conv_sgu_macaron"


def rms_norm(x, g):
    xf = x.astype(jnp.float32)
    y = xf * lax.rsqrt(jnp.mean(xf * xf, axis=-1, keepdims=True) + EPS)
    return (y * g.astype(jnp.float32)).astype(x.dtype)


def layer_norm(x, g, b):
    xf = x.astype(jnp.float32)
    mu = jnp.mean(xf, axis=-1, keepdims=True)
    var = jnp.mean(jnp.square(xf - mu), axis=-1, keepdims=True)
    y = (xf - mu) * lax.rsqrt(var + EPS)
    return (y * g.astype(jnp.float32) + b.astype(jnp.float32)).astype(x.dtype)


def swiglu_ffn(x, w_in, w_out):
    gate, up = jnp.split(x @ w_in, 2, axis=-1)
    return (jax.nn.silu(gate) * up) @ w_out


def pool_mixer(u, w, b, scale):
    bsz, t_len, _ = u.shape
    csum = jnp.cumsum(u.astype(jnp.float32), axis=1)
    pos = jnp.arange(t_len, dtype=jnp.int32)
    means = []
    for g, win in enumerate(POOL_WINDOWS):
        cg = csum[..., g * POOL_GC:(g + 1) * POOL_GC]
        shifted = jnp.pad(cg[:, :t_len - win], ((0, 0), (win, 0), (0, 0)))
        count = jnp.minimum(pos + 1, win).astype(jnp.float32)[None, :, None]
        means.append((cg - shifted) / count)
    pooled = jnp.concatenate(means, axis=-1).astype(u.dtype) - u
    pooled = pooled.reshape(bsz, t_len, POOL_GROUPS, POOL_GC)
    mixed = jnp.einsum('btgc,gcd->btgd', pooled, w) + b
    return mixed.reshape(bsz, t_len, POOL_CH) * scale


def conv_module(h, conv_w, conv_b, ln_g, ln_b):
    a, gate = jnp.split(h, 2, axis=-1)
    g = a * jax.nn.sigmoid(gate)
    y = lax.conv_general_dilated(
        g, conv_w[:, None, :], window_strides=(1,), padding=((CONV_WIDTH - 1, 0),),
        dimension_numbers=('NWC', 'WIO', 'NWC'), feature_group_count=CONV_CH) + conv_b
    return jax.nn.silu(layer_norm(y, ln_g, ln_b))


def pool_conv_mixer(xn, w_in, pool_w, pool_b, pool_scale, conv_w, conv_b, ln_g, ln_b, w_out):
    h = xn @ w_in
    ya = pool_mixer(h[..., :POOL_CH], pool_w, pool_b, pool_scale)
    yb = conv_module(h[..., POOL_CH:], conv_w, conv_b, ln_g, ln_b)
    return jnp.concatenate([ya, yb], axis=-1) @ w_out


def sgu_mixer(xn, w_in, ln_g, ln_b, w_s, b_s, w_out):
    bsz, t_len, _ = xn.shape
    z = jax.nn.gelu(xn @ w_in, approximate=False)
    u, v = jnp.split(z, 2, axis=-1)
    v = layer_norm(v, ln_g, ln_b)
    v = v.reshape(bsz, t_len // CHUNK, CHUNK, SGU_HEADS, SGU_HC)
    mask = jnp.tril(jnp.ones((CHUNK, CHUNK), dtype=w_s.dtype))
    w_masked = w_s * mask
    v = jnp.einsum('hst,bnthc->bnshc', w_masked, v) + b_s.T[None, None, :, :, None]
    v = v.reshape(bsz, t_len, SGU_CH)
    return (u * v) @ w_out


def _fwd_setup_inputs(seed: int = 0) -> dict:
    key = jax.random.key(seed)
    ks = iter(jax.random.split(key, 32))

    def nrm(shape, scale):
        return jax.random.normal(next(ks), shape, jnp.float32) * scale

    def gain(shape):
        return 1.0 + nrm(shape, 0.05)

    return {
        "x": nrm((BATCH, SEQ, D_MODEL), 1.0),
        "ffn1_norm": gain((DEPTH, D_MODEL)),
        "ffn1_w_in": nrm((DEPTH, D_MODEL, 2 * D_FF), D_MODEL ** -0.5),
        "ffn1_w_out": nrm((DEPTH, D_FF, D_MODEL), D_FF ** -0.5),
        "mix_norm": gain((DEPTH, D_MODEL)),
        "ffn2_norm": gain((DEPTH, D_MODEL)),
        "ffn2_w_in": nrm((DEPTH, D_MODEL, 2 * D_FF), D_MODEL ** -0.5),
        "ffn2_w_out": nrm((DEPTH, D_FF, D_MODEL), D_FF ** -0.5),
        "ab_w_in": nrm((N_EVEN, D_MODEL, AB_IN), D_MODEL ** -0.5),
        "pool_w": nrm((N_EVEN, POOL_GROUPS, POOL_GC, POOL_GC), POOL_GC ** -0.5),
        "pool_b": nrm((N_EVEN, POOL_GROUPS, POOL_GC), 0.02),
        "pool_scale": 0.5 + nrm((N_EVEN, POOL_CH), 0.05),
        "conv_w": nrm((N_EVEN, CONV_WIDTH, CONV_CH), CONV_WIDTH ** -0.5),
        "conv_b": nrm((N_EVEN, CONV_CH), 0.02),
        "conv_ln_g": gain((N_EVEN, CONV_CH)),
        "conv_ln_b": nrm((N_EVEN, CONV_CH), 0.02),
        "ab_w_out": nrm((N_EVEN, D_MIX, D_MODEL), D_MIX ** -0.5),
        "sgu_w_in": nrm((N_ODD, D_MODEL, 2 * SGU_CH), D_MODEL ** -0.5),
        "sgu_ln_g": gain((N_ODD, SGU_CH)),
        "sgu_ln_b": nrm((N_ODD, SGU_CH), 0.02),
        "sgu_w": nrm((N_ODD, SGU_HEADS, CHUNK, CHUNK), CHUNK ** -0.5),
        "sgu_b": 1.0 + nrm((N_ODD, SGU_HEADS, CHUNK), 0.05),
        "sgu_w_out": nrm((N_ODD, SGU_CH, D_MODEL), SGU_CH ** -0.5),
        "final_norm": gain((D_MODEL,)),
    }


def _fwd_reference(x, ffn1_norm, ffn1_w_in, ffn1_w_out, mix_norm, ffn2_norm, ffn2_w_in, ffn2_w_out,
              ab_w_in, pool_w, pool_b, pool_scale, conv_w, conv_b, conv_ln_g, conv_ln_b, ab_w_out,
              sgu_w_in, sgu_ln_g, sgu_ln_b, sgu_w, sgu_b, sgu_w_out, final_norm):
    for i in range(DEPTH):
        x = x + 0.5 * swiglu_ffn(rms_norm(x, ffn1_norm[i]), ffn1_w_in[i], ffn1_w_out[i])
        xn = rms_norm(x, mix_norm[i])
        if i % 2 == 0:
            j = i // 2
            y = pool_conv_mixer(xn, ab_w_in[j], pool_w[j], pool_b[j], pool_scale[j],
                                conv_w[j], conv_b[j], conv_ln_g[j], conv_ln_b[j], ab_w_out[j])
        else:
            j = i // 2
            y = sgu_mixer(xn, sgu_w_in[j], sgu_ln_g[j], sgu_ln_b[j], sgu_w[j], sgu_b[j], sgu_w_out[j])
        x = x + y
        x = x + 0.5 * swiglu_ffn(rms_norm(x, ffn2_norm[i]), ffn2_w_in[i], ffn2_w_out[i])
    return rms_norm(x, final_norm)


import jax as _jax
import jax.numpy as _jnp

TWIN_FORMAT = 'train_step'
FWD_PARAMS = ['x', 'ffn1_norm', 'ffn1_w_in', 'ffn1_w_out', 'mix_norm', 'ffn2_norm', 'ffn2_w_in', 'ffn2_w_out', 'ab_w_in', 'pool_w', 'pool_b', 'pool_scale', 'conv_w', 'conv_b', 'conv_ln_g', 'conv_ln_b', 'ab_w_out', 'sgu_w_in', 'sgu_ln_g', 'sgu_ln_b', 'sgu_w', 'sgu_b', 'sgu_w_out', 'final_norm']
TWIN_WEIGHTS = ['ffn1_norm', 'ffn1_w_in', 'ffn1_w_out', 'mix_norm', 'ffn2_norm', 'ffn2_w_in', 'ffn2_w_out', 'ab_w_in', 'pool_w', 'pool_b', 'pool_scale', 'conv_w', 'conv_b', 'conv_ln_g', 'conv_ln_b', 'ab_w_out', 'sgu_w_in', 'sgu_ln_g', 'sgu_ln_b', 'sgu_w', 'sgu_b', 'sgu_w_out', 'final_norm']
TWIN_DIFF_INPUT = 'x'
TWIN_INPUTS = ['x', 'ffn1_norm', 'ffn1_w_in', 'ffn1_w_out', 'mix_norm', 'ffn2_norm', 'ffn2_w_in', 'ffn2_w_out', 'ab_w_in', 'pool_w', 'pool_b', 'pool_scale', 'conv_w', 'conv_b', 'conv_ln_g', 'conv_ln_b', 'ab_w_out', 'sgu_w_in', 'sgu_ln_g', 'sgu_ln_b', 'sgu_w', 'sgu_b', 'sgu_w_out', 'final_norm', 'loss_target', 'm_ffn1_norm', 'm_ffn1_w_in', 'm_ffn1_w_out', 'm_mix_norm', 'm_ffn2_norm', 'm_ffn2_w_in', 'm_ffn2_w_out', 'm_ab_w_in', 'm_pool_w', 'm_pool_b', 'm_pool_scale', 'm_conv_w', 'm_conv_b', 'm_conv_ln_g', 'm_conv_ln_b', 'm_ab_w_out', 'm_sgu_w_in', 'm_sgu_ln_g', 'm_sgu_ln_b', 'm_sgu_w', 'm_sgu_b', 'm_sgu_w_out', 'm_final_norm', 'v_ffn1_norm', 'v_ffn1_w_in', 'v_ffn1_w_out', 'v_mix_norm', 'v_ffn2_norm', 'v_ffn2_w_in', 'v_ffn2_w_out', 'v_ab_w_in', 'v_pool_w', 'v_pool_b', 'v_pool_scale', 'v_conv_w', 'v_conv_b', 'v_conv_ln_g', 'v_conv_ln_b', 'v_ab_w_out', 'v_sgu_w_in', 'v_sgu_ln_g', 'v_sgu_ln_b', 'v_sgu_w', 'v_sgu_b', 'v_sgu_w_out', 'v_final_norm']
TWIN_OUTPUTS = ['loss', 'grad_x', 'grad_ffn1_norm', 'grad_ffn1_w_in', 'grad_ffn1_w_out', 'grad_mix_norm', 'grad_ffn2_norm', 'grad_ffn2_w_in', 'grad_ffn2_w_out', 'grad_ab_w_in', 'grad_pool_w', 'grad_pool_b', 'grad_pool_scale', 'grad_conv_w', 'grad_conv_b', 'grad_conv_ln_g', 'grad_conv_ln_b', 'grad_ab_w_out', 'grad_sgu_w_in', 'grad_sgu_ln_g', 'grad_sgu_ln_b', 'grad_sgu_w', 'grad_sgu_b', 'grad_sgu_w_out', 'grad_final_norm', 'delta_ffn1_norm', 'delta_ffn1_w_in', 'delta_ffn1_w_out', 'delta_mix_norm', 'delta_ffn2_norm', 'delta_ffn2_w_in', 'delta_ffn2_w_out', 'delta_ab_w_in', 'delta_pool_w', 'delta_pool_b', 'delta_pool_scale', 'delta_conv_w', 'delta_conv_b', 'delta_conv_ln_g', 'delta_conv_ln_b', 'delta_ab_w_out', 'delta_sgu_w_in', 'delta_sgu_ln_g', 'delta_sgu_ln_b', 'delta_sgu_w', 'delta_sgu_b', 'delta_sgu_w_out', 'delta_final_norm', 'new_m_ffn1_norm', 'new_m_ffn1_w_in', 'new_m_ffn1_w_out', 'new_m_mix_norm', 'new_m_ffn2_norm', 'new_m_ffn2_w_in', 'new_m_ffn2_w_out', 'new_m_ab_w_in', 'new_m_pool_w', 'new_m_pool_b', 'new_m_pool_scale', 'new_m_conv_w', 'new_m_conv_b', 'new_m_conv_ln_g', 'new_m_conv_ln_b', 'new_m_ab_w_out', 'new_m_sgu_w_in', 'new_m_sgu_ln_g', 'new_m_sgu_ln_b', 'new_m_sgu_w', 'new_m_sgu_b', 'new_m_sgu_w_out', 'new_m_final_norm', 'new_v_ffn1_norm', 'new_v_ffn1_w_in', 'new_v_ffn1_w_out', 'new_v_mix_norm', 'new_v_ffn2_norm', 'new_v_ffn2_w_in', 'new_v_ffn2_w_out', 'new_v_ab_w_in', 'new_v_pool_w', 'new_v_pool_b', 'new_v_pool_scale', 'new_v_conv_w', 'new_v_conv_b', 'new_v_conv_ln_g', 'new_v_conv_ln_b', 'new_v_ab_w_out', 'new_v_sgu_w_in', 'new_v_sgu_ln_g', 'new_v_sgu_ln_b', 'new_v_sgu_w', 'new_v_sgu_b', 'new_v_sgu_w_out', 'new_v_final_norm']
TWIN_LEAF_KINDS = {'loss': 'loss', 'grad_x': 'grad_x', 'grad_ffn1_norm': 'grad_w', 'grad_ffn1_w_in': 'grad_w', 'grad_ffn1_w_out': 'grad_w', 'grad_mix_norm': 'grad_w', 'grad_ffn2_norm': 'grad_w', 'grad_ffn2_w_in': 'grad_w', 'grad_ffn2_w_out': 'grad_w', 'grad_ab_w_in': 'grad_w', 'grad_pool_w': 'grad_w', 'grad_pool_b': 'grad_w', 'grad_pool_scale': 'grad_w', 'grad_conv_w': 'grad_w', 'grad_conv_b': 'grad_w', 'grad_conv_ln_g': 'grad_w', 'grad_conv_ln_b': 'grad_w', 'grad_ab_w_out': 'grad_w', 'grad_sgu_w_in': 'grad_w', 'grad_sgu_ln_g': 'grad_w', 'grad_sgu_ln_b': 'grad_w', 'grad_sgu_w': 'grad_w', 'grad_sgu_b': 'grad_w', 'grad_sgu_w_out': 'grad_w', 'grad_final_norm': 'grad_w', 'delta_ffn1_norm': 'delta_w', 'delta_ffn1_w_in': 'delta_w', 'delta_ffn1_w_out': 'delta_w', 'delta_mix_norm': 'delta_w', 'delta_ffn2_norm': 'delta_w', 'delta_ffn2_w_in': 'delta_w', 'delta_ffn2_w_out': 'delta_w', 'delta_ab_w_in': 'delta_w', 'delta_pool_w': 'delta_w', 'delta_pool_b': 'delta_w', 'delta_pool_scale': 'delta_w', 'delta_conv_w': 'delta_w', 'delta_conv_b': 'delta_w', 'delta_conv_ln_g': 'delta_w', 'delta_conv_ln_b': 'delta_w', 'delta_ab_w_out': 'delta_w', 'delta_sgu_w_in': 'delta_w', 'delta_sgu_ln_g': 'delta_w', 'delta_sgu_ln_b': 'delta_w', 'delta_sgu_w': 'delta_w', 'delta_sgu_b': 'delta_w', 'delta_sgu_w_out': 'delta_w', 'delta_final_norm': 'delta_w', 'new_m_ffn1_norm': 'new_m', 'new_m_ffn1_w_in': 'new_m', 'new_m_ffn1_w_out': 'new_m', 'new_m_mix_norm': 'new_m', 'new_m_ffn2_norm': 'new_m', 'new_m_ffn2_w_in': 'new_m', 'new_m_ffn2_w_out': 'new_m', 'new_m_ab_w_in': 'new_m', 'new_m_pool_w': 'new_m', 'new_m_pool_b': 'new_m', 'new_m_pool_scale': 'new_m', 'new_m_conv_w': 'new_m', 'new_m_conv_b': 'new_m', 'new_m_conv_ln_g': 'new_m', 'new_m_conv_ln_b': 'new_m', 'new_m_ab_w_out': 'new_m', 'new_m_sgu_w_in': 'new_m', 'new_m_sgu_ln_g': 'new_m', 'new_m_sgu_ln_b': 'new_m', 'new_m_sgu_w': 'new_m', 'new_m_sgu_b': 'new_m', 'new_m_sgu_w_out': 'new_m', 'new_m_final_norm': 'new_m', 'new_v_ffn1_norm': 'new_v', 'new_v_ffn1_w_in': 'new_v', 'new_v_ffn1_w_out': 'new_v', 'new_v_mix_norm': 'new_v', 'new_v_ffn2_norm': 'new_v', 'new_v_ffn2_w_in': 'new_v', 'new_v_ffn2_w_out': 'new_v', 'new_v_ab_w_in': 'new_v', 'new_v_pool_w': 'new_v', 'new_v_pool_b': 'new_v', 'new_v_pool_scale': 'new_v', 'new_v_conv_w': 'new_v', 'new_v_conv_b': 'new_v', 'new_v_conv_ln_g': 'new_v', 'new_v_conv_ln_b': 'new_v', 'new_v_ab_w_out': 'new_v', 'new_v_sgu_w_in': 'new_v', 'new_v_sgu_ln_g': 'new_v', 'new_v_sgu_ln_b': 'new_v', 'new_v_sgu_w': 'new_v', 'new_v_sgu_b': 'new_v', 'new_v_sgu_w_out': 'new_v', 'new_v_final_norm': 'new_v'}


def _forward(args):
    return _fwd_reference(*[args[k] for k in FWD_PARAMS])


def _output_shape():
    def fwd():
        inp = _fwd_setup_inputs(0)
        return _fwd_reference(*[inp[k] for k in FWD_PARAMS])
    out = _jax.eval_shape(fwd)
    return out.shape, out.dtype

N_MICROBATCH = 1
ADAM_LR = 0.001
ADAM_B1 = 0.9
ADAM_B2 = 0.999
ADAM_EPS = 1e-08
ADAM_WD = 0.01
ADAM_STEP = 10
PER_EXAMPLE_BATCH_AXIS = {'x': 0, 'loss_target': 0}
SHARED_INPUTS = []
_WEIGHT_DTYPES = {'ffn1_norm': _jnp.float32, 'ffn1_w_in': _jnp.float32, 'ffn1_w_out': _jnp.float32, 'mix_norm': _jnp.float32, 'ffn2_norm': _jnp.float32, 'ffn2_w_in': _jnp.float32, 'ffn2_w_out': _jnp.float32, 'ab_w_in': _jnp.float32, 'pool_w': _jnp.float32, 'pool_b': _jnp.float32, 'pool_scale': _jnp.float32, 'conv_w': _jnp.float32, 'conv_b': _jnp.float32, 'conv_ln_g': _jnp.float32, 'conv_ln_b': _jnp.float32, 'ab_w_out': _jnp.float32, 'sgu_w_in': _jnp.float32, 'sgu_ln_g': _jnp.float32, 'sgu_ln_b': _jnp.float32, 'sgu_w': _jnp.float32, 'sgu_b': _jnp.float32, 'sgu_w_out': _jnp.float32, 'final_norm': _jnp.float32}
MOMENT_SCALE = {'ffn1_norm': 1.160601e-01, 'ffn1_w_in': 4.706569e-02, 'ffn1_w_out': 7.703430e-02, 'mix_norm': 1.667206e-01, 'ffn2_norm': 9.420495e-02, 'ffn2_w_in': 3.801635e-02, 'ffn2_w_out': 6.280643e-02, 'ab_w_in': 1.175868e-01, 'pool_w': 1.140470e-01, 'pool_b': 4.047457e-01, 'pool_scale': 2.339624e-01, 'conv_w': 1.615965e-01, 'conv_b': 6.030882e-01, 'conv_ln_g': 2.779037e-01, 'conv_ln_b': 3.782648e-01, 'ab_w_out': 1.603831e-01, 'sgu_w_in': 1.254702e-01, 'sgu_ln_g': 8.269847e-02, 'sgu_ln_b': 8.407279e-02, 'sgu_w': 7.962839e-02, 'sgu_b': 1.144002e-01, 'sgu_w_out': 2.181644e-01, 'final_norm': 6.416450e+01}


def _to_microbatches(a, axis):
    t = _jnp.moveaxis(a, axis, 0)
    t = t.reshape((N_MICROBATCH, t.shape[0] // N_MICROBATCH) + t.shape[1:])
    return _jnp.moveaxis(t, 1, axis + 1)


def setup_inputs(seed: int = 0) -> dict:
    inp = _fwd_setup_inputs(seed)
    key = _jax.random.fold_in(_jax.random.key(seed), 7919)
    shape, _ = _output_shape()
    out = dict(inp)
    out["loss_target"] = _jax.random.normal(_jax.random.fold_in(key, 0), shape, _jnp.float32)
    for i, name in enumerate(TWIN_WEIGHTS):
        w = inp[name].astype(_jnp.float32)
        if MOMENT_SCALE is None:
            s = _jnp.sqrt(_jnp.mean(_jnp.square(w)) + 1e-30)
        else:
            s = MOMENT_SCALE[name]
        km, kv = _jax.random.split(_jax.random.fold_in(key, i + 1))
        out[name] = w
        out["m_" + name] = s * _jax.random.normal(km, w.shape, _jnp.float32)
        out["v_" + name] = (s * s) * _jax.random.uniform(kv, w.shape, _jnp.float32, 0.5, 1.5)
    if N_MICROBATCH > 1:
        for name, axis in PER_EXAMPLE_BATCH_AXIS.items():
            out[name] = _to_microbatches(out[name], axis)
    return {'x': out['x'], 'ffn1_norm': out['ffn1_norm'], 'ffn1_w_in': out['ffn1_w_in'], 'ffn1_w_out': out['ffn1_w_out'], 'mix_norm': out['mix_norm'], 'ffn2_norm': out['ffn2_norm'], 'ffn2_w_in': out['ffn2_w_in'], 'ffn2_w_out': out['ffn2_w_out'], 'ab_w_in': out['ab_w_in'], 'pool_w': out['pool_w'], 'pool_b': out['pool_b'], 'pool_scale': out['pool_scale'], 'conv_w': out['conv_w'], 'conv_b': out['conv_b'], 'conv_ln_g': out['conv_ln_g'], 'conv_ln_b': out['conv_ln_b'], 'ab_w_out': out['ab_w_out'], 'sgu_w_in': out['sgu_w_in'], 'sgu_ln_g': out['sgu_ln_g'], 'sgu_ln_b': out['sgu_ln_b'], 'sgu_w': out['sgu_w'], 'sgu_b': out['sgu_b'], 'sgu_w_out': out['sgu_w_out'], 'final_norm': out['final_norm'], 'loss_target': out['loss_target'], 'm_ffn1_norm': out['m_ffn1_norm'], 'm_ffn1_w_in': out['m_ffn1_w_in'], 'm_ffn1_w_out': out['m_ffn1_w_out'], 'm_mix_norm': out['m_mix_norm'], 'm_ffn2_norm': out['m_ffn2_norm'], 'm_ffn2_w_in': out['m_ffn2_w_in'], 'm_ffn2_w_out': out['m_ffn2_w_out'], 'm_ab_w_in': out['m_ab_w_in'], 'm_pool_w': out['m_pool_w'], 'm_pool_b': out['m_pool_b'], 'm_pool_scale': out['m_pool_scale'], 'm_conv_w': out['m_conv_w'], 'm_conv_b': out['m_conv_b'], 'm_conv_ln_g': out['m_conv_ln_g'], 'm_conv_ln_b': out['m_conv_ln_b'], 'm_ab_w_out': out['m_ab_w_out'], 'm_sgu_w_in': out['m_sgu_w_in'], 'm_sgu_ln_g': out['m_sgu_ln_g'], 'm_sgu_ln_b': out['m_sgu_ln_b'], 'm_sgu_w': out['m_sgu_w'], 'm_sgu_b': out['m_sgu_b'], 'm_sgu_w_out': out['m_sgu_w_out'], 'm_final_norm': out['m_final_norm'], 'v_ffn1_norm': out['v_ffn1_norm'], 'v_ffn1_w_in': out['v_ffn1_w_in'], 'v_ffn1_w_out': out['v_ffn1_w_out'], 'v_mix_norm': out['v_mix_norm'], 'v_ffn2_norm': out['v_ffn2_norm'], 'v_ffn2_w_in': out['v_ffn2_w_in'], 'v_ffn2_w_out': out['v_ffn2_w_out'], 'v_ab_w_in': out['v_ab_w_in'], 'v_pool_w': out['v_pool_w'], 'v_pool_b': out['v_pool_b'], 'v_pool_scale': out['v_pool_scale'], 'v_conv_w': out['v_conv_w'], 'v_conv_b': out['v_conv_b'], 'v_conv_ln_g': out['v_conv_ln_g'], 'v_conv_ln_b': out['v_conv_ln_b'], 'v_ab_w_out': out['v_ab_w_out'], 'v_sgu_w_in': out['v_sgu_w_in'], 'v_sgu_ln_g': out['v_sgu_ln_g'], 'v_sgu_ln_b': out['v_sgu_ln_b'], 'v_sgu_w': out['v_sgu_w'], 'v_sgu_b': out['v_sgu_b'], 'v_sgu_w_out': out['v_sgu_w_out'], 'v_final_norm': out['v_final_norm']}


def _loss(weights, diff, rest, loss_target):
    with _jax.named_scope("forward"):
        args = {**rest, TWIN_DIFF_INPUT: diff, **{k: w.astype(_WEIGHT_DTYPES[k]) for k, w in weights.items()}}
        y = _forward(args)
    with _jax.named_scope("loss_head"):
        err = _jnp.square(y.astype(_jnp.float32) - loss_target)
        return 0.5 * _jnp.sum(_jnp.mean(err, axis=-1)) if err.ndim else 0.5 * err


def _adamw(w, g, m, v):
    m = ADAM_B1 * m + (1.0 - ADAM_B1) * g
    v = ADAM_B2 * v + (1.0 - ADAM_B2) * _jnp.square(g)
    m_hat = m / (1.0 - ADAM_B1 ** ADAM_STEP)
    v_hat = v / (1.0 - ADAM_B2 ** ADAM_STEP)
    delta = -ADAM_LR * (m_hat / (_jnp.sqrt(v_hat) + ADAM_EPS) + ADAM_WD * w)
    return delta, m, v


def reference(x, ffn1_norm, ffn1_w_in, ffn1_w_out, mix_norm, ffn2_norm, ffn2_w_in, ffn2_w_out, ab_w_in, pool_w, pool_b, pool_scale, conv_w, conv_b, conv_ln_g, conv_ln_b, ab_w_out, sgu_w_in, sgu_ln_g, sgu_ln_b, sgu_w, sgu_b, sgu_w_out, final_norm, loss_target, m_ffn1_norm, m_ffn1_w_in, m_ffn1_w_out, m_mix_norm, m_ffn2_norm, m_ffn2_w_in, m_ffn2_w_out, m_ab_w_in, m_pool_w, m_pool_b, m_pool_scale, m_conv_w, m_conv_b, m_conv_ln_g, m_conv_ln_b, m_ab_w_out, m_sgu_w_in, m_sgu_ln_g, m_sgu_ln_b, m_sgu_w, m_sgu_b, m_sgu_w_out, m_final_norm, v_ffn1_norm, v_ffn1_w_in, v_ffn1_w_out, v_mix_norm, v_ffn2_norm, v_ffn2_w_in, v_ffn2_w_out, v_ab_w_in, v_pool_w, v_pool_b, v_pool_scale, v_conv_w, v_conv_b, v_conv_ln_g, v_conv_ln_b, v_ab_w_out, v_sgu_w_in, v_sgu_ln_g, v_sgu_ln_b, v_sgu_w, v_sgu_b, v_sgu_w_out, v_final_norm):
    given = dict(x=x, ffn1_norm=ffn1_norm, ffn1_w_in=ffn1_w_in, ffn1_w_out=ffn1_w_out, mix_norm=mix_norm, ffn2_norm=ffn2_norm, ffn2_w_in=ffn2_w_in, ffn2_w_out=ffn2_w_out, ab_w_in=ab_w_in, pool_w=pool_w, pool_b=pool_b, pool_scale=pool_scale, conv_w=conv_w, conv_b=conv_b, conv_ln_g=conv_ln_g, conv_ln_b=conv_ln_b, ab_w_out=ab_w_out, sgu_w_in=sgu_w_in, sgu_ln_g=sgu_ln_g, sgu_ln_b=sgu_ln_b, sgu_w=sgu_w, sgu_b=sgu_b, sgu_w_out=sgu_w_out, final_norm=final_norm, loss_target=loss_target, m_ffn1_norm=m_ffn1_norm, m_ffn1_w_in=m_ffn1_w_in, m_ffn1_w_out=m_ffn1_w_out, m_mix_norm=m_mix_norm, m_ffn2_norm=m_ffn2_norm, m_ffn2_w_in=m_ffn2_w_in, m_ffn2_w_out=m_ffn2_w_out, m_ab_w_in=m_ab_w_in, m_pool_w=m_pool_w, m_pool_b=m_pool_b, m_pool_scale=m_pool_scale, m_conv_w=m_conv_w, m_conv_b=m_conv_b, m_conv_ln_g=m_conv_ln_g, m_conv_ln_b=m_conv_ln_b, m_ab_w_out=m_ab_w_out, m_sgu_w_in=m_sgu_w_in, m_sgu_ln_g=m_sgu_ln_g, m_sgu_ln_b=m_sgu_ln_b, m_sgu_w=m_sgu_w, m_sgu_b=m_sgu_b, m_sgu_w_out=m_sgu_w_out, m_final_norm=m_final_norm, v_ffn1_norm=v_ffn1_norm, v_ffn1_w_in=v_ffn1_w_in, v_ffn1_w_out=v_ffn1_w_out, v_mix_norm=v_mix_norm, v_ffn2_norm=v_ffn2_norm, v_ffn2_w_in=v_ffn2_w_in, v_ffn2_w_out=v_ffn2_w_out, v_ab_w_in=v_ab_w_in, v_pool_w=v_pool_w, v_pool_b=v_pool_b, v_pool_scale=v_pool_scale, v_conv_w=v_conv_w, v_conv_b=v_conv_b, v_conv_ln_g=v_conv_ln_g, v_conv_ln_b=v_conv_ln_b, v_ab_w_out=v_ab_w_out, v_sgu_w_in=v_sgu_w_in, v_sgu_ln_g=v_sgu_ln_g, v_sgu_ln_b=v_sgu_ln_b, v_sgu_w=v_sgu_w, v_sgu_b=v_sgu_b, v_sgu_w_out=v_sgu_w_out, v_final_norm=v_final_norm)
    weights = {n: given[n] for n in TWIN_WEIGHTS}
    shared = {n: given[n] for n in SHARED_INPUTS}
    per_example = {n: given[n] for n in ['x']}
    grad_fn = _jax.value_and_grad(_loss, argnums=(0, 1))

    def one_microbatch(ex, loss_target):
        ex = dict(ex)
        diff = ex.pop(TWIN_DIFF_INPUT)
        return grad_fn(weights, diff, {**shared, **ex}, loss_target)

    if N_MICROBATCH == 1:
        loss, (grad_w, grad_x) = one_microbatch(per_example, given["loss_target"])
    else:
        def body(carry, xs):
            loss_sum, grad_sum = carry
            l_k, (gw_k, gx_k) = one_microbatch(xs[0], xs[1])
            with _jax.named_scope("update"):
                return (loss_sum + l_k, _jax.tree.map(_jnp.add, grad_sum, gw_k)), gx_k

        init = (_jnp.zeros((), _jnp.float32), _jax.tree.map(_jnp.zeros_like, weights))
        (loss, grad_w), grad_x = _jax.lax.scan(body, init, (per_example, given["loss_target"]))
    with _jax.named_scope("update"):
        delta_w, new_m, new_v = {}, {}, {}
        for n in TWIN_WEIGHTS:
            delta_w[n], new_m[n], new_v[n] = _adamw(weights[n], grad_w[n], given["m_" + n], given["v_" + n])
    return (loss, grad_x, *[grad_w[n] for n in TWIN_WEIGHTS], *[delta_w[n] for n in TWIN_WEIGHTS],
            *[new_m[n] for n in TWIN_WEIGHTS], *[new_v[n] for n in TWIN_WEIGHTS])
```

```python
import functools

import jax
import jax.numpy as jnp
from jax import lax
from jax.experimental import pallas as pl
from jax.experimental.pallas import tpu as pltpu

F32, BF16 = jnp.float32, jnp.bfloat16
EPS = 1e-6
N_CHIPS = 4
POOL_WINDOWS = (2, 4, 8, 16)
POOL_GC = 128
POOL_CH = 512
CONV_CH = 512
CONV_WIDTH = 31
HALO = 32
SGU_HEADS = 8
CHUNK = 128
ADAM_LR, ADAM_B1, ADAM_B2, ADAM_EPS, ADAM_WD, ADAM_STEP = 0.001, 0.9, 0.999, 1e-08, 0.01, 10
VMEM_LIMIT_BYTES = 60 * 1024 * 1024
MESH_AXES = ("x", "y", "c")
MESH = pl.DeviceIdType.MESH


def _cparams():
    return pltpu.CompilerParams(dimension_semantics=("arbitrary",), vmem_limit_bytes=VMEM_LIMIT_BYTES)


def _dot(a, b):
    return jnp.dot(a, b, preferred_element_type=F32)


def _dot_nt(a, b):
    return lax.dot_general(a, b, (((1,), (1,)), ((), ())), preferred_element_type=F32)


def _dot_tn(a, b):
    return lax.dot_general(a, b, (((0,), (0,)), ((), ())), preferred_element_type=F32)


def _rms_fwd(x):
    r = lax.rsqrt(jnp.mean(x * x, axis=-1, keepdims=True) + EPS)
    return x * r, r


def _rms_bwd(dxn, xh, r, g):
    dxh = dxn * g
    return r * (dxh - xh * jnp.mean(dxh * xh, axis=-1, keepdims=True))


def _ln_fwd(y):
    mu = jnp.mean(y, axis=-1, keepdims=True)
    yc = y - mu
    rs = lax.rsqrt(jnp.mean(yc * yc, axis=-1, keepdims=True) + EPS)
    return yc * rs, rs


def _ln_bwd(dyhat, yhat, rs):
    return rs * (dyhat - jnp.mean(dyhat, axis=-1, keepdims=True) - yhat * jnp.mean(dyhat * yhat, axis=-1, keepdims=True))


def _sigmoid(x):
    return 1.0 / (1.0 + jnp.exp(-x))


def _const_spec(shape):
    n = len(shape)
    return pl.BlockSpec(shape, lambda i: (0,) * n)


def _row_spec(tm, cols):
    return pl.BlockSpec((tm, cols), lambda i: (i, 0))


HBM_SPEC = pl.BlockSpec(memory_space=pltpu.HBM)


def _in_weight_copies(w_hbm, layer, w_v, sem, base=0):
    return [pltpu.make_async_copy(w_hbm.at[q, layer], w_v.at[q], sem.at[base + q]) for q in range(N_CHIPS)]


def _out_weight_copies(w_hbm, layer, w_v, sem, base=0):
    rows = w_hbm.shape[2]
    return [pltpu.make_async_copy(w_hbm.at[q, layer], w_v.at[pl.ds(q * rows, rows)], sem.at[base + q]) for q in range(N_CHIPS)]


def _load_at_first_step(copies):
    @pl.when(pl.program_id(0) == 0)
    def _():
        for cp in copies:
            cp.start()
        for cp in copies:
            cp.wait()


def _ffn_fwd(x, g, win_g, wout_g, layer, name):
    t, d = x.shape
    c = win_g.shape[-1]
    ff = 2 * c
    tm = min(256, t)

    def body(x_ref, g_ref, win_hbm, wout_hbm, xo_ref, h_ref, xn_ref, win_v, wout_v, sem):
        _load_at_first_step(_in_weight_copies(win_hbm, layer, win_v, sem) + _out_weight_copies(wout_hbm, layer, wout_v, sem, N_CHIPS))
        xv = x_ref[...]
        xh, _ = _rms_fwd(xv)
        xn = (xh * g_ref[...]).astype(BF16)
        xn_ref[...] = xn
        acc = jnp.zeros((tm, d), F32)
        for j in range(2):
            gate = _dot(xn, win_v[j])
            up = _dot(xn, win_v[j + 2])
            h_ref[:, j * c:(j + 1) * c] = gate.astype(BF16)
            h_ref[:, ff + j * c:ff + (j + 1) * c] = up.astype(BF16)
            act = (gate * _sigmoid(gate) * up).astype(BF16)
            acc = acc + _dot(act, wout_v[j * c:(j + 1) * c, :])
        xo_ref[...] = xv + 0.5 * acc

    return pl.pallas_call(
        body, name=name, grid=(t // tm,),
        in_specs=[_row_spec(tm, d), _const_spec((1, d)), HBM_SPEC, HBM_SPEC],
        out_specs=[_row_spec(tm, d), _row_spec(tm, 2 * ff), _row_spec(tm, d)],
        out_shape=[jax.ShapeDtypeStruct((t, d), F32), jax.ShapeDtypeStruct((t, 2 * ff), BF16), jax.ShapeDtypeStruct((t, d), BF16)],
        scratch_shapes=[pltpu.VMEM((N_CHIPS, d, c), BF16), pltpu.VMEM((ff, d), BF16), pltpu.SemaphoreType.DMA((2 * N_CHIPS,))],
        compiler_params=_cparams(),
    )(x, g, win_g, wout_g)


def _ffn_bwd(x, dy, h, g, win_g, wout_g, layer, name):
    t, d = x.shape
    c = win_g.shape[-1]
    ff = 2 * c
    tm = min(256, t)

    def body(x_ref, dy_ref, h_ref, g_ref, win_hbm, wout_hbm, dx_ref, dh_ref, act_ref, dg_ref, win_v, wout_v, sem):
        _load_at_first_step(_in_weight_copies(win_hbm, layer, win_v, sem) + _out_weight_copies(wout_hbm, layer, wout_v, sem, N_CHIPS))

        @pl.when(pl.program_id(0) == 0)
        def _():
            dg_ref[...] = jnp.zeros_like(dg_ref)

        xv, dyv, gv = x_ref[...], dy_ref[...], g_ref[...]
        xh, r = _rms_fwd(xv)
        dyh = (0.5 * dyv).astype(BF16)
        dxn = jnp.zeros((tm, d), F32)
        for j in range(2):
            gate = h_ref[:, j * c:(j + 1) * c].astype(F32)
            up = h_ref[:, ff + j * c:ff + (j + 1) * c].astype(F32)
            dact = _dot_nt(dyh, wout_v[j * c:(j + 1) * c, :])
            s = _sigmoid(gate)
            sl = gate * s
            act_ref[:, j * c:(j + 1) * c] = (sl * up).astype(BF16)
            dgate = (dact * up * (s * (1.0 + gate * (1.0 - s)))).astype(BF16)
            dup = (dact * sl).astype(BF16)
            dh_ref[:, j * c:(j + 1) * c] = dgate
            dh_ref[:, ff + j * c:ff + (j + 1) * c] = dup
            dxn = dxn + _dot_nt(dgate, win_v[j]) + _dot_nt(dup, win_v[j + 2])
        dg_ref[...] += jnp.sum(dxn * xh, axis=0, keepdims=True)
        dx_ref[...] = dyv + _rms_bwd(dxn, xh, r, gv)

    return pl.pallas_call(
        body, name=name, grid=(t // tm,),
        in_specs=[_row_spec(tm, d), _row_spec(tm, d), _row_spec(tm, 2 * ff), _const_spec((1, d)), HBM_SPEC, HBM_SPEC],
        out_specs=[_row_spec(tm, d), _row_spec(tm, 2 * ff), _row_spec(tm, ff), _const_spec((1, d))],
        out_shape=[jax.ShapeDtypeStruct((t, d), F32), jax.ShapeDtypeStruct((t, 2 * ff), BF16), jax.ShapeDtypeStruct((t, ff), BF16),
                   jax.ShapeDtypeStruct((1, d), F32)],
        scratch_shapes=[pltpu.VMEM((N_CHIPS, d, c), BF16), pltpu.VMEM((ff, d), BF16), pltpu.SemaphoreType.DMA((2 * N_CHIPS,))],
        compiler_params=_cparams(),
    )(x, dy, h, g, win_g, wout_g)


def _norm_matmul(x, g, win_g, layer, name):
    t, d = x.shape
    c = win_g.shape[-1]
    tm = min(512, t)

    def body(x_ref, g_ref, win_hbm, o_ref, xn_ref, win_v, sem):
        _load_at_first_step(_in_weight_copies(win_hbm, layer, win_v, sem))
        xh, _ = _rms_fwd(x_ref[...])
        xn = (xh * g_ref[...]).astype(BF16)
        xn_ref[...] = xn
        for q in range(N_CHIPS):
            o_ref[:, q * c:(q + 1) * c] = _dot(xn, win_v[q])

    return pl.pallas_call(
        body, name=name, grid=(t // tm,),
        in_specs=[_row_spec(tm, d), _const_spec((1, d)), HBM_SPEC],
        out_specs=[_row_spec(tm, N_CHIPS * c), _row_spec(tm, d)],
        out_shape=[jax.ShapeDtypeStruct((t, N_CHIPS * c), F32), jax.ShapeDtypeStruct((t, d), BF16)],
        scratch_shapes=[pltpu.VMEM((N_CHIPS, d, c), BF16), pltpu.SemaphoreType.DMA((N_CHIPS,))],
        compiler_params=_cparams(),
    )(x, g, win_g)


def _proj_in_bwd_tail(dh, win_v, c):
    dxn = _dot_nt(dh[:, 0:c], win_v[0])
    for q in range(1, N_CHIPS):
        dxn = dxn + _dot_nt(dh[:, q * c:(q + 1) * c], win_v[q])
    return dxn


def _prev_halo_spec(tm, cols):
    return pl.BlockSpec((HALO, cols), lambda i: (jnp.maximum(i * (tm // HALO) - 1, 0), 0))


def _next_halo_spec(tm, cols, t):
    last = t // HALO - 1
    return pl.BlockSpec((HALO, cols), lambda i: (jnp.minimum((i + 1) * (tm // HALO), last), 0))


def _shift_down(w, k):
    return w if k == 0 else pltpu.roll(w, k, 0)


def _shift_up(w, k):
    return w if k == 0 else pltpu.roll(w, w.shape[0] - k, 0)


def _pool_counts(i, tm):
    pos = (i * tm + lax.broadcasted_iota(jnp.int32, (tm, POOL_CH), 0) + 1).astype(F32)
    lane = lax.broadcasted_iota(jnp.int32, (tm, POOL_CH), 1)
    win = jnp.where(lane < POOL_GC, 2.0, jnp.where(lane < 2 * POOL_GC, 4.0, jnp.where(lane < 3 * POOL_GC, 8.0, 16.0)))
    return jnp.minimum(pos, win)


def _group_select(parts):
    return jnp.concatenate([p[:, k * POOL_GC:(k + 1) * POOL_GC] for k, p in enumerate(parts)], axis=1)


def _mix0_recompute(i, tm, h_cur, h_prev, conv_w, conv_b):
    prev = jnp.where(i > 0, h_prev, 0.0)
    win = jnp.concatenate([prev, h_cur], axis=0)
    u_w = win[:, 0:POOL_CH]
    a_w = win[:, POOL_CH:POOL_CH + CONV_CH]
    gt_w = win[:, POOL_CH + CONV_CH:]
    g_w = a_w * _sigmoid(gt_w)
    y = jnp.zeros((tm, CONV_CH), F32)
    for k in range(CONV_WIDTH):
        y = y + conv_w[k:k + 1, :] * _shift_down(g_w, CONV_WIDTH - 1 - k)[HALO:, :]
    y = y + conv_b
    s2 = u_w + _shift_down(u_w, 1)
    s4 = s2 + _shift_down(s2, 2)
    s8 = s4 + _shift_down(s4, 4)
    s16 = s8 + _shift_down(s8, 8)
    sums = _group_select([s2[HALO:], s4[HALO:], s8[HALO:], s16[HALO:]])
    cnt = _pool_counts(i, tm)
    pooled = sums / cnt - h_cur[:, 0:POOL_CH]
    return g_w, y, pooled, cnt


def _pool_linear(pooled, pw_ref, pb, scale):
    mixed = jnp.concatenate(
        [_dot(pooled[:, k * POOL_GC:(k + 1) * POOL_GC].astype(BF16), pw_ref[k].astype(BF16)) for k in range(len(POOL_WINDOWS))], axis=1) + pb
    return mixed


def _mix0_fwd(x, h0, pool_w, pool_b, pool_scale, conv_w, conv_b, ln_g, ln_b, wout_g, name):
    t, d = x.shape
    tm = min(256, t)
    hc = h0.shape[1]

    def body(x_ref, h_ref, hp_ref, pw_ref, pb_ref, ps_ref, cw_ref, cb_ref, lg_ref, lb_ref, wout_hbm, xo_ref, ycat_ref, wout_v, sem):
        _load_at_first_step(_out_weight_copies(wout_hbm, 0, wout_v, sem))
        i = pl.program_id(0)
        _, y, pooled, _ = _mix0_recompute(i, tm, h_ref[...], hp_ref[...], cw_ref[...], cb_ref[...])
        yhat, _ = _ln_fwd(y)
        yn = yhat * lg_ref[...] + lb_ref[...]
        yb = yn * _sigmoid(yn)
        ya = _pool_linear(pooled, pw_ref, pb_ref[...], None) * ps_ref[...]
        ycat = jnp.concatenate([ya, yb], axis=1).astype(BF16)
        ycat_ref[...] = ycat
        xo_ref[...] = x_ref[...] + _dot(ycat, wout_v[...])

    return pl.pallas_call(
        body, name=name, grid=(t // tm,),
        in_specs=[_row_spec(tm, d), _row_spec(tm, hc), _prev_halo_spec(tm, hc), _const_spec(pool_w.shape), _const_spec((1, POOL_CH)),
                  _const_spec((1, POOL_CH)), _const_spec(conv_w.shape), _const_spec((1, CONV_CH)), _const_spec((1, CONV_CH)),
                  _const_spec((1, CONV_CH)), HBM_SPEC],
        out_specs=[_row_spec(tm, d), _row_spec(tm, d)],
        out_shape=[jax.ShapeDtypeStruct((t, d), F32), jax.ShapeDtypeStruct((t, d), BF16)],
        scratch_shapes=[pltpu.VMEM((d, d), BF16), pltpu.SemaphoreType.DMA((N_CHIPS,))],
        compiler_params=_cparams(),
    )(x, h0, h0, pool_w, pool_b, pool_scale, conv_w, conv_b, ln_g, ln_b, wout_g)


def _mix0_bwd_a(dy, h0, pool_w, pool_b, pool_scale, conv_w, conv_b, ln_g, ln_b, wout_g, name):
    t, d = dy.shape
    tm = min(256, t)
    hc = h0.shape[1]
    n_small = 40

    def body(dy_ref, h_ref, hp_ref, pw_ref, pb_ref, ps_ref, cw_ref, cb_ref, lg_ref, lb_ref, wout_hbm,
             dconv_ref, dpc_ref, dpw_ref, small_ref, wout_v, sem):
        _load_at_first_step(_out_weight_copies(wout_hbm, 0, wout_v, sem))
        i = pl.program_id(0)

        @pl.when(i == 0)
        def _():
            dpw_ref[...] = jnp.zeros_like(dpw_ref)
            small_ref[...] = jnp.zeros_like(small_ref)

        g_w, y, pooled, cnt = _mix0_recompute(i, tm, h_ref[...], hp_ref[...], cw_ref[...], cb_ref[...])
        yhat, rs = _ln_fwd(y)
        lg = lg_ref[...]
        yn = yhat * lg + lb_ref[...]
        mixed = _pool_linear(pooled, pw_ref, pb_ref[...], None)
        dycat = _dot_nt(dy_ref[...].astype(BF16), wout_v[...])
        dya, dyb = dycat[:, 0:POOL_CH], dycat[:, POOL_CH:]
        sg = _sigmoid(yn)
        dyn = dyb * (sg * (1.0 + yn * (1.0 - sg)))
        dyc = _ln_bwd(dyn * lg, yhat, rs)
        dconv_ref[...] = dyc
        def add_row(k, value):
            small_ref[k:k + 1, :] += jnp.sum(value, axis=0, keepdims=True)

        for k in range(CONV_WIDTH):
            add_row(k, dyc * _shift_down(g_w, CONV_WIDTH - 1 - k)[HALO:, :])
        add_row(32, dyc)
        add_row(33, dyn * yhat)
        add_row(34, dyn)
        scale = ps_ref[...]
        dmixed = dya * scale
        add_row(35, dya * mixed)
        add_row(36, dmixed)
        dmb = dmixed.astype(BF16)
        dpooled = []
        for k in range(len(POOL_WINDOWS)):
            sl = slice(k * POOL_GC, (k + 1) * POOL_GC)
            dpw_ref[k] += _dot_tn(pooled[:, sl].astype(BF16), dmb[:, sl])
            dpooled.append(_dot_nt(dmb[:, sl], pw_ref[k].astype(BF16)))
        dpc_ref[...] = jnp.concatenate(dpooled, axis=1) / cnt

    return pl.pallas_call(
        body, name=name, grid=(t // tm,),
        in_specs=[_row_spec(tm, d), _row_spec(tm, hc), _prev_halo_spec(tm, hc), _const_spec(pool_w.shape), _const_spec((1, POOL_CH)),
                  _const_spec((1, POOL_CH)), _const_spec(conv_w.shape), _const_spec((1, CONV_CH)), _const_spec((1, CONV_CH)),
                  _const_spec((1, CONV_CH)), HBM_SPEC],
        out_specs=[_row_spec(tm, CONV_CH), _row_spec(tm, POOL_CH), _const_spec(pool_w.shape), _const_spec((n_small, CONV_CH))],
        out_shape=[jax.ShapeDtypeStruct((t, CONV_CH), F32), jax.ShapeDtypeStruct((t, POOL_CH), F32),
                   jax.ShapeDtypeStruct(pool_w.shape, F32), jax.ShapeDtypeStruct((n_small, CONV_CH), F32)],
        scratch_shapes=[pltpu.VMEM((d, d), BF16), pltpu.SemaphoreType.DMA((N_CHIPS,))],
        compiler_params=_cparams(),
    )(dy, h0, h0, pool_w, pool_b, pool_scale, conv_w, conv_b, ln_g, ln_b, wout_g)


def _mix0_bwd_b(x, dy, h0, dconv, dpc, g, conv_w, win_g, name):
    t, d = x.shape
    tm = min(256, t)
    hc = h0.shape[1]
    c = win_g.shape[-1]
    n_tiles = t // tm

    def body(x_ref, dy_ref, h_ref, dc_ref, dcn_ref, dp_ref, dpn_ref, g_ref, cw_ref, win_hbm, dx_ref, dh_ref, dg_ref, win_v, sem):
        _load_at_first_step(_in_weight_copies(win_hbm, 0, win_v, sem))
        i = pl.program_id(0)

        @pl.when(i == 0)
        def _():
            dg_ref[...] = jnp.zeros_like(dg_ref)

        not_last = i < n_tiles - 1
        dc_w = jnp.concatenate([dc_ref[...], jnp.where(not_last, dcn_ref[...], 0.0)], axis=0)
        dp_w = jnp.concatenate([dp_ref[...], jnp.where(not_last, dpn_ref[...], 0.0)], axis=0)
        cw = cw_ref[...]
        dg = jnp.zeros((tm, CONV_CH), F32)
        for k in range(CONV_WIDTH):
            dg = dg + cw[k:k + 1, :] * _shift_up(dc_w, CONV_WIDTH - 1 - k)[0:tm, :]
        a2 = dp_w + _shift_up(dp_w, 1)
        a4 = a2 + _shift_up(a2, 2)
        a8 = a4 + _shift_up(a4, 4)
        a16 = a8 + _shift_up(a8, 8)
        back = _group_select([a2[0:tm], a4[0:tm], a8[0:tm], a16[0:tm]])
        du = back - dp_ref[...] * _pool_counts(i, tm)
        hv = h_ref[...]
        a = hv[:, POOL_CH:POOL_CH + CONV_CH]
        sig = _sigmoid(hv[:, POOL_CH + CONV_CH:])
        dh = jnp.concatenate([du, dg * sig, dg * a * sig * (1.0 - sig)], axis=1).astype(BF16)
        dh_ref[...] = dh
        dxn = _proj_in_bwd_tail(dh, win_v, c)
        xh, r = _rms_fwd(x_ref[...])
        dg_ref[...] += jnp.sum(dxn * xh, axis=0, keepdims=True)
        dx_ref[...] = dy_ref[...] + _rms_bwd(dxn, xh, r, g_ref[...])

    return pl.pallas_call(
        body, name=name, grid=(n_tiles,),
        in_specs=[_row_spec(tm, d), _row_spec(tm, d), _row_spec(tm, hc), _row_spec(tm, CONV_CH), _next_halo_spec(tm, CONV_CH, t),
                  _row_spec(tm, POOL_CH), _next_halo_spec(tm, POOL_CH, t), _const_spec((1, d)), _const_spec(conv_w.shape), HBM_SPEC],
        out_specs=[_row_spec(tm, d), _row_spec(tm, hc), _const_spec((1, d))],
        out_shape=[jax.ShapeDtypeStruct((t, d), F32), jax.ShapeDtypeStruct((t, hc), BF16), jax.ShapeDtypeStruct((1, d), F32)],
        scratch_shapes=[pltpu.VMEM((N_CHIPS, d, c), BF16), pltpu.SemaphoreType.DMA((N_CHIPS,))],
        compiler_params=_cparams(),
    )(x, dy, h0, dconv, dconv, dpc, dpc, g, conv_w, win_g)


SQRT_HALF = 0.7071067811865476
INV_SQRT_2PI = 0.3989422804014327


def _gelu(x):
    return 0.5 * x * (1.0 + lax.erf(x * SQRT_HALF))


def _gelu_grad(x):
    return 0.5 * (1.0 + lax.erf(x * SQRT_HALF)) + x * jnp.exp(-0.5 * x * x) * INV_SQRT_2PI


def _causal_mask():
    return (lax.broadcasted_iota(jnp.int32, (CHUNK, CHUNK), 1) <= lax.broadcasted_iota(jnp.int32, (CHUNK, CHUNK), 0)).astype(F32)


def _sgu_recompute(pre, lg, lb):
    half = pre.shape[1] // 2
    z = _gelu(pre)
    u, v = z[:, 0:half], z[:, half:]
    vhat, rs = _ln_fwd(v)
    return u, vhat, rs, vhat * lg + lb


def _sgu_spatial(vln, w_ref, bt, tm):
    mask = _causal_mask()
    wm = [(w_ref[hd] * mask).astype(BF16) for hd in range(SGU_HEADS)]
    vb = vln.astype(BF16)
    rows = []
    for ch in range(tm // CHUNK):
        blocks = [_dot(wm[hd], vb[ch * CHUNK:(ch + 1) * CHUNK, hd * CHUNK:(hd + 1) * CHUNK]) + bt[:, hd:hd + 1] for hd in range(SGU_HEADS)]
        rows.append(jnp.concatenate(blocks, axis=1))
    return jnp.concatenate(rows, axis=0), wm


def _sgu_fwd(x, pre, ln_g, ln_b, w, bt, wout_g, name):
    t, d = x.shape
    tm = min(256, t)
    pc = pre.shape[1]

    def body(x_ref, pre_ref, lg_ref, lb_ref, w_ref, bt_ref, wout_hbm, xo_ref, p_ref, wout_v, sem):
        _load_at_first_step(_out_weight_copies(wout_hbm, 0, wout_v, sem))
        u, _, _, vln = _sgu_recompute(pre_ref[...], lg_ref[...], lb_ref[...])
        vo, _ = _sgu_spatial(vln, w_ref, bt_ref[...], tm)
        p = (u * vo).astype(BF16)
        p_ref[...] = p
        xo_ref[...] = x_ref[...] + _dot(p, wout_v[...])

    return pl.pallas_call(
        body, name=name, grid=(t // tm,),
        in_specs=[_row_spec(tm, d), _row_spec(tm, pc), _const_spec((1, d)), _const_spec((1, d)), _const_spec(w.shape),
                  _const_spec(bt.shape), HBM_SPEC],
        out_specs=[_row_spec(tm, d), _row_spec(tm, d)],
        out_shape=[jax.ShapeDtypeStruct((t, d), F32), jax.ShapeDtypeStruct((t, d), BF16)],
        scratch_shapes=[pltpu.VMEM((d, d), BF16), pltpu.SemaphoreType.DMA((N_CHIPS,))],
        compiler_params=_cparams(),
    )(x, pre, ln_g, ln_b, w, bt, wout_g)


def _sgu_bwd(x, dy, pre, g, ln_g, ln_b, w, bt, win_g, wout_g, name):
    t, d = x.shape
    tm = min(256, t)
    pc = pre.shape[1]
    c = win_g.shape[-1]

    def body(x_ref, dy_ref, pre_ref, g_ref, lg_ref, lb_ref, w_ref, bt_ref, win_hbm, wout_hbm,
             dx_ref, dpre_ref, dg_ref, dlg_ref, dlb_ref, dw_ref, dbt_ref, win_v, wout_v, sem):
        _load_at_first_step(_in_weight_copies(win_hbm, 0, win_v, sem) + _out_weight_copies(wout_hbm, 0, wout_v, sem, N_CHIPS))
        i = pl.program_id(0)

        @pl.when(i == 0)
        def _():
            for ref in (dg_ref, dlg_ref, dlb_ref, dw_ref, dbt_ref):
                ref[...] = jnp.zeros_like(ref)

        prev = pre_ref[...]
        lg = lg_ref[...]
        u, vhat, rs, vln = _sgu_recompute(prev, lg, lb_ref[...])
        vo, wm = _sgu_spatial(vln, w_ref, bt_ref[...], tm)
        dp = _dot_nt(dy_ref[...].astype(BF16), wout_v[...])
        du = dp * vo
        dvo = dp * u
        dvob = dvo.astype(BF16)
        vb = vln.astype(BF16)
        head_lane = lax.broadcasted_iota(jnp.int32, (CHUNK, SGU_HEADS), 1)
        dbt = jnp.zeros((CHUNK, SGU_HEADS), F32)
        dw = [jnp.zeros((CHUNK, CHUNK), F32) for _ in range(SGU_HEADS)]
        rows = []
        for ch in range(tm // CHUNK):
            rs_ = slice(ch * CHUNK, (ch + 1) * CHUNK)
            blocks = []
            for hd in range(SGU_HEADS):
                cs = slice(hd * CHUNK, (hd + 1) * CHUNK)
                dbt = dbt + jnp.where(head_lane == hd, jnp.sum(dvo[rs_, cs], axis=1, keepdims=True), 0.0)
                dw[hd] = dw[hd] + _dot_nt(dvob[rs_, cs], vb[rs_, cs])
                blocks.append(_dot_tn(wm[hd], dvob[rs_, cs]))
            rows.append(jnp.concatenate(blocks, axis=1))
        dvln = jnp.concatenate(rows, axis=0)
        mask = _causal_mask()
        for hd in range(SGU_HEADS):
            dw_ref[hd] += dw[hd] * mask
        dbt_ref[...] += dbt
        dlg_ref[...] += jnp.sum(dvln * vhat, axis=0, keepdims=True)
        dlb_ref[...] += jnp.sum(dvln, axis=0, keepdims=True)
        dv = _ln_bwd(dvln * lg, vhat, rs)
        dpre = (jnp.concatenate([du, dv], axis=1) * _gelu_grad(prev)).astype(BF16)
        dpre_ref[...] = dpre
        dxn = _proj_in_bwd_tail(dpre, win_v, c)
        xh, r = _rms_fwd(x_ref[...])
        dg_ref[...] += jnp.sum(dxn * xh, axis=0, keepdims=True)
        dx_ref[...] = dy_ref[...] + _rms_bwd(dxn, xh, r, g_ref[...])

    return pl.pallas_call(
        body, name=name, grid=(t // tm,),
        in_specs=[_row_spec(tm, d), _row_spec(tm, d), _row_spec(tm, pc), _const_spec((1, d)), _const_spec((1, d)), _const_spec((1, d)),
                  _const_spec(w.shape), _const_spec(bt.shape), HBM_SPEC, HBM_SPEC],
        out_specs=[_row_spec(tm, d), _row_spec(tm, pc), _const_spec((1, d)), _const_spec((1, d)), _const_spec((1, d)),
                   _const_spec(w.shape), _const_spec(bt.shape)],
        out_shape=[jax.ShapeDtypeStruct((t, d), F32), jax.ShapeDtypeStruct((t, pc), BF16), jax.ShapeDtypeStruct((1, d), F32),
                   jax.ShapeDtypeStruct((1, d), F32), jax.ShapeDtypeStruct((1, d), F32), jax.ShapeDtypeStruct(w.shape, F32),
                   jax.ShapeDtypeStruct(bt.shape, F32)],
        scratch_shapes=[pltpu.VMEM((N_CHIPS, d, c), BF16), pltpu.VMEM((d, d), BF16), pltpu.SemaphoreType.DMA((2 * N_CHIPS,))],
        compiler_params=_cparams(),
    )(x, dy, pre, g, ln_g, ln_b, w, bt, win_g, wout_g)


def _final_loss(x, tgt, g, name):
    t, d = x.shape
    tm = min(512, t)

    def body(x_ref, t_ref, g_ref, dx_ref, loss_ref, dg_ref):
        @pl.when(pl.program_id(0) == 0)
        def _():
            loss_ref[...] = jnp.zeros_like(loss_ref)
            dg_ref[...] = jnp.zeros_like(dg_ref)

        gv = g_ref[...]
        xh, r = _rms_fwd(x_ref[...])
        diff = xh * gv - t_ref[...]
        loss_ref[...] += 0.5 * jnp.sum(jnp.sum(diff * diff, axis=1, keepdims=True), axis=0, keepdims=True) / d
        dout = diff / d
        dg_ref[...] += jnp.sum(dout * xh, axis=0, keepdims=True)
        dx_ref[...] = _rms_bwd(dout, xh, r, gv)

    return pl.pallas_call(
        body, name=name, grid=(t // tm,),
        in_specs=[_row_spec(tm, d), _row_spec(tm, d), _const_spec((1, d))],
        out_specs=[_row_spec(tm, d), _const_spec((1, 1)), _const_spec((1, d))],
        out_shape=[jax.ShapeDtypeStruct((t, d), F32), jax.ShapeDtypeStruct((1, 1), F32), jax.ShapeDtypeStruct((1, d), F32)],
        compiler_params=_cparams(),
    )(x, tgt, g)


def _tn_matmul(a, b, scale, bm, bn, name):
    t, m = a.shape
    n = b.shape[1]
    tk = min(512, t)
    bm, bn = min(bm, m), min(bn, n)
    nk = t // tk

    def body(a_ref, b_ref, o_ref, acc_ref):
        k = pl.program_id(2)

        @pl.when(k == 0)
        def _():
            acc_ref[...] = jnp.zeros_like(acc_ref)

        bv = b_ref[...]
        if bv.dtype != BF16:
            bv = (scale * bv).astype(BF16)
        acc_ref[...] += _dot_tn(a_ref[...], bv)

        @pl.when(k == nk - 1)
        def _():
            o_ref[...] = acc_ref[...].astype(BF16)

    return pl.pallas_call(
        body, name=name, grid=(m // bm, n // bn, nk),
        in_specs=[pl.BlockSpec((tk, bm), lambda i, j, k: (k, i)), pl.BlockSpec((tk, bn), lambda i, j, k: (k, j))],
        out_specs=pl.BlockSpec((bm, bn), lambda i, j, k: (i, j)),
        out_shape=jax.ShapeDtypeStruct((m, n), BF16),
        scratch_shapes=[pltpu.VMEM((bm, bn), F32)],
        compiler_params=pltpu.CompilerParams(dimension_semantics=("arbitrary", "arbitrary", "arbitrary"), vmem_limit_bytes=VMEM_LIMIT_BYTES),
    )(a, b)


def _row_tile(rows, cols, budget_bytes=2 * 1024 * 1024):
    best = None
    for cand in range(16, rows + 1, 16):
        if rows % cand == 0 and cand * cols * 4 <= budget_bytes:
            best = cand
    return best or rows


def _cast_bf16(w, name):
    shape = w.shape
    w2 = w.reshape(-1, shape[-1])
    rows, cols = w2.shape
    tr = _row_tile(rows, cols)

    def body(w_ref, o_ref):
        o_ref[...] = w_ref[...].astype(BF16)

    out = pl.pallas_call(
        body, name=name, grid=(rows // tr,), in_specs=[_row_spec(tr, cols)], out_specs=_row_spec(tr, cols),
        out_shape=jax.ShapeDtypeStruct((rows, cols), BF16), compiler_params=_cparams())(w2)
    return out.reshape(shape)


def _add_bf16(a, b, name):
    shape = a.shape
    a2, b2 = a.reshape(-1, shape[-1]), b.reshape(-1, shape[-1])
    rows, cols = a2.shape
    tr = _row_tile(rows, cols)

    def body(a_ref, b_ref, o_ref):
        o_ref[...] = (a_ref[...].astype(F32) + b_ref[...].astype(F32)).astype(BF16)

    out = pl.pallas_call(
        body, name=name, grid=(rows // tr,), in_specs=[_row_spec(tr, cols), _row_spec(tr, cols)], out_specs=_row_spec(tr, cols),
        out_shape=jax.ShapeDtypeStruct((rows, cols), BF16), compiler_params=_cparams())(a2, b2)
    return out.reshape(shape)


def _sum_leading(s, name):
    n = s.shape[0]
    shape = s.shape[1:]
    s3 = s.reshape(n, -1, shape[-1])
    rows, cols = s3.shape[1:]
    tr = _row_tile(rows, cols, budget_bytes=1024 * 1024)

    def body(s_ref, o_ref):
        acc = s_ref[0].astype(F32)
        for k in range(1, n):
            acc = acc + s_ref[k].astype(F32)
        o_ref[...] = acc

    out = pl.pallas_call(
        body, name=name, grid=(rows // tr,), in_specs=[pl.BlockSpec((n, tr, cols), lambda i: (0, i, 0))], out_specs=_row_spec(tr, cols),
        out_shape=jax.ShapeDtypeStruct((rows, cols), F32), compiler_params=_cparams())(s3)
    return out.reshape(shape)


def _adamw(w, g, m, v, name):
    shape = w.shape
    cols = shape[-1] if w.ndim > 1 else 128
    w2, g2, m2, v2 = (a.reshape(-1, cols) for a in (w, g, m, v))
    rows = w2.shape[0]
    tr = _row_tile(rows, cols, budget_bytes=1024 * 1024)
    c1 = 1.0 / (1.0 - ADAM_B1 ** ADAM_STEP)
    c2 = 1.0 / (1.0 - ADAM_B2 ** ADAM_STEP)

    def body(w_ref, g_ref, m_ref, v_ref, d_ref, mo_ref, vo_ref):
        gv = g_ref[...]
        mn = ADAM_B1 * m_ref[...] + (1.0 - ADAM_B1) * gv
        vn = ADAM_B2 * v_ref[...] + (1.0 - ADAM_B2) * (gv * gv)
        mo_ref[...] = mn
        vo_ref[...] = vn
        d_ref[...] = -ADAM_LR * ((mn * c1) / (jnp.sqrt(vn * c2) + ADAM_EPS) + ADAM_WD * w_ref[...])

    spec = _row_spec(tr, cols)
    outs = pl.pallas_call(
        body, name=name, grid=(rows // tr,), in_specs=[spec] * 4, out_specs=[spec] * 3,
        out_shape=[jax.ShapeDtypeStruct((rows, cols), F32)] * 3, compiler_params=_cparams())(w2, g2, m2, v2)
    return tuple(o.reshape(shape) for o in outs)


def _my_place():
    return lax.axis_index("x"), lax.axis_index("y"), lax.axis_index("c")


def _other_chips(x, y):
    return [(1 - x, y), (x, 1 - y), (1 - x, 1 - y)]


def _all_gather_weights(shards):
    n = len(shards)

    def body(*refs):
        ins, outs = refs[:n], refs[n:2 * n]
        send_sems, recv_sems, local_sems = refs[2 * n:]
        x, y, c = _my_place()
        sibling = (x, y, 1 - c)
        chips = _other_chips(x, y)
        me = 2 * x + y

        def half(ref, chip_index, core):
            rows = ref.shape[2] // 2
            return ref.at[chip_index, :, pl.ds(core * rows, rows), :]

        def src_half(ref, core):
            rows = ref.shape[1] // 2
            return ref.at[:, pl.ds(core * rows, rows), :]

        def copy(a, k, src, dst, to):
            return pltpu.make_async_remote_copy(src_ref=src, dst_ref=dst, send_sem=send_sems.at[6 * a + k], recv_sem=recv_sems.at[6 * a + k],
                                                device_id=to, device_id_type=MESH)

        mine = [pltpu.make_async_copy(ins[a], outs[a].at[me], local_sems.at[a]) for a in range(n)]
        for cp in mine:
            cp.start()
        first = [copy(a, j, src_half(ins[a], c), half(outs[a], me, c), (*chip, c)) for a in range(n) for j, chip in enumerate(chips)]
        for cp in first:
            cp.start()
        passed = []
        for a in range(n):
            for j, (px, py) in enumerate(chips):
                landed = half(outs[a], 2 * px + py, c)
                copy(a, j, landed, landed, (px, py, c)).wait_recv()
                fwd = copy(a, 3 + j, landed, landed, sibling)
                fwd.start()
                passed.append(fwd)
        for a in range(n):
            for j, (px, py) in enumerate(chips):
                other = half(outs[a], 2 * px + py, 1 - c)
                copy(a, 3 + j, other, other, sibling).wait_recv()
        for cp in first + passed:
            cp.wait_send()
        for cp in mine:
            cp.wait()

    return pl.pallas_call(
        body, name="all_gather_weights",
        in_specs=[HBM_SPEC] * n, out_specs=[HBM_SPEC] * n,
        out_shape=[jax.ShapeDtypeStruct((N_CHIPS,) + s.shape, s.dtype) for s in shards],
        scratch_shapes=[pltpu.SemaphoreType.DMA((6 * n,)), pltpu.SemaphoreType.DMA((6 * n,)), pltpu.SemaphoreType.DMA((n,))],
    )(*shards)


def _exchange_call(name, inputs, out_shapes, plan):
    n_in, n_out = len(inputs), len(out_shapes)

    def body(*refs):
        ins, outs = refs[:n_in], refs[n_in:n_in + n_out]
        send_sems, recv_sems, local_sems = refs[n_in + n_out:]
        remote, local = plan(ins, outs, _my_place())
        local_copies = [pltpu.make_async_copy(src, dst, local_sems.at[k]) for k, (src, dst) in enumerate(local)]
        for cp in local_copies:
            cp.start()
        sends = [pltpu.make_async_remote_copy(src_ref=src, dst_ref=dst, send_sem=send_sems.at[k], recv_sem=recv_sems.at[k],
                                              device_id=dev, device_id_type=MESH) for k, (src, dst, dev, _) in enumerate(remote)]
        for cp in sends:
            cp.start()
        for k, (src, _, dev, incoming) in enumerate(remote):
            pltpu.make_async_remote_copy(src_ref=src, dst_ref=incoming, send_sem=send_sems.at[k], recv_sem=recv_sems.at[k],
                                         device_id=dev, device_id_type=MESH).wait_recv()
        for cp in sends:
            cp.wait_send()
        for cp in local_copies:
            cp.wait()

    n_remote, n_local = plan.counts
    return pl.pallas_call(
        body, name=name, in_specs=[HBM_SPEC] * n_in, out_specs=[HBM_SPEC] * n_out, out_shape=out_shapes,
        scratch_shapes=[pltpu.SemaphoreType.DMA((max(n_remote, 1),)), pltpu.SemaphoreType.DMA((max(n_remote, 1),)),
                        pltpu.SemaphoreType.DMA((max(n_local, 1),))],
    )(*inputs)


def _swap_halves(grads):
    def plan(ins, outs, place):
        x, y, c = place
        sibling = (x, y, 1 - c)
        return [(ins[a].at[:, 1 - c], outs[a], sibling, outs[a]) for a in range(len(grads))], []

    plan.counts = (len(grads), 0)
    shapes = [jax.ShapeDtypeStruct((g.shape[0],) + g.shape[2:], g.dtype) for g in grads]
    return _exchange_call("swap_grad_halves", grads, shapes, plan)


def _scatter_partials(partials, pieces, piece_shapes):
    n = len(partials)

    def plan(ins, outs, place):
        x, y, c = place
        me = 2 * x + y
        remote, local = [], []
        for a in range(n):
            local.append((pieces[a](ins[a], me), outs[a].at[me]))
            for (px, py) in _other_chips(x, y):
                q = 2 * px + py
                remote.append((pieces[a](ins[a], q), outs[a].at[me], (px, py, c), outs[a].at[q]))
        return remote, local

    plan.counts = (3 * n, n)
    shapes = [jax.ShapeDtypeStruct((N_CHIPS,) + tuple(ps), p.dtype) for ps, p in zip(piece_shapes, partials)]
    return _exchange_call("scatter_chip_partials", partials, shapes, plan)


def _join_halves(halves):
    n = len(halves)

    def plan(ins, outs, place):
        x, y, c = place
        sibling = (x, y, 1 - c)
        remote, local = [], []
        for a in range(n):
            rows = ins[a].shape[1]
            mine = outs[a].at[:, pl.ds(c * rows, rows), :]
            theirs = outs[a].at[:, pl.ds((1 - c) * rows, rows), :]
            local.append((ins[a], mine))
            remote.append((ins[a], mine, sibling, theirs))
        return remote, local

    plan.counts = (n, n)
    shapes = [jax.ShapeDtypeStruct((h.shape[0], 2 * h.shape[1], h.shape[2]), h.dtype) for h in halves]
    return _exchange_call("join_grad_halves", halves, shapes, plan)


def _all_gather_small(buf):
    rows, cols = buf.shape

    def body(x_ref, out_ref, send_sems, recv_sems, local_sem):
        x, y, c = _my_place()
        me, sibling = (x, y, c), (x, y, 1 - c)
        chips = _other_chips(x, y)

        def block(px, py, pc):
            return out_ref.at[4 * px + 2 * py + pc]

        def copy(k, blk, to, src=None):
            return pltpu.make_async_remote_copy(src_ref=block(*blk) if src is None else src, dst_ref=block(*blk), send_sem=send_sems.at[k],
                                                recv_sem=recv_sems.at[k], device_id=to, device_id_type=MESH)

        mine = pltpu.make_async_copy(x_ref, block(*me), local_sem)
        mine.start()
        first = [copy(0, me, sibling, src=x_ref)] + [copy(1 + j, me, (*chip, c), src=x_ref) for j, chip in enumerate(chips)]
        for cp in first:
            cp.start()
        passed = [copy(4 + j, (*chip, c), sibling) for j, chip in enumerate(chips)]
        for j, chip in enumerate(chips):
            copy(1 + j, (*chip, c), me).wait_recv()
            passed[j].start()
        copy(0, sibling, me).wait_recv()
        for j, chip in enumerate(chips):
            copy(4 + j, (*chip, 1 - c), me).wait_recv()
        for cp in first + passed:
            cp.wait_send()
        mine.wait()

    return pl.pallas_call(
        body, name="all_gather_small_grads", in_specs=[HBM_SPEC], out_specs=HBM_SPEC,
        out_shape=jax.ShapeDtypeStruct((2 * N_CHIPS, rows, cols), buf.dtype),
        scratch_shapes=[pltpu.SemaphoreType.DMA((7,)), pltpu.SemaphoreType.DMA((7,)), pltpu.SemaphoreType.DMA],
    )(buf)


BIG_IN = ("ffn1_w_in", "ffn2_w_in", "ab_w_in", "sgu_w_in")
BIG_OUT = ("ffn1_w_out", "ffn2_w_out", "ab_w_out", "sgu_w_out")
DEPTH = 2


def _local_step(x, tgt, p, gw):
    big, small = {}, {}
    row = lambda v: v.reshape(1, -1)
    saved = []
    cur = x
    for i in range(DEPTH):
        st = {"xa": cur}
        cur, st["h1"], st["xn1"] = _ffn_fwd(cur, row(p["ffn1_norm"][i]), gw["ffn1_w_in"], gw["ffn1_w_out"], i, f"ffn1_fwd_{i}")
        st["xb"] = cur
        if i % 2 == 0:
            st["h0"], st["xnm"] = _norm_matmul(cur, row(p["mix_norm"][i]), gw["ab_w_in"], 0, "mix0_proj_in")
            cur, st["ycat"] = _mix0_fwd(cur, st["h0"], p["pool_w"][0], row(p["pool_b"][0]), row(p["pool_scale"][0]), p["conv_w"][0],
                                        row(p["conv_b"][0]), row(p["conv_ln_g"][0]), row(p["conv_ln_b"][0]), gw["ab_w_out"], "mix0_fwd")
        else:
            st["pre"], st["xnm"] = _norm_matmul(cur, row(p["mix_norm"][i]), gw["sgu_w_in"], 0, "sgu_proj_in")
            cur, st["p"] = _sgu_fwd(cur, st["pre"], row(p["sgu_ln_g"][0]), row(p["sgu_ln_b"][0]), p["sgu_w"][0], p["sgu_b"][0].T,
                                    gw["sgu_w_out"], "sgu_fwd")
        st["xc"] = cur
        cur, st["h2"], st["xn2"] = _ffn_fwd(cur, row(p["ffn2_norm"][i]), gw["ffn2_w_in"], gw["ffn2_w_out"], i, f"ffn2_fwd_{i}")
        saved.append(st)
    dy, loss, small["final_norm"] = _final_loss(cur, tgt, row(p["final_norm"]), "final_loss")
    norm_grads = {"ffn1_norm": [None] * DEPTH, "mix_norm": [None] * DEPTH, "ffn2_norm": [None] * DEPTH}
    for i in reversed(range(DEPTH)):
        st = saved[i]
        dy_in = dy
        dy, dh, act, norm_grads["ffn2_norm"][i] = _ffn_bwd(st["xc"], dy_in, st["h2"], row(p["ffn2_norm"][i]), gw["ffn2_w_in"],
                                                            gw["ffn2_w_out"], i, f"ffn2_bwd_{i}")
        big[("ffn2_w_in", i)] = _tn_matmul(st["xn2"], dh, 1.0, 1024, 1408, f"ffn2_dwin_{i}")
        big[("ffn2_w_out", i)] = _tn_matmul(act, dy_in, 0.5, 1408, 1024, f"ffn2_dwout_{i}")
        dy_in = dy
        if i % 2 == 0:
            cw = p["conv_w"][0]
            dconv, dpc, dpw, rows = _mix0_bwd_a(dy_in, st["h0"], p["pool_w"][0], row(p["pool_b"][0]), row(p["pool_scale"][0]), cw,
                                                row(p["conv_b"][0]), row(p["conv_ln_g"][0]), row(p["conv_ln_b"][0]), gw["ab_w_out"], "mix0_bwd_a")
            dy, dh0, norm_grads["mix_norm"][i] = _mix0_bwd_b(st["xb"], dy_in, st["h0"], dconv, dpc, row(p["mix_norm"][i]), cw,
                                                              gw["ab_w_in"], "mix0_bwd_b")
            big[("ab_w_in", 0)] = _tn_matmul(st["xnm"], dh0, 1.0, 1024, 1536, "ab_dwin")
            big[("ab_w_out", 0)] = _tn_matmul(st["ycat"], dy_in, 1.0, 1024, 1024, "ab_dwout")
            small["pool_w"] = dpw[None]
            small["conv_w"] = rows[None, 0:CONV_WIDTH]
            small["conv_b"], small["conv_ln_g"], small["conv_ln_b"] = rows[32:33], rows[33:34], rows[34:35]
            small["pool_scale"] = rows[35:36]
            small["pool_b"] = rows[36:37].reshape(1, len(POOL_WINDOWS), POOL_GC)
        else:
            dy, dpre, norm_grads["mix_norm"][i], dlg, dlb, dw, dbt = _sgu_bwd(
                st["xb"], dy_in, st["pre"], row(p["mix_norm"][i]), row(p["sgu_ln_g"][0]), row(p["sgu_ln_b"][0]), p["sgu_w"][0],
                p["sgu_b"][0].T, gw["sgu_w_in"], gw["sgu_w_out"], "sgu_bwd")
            big[("sgu_w_in", 0)] = _tn_matmul(st["xnm"], dpre, 1.0, 1024, 2048, "sgu_dwin")
            big[("sgu_w_out", 0)] = _tn_matmul(st["p"], dy_in, 1.0, 1024, 1024, "sgu_dwout")
            small["sgu_ln_g"], small["sgu_ln_b"] = dlg, dlb
            small["sgu_w"] = dw[None]
            small["sgu_b"] = dbt.T[None]
        dy_in = dy
        dy, dh, act, norm_grads["ffn1_norm"][i] = _ffn_bwd(st["xa"], dy_in, st["h1"], row(p["ffn1_norm"][i]), gw["ffn1_w_in"],
                                                            gw["ffn1_w_out"], i, f"ffn1_bwd_{i}")
        big[("ffn1_w_in", i)] = _tn_matmul(st["xn1"], dh, 1.0, 1024, 1408, f"ffn1_dwin_{i}")
        big[("ffn1_w_out", i)] = _tn_matmul(act, dy_in, 0.5, 1408, 1024, f"ffn1_dwout_{i}")
    for k, v in norm_grads.items():
        small[k] = jnp.concatenate(v, axis=0)
    small["final_norm"] = small["final_norm"].reshape(-1)
    return loss, dy, big, small


WEIGHT_NAMES = ("ffn1_norm", "ffn1_w_in", "ffn1_w_out", "mix_norm", "ffn2_norm", "ffn2_w_in", "ffn2_w_out", "ab_w_in", "pool_w", "pool_b",
                "pool_scale", "conv_w", "conv_b", "conv_ln_g", "conv_ln_b", "ab_w_out", "sgu_w_in", "sgu_ln_g", "sgu_ln_b", "sgu_w", "sgu_b",
                "sgu_w_out", "final_norm")
BIG = BIG_IN + BIG_OUT
SHARDED_SMALL = ("conv_w", "sgu_ln_g", "sgu_ln_b")
PACK_ROWS = 48


def _gather_parameters(w):
    pack = jnp.concatenate([
        w["conv_w"][0], jnp.zeros((1, 128), F32), w["sgu_ln_g"].reshape(2, 128), w["sgu_ln_b"].reshape(2, 128),
        jnp.zeros((PACK_ROWS - 36, 128), F32)], axis=0)[None]
    shards = [_cast_bf16(w[n], f"cast_{n}") for n in BIG] + [pack]
    gathered = _all_gather_weights(shards)
    gw = dict(zip(BIG, gathered[:-1]))
    gp = gathered[-1][:, 0]
    conv_w = jnp.transpose(gp[:, 0:CONV_WIDTH], (1, 0, 2)).reshape(1, CONV_WIDTH, N_CHIPS * 128)
    ln_g = gp[:, 32:34].reshape(1, -1)
    ln_b = gp[:, 34:36].reshape(1, -1)
    return gw, conv_w, ln_g, ln_b


def _reduce_big(big):
    keys = sorted(big)
    views, pieces, piece_shapes = [], [], []
    for name, _ in keys:
        g = big[(name, _)]
        m, n = g.shape
        if name in BIG_IN:
            views.append(g.reshape(1, 2, m // 2, n))
            cq = n // N_CHIPS
            pieces.append(lambda ref, q, cq=cq: ref.at[0, :, pl.ds(q * cq, cq)])
            piece_shapes.append((m // 2, cq))
        else:
            views.append(g.reshape(N_CHIPS, 2, m // (2 * N_CHIPS), n))
            pieces.append(lambda ref, q: ref.at[q])
            piece_shapes.append((m // (2 * N_CHIPS), n))
    c = lax.axis_index("c")
    theirs = _swap_halves(views)
    partial = [_add_bf16(lax.dynamic_index_in_dim(v, c, axis=1, keepdims=False), o, f"chip_partial_{k[0]}_{k[1]}")
               for v, o, k in zip(views, theirs, keys)]
    staged = _scatter_partials(partial, pieces, piece_shapes)
    names = []
    for name, _ in keys:
        if name not in names:
            names.append(name)
    halves = []
    for name in names:
        layers = [s for s, k in zip(staged, keys) if k[0] == name]
        stacked = layers[0][:, None] if len(layers) == 1 else jnp.stack(layers, axis=1)
        halves.append(_sum_leading(stacked, f"reduce_{name}"))
    return dict(zip(names, _join_halves(halves)))


SMALL = tuple(n for n in WEIGHT_NAMES if n not in BIG)


def _reduce_small(small):
    flat = [small[n].reshape(-1, 128) for n in SMALL]
    sizes = [f.shape[0] for f in flat]
    buf = jnp.concatenate(flat, axis=0)
    pad = (-buf.shape[0]) % 8
    if pad:
        buf = jnp.concatenate([buf, jnp.zeros((pad, 128), F32)], axis=0)
    total = _sum_leading(_all_gather_small(buf), "reduce_small")
    out, at = {}, 0
    for n, s in zip(SMALL, sizes):
        out[n] = total[at:at + s].reshape(small[n].shape)
        at += s
    return out


def kernel(x, ffn1_norm, ffn1_w_in, ffn1_w_out, mix_norm, ffn2_norm, ffn2_w_in, ffn2_w_out, ab_w_in, pool_w, pool_b, pool_scale, conv_w, conv_b, conv_ln_g, conv_ln_b, ab_w_out, sgu_w_in, sgu_ln_g, sgu_ln_b, sgu_w, sgu_b, sgu_w_out, final_norm, loss_target, m_ffn1_norm, m_ffn1_w_in, m_ffn1_w_out, m_mix_norm, m_ffn2_norm, m_ffn2_w_in, m_ffn2_w_out, m_ab_w_in, m_pool_w, m_pool_b, m_pool_scale, m_conv_w, m_conv_b, m_conv_ln_g, m_conv_ln_b, m_ab_w_out, m_sgu_w_in, m_sgu_ln_g, m_sgu_ln_b, m_sgu_w, m_sgu_b, m_sgu_w_out, m_final_norm, v_ffn1_norm, v_ffn1_w_in, v_ffn1_w_out, v_mix_norm, v_ffn2_norm, v_ffn2_w_in, v_ffn2_w_out, v_ab_w_in, v_pool_w, v_pool_b, v_pool_scale, v_conv_w, v_conv_b, v_conv_ln_g, v_conv_ln_b, v_ab_w_out, v_sgu_w_in, v_sgu_ln_g, v_sgu_ln_b, v_sgu_w, v_sgu_b, v_sgu_w_out, v_final_norm):
    given = dict(locals())
    w = {n: given[n] for n in WEIGHT_NAMES}
    gw, conv_w_full, ln_g_full, ln_b_full = _gather_parameters(w)
    p = {n: w[n] for n in SMALL}
    p.update(conv_w=conv_w_full, sgu_ln_g=ln_g_full, sgu_ln_b=ln_b_full)
    loss, grad_x, big, small = _local_step(x[0], loss_target[0], p, gw)
    grads = _reduce_big(big)
    small_sum = _reduce_small(small)
    chip = 2 * lax.axis_index("x") + lax.axis_index("y")
    for n in SMALL:
        g = small_sum[n]
        if n in SHARDED_SMALL:
            width = w[n].shape[-1]
            g = lax.dynamic_slice_in_dim(g, chip * width, width, axis=g.ndim - 1)
        grads[n] = g
    loss = lax.psum(loss[0, 0], MESH_AXES)
    delta, new_m, new_v = {}, {}, {}
    for n in WEIGHT_NAMES:
        delta[n], new_m[n], new_v[n] = _adamw(w[n], grads[n], given["m_" + n], given["v_" + n], f"adamw_{n}")
    return (loss, grad_x[None], *[grads[n] for n in WEIGHT_NAMES], *[delta[n] for n in WEIGHT_NAMES],
            *[new_m[n] for n in WEIGHT_NAMES], *[new_v[n] for n in WEIGHT_NAMES])
```

```python
import functools

import jax
import jax.numpy as jnp
from jax import lax
from jax.experimental import pallas as pl
from jax.experimental.pallas import tpu as pltpu

F32, BF16 = jnp.float32, jnp.bfloat16
EPS = 1e-6
N_CHIPS = 4
POOL_WINDOWS = (2, 4, 8, 16)
POOL_GC = 128
POOL_CH = 512
CONV_CH = 512
CONV_WIDTH = 31
HALO = 32
SGU_HEADS = 8
CHUNK = 128
ADAM_LR, ADAM_B1, ADAM_B2, ADAM_EPS, ADAM_WD, ADAM_STEP = 0.001, 0.9, 0.999, 1e-08, 0.01, 10
VMEM_LIMIT_BYTES = 60 * 1024 * 1024
MESH_AXES = ("x", "y", "c")
MESH = pl.DeviceIdType.MESH


def _cparams():
    return pltpu.CompilerParams(dimension_semantics=("arbitrary",), vmem_limit_bytes=VMEM_LIMIT_BYTES)


def _dot(a, b):
    return jnp.dot(a, b, preferred_element_type=F32)


def _dot_nt(a, b):
    return lax.dot_general(a, b, (((1,), (1,)), ((), ())), preferred_element_type=F32)


def _dot_tn(a, b):
    return lax.dot_general(a, b, (((0,), (0,)), ((), ())), preferred_element_type=F32)


def _rms_fwd(x):
    r = lax.rsqrt(jnp.mean(x * x, axis=-1, keepdims=True) + EPS)
    return x * r, r


def _rms_bwd(dxn, xh, r, g):
    dxh = dxn * g
    return r * (dxh - xh * jnp.mean(dxh * xh, axis=-1, keepdims=True))


def _ln_fwd(y):
    mu = jnp.mean(y, axis=-1, keepdims=True)
    yc = y - mu
    rs = lax.rsqrt(jnp.mean(yc * yc, axis=-1, keepdims=True) + EPS)
    return yc * rs, rs


def _ln_bwd(dyhat, yhat, rs):
    return rs * (dyhat - jnp.mean(dyhat, axis=-1, keepdims=True) - yhat * jnp.mean(dyhat * yhat, axis=-1, keepdims=True))


def _sigmoid(x):
    return 1.0 / (1.0 + jnp.exp(-x))


def _const_spec(shape):
    n = len(shape)
    return pl.BlockSpec(shape, lambda i: (0,) * n)


def _row_spec(tm, cols):
    return pl.BlockSpec((tm, cols), lambda i: (i, 0))


HBM_SPEC = pl.BlockSpec(memory_space=pltpu.HBM)


def _in_weight_copies(w_hbm, layer, w_v, sem, base=0):
    return [pltpu.make_async_copy(w_hbm.at[q, layer], w_v.at[q], sem.at[base + q]) for q in range(N_CHIPS)]


def _out_weight_copies(w_hbm, layer, w_v, sem, base=0):
    rows = w_hbm.shape[2]
    return [pltpu.make_async_copy(w_hbm.at[q, layer], w_v.at[pl.ds(q * rows, rows)], sem.at[base + q]) for q in range(N_CHIPS)]


def _load_at_first_step(copies):
    @pl.when(pl.program_id(0) == 0)
    def _():
        for cp in copies:
            cp.start()
        for cp in copies:
            cp.wait()


def _ffn_fwd(x, g, win_g, wout_g, layer, name):
    t, d = x.shape
    c = win_g.shape[-1]
    ff = 2 * c
    tm = min(256, t)

    def body(x_ref, g_ref, win_hbm, wout_hbm, xo_ref, h_ref, xn_ref, win_v, wout_v, sem):
        _load_at_first_step(_in_weight_copies(win_hbm, layer, win_v, sem) + _out_weight_copies(wout_hbm, layer, wout_v, sem, N_CHIPS))
        xv = x_ref[...]
        xh, _ = _rms_fwd(xv)
        xn = (xh * g_ref[...]).astype(BF16)
        xn_ref[...] = xn
        acc = jnp.zeros((tm, d), F32)
        for j in range(2):
            gate = _dot(xn, win_v[j])
            up = _dot(xn, win_v[j + 2])
            h_ref[:, j * c:(j + 1) * c] = gate.astype(BF16)
            h_ref[:, ff + j * c:ff + (j + 1) * c] = up.astype(BF16)
            act = (gate * _sigmoid(gate) * up).astype(BF16)
            acc = acc + _dot(act, wout_v[j * c:(j + 1) * c, :])
        xo_ref[...] = xv + 0.5 * acc

    return pl.pallas_call(
        body, name=name, grid=(t // tm,),
        in_specs=[_row_spec(tm, d), _const_spec((1, d)), HBM_SPEC, HBM_SPEC],
        out_specs=[_row_spec(tm, d), _row_spec(tm, 2 * ff), _row_spec(tm, d)],
        out_shape=[jax.ShapeDtypeStruct((t, d), F32), jax.ShapeDtypeStruct((t, 2 * ff), BF16), jax.ShapeDtypeStruct((t, d), BF16)],
        scratch_shapes=[pltpu.VMEM((N_CHIPS, d, c), BF16), pltpu.VMEM((ff, d), BF16), pltpu.SemaphoreType.DMA((2 * N_CHIPS,))],
        compiler_params=_cparams(),
    )(x, g, win_g, wout_g)


def _ffn_bwd(x, dy, h, g, win_g, wout_g, layer, name):
    t, d = x.shape
    c = win_g.shape[-1]
    ff = 2 * c
    tm = min(256, t)

    def body(x_ref, dy_ref, h_ref, g_ref, win_hbm, wout_hbm, dx_ref, dh_ref, act_ref, dg_ref, win_v, wout_v, sem):
        _load_at_first_step(_in_weight_copies(win_hbm, layer, win_v, sem) + _out_weight_copies(wout_hbm, layer, wout_v, sem, N_CHIPS))

        @pl.when(pl.program_id(0) == 0)
        def _():
            dg_ref[...] = jnp.zeros_like(dg_ref)

        xv, dyv, gv = x_ref[...], dy_ref[...], g_ref[...]
        xh, r = _rms_fwd(xv)
        dyh = (0.5 * dyv).astype(BF16)
        dxn = jnp.zeros((tm, d), F32)
        for j in range(2):
            gate = h_ref[:, j * c:(j + 1) * c].astype(F32)
            up = h_ref[:, ff + j * c:ff + (j + 1) * c].astype(F32)
            dact = _dot_nt(dyh, wout_v[j * c:(j + 1) * c, :])
            s = _sigmoid(gate)
            sl = gate * s
            act_ref[:, j * c:(j + 1) * c] = (sl * up).astype(BF16)
            dgate = (dact * up * (s * (1.0 + gate * (1.0 - s)))).astype(BF16)
            dup = (dact * sl).astype(BF16)
            dh_ref[:, j * c:(j + 1) * c] = dgate
            dh_ref[:, ff + j * c:ff + (j + 1) * c] = dup
            dxn = dxn + _dot_nt(dgate, win_v[j]) + _dot_nt(dup, win_v[j + 2])
        dg_ref[...] += jnp.sum(dxn * xh, axis=0, keepdims=True)
        dx_ref[...] = dyv + _rms_bwd(dxn, xh, r, gv)

    return pl.pallas_call(
        body, name=name, grid=(t // tm,),
        in_specs=[_row_spec(tm, d), _row_spec(tm, d), _row_spec(tm, 2 * ff), _const_spec((1, d)), HBM_SPEC, HBM_SPEC],
        out_specs=[_row_spec(tm, d), _row_spec(tm, 2 * ff), _row_spec(tm, ff), _const_spec((1, d))],
        out_shape=[jax.ShapeDtypeStruct((t, d), F32), jax.ShapeDtypeStruct((t, 2 * ff), BF16), jax.ShapeDtypeStruct((t, ff), BF16),
                   jax.ShapeDtypeStruct((1, d), F32)],
        scratch_shapes=[pltpu.VMEM((N_CHIPS, d, c), BF16), pltpu.VMEM((ff, d), BF16), pltpu.SemaphoreType.DMA((2 * N_CHIPS,))],
        compiler_params=_cparams(),
    )(x, dy, h, g, win_g, wout_g)


def _norm_matmul(x, g, win_g, layer, name):
    t, d = x.shape
    c = win_g.shape[-1]
    tm = min(512, t)

    def body(x_ref, g_ref, win_hbm, o_ref, xn_ref, win_v, sem):
        _load_at_first_step(_in_weight_copies(win_hbm, layer, win_v, sem))
        xh, _ = _rms_fwd(x_ref[...])
        xn = (xh * g_ref[...]).astype(BF16)
        xn_ref[...] = xn
        for q in range(N_CHIPS):
            o_ref[:, q * c:(q + 1) * c] = _dot(xn, win_v[q])

    return pl.pallas_call(
        body, name=name, grid=(t // tm,),
        in_specs=[_row_spec(tm, d), _const_spec((1, d)), HBM_SPEC],
        out_specs=[_row_spec(tm, N_CHIPS * c), _row_spec(tm, d)],
        out_shape=[jax.ShapeDtypeStruct((t, N_CHIPS * c), F32), jax.ShapeDtypeStruct((t, d), BF16)],
        scratch_shapes=[pltpu.VMEM((N_CHIPS, d, c), BF16), pltpu.SemaphoreType.DMA((N_CHIPS,))],
        compiler_params=_cparams(),
    )(x, g, win_g)


def _proj_in_bwd_tail(dh, win_v, c):
    dxn = _dot_nt(dh[:, 0:c], win_v[0])
    for q in range(1, N_CHIPS):
        dxn = dxn + _dot_nt(dh[:, q * c:(q + 1) * c], win_v[q])
    return dxn


def _prev_halo_spec(tm, cols):
    return pl.BlockSpec((HALO, cols), lambda i: (jnp.maximum(i * (tm // HALO) - 1, 0), 0))


def _next_halo_spec(tm, cols, t):
    last = t // HALO - 1
    return pl.BlockSpec((HALO, cols), lambda i: (jnp.minimum((i + 1) * (tm // HALO), last), 0))


def _shift_down(w, k):
    return w if k == 0 else pltpu.roll(w, k, 0)


def _shift_up(w, k):
    return w if k == 0 else pltpu.roll(w, w.shape[0] - k, 0)


def _pool_counts(i, tm):
    pos = (i * tm + lax.broadcasted_iota(jnp.int32, (tm, POOL_CH), 0) + 1).astype(F32)
    lane = lax.broadcasted_iota(jnp.int32, (tm, POOL_CH), 1)
    win = jnp.where(lane < POOL_GC, 2.0, jnp.where(lane < 2 * POOL_GC, 4.0, jnp.where(lane < 3 * POOL_GC, 8.0, 16.0)))
    return jnp.minimum(pos, win)


def _group_select(parts):
    return jnp.concatenate([p[:, k * POOL_GC:(k + 1) * POOL_GC] for k, p in enumerate(parts)], axis=1)


def _mix0_recompute(i, tm, h_cur, h_prev, conv_w, conv_b):
    prev = jnp.where(i > 0, h_prev, 0.0)
    win = jnp.concatenate([prev, h_cur], axis=0)
    u_w = win[:, 0:POOL_CH]
    a_w = win[:, POOL_CH:POOL_CH + CONV_CH]
    gt_w = win[:, POOL_CH + CONV_CH:]
    g_w = a_w * _sigmoid(gt_w)
    y = jnp.zeros((tm, CONV_CH), F32)
    for k in range(CONV_WIDTH):
        y = y + conv_w[k:k + 1, :] * _shift_down(g_w, CONV_WIDTH - 1 - k)[HALO:, :]
    y = y + conv_b
    s2 = u_w + _shift_down(u_w, 1)
    s4 = s2 + _shift_down(s2, 2)
    s8 = s4 + _shift_down(s4, 4)
    s16 = s8 + _shift_down(s8, 8)
    sums = _group_select([s2[HALO:], s4[HALO:], s8[HALO:], s16[HALO:]])
    cnt = _pool_counts(i, tm)
    pooled = sums / cnt - h_cur[:, 0:POOL_CH]
    return g_w, y, pooled, cnt


def _pool_linear(pooled, pw_ref, pb, scale):
    mixed = jnp.concatenate(
        [_dot(pooled[:, k * POOL_GC:(k + 1) * POOL_GC].astype(BF16), pw_ref[k].astype(BF16)) for k in range(len(POOL_WINDOWS))], axis=1) + pb
    return mixed


def _mix0_fwd(x, h0, pool_w, pool_b, pool_scale, conv_w, conv_b, ln_g, ln_b, wout_g, name):
    t, d = x.shape
    tm = min(256, t)
    hc = h0.shape[1]

    def body(x_ref, h_ref, hp_ref, pw_ref, pb_ref, ps_ref, cw_ref, cb_ref, lg_ref, lb_ref, wout_hbm, xo_ref, ycat_ref, wout_v, sem):
        _load_at_first_step(_out_weight_copies(wout_hbm, 0, wout_v, sem))
        i = pl.program_id(0)
        _, y, pooled, _ = _mix0_recompute(i, tm, h_ref[...], hp_ref[...], cw_ref[...], cb_ref[...])
        yhat, _ = _ln_fwd(y)
        yn = yhat * lg_ref[...] + lb_ref[...]
        yb = yn * _sigmoid(yn)
        ya = _pool_linear(pooled, pw_ref, pb_ref[...], None) * ps_ref[...]
        ycat = jnp.concatenate([ya, yb], axis=1).astype(BF16)
        ycat_ref[...] = ycat
        xo_ref[...] = x_ref[...] + _dot(ycat, wout_v[...])

    return pl.pallas_call(
        body, name=name, grid=(t // tm,),
        in_specs=[_row_spec(tm, d), _row_spec(tm, hc), _prev_halo_spec(tm, hc), _const_spec(pool_w.shape), _const_spec((1, POOL_CH)),
                  _const_spec((1, POOL_CH)), _const_spec(conv_w.shape), _const_spec((1, CONV_CH)), _const_spec((1, CONV_CH)),
                  _const_spec((1, CONV_CH)), HBM_SPEC],
        out_specs=[_row_spec(tm, d), _row_spec(tm, d)],
        out_shape=[jax.ShapeDtypeStruct((t, d), F32), jax.ShapeDtypeStruct((t, d), BF16)],
        scratch_shapes=[pltpu.VMEM((d, d), BF16), pltpu.SemaphoreType.DMA((N_CHIPS,))],
        compiler_params=_cparams(),
    )(x, h0, h0, pool_w, pool_b, pool_scale, conv_w, conv_b, ln_g, ln_b, wout_g)


def _mix0_bwd_a(dy, h0, pool_w, pool_b, pool_scale, conv_w, conv_b, ln_g, ln_b, wout_g, name):
    t, d = dy.shape
    tm = min(256, t)
    hc = h0.shape[1]
    n_small = 40

    def body(dy_ref, h_ref, hp_ref, pw_ref, pb_ref, ps_ref, cw_ref, cb_ref, lg_ref, lb_ref, wout_hbm,
             dconv_ref, dpc_ref, dpw_ref, small_ref, wout_v, sem):
        _load_at_first_step(_out_weight_copies(wout_hbm, 0, wout_v, sem))
        i = pl.program_id(0)

        @pl.when(i == 0)
        def _():
            dpw_ref[...] = jnp.zeros_like(dpw_ref)
            small_ref[...] = jnp.zeros_like(small_ref)

        g_w, y, pooled, cnt = _mix0_recompute(i, tm, h_ref[...], hp_ref[...], cw_ref[...], cb_ref[...])
        yhat, rs = _ln_fwd(y)
        lg = lg_ref[...]
        yn = yhat * lg + lb_ref[...]
        mixed = _pool_linear(pooled, pw_ref, pb_ref[...], None)
        dycat = _dot_nt(dy_ref[...].astype(BF16), wout_v[...])
        dya, dyb = dycat[:, 0:POOL_CH], dycat[:, POOL_CH:]
        sg = _sigmoid(yn)
        dyn = dyb * (sg * (1.0 + yn * (1.0 - sg)))
        dyc = _ln_bwd(dyn * lg, yhat, rs)
        dconv_ref[...] = dyc
        def add_row(k, value):
            small_ref[k:k + 1, :] += jnp.sum(value, axis=0, keepdims=True)

        for k in range(CONV_WIDTH):
            add_row(k, dyc * _shift_down(g_w, CONV_WIDTH - 1 - k)[HALO:, :])
        add_row(32, dyc)
        add_row(33, dyn * yhat)
        add_row(34, dyn)
        scale = ps_ref[...]
        dmixed = dya * scale
        add_row(35, dya * mixed)
        add_row(36, dmixed)
        dmb = dmixed.astype(BF16)
        dpooled = []
        for k in range(len(POOL_WINDOWS)):
            sl = slice(k * POOL_GC, (k + 1) * POOL_GC)
            dpw_ref[k] += _dot_tn(pooled[:, sl].astype(BF16), dmb[:, sl])
            dpooled.append(_dot_nt(dmb[:, sl], pw_ref[k].astype(BF16)))
        dpc_ref[...] = jnp.concatenate(dpooled, axis=1) / cnt

    return pl.pallas_call(
        body, name=name, grid=(t // tm,),
        in_specs=[_row_spec(tm, d), _row_spec(tm, hc), _prev_halo_spec(tm, hc), _const_spec(pool_w.shape), _const_spec((1, POOL_CH)),
                  _const_spec((1, POOL_CH)), _const_spec(conv_w.shape), _const_spec((1, CONV_CH)), _const_spec((1, CONV_CH)),
                  _const_spec((1, CONV_CH)), HBM_SPEC],
        out_specs=[_row_spec(tm, CONV_CH), _row_spec(tm, POOL_CH), _const_spec(pool_w.shape), _const_spec((n_small, CONV_CH))],
        out_shape=[jax.ShapeDtypeStruct((t, CONV_CH), F32), jax.ShapeDtypeStruct((t, POOL_CH), F32),
                   jax.ShapeDtypeStruct(pool_w.shape, F32), jax.ShapeDtypeStruct((n_small, CONV_CH), F32)],
        scratch_shapes=[pltpu.VMEM((d, d), BF16), pltpu.SemaphoreType.DMA((N_CHIPS,))],
        compiler_params=_cparams(),
    )(dy, h0, h0, pool_w, pool_b, pool_scale, conv_w, conv_b, ln_g, ln_b, wout_g)


def _mix0_bwd_b(x, dy, h0, dconv, dpc, g, conv_w, win_g, name):
    t, d = x.shape
    tm = min(256, t)
    hc = h0.shape[1]
    c = win_g.shape[-1]
    n_tiles = t // tm

    def body(x_ref, dy_ref, h_ref, dc_ref, dcn_ref, dp_ref, dpn_ref, g_ref, cw_ref, win_hbm, dx_ref, dh_ref, dg_ref, win_v, sem):
        _load_at_first_step(_in_weight_copies(win_hbm, 0, win_v, sem))
        i = pl.program_id(0)

        @pl.when(i == 0)
        def _():
            dg_ref[...] = jnp.zeros_like(dg_ref)

        not_last = i < n_tiles - 1
        dc_w = jnp.concatenate([dc_ref[...], jnp.where(not_last, dcn_ref[...], 0.0)], axis=0)
        dp_w = jnp.concatenate([dp_ref[...], jnp.where(not_last, dpn_ref[...], 0.0)], axis=0)
        cw = cw_ref[...]
        dg = jnp.zeros((tm, CONV_CH), F32)
        for k in range(CONV_WIDTH):
            dg = dg + cw[k:k + 1, :] * _shift_up(dc_w, CONV_WIDTH - 1 - k)[0:tm, :]
        a2 = dp_w + _shift_up(dp_w, 1)
        a4 = a2 + _shift_up(a2, 2)
        a8 = a4 + _shift_up(a4, 4)
        a16 = a8 + _shift_up(a8, 8)
        back = _group_select([a2[0:tm], a4[0:tm], a8[0:tm], a16[0:tm]])
        du = back - dp_ref[...] * _pool_counts(i, tm)
        hv = h_ref[...]
        a = hv[:, POOL_CH:POOL_CH + CONV_CH]
        sig = _sigmoid(hv[:, POOL_CH + CONV_CH:])
        dh = jnp.concatenate([du, dg * sig, dg * a * sig * (1.0 - sig)], axis=1).astype(BF16)
        dh_ref[...] = dh
        dxn = _proj_in_bwd_tail(dh, win_v, c)
        xh, r = _rms_fwd(x_ref[...])
        dg_ref[...] += jnp.sum(dxn * xh, axis=0, keepdims=True)
        dx_ref[...] = dy_ref[...] + _rms_bwd(dxn, xh, r, g_ref[...])

    return pl.pallas_call(
        body, name=name, grid=(n_tiles,),
        in_specs=[_row_spec(tm, d), _row_spec(tm, d), _row_spec(tm, hc), _row_spec(tm, CONV_CH), _next_halo_spec(tm, CONV_CH, t),
                  _row_spec(tm, POOL_CH), _next_halo_spec(tm, POOL_CH, t), _const_spec((1, d)), _const_spec(conv_w.shape), HBM_SPEC],
        out_specs=[_row_spec(tm, d), _row_spec(tm, hc), _const_spec((1, d))],
        out_shape=[jax.ShapeDtypeStruct((t, d), F32), jax.ShapeDtypeStruct((t, hc), BF16), jax.ShapeDtypeStruct((1, d), F32)],
        scratch_shapes=[pltpu.VMEM((N_CHIPS, d, c), BF16), pltpu.SemaphoreType.DMA((N_CHIPS,))],
        compiler_params=_cparams(),
    )(x, dy, h0, dconv, dconv, dpc, dpc, g, conv_w, win_g)


SQRT_HALF = 0.7071067811865476
INV_SQRT_2PI = 0.3989422804014327


def _gelu(x):
    return 0.5 * x * (1.0 + lax.erf(x * SQRT_HALF))


def _gelu_grad(x):
    return 0.5 * (1.0 + lax.erf(x * SQRT_HALF)) + x * jnp.exp(-0.5 * x * x) * INV_SQRT_2PI


def _causal_mask():
    return (lax.broadcasted_iota(jnp.int32, (CHUNK, CHUNK), 1) <= lax.broadcasted_iota(jnp.int32, (CHUNK, CHUNK), 0)).astype(F32)


def _sgu_recompute(pre, lg, lb):
    half = pre.shape[1] // 2
    z = _gelu(pre)
    u, v = z[:, 0:half], z[:, half:]
    vhat, rs = _ln_fwd(v)
    return u, vhat, rs, vhat * lg + lb


def _sgu_spatial(vln, w_ref, bt, tm):
    mask = _causal_mask()
    wm = [(w_ref[hd] * mask).astype(BF16) for hd in range(SGU_HEADS)]
    vb = vln.astype(BF16)
    rows = []
    for ch in range(tm // CHUNK):
        blocks = [_dot(wm[hd], vb[ch * CHUNK:(ch + 1) * CHUNK, hd * CHUNK:(hd + 1) * CHUNK]) + bt[:, hd:hd + 1] for hd in range(SGU_HEADS)]
        rows.append(jnp.concatenate(blocks, axis=1))
    return jnp.concatenate(rows, axis=0), wm


def _sgu_fwd(x, pre, ln_g, ln_b, w, bt, wout_g, name):
    t, d = x.shape
    tm = min(256, t)
    pc = pre.shape[1]

    def body(x_ref, pre_ref, lg_ref, lb_ref, w_ref, bt_ref, wout_hbm, xo_ref, p_ref, wout_v, sem):
        _load_at_first_step(_out_weight_copies(wout_hbm, 0, wout_v, sem))
        u, _, _, vln = _sgu_recompute(pre_ref[...], lg_ref[...], lb_ref[...])
        vo, _ = _sgu_spatial(vln, w_ref, bt_ref[...], tm)
        p = (u * vo).astype(BF16)
        p_ref[...] = p
        xo_ref[...] = x_ref[...] + _dot(p, wout_v[...])

    return pl.pallas_call(
        body, name=name, grid=(t // tm,),
        in_specs=[_row_spec(tm, d), _row_spec(tm, pc), _const_spec((1, d)), _const_spec((1, d)), _const_spec(w.shape),
                  _const_spec(bt.shape), HBM_SPEC],
        out_specs=[_row_spec(tm, d), _row_spec(tm, d)],
        out_shape=[jax.ShapeDtypeStruct((t, d), F32), jax.ShapeDtypeStruct((t, d), BF16)],
        scratch_shapes=[pltpu.VMEM((d, d), BF16), pltpu.SemaphoreType.DMA((N_CHIPS,))],
        compiler_params=_cparams(),
    )(x, pre, ln_g, ln_b, w, bt, wout_g)


def _sgu_bwd(x, dy, pre, g, ln_g, ln_b, w, bt, win_g, wout_g, name):
    t, d = x.shape
    tm = min(256, t)
    pc = pre.shape[1]
    c = win_g.shape[-1]

    def body(x_ref, dy_ref, pre_ref, g_ref, lg_ref, lb_ref, w_ref, bt_ref, win_hbm, wout_hbm,
             dx_ref, dpre_ref, dg_ref, dlg_ref, dlb_ref, dw_ref, dbt_ref, win_v, wout_v, sem):
        _load_at_first_step(_in_weight_copies(win_hbm, 0, win_v, sem) + _out_weight_copies(wout_hbm, 0, wout_v, sem, N_CHIPS))
        i = pl.program_id(0)

        @pl.when(i == 0)
        def _():
            for ref in (dg_ref, dlg_ref, dlb_ref, dw_ref, dbt_ref):
                ref[...] = jnp.zeros_like(ref)

        prev = pre_ref[...]
        lg = lg_ref[...]
        u, vhat, rs, vln = _sgu_recompute(prev, lg, lb_ref[...])
        vo, wm = _sgu_spatial(vln, w_ref, bt_ref[...], tm)
        dp = _dot_nt(dy_ref[...].astype(BF16), wout_v[...])
        du = dp * vo
        dvo = dp * u
        dvob = dvo.astype(BF16)
        vb = vln.astype(BF16)
        head_lane = lax.broadcasted_iota(jnp.int32, (CHUNK, SGU_HEADS), 1)
        dbt = jnp.zeros((CHUNK, SGU_HEADS), F32)
        dw = [jnp.zeros((CHUNK, CHUNK), F32) for _ in range(SGU_HEADS)]
        rows = []
        for ch in range(tm // CHUNK):
            rs_ = slice(ch * CHUNK, (ch + 1) * CHUNK)
            blocks = []
            for hd in range(SGU_HEADS):
                cs = slice(hd * CHUNK, (hd + 1) * CHUNK)
                dbt = dbt + jnp.where(head_lane == hd, jnp.sum(dvo[rs_, cs], axis=1, keepdims=True), 0.0)
                dw[hd] = dw[hd] + _dot_nt(dvob[rs_, cs], vb[rs_, cs])
                blocks.append(_dot_tn(wm[hd], dvob[rs_, cs]))
            rows.append(jnp.concatenate(blocks, axis=1))
        dvln = jnp.concatenate(rows, axis=0)
        mask = _causal_mask()
        for hd in range(SGU_HEADS):
            dw_ref[hd] += dw[hd] * mask
        dbt_ref[...] += dbt
        dlg_ref[...] += jnp.sum(dvln * vhat, axis=0, keepdims=True)
        dlb_ref[...] += jnp.sum(dvln, axis=0, keepdims=True)
        dv = _ln_bwd(dvln * lg, vhat, rs)
        dpre = (jnp.concatenate([du, dv], axis=1) * _gelu_grad(prev)).astype(BF16)
        dpre_ref[...] = dpre
        dxn = _proj_in_bwd_tail(dpre, win_v, c)
        xh, r = _rms_fwd(x_ref[...])
        dg_ref[...] += jnp.sum(dxn * xh, axis=0, keepdims=True)
        dx_ref[...] = dy_ref[...] + _rms_bwd(dxn, xh, r, g_ref[...])

    return pl.pallas_call(
        body, name=name, grid=(t // tm,),
        in_specs=[_row_spec(tm, d), _row_spec(tm, d), _row_spec(tm, pc), _const_spec((1, d)), _const_spec((1, d)), _const_spec((1, d)),
                  _const_spec(w.shape), _const_spec(bt.shape), HBM_SPEC, HBM_SPEC],
        out_specs=[_row_spec(tm, d), _row_spec(tm, pc), _const_spec((1, d)), _const_spec((1, d)), _const_spec((1, d)),
                   _const_spec(w.shape), _const_spec(bt.shape)],
        out_shape=[jax.ShapeDtypeStruct((t, d), F32), jax.ShapeDtypeStruct((t, pc), BF16), jax.ShapeDtypeStruct((1, d), F32),
                   jax.ShapeDtypeStruct((1, d), F32), jax.ShapeDtypeStruct((1, d), F32), jax.ShapeDtypeStruct(w.shape, F32),
                   jax.ShapeDtypeStruct(bt.shape, F32)],
        scratch_shapes=[pltpu.VMEM((N_CHIPS, d, c), BF16), pltpu.VMEM((d, d), BF16), pltpu.SemaphoreType.DMA((2 * N_CHIPS,))],
        compiler_params=_cparams(),
    )(x, dy, pre, g, ln_g, ln_b, w, bt, win_g, wout_g)


def _final_loss(x, tgt, g, name):
    t, d = x.shape
    tm = min(512, t)

    def body(x_ref, t_ref, g_ref, dx_ref, loss_ref, dg_ref):
        @pl.when(pl.program_id(0) == 0)
        def _():
            loss_ref[...] = jnp.zeros_like(loss_ref)
            dg_ref[...] = jnp.zeros_like(dg_ref)

        gv = g_ref[...]
        xh, r = _rms_fwd(x_ref[...])
        diff = xh * gv - t_ref[...]
        loss_ref[...] += 0.5 * jnp.sum(jnp.sum(diff * diff, axis=1, keepdims=True), axis=0, keepdims=True) / d
        dout = diff / d
        dg_ref[...] += jnp.sum(dout * xh, axis=0, keepdims=True)
        dx_ref[...] = _rms_bwd(dout, xh, r, gv)

    return pl.pallas_call(
        body, name=name, grid=(t // tm,),
        in_specs=[_row_spec(tm, d), _row_spec(tm, d), _const_spec((1, d))],
        out_specs=[_row_spec(tm, d), _const_spec((1, 1)), _const_spec((1, d))],
        out_shape=[jax.ShapeDtypeStruct((t, d), F32), jax.ShapeDtypeStruct((1, 1), F32), jax.ShapeDtypeStruct((1, d), F32)],
        compiler_params=_cparams(),
    )(x, tgt, g)


def _tn_matmul(a, b, scale, bm, bn, name):
    t, m = a.shape
    n = b.shape[1]
    tk = min(512, t)
    bm, bn = min(bm, m), min(bn, n)
    nk = t // tk

    def body(a_ref, b_ref, o_ref, acc_ref):
        k = pl.program_id(2)

        @pl.when(k == 0)
        def _():
            acc_ref[...] = jnp.zeros_like(acc_ref)

        bv = b_ref[...]
        if bv.dtype != BF16:
            bv = (scale * bv).astype(BF16)
        acc_ref[...] += _dot_tn(a_ref[...], bv)

        @pl.when(k == nk - 1)
        def _():
            o_ref[...] = acc_ref[...].astype(BF16)

    return pl.pallas_call(
        body, name=name, grid=(m // bm, n // bn, nk),
        in_specs=[pl.BlockSpec((tk, bm), lambda i, j, k: (k, i)), pl.BlockSpec((tk, bn), lambda i, j, k: (k, j))],
        out_specs=pl.BlockSpec((bm, bn), lambda i, j, k: (i, j)),
        out_shape=jax.ShapeDtypeStruct((m, n), BF16),
        scratch_shapes=[pltpu.VMEM((bm, bn), F32)],
        compiler_params=pltpu.CompilerParams(dimension_semantics=("arbitrary", "arbitrary", "arbitrary"), vmem_limit_bytes=VMEM_LIMIT_BYTES),
    )(a, b)


def _row_tile(rows, cols, budget_bytes=2 * 1024 * 1024):
    best = None
    for cand in range(16, rows + 1, 16):
        if rows % cand == 0 and cand * cols * 4 <= budget_bytes:
            best = cand
    return best or rows


def _scalar_grid(grid, in_specs, out_specs):
    return pltpu.PrefetchScalarGridSpec(num_scalar_prefetch=1, grid=grid, in_specs=in_specs, out_specs=out_specs)


def _cparams_nd(n):
    return pltpu.CompilerParams(dimension_semantics=("arbitrary",) * n, vmem_limit_bytes=VMEM_LIMIT_BYTES)


def _cast_into_slot(w, me, name):
    shape = w.shape
    w2 = w.reshape(-1, shape[-1])
    rows, cols = w2.shape
    tr = _row_tile(rows, cols)

    def body(me_ref, w_ref, o_ref):
        o_ref[...] = w_ref[...].astype(BF16)

    out = pl.pallas_call(
        body, name=name,
        grid_spec=_scalar_grid((rows // tr,), [pl.BlockSpec((tr, cols), lambda i, me: (i, 0))],
                               pl.BlockSpec((None, tr, cols), lambda i, me: (me[0], i, 0))),
        out_shape=jax.ShapeDtypeStruct((N_CHIPS, rows, cols), BF16), compiler_params=_cparams())(me, w2)
    return out.reshape((N_CHIPS,) + shape)


def _add_half(view, other, core, name):
    q, _, r, c = view.shape
    tr = _row_tile(r, c)

    def body(core_ref, a_ref, b_ref, o_ref):
        o_ref[...] = (a_ref[...].astype(F32) + b_ref[...].astype(F32)).astype(BF16)

    return pl.pallas_call(
        body, name=name,
        grid_spec=_scalar_grid((q, r // tr), [pl.BlockSpec((None, None, tr, c), lambda k, i, core: (k, core[0], i, 0)),
                                             pl.BlockSpec((None, tr, c), lambda k, i, core: (k, i, 0))],
                               pl.BlockSpec((None, tr, c), lambda k, i, core: (k, i, 0))),
        out_shape=jax.ShapeDtypeStruct((q, r, c), BF16), compiler_params=_cparams_nd(2))(core, view, other)


def _reduce_piece(prev, partial, staged, me, layer, n_layers, column_sharded, name):
    _, r, c = staged.shape
    tr = _row_tile(r, c, budget_bytes=1024 * 1024)
    nt = r // tr
    if column_sharded:
        own2d = partial.reshape(r, N_CHIPS * c)
        own_spec = pl.BlockSpec((tr, c), lambda i, me: (i, me[0]))
    else:
        own2d = partial.reshape(N_CHIPS * r, c)
        own_spec = pl.BlockSpec((tr, c), lambda i, me: (me[0] * nt + i, 0))
    ring = [pl.BlockSpec((None, tr, c), lambda i, me, k=k: ((me[0] + k) % N_CHIPS, i, 0)) for k in (1, 2, 3)]

    def body(me_ref, own_ref, s1_ref, s2_ref, s3_ref, *rest):
        o_ref = rest[-1]
        o_ref[...] = ((own_ref[...].astype(F32) + s1_ref[...].astype(F32)) + s2_ref[...].astype(F32)) + s3_ref[...].astype(F32)

    args = [me, own2d, staged, staged, staged]
    in_specs = [own_spec] + ring
    aliases = {}
    if prev is not None:
        args.append(prev)
        in_specs.append(pl.BlockSpec(memory_space=pl.ANY))
        aliases = {len(args) - 1: 0}
    return pl.pallas_call(
        body, name=name,
        grid_spec=_scalar_grid((nt,), in_specs, pl.BlockSpec((None, tr, c), lambda i, me: (layer, i, 0))),
        out_shape=jax.ShapeDtypeStruct((n_layers, r, c), F32), input_output_aliases=aliases, compiler_params=_cparams())(*args)


def _sum_leading(s, name):
    n = s.shape[0]
    shape = s.shape[1:]
    s3 = s.reshape(n, -1, shape[-1])
    rows, cols = s3.shape[1:]
    tr = _row_tile(rows, cols, budget_bytes=1024 * 1024)

    def body(s_ref, o_ref):
        acc = s_ref[0].astype(F32)
        for k in range(1, n):
            acc = acc + s_ref[k].astype(F32)
        o_ref[...] = acc

    out = pl.pallas_call(
        body, name=name, grid=(rows // tr,), in_specs=[pl.BlockSpec((n, tr, cols), lambda i: (0, i, 0))], out_specs=_row_spec(tr, cols),
        out_shape=jax.ShapeDtypeStruct((rows, cols), F32), compiler_params=_cparams())(s3)
    return out.reshape(shape)


def _adamw(w, g, m, v, name):
    shape = w.shape
    cols = shape[-1] if w.ndim > 1 else 128
    w2, g2, m2, v2 = (a.reshape(-1, cols) for a in (w, g, m, v))
    rows = w2.shape[0]
    tr = _row_tile(rows, cols, budget_bytes=1024 * 1024)
    c1 = 1.0 / (1.0 - ADAM_B1 ** ADAM_STEP)
    c2 = 1.0 / (1.0 - ADAM_B2 ** ADAM_STEP)

    def body(w_ref, g_ref, m_ref, v_ref, d_ref, mo_ref, vo_ref):
        gv = g_ref[...]
        mn = ADAM_B1 * m_ref[...] + (1.0 - ADAM_B1) * gv
        vn = ADAM_B2 * v_ref[...] + (1.0 - ADAM_B2) * (gv * gv)
        mo_ref[...] = mn
        vo_ref[...] = vn
        d_ref[...] = -ADAM_LR * ((mn * c1) / (jnp.sqrt(vn * c2) + ADAM_EPS) + ADAM_WD * w_ref[...])

    spec = _row_spec(tr, cols)
    outs = pl.pallas_call(
        body, name=name, grid=(rows // tr,), in_specs=[spec] * 4, out_specs=[spec] * 3,
        out_shape=[jax.ShapeDtypeStruct((rows, cols), F32)] * 3, compiler_params=_cparams())(w2, g2, m2, v2)
    return tuple(o.reshape(shape) for o in outs)


def _adamw_sharded(w, g_mine, g_sibling, m, v, core, name):
    n_layers, r, c = w.shape
    half = r // 2
    tr = _row_tile(half, c, budget_bytes=1024 * 1024)
    nt = half // tr
    c1 = 1.0 / (1.0 - ADAM_B1 ** ADAM_STEP)
    c2 = 1.0 / (1.0 - ADAM_B2 ** ADAM_STEP)

    def body(core_ref, w_ref, gm_ref, gs_ref, m_ref, v_ref, g_ref, d_ref, mo_ref, vo_ref):
        gv = jnp.where(pl.program_id(1) == core_ref[0], gm_ref[...], gs_ref[...])
        g_ref[...] = gv
        mn = ADAM_B1 * m_ref[...] + (1.0 - ADAM_B1) * gv
        vn = ADAM_B2 * v_ref[...] + (1.0 - ADAM_B2) * (gv * gv)
        mo_ref[...] = mn
        vo_ref[...] = vn
        d_ref[...] = -ADAM_LR * ((mn * c1) / (jnp.sqrt(vn * c2) + ADAM_EPS) + ADAM_WD * w_ref[...])

    full = pl.BlockSpec((None, tr, c), lambda l, h, i, core: (l, h * nt + i, 0))
    part = pl.BlockSpec((None, tr, c), lambda l, h, i, core: (l, i, 0))
    return pl.pallas_call(
        body, name=name, grid_spec=_scalar_grid((n_layers, 2, nt), [full, part, part, full, full], [full] * 4),
        out_shape=[jax.ShapeDtypeStruct(w.shape, F32)] * 4, compiler_params=_cparams_nd(3))(core, w, g_mine, g_sibling, m, v)


def _my_place():
    return lax.axis_index("x"), lax.axis_index("y"), lax.axis_index("c")


def _other_chips(x, y):
    return [(1 - x, y), (x, 1 - y), (1 - x, 1 - y)]


def _all_gather_weights(slots):
    n = len(slots)

    def body(*refs):
        ins, outs = refs[:n], refs[n:2 * n]
        send_sems, recv_sems = refs[2 * n:]
        x, y, c = _my_place()
        sibling = (x, y, 1 - c)
        chips = _other_chips(x, y)
        me = 2 * x + y

        def half(ref, chip_index, core):
            rows = ref.shape[2] // 2
            return ref.at[chip_index, :, pl.ds(core * rows, rows), :]

        def copy(a, k, src, dst, to):
            return pltpu.make_async_remote_copy(src_ref=src, dst_ref=dst, send_sem=send_sems.at[6 * a + k], recv_sem=recv_sems.at[6 * a + k],
                                                device_id=to, device_id_type=MESH)

        first = [copy(a, j, half(ins[a], me, c), half(outs[a], me, c), (*chip, c)) for a in range(n) for j, chip in enumerate(chips)]
        for cp in first:
            cp.start()
        passed = []
        for a in range(n):
            for j, (px, py) in enumerate(chips):
                landed = half(outs[a], 2 * px + py, c)
                copy(a, j, landed, landed, (px, py, c)).wait_recv()
                fwd = copy(a, 3 + j, landed, landed, sibling)
                fwd.start()
                passed.append(fwd)
        for a in range(n):
            for j, (px, py) in enumerate(chips):
                other = half(outs[a], 2 * px + py, 1 - c)
                copy(a, 3 + j, other, other, sibling).wait_recv()
        for cp in first + passed:
            cp.wait_send()

    return pl.pallas_call(
        body, name="all_gather_weights",
        in_specs=[HBM_SPEC] * n, out_specs=[HBM_SPEC] * n,
        out_shape=[jax.ShapeDtypeStruct(s.shape, s.dtype) for s in slots],
        input_output_aliases={a: a for a in range(n)},
        scratch_shapes=[pltpu.SemaphoreType.DMA((6 * n,)), pltpu.SemaphoreType.DMA((6 * n,))],
    )(*slots)


def _exchange_call(name, inputs, out_shapes, plan):
    n_in, n_out = len(inputs), len(out_shapes)

    def body(*refs):
        ins, outs = refs[:n_in], refs[n_in:n_in + n_out]
        send_sems, recv_sems = refs[n_in + n_out:]
        remote = plan(ins, outs, _my_place())
        sends = [pltpu.make_async_remote_copy(src_ref=src, dst_ref=dst, send_sem=send_sems.at[k], recv_sem=recv_sems.at[k],
                                              device_id=dev, device_id_type=MESH) for k, (src, dst, dev, _) in enumerate(remote)]
        for cp in sends:
            cp.start()
        for k, (src, _, dev, incoming) in enumerate(remote):
            pltpu.make_async_remote_copy(src_ref=src, dst_ref=incoming, send_sem=send_sems.at[k], recv_sem=recv_sems.at[k],
                                         device_id=dev, device_id_type=MESH).wait_recv()
        for cp in sends:
            cp.wait_send()

    return pl.pallas_call(
        body, name=name, in_specs=[HBM_SPEC] * n_in, out_specs=[HBM_SPEC] * n_out, out_shape=out_shapes,
        scratch_shapes=[pltpu.SemaphoreType.DMA((plan.count,)), pltpu.SemaphoreType.DMA((plan.count,))],
    )(*inputs)


def _swap_halves(grads):
    def plan(ins, outs, place):
        x, y, c = place
        sibling = (x, y, 1 - c)
        return [(ins[a].at[:, 1 - c], outs[a], sibling, outs[a]) for a in range(len(grads))]

    plan.count = len(grads)
    shapes = [jax.ShapeDtypeStruct((g.shape[0],) + g.shape[2:], g.dtype) for g in grads]
    return _exchange_call("swap_grad_halves", grads, shapes, plan)


def _scatter_partials(partials, pieces, piece_shapes):
    n = len(partials)

    def plan(ins, outs, place):
        x, y, c = place
        me = 2 * x + y
        remote = []
        for a in range(n):
            for (px, py) in _other_chips(x, y):
                q = 2 * px + py
                remote.append((pieces[a](ins[a], q), outs[a].at[me], (px, py, c), outs[a].at[q]))
        return remote

    plan.count = 3 * n
    shapes = [jax.ShapeDtypeStruct((N_CHIPS,) + tuple(ps), p.dtype) for ps, p in zip(piece_shapes, partials)]
    return _exchange_call("scatter_chip_partials", partials, shapes, plan)


def _swap_reduced(halves):
    def plan(ins, outs, place):
        x, y, c = place
        return [(ins[a], outs[a], (x, y, 1 - c), outs[a]) for a in range(len(halves))]

    plan.count = len(halves)
    return _exchange_call("swap_reduced_halves", halves, [jax.ShapeDtypeStruct(h.shape, h.dtype) for h in halves], plan)


def _all_gather_small(buf):
    rows, cols = buf.shape

    def body(x_hbm, out_ref, x_ref, send_sems, recv_sems, local_sem):
        x, y, c = _my_place()
        me, sibling = (x, y, c), (x, y, 1 - c)
        chips = _other_chips(x, y)

        def block(px, py, pc):
            return out_ref.at[4 * px + 2 * py + pc]

        def copy(k, blk, to, src=None):
            return pltpu.make_async_remote_copy(src_ref=block(*blk) if src is None else src, dst_ref=block(*blk), send_sem=send_sems.at[k],
                                                recv_sem=recv_sems.at[k], device_id=to, device_id_type=MESH)

        stage = pltpu.make_async_copy(x_hbm, x_ref, local_sem)
        stage.start()
        stage.wait()
        mine = pltpu.make_async_copy(x_ref, block(*me), local_sem)
        mine.start()
        first = [copy(0, me, sibling, src=x_ref)] + [copy(1 + j, me, (*chip, c), src=x_ref) for j, chip in enumerate(chips)]
        for cp in first:
            cp.start()
        passed = [copy(4 + j, (*chip, c), sibling) for j, chip in enumerate(chips)]
        for j, chip in enumerate(chips):
            copy(1 + j, (*chip, c), me).wait_recv()
            passed[j].start()
        copy(0, sibling, me).wait_recv()
        for j, chip in enumerate(chips):
            copy(4 + j, (*chip, 1 - c), me).wait_recv()
        for cp in first + passed:
            cp.wait_send()
        mine.wait()

    return pl.pallas_call(
        body, name="all_gather_small_grads", in_specs=[HBM_SPEC], out_specs=HBM_SPEC,
        out_shape=jax.ShapeDtypeStruct((2 * N_CHIPS, rows, cols), buf.dtype),
        scratch_shapes=[pltpu.VMEM((rows, cols), buf.dtype), pltpu.SemaphoreType.DMA((7,)), pltpu.SemaphoreType.DMA((7,)),
                        pltpu.SemaphoreType.DMA],
    )(buf)


BIG_IN = ("ffn1_w_in", "ffn2_w_in", "ab_w_in", "sgu_w_in")
BIG_OUT = ("ffn1_w_out", "ffn2_w_out", "ab_w_out", "sgu_w_out")
DEPTH = 2


def _local_step(x, tgt, p, gw):
    big, small = {}, {}
    row = lambda v: v.reshape(1, -1)
    saved = []
    cur = x
    for i in range(DEPTH):
        st = {"xa": cur}
        cur, st["h1"], st["xn1"] = _ffn_fwd(cur, row(p["ffn1_norm"][i]), gw["ffn1_w_in"], gw["ffn1_w_out"], i, f"ffn1_fwd_{i}")
        st["xb"] = cur
        if i % 2 == 0:
            st["h0"], st["xnm"] = _norm_matmul(cur, row(p["mix_norm"][i]), gw["ab_w_in"], 0, "mix0_proj_in")
            cur, st["ycat"] = _mix0_fwd(cur, st["h0"], p["pool_w"][0], row(p["pool_b"][0]), row(p["pool_scale"][0]), p["conv_w"][0],
                                        row(p["conv_b"][0]), row(p["conv_ln_g"][0]), row(p["conv_ln_b"][0]), gw["ab_w_out"], "mix0_fwd")
        else:
            st["pre"], st["xnm"] = _norm_matmul(cur, row(p["mix_norm"][i]), gw["sgu_w_in"], 0, "sgu_proj_in")
            cur, st["p"] = _sgu_fwd(cur, st["pre"], row(p["sgu_ln_g"][0]), row(p["sgu_ln_b"][0]), p["sgu_w"][0], p["sgu_b"][0].T,
                                    gw["sgu_w_out"], "sgu_fwd")
        st["xc"] = cur
        cur, st["h2"], st["xn2"] = _ffn_fwd(cur, row(p["ffn2_norm"][i]), gw["ffn2_w_in"], gw["ffn2_w_out"], i, f"ffn2_fwd_{i}")
        saved.append(st)
    dy, loss, small["final_norm"] = _final_loss(cur, tgt, row(p["final_norm"]), "final_loss")
    norm_grads = {"ffn1_norm": [None] * DEPTH, "mix_norm": [None] * DEPTH, "ffn2_norm": [None] * DEPTH}
    for i in reversed(range(DEPTH)):
        st = saved[i]
        dy_in = dy
        dy, dh, act, norm_grads["ffn2_norm"][i] = _ffn_bwd(st["xc"], dy_in, st["h2"], row(p["ffn2_norm"][i]), gw["ffn2_w_in"],
                                                            gw["ffn2_w_out"], i, f"ffn2_bwd_{i}")
        big[("ffn2_w_in", i)] = _tn_matmul(st["xn2"], dh, 1.0, 1024, 1408, f"ffn2_dwin_{i}")
        big[("ffn2_w_out", i)] = _tn_matmul(act, dy_in, 0.5, 1408, 1024, f"ffn2_dwout_{i}")
        dy_in = dy
        if i % 2 == 0:
            cw = p["conv_w"][0]
            dconv, dpc, dpw, rows = _mix0_bwd_a(dy_in, st["h0"], p["pool_w"][0], row(p["pool_b"][0]), row(p["pool_scale"][0]), cw,
                                                row(p["conv_b"][0]), row(p["conv_ln_g"][0]), row(p["conv_ln_b"][0]), gw["ab_w_out"], "mix0_bwd_a")
            dy, dh0, norm_grads["mix_norm"][i] = _mix0_bwd_b(st["xb"], dy_in, st["h0"], dconv, dpc, row(p["mix_norm"][i]), cw,
                                                              gw["ab_w_in"], "mix0_bwd_b")
            big[("ab_w_in", 0)] = _tn_matmul(st["xnm"], dh0, 1.0, 1024, 1536, "ab_dwin")
            big[("ab_w_out", 0)] = _tn_matmul(st["ycat"], dy_in, 1.0, 1024, 1024, "ab_dwout")
            small["pool_w"] = dpw[None]
            small["conv_w"] = rows[None, 0:CONV_WIDTH]
            small["conv_b"], small["conv_ln_g"], small["conv_ln_b"] = rows[32:33], rows[33:34], rows[34:35]
            small["pool_scale"] = rows[35:36]
            small["pool_b"] = rows[36:37].reshape(1, len(POOL_WINDOWS), POOL_GC)
        else:
            dy, dpre, norm_grads["mix_norm"][i], dlg, dlb, dw, dbt = _sgu_bwd(
                st["xb"], dy_in, st["pre"], row(p["mix_norm"][i]), row(p["sgu_ln_g"][0]), row(p["sgu_ln_b"][0]), p["sgu_w"][0],
                p["sgu_b"][0].T, gw["sgu_w_in"], gw["sgu_w_out"], "sgu_bwd")
            big[("sgu_w_in", 0)] = _tn_matmul(st["xnm"], dpre, 1.0, 1024, 2048, "sgu_dwin")
            big[("sgu_w_out", 0)] = _tn_matmul(st["p"], dy_in, 1.0, 1024, 1024, "sgu_dwout")
            small["sgu_ln_g"], small["sgu_ln_b"] = dlg, dlb
            small["sgu_w"] = dw[None]
            small["sgu_b"] = dbt.T[None]
        dy_in = dy
        dy, dh, act, norm_grads["ffn1_norm"][i] = _ffn_bwd(st["xa"], dy_in, st["h1"], row(p["ffn1_norm"][i]), gw["ffn1_w_in"],
                                                            gw["ffn1_w_out"], i, f"ffn1_bwd_{i}")
        big[("ffn1_w_in", i)] = _tn_matmul(st["xn1"], dh, 1.0, 1024, 1408, f"ffn1_dwin_{i}")
        big[("ffn1_w_out", i)] = _tn_matmul(act, dy_in, 0.5, 1408, 1024, f"ffn1_dwout_{i}")
    for k, v in norm_grads.items():
        small[k] = jnp.concatenate(v, axis=0)
    small["final_norm"] = small["final_norm"].reshape(-1)
    return loss, dy, big, small


WEIGHT_NAMES = ("ffn1_norm", "ffn1_w_in", "ffn1_w_out", "mix_norm", "ffn2_norm", "ffn2_w_in", "ffn2_w_out", "ab_w_in", "pool_w", "pool_b",
                "pool_scale", "conv_w", "conv_b", "conv_ln_g", "conv_ln_b", "ab_w_out", "sgu_w_in", "sgu_ln_g", "sgu_ln_b", "sgu_w", "sgu_b",
                "sgu_w_out", "final_norm")
BIG = BIG_IN + BIG_OUT
SHARDED_SMALL = ("conv_w", "sgu_ln_g", "sgu_ln_b")
PACK_ROWS = 48


def _gather_parameters(w, me):
    pack = jnp.concatenate([
        w["conv_w"][0], jnp.zeros((1, 128), F32), w["sgu_ln_g"].reshape(2, 128), w["sgu_ln_b"].reshape(2, 128),
        jnp.zeros((PACK_ROWS - 36, 128), F32)], axis=0)
    pack_slots = lax.dynamic_update_slice(jnp.zeros((N_CHIPS, 1, PACK_ROWS, 128), F32), pack[None, None], (me[0], 0, 0, 0))
    slots = [_cast_into_slot(w[n], me, f"cast_{n}") for n in BIG] + [pack_slots]
    gathered = _all_gather_weights(slots)
    gw = dict(zip(BIG, gathered[:-1]))
    gp = gathered[-1][:, 0]
    conv_w = jnp.transpose(gp[:, 0:CONV_WIDTH], (1, 0, 2)).reshape(1, CONV_WIDTH, N_CHIPS * 128)
    ln_g = gp[:, 32:34].reshape(1, -1)
    ln_b = gp[:, 34:36].reshape(1, -1)
    return gw, conv_w, ln_g, ln_b


def _reduce_big(big, me, core):
    keys = sorted(big)
    views, pieces, piece_shapes = [], [], []
    for name, _ in keys:
        g = big[(name, _)]
        m, n = g.shape
        if name in BIG_IN:
            views.append(g.reshape(1, 2, m // 2, n))
            cq = n // N_CHIPS
            pieces.append(lambda ref, q, cq=cq: ref.at[0, :, pl.ds(q * cq, cq)])
            piece_shapes.append((m // 2, cq))
        else:
            views.append(g.reshape(N_CHIPS, 2, m // (2 * N_CHIPS), n))
            pieces.append(lambda ref, q: ref.at[q])
            piece_shapes.append((m // (2 * N_CHIPS), n))
    theirs = _swap_halves(views)
    partial = [_add_half(v, o, core, f"chip_partial_{k[0]}_{k[1]}") for v, o, k in zip(views, theirs, keys)]
    staged = _scatter_partials(partial, pieces, piece_shapes)
    mine = {}
    for (name, layer), part, stage in zip(keys, partial, staged):
        n_layers = sum(1 for k in keys if k[0] == name)
        mine[name] = _reduce_piece(mine.get(name), part, stage, me, layer, n_layers, name in BIG_IN, f"reduce_{name}_{layer}")
    names = list(mine)
    return mine, dict(zip(names, _swap_reduced([mine[n] for n in names])))


SMALL = tuple(n for n in WEIGHT_NAMES if n not in BIG)


def _reduce_small(small):
    flat = [small[n].reshape(-1, 128) for n in SMALL]
    sizes = [f.shape[0] for f in flat]
    buf = jnp.concatenate(flat, axis=0)
    pad = (-buf.shape[0]) % 8
    if pad:
        buf = jnp.concatenate([buf, jnp.zeros((pad, 128), F32)], axis=0)
    total = _sum_leading(_all_gather_small(buf), "reduce_small")
    out, at = {}, 0
    for n, s in zip(SMALL, sizes):
        out[n] = total[at:at + s].reshape(small[n].shape)
        at += s
    return out


def kernel(x, ffn1_norm, ffn1_w_in, ffn1_w_out, mix_norm, ffn2_norm, ffn2_w_in, ffn2_w_out, ab_w_in, pool_w, pool_b, pool_scale, conv_w, conv_b, conv_ln_g, conv_ln_b, ab_w_out, sgu_w_in, sgu_ln_g, sgu_ln_b, sgu_w, sgu_b, sgu_w_out, final_norm, loss_target, m_ffn1_norm, m_ffn1_w_in, m_ffn1_w_out, m_mix_norm, m_ffn2_norm, m_ffn2_w_in, m_ffn2_w_out, m_ab_w_in, m_pool_w, m_pool_b, m_pool_scale, m_conv_w, m_conv_b, m_conv_ln_g, m_conv_ln_b, m_ab_w_out, m_sgu_w_in, m_sgu_ln_g, m_sgu_ln_b, m_sgu_w, m_sgu_b, m_sgu_w_out, m_final_norm, v_ffn1_norm, v_ffn1_w_in, v_ffn1_w_out, v_mix_norm, v_ffn2_norm, v_ffn2_w_in, v_ffn2_w_out, v_ab_w_in, v_pool_w, v_pool_b, v_pool_scale, v_conv_w, v_conv_b, v_conv_ln_g, v_conv_ln_b, v_ab_w_out, v_sgu_w_in, v_sgu_ln_g, v_sgu_ln_b, v_sgu_w, v_sgu_b, v_sgu_w_out, v_final_norm):
    given = dict(locals())
    w = {n: given[n] for n in WEIGHT_NAMES}
    chip = 2 * lax.axis_index("x") + lax.axis_index("y")
    me = chip.astype(jnp.int32).reshape(1)
    core = lax.axis_index("c").astype(jnp.int32).reshape(1)
    gw, conv_w_full, ln_g_full, ln_b_full = _gather_parameters(w, me)
    p = {n: w[n] for n in SMALL}
    p.update(conv_w=conv_w_full, sgu_ln_g=ln_g_full, sgu_ln_b=ln_b_full)
    loss, grad_x, big, small = _local_step(x[0], loss_target[0], p, gw)
    mine, theirs = _reduce_big(big, me, core)
    small_sum = _reduce_small(small)
    loss = lax.psum(loss[0, 0], MESH_AXES)
    grads, delta, new_m, new_v = {}, {}, {}, {}
    for n in WEIGHT_NAMES:
        if n in BIG:
            grads[n], delta[n], new_m[n], new_v[n] = _adamw_sharded(w[n], mine[n], theirs[n], given["m_" + n], given["v_" + n], core,
                                                                    f"adamw_{n}")
            continue
        g = small_sum[n]
        if n in SHARDED_SMALL:
            width = w[n].shape[-1]
            g = lax.dynamic_slice_in_dim(g, chip * width, width, axis=g.ndim - 1)
        grads[n] = g
        delta[n], new_m[n], new_v[n] = _adamw(w[n], g, given["m_" + n], given["v_" + n], f"adamw_{n}")
    return (loss, grad_x[None], *[grads[n] for n in WEIGHT_NAMES], *[delta[n] for n in WEIGHT_NAMES],
            *[new_m[n] for n in WEIGHT_NAMES], *[new_v[n] for n in WEIGHT_NAMES])
```

```python
import jax
import jax.numpy as jnp
from jax import lax
from jax.experimental import pallas as pl
from jax.experimental.pallas import tpu as pltpu

F32, BF16 = jnp.float32, jnp.bfloat16
EPS = 1e-6
N_CHIPS = 4
POOL_WINDOWS = (2, 4, 8, 16)
POOL_GC = 128
POOL_CH = 512
CONV_CH = 512
CONV_WIDTH = 31
HALO = 32
SGU_HEADS = 8
CHUNK = 128
DEPTH = 2
ADAM_LR, ADAM_B1, ADAM_B2, ADAM_EPS, ADAM_WD, ADAM_STEP = 0.001, 0.9, 0.999, 1e-08, 0.01, 10
VMEM_LIMIT_BYTES = 60 * 1024 * 1024
MESH_AXES = ("x", "y", "c")
MESH = pl.DeviceIdType.MESH
HBM_SPEC = pl.BlockSpec(memory_space=pltpu.HBM)


def _sds(a):
    return jax.ShapeDtypeStruct(a.shape, a.dtype)


def _cparams_nd(n):
    return pltpu.CompilerParams(dimension_semantics=("arbitrary",) * n, vmem_limit_bytes=VMEM_LIMIT_BYTES)


def _cparams():
    return _cparams_nd(1)


def _dot(a, b):
    return jnp.dot(a, b, preferred_element_type=F32)


def _dot_nt(a, b):
    return lax.dot_general(a, b, (((1,), (1,)), ((), ())), preferred_element_type=F32)


def _dot_tn(a, b):
    return lax.dot_general(a, b, (((0,), (0,)), ((), ())), preferred_element_type=F32)


def _rms_fwd(x):
    r = lax.rsqrt(jnp.mean(x * x, axis=-1, keepdims=True) + EPS)
    return x * r, r


def _rms_bwd(dxn, xh, r, g):
    dxh = dxn * g
    return r * (dxh - xh * jnp.mean(dxh * xh, axis=-1, keepdims=True))


def _ln_fwd(y):
    mu = jnp.mean(y, axis=-1, keepdims=True)
    yc = y - mu
    rs = lax.rsqrt(jnp.mean(yc * yc, axis=-1, keepdims=True) + EPS)
    return yc * rs, rs


def _ln_bwd(dyhat, yhat, rs):
    return rs * (dyhat - jnp.mean(dyhat, axis=-1, keepdims=True) - yhat * jnp.mean(dyhat * yhat, axis=-1, keepdims=True))


def _sigmoid(x):
    return 1.0 / (1.0 + jnp.exp(-x))


def _const_spec(shape):
    n = len(shape)
    return pl.BlockSpec(shape, lambda i: (0,) * n)


def _row_spec(tm, cols):
    return pl.BlockSpec((tm, cols), lambda i: (i, 0))


def _my_place():
    return lax.axis_index("x"), lax.axis_index("y"), lax.axis_index("c")


def _other_chips(x, y):
    return [(1 - x, y), (x, 1 - y), (1 - x, 1 - y)]


class _Exchange:
    def __init__(self, inputs, out_shapes, plan, count, aliases=None, finish=None, local=None, n_local=0):
        self.inputs, self.out_shapes, self.plan, self.count = list(inputs), list(out_shapes), plan, count
        self.aliases, self.finish, self.local, self.n_local = dict(aliases or {}), finish, local, n_local


def _call(body, *, name, grid, in_specs, out_specs, out_shape, args, scratch_shapes=(), comms=()):
    comms = [cm if isinstance(cm, _Exchange) else cm() for cm in comms]
    in_specs, out_specs, out_shape, scratch_shapes = list(in_specs), list(out_specs), list(out_shape), list(scratch_shapes)
    n_in, n_out, n_scr = len(in_specs), len(out_specs), len(scratch_shapes)
    c_in = [a for cm in comms for a in cm.inputs]
    c_out = [s for cm in comms for s in cm.out_shapes]
    n_remote = sum(cm.count for cm in comms)
    n_local = sum(cm.n_local for cm in comms)
    aliases, at_in, at_out = {}, n_in, n_out
    for cm in comms:
        for i, o in cm.aliases.items():
            aliases[at_in + i] = at_out + o
        at_in += len(cm.inputs)
        at_out += len(cm.out_shapes)

    def wrapped(*refs):
        ins, ci = refs[:n_in], refs[n_in:n_in + len(c_in)]
        at = n_in + len(c_in)
        outs, co = refs[at:at + n_out], refs[at + n_out:at + n_out + len(c_out)]
        at += n_out + len(c_out)
        scr = refs[at:at + n_scr]
        if not comms:
            body(*ins, *outs, *scr)
            return
        send_sems, recv_sems, local_sems = refs[at + n_scr:]
        place = _my_place()
        sends, arrivals, locals_ = [], [], []
        i0 = o0 = 0
        for cm in comms:
            cm_in, cm_out = ci[i0:i0 + len(cm.inputs)], co[o0:o0 + len(cm.out_shapes)]
            i0 += len(cm.inputs)
            o0 += len(cm.out_shapes)
            for src, dst, dev, incoming in cm.plan(cm_in, cm_out, place):
                k = len(sends)
                sends.append(pltpu.make_async_remote_copy(src_ref=src, dst_ref=dst, send_sem=send_sems.at[k], recv_sem=recv_sems.at[k],
                                                          device_id=dev, device_id_type=MESH))
                arrivals.append(pltpu.make_async_remote_copy(src_ref=src, dst_ref=incoming, send_sem=send_sems.at[k],
                                                             recv_sem=recv_sems.at[k], device_id=dev, device_id_type=MESH))
            if cm.local is not None:
                for src, dst in cm.local(cm_in, cm_out, place):
                    locals_.append(pltpu.make_async_copy(src, dst, local_sems.at[len(locals_)]))

        def start():
            for cp in locals_ + sends:
                cp.start()

        def finish():
            for cp in arrivals:
                cp.wait_recv()
            for cp in sends:
                cp.wait_send()
            for cp in locals_:
                cp.wait()

        if not grid:
            start()
            body(*ins, *outs, *scr)
            finish()
            return
        ids = [pl.program_id(a) for a in range(len(grid))]
        first, last = ids[0] == 0, ids[0] == grid[0] - 1
        for a in range(1, len(grid)):
            first = jnp.logical_and(first, ids[a] == 0)
            last = jnp.logical_and(last, ids[a] == grid[a] - 1)
        pl.when(first)(start)
        body(*ins, *outs, *scr)
        pl.when(last)(finish)

    sems = []
    if comms:
        sems = [pltpu.SemaphoreType.DMA((max(n_remote, 1),)), pltpu.SemaphoreType.DMA((max(n_remote, 1),)),
                pltpu.SemaphoreType.DMA((max(n_local, 1),))]
    kwargs = dict(grid=grid, compiler_params=_cparams_nd(len(grid))) if grid else {}
    res = pl.pallas_call(
        wrapped, name=name, in_specs=in_specs + [HBM_SPEC] * len(c_in), out_specs=out_specs + [HBM_SPEC] * len(c_out),
        out_shape=out_shape + c_out, scratch_shapes=scratch_shapes + sems, input_output_aliases=aliases, **kwargs)(*args, *c_in)
    at = n_out
    for cm in comms:
        got = res[at:at + len(cm.out_shapes)]
        at += len(cm.out_shapes)
        if cm.finish is not None:
            cm.finish(got)
    return list(res[:n_out])


def _exchange_alone(name, comms):
    _call(lambda: None, name=name, grid=(), in_specs=[], out_specs=[], out_shape=[], args=[], comms=comms)


def _in_weight_copies(w_hbm, w_v, sem, base=0):
    return [pltpu.make_async_copy(w_hbm.at[q], w_v.at[q], sem.at[base + q]) for q in range(N_CHIPS)]


def _out_weight_copies(w_hbm, w_v, sem, base=0):
    rows = w_hbm.shape[1]
    return [pltpu.make_async_copy(w_hbm.at[q], w_v.at[pl.ds(q * rows, rows)], sem.at[base + q]) for q in range(N_CHIPS)]


def _load_at_first_step(copies):
    @pl.when(pl.program_id(0) == 0)
    def _():
        for cp in copies:
            cp.start()
        for cp in copies:
            cp.wait()


def _ffn_fwd(x, g, win_g, wout_g, name, comms=()):
    t, d = x.shape
    c = win_g.shape[-1]
    ff = 2 * c
    tm = min(256, t)

    def body(x_ref, g_ref, win_hbm, wout_hbm, xo_ref, h_ref, xn_ref, win_v, wout_v, sem):
        _load_at_first_step(_in_weight_copies(win_hbm, win_v, sem) + _out_weight_copies(wout_hbm, wout_v, sem, N_CHIPS))
        xv = x_ref[...]
        xh, _ = _rms_fwd(xv)
        xn = (xh * g_ref[...]).astype(BF16)
        xn_ref[...] = xn
        acc = jnp.zeros((tm, d), F32)
        for j in range(2):
            gate = _dot(xn, win_v[j])
            up = _dot(xn, win_v[j + 2])
            h_ref[:, j * c:(j + 1) * c] = gate.astype(BF16)
            h_ref[:, ff + j * c:ff + (j + 1) * c] = up.astype(BF16)
            act = (gate * _sigmoid(gate) * up).astype(BF16)
            acc = acc + _dot(act, wout_v[j * c:(j + 1) * c, :])
        xo_ref[...] = xv + 0.5 * acc

    return _call(
        body, name=name, grid=(t // tm,),
        in_specs=[_row_spec(tm, d), _const_spec((1, d)), HBM_SPEC, HBM_SPEC],
        out_specs=[_row_spec(tm, d), _row_spec(tm, 2 * ff), _row_spec(tm, d)],
        out_shape=[jax.ShapeDtypeStruct((t, d), F32), jax.ShapeDtypeStruct((t, 2 * ff), BF16), jax.ShapeDtypeStruct((t, d), BF16)],
        scratch_shapes=[pltpu.VMEM((N_CHIPS, d, c), BF16), pltpu.VMEM((ff, d), BF16), pltpu.SemaphoreType.DMA((2 * N_CHIPS,))],
        args=(x, g, win_g, wout_g), comms=comms)


def _ffn_bwd(x, dy, h, g, win_g, wout_g, name, comms=()):
    t, d = x.shape
    c = win_g.shape[-1]
    ff = 2 * c
    tm = min(256, t)

    def body(x_ref, dy_ref, h_ref, g_ref, win_hbm, wout_hbm, dx_ref, dh_ref, act_ref, dg_ref, win_v, wout_v, sem):
        _load_at_first_step(_in_weight_copies(win_hbm, win_v, sem) + _out_weight_copies(wout_hbm, wout_v, sem, N_CHIPS))

        @pl.when(pl.program_id(0) == 0)
        def _():
            dg_ref[...] = jnp.zeros_like(dg_ref)

        xv, dyv, gv = x_ref[...], dy_ref[...], g_ref[...]
        xh, r = _rms_fwd(xv)
        dyh = (0.5 * dyv).astype(BF16)
        dxn = jnp.zeros((tm, d), F32)
        for j in range(2):
            gate = h_ref[:, j * c:(j + 1) * c].astype(F32)
            up = h_ref[:, ff + j * c:ff + (j + 1) * c].astype(F32)
            dact = _dot_nt(dyh, wout_v[j * c:(j + 1) * c, :])
            s = _sigmoid(gate)
            sl = gate * s
            act_ref[:, j * c:(j + 1) * c] = (sl * up).astype(BF16)
            dgate = (dact * up * (s * (1.0 + gate * (1.0 - s)))).astype(BF16)
            dup = (dact * sl).astype(BF16)
            dh_ref[:, j * c:(j + 1) * c] = dgate
            dh_ref[:, ff + j * c:ff + (j + 1) * c] = dup
            dxn = dxn + _dot_nt(dgate, win_v[j]) + _dot_nt(dup, win_v[j + 2])
        dg_ref[...] += jnp.sum(dxn * xh, axis=0, keepdims=True)
        dx_ref[...] = dyv + _rms_bwd(dxn, xh, r, gv)

    return _call(
        body, name=name, grid=(t // tm,),
        in_specs=[_row_spec(tm, d), _row_spec(tm, d), _row_spec(tm, 2 * ff), _const_spec((1, d)), HBM_SPEC, HBM_SPEC],
        out_specs=[_row_spec(tm, d), _row_spec(tm, 2 * ff), _row_spec(tm, ff), _const_spec((1, d))],
        out_shape=[jax.ShapeDtypeStruct((t, d), F32), jax.ShapeDtypeStruct((t, 2 * ff), BF16), jax.ShapeDtypeStruct((t, ff), BF16),
                   jax.ShapeDtypeStruct((1, d), F32)],
        scratch_shapes=[pltpu.VMEM((N_CHIPS, d, c), BF16), pltpu.VMEM((ff, d), BF16), pltpu.SemaphoreType.DMA((2 * N_CHIPS,))],
        args=(x, dy, h, g, win_g, wout_g), comms=comms)


def _norm_matmul(x, g, win_g, name, comms=()):
    t, d = x.shape
    c = win_g.shape[-1]
    tm = min(512, t)

    def body(x_ref, g_ref, win_hbm, o_ref, xn_ref, win_v, sem):
        _load_at_first_step(_in_weight_copies(win_hbm, win_v, sem))
        xh, _ = _rms_fwd(x_ref[...])
        xn = (xh * g_ref[...]).astype(BF16)
        xn_ref[...] = xn
        for q in range(N_CHIPS):
            o_ref[:, q * c:(q + 1) * c] = _dot(xn, win_v[q])

    return _call(
        body, name=name, grid=(t // tm,),
        in_specs=[_row_spec(tm, d), _const_spec((1, d)), HBM_SPEC],
        out_specs=[_row_spec(tm, N_CHIPS * c), _row_spec(tm, d)],
        out_shape=[jax.ShapeDtypeStruct((t, N_CHIPS * c), F32), jax.ShapeDtypeStruct((t, d), BF16)],
        scratch_shapes=[pltpu.VMEM((N_CHIPS, d, c), BF16), pltpu.SemaphoreType.DMA((N_CHIPS,))],
        args=(x, g, win_g), comms=comms)


def _proj_in_bwd_tail(dh, win_v, c):
    dxn = _dot_nt(dh[:, 0:c], win_v[0])
    for q in range(1, N_CHIPS):
        dxn = dxn + _dot_nt(dh[:, q * c:(q + 1) * c], win_v[q])
    return dxn


def _prev_halo_spec(tm, cols):
    return pl.BlockSpec((HALO, cols), lambda i: (jnp.maximum(i * (tm // HALO) - 1, 0), 0))


def _next_halo_spec(tm, cols, t):
    last = t // HALO - 1
    return pl.BlockSpec((HALO, cols), lambda i: (jnp.minimum((i + 1) * (tm // HALO), last), 0))


def _shift_down(w, k):
    return w if k == 0 else pltpu.roll(w, k, 0)


def _shift_up(w, k):
    return w if k == 0 else pltpu.roll(w, w.shape[0] - k, 0)


def _pool_counts(i, tm):
    pos = (i * tm + lax.broadcasted_iota(jnp.int32, (tm, POOL_CH), 0) + 1).astype(F32)
    lane = lax.broadcasted_iota(jnp.int32, (tm, POOL_CH), 1)
    win = jnp.where(lane < POOL_GC, 2.0, jnp.where(lane < 2 * POOL_GC, 4.0, jnp.where(lane < 3 * POOL_GC, 8.0, 16.0)))
    return jnp.minimum(pos, win)


def _group_select(parts):
    return jnp.concatenate([p[:, k * POOL_GC:(k + 1) * POOL_GC] for k, p in enumerate(parts)], axis=1)


def _mix0_recompute(i, tm, h_cur, h_prev, conv_w, conv_b):
    prev = jnp.where(i > 0, h_prev, 0.0)
    win = jnp.concatenate([prev, h_cur], axis=0)
    u_w = win[:, 0:POOL_CH]
    a_w = win[:, POOL_CH:POOL_CH + CONV_CH]
    gt_w = win[:, POOL_CH + CONV_CH:]
    g_w = a_w * _sigmoid(gt_w)
    y = jnp.zeros((tm, CONV_CH), F32)
    for k in range(CONV_WIDTH):
        y = y + conv_w[k:k + 1, :] * _shift_down(g_w, CONV_WIDTH - 1 - k)[HALO:, :]
    y = y + conv_b
    s2 = u_w + _shift_down(u_w, 1)
    s4 = s2 + _shift_down(s2, 2)
    s8 = s4 + _shift_down(s4, 4)
    s16 = s8 + _shift_down(s8, 8)
    sums = _group_select([s2[HALO:], s4[HALO:], s8[HALO:], s16[HALO:]])
    cnt = _pool_counts(i, tm)
    pooled = sums / cnt - h_cur[:, 0:POOL_CH]
    return g_w, y, pooled, cnt


def _pool_linear(pooled, pw_ref, pb):
    return jnp.concatenate(
        [_dot(pooled[:, k * POOL_GC:(k + 1) * POOL_GC].astype(BF16), pw_ref[k].astype(BF16)) for k in range(len(POOL_WINDOWS))], axis=1) + pb


def _mix0_fwd(x, h0, pool_w, pool_b, pool_scale, conv_w, conv_b, ln_g, ln_b, wout_g, name):
    t, d = x.shape
    tm = min(256, t)
    hc = h0.shape[1]

    def body(x_ref, h_ref, hp_ref, pw_ref, pb_ref, ps_ref, cw_ref, cb_ref, lg_ref, lb_ref, wout_hbm, xo_ref, ycat_ref, wout_v, sem):
        _load_at_first_step(_out_weight_copies(wout_hbm, wout_v, sem))
        i = pl.program_id(0)
        _, y, pooled, _ = _mix0_recompute(i, tm, h_ref[...], hp_ref[...], cw_ref[...], cb_ref[...])
        yhat, _ = _ln_fwd(y)
        yn = yhat * lg_ref[...] + lb_ref[...]
        yb = yn * _sigmoid(yn)
        ya = _pool_linear(pooled, pw_ref, pb_ref[...]) * ps_ref[...]
        ycat = jnp.concatenate([ya, yb], axis=1).astype(BF16)
        ycat_ref[...] = ycat
        xo_ref[...] = x_ref[...] + _dot(ycat, wout_v[...])

    return _call(
        body, name=name, grid=(t // tm,),
        in_specs=[_row_spec(tm, d), _row_spec(tm, hc), _prev_halo_spec(tm, hc), _const_spec(pool_w.shape), _const_spec((1, POOL_CH)),
                  _const_spec((1, POOL_CH)), _const_spec(conv_w.shape), _const_spec((1, CONV_CH)), _const_spec((1, CONV_CH)),
                  _const_spec((1, CONV_CH)), HBM_SPEC],
        out_specs=[_row_spec(tm, d), _row_spec(tm, d)],
        out_shape=[jax.ShapeDtypeStruct((t, d), F32), jax.ShapeDtypeStruct((t, d), BF16)],
        scratch_shapes=[pltpu.VMEM((d, d), BF16), pltpu.SemaphoreType.DMA((N_CHIPS,))],
        args=(x, h0, h0, pool_w, pool_b, pool_scale, conv_w, conv_b, ln_g, ln_b, wout_g))


def _mix0_bwd_a(dy, h0, pool_w, pool_b, pool_scale, conv_w, conv_b, ln_g, ln_b, wout_g, name):
    t, d = dy.shape
    tm = min(256, t)
    hc = h0.shape[1]
    n_small = 40

    def body(dy_ref, h_ref, hp_ref, pw_ref, pb_ref, ps_ref, cw_ref, cb_ref, lg_ref, lb_ref, wout_hbm,
             dconv_ref, dpc_ref, dpw_ref, small_ref, wout_v, sem):
        _load_at_first_step(_out_weight_copies(wout_hbm, wout_v, sem))
        i = pl.program_id(0)

        @pl.when(i == 0)
        def _():
            dpw_ref[...] = jnp.zeros_like(dpw_ref)
            small_ref[...] = jnp.zeros_like(small_ref)

        g_w, y, pooled, cnt = _mix0_recompute(i, tm, h_ref[...], hp_ref[...], cw_ref[...], cb_ref[...])
        yhat, rs = _ln_fwd(y)
        lg = lg_ref[...]
        yn = yhat * lg + lb_ref[...]
        mixed = _pool_linear(pooled, pw_ref, pb_ref[...])
        dycat = _dot_nt(dy_ref[...].astype(BF16), wout_v[...])
        dya, dyb = dycat[:, 0:POOL_CH], dycat[:, POOL_CH:]
        sg = _sigmoid(yn)
        dyn = dyb * (sg * (1.0 + yn * (1.0 - sg)))
        dyc = _ln_bwd(dyn * lg, yhat, rs)
        dconv_ref[...] = dyc

        def add_row(k, value):
            small_ref[k:k + 1, :] += jnp.sum(value, axis=0, keepdims=True)

        for k in range(CONV_WIDTH):
            add_row(k, dyc * _shift_down(g_w, CONV_WIDTH - 1 - k)[HALO:, :])
        add_row(32, dyc)
        add_row(33, dyn * yhat)
        add_row(34, dyn)
        scale = ps_ref[...]
        dmixed = dya * scale
        add_row(35, dya * mixed)
        add_row(36, dmixed)
        dmb = dmixed.astype(BF16)
        dpooled = []
        for k in range(len(POOL_WINDOWS)):
            sl = slice(k * POOL_GC, (k + 1) * POOL_GC)
            dpw_ref[k] += _dot_tn(pooled[:, sl].astype(BF16), dmb[:, sl])
            dpooled.append(_dot_nt(dmb[:, sl], pw_ref[k].astype(BF16)))
        dpc_ref[...] = jnp.concatenate(dpooled, axis=1) / cnt

    return _call(
        body, name=name, grid=(t // tm,),
        in_specs=[_row_spec(tm, d), _row_spec(tm, hc), _prev_halo_spec(tm, hc), _const_spec(pool_w.shape), _const_spec((1, POOL_CH)),
                  _const_spec((1, POOL_CH)), _const_spec(conv_w.shape), _const_spec((1, CONV_CH)), _const_spec((1, CONV_CH)),
                  _const_spec((1, CONV_CH)), HBM_SPEC],
        out_specs=[_row_spec(tm, CONV_CH), _row_spec(tm, POOL_CH), _const_spec(pool_w.shape), _const_spec((n_small, CONV_CH))],
        out_shape=[jax.ShapeDtypeStruct((t, CONV_CH), F32), jax.ShapeDtypeStruct((t, POOL_CH), F32),
                   jax.ShapeDtypeStruct(pool_w.shape, F32), jax.ShapeDtypeStruct((n_small, CONV_CH), F32)],
        scratch_shapes=[pltpu.VMEM((d, d), BF16), pltpu.SemaphoreType.DMA((N_CHIPS,))],
        args=(dy, h0, h0, pool_w, pool_b, pool_scale, conv_w, conv_b, ln_g, ln_b, wout_g))


def _mix0_bwd_b(x, dy, h0, dconv, dpc, g, conv_w, win_g, name):
    t, d = x.shape
    tm = min(256, t)
    hc = h0.shape[1]
    c = win_g.shape[-1]
    n_tiles = t // tm

    def body(x_ref, dy_ref, h_ref, dc_ref, dcn_ref, dp_ref, dpn_ref, g_ref, cw_ref, win_hbm, dx_ref, dh_ref, dg_ref, win_v, sem):
        _load_at_first_step(_in_weight_copies(win_hbm, win_v, sem))
        i = pl.program_id(0)

        @pl.when(i == 0)
        def _():
            dg_ref[...] = jnp.zeros_like(dg_ref)

        not_last = i < n_tiles - 1
        dc_w = jnp.concatenate([dc_ref[...], jnp.where(not_last, dcn_ref[...], 0.0)], axis=0)
        dp_w = jnp.concatenate([dp_ref[...], jnp.where(not_last, dpn_ref[...], 0.0)], axis=0)
        cw = cw_ref[...]
        dg = jnp.zeros((tm, CONV_CH), F32)
        for k in range(CONV_WIDTH):
            dg = dg + cw[k:k + 1, :] * _shift_up(dc_w, CONV_WIDTH - 1 - k)[0:tm, :]
        a2 = dp_w + _shift_up(dp_w, 1)
        a4 = a2 + _shift_up(a2, 2)
        a8 = a4 + _shift_up(a4, 4)
        a16 = a8 + _shift_up(a8, 8)
        back = _group_select([a2[0:tm], a4[0:tm], a8[0:tm], a16[0:tm]])
        du = back - dp_ref[...] * _pool_counts(i, tm)
        hv = h_ref[...]
        a = hv[:, POOL_CH:POOL_CH + CONV_CH]
        sig = _sigmoid(hv[:, POOL_CH + CONV_CH:])
        dh = jnp.concatenate([du, dg * sig, dg * a * sig * (1.0 - sig)], axis=1).astype(BF16)
        dh_ref[...] = dh
        dxn = _proj_in_bwd_tail(dh, win_v, c)
        xh, r = _rms_fwd(x_ref[...])
        dg_ref[...] += jnp.sum(dxn * xh, axis=0, keepdims=True)
        dx_ref[...] = dy_ref[...] + _rms_bwd(dxn, xh, r, g_ref[...])

    return _call(
        body, name=name, grid=(n_tiles,),
        in_specs=[_row_spec(tm, d), _row_spec(tm, d), _row_spec(tm, hc), _row_spec(tm, CONV_CH), _next_halo_spec(tm, CONV_CH, t),
                  _row_spec(tm, POOL_CH), _next_halo_spec(tm, POOL_CH, t), _const_spec((1, d)), _const_spec(conv_w.shape), HBM_SPEC],
        out_specs=[_row_spec(tm, d), _row_spec(tm, hc), _const_spec((1, d))],
        out_shape=[jax.ShapeDtypeStruct((t, d), F32), jax.ShapeDtypeStruct((t, hc), BF16), jax.ShapeDtypeStruct((1, d), F32)],
        scratch_shapes=[pltpu.VMEM((N_CHIPS, d, c), BF16), pltpu.SemaphoreType.DMA((N_CHIPS,))],
        args=(x, dy, h0, dconv, dconv, dpc, dpc, g, conv_w, win_g))


SQRT_HALF = 0.7071067811865476
INV_SQRT_2PI = 0.3989422804014327


def _gelu(x):
    return 0.5 * x * (1.0 + lax.erf(x * SQRT_HALF))


def _gelu_grad(x):
    return 0.5 * (1.0 + lax.erf(x * SQRT_HALF)) + x * jnp.exp(-0.5 * x * x) * INV_SQRT_2PI


def _causal_mask():
    return (lax.broadcasted_iota(jnp.int32, (CHUNK, CHUNK), 1) <= lax.broadcasted_iota(jnp.int32, (CHUNK, CHUNK), 0)).astype(F32)


def _sgu_recompute(pre, lg, lb):
    half = pre.shape[1] // 2
    z = _gelu(pre)
    u, v = z[:, 0:half], z[:, half:]
    vhat, rs = _ln_fwd(v)
    return u, vhat, rs, vhat * lg + lb


def _sgu_spatial(vln, w_ref, bt, tm):
    mask = _causal_mask()
    wm = [(w_ref[hd] * mask).astype(BF16) for hd in range(SGU_HEADS)]
    vb = vln.astype(BF16)
    rows = []
    for ch in range(tm // CHUNK):
        blocks = [_dot(wm[hd], vb[ch * CHUNK:(ch + 1) * CHUNK, hd * CHUNK:(hd + 1) * CHUNK]) + bt[:, hd:hd + 1] for hd in range(SGU_HEADS)]
        rows.append(jnp.concatenate(blocks, axis=1))
    return jnp.concatenate(rows, axis=0), wm


def _sgu_fwd(x, pre, ln_g, ln_b, w, bt, wout_g, name):
    t, d = x.shape
    tm = min(256, t)
    pc = pre.shape[1]

    def body(x_ref, pre_ref, lg_ref, lb_ref, w_ref, bt_ref, wout_hbm, xo_ref, p_ref, wout_v, sem):
        _load_at_first_step(_out_weight_copies(wout_hbm, wout_v, sem))
        u, _, _, vln = _sgu_recompute(pre_ref[...], lg_ref[...], lb_ref[...])
        vo, _ = _sgu_spatial(vln, w_ref, bt_ref[...], tm)
        p = (u * vo).astype(BF16)
        p_ref[...] = p
        xo_ref[...] = x_ref[...] + _dot(p, wout_v[...])

    return _call(
        body, name=name, grid=(t // tm,),
        in_specs=[_row_spec(tm, d), _row_spec(tm, pc), _const_spec((1, d)), _const_spec((1, d)), _const_spec(w.shape),
                  _const_spec(bt.shape), HBM_SPEC],
        out_specs=[_row_spec(tm, d), _row_spec(tm, d)],
        out_shape=[jax.ShapeDtypeStruct((t, d), F32), jax.ShapeDtypeStruct((t, d), BF16)],
        scratch_shapes=[pltpu.VMEM((d, d), BF16), pltpu.SemaphoreType.DMA((N_CHIPS,))],
        args=(x, pre, ln_g, ln_b, w, bt, wout_g))


def _sgu_bwd(x, dy, pre, g, ln_g, ln_b, w, bt, win_g, wout_g, name, comms=()):
    t, d = x.shape
    tm = min(256, t)
    pc = pre.shape[1]
    c = win_g.shape[-1]

    def body(x_ref, dy_ref, pre_ref, g_ref, lg_ref, lb_ref, w_ref, bt_ref, win_hbm, wout_hbm,
             dx_ref, dpre_ref, dg_ref, dlg_ref, dlb_ref, dw_ref, dbt_ref, win_v, wout_v, sem):
        _load_at_first_step(_in_weight_copies(win_hbm, win_v, sem) + _out_weight_copies(wout_hbm, wout_v, sem, N_CHIPS))
        i = pl.program_id(0)

        @pl.when(i == 0)
        def _():
            for ref in (dg_ref, dlg_ref, dlb_ref, dw_ref, dbt_ref):
                ref[...] = jnp.zeros_like(ref)

        prev = pre_ref[...]
        lg = lg_ref[...]
        u, vhat, rs, vln = _sgu_recompute(prev, lg, lb_ref[...])
        vo, wm = _sgu_spatial(vln, w_ref, bt_ref[...], tm)
        dp = _dot_nt(dy_ref[...].astype(BF16), wout_v[...])
        du = dp * vo
        dvo = dp * u
        dvob = dvo.astype(BF16)
        vb = vln.astype(BF16)
        head_lane = lax.broadcasted_iota(jnp.int32, (CHUNK, SGU_HEADS), 1)
        dbt = jnp.zeros((CHUNK, SGU_HEADS), F32)
        dw = [jnp.zeros((CHUNK, CHUNK), F32) for _ in range(SGU_HEADS)]
        rows = []
        for ch in range(tm // CHUNK):
            rs_ = slice(ch * CHUNK, (ch + 1) * CHUNK)
            blocks = []
            for hd in range(SGU_HEADS):
                cs = slice(hd * CHUNK, (hd + 1) * CHUNK)
                dbt = dbt + jnp.where(head_lane == hd, jnp.sum(dvo[rs_, cs], axis=1, keepdims=True), 0.0)
                dw[hd] = dw[hd] + _dot_nt(dvob[rs_, cs], vb[rs_, cs])
                blocks.append(_dot_tn(wm[hd], dvob[rs_, cs]))
            rows.append(jnp.concatenate(blocks, axis=1))
        dvln = jnp.concatenate(rows, axis=0)
        mask = _causal_mask()
        for hd in range(SGU_HEADS):
            dw_ref[hd] += dw[hd] * mask
        dbt_ref[...] += dbt
        dlg_ref[...] += jnp.sum(dvln * vhat, axis=0, keepdims=True)
        dlb_ref[...] += jnp.sum(dvln, axis=0, keepdims=True)
        dv = _ln_bwd(dvln * lg, vhat, rs)
        dpre = (jnp.concatenate([du, dv], axis=1) * _gelu_grad(prev)).astype(BF16)
        dpre_ref[...] = dpre
        dxn = _proj_in_bwd_tail(dpre, win_v, c)
        xh, r = _rms_fwd(x_ref[...])
        dg_ref[...] += jnp.sum(dxn * xh, axis=0, keepdims=True)
        dx_ref[...] = dy_ref[...] + _rms_bwd(dxn, xh, r, g_ref[...])

    return _call(
        body, name=name, grid=(t // tm,),
        in_specs=[_row_spec(tm, d), _row_spec(tm, d), _row_spec(tm, pc), _const_spec((1, d)), _const_spec((1, d)), _const_spec((1, d)),
                  _const_spec(w.shape), _const_spec(bt.shape), HBM_SPEC, HBM_SPEC],
        out_specs=[_row_spec(tm, d), _row_spec(tm, pc), _const_spec((1, d)), _const_spec((1, d)), _const_spec((1, d)),
                   _const_spec(w.shape), _const_spec(bt.shape)],
        out_shape=[jax.ShapeDtypeStruct((t, d), F32), jax.ShapeDtypeStruct((t, pc), BF16), jax.ShapeDtypeStruct((1, d), F32),
                   jax.ShapeDtypeStruct((1, d), F32), jax.ShapeDtypeStruct((1, d), F32), jax.ShapeDtypeStruct(w.shape, F32),
                   jax.ShapeDtypeStruct(bt.shape, F32)],
        scratch_shapes=[pltpu.VMEM((N_CHIPS, d, c), BF16), pltpu.VMEM((d, d), BF16), pltpu.SemaphoreType.DMA((2 * N_CHIPS,))],
        args=(x, dy, pre, g, ln_g, ln_b, w, bt, win_g, wout_g), comms=comms)


def _final_loss(x, tgt, g, name):
    t, d = x.shape
    tm = min(512, t)

    def body(x_ref, t_ref, g_ref, dx_ref, loss_ref, dg_ref):
        @pl.when(pl.program_id(0) == 0)
        def _():
            loss_ref[...] = jnp.zeros_like(loss_ref)
            dg_ref[...] = jnp.zeros_like(dg_ref)

        gv = g_ref[...]
        xh, r = _rms_fwd(x_ref[...])
        diff = xh * gv - t_ref[...]
        loss_ref[...] += 0.5 * jnp.sum(jnp.sum(diff * diff, axis=1, keepdims=True), axis=0, keepdims=True) / d
        dout = diff / d
        dg_ref[...] += jnp.sum(dout * xh, axis=0, keepdims=True)
        dx_ref[...] = _rms_bwd(dout, xh, r, gv)

    return _call(
        body, name=name, grid=(t // tm,),
        in_specs=[_row_spec(tm, d), _row_spec(tm, d), _const_spec((1, d))],
        out_specs=[_row_spec(tm, d), _const_spec((1, 1)), _const_spec((1, d))],
        out_shape=[jax.ShapeDtypeStruct((t, d), F32), jax.ShapeDtypeStruct((1, 1), F32), jax.ShapeDtypeStruct((1, d), F32)],
        args=(x, tgt, g))


def _tn_matmul(a, b, scale, bm, bn, name, comms=()):
    t, m = a.shape
    n = b.shape[1]
    tk = min(512, t)
    bm, bn = min(bm, m), min(bn, n)
    nk = t // tk

    def body(a_ref, b_ref, o_ref, acc_ref):
        k = pl.program_id(2)

        @pl.when(k == 0)
        def _():
            acc_ref[...] = jnp.zeros_like(acc_ref)

        bv = b_ref[...]
        if bv.dtype != BF16:
            bv = (scale * bv).astype(BF16)
        acc_ref[...] += _dot_tn(a_ref[...], bv)

        @pl.when(k == nk - 1)
        def _():
            o_ref[...] = acc_ref[...].astype(BF16)

    return _call(
        body, name=name, grid=(m // bm, n // bn, nk),
        in_specs=[pl.BlockSpec((tk, bm), lambda i, j, k: (k, i)), pl.BlockSpec((tk, bn), lambda i, j, k: (k, j))],
        out_specs=[pl.BlockSpec((bm, bn), lambda i, j, k: (i, j))],
        out_shape=[jax.ShapeDtypeStruct((m, n), BF16)],
        scratch_shapes=[pltpu.VMEM((bm, bn), F32)],
        args=(a, b), comms=comms)[0]


def _row_tile(rows, cols, budget_bytes=2 * 1024 * 1024):
    best = None
    for cand in range(16, rows + 1, 16):
        if rows % cand == 0 and cand * cols * 4 <= budget_bytes:
            best = cand
    return best or rows


def _scalar_grid(grid, in_specs, out_specs):
    return pltpu.PrefetchScalarGridSpec(num_scalar_prefetch=1, grid=grid, in_specs=in_specs, out_specs=out_specs)


def _cast_into_slot(w, layer, me, name):
    _, rows, cols = w.shape
    tr = _row_tile(rows, cols)

    def body(me_ref, w_ref, o_ref):
        o_ref[...] = w_ref[...].astype(BF16)

    return pl.pallas_call(
        body, name=name,
        grid_spec=_scalar_grid((rows // tr,), [pl.BlockSpec((None, tr, cols), lambda i, me: (layer, i, 0))],
                               pl.BlockSpec((None, tr, cols), lambda i, me: (me[0], i, 0))),
        out_shape=jax.ShapeDtypeStruct((N_CHIPS, rows, cols), BF16), compiler_params=_cparams())(me, w)


def _add_half(view, other, core, name):
    q, _, r, c = view.shape
    tr = _row_tile(r, c)

    def body(core_ref, a_ref, b_ref, o_ref):
        o_ref[...] = (a_ref[...].astype(F32) + b_ref[...].astype(F32)).astype(BF16)

    return pl.pallas_call(
        body, name=name,
        grid_spec=_scalar_grid((q, r // tr), [pl.BlockSpec((None, None, tr, c), lambda k, i, core: (k, core[0], i, 0)),
                                             pl.BlockSpec((None, tr, c), lambda k, i, core: (k, i, 0))],
                               pl.BlockSpec((None, tr, c), lambda k, i, core: (k, i, 0))),
        out_shape=jax.ShapeDtypeStruct((q, r, c), BF16), compiler_params=_cparams_nd(2))(core, view, other)


def _reduce_piece(partial, staged, me, column_sharded, name):
    _, r, c = staged.shape
    tr = _row_tile(r, c, budget_bytes=1024 * 1024)
    nt = r // tr
    if column_sharded:
        own2d = partial.reshape(r, N_CHIPS * c)
        own_spec = pl.BlockSpec((tr, c), lambda i, me: (i, me[0]))
    else:
        own2d = partial.reshape(N_CHIPS * r, c)
        own_spec = pl.BlockSpec((tr, c), lambda i, me: (me[0] * nt + i, 0))
    ring = [pl.BlockSpec((None, tr, c), lambda i, me, k=k: ((me[0] + k) % N_CHIPS, i, 0)) for k in (1, 2, 3)]

    def body(me_ref, own_ref, s1_ref, s2_ref, s3_ref, o_ref):
        o_ref[...] = ((own_ref[...].astype(F32) + s1_ref[...].astype(F32)) + s2_ref[...].astype(F32)) + s3_ref[...].astype(F32)

    return pl.pallas_call(
        body, name=name, grid_spec=_scalar_grid((nt,), [own_spec] + ring, pl.BlockSpec((tr, c), lambda i, me: (i, 0))),
        out_shape=jax.ShapeDtypeStruct((r, c), F32), compiler_params=_cparams())(me, own2d, staged, staged, staged)


def _sum_leading(s, name):
    n, rows, cols = s.shape
    tr = _row_tile(rows, cols, budget_bytes=1024 * 1024)

    def body(s_ref, o_ref):
        acc = s_ref[0].astype(F32)
        for k in range(1, n):
            acc = acc + s_ref[k].astype(F32)
        o_ref[...] = acc

    return pl.pallas_call(
        body, name=name, grid=(rows // tr,), in_specs=[pl.BlockSpec((n, tr, cols), lambda i: (0, i, 0))], out_specs=_row_spec(tr, cols),
        out_shape=jax.ShapeDtypeStruct((rows, cols), F32), compiler_params=_cparams())(s)


ADAM_C1 = 1.0 / (1.0 - ADAM_B1 ** ADAM_STEP)
ADAM_C2 = 1.0 / (1.0 - ADAM_B2 ** ADAM_STEP)


def _adamw_math(w, g, m, v):
    mn = ADAM_B1 * m + (1.0 - ADAM_B1) * g
    vn = ADAM_B2 * v + (1.0 - ADAM_B2) * (g * g)
    return -ADAM_LR * ((mn * ADAM_C1) / (jnp.sqrt(vn * ADAM_C2) + ADAM_EPS) + ADAM_WD * w), mn, vn


def _adamw(w, g, m, v, name):
    shape = w.shape
    cols = shape[-1] if w.ndim > 1 else 128
    w2, g2, m2, v2 = (a.reshape(-1, cols) for a in (w, g, m, v))
    rows = w2.shape[0]
    tr = _row_tile(rows, cols, budget_bytes=1024 * 1024)

    def body(w_ref, g_ref, m_ref, v_ref, d_ref, mo_ref, vo_ref):
        d_ref[...], mo_ref[...], vo_ref[...] = _adamw_math(w_ref[...], g_ref[...], m_ref[...], v_ref[...])

    spec = _row_spec(tr, cols)
    outs = pl.pallas_call(
        body, name=name, grid=(rows // tr,), in_specs=[spec] * 4, out_specs=[spec] * 3,
        out_shape=[jax.ShapeDtypeStruct((rows, cols), F32)] * 3, compiler_params=_cparams())(w2, g2, m2, v2)
    return tuple(o.reshape(shape) for o in outs)


def _adamw_sharded(w, g_mine, g_sibling, m, v, core, layer, prev, name):
    n_layers, r, c = w.shape
    half = r // 2
    tr = _row_tile(half, c, budget_bytes=1024 * 1024)
    nt = half // tr

    def body(core_ref, w_ref, gm_ref, gs_ref, m_ref, v_ref, *rest):
        g_ref, d_ref, mo_ref, vo_ref = rest[-4:]
        gv = jnp.where(pl.program_id(0) == core_ref[0], gm_ref[...], gs_ref[...])
        g_ref[...] = gv
        d_ref[...], mo_ref[...], vo_ref[...] = _adamw_math(w_ref[...], gv, m_ref[...], v_ref[...])

    full = pl.BlockSpec((None, tr, c), lambda h, i, core: (layer, h * nt + i, 0))
    part = pl.BlockSpec((tr, c), lambda h, i, core: (i, 0))
    args = [core, w, g_mine, g_sibling, m, v]
    in_specs = [full, part, part, full, full]
    aliases = {}
    if prev is not None:
        aliases = {len(args) + k: k for k in range(4)}
        args += list(prev)
        in_specs += [pl.BlockSpec(memory_space=pl.ANY)] * 4
    return pl.pallas_call(
        body, name=name, grid_spec=_scalar_grid((2, nt), in_specs, [full] * 4),
        out_shape=[jax.ShapeDtypeStruct(w.shape, F32)] * 4, input_output_aliases=aliases, compiler_params=_cparams_nd(2))(*args)


BIG_IN = ("ffn1_w_in", "ffn2_w_in", "ab_w_in", "sgu_w_in")
BIG_OUT = ("ffn1_w_out", "ffn2_w_out", "ab_w_out", "sgu_w_out")
BIG = BIG_IN + BIG_OUT


class _Gatherer:
    def __init__(self, slots):
        self.slots = dict(slots)

    def _stage(self, keys, d2d):
        n = len(keys)

        def plan(ins, outs, place):
            x, y, c = place
            me = 2 * x + y
            remote = []
            for a in range(n):
                rows = ins[a].shape[1] // 2

                def half(ref, q, core, rows=rows):
                    return ref.at[q, pl.ds(core * rows, rows), :]

                for (px, py) in _other_chips(x, y):
                    q = 2 * px + py
                    if d2d:
                        remote.append((half(ins[a], q, c), half(outs[a], q, c), (x, y, 1 - c), half(outs[a], q, 1 - c)))
                    else:
                        remote.append((half(ins[a], me, c), half(outs[a], me, c), (px, py, c), half(outs[a], q, c)))
            return remote

        def finish(outs):
            for k, o in zip(keys, outs):
                self.slots[k] = o

        arrays = [self.slots[k] for k in keys]
        return _Exchange(arrays, [_sds(a) for a in arrays], plan, 3 * n, {a: a for a in range(n)}, finish)

    def ici(self, keys):
        return self._stage(keys, False)

    def d2d(self, keys):
        return self._stage(keys, True)


class _Reducer:
    def __init__(self, me, core):
        self.me, self.core = me, core
        self.views, self.partial, self.mine, self.theirs = {}, {}, {}, {}

    def add(self, key, g):
        m, n = g.shape
        if key[0] in BIG_IN:
            self.views[key] = g.reshape(1, 2, m // 2, n)
        else:
            self.views[key] = g.reshape(N_CHIPS, 2, m // (2 * N_CHIPS), n)

    def swap(self, keys):
        views = [self.views[k] for k in keys]

        def plan(ins, outs, place):
            x, y, c = place
            return [(ins[a].at[:, 1 - c], outs[a], (x, y, 1 - c), outs[a]) for a in range(len(keys))]

        def finish(outs):
            for k, v, o in zip(keys, views, outs):
                self.partial[k] = _add_half(v, o, self.core, f"chip_partial_{k[0]}_{k[1]}")

        shapes = [jax.ShapeDtypeStruct((v.shape[0],) + v.shape[2:], v.dtype) for v in views]
        return _Exchange(views, shapes, plan, len(keys), None, finish)

    def scatter(self, keys):
        parts = [self.partial[k] for k in keys]
        shapes = []
        for k, p in zip(keys, parts):
            q, r, c = p.shape
            shapes.append(jax.ShapeDtypeStruct((N_CHIPS, r, c // N_CHIPS if k[0] in BIG_IN else c), p.dtype))

        def piece(ref, key, q, cols):
            return ref.at[0, :, pl.ds(q * cols, cols)] if key[0] in BIG_IN else ref.at[q]

        def plan(ins, outs, place):
            x, y, c = place
            me = 2 * x + y
            remote = []
            for a, k in enumerate(keys):
                cols = shapes[a].shape[2]
                for (px, py) in _other_chips(x, y):
                    q = 2 * px + py
                    remote.append((piece(ins[a], k, q, cols), outs[a].at[me], (px, py, c), outs[a].at[q]))
            return remote

        def finish(outs):
            for k, p, o in zip(keys, parts, outs):
                self.mine[k] = _reduce_piece(p, o, self.me, k[0] in BIG_IN, f"reduce_{k[0]}_{k[1]}")

        return _Exchange(parts, shapes, plan, 3 * len(keys), None, finish)

    def exchange(self, keys):
        mine = [self.mine[k] for k in keys]

        def plan(ins, outs, place):
            x, y, c = place
            return [(ins[a], outs[a], (x, y, 1 - c), outs[a]) for a in range(len(keys))]

        def finish(outs):
            for k, o in zip(keys, outs):
                self.theirs[k] = o

        return _Exchange(mine, [_sds(a) for a in mine], plan, len(keys), None, finish)


def _all_gather_full(gat, keys):
    n = len(keys)
    arrays = [gat.slots[k] for k in keys]

    def body(*refs):
        ins, outs = refs[:n], refs[n:2 * n]
        send_sems, recv_sems = refs[2 * n:]
        x, y, c = _my_place()
        sibling = (x, y, 1 - c)
        chips = _other_chips(x, y)
        me = 2 * x + y

        def half(ref, q, core):
            rows = ref.shape[1] // 2
            return ref.at[q, pl.ds(core * rows, rows), :]

        def copy(a, k, src, dst, to):
            return pltpu.make_async_remote_copy(src_ref=src, dst_ref=dst, send_sem=send_sems.at[6 * a + k], recv_sem=recv_sems.at[6 * a + k],
                                                device_id=to, device_id_type=MESH)

        first = [copy(a, j, half(ins[a], me, c), half(outs[a], me, c), (*chip, c)) for a in range(n) for j, chip in enumerate(chips)]
        for cp in first:
            cp.start()
        passed = []
        for a in range(n):
            for j, (px, py) in enumerate(chips):
                landed = half(outs[a], 2 * px + py, c)
                copy(a, j, landed, landed, (px, py, c)).wait_recv()
                fwd = copy(a, 3 + j, landed, landed, sibling)
                fwd.start()
                passed.append(fwd)
        for a in range(n):
            for j, (px, py) in enumerate(chips):
                other = half(outs[a], 2 * px + py, 1 - c)
                copy(a, 3 + j, other, other, sibling).wait_recv()
        for cp in first + passed:
            cp.wait_send()

    outs = pl.pallas_call(
        body, name="all_gather_first_weights", in_specs=[HBM_SPEC] * n, out_specs=[HBM_SPEC] * n,
        out_shape=[_sds(a) for a in arrays], input_output_aliases={a: a for a in range(n)},
        scratch_shapes=[pltpu.SemaphoreType.DMA((6 * n,)), pltpu.SemaphoreType.DMA((6 * n,))])(*arrays)
    for k, o in zip(keys, outs):
        gat.slots[k] = o


def _small_all_gather(buf, done):
    peers = [(0, 0, 1), (1, 0, 0), (0, 1, 0), (1, 1, 0), (1, 0, 1), (0, 1, 1), (1, 1, 1)]

    def index(x, y, c):
        return 4 * x + 2 * y + c

    def plan(ins, outs, place):
        x, y, c = place
        remote = []
        for fx, fy, fc in peers:
            px, py, pc = (1 - x if fx else x), (1 - y if fy else y), (1 - c if fc else c)
            remote.append((ins[0], outs[0].at[index(x, y, c)], (px, py, pc), outs[0].at[index(px, py, pc)]))
        return remote

    def local(ins, outs, place):
        return [(ins[0], outs[0].at[index(*place)])]

    return _Exchange([buf], [jax.ShapeDtypeStruct((2 * N_CHIPS,) + buf.shape, buf.dtype)], plan, len(peers), None,
                     lambda outs: done(outs[0]), local, 1)


WEIGHT_NAMES = ("ffn1_norm", "ffn1_w_in", "ffn1_w_out", "mix_norm", "ffn2_norm", "ffn2_w_in", "ffn2_w_out", "ab_w_in", "pool_w", "pool_b",
                "pool_scale", "conv_w", "conv_b", "conv_ln_g", "conv_ln_b", "ab_w_out", "sgu_w_in", "sgu_ln_g", "sgu_ln_b", "sgu_w", "sgu_b",
                "sgu_w_out", "final_norm")
SMALL = tuple(n for n in WEIGHT_NAMES if n not in BIG)
SHARDED_SMALL = ("conv_w", "sgu_ln_g", "sgu_ln_b")
PACK_ROWS = 48
PACK = ("pack", 0)


def _pair(prefix, layer):
    return [(prefix + "_w_in", layer), (prefix + "_w_out", layer)]


def kernel(x, ffn1_norm, ffn1_w_in, ffn1_w_out, mix_norm, ffn2_norm, ffn2_w_in, ffn2_w_out, ab_w_in, pool_w, pool_b, pool_scale, conv_w, conv_b, conv_ln_g, conv_ln_b, ab_w_out, sgu_w_in, sgu_ln_g, sgu_ln_b, sgu_w, sgu_b, sgu_w_out, final_norm, loss_target, m_ffn1_norm, m_ffn1_w_in, m_ffn1_w_out, m_mix_norm, m_ffn2_norm, m_ffn2_w_in, m_ffn2_w_out, m_ab_w_in, m_pool_w, m_pool_b, m_pool_scale, m_conv_w, m_conv_b, m_conv_ln_g, m_conv_ln_b, m_ab_w_out, m_sgu_w_in, m_sgu_ln_g, m_sgu_ln_b, m_sgu_w, m_sgu_b, m_sgu_w_out, m_final_norm, v_ffn1_norm, v_ffn1_w_in, v_ffn1_w_out, v_mix_norm, v_ffn2_norm, v_ffn2_w_in, v_ffn2_w_out, v_ab_w_in, v_pool_w, v_pool_b, v_pool_scale, v_conv_w, v_conv_b, v_conv_ln_g, v_conv_ln_b, v_ab_w_out, v_sgu_w_in, v_sgu_ln_g, v_sgu_ln_b, v_sgu_w, v_sgu_b, v_sgu_w_out, v_final_norm):
    given = dict(locals())
    w = {n: given[n] for n in WEIGHT_NAMES}
    chip = 2 * lax.axis_index("x") + lax.axis_index("y")
    me = chip.astype(jnp.int32).reshape(1)
    core = lax.axis_index("c").astype(jnp.int32).reshape(1)
    row = lambda v: v.reshape(1, -1)
    xin, tgt = x[0], loss_target[0]

    pack = jnp.concatenate([
        w["conv_w"][0], jnp.zeros((1, 128), F32), w["sgu_ln_g"].reshape(2, 128), w["sgu_ln_b"].reshape(2, 128),
        jnp.zeros((PACK_ROWS - 36, 128), F32)], axis=0)
    slots = {PACK: lax.dynamic_update_slice(jnp.zeros((N_CHIPS, PACK_ROWS, 128), F32), pack[None], (me[0], 0, 0))}
    for n in BIG:
        for layer in range(w[n].shape[0]):
            slots[(n, layer)] = _cast_into_slot(w[n], layer, me, f"cast_{n}_{layer}")
    gat = _Gatherer(slots)
    _all_gather_full(gat, _pair("ffn1", 0) + _pair("ab", 0) + [PACK])
    gp = gat.slots[PACK]
    conv_w_full = jnp.transpose(gp[:, 0:CONV_WIDTH], (1, 0, 2)).reshape(CONV_WIDTH, N_CHIPS * 128)
    sgu_ln_g_full = gp[:, 32:34].reshape(1, -1)
    sgu_ln_b_full = gp[:, 34:36].reshape(1, -1)
    gw = lambda n, layer: gat.slots[(n, layer)]

    st = [dict(), dict()]
    st[0]["xa"] = xin
    later1, later2 = _pair("ffn2", 0) + _pair("ffn1", 1), _pair("sgu", 0) + _pair("ffn2", 1)
    cur, st[0]["h1"], st[0]["xn1"] = _ffn_fwd(xin, row(w["ffn1_norm"][0]), gw("ffn1_w_in", 0), gw("ffn1_w_out", 0), "ffn1_fwd_0",
                                              comms=[gat.ici(later1)])
    st[0]["xb"] = cur
    st[0]["h0"], st[0]["xnm"] = _norm_matmul(cur, row(w["mix_norm"][0]), gw("ab_w_in", 0), "mix0_proj_in", comms=[gat.d2d(later1)])
    pool_args = (w["pool_w"][0], row(w["pool_b"][0]), row(w["pool_scale"][0]), conv_w_full, row(w["conv_b"][0]), row(w["conv_ln_g"][0]),
                 row(w["conv_ln_b"][0]), gw("ab_w_out", 0))
    cur, st[0]["ycat"] = _mix0_fwd(cur, st[0]["h0"], *pool_args, "mix0_fwd")
    st[0]["xc"] = cur
    cur, st[0]["h2"], st[0]["xn2"] = _ffn_fwd(cur, row(w["ffn2_norm"][0]), gw("ffn2_w_in", 0), gw("ffn2_w_out", 0), "ffn2_fwd_0",
                                              comms=[gat.ici(later2)])
    st[1]["xa"] = cur
    cur, st[1]["h1"], st[1]["xn1"] = _ffn_fwd(cur, row(w["ffn1_norm"][1]), gw("ffn1_w_in", 1), gw("ffn1_w_out", 1), "ffn1_fwd_1",
                                              comms=[gat.d2d(later2)])
    st[1]["xb"] = cur
    st[1]["pre"], st[1]["xnm"] = _norm_matmul(cur, row(w["mix_norm"][1]), gw("sgu_w_in", 0), "sgu_proj_in")
    sgu_args = (sgu_ln_g_full, sgu_ln_b_full, w["sgu_w"][0], w["sgu_b"][0].T)
    cur, st[1]["p"] = _sgu_fwd(cur, st[1]["pre"], *sgu_args, gw("sgu_w_out", 0), "sgu_fwd")
    st[1]["xc"] = cur
    cur, st[1]["h2"], st[1]["xn2"] = _ffn_fwd(cur, row(w["ffn2_norm"][1]), gw("ffn2_w_in", 1), gw("ffn2_w_out", 1), "ffn2_fwd_1")
    dy, loss, d_final = _final_loss(cur, tgt, row(w["final_norm"]), "final_loss")

    red = _Reducer(me, core)
    small = {"final_norm": d_final.reshape(-1)}
    norm_grads = {"ffn1_norm": [None] * DEPTH, "mix_norm": [None] * DEPTH, "ffn2_norm": [None] * DEPTH}
    ga, gb, gc, gd, ge, gf = _pair("ffn2", 1), _pair("sgu", 0), _pair("ffn1", 1), _pair("ffn2", 0), _pair("ab", 0), _pair("ffn1", 0)

    def ffn_backward(prefix, layer, xs, hs, xns, dy_in, bwd_comms=(), dwin_comms=(), dwout_comms=()):
        dx, dh, act, norm_grads[prefix + "_norm"][layer] = _ffn_bwd(
            xs, dy_in, hs, row(w[prefix + "_norm"][layer]), gw(prefix + "_w_in", layer), gw(prefix + "_w_out", layer),
            f"{prefix}_bwd_{layer}", comms=bwd_comms)
        red.add((prefix + "_w_in", layer), _tn_matmul(xns, dh, 1.0, 1024, 1408, f"{prefix}_dwin_{layer}", comms=dwin_comms))
        red.add((prefix + "_w_out", layer), _tn_matmul(act, dy_in, 0.5, 1408, 1024, f"{prefix}_dwout_{layer}", comms=dwout_comms))
        return dx

    s1, s0 = st[1], st[0]
    dy = ffn_backward("ffn2", 1, s1["xc"], s1["h2"], s1["xn2"], dy)
    dy_in = dy
    dy, dpre, norm_grads["mix_norm"][1], dlg, dlb, dw, dbt = _sgu_bwd(
        s1["xb"], dy_in, s1["pre"], row(w["mix_norm"][1]), *sgu_args, gw("sgu_w_in", 0), gw("sgu_w_out", 0), "sgu_bwd", comms=[red.swap(ga)])
    red.add(("sgu_w_in", 0), _tn_matmul(s1["xnm"], dpre, 1.0, 1024, 2048, "sgu_dwin"))
    red.add(("sgu_w_out", 0), _tn_matmul(s1["p"], dy_in, 1.0, 1024, 1024, "sgu_dwout"))
    small.update(sgu_ln_g=dlg, sgu_ln_b=dlb, sgu_w=dw[None], sgu_b=dbt.T[None])
    dy = ffn_backward("ffn1", 1, s1["xa"], s1["h1"], s1["xn1"], dy, bwd_comms=[lambda: red.scatter(ga), lambda: red.swap(gb)],
                      dwin_comms=[lambda: red.scatter(gb), lambda: red.exchange(ga)])
    dy = ffn_backward("ffn2", 0, s0["xc"], s0["h2"], s0["xn2"], dy, bwd_comms=[lambda: red.swap(gc), lambda: red.exchange(gb)],
                      dwin_comms=[lambda: red.scatter(gc)])
    dy_in = dy
    dconv, dpc, dpw, rows = _mix0_bwd_a(dy_in, s0["h0"], *pool_args, "mix0_bwd_a")
    dy, dh0, norm_grads["mix_norm"][0] = _mix0_bwd_b(s0["xb"], dy_in, s0["h0"], dconv, dpc, row(w["mix_norm"][0]), conv_w_full,
                                                      gw("ab_w_in", 0), "mix0_bwd_b")
    red.add(("ab_w_in", 0), _tn_matmul(s0["xnm"], dh0, 1.0, 1024, 1536, "ab_dwin", comms=[red.swap(gd), red.exchange(gc)]))
    red.add(("ab_w_out", 0), _tn_matmul(s0["ycat"], dy_in, 1.0, 1024, 1024, "ab_dwout"))
    small.update(pool_w=dpw[None], conv_w=rows[None, 0:CONV_WIDTH], conv_b=rows[32:33], conv_ln_g=rows[33:34], conv_ln_b=rows[34:35],
                 pool_scale=rows[35:36], pool_b=rows[36:37].reshape(1, len(POOL_WINDOWS), POOL_GC))

    small_sum = {}

    def small_ready():
        for k, v in norm_grads.items():
            small[k] = jnp.concatenate(v, axis=0)
        flat = [small[n].reshape(-1, 128) for n in SMALL]
        buf = jnp.concatenate(flat, axis=0)
        pad = (-buf.shape[0]) % 8
        if pad:
            buf = jnp.concatenate([buf, jnp.zeros((pad, 128), F32)], axis=0)

        def done(gathered):
            total, at = _sum_leading(gathered, "reduce_small"), 0
            for n, f in zip(SMALL, flat):
                small_sum[n] = total[at:at + f.shape[0]].reshape(small[n].shape)
                at += f.shape[0]

        return _small_all_gather(buf, done)

    dx, dh, act, norm_grads["ffn1_norm"][0] = _ffn_bwd(
        s0["xa"], dy, s0["h1"], row(w["ffn1_norm"][0]), gw("ffn1_w_in", 0), gw("ffn1_w_out", 0), "ffn1_bwd_0",
        comms=[red.scatter(gd), red.swap(ge)])
    red.add(("ffn1_w_in", 0), _tn_matmul(s0["xn1"], dh, 1.0, 1024, 1408, "ffn1_dwin_0",
                                         comms=[red.scatter(ge), red.exchange(gd), small_ready()]))
    red.add(("ffn1_w_out", 0), _tn_matmul(act, dy, 0.5, 1408, 1024, "ffn1_dwout_0", comms=[red.exchange(ge)]))
    grad_x = dx
    _exchange_alone("swap_last_grads", [red.swap(gf)])
    _exchange_alone("scatter_last_grads", [red.scatter(gf)])
    _exchange_alone("exchange_last_grads", [red.exchange(gf)])

    loss = lax.psum(loss[0, 0], MESH_AXES)
    grads, delta, new_m, new_v = {}, {}, {}, {}
    for n in WEIGHT_NAMES:
        mom, var = given["m_" + n], given["v_" + n]
        if n in BIG:
            outs = None
            for layer in reversed(range(w[n].shape[0])):
                outs = _adamw_sharded(w[n], red.mine[(n, layer)], red.theirs[(n, layer)], mom, var, core, layer, outs, f"adamw_{n}_{layer}")
            grads[n], delta[n], new_m[n], new_v[n] = outs
            continue
        g = small_sum[n]
        if n in SHARDED_SMALL:
            width = w[n].shape[-1]
            g = lax.dynamic_slice_in_dim(g, chip * width, width, axis=g.ndim - 1)
        grads[n] = g
        delta[n], new_m[n], new_v[n] = _adamw(w[n], g, mom, var, f"adamw_{n}")
    return (loss, grad_x[None], *[grads[n] for n in WEIGHT_NAMES], *[delta[n] for n in WEIGHT_NAMES],
            *[new_m[n] for n in WEIGHT_NAMES], *[new_v[n] for n in WEIGHT_NAMES])
```

```python
import jax
import jax.numpy as jnp
from jax import lax
from jax.experimental import pallas as pl
from jax.experimental.pallas import tpu as pltpu

F32, BF16 = jnp.float32, jnp.bfloat16
EPS = 1e-6
N_CHIPS = 4
POOL_WINDOWS = (2, 4, 8, 16)
POOL_GC = 128
POOL_CH = 512
CONV_CH = 512
CONV_WIDTH = 31
HALO = 32
SGU_HEADS = 8
CHUNK = 128
DEPTH = 2
ADAM_LR, ADAM_B1, ADAM_B2, ADAM_EPS, ADAM_WD, ADAM_STEP = 0.001, 0.9, 0.999, 1e-08, 0.01, 10
VMEM_LIMIT_BYTES = 60 * 1024 * 1024
MESH_AXES = ("x", "y", "c")
MESH = pl.DeviceIdType.MESH
HBM_SPEC = pl.BlockSpec(memory_space=pltpu.HBM)


def _sds(a):
    return jax.ShapeDtypeStruct(a.shape, a.dtype)


def _cparams_nd(n):
    return pltpu.CompilerParams(dimension_semantics=("arbitrary",) * n, vmem_limit_bytes=VMEM_LIMIT_BYTES)


def _cparams():
    return _cparams_nd(1)


def _dot(a, b):
    return jnp.dot(a, b, preferred_element_type=F32)


def _dot_nt(a, b):
    return lax.dot_general(a, b, (((1,), (1,)), ((), ())), preferred_element_type=F32)


def _dot_tn(a, b):
    return lax.dot_general(a, b, (((0,), (0,)), ((), ())), preferred_element_type=F32)


def _rms_fwd(x):
    r = lax.rsqrt(jnp.mean(x * x, axis=-1, keepdims=True) + EPS)
    return x * r, r


def _rms_bwd(dxn, xh, r, g):
    dxh = dxn * g
    return r * (dxh - xh * jnp.mean(dxh * xh, axis=-1, keepdims=True))


def _ln_fwd(y):
    mu = jnp.mean(y, axis=-1, keepdims=True)
    yc = y - mu
    rs = lax.rsqrt(jnp.mean(yc * yc, axis=-1, keepdims=True) + EPS)
    return yc * rs, rs


def _ln_bwd(dyhat, yhat, rs):
    return rs * (dyhat - jnp.mean(dyhat, axis=-1, keepdims=True) - yhat * jnp.mean(dyhat * yhat, axis=-1, keepdims=True))


def _sigmoid(x):
    return 0.5 * jnp.tanh(0.5 * x) + 0.5


def _const_spec(shape):
    n = len(shape)
    return pl.BlockSpec(shape, lambda i: (0,) * n)


def _row_spec(tm, cols):
    return pl.BlockSpec((tm, cols), lambda i: (i, 0))


def _my_place():
    return lax.axis_index("x"), lax.axis_index("y"), lax.axis_index("c")


def _other_chips(x, y):
    return [(1 - x, y), (x, 1 - y), (1 - x, 1 - y)]


class _Exchange:
    def __init__(self, inputs, out_shapes, plan, count, aliases=None, finish=None, local=None, n_local=0):
        self.inputs, self.out_shapes, self.plan, self.count = list(inputs), list(out_shapes), plan, count
        self.aliases, self.finish, self.local, self.n_local = dict(aliases or {}), finish, local, n_local


def _call(body, *, name, grid, in_specs, out_specs, out_shape, args, scratch_shapes=(), comms=(), scalar=None, aliases=None):
    comms = [cm if isinstance(cm, _Exchange) else cm() for cm in comms]
    in_specs, out_specs, out_shape, scratch_shapes = list(in_specs), list(out_specs), list(out_shape), list(scratch_shapes)
    n_in, n_out, n_scr = len(in_specs), len(out_specs), len(scratch_shapes)
    c_in = [a for cm in comms for a in cm.inputs]
    c_out = [s for cm in comms for s in cm.out_shapes]
    n_remote = sum(cm.count for cm in comms)
    n_local = sum(cm.n_local for cm in comms)
    n_scalar = 0 if scalar is None else 1
    all_aliases = {n_scalar + i: o for i, o in (aliases or {}).items()}
    at_in, at_out = n_scalar + n_in, n_out
    for cm in comms:
        for i, o in cm.aliases.items():
            all_aliases[at_in + i] = at_out + o
        at_in += len(cm.inputs)
        at_out += len(cm.out_shapes)

    def wrapped(*all_refs):
        scalar_ref, refs = all_refs[:n_scalar], all_refs[n_scalar:]
        ins, ci = refs[:n_in], refs[n_in:n_in + len(c_in)]
        at = n_in + len(c_in)
        outs, co = refs[at:at + n_out], refs[at + n_out:at + n_out + len(c_out)]
        at += n_out + len(c_out)
        scr = refs[at:at + n_scr]

        def run_body():
            body(*scalar_ref, *ins, *outs, *scr)

        if not comms:
            run_body()
            return
        send_sems, recv_sems, local_sems = refs[at + n_scr:]
        place = _my_place()
        sends, arrivals, locals_ = [], [], []
        i0 = o0 = 0
        for cm in comms:
            cm_in, cm_out = ci[i0:i0 + len(cm.inputs)], co[o0:o0 + len(cm.out_shapes)]
            i0 += len(cm.inputs)
            o0 += len(cm.out_shapes)
            for src, dst, dev, incoming in cm.plan(cm_in, cm_out, place):
                k = len(sends)
                sends.append(pltpu.make_async_remote_copy(src_ref=src, dst_ref=dst, send_sem=send_sems.at[k], recv_sem=recv_sems.at[k],
                                                          device_id=dev, device_id_type=MESH))
                arrivals.append(pltpu.make_async_remote_copy(src_ref=src, dst_ref=incoming, send_sem=send_sems.at[k],
                                                             recv_sem=recv_sems.at[k], device_id=dev, device_id_type=MESH))
            if cm.local is not None:
                for src, dst in cm.local(cm_in, cm_out, place):
                    locals_.append(pltpu.make_async_copy(src, dst, local_sems.at[len(locals_)]))

        def start():
            for cp in locals_ + sends:
                cp.start()

        def finish():
            for cp in arrivals:
                cp.wait_recv()
            for cp in sends:
                cp.wait_send()
            for cp in locals_:
                cp.wait()

        if not grid:
            start()
            run_body()
            finish()
            return
        ids = [pl.program_id(a) for a in range(len(grid))]
        first, last = ids[0] == 0, ids[0] == grid[0] - 1
        for a in range(1, len(grid)):
            first = jnp.logical_and(first, ids[a] == 0)
            last = jnp.logical_and(last, ids[a] == grid[a] - 1)
        pl.when(first)(start)
        run_body()
        pl.when(last)(finish)

    sems = []
    if comms:
        sems = [pltpu.SemaphoreType.DMA((max(n_remote, 1),)), pltpu.SemaphoreType.DMA((max(n_remote, 1),)),
                pltpu.SemaphoreType.DMA((max(n_local, 1),))]
    all_in, all_out = in_specs + [HBM_SPEC] * len(c_in), out_specs + [HBM_SPEC] * len(c_out)
    if scalar is None:
        kwargs = dict(grid=grid, compiler_params=_cparams_nd(len(grid))) if grid else {}
        res = pl.pallas_call(
            wrapped, name=name, in_specs=all_in, out_specs=all_out, out_shape=out_shape + c_out, scratch_shapes=scratch_shapes + sems,
            input_output_aliases=all_aliases, **kwargs)(*args, *c_in)
    else:
        spec = pltpu.PrefetchScalarGridSpec(num_scalar_prefetch=1, grid=grid, in_specs=all_in, out_specs=all_out,
                                            scratch_shapes=scratch_shapes + sems)
        res = pl.pallas_call(
            wrapped, name=name, grid_spec=spec, out_shape=out_shape + c_out, input_output_aliases=all_aliases,
            compiler_params=_cparams_nd(len(grid)))(scalar, *args, *c_in)
    at = n_out
    for cm in comms:
        got = res[at:at + len(cm.out_shapes)]
        at += len(cm.out_shapes)
        if cm.finish is not None:
            cm.finish(got)
    return list(res[:n_out])


def _in_weight_copies(w_hbm, w_v, sem, base=0):
    return [pltpu.make_async_copy(w_hbm.at[q], w_v.at[q], sem.at[base + q]) for q in range(N_CHIPS)]


def _out_weight_copies(w_hbm, w_v, sem, base=0):
    rows = w_hbm.shape[1]
    return [pltpu.make_async_copy(w_hbm.at[q], w_v.at[pl.ds(q * rows, rows)], sem.at[base + q]) for q in range(N_CHIPS)]


def _load_at_first_step(copies):
    @pl.when(pl.program_id(0) == 0)
    def _():
        for cp in copies:
            cp.start()
        for cp in copies:
            cp.wait()


def _ffn_fwd(x, g, win_g, wout_g, name, comms=()):
    t, d = x.shape
    c = win_g.shape[-1]
    ff = 2 * c
    tm = min(256, t)

    def body(x_ref, g_ref, win_hbm, wout_hbm, xo_ref, h_ref, xn_ref, win_v, wout_v, sem):
        _load_at_first_step(_in_weight_copies(win_hbm, win_v, sem) + _out_weight_copies(wout_hbm, wout_v, sem, N_CHIPS))
        xv = x_ref[...]
        xh, _ = _rms_fwd(xv)
        xn = (xh * g_ref[...]).astype(BF16)
        xn_ref[...] = xn
        acc = jnp.zeros((tm, d), F32)
        for j in range(2):
            gate = _dot(xn, win_v[j])
            up = _dot(xn, win_v[j + 2])
            h_ref[:, j * c:(j + 1) * c] = gate.astype(BF16)
            h_ref[:, ff + j * c:ff + (j + 1) * c] = up.astype(BF16)
            act = (gate * _sigmoid(gate) * up).astype(BF16)
            acc = acc + _dot(act, wout_v[j * c:(j + 1) * c, :])
        xo_ref[...] = xv + 0.5 * acc

    return _call(
        body, name=name, grid=(t // tm,),
        in_specs=[_row_spec(tm, d), _const_spec((1, d)), HBM_SPEC, HBM_SPEC],
        out_specs=[_row_spec(tm, d), _row_spec(tm, 2 * ff), _row_spec(tm, d)],
        out_shape=[jax.ShapeDtypeStruct((t, d), F32), jax.ShapeDtypeStruct((t, 2 * ff), BF16), jax.ShapeDtypeStruct((t, d), BF16)],
        scratch_shapes=[pltpu.VMEM((N_CHIPS, d, c), BF16), pltpu.VMEM((ff, d), BF16), pltpu.SemaphoreType.DMA((2 * N_CHIPS,))],
        args=(x, g, win_g, wout_g), comms=comms)


def _ffn_bwd(x, dy, h, g, win_g, wout_g, name, comms=()):
    t, d = x.shape
    c = win_g.shape[-1]
    ff = 2 * c
    tm = min(256, t)

    def body(x_ref, dy_ref, h_ref, g_ref, win_hbm, wout_hbm, dx_ref, dh_ref, act_ref, dg_ref, win_v, wout_v, sem):
        _load_at_first_step(_in_weight_copies(win_hbm, win_v, sem) + _out_weight_copies(wout_hbm, wout_v, sem, N_CHIPS))

        @pl.when(pl.program_id(0) == 0)
        def _():
            dg_ref[...] = jnp.zeros_like(dg_ref)

        xv, dyv, gv = x_ref[...], dy_ref[...], g_ref[...]
        xh, r = _rms_fwd(xv)
        dyh = (0.5 * dyv).astype(BF16)
        dxn = jnp.zeros((tm, d), F32)
        for j in range(2):
            gate = h_ref[:, j * c:(j + 1) * c].astype(F32)
            up = h_ref[:, ff + j * c:ff + (j + 1) * c].astype(F32)
            dact = _dot_nt(dyh, wout_v[j * c:(j + 1) * c, :])
            s = _sigmoid(gate)
            sl = gate * s
            act_ref[:, j * c:(j + 1) * c] = (sl * up).astype(BF16)
            dgate = (dact * up * (s * (1.0 + gate * (1.0 - s)))).astype(BF16)
            dup = (dact * sl).astype(BF16)
            dh_ref[:, j * c:(j + 1) * c] = dgate
            dh_ref[:, ff + j * c:ff + (j + 1) * c] = dup
            dxn = dxn + _dot_nt(dgate, win_v[j]) + _dot_nt(dup, win_v[j + 2])
        dg_ref[...] += jnp.sum(dxn * xh, axis=0, keepdims=True)
        dx_ref[...] = dyv + _rms_bwd(dxn, xh, r, gv)

    return _call(
        body, name=name, grid=(t // tm,),
        in_specs=[_row_spec(tm, d), _row_spec(tm, d), _row_spec(tm, 2 * ff), _const_spec((1, d)), HBM_SPEC, HBM_SPEC],
        out_specs=[_row_spec(tm, d), _row_spec(tm, 2 * ff), _row_spec(tm, ff), _const_spec((1, d))],
        out_shape=[jax.ShapeDtypeStruct((t, d), F32), jax.ShapeDtypeStruct((t, 2 * ff), BF16), jax.ShapeDtypeStruct((t, ff), BF16),
                   jax.ShapeDtypeStruct((1, d), F32)],
        scratch_shapes=[pltpu.VMEM((N_CHIPS, d, c), BF16), pltpu.VMEM((ff, d), BF16), pltpu.SemaphoreType.DMA((2 * N_CHIPS,))],
        args=(x, dy, h, g, win_g, wout_g), comms=comms)


def _norm_matmul(x, g, win_g, name, comms=()):
    t, d = x.shape
    c = win_g.shape[-1]
    tm = min(512, t)

    def body(x_ref, g_ref, win_hbm, o_ref, xn_ref, win_v, sem):
        _load_at_first_step(_in_weight_copies(win_hbm, win_v, sem))
        xh, _ = _rms_fwd(x_ref[...])
        xn = (xh * g_ref[...]).astype(BF16)
        xn_ref[...] = xn
        for q in range(N_CHIPS):
            o_ref[:, q * c:(q + 1) * c] = _dot(xn, win_v[q])

    return _call(
        body, name=name, grid=(t // tm,),
        in_specs=[_row_spec(tm, d), _const_spec((1, d)), HBM_SPEC],
        out_specs=[_row_spec(tm, N_CHIPS * c), _row_spec(tm, d)],
        out_shape=[jax.ShapeDtypeStruct((t, N_CHIPS * c), F32), jax.ShapeDtypeStruct((t, d), BF16)],
        scratch_shapes=[pltpu.VMEM((N_CHIPS, d, c), BF16), pltpu.SemaphoreType.DMA((N_CHIPS,))],
        args=(x, g, win_g), comms=comms)


def _proj_in_bwd_tail(dh, win_v, c):
    dxn = _dot_nt(dh[:, 0:c], win_v[0])
    for q in range(1, N_CHIPS):
        dxn = dxn + _dot_nt(dh[:, q * c:(q + 1) * c], win_v[q])
    return dxn


def _prev_halo_spec(tm, cols):
    return pl.BlockSpec((HALO, cols), lambda i: (jnp.maximum(i * (tm // HALO) - 1, 0), 0))


def _next_halo_spec(tm, cols, t):
    last = t // HALO - 1
    return pl.BlockSpec((HALO, cols), lambda i: (jnp.minimum((i + 1) * (tm // HALO), last), 0))


def _shift_down(w, k):
    return w if k == 0 else pltpu.roll(w, k, 0)


def _shift_up(w, k):
    return w if k == 0 else pltpu.roll(w, w.shape[0] - k, 0)


def _pool_counts(i, tm):
    pos = (i * tm + lax.broadcasted_iota(jnp.int32, (tm, POOL_CH), 0) + 1).astype(F32)
    lane = lax.broadcasted_iota(jnp.int32, (tm, POOL_CH), 1)
    win = jnp.where(lane < POOL_GC, 2.0, jnp.where(lane < 2 * POOL_GC, 4.0, jnp.where(lane < 3 * POOL_GC, 8.0, 16.0)))
    return jnp.minimum(pos, win)


def _group_select(parts):
    return jnp.concatenate([p[:, k * POOL_GC:(k + 1) * POOL_GC] for k, p in enumerate(parts)], axis=1)


def _mix0_recompute(i, tm, h_cur, h_prev, conv_w, conv_b, y=None):
    prev = jnp.where(i > 0, h_prev, 0.0)
    win = jnp.concatenate([prev, h_cur], axis=0)
    u_w = win[:, 0:POOL_CH]
    a_w = win[:, POOL_CH:POOL_CH + CONV_CH]
    gt_w = win[:, POOL_CH + CONV_CH:]
    g_w = a_w * _sigmoid(gt_w)
    if y is None:
        y = jnp.zeros((tm, CONV_CH), F32)
        for k in range(CONV_WIDTH):
            y = y + conv_w[k:k + 1, :] * _shift_down(g_w, CONV_WIDTH - 1 - k)[HALO:, :]
        y = y + conv_b
    s2 = u_w + _shift_down(u_w, 1)
    s4 = s2 + _shift_down(s2, 2)
    s8 = s4 + _shift_down(s4, 4)
    s16 = s8 + _shift_down(s8, 8)
    sums = _group_select([s2[HALO:], s4[HALO:], s8[HALO:], s16[HALO:]])
    cnt = _pool_counts(i, tm)
    pooled = sums / cnt - h_cur[:, 0:POOL_CH]
    return g_w, y, pooled, cnt


def _pool_linear(pooled, pw_ref, pb):
    return jnp.concatenate(
        [_dot(pooled[:, k * POOL_GC:(k + 1) * POOL_GC].astype(BF16), pw_ref[k].astype(BF16)) for k in range(len(POOL_WINDOWS))], axis=1) + pb


def _mix0_fwd(x, h0, pool_w, pool_b, pool_scale, conv_w, conv_b, ln_g, ln_b, wout_g, name, comms=()):
    t, d = x.shape
    tm = min(256, t)
    hc = h0.shape[1]

    def body(x_ref, h_ref, hp_ref, pw_ref, pb_ref, ps_ref, cw_ref, cb_ref, lg_ref, lb_ref, wout_hbm, xo_ref, ycat_ref, y_ref, wout_v, sem):
        _load_at_first_step(_out_weight_copies(wout_hbm, wout_v, sem))
        i = pl.program_id(0)
        _, y, pooled, _ = _mix0_recompute(i, tm, h_ref[...], hp_ref[...], cw_ref[...], cb_ref[...])
        y_ref[...] = y
        yhat, _ = _ln_fwd(y)
        yn = yhat * lg_ref[...] + lb_ref[...]
        yb = yn * _sigmoid(yn)
        ya = _pool_linear(pooled, pw_ref, pb_ref[...]) * ps_ref[...]
        ycat = jnp.concatenate([ya, yb], axis=1).astype(BF16)
        ycat_ref[...] = ycat
        xo_ref[...] = x_ref[...] + _dot(ycat, wout_v[...])

    return _call(
        body, name=name, grid=(t // tm,),
        in_specs=[_row_spec(tm, d), _row_spec(tm, hc), _prev_halo_spec(tm, hc), _const_spec(pool_w.shape), _const_spec((1, POOL_CH)),
                  _const_spec((1, POOL_CH)), _const_spec(conv_w.shape), _const_spec((1, CONV_CH)), _const_spec((1, CONV_CH)),
                  _const_spec((1, CONV_CH)), HBM_SPEC],
        out_specs=[_row_spec(tm, d), _row_spec(tm, d), _row_spec(tm, CONV_CH)],
        out_shape=[jax.ShapeDtypeStruct((t, d), F32), jax.ShapeDtypeStruct((t, d), BF16), jax.ShapeDtypeStruct((t, CONV_CH), F32)],
        scratch_shapes=[pltpu.VMEM((d, d), BF16), pltpu.SemaphoreType.DMA((N_CHIPS,))],
        args=(x, h0, h0, pool_w, pool_b, pool_scale, conv_w, conv_b, ln_g, ln_b, wout_g), comms=comms)


def _mix0_bwd_a(dy, h0, y_conv, pool_w, pool_b, pool_scale, conv_w, conv_b, ln_g, ln_b, wout_g, name):
    t, d = dy.shape
    tm = min(256, t)
    hc = h0.shape[1]
    n_small = 40

    def body(dy_ref, h_ref, hp_ref, y_ref, pw_ref, pb_ref, ps_ref, cw_ref, cb_ref, lg_ref, lb_ref, wout_hbm,
             dconv_ref, dpc_ref, dpw_ref, small_ref, wout_v, sem):
        _load_at_first_step(_out_weight_copies(wout_hbm, wout_v, sem))
        i = pl.program_id(0)

        @pl.when(i == 0)
        def _():
            dpw_ref[...] = jnp.zeros_like(dpw_ref)
            small_ref[...] = jnp.zeros_like(small_ref)

        g_w, y, pooled, cnt = _mix0_recompute(i, tm, h_ref[...], hp_ref[...], cw_ref[...], cb_ref[...], y_ref[...])
        yhat, rs = _ln_fwd(y)
        lg = lg_ref[...]
        yn = yhat * lg + lb_ref[...]
        mixed = _pool_linear(pooled, pw_ref, pb_ref[...])
        dycat = _dot_nt(dy_ref[...].astype(BF16), wout_v[...])
        dya, dyb = dycat[:, 0:POOL_CH], dycat[:, POOL_CH:]
        sg = _sigmoid(yn)
        dyn = dyb * (sg * (1.0 + yn * (1.0 - sg)))
        dyc = _ln_bwd(dyn * lg, yhat, rs)
        dconv_ref[...] = dyc

        def add_row(k, value):
            small_ref[k:k + 1, :] += jnp.sum(value, axis=0, keepdims=True)

        for k in range(CONV_WIDTH):
            add_row(k, dyc * _shift_down(g_w, CONV_WIDTH - 1 - k)[HALO:, :])
        add_row(32, dyc)
        add_row(33, dyn * yhat)
        add_row(34, dyn)
        scale = ps_ref[...]
        dmixed = dya * scale
        add_row(35, dya * mixed)
        add_row(36, dmixed)
        dmb = dmixed.astype(BF16)
        dpooled = []
        for k in range(len(POOL_WINDOWS)):
            sl = slice(k * POOL_GC, (k + 1) * POOL_GC)
            dpw_ref[k] += _dot_tn(pooled[:, sl].astype(BF16), dmb[:, sl])
            dpooled.append(_dot_nt(dmb[:, sl], pw_ref[k].astype(BF16)))
        dpc_ref[...] = jnp.concatenate(dpooled, axis=1) / cnt

    return _call(
        body, name=name, grid=(t // tm,),
        in_specs=[_row_spec(tm, d), _row_spec(tm, hc), _prev_halo_spec(tm, hc), _row_spec(tm, CONV_CH), _const_spec(pool_w.shape),
                  _const_spec((1, POOL_CH)), _const_spec((1, POOL_CH)), _const_spec(conv_w.shape), _const_spec((1, CONV_CH)),
                  _const_spec((1, CONV_CH)), _const_spec((1, CONV_CH)), HBM_SPEC],
        out_specs=[_row_spec(tm, CONV_CH), _row_spec(tm, POOL_CH), _const_spec(pool_w.shape), _const_spec((n_small, CONV_CH))],
        out_shape=[jax.ShapeDtypeStruct((t, CONV_CH), F32), jax.ShapeDtypeStruct((t, POOL_CH), F32),
                   jax.ShapeDtypeStruct(pool_w.shape, F32), jax.ShapeDtypeStruct((n_small, CONV_CH), F32)],
        scratch_shapes=[pltpu.VMEM((d, d), BF16), pltpu.SemaphoreType.DMA((N_CHIPS,))],
        args=(dy, h0, h0, y_conv, pool_w, pool_b, pool_scale, conv_w, conv_b, ln_g, ln_b, wout_g))


def _mix0_bwd_b(x, dy, h0, dconv, dpc, g, conv_w, win_g, name):
    t, d = x.shape
    tm = min(256, t)
    hc = h0.shape[1]
    c = win_g.shape[-1]
    n_tiles = t // tm

    def body(x_ref, dy_ref, h_ref, dc_ref, dcn_ref, dp_ref, dpn_ref, g_ref, cw_ref, win_hbm, dx_ref, dh_ref, dg_ref, win_v, sem):
        _load_at_first_step(_in_weight_copies(win_hbm, win_v, sem))
        i = pl.program_id(0)

        @pl.when(i == 0)
        def _():
            dg_ref[...] = jnp.zeros_like(dg_ref)

        not_last = i < n_tiles - 1
        dc_w = jnp.concatenate([dc_ref[...], jnp.where(not_last, dcn_ref[...], 0.0)], axis=0)
        dp_w = jnp.concatenate([dp_ref[...], jnp.where(not_last, dpn_ref[...], 0.0)], axis=0)
        cw = cw_ref[...]
        dg = jnp.zeros((tm, CONV_CH), F32)
        for k in range(CONV_WIDTH):
            dg = dg + cw[k:k + 1, :] * _shift_up(dc_w, CONV_WIDTH - 1 - k)[0:tm, :]
        a2 = dp_w + _shift_up(dp_w, 1)
        a4 = a2 + _shift_up(a2, 2)
        a8 = a4 + _shift_up(a4, 4)
        a16 = a8 + _shift_up(a8, 8)
        back = _group_select([a2[0:tm], a4[0:tm], a8[0:tm], a16[0:tm]])
        du = back - dp_ref[...] * _pool_counts(i, tm)
        hv = h_ref[...]
        a = hv[:, POOL_CH:POOL_CH + CONV_CH]
        sig = _sigmoid(hv[:, POOL_CH + CONV_CH:])
        dh = jnp.concatenate([du, dg * sig, dg * a * sig * (1.0 - sig)], axis=1).astype(BF16)
        dh_ref[...] = dh
        dxn = _proj_in_bwd_tail(dh, win_v, c)
        xh, r = _rms_fwd(x_ref[...])
        dg_ref[...] += jnp.sum(dxn * xh, axis=0, keepdims=True)
        dx_ref[...] = dy_ref[...] + _rms_bwd(dxn, xh, r, g_ref[...])

    return _call(
        body, name=name, grid=(n_tiles,),
        in_specs=[_row_spec(tm, d), _row_spec(tm, d), _row_spec(tm, hc), _row_spec(tm, CONV_CH), _next_halo_spec(tm, CONV_CH, t),
                  _row_spec(tm, POOL_CH), _next_halo_spec(tm, POOL_CH, t), _const_spec((1, d)), _const_spec(conv_w.shape), HBM_SPEC],
        out_specs=[_row_spec(tm, d), _row_spec(tm, hc), _const_spec((1, d))],
        out_shape=[jax.ShapeDtypeStruct((t, d), F32), jax.ShapeDtypeStruct((t, hc), BF16), jax.ShapeDtypeStruct((1, d), F32)],
        scratch_shapes=[pltpu.VMEM((N_CHIPS, d, c), BF16), pltpu.SemaphoreType.DMA((N_CHIPS,))],
        args=(x, dy, h0, dconv, dconv, dpc, dpc, g, conv_w, win_g))


SQRT_HALF = 0.7071067811865476
INV_SQRT_2PI = 0.3989422804014327


def _causal_mask():
    return (lax.broadcasted_iota(jnp.int32, (CHUNK, CHUNK), 1) <= lax.broadcasted_iota(jnp.int32, (CHUNK, CHUNK), 0)).astype(F32)


def _sgu_recompute(pre, lg, lb):
    half = pre.shape[1] // 2
    phi = 0.5 * (1.0 + lax.erf(pre * SQRT_HALF))
    z = pre * phi
    u, v = z[:, 0:half], z[:, half:]
    vhat, rs = _ln_fwd(v)
    return u, vhat, rs, vhat * lg + lb, phi


def _sgu_spatial(vln, w_ref, bt, tm):
    mask = _causal_mask()
    wm = [(w_ref[hd] * mask).astype(BF16) for hd in range(SGU_HEADS)]
    vb = vln.astype(BF16)
    rows = []
    for ch in range(tm // CHUNK):
        blocks = [_dot(wm[hd], vb[ch * CHUNK:(ch + 1) * CHUNK, hd * CHUNK:(hd + 1) * CHUNK]) + bt[:, hd:hd + 1] for hd in range(SGU_HEADS)]
        rows.append(jnp.concatenate(blocks, axis=1))
    return jnp.concatenate(rows, axis=0), wm


def _sgu_fwd(x, pre, ln_g, ln_b, w, bt, wout_g, name):
    t, d = x.shape
    tm = min(256, t)
    pc = pre.shape[1]

    def body(x_ref, pre_ref, lg_ref, lb_ref, w_ref, bt_ref, wout_hbm, xo_ref, p_ref, wout_v, sem):
        _load_at_first_step(_out_weight_copies(wout_hbm, wout_v, sem))
        u, _, _, vln, _ = _sgu_recompute(pre_ref[...], lg_ref[...], lb_ref[...])
        vo, _ = _sgu_spatial(vln, w_ref, bt_ref[...], tm)
        p = (u * vo).astype(BF16)
        p_ref[...] = p
        xo_ref[...] = x_ref[...] + _dot(p, wout_v[...])

    return _call(
        body, name=name, grid=(t // tm,),
        in_specs=[_row_spec(tm, d), _row_spec(tm, pc), _const_spec((1, d)), _const_spec((1, d)), _const_spec(w.shape),
                  _const_spec(bt.shape), HBM_SPEC],
        out_specs=[_row_spec(tm, d), _row_spec(tm, d)],
        out_shape=[jax.ShapeDtypeStruct((t, d), F32), jax.ShapeDtypeStruct((t, d), BF16)],
        scratch_shapes=[pltpu.VMEM((d, d), BF16), pltpu.SemaphoreType.DMA((N_CHIPS,))],
        args=(x, pre, ln_g, ln_b, w, bt, wout_g))


def _sgu_bwd(x, dy, pre, g, ln_g, ln_b, w, bt, win_g, wout_g, name, comms=()):
    t, d = x.shape
    tm = min(256, t)
    pc = pre.shape[1]
    c = win_g.shape[-1]

    def body(x_ref, dy_ref, pre_ref, g_ref, lg_ref, lb_ref, w_ref, bt_ref, win_hbm, wout_hbm,
             dx_ref, dpre_ref, dg_ref, dlg_ref, dlb_ref, dw_ref, dbt_ref, win_v, wout_v, sem):
        _load_at_first_step(_in_weight_copies(win_hbm, win_v, sem) + _out_weight_copies(wout_hbm, wout_v, sem, N_CHIPS))
        i = pl.program_id(0)

        @pl.when(i == 0)
        def _():
            for ref in (dg_ref, dlg_ref, dlb_ref, dw_ref, dbt_ref):
                ref[...] = jnp.zeros_like(ref)

        prev = pre_ref[...]
        lg = lg_ref[...]
        u, vhat, rs, vln, phi = _sgu_recompute(prev, lg, lb_ref[...])
        vo, wm = _sgu_spatial(vln, w_ref, bt_ref[...], tm)
        dp = _dot_nt(dy_ref[...].astype(BF16), wout_v[...])
        du = dp * vo
        dvo = dp * u
        dvob = dvo.astype(BF16)
        vb = vln.astype(BF16)
        head_lane = lax.broadcasted_iota(jnp.int32, (CHUNK, SGU_HEADS), 1)
        dbt = jnp.zeros((CHUNK, SGU_HEADS), F32)
        dw = [jnp.zeros((CHUNK, CHUNK), F32) for _ in range(SGU_HEADS)]
        rows = []
        for ch in range(tm // CHUNK):
            rs_ = slice(ch * CHUNK, (ch + 1) * CHUNK)
            blocks = []
            for hd in range(SGU_HEADS):
                cs = slice(hd * CHUNK, (hd + 1) * CHUNK)
                dbt = dbt + jnp.where(head_lane == hd, jnp.sum(dvo[rs_, cs], axis=1, keepdims=True), 0.0)
                dw[hd] = dw[hd] + _dot_nt(dvob[rs_, cs], vb[rs_, cs])
                blocks.append(_dot_tn(wm[hd], dvob[rs_, cs]))
            rows.append(jnp.concatenate(blocks, axis=1))
        dvln = jnp.concatenate(rows, axis=0)
        mask = _causal_mask()
        for hd in range(SGU_HEADS):
            dw_ref[hd] += dw[hd] * mask
        dbt_ref[...] += dbt
        dlg_ref[...] += jnp.sum(dvln * vhat, axis=0, keepdims=True)
        dlb_ref[...] += jnp.sum(dvln, axis=0, keepdims=True)
        dv = _ln_bwd(dvln * lg, vhat, rs)
        gelu_grad = phi + prev * jnp.exp(-0.5 * prev * prev) * INV_SQRT_2PI
        dpre = (jnp.concatenate([du, dv], axis=1) * gelu_grad).astype(BF16)
        dpre_ref[...] = dpre
        dxn = _proj_in_bwd_tail(dpre, win_v, c)
        xh, r = _rms_fwd(x_ref[...])
        dg_ref[...] += jnp.sum(dxn * xh, axis=0, keepdims=True)
        dx_ref[...] = dy_ref[...] + _rms_bwd(dxn, xh, r, g_ref[...])

    return _call(
        body, name=name, grid=(t // tm,),
        in_specs=[_row_spec(tm, d), _row_spec(tm, d), _row_spec(tm, pc), _const_spec((1, d)), _const_spec((1, d)), _const_spec((1, d)),
                  _const_spec(w.shape), _const_spec(bt.shape), HBM_SPEC, HBM_SPEC],
        out_specs=[_row_spec(tm, d), _row_spec(tm, pc), _const_spec((1, d)), _const_spec((1, d)), _const_spec((1, d)),
                   _const_spec(w.shape), _const_spec(bt.shape)],
        out_shape=[jax.ShapeDtypeStruct((t, d), F32), jax.ShapeDtypeStruct((t, pc), BF16), jax.ShapeDtypeStruct((1, d), F32),
                   jax.ShapeDtypeStruct((1, d), F32), jax.ShapeDtypeStruct((1, d), F32), jax.ShapeDtypeStruct(w.shape, F32),
                   jax.ShapeDtypeStruct(bt.shape, F32)],
        scratch_shapes=[pltpu.VMEM((N_CHIPS, d, c), BF16), pltpu.VMEM((d, d), BF16), pltpu.SemaphoreType.DMA((2 * N_CHIPS,))],
        args=(x, dy, pre, g, ln_g, ln_b, w, bt, win_g, wout_g), comms=comms)


def _final_loss(x, tgt, g, name):
    t, d = x.shape
    tm = min(512, t)

    def body(x_ref, t_ref, g_ref, dx_ref, loss_ref, dg_ref):
        @pl.when(pl.program_id(0) == 0)
        def _():
            loss_ref[...] = jnp.zeros_like(loss_ref)
            dg_ref[...] = jnp.zeros_like(dg_ref)

        gv = g_ref[...]
        xh, r = _rms_fwd(x_ref[...])
        diff = xh * gv - t_ref[...]
        loss_ref[...] += 0.5 * jnp.sum(jnp.sum(diff * diff, axis=1, keepdims=True), axis=0, keepdims=True) / d
        dout = diff / d
        dg_ref[...] += jnp.sum(dout * xh, axis=0, keepdims=True)
        dx_ref[...] = _rms_bwd(dout, xh, r, gv)

    return _call(
        body, name=name, grid=(t // tm,),
        in_specs=[_row_spec(tm, d), _row_spec(tm, d), _const_spec((1, d))],
        out_specs=[_row_spec(tm, d), _const_spec((1, 1)), _const_spec((1, d))],
        out_shape=[jax.ShapeDtypeStruct((t, d), F32), jax.ShapeDtypeStruct((1, 1), F32), jax.ShapeDtypeStruct((1, d), F32)],
        args=(x, tgt, g))


def _tn_matmul(a, b, scale, bm, bn, name, comms=()):
    t, m = a.shape
    n = b.shape[1]
    tk = min(512, t)
    bm, bn = min(bm, m), min(bn, n)
    nk = t // tk

    def body(a_ref, b_ref, o_ref, acc_ref):
        k = pl.program_id(2)

        @pl.when(k == 0)
        def _():
            acc_ref[...] = jnp.zeros_like(acc_ref)

        bv = b_ref[...]
        if bv.dtype != BF16:
            bv = (scale * bv).astype(BF16)
        acc_ref[...] += _dot_tn(a_ref[...], bv)

        @pl.when(k == nk - 1)
        def _():
            o_ref[...] = acc_ref[...].astype(BF16)

    return _call(
        body, name=name, grid=(m // bm, n // bn, nk),
        in_specs=[pl.BlockSpec((tk, bm), lambda i, j, k: (k, i)), pl.BlockSpec((tk, bn), lambda i, j, k: (k, j))],
        out_specs=[pl.BlockSpec((bm, bn), lambda i, j, k: (i, j))],
        out_shape=[jax.ShapeDtypeStruct((m, n), BF16)],
        scratch_shapes=[pltpu.VMEM((bm, bn), F32)],
        args=(a, b), comms=comms)[0]


def _row_tile(rows, cols, budget_bytes=2 * 1024 * 1024):
    best = None
    for cand in range(16, rows + 1, 16):
        if rows % cand == 0 and cand * cols * 4 <= budget_bytes:
            best = cand
    return best or rows


def _scalar_grid(grid, in_specs, out_specs):
    return pltpu.PrefetchScalarGridSpec(num_scalar_prefetch=1, grid=grid, in_specs=in_specs, out_specs=out_specs)


def _cast_into_slot(w, layer, me, name):
    _, rows, cols = w.shape
    tr = _row_tile(rows, cols)

    def body(me_ref, w_ref, o_ref):
        o_ref[...] = w_ref[...].astype(BF16)

    return pl.pallas_call(
        body, name=name,
        grid_spec=_scalar_grid((rows // tr,), [pl.BlockSpec((None, tr, cols), lambda i, me: (layer, i, 0))],
                               pl.BlockSpec((None, tr, cols), lambda i, me: (me[0], i, 0))),
        out_shape=jax.ShapeDtypeStruct((N_CHIPS, rows, cols), BF16), compiler_params=_cparams())(me, w)


def _add_half(view, other, core, name):
    q, _, r, c = view.shape
    tr = _row_tile(r, c)

    def body(core_ref, a_ref, b_ref, o_ref):
        o_ref[...] = (a_ref[...].astype(F32) + b_ref[...].astype(F32)).astype(BF16)

    return pl.pallas_call(
        body, name=name,
        grid_spec=_scalar_grid((q, r // tr), [pl.BlockSpec((None, None, tr, c), lambda k, i, core: (k, core[0], i, 0)),
                                             pl.BlockSpec((None, tr, c), lambda k, i, core: (k, i, 0))],
                               pl.BlockSpec((None, tr, c), lambda k, i, core: (k, i, 0))),
        out_shape=jax.ShapeDtypeStruct((q, r, c), BF16), compiler_params=_cparams_nd(2))(core, view, other)


def _reduce_piece(partial, staged, me, column_sharded, name):
    _, r, c = staged.shape
    tr = _row_tile(r, c, budget_bytes=1024 * 1024)
    nt = r // tr
    if column_sharded:
        own2d = partial.reshape(r, N_CHIPS * c)
        own_spec = pl.BlockSpec((tr, c), lambda i, me: (i, me[0]))
    else:
        own2d = partial.reshape(N_CHIPS * r, c)
        own_spec = pl.BlockSpec((tr, c), lambda i, me: (me[0] * nt + i, 0))
    ring = [pl.BlockSpec((None, tr, c), lambda i, me, k=k: ((me[0] + k) % N_CHIPS, i, 0)) for k in (1, 2, 3)]

    def body(me_ref, own_ref, s1_ref, s2_ref, s3_ref, o_ref):
        o_ref[...] = ((own_ref[...].astype(F32) + s1_ref[...].astype(F32)) + s2_ref[...].astype(F32)) + s3_ref[...].astype(F32)

    return pl.pallas_call(
        body, name=name, grid_spec=_scalar_grid((nt,), [own_spec] + ring, pl.BlockSpec((tr, c), lambda i, me: (i, 0))),
        out_shape=jax.ShapeDtypeStruct((r, c), F32), compiler_params=_cparams())(me, own2d, staged, staged, staged)


def _sum_leading(s, name):
    n, rows, cols = s.shape
    tr = _row_tile(rows, cols, budget_bytes=1024 * 1024)

    def body(s_ref, o_ref):
        acc = s_ref[0].astype(F32)
        for k in range(1, n):
            acc = acc + s_ref[k].astype(F32)
        o_ref[...] = acc

    return pl.pallas_call(
        body, name=name, grid=(rows // tr,), in_specs=[pl.BlockSpec((n, tr, cols), lambda i: (0, i, 0))], out_specs=_row_spec(tr, cols),
        out_shape=jax.ShapeDtypeStruct((rows, cols), F32), compiler_params=_cparams())(s)


ADAM_C1 = 1.0 / (1.0 - ADAM_B1 ** ADAM_STEP)
ADAM_C2 = 1.0 / (1.0 - ADAM_B2 ** ADAM_STEP)


def _adamw_math(w, g, m, v):
    mn = ADAM_B1 * m + (1.0 - ADAM_B1) * g
    vn = ADAM_B2 * v + (1.0 - ADAM_B2) * (g * g)
    return -ADAM_LR * ((mn * ADAM_C1) / (jnp.sqrt(vn * ADAM_C2) + ADAM_EPS) + ADAM_WD * w), mn, vn


def _adamw(w, g, m, v, name):
    shape = w.shape
    cols = shape[-1] if w.ndim > 1 else 128
    w2, g2, m2, v2 = (a.reshape(-1, cols) for a in (w, g, m, v))
    rows = w2.shape[0]
    tr = _row_tile(rows, cols, budget_bytes=1024 * 1024)

    def body(w_ref, g_ref, m_ref, v_ref, d_ref, mo_ref, vo_ref):
        d_ref[...], mo_ref[...], vo_ref[...] = _adamw_math(w_ref[...], g_ref[...], m_ref[...], v_ref[...])

    spec = _row_spec(tr, cols)
    outs = pl.pallas_call(
        body, name=name, grid=(rows // tr,), in_specs=[spec] * 4, out_specs=[spec] * 3,
        out_shape=[jax.ShapeDtypeStruct((rows, cols), F32)] * 3, compiler_params=_cparams())(w2, g2, m2, v2)
    return tuple(o.reshape(shape) for o in outs)


def _adamw_sharded(w, g_mine, g_sibling, m, v, core, layer, prev, name, comms=()):
    n_layers, r, c = w.shape
    half = r // 2
    tr = _row_tile(half, c)
    nt = half // tr

    def body(core_ref, w_ref, gm_ref, gs_ref, m_ref, v_ref, *rest):
        g_ref, d_ref, mo_ref, vo_ref = rest[-4:]
        gv = jnp.where(pl.program_id(0) == core_ref[0], gm_ref[...], gs_ref[...])
        g_ref[...] = gv
        d_ref[...], mo_ref[...], vo_ref[...] = _adamw_math(w_ref[...], gv, m_ref[...], v_ref[...])

    full = pl.BlockSpec((None, tr, c), lambda h, i, core: (layer, h * nt + i, 0))
    part = pl.BlockSpec((tr, c), lambda h, i, core: (i, 0))
    args = [w, g_mine, g_sibling, m, v]
    in_specs = [full, part, part, full, full]
    aliases = {}
    if prev is not None:
        aliases = {len(args) + k: k for k in range(4)}
        args += list(prev)
        in_specs += [pl.BlockSpec(memory_space=pl.ANY)] * 4
    return _call(body, name=name, grid=(2, nt), in_specs=in_specs, out_specs=[full] * 4, out_shape=[jax.ShapeDtypeStruct(w.shape, F32)] * 4,
                 args=args, comms=comms, scalar=core, aliases=aliases)


BIG_IN = ("ffn1_w_in", "ffn2_w_in", "ab_w_in", "sgu_w_in")
BIG_OUT = ("ffn1_w_out", "ffn2_w_out", "ab_w_out", "sgu_w_out")
BIG = BIG_IN + BIG_OUT


class _Gatherer:
    def __init__(self, slots):
        self.slots = dict(slots)

    def _stage(self, keys, d2d):
        n = len(keys)

        def plan(ins, outs, place):
            x, y, c = place
            me = 2 * x + y
            remote = []
            for a in range(n):
                rows = ins[a].shape[1] // 2

                def half(ref, q, core, rows=rows):
                    return ref.at[q, pl.ds(core * rows, rows), :]

                for (px, py) in _other_chips(x, y):
                    q = 2 * px + py
                    if d2d:
                        remote.append((half(ins[a], q, c), half(outs[a], q, c), (x, y, 1 - c), half(outs[a], q, 1 - c)))
                    else:
                        remote.append((half(ins[a], me, c), half(outs[a], me, c), (px, py, c), half(outs[a], q, c)))
            return remote

        def finish(outs):
            for k, o in zip(keys, outs):
                self.slots[k] = o

        arrays = [self.slots[k] for k in keys]
        return _Exchange(arrays, [_sds(a) for a in arrays], plan, 3 * n, {a: a for a in range(n)}, finish)

    def ici(self, keys):
        return self._stage(keys, False)

    def d2d(self, keys):
        return self._stage(keys, True)


class _Reducer:
    def __init__(self, me, core):
        self.me, self.core = me, core
        self.views, self.partial, self.staged, self.mine, self.theirs = {}, {}, {}, {}, {}

    def add(self, key, g):
        m, n = g.shape
        if key[0] in BIG_IN:
            self.views[key] = g.reshape(1, 2, m // 2, n)
        else:
            self.views[key] = g.reshape(N_CHIPS, 2, m // (2 * N_CHIPS), n)

    def swap(self, keys):
        views = [self.views[k] for k in keys]

        def plan(ins, outs, place):
            x, y, c = place
            return [(ins[a].at[:, 1 - c], outs[a], (x, y, 1 - c), outs[a]) for a in range(len(keys))]

        def finish(outs):
            for k, v, o in zip(keys, views, outs):
                self.partial[k] = _add_half(v, o, self.core, f"chip_partial_{k[0]}_{k[1]}")

        shapes = [jax.ShapeDtypeStruct((v.shape[0],) + v.shape[2:], v.dtype) for v in views]
        return _Exchange(views, shapes, plan, len(keys), None, finish)

    def scatter(self, keys, part=(0, 1)):
        i, n = part
        n_keys = len(keys)
        parts = [self.partial[k] for k in keys]
        shapes = []
        for k, p in zip(keys, parts):
            q, r, c = p.shape
            shapes.append(jax.ShapeDtypeStruct((N_CHIPS, r, c // N_CHIPS if k[0] in BIG_IN else c), p.dtype))

        def piece(ref, key, q, rows, cols):
            return ref.at[0, rows, pl.ds(q * cols, cols)] if key[0] in BIG_IN else ref.at[q, rows, :]

        def plan(ins, outs, place):
            x, y, c = place
            me = 2 * x + y
            remote = []
            for a, k in enumerate(keys):
                _, r, cols = shapes[a].shape
                rows = pl.ds(i * (r // n), r // n)
                for (px, py) in _other_chips(x, y):
                    q = 2 * px + py
                    remote.append((piece(ins[a], k, q, rows, cols), outs[a].at[me, rows, :], (px, py, c), outs[a].at[q, rows, :]))
            return remote

        def finish(outs):
            for k, p, o in zip(keys, parts, outs):
                self.staged[k] = o
                if i == n - 1:
                    self.mine[k] = _reduce_piece(p, o, self.me, k[0] in BIG_IN, f"reduce_{k[0]}_{k[1]}")

        inputs, aliases = parts, None
        if i > 0:
            inputs = parts + [self.staged[k] for k in keys]
            aliases = {n_keys + a: a for a in range(n_keys)}
        return _Exchange(inputs, shapes, plan, 3 * n_keys, aliases, finish)

    def exchange(self, keys):
        mine = [self.mine[k] for k in keys]

        def plan(ins, outs, place):
            x, y, c = place
            return [(ins[a], outs[a], (x, y, 1 - c), outs[a]) for a in range(len(keys))]

        def finish(outs):
            for k, o in zip(keys, outs):
                self.theirs[k] = o

        return _Exchange(mine, [_sds(a) for a in mine], plan, len(keys), None, finish)


def _all_gather_full(gat, keys):
    n = len(keys)
    arrays = [gat.slots[k] for k in keys]

    def body(*refs):
        ins, outs = refs[:n], refs[n:2 * n]
        send_sems, recv_sems = refs[2 * n:]
        x, y, c = _my_place()
        sibling = (x, y, 1 - c)
        chips = _other_chips(x, y)
        me = 2 * x + y

        def half(ref, q, core):
            rows = ref.shape[1] // 2
            return ref.at[q, pl.ds(core * rows, rows), :]

        def copy(a, k, src, dst, to):
            return pltpu.make_async_remote_copy(src_ref=src, dst_ref=dst, send_sem=send_sems.at[6 * a + k], recv_sem=recv_sems.at[6 * a + k],
                                                device_id=to, device_id_type=MESH)

        first = [copy(a, j, half(ins[a], me, c), half(outs[a], me, c), (*chip, c)) for a in range(n) for j, chip in enumerate(chips)]
        for cp in first:
            cp.start()
        passed = []
        for a in range(n):
            for j, (px, py) in enumerate(chips):
                landed = half(outs[a], 2 * px + py, c)
                copy(a, j, landed, landed, (px, py, c)).wait_recv()
                fwd = copy(a, 3 + j, landed, landed, sibling)
                fwd.start()
                passed.append(fwd)
        for a in range(n):
            for j, (px, py) in enumerate(chips):
                other = half(outs[a], 2 * px + py, 1 - c)
                copy(a, 3 + j, other, other, sibling).wait_recv()
        for cp in first + passed:
            cp.wait_send()

    outs = pl.pallas_call(
        body, name="all_gather_first_weights", in_specs=[HBM_SPEC] * n, out_specs=[HBM_SPEC] * n,
        out_shape=[_sds(a) for a in arrays], input_output_aliases={a: a for a in range(n)},
        scratch_shapes=[pltpu.SemaphoreType.DMA((6 * n,)), pltpu.SemaphoreType.DMA((6 * n,))])(*arrays)
    for k, o in zip(keys, outs):
        gat.slots[k] = o


def _small_all_gather(buf, done):
    peers = [(0, 0, 1), (1, 0, 0), (0, 1, 0), (1, 1, 0), (1, 0, 1), (0, 1, 1), (1, 1, 1)]

    def index(x, y, c):
        return 4 * x + 2 * y + c

    def plan(ins, outs, place):
        x, y, c = place
        remote = []
        for fx, fy, fc in peers:
            px, py, pc = (1 - x if fx else x), (1 - y if fy else y), (1 - c if fc else c)
            remote.append((ins[0], outs[0].at[index(x, y, c)], (px, py, pc), outs[0].at[index(px, py, pc)]))
        return remote

    def local(ins, outs, place):
        return [(ins[0], outs[0].at[index(*place)])]

    return _Exchange([buf], [jax.ShapeDtypeStruct((2 * N_CHIPS,) + buf.shape, buf.dtype)], plan, len(peers), None,
                     lambda outs: done(outs[0]), local, 1)


WEIGHT_NAMES = ("ffn1_norm", "ffn1_w_in", "ffn1_w_out", "mix_norm", "ffn2_norm", "ffn2_w_in", "ffn2_w_out", "ab_w_in", "pool_w", "pool_b",
                "pool_scale", "conv_w", "conv_b", "conv_ln_g", "conv_ln_b", "ab_w_out", "sgu_w_in", "sgu_ln_g", "sgu_ln_b", "sgu_w", "sgu_b",
                "sgu_w_out", "final_norm")
SMALL = tuple(n for n in WEIGHT_NAMES if n not in BIG)
SHARDED_SMALL = ("conv_w", "sgu_ln_g", "sgu_ln_b")
PACK_ROWS = 48
PACK = ("pack", 0)


def _pair(prefix, layer):
    return [(prefix + "_w_in", layer), (prefix + "_w_out", layer)]


def kernel(x, ffn1_norm, ffn1_w_in, ffn1_w_out, mix_norm, ffn2_norm, ffn2_w_in, ffn2_w_out, ab_w_in, pool_w, pool_b, pool_scale, conv_w, conv_b, conv_ln_g, conv_ln_b, ab_w_out, sgu_w_in, sgu_ln_g, sgu_ln_b, sgu_w, sgu_b, sgu_w_out, final_norm, loss_target, m_ffn1_norm, m_ffn1_w_in, m_ffn1_w_out, m_mix_norm, m_ffn2_norm, m_ffn2_w_in, m_ffn2_w_out, m_ab_w_in, m_pool_w, m_pool_b, m_pool_scale, m_conv_w, m_conv_b, m_conv_ln_g, m_conv_ln_b, m_ab_w_out, m_sgu_w_in, m_sgu_ln_g, m_sgu_ln_b, m_sgu_w, m_sgu_b, m_sgu_w_out, m_final_norm, v_ffn1_norm, v_ffn1_w_in, v_ffn1_w_out, v_mix_norm, v_ffn2_norm, v_ffn2_w_in, v_ffn2_w_out, v_ab_w_in, v_pool_w, v_pool_b, v_pool_scale, v_conv_w, v_conv_b, v_conv_ln_g, v_conv_ln_b, v_ab_w_out, v_sgu_w_in, v_sgu_ln_g, v_sgu_ln_b, v_sgu_w, v_sgu_b, v_sgu_w_out, v_final_norm):
    given = dict(locals())
    w = {n: given[n] for n in WEIGHT_NAMES}
    chip = 2 * lax.axis_index("x") + lax.axis_index("y")
    me = chip.astype(jnp.int32).reshape(1)
    core = lax.axis_index("c").astype(jnp.int32).reshape(1)
    row = lambda v: v.reshape(1, -1)
    xin, tgt = x[0], loss_target[0]

    pack = jnp.concatenate([
        w["conv_w"][0], jnp.zeros((1, 128), F32), w["sgu_ln_g"].reshape(2, 128), w["sgu_ln_b"].reshape(2, 128),
        jnp.zeros((PACK_ROWS - 36, 128), F32)], axis=0)
    slots = {PACK: lax.dynamic_update_slice(jnp.zeros((N_CHIPS, PACK_ROWS, 128), F32), pack[None], (me[0], 0, 0))}
    for n in BIG:
        for layer in range(w[n].shape[0]):
            slots[(n, layer)] = _cast_into_slot(w[n], layer, me, f"cast_{n}_{layer}")
    gat = _Gatherer(slots)
    _all_gather_full(gat, _pair("ffn1", 0) + _pair("ab", 0) + [PACK])
    gp = gat.slots[PACK]
    conv_w_full = jnp.transpose(gp[:, 0:CONV_WIDTH], (1, 0, 2)).reshape(CONV_WIDTH, N_CHIPS * 128)
    sgu_ln_g_full = gp[:, 32:34].reshape(1, -1)
    sgu_ln_b_full = gp[:, 34:36].reshape(1, -1)
    gw = lambda n, layer: gat.slots[(n, layer)]

    st = [dict(), dict()]
    st[0]["xa"] = xin
    later = _pair("sgu", 0) + _pair("ffn2", 1)
    cur, st[0]["h1"], st[0]["xn1"] = _ffn_fwd(xin, row(w["ffn1_norm"][0]), gw("ffn1_w_in", 0), gw("ffn1_w_out", 0), "ffn1_fwd_0",
                                              comms=[gat.ici(_pair("ffn2", 0))])
    st[0]["xb"] = cur
    st[0]["h0"], st[0]["xnm"] = _norm_matmul(cur, row(w["mix_norm"][0]), gw("ab_w_in", 0), "mix0_proj_in",
                                             comms=[gat.d2d(_pair("ffn2", 0))])
    pool_args = (w["pool_w"][0], row(w["pool_b"][0]), row(w["pool_scale"][0]), conv_w_full, row(w["conv_b"][0]), row(w["conv_ln_g"][0]),
                 row(w["conv_ln_b"][0]), gw("ab_w_out", 0))
    cur, st[0]["ycat"], st[0]["yconv"] = _mix0_fwd(cur, st[0]["h0"], *pool_args, "mix0_fwd", comms=[gat.ici(_pair("ffn1", 1))])
    st[0]["xc"] = cur
    cur, st[0]["h2"], st[0]["xn2"] = _ffn_fwd(cur, row(w["ffn2_norm"][0]), gw("ffn2_w_in", 0), gw("ffn2_w_out", 0), "ffn2_fwd_0",
                                              comms=[gat.d2d(_pair("ffn1", 1)), gat.ici(later)])
    st[1]["xa"] = cur
    cur, st[1]["h1"], st[1]["xn1"] = _ffn_fwd(cur, row(w["ffn1_norm"][1]), gw("ffn1_w_in", 1), gw("ffn1_w_out", 1), "ffn1_fwd_1",
                                              comms=[gat.d2d(later)])
    st[1]["xb"] = cur
    st[1]["pre"], st[1]["xnm"] = _norm_matmul(cur, row(w["mix_norm"][1]), gw("sgu_w_in", 0), "sgu_proj_in")
    sgu_args = (sgu_ln_g_full, sgu_ln_b_full, w["sgu_w"][0], w["sgu_b"][0].T)
    cur, st[1]["p"] = _sgu_fwd(cur, st[1]["pre"], *sgu_args, gw("sgu_w_out", 0), "sgu_fwd")
    st[1]["xc"] = cur
    cur, st[1]["h2"], st[1]["xn2"] = _ffn_fwd(cur, row(w["ffn2_norm"][1]), gw("ffn2_w_in", 1), gw("ffn2_w_out", 1), "ffn2_fwd_1")
    dy, loss, d_final = _final_loss(cur, tgt, row(w["final_norm"]), "final_loss")

    red = _Reducer(me, core)
    small = {"final_norm": d_final.reshape(-1)}
    norm_grads = {"ffn1_norm": [None] * DEPTH, "mix_norm": [None] * DEPTH, "ffn2_norm": [None] * DEPTH}
    ga, gb, gc, gd, ge, gf = _pair("ffn2", 1), _pair("sgu", 0), _pair("ffn1", 1), _pair("ffn2", 0), _pair("ab", 0), _pair("ffn1", 0)

    def ffn_backward(prefix, layer, xs, hs, xns, dy_in, bwd_comms=(), dwin_comms=(), dwout_comms=()):
        dx, dh, act, norm_grads[prefix + "_norm"][layer] = _ffn_bwd(
            xs, dy_in, hs, row(w[prefix + "_norm"][layer]), gw(prefix + "_w_in", layer), gw(prefix + "_w_out", layer),
            f"{prefix}_bwd_{layer}", comms=bwd_comms)
        red.add((prefix + "_w_in", layer), _tn_matmul(xns, dh, 1.0, 1024, 1408, f"{prefix}_dwin_{layer}", comms=dwin_comms))
        red.add((prefix + "_w_out", layer), _tn_matmul(act, dy_in, 0.5, 1408, 1024, f"{prefix}_dwout_{layer}", comms=dwout_comms))
        return dx

    s1, s0 = st[1], st[0]
    dy = ffn_backward("ffn2", 1, s1["xc"], s1["h2"], s1["xn2"], dy)
    dy_in = dy
    dy, dpre, norm_grads["mix_norm"][1], dlg, dlb, dw, dbt = _sgu_bwd(
        s1["xb"], dy_in, s1["pre"], row(w["mix_norm"][1]), *sgu_args, gw("sgu_w_in", 0), gw("sgu_w_out", 0), "sgu_bwd", comms=[red.swap(ga)])
    red.add(("sgu_w_in", 0), _tn_matmul(s1["xnm"], dpre, 1.0, 1024, 2048, "sgu_dwin"))
    red.add(("sgu_w_out", 0), _tn_matmul(s1["p"], dy_in, 1.0, 1024, 1024, "sgu_dwout"))
    small.update(sgu_ln_g=dlg, sgu_ln_b=dlb, sgu_w=dw[None], sgu_b=dbt.T[None])
    dy = ffn_backward("ffn1", 1, s1["xa"], s1["h1"], s1["xn1"], dy, bwd_comms=[lambda: red.scatter(ga), lambda: red.swap(gb)],
                      dwin_comms=[lambda: red.scatter(gb), lambda: red.exchange(ga)])
    dy = ffn_backward("ffn2", 0, s0["xc"], s0["h2"], s0["xn2"], dy, bwd_comms=[lambda: red.swap(gc), lambda: red.exchange(gb)],
                      dwin_comms=[lambda: red.scatter(gc)])
    dy_in = dy
    dconv, dpc, dpw, rows = _mix0_bwd_a(dy_in, s0["h0"], s0["yconv"], *pool_args, "mix0_bwd_a")
    dy, dh0, norm_grads["mix_norm"][0] = _mix0_bwd_b(s0["xb"], dy_in, s0["h0"], dconv, dpc, row(w["mix_norm"][0]), conv_w_full,
                                                      gw("ab_w_in", 0), "mix0_bwd_b")
    red.add(("ab_w_in", 0), _tn_matmul(s0["xnm"], dh0, 1.0, 1024, 1536, "ab_dwin", comms=[red.swap(gd), red.exchange(gc)]))
    red.add(("ab_w_out", 0), _tn_matmul(s0["ycat"], dy_in, 1.0, 1024, 1024, "ab_dwout"))
    small.update(pool_w=dpw[None], conv_w=rows[None, 0:CONV_WIDTH], conv_b=rows[32:33], conv_ln_g=rows[33:34], conv_ln_b=rows[34:35],
                 pool_scale=rows[35:36], pool_b=rows[36:37].reshape(1, len(POOL_WINDOWS), POOL_GC))

    small_sum = {}

    def small_ready():
        for k, v in norm_grads.items():
            small[k] = jnp.concatenate(v, axis=0)
        flat = [small[n].reshape(-1, 128) for n in SMALL]
        buf = jnp.concatenate(flat, axis=0)
        pad = (-buf.shape[0]) % 8
        if pad:
            buf = jnp.concatenate([buf, jnp.zeros((pad, 128), F32)], axis=0)

        def done(gathered):
            total, at = _sum_leading(gathered, "reduce_small"), 0
            for n, f in zip(SMALL, flat):
                small_sum[n] = total[at:at + f.shape[0]].reshape(small[n].shape)
                at += f.shape[0]

        return _small_all_gather(buf, done)

    dx, dh, act, norm_grads["ffn1_norm"][0] = _ffn_bwd(
        s0["xa"], dy, s0["h1"], row(w["ffn1_norm"][0]), gw("ffn1_w_in", 0), gw("ffn1_w_out", 0), "ffn1_bwd_0",
        comms=[red.scatter(gd), red.swap(ge)])
    red.add(("ffn1_w_in", 0), _tn_matmul(s0["xn1"], dh, 1.0, 1024, 1408, "ffn1_dwin_0",
                                         comms=[red.scatter(ge), red.exchange(gd), small_ready()]))
    red.add(("ffn1_w_out", 0), _tn_matmul(act, dy, 0.5, 1408, 1024, "ffn1_dwout_0", comms=[red.exchange(ge)]))
    grad_x = dx

    big_out = {}

    def adamw_big(n, layer, comms=()):
        big_out[n] = _adamw_sharded(w[n], red.mine[(n, layer)], red.theirs[(n, layer)], given["m_" + n], given["v_" + n], core, layer,
                                    big_out.get(n), f"adamw_{n}_{layer}", comms=comms)

    f_in, f_out = [gf[0]], [gf[1]]
    adamw_big("ffn2_w_in", 1, [red.swap(gf)])
    adamw_big("ffn2_w_in", 0, [red.scatter(f_in, (0, 4))])
    adamw_big("ffn1_w_in", 1, [red.scatter(f_in, (1, 4))])
    adamw_big("ffn2_w_out", 1, [red.scatter(f_in, (2, 4))])
    adamw_big("ffn2_w_out", 0, [red.scatter(f_in, (3, 4))])
    adamw_big("ffn1_w_out", 1, [red.scatter(f_out, (0, 2))])
    adamw_big("sgu_w_in", 0, [red.scatter(f_out, (1, 2))])
    adamw_big("ab_w_in", 0, [red.exchange(gf)])
    adamw_big("sgu_w_out", 0)
    adamw_big("ab_w_out", 0)
    adamw_big("ffn1_w_in", 0)
    adamw_big("ffn1_w_out", 0)

    loss = lax.psum(loss[0, 0], MESH_AXES)
    grads, delta, new_m, new_v = {}, {}, {}, {}
    for n in WEIGHT_NAMES:
        mom, var = given["m_" + n], given["v_" + n]
        if n in BIG:
            grads[n], delta[n], new_m[n], new_v[n] = big_out[n]
            continue
        g = small_sum[n]
        if n in SHARDED_SMALL:
            width = w[n].shape[-1]
            g = lax.dynamic_slice_in_dim(g, chip * width, width, axis=g.ndim - 1)
        grads[n] = g
        delta[n], new_m[n], new_v[n] = _adamw(w[n], g, mom, var, f"adamw_{n}")
    return (loss, grad_x[None], *[grads[n] for n in WEIGHT_NAMES], *[delta[n] for n in WEIGHT_NAMES],
            *[new_m[n] for n in WEIGHT_NAMES], *[new_v[n] for n in WEIGHT_NAMES])
```

```python
import jax
import jax.numpy as jnp
from jax import lax
from jax.experimental import pallas as pl
from jax.experimental.pallas import tpu as pltpu

F32, BF16 = jnp.float32, jnp.bfloat16
EPS = 1e-6
N_CHIPS = 4
POOL_WINDOWS = (2, 4, 8, 16)
POOL_GC = 128
POOL_CH = 512
CONV_CH = 512
CONV_WIDTH = 31
HALO = 32
SGU_HEADS = 8
CHUNK = 128
DEPTH = 2
ADAM_LR, ADAM_B1, ADAM_B2, ADAM_EPS, ADAM_WD, ADAM_STEP = 0.001, 0.9, 0.999, 1e-08, 0.01, 10
VMEM_LIMIT_BYTES = 60 * 1024 * 1024
MESH_AXES = ("x", "y", "c")
MESH = pl.DeviceIdType.MESH
HBM_SPEC = pl.BlockSpec(memory_space=pltpu.HBM)


def _sds(a):
    return jax.ShapeDtypeStruct(a.shape, a.dtype)


def _cparams_nd(n):
    return pltpu.CompilerParams(dimension_semantics=("arbitrary",) * n, vmem_limit_bytes=VMEM_LIMIT_BYTES)


def _cparams():
    return _cparams_nd(1)


def _dot(a, b):
    return jnp.dot(a, b, preferred_element_type=F32)


def _dot_nt(a, b):
    return lax.dot_general(a, b, (((1,), (1,)), ((), ())), preferred_element_type=F32)


def _dot_tn(a, b):
    return lax.dot_general(a, b, (((0,), (0,)), ((), ())), preferred_element_type=F32)


def _rms_fwd(x):
    r = lax.rsqrt(jnp.mean(x * x, axis=-1, keepdims=True) + EPS)
    return x * r, r


def _rms_bwd(dxn, xh, r, g):
    dxh = dxn * g
    return r * (dxh - xh * jnp.mean(dxh * xh, axis=-1, keepdims=True))


def _ln_fwd(y):
    mu = jnp.mean(y, axis=-1, keepdims=True)
    yc = y - mu
    rs = lax.rsqrt(jnp.mean(yc * yc, axis=-1, keepdims=True) + EPS)
    return yc * rs, rs


def _ln_bwd(dyhat, yhat, rs):
    return rs * (dyhat - jnp.mean(dyhat, axis=-1, keepdims=True) - yhat * jnp.mean(dyhat * yhat, axis=-1, keepdims=True))


def _sigmoid(x):
    return 0.5 * jnp.tanh(0.5 * x) + 0.5


def _const_spec(shape):
    n = len(shape)
    return pl.BlockSpec(shape, lambda i: (0,) * n)


def _row_spec(tm, cols):
    return pl.BlockSpec((tm, cols), lambda i: (i, 0))


def _my_place():
    return lax.axis_index("x"), lax.axis_index("y"), lax.axis_index("c")


def _other_chips(x, y):
    return [(1 - x, y), (x, 1 - y), (1 - x, 1 - y)]


class _Exchange:
    def __init__(self, inputs, out_shapes, plan, count, aliases=None, finish=None, local=None, n_local=0):
        self.inputs, self.out_shapes, self.plan, self.count = list(inputs), list(out_shapes), plan, count
        self.aliases, self.finish, self.local, self.n_local = dict(aliases or {}), finish, local, n_local


def _call(body, *, name, grid, in_specs, out_specs, out_shape, args, scratch_shapes=(), comms=(), scalar=None, aliases=None):
    comms = [cm if isinstance(cm, _Exchange) else cm() for cm in comms]
    in_specs, out_specs, out_shape, scratch_shapes = list(in_specs), list(out_specs), list(out_shape), list(scratch_shapes)
    n_in, n_out, n_scr = len(in_specs), len(out_specs), len(scratch_shapes)
    c_in = [a for cm in comms for a in cm.inputs]
    c_out = [s for cm in comms for s in cm.out_shapes]
    n_remote = sum(cm.count for cm in comms)
    n_local = sum(cm.n_local for cm in comms)
    n_scalar = 0 if scalar is None else 1
    all_aliases = {n_scalar + i: o for i, o in (aliases or {}).items()}
    at_in, at_out = n_scalar + n_in, n_out
    for cm in comms:
        for i, o in cm.aliases.items():
            all_aliases[at_in + i] = at_out + o
        at_in += len(cm.inputs)
        at_out += len(cm.out_shapes)

    def wrapped(*all_refs):
        scalar_ref, refs = all_refs[:n_scalar], all_refs[n_scalar:]
        ins, ci = refs[:n_in], refs[n_in:n_in + len(c_in)]
        at = n_in + len(c_in)
        outs, co = refs[at:at + n_out], refs[at + n_out:at + n_out + len(c_out)]
        at += n_out + len(c_out)
        scr = refs[at:at + n_scr]

        def run_body():
            body(*scalar_ref, *ins, *outs, *scr)

        if not comms:
            run_body()
            return
        send_sems, recv_sems, local_sems = refs[at + n_scr:]
        place = _my_place()
        sends, arrivals, locals_ = [], [], []
        i0 = o0 = 0
        for cm in comms:
            cm_in, cm_out = ci[i0:i0 + len(cm.inputs)], co[o0:o0 + len(cm.out_shapes)]
            i0 += len(cm.inputs)
            o0 += len(cm.out_shapes)
            for src, dst, dev, incoming in cm.plan(cm_in, cm_out, place):
                k = len(sends)
                sends.append(pltpu.make_async_remote_copy(src_ref=src, dst_ref=dst, send_sem=send_sems.at[k], recv_sem=recv_sems.at[k],
                                                          device_id=dev, device_id_type=MESH))
                arrivals.append(pltpu.make_async_remote_copy(src_ref=src, dst_ref=incoming, send_sem=send_sems.at[k],
                                                             recv_sem=recv_sems.at[k], device_id=dev, device_id_type=MESH))
            if cm.local is not None:
                for src, dst in cm.local(cm_in, cm_out, place):
                    locals_.append(pltpu.make_async_copy(src, dst, local_sems.at[len(locals_)]))

        def start():
            for cp in locals_ + sends:
                cp.start()

        def finish():
            for cp in arrivals:
                cp.wait_recv()
            for cp in sends:
                cp.wait_send()
            for cp in locals_:
                cp.wait()

        if not grid:
            start()
            run_body()
            finish()
            return
        ids = [pl.program_id(a) for a in range(len(grid))]
        first, last = ids[0] == 0, ids[0] == grid[0] - 1
        for a in range(1, len(grid)):
            first = jnp.logical_and(first, ids[a] == 0)
            last = jnp.logical_and(last, ids[a] == grid[a] - 1)
        pl.when(first)(start)
        run_body()
        pl.when(last)(finish)

    sems = []
    if comms:
        sems = [pltpu.SemaphoreType.DMA((max(n_remote, 1),)), pltpu.SemaphoreType.DMA((max(n_remote, 1),)),
                pltpu.SemaphoreType.DMA((max(n_local, 1),))]
    all_in, all_out = in_specs + [HBM_SPEC] * len(c_in), out_specs + [HBM_SPEC] * len(c_out)
    if scalar is None:
        kwargs = dict(grid=grid, compiler_params=_cparams_nd(len(grid))) if grid else {}
        res = pl.pallas_call(
            wrapped, name=name, in_specs=all_in, out_specs=all_out, out_shape=out_shape + c_out, scratch_shapes=scratch_shapes + sems,
            input_output_aliases=all_aliases, **kwargs)(*args, *c_in)
    else:
        spec = pltpu.PrefetchScalarGridSpec(num_scalar_prefetch=1, grid=grid, in_specs=all_in, out_specs=all_out,
                                            scratch_shapes=scratch_shapes + sems)
        res = pl.pallas_call(
            wrapped, name=name, grid_spec=spec, out_shape=out_shape + c_out, input_output_aliases=all_aliases,
            compiler_params=_cparams_nd(len(grid)))(scalar, *args, *c_in)
    at = n_out
    for cm in comms:
        got = res[at:at + len(cm.out_shapes)]
        at += len(cm.out_shapes)
        if cm.finish is not None:
            cm.finish(got)
    return list(res[:n_out])


def _exchange_alone(name, comms):
    _call(lambda: None, name=name, grid=(), in_specs=[], out_specs=[], out_shape=[], args=[], comms=comms)


def _in_weight_copies(w_hbm, w_v, sem, base=0):
    return [pltpu.make_async_copy(w_hbm.at[q], w_v.at[q], sem.at[base + q]) for q in range(N_CHIPS)]


def _out_weight_copies(w_hbm, w_v, sem, base=0):
    rows = w_hbm.shape[1]
    return [pltpu.make_async_copy(w_hbm.at[q], w_v.at[pl.ds(q * rows, rows)], sem.at[base + q]) for q in range(N_CHIPS)]


def _load_at_first_step(copies):
    @pl.when(pl.program_id(0) == 0)
    def _():
        for cp in copies:
            cp.start()
        for cp in copies:
            cp.wait()


def _ffn_fwd(x, g, win_g, wout_g, name, comms=()):
    t, d = x.shape
    c = win_g.shape[-1]
    ff = 2 * c
    tm = min(256, t)

    def body(x_ref, g_ref, win_hbm, wout_hbm, xo_ref, h_ref, xn_ref, win_v, wout_v, sem):
        _load_at_first_step(_in_weight_copies(win_hbm, win_v, sem) + _out_weight_copies(wout_hbm, wout_v, sem, N_CHIPS))
        xv = x_ref[...]
        xh, _ = _rms_fwd(xv)
        xn = (xh * g_ref[...]).astype(BF16)
        xn_ref[...] = xn
        acc = jnp.zeros((tm, d), F32)
        for j in range(2):
            gate = _dot(xn, win_v[j])
            up = _dot(xn, win_v[j + 2])
            h_ref[:, j * c:(j + 1) * c] = gate.astype(BF16)
            h_ref[:, ff + j * c:ff + (j + 1) * c] = up.astype(BF16)
            act = (gate * _sigmoid(gate) * up).astype(BF16)
            acc = acc + _dot(act, wout_v[j * c:(j + 1) * c, :])
        xo_ref[...] = xv + 0.5 * acc

    return _call(
        body, name=name, grid=(t // tm,),
        in_specs=[_row_spec(tm, d), _const_spec((1, d)), HBM_SPEC, HBM_SPEC],
        out_specs=[_row_spec(tm, d), _row_spec(tm, 2 * ff), _row_spec(tm, d)],
        out_shape=[jax.ShapeDtypeStruct((t, d), F32), jax.ShapeDtypeStruct((t, 2 * ff), BF16), jax.ShapeDtypeStruct((t, d), BF16)],
        scratch_shapes=[pltpu.VMEM((N_CHIPS, d, c), BF16), pltpu.VMEM((ff, d), BF16), pltpu.SemaphoreType.DMA((2 * N_CHIPS,))],
        args=(x, g, win_g, wout_g), comms=comms)


def _ffn_bwd(x, dy, h, g, win_g, wout_g, name, comms=()):
    t, d = x.shape
    c = win_g.shape[-1]
    ff = 2 * c
    tm = min(256, t)

    def body(x_ref, dy_ref, h_ref, g_ref, win_hbm, wout_hbm, dx_ref, dh_ref, act_ref, dg_ref, win_v, wout_v, sem):
        _load_at_first_step(_in_weight_copies(win_hbm, win_v, sem) + _out_weight_copies(wout_hbm, wout_v, sem, N_CHIPS))

        @pl.when(pl.program_id(0) == 0)
        def _():
            dg_ref[...] = jnp.zeros_like(dg_ref)

        xv, dyv, gv = x_ref[...], dy_ref[...], g_ref[...]
        xh, r = _rms_fwd(xv)
        dyh = (0.5 * dyv).astype(BF16)
        dxn = jnp.zeros((tm, d), F32)
        for j in range(2):
            gate = h_ref[:, j * c:(j + 1) * c].astype(F32)
            up = h_ref[:, ff + j * c:ff + (j + 1) * c].astype(F32)
            dact = _dot_nt(dyh, wout_v[j * c:(j + 1) * c, :])
            s = _sigmoid(gate)
            sl = gate * s
            act_ref[:, j * c:(j + 1) * c] = (sl * up).astype(BF16)
            dgate = (dact * up * (s * (1.0 + gate * (1.0 - s)))).astype(BF16)
            dup = (dact * sl).astype(BF16)
            dh_ref[:, j * c:(j + 1) * c] = dgate
            dh_ref[:, ff + j * c:ff + (j + 1) * c] = dup
            dxn = dxn + _dot_nt(dgate, win_v[j]) + _dot_nt(dup, win_v[j + 2])
        dg_ref[...] += jnp.sum(dxn * xh, axis=0, keepdims=True)
        dx_ref[...] = dyv + _rms_bwd(dxn, xh, r, gv)

    return _call(
        body, name=name, grid=(t // tm,),
        in_specs=[_row_spec(tm, d), _row_spec(tm, d), _row_spec(tm, 2 * ff), _const_spec((1, d)), HBM_SPEC, HBM_SPEC],
        out_specs=[_row_spec(tm, d), _row_spec(tm, 2 * ff), _row_spec(tm, ff), _const_spec((1, d))],
        out_shape=[jax.ShapeDtypeStruct((t, d), F32), jax.ShapeDtypeStruct((t, 2 * ff), BF16), jax.ShapeDtypeStruct((t, ff), BF16),
                   jax.ShapeDtypeStruct((1, d), F32)],
        scratch_shapes=[pltpu.VMEM((N_CHIPS, d, c), BF16), pltpu.VMEM((ff, d), BF16), pltpu.SemaphoreType.DMA((2 * N_CHIPS,))],
        args=(x, dy, h, g, win_g, wout_g), comms=comms)


def _norm_matmul(x, g, win_g, name, comms=()):
    t, d = x.shape
    c = win_g.shape[-1]
    tm = min(512, t)

    def body(x_ref, g_ref, win_hbm, o_ref, xn_ref, win_v, sem):
        _load_at_first_step(_in_weight_copies(win_hbm, win_v, sem))
        xh, _ = _rms_fwd(x_ref[...])
        xn = (xh * g_ref[...]).astype(BF16)
        xn_ref[...] = xn
        for q in range(N_CHIPS):
            o_ref[:, q * c:(q + 1) * c] = _dot(xn, win_v[q])

    return _call(
        body, name=name, grid=(t // tm,),
        in_specs=[_row_spec(tm, d), _const_spec((1, d)), HBM_SPEC],
        out_specs=[_row_spec(tm, N_CHIPS * c), _row_spec(tm, d)],
        out_shape=[jax.ShapeDtypeStruct((t, N_CHIPS * c), F32), jax.ShapeDtypeStruct((t, d), BF16)],
        scratch_shapes=[pltpu.VMEM((N_CHIPS, d, c), BF16), pltpu.SemaphoreType.DMA((N_CHIPS,))],
        args=(x, g, win_g), comms=comms)


def _proj_in_bwd_tail(dh, win_v, c):
    dxn = _dot_nt(dh[:, 0:c], win_v[0])
    for q in range(1, N_CHIPS):
        dxn = dxn + _dot_nt(dh[:, q * c:(q + 1) * c], win_v[q])
    return dxn


def _prev_halo_spec(tm, cols):
    return pl.BlockSpec((HALO, cols), lambda i: (jnp.maximum(i * (tm // HALO) - 1, 0), 0))


def _next_halo_spec(tm, cols, t):
    last = t // HALO - 1
    return pl.BlockSpec((HALO, cols), lambda i: (jnp.minimum((i + 1) * (tm // HALO), last), 0))


def _shift_down(w, k):
    return w if k == 0 else pltpu.roll(w, k, 0)


def _shift_up(w, k):
    return w if k == 0 else pltpu.roll(w, w.shape[0] - k, 0)


def _pool_counts(i, tm):
    pos = (i * tm + lax.broadcasted_iota(jnp.int32, (tm, POOL_CH), 0) + 1).astype(F32)
    lane = lax.broadcasted_iota(jnp.int32, (tm, POOL_CH), 1)
    win = jnp.where(lane < POOL_GC, 2.0, jnp.where(lane < 2 * POOL_GC, 4.0, jnp.where(lane < 3 * POOL_GC, 8.0, 16.0)))
    return jnp.minimum(pos, win)


def _group_select(parts):
    return jnp.concatenate([p[:, k * POOL_GC:(k + 1) * POOL_GC] for k, p in enumerate(parts)], axis=1)


def _mix0_recompute(i, tm, h_cur, h_prev, conv_w, conv_b, y=None):
    prev = jnp.where(i > 0, h_prev, 0.0)
    win = jnp.concatenate([prev, h_cur], axis=0)
    u_w = win[:, 0:POOL_CH]
    a_w = win[:, POOL_CH:POOL_CH + CONV_CH]
    gt_w = win[:, POOL_CH + CONV_CH:]
    g_w = a_w * _sigmoid(gt_w)
    if y is None:
        y = jnp.zeros((tm, CONV_CH), F32)
        for k in range(CONV_WIDTH):
            y = y + conv_w[k:k + 1, :] * _shift_down(g_w, CONV_WIDTH - 1 - k)[HALO:, :]
        y = y + conv_b
    s2 = u_w + _shift_down(u_w, 1)
    s4 = s2 + _shift_down(s2, 2)
    s8 = s4 + _shift_down(s4, 4)
    s16 = s8 + _shift_down(s8, 8)
    sums = _group_select([s2[HALO:], s4[HALO:], s8[HALO:], s16[HALO:]])
    cnt = _pool_counts(i, tm)
    pooled = sums / cnt - h_cur[:, 0:POOL_CH]
    return g_w, y, pooled, cnt


def _pool_linear(pooled, pw_ref, pb):
    return jnp.concatenate(
        [_dot(pooled[:, k * POOL_GC:(k + 1) * POOL_GC].astype(BF16), pw_ref[k].astype(BF16)) for k in range(len(POOL_WINDOWS))], axis=1) + pb


def _mix0_fwd(x, h0, pool_w, pool_b, pool_scale, conv_w, conv_b, ln_g, ln_b, wout_g, name, comms=()):
    t, d = x.shape
    tm = min(256, t)
    hc = h0.shape[1]

    def body(x_ref, h_ref, hp_ref, pw_ref, pb_ref, ps_ref, cw_ref, cb_ref, lg_ref, lb_ref, wout_hbm, xo_ref, ycat_ref, y_ref, wout_v, sem):
        _load_at_first_step(_out_weight_copies(wout_hbm, wout_v, sem))
        i = pl.program_id(0)
        _, y, pooled, _ = _mix0_recompute(i, tm, h_ref[...], hp_ref[...], cw_ref[...], cb_ref[...])
        y_ref[...] = y
        yhat, _ = _ln_fwd(y)
        yn = yhat * lg_ref[...] + lb_ref[...]
        yb = yn * _sigmoid(yn)
        ya = _pool_linear(pooled, pw_ref, pb_ref[...]) * ps_ref[...]
        ycat = jnp.concatenate([ya, yb], axis=1).astype(BF16)
        ycat_ref[...] = ycat
        xo_ref[...] = x_ref[...] + _dot(ycat, wout_v[...])

    return _call(
        body, name=name, grid=(t // tm,),
        in_specs=[_row_spec(tm, d), _row_spec(tm, hc), _prev_halo_spec(tm, hc), _const_spec(pool_w.shape), _const_spec((1, POOL_CH)),
                  _const_spec((1, POOL_CH)), _const_spec(conv_w.shape), _const_spec((1, CONV_CH)), _const_spec((1, CONV_CH)),
                  _const_spec((1, CONV_CH)), HBM_SPEC],
        out_specs=[_row_spec(tm, d), _row_spec(tm, d), _row_spec(tm, CONV_CH)],
        out_shape=[jax.ShapeDtypeStruct((t, d), F32), jax.ShapeDtypeStruct((t, d), BF16), jax.ShapeDtypeStruct((t, CONV_CH), F32)],
        scratch_shapes=[pltpu.VMEM((d, d), BF16), pltpu.SemaphoreType.DMA((N_CHIPS,))],
        args=(x, h0, h0, pool_w, pool_b, pool_scale, conv_w, conv_b, ln_g, ln_b, wout_g), comms=comms)


def _mix0_bwd_a(dy, h0, y_conv, pool_w, pool_b, pool_scale, conv_w, conv_b, ln_g, ln_b, wout_g, name):
    t, d = dy.shape
    tm = min(256, t)
    hc = h0.shape[1]
    n_small = 40

    def body(dy_ref, h_ref, hp_ref, y_ref, pw_ref, pb_ref, ps_ref, cw_ref, cb_ref, lg_ref, lb_ref, wout_hbm,
             dconv_ref, dpc_ref, dpw_ref, small_ref, wout_v, sem):
        _load_at_first_step(_out_weight_copies(wout_hbm, wout_v, sem))
        i = pl.program_id(0)

        @pl.when(i == 0)
        def _():
            dpw_ref[...] = jnp.zeros_like(dpw_ref)
            small_ref[...] = jnp.zeros_like(small_ref)

        g_w, y, pooled, cnt = _mix0_recompute(i, tm, h_ref[...], hp_ref[...], cw_ref[...], cb_ref[...], y_ref[...])
        yhat, rs = _ln_fwd(y)
        lg = lg_ref[...]
        yn = yhat * lg + lb_ref[...]
        mixed = _pool_linear(pooled, pw_ref, pb_ref[...])
        dycat = _dot_nt(dy_ref[...].astype(BF16), wout_v[...])
        dya, dyb = dycat[:, 0:POOL_CH], dycat[:, POOL_CH:]
        sg = _sigmoid(yn)
        dyn = dyb * (sg * (1.0 + yn * (1.0 - sg)))
        dyc = _ln_bwd(dyn * lg, yhat, rs)
        dconv_ref[...] = dyc

        def add_row(k, value):
            small_ref[k:k + 1, :] += jnp.sum(value, axis=0, keepdims=True)

        for k in range(CONV_WIDTH):
            add_row(k, dyc * _shift_down(g_w, CONV_WIDTH - 1 - k)[HALO:, :])
        add_row(32, dyc)
        add_row(33, dyn * yhat)
        add_row(34, dyn)
        scale = ps_ref[...]
        dmixed = dya * scale
        add_row(35, dya * mixed)
        add_row(36, dmixed)
        dmb = dmixed.astype(BF16)
        dpooled = []
        for k in range(len(POOL_WINDOWS)):
            sl = slice(k * POOL_GC, (k + 1) * POOL_GC)
            dpw_ref[k] += _dot_tn(pooled[:, sl].astype(BF16), dmb[:, sl])
            dpooled.append(_dot_nt(dmb[:, sl], pw_ref[k].astype(BF16)))
        dpc_ref[...] = jnp.concatenate(dpooled, axis=1) / cnt

    return _call(
        body, name=name, grid=(t // tm,),
        in_specs=[_row_spec(tm, d), _row_spec(tm, hc), _prev_halo_spec(tm, hc), _row_spec(tm, CONV_CH), _const_spec(pool_w.shape),
                  _const_spec((1, POOL_CH)), _const_spec((1, POOL_CH)), _const_spec(conv_w.shape), _const_spec((1, CONV_CH)),
                  _const_spec((1, CONV_CH)), _const_spec((1, CONV_CH)), HBM_SPEC],
        out_specs=[_row_spec(tm, CONV_CH), _row_spec(tm, POOL_CH), _const_spec(pool_w.shape), _const_spec((n_small, CONV_CH))],
        out_shape=[jax.ShapeDtypeStruct((t, CONV_CH), F32), jax.ShapeDtypeStruct((t, POOL_CH), F32),
                   jax.ShapeDtypeStruct(pool_w.shape, F32), jax.ShapeDtypeStruct((n_small, CONV_CH), F32)],
        scratch_shapes=[pltpu.VMEM((d, d), BF16), pltpu.SemaphoreType.DMA((N_CHIPS,))],
        args=(dy, h0, h0, y_conv, pool_w, pool_b, pool_scale, conv_w, conv_b, ln_g, ln_b, wout_g))


def _mix0_bwd_b(x, dy, h0, dconv, dpc, g, conv_w, win_g, name):
    t, d = x.shape
    tm = min(256, t)
    hc = h0.shape[1]
    c = win_g.shape[-1]
    n_tiles = t // tm

    def body(x_ref, dy_ref, h_ref, dc_ref, dcn_ref, dp_ref, dpn_ref, g_ref, cw_ref, win_hbm, dx_ref, dh_ref, dg_ref, win_v, sem):
        _load_at_first_step(_in_weight_copies(win_hbm, win_v, sem))
        i = pl.program_id(0)

        @pl.when(i == 0)
        def _():
            dg_ref[...] = jnp.zeros_like(dg_ref)

        not_last = i < n_tiles - 1
        dc_w = jnp.concatenate([dc_ref[...], jnp.where(not_last, dcn_ref[...], 0.0)], axis=0)
        dp_w = jnp.concatenate([dp_ref[...], jnp.where(not_last, dpn_ref[...], 0.0)], axis=0)
        cw = cw_ref[...]
        dg = jnp.zeros((tm, CONV_CH), F32)
        for k in range(CONV_WIDTH):
            dg = dg + cw[k:k + 1, :] * _shift_up(dc_w, CONV_WIDTH - 1 - k)[0:tm, :]
        a2 = dp_w + _shift_up(dp_w, 1)
        a4 = a2 + _shift_up(a2, 2)
        a8 = a4 + _shift_up(a4, 4)
        a16 = a8 + _shift_up(a8, 8)
        back = _group_select([a2[0:tm], a4[0:tm], a8[0:tm], a16[0:tm]])
        du = back - dp_ref[...] * _pool_counts(i, tm)
        hv = h_ref[...]
        a = hv[:, POOL_CH:POOL_CH + CONV_CH]
        sig = _sigmoid(hv[:, POOL_CH + CONV_CH:])
        dh = jnp.concatenate([du, dg * sig, dg * a * sig * (1.0 - sig)], axis=1).astype(BF16)
        dh_ref[...] = dh
        dxn = _proj_in_bwd_tail(dh, win_v, c)
        xh, r = _rms_fwd(x_ref[...])
        dg_ref[...] += jnp.sum(dxn * xh, axis=0, keepdims=True)
        dx_ref[...] = dy_ref[...] + _rms_bwd(dxn, xh, r, g_ref[...])

    return _call(
        body, name=name, grid=(n_tiles,),
        in_specs=[_row_spec(tm, d), _row_spec(tm, d), _row_spec(tm, hc), _row_spec(tm, CONV_CH), _next_halo_spec(tm, CONV_CH, t),
                  _row_spec(tm, POOL_CH), _next_halo_spec(tm, POOL_CH, t), _const_spec((1, d)), _const_spec(conv_w.shape), HBM_SPEC],
        out_specs=[_row_spec(tm, d), _row_spec(tm, hc), _const_spec((1, d))],
        out_shape=[jax.ShapeDtypeStruct((t, d), F32), jax.ShapeDtypeStruct((t, hc), BF16), jax.ShapeDtypeStruct((1, d), F32)],
        scratch_shapes=[pltpu.VMEM((N_CHIPS, d, c), BF16), pltpu.SemaphoreType.DMA((N_CHIPS,))],
        args=(x, dy, h0, dconv, dconv, dpc, dpc, g, conv_w, win_g))


SQRT_HALF = 0.7071067811865476
INV_SQRT_2PI = 0.3989422804014327


def _causal_mask():
    return (lax.broadcasted_iota(jnp.int32, (CHUNK, CHUNK), 1) <= lax.broadcasted_iota(jnp.int32, (CHUNK, CHUNK), 0)).astype(F32)


def _sgu_recompute(pre, lg, lb):
    half = pre.shape[1] // 2
    phi = 0.5 * (1.0 + lax.erf(pre * SQRT_HALF))
    z = pre * phi
    u, v = z[:, 0:half], z[:, half:]
    vhat, rs = _ln_fwd(v)
    return u, vhat, rs, vhat * lg + lb, phi


def _sgu_spatial(vln, w_ref, bt, tm):
    mask = _causal_mask()
    wm = [(w_ref[hd] * mask).astype(BF16) for hd in range(SGU_HEADS)]
    vb = vln.astype(BF16)
    rows = []
    for ch in range(tm // CHUNK):
        blocks = [_dot(wm[hd], vb[ch * CHUNK:(ch + 1) * CHUNK, hd * CHUNK:(hd + 1) * CHUNK]) + bt[:, hd:hd + 1] for hd in range(SGU_HEADS)]
        rows.append(jnp.concatenate(blocks, axis=1))
    return jnp.concatenate(rows, axis=0), wm


def _sgu_fwd(x, pre, ln_g, ln_b, w, bt, wout_g, name):
    t, d = x.shape
    tm = min(256, t)
    pc = pre.shape[1]

    def body(x_ref, pre_ref, lg_ref, lb_ref, w_ref, bt_ref, wout_hbm, xo_ref, p_ref, wout_v, sem):
        _load_at_first_step(_out_weight_copies(wout_hbm, wout_v, sem))
        u, _, _, vln, _ = _sgu_recompute(pre_ref[...], lg_ref[...], lb_ref[...])
        vo, _ = _sgu_spatial(vln, w_ref, bt_ref[...], tm)
        p = (u * vo).astype(BF16)
        p_ref[...] = p
        xo_ref[...] = x_ref[...] + _dot(p, wout_v[...])

    return _call(
        body, name=name, grid=(t // tm,),
        in_specs=[_row_spec(tm, d), _row_spec(tm, pc), _const_spec((1, d)), _const_spec((1, d)), _const_spec(w.shape),
                  _const_spec(bt.shape), HBM_SPEC],
        out_specs=[_row_spec(tm, d), _row_spec(tm, d)],
        out_shape=[jax.ShapeDtypeStruct((t, d), F32), jax.ShapeDtypeStruct((t, d), BF16)],
        scratch_shapes=[pltpu.VMEM((d, d), BF16), pltpu.SemaphoreType.DMA((N_CHIPS,))],
        args=(x, pre, ln_g, ln_b, w, bt, wout_g))


def _sgu_bwd(x, dy, pre, g, ln_g, ln_b, w, bt, win_g, wout_g, name, comms=()):
    t, d = x.shape
    tm = min(256, t)
    pc = pre.shape[1]
    c = win_g.shape[-1]

    def body(x_ref, dy_ref, pre_ref, g_ref, lg_ref, lb_ref, w_ref, bt_ref, win_hbm, wout_hbm,
             dx_ref, dpre_ref, dg_ref, dlg_ref, dlb_ref, dw_ref, dbt_ref, win_v, wout_v, sem):
        _load_at_first_step(_in_weight_copies(win_hbm, win_v, sem) + _out_weight_copies(wout_hbm, wout_v, sem, N_CHIPS))
        i = pl.program_id(0)

        @pl.when(i == 0)
        def _():
            for ref in (dg_ref, dlg_ref, dlb_ref, dw_ref, dbt_ref):
                ref[...] = jnp.zeros_like(ref)

        prev = pre_ref[...]
        lg = lg_ref[...]
        u, vhat, rs, vln, phi = _sgu_recompute(prev, lg, lb_ref[...])
        vo, wm = _sgu_spatial(vln, w_ref, bt_ref[...], tm)
        dp = _dot_nt(dy_ref[...].astype(BF16), wout_v[...])
        du = dp * vo
        dvo = dp * u
        dvob = dvo.astype(BF16)
        vb = vln.astype(BF16)
        head_lane = lax.broadcasted_iota(jnp.int32, (CHUNK, SGU_HEADS), 1)
        dbt = jnp.zeros((CHUNK, SGU_HEADS), F32)
        dw = [jnp.zeros((CHUNK, CHUNK), F32) for _ in range(SGU_HEADS)]
        rows = []
        for ch in range(tm // CHUNK):
            rs_ = slice(ch * CHUNK, (ch + 1) * CHUNK)
            blocks = []
            for hd in range(SGU_HEADS):
                cs = slice(hd * CHUNK, (hd + 1) * CHUNK)
                dbt = dbt + jnp.where(head_lane == hd, jnp.sum(dvo[rs_, cs], axis=1, keepdims=True), 0.0)
                dw[hd] = dw[hd] + _dot_nt(dvob[rs_, cs], vb[rs_, cs])
                blocks.append(_dot_tn(wm[hd], dvob[rs_, cs]))
            rows.append(jnp.concatenate(blocks, axis=1))
        dvln = jnp.concatenate(rows, axis=0)
        mask = _causal_mask()
        for hd in range(SGU_HEADS):
            dw_ref[hd] += dw[hd] * mask
        dbt_ref[...] += dbt
        dlg_ref[...] += jnp.sum(dvln * vhat, axis=0, keepdims=True)
        dlb_ref[...] += jnp.sum(dvln, axis=0, keepdims=True)
        dv = _ln_bwd(dvln * lg, vhat, rs)
        gelu_grad = phi + prev * jnp.exp(-0.5 * prev * prev) * INV_SQRT_2PI
        dpre = (jnp.concatenate([du, dv], axis=1) * gelu_grad).astype(BF16)
        dpre_ref[...] = dpre
        dxn = _proj_in_bwd_tail(dpre, win_v, c)
        xh, r = _rms_fwd(x_ref[...])
        dg_ref[...] += jnp.sum(dxn * xh, axis=0, keepdims=True)
        dx_ref[...] = dy_ref[...] + _rms_bwd(dxn, xh, r, g_ref[...])

    return _call(
        body, name=name, grid=(t // tm,),
        in_specs=[_row_spec(tm, d), _row_spec(tm, d), _row_spec(tm, pc), _const_spec((1, d)), _const_spec((1, d)), _const_spec((1, d)),
                  _const_spec(w.shape), _const_spec(bt.shape), HBM_SPEC, HBM_SPEC],
        out_specs=[_row_spec(tm, d), _row_spec(tm, pc), _const_spec((1, d)), _const_spec((1, d)), _const_spec((1, d)),
                   _const_spec(w.shape), _const_spec(bt.shape)],
        out_shape=[jax.ShapeDtypeStruct((t, d), F32), jax.ShapeDtypeStruct((t, pc), BF16), jax.ShapeDtypeStruct((1, d), F32),
                   jax.ShapeDtypeStruct((1, d), F32), jax.ShapeDtypeStruct((1, d), F32), jax.ShapeDtypeStruct(w.shape, F32),
                   jax.ShapeDtypeStruct(bt.shape, F32)],
        scratch_shapes=[pltpu.VMEM((N_CHIPS, d, c), BF16), pltpu.VMEM((d, d), BF16), pltpu.SemaphoreType.DMA((2 * N_CHIPS,))],
        args=(x, dy, pre, g, ln_g, ln_b, w, bt, win_g, wout_g), comms=comms)


def _final_loss(x, tgt, g, name):
    t, d = x.shape
    tm = min(512, t)

    def body(x_ref, t_ref, g_ref, dx_ref, loss_ref, dg_ref):
        @pl.when(pl.program_id(0) == 0)
        def _():
            loss_ref[...] = jnp.zeros_like(loss_ref)
            dg_ref[...] = jnp.zeros_like(dg_ref)

        gv = g_ref[...]
        xh, r = _rms_fwd(x_ref[...])
        diff = xh * gv - t_ref[...]
        loss_ref[...] += 0.5 * jnp.sum(jnp.sum(diff * diff, axis=1, keepdims=True), axis=0, keepdims=True) / d
        dout = diff / d
        dg_ref[...] += jnp.sum(dout * xh, axis=0, keepdims=True)
        dx_ref[...] = _rms_bwd(dout, xh, r, gv)

    return _call(
        body, name=name, grid=(t // tm,),
        in_specs=[_row_spec(tm, d), _row_spec(tm, d), _const_spec((1, d))],
        out_specs=[_row_spec(tm, d), _const_spec((1, 1)), _const_spec((1, d))],
        out_shape=[jax.ShapeDtypeStruct((t, d), F32), jax.ShapeDtypeStruct((1, 1), F32), jax.ShapeDtypeStruct((1, d), F32)],
        args=(x, tgt, g))


def _tn_matmul(a, b, scale, bm, bn, name, comms=()):
    t, m = a.shape
    n = b.shape[1]
    tk = min(512, t)
    bm, bn = min(bm, m), min(bn, n)
    nk = t // tk

    def body(a_ref, b_ref, o_ref, acc_ref):
        k = pl.program_id(2)

        @pl.when(k == 0)
        def _():
            acc_ref[...] = jnp.zeros_like(acc_ref)

        bv = b_ref[...]
        if bv.dtype != BF16:
            bv = (scale * bv).astype(BF16)
        acc_ref[...] += _dot_tn(a_ref[...], bv)

        @pl.when(k == nk - 1)
        def _():
            o_ref[...] = acc_ref[...].astype(BF16)

    return _call(
        body, name=name, grid=(m // bm, n // bn, nk),
        in_specs=[pl.BlockSpec((tk, bm), lambda i, j, k: (k, i)), pl.BlockSpec((tk, bn), lambda i, j, k: (k, j))],
        out_specs=[pl.BlockSpec((bm, bn), lambda i, j, k: (i, j))],
        out_shape=[jax.ShapeDtypeStruct((m, n), BF16)],
        scratch_shapes=[pltpu.VMEM((bm, bn), F32)],
        args=(a, b), comms=comms)[0]


def _row_tile(rows, cols, budget_bytes=2 * 1024 * 1024):
    best = None
    for cand in range(16, rows + 1, 16):
        if rows % cand == 0 and cand * cols * 4 <= budget_bytes:
            best = cand
    return best or rows


def _scalar_grid(grid, in_specs, out_specs):
    return pltpu.PrefetchScalarGridSpec(num_scalar_prefetch=1, grid=grid, in_specs=in_specs, out_specs=out_specs)


def _cast_into_slot(w, layer, me, name):
    _, rows, cols = w.shape
    tr = _row_tile(rows, cols)

    def body(me_ref, w_ref, o_ref):
        o_ref[...] = w_ref[...].astype(BF16)

    return pl.pallas_call(
        body, name=name,
        grid_spec=_scalar_grid((rows // tr,), [pl.BlockSpec((None, tr, cols), lambda i, me: (layer, i, 0))],
                               pl.BlockSpec((None, tr, cols), lambda i, me: (me[0], i, 0))),
        out_shape=jax.ShapeDtypeStruct((N_CHIPS, rows, cols), BF16), compiler_params=_cparams())(me, w)


def _add_half(view, other, core, name):
    q, _, r, c = view.shape
    tr = _row_tile(r, c)

    def body(core_ref, a_ref, b_ref, o_ref):
        o_ref[...] = (a_ref[...].astype(F32) + b_ref[...].astype(F32)).astype(BF16)

    return pl.pallas_call(
        body, name=name,
        grid_spec=_scalar_grid((q, r // tr), [pl.BlockSpec((None, None, tr, c), lambda k, i, core: (k, core[0], i, 0)),
                                             pl.BlockSpec((None, tr, c), lambda k, i, core: (k, i, 0))],
                               pl.BlockSpec((None, tr, c), lambda k, i, core: (k, i, 0))),
        out_shape=jax.ShapeDtypeStruct((q, r, c), BF16), compiler_params=_cparams_nd(2))(core, view, other)


def _reduce_piece(partial, staged, me, column_sharded, name):
    _, r, c = staged.shape
    tr = _row_tile(r, c, budget_bytes=1024 * 1024)
    nt = r // tr
    if column_sharded:
        own2d = partial.reshape(r, N_CHIPS * c)
        own_spec = pl.BlockSpec((tr, c), lambda i, me: (i, me[0]))
    else:
        own2d = partial.reshape(N_CHIPS * r, c)
        own_spec = pl.BlockSpec((tr, c), lambda i, me: (me[0] * nt + i, 0))
    ring = [pl.BlockSpec((None, tr, c), lambda i, me, k=k: ((me[0] + k) % N_CHIPS, i, 0)) for k in (1, 2, 3)]

    def body(me_ref, own_ref, s1_ref, s2_ref, s3_ref, o_ref):
        o_ref[...] = ((own_ref[...].astype(F32) + s1_ref[...].astype(F32)) + s2_ref[...].astype(F32)) + s3_ref[...].astype(F32)

    return pl.pallas_call(
        body, name=name, grid_spec=_scalar_grid((nt,), [own_spec] + ring, pl.BlockSpec((tr, c), lambda i, me: (i, 0))),
        out_shape=jax.ShapeDtypeStruct((r, c), F32), compiler_params=_cparams())(me, own2d, staged, staged, staged)


def _sum_leading(s, name):
    n, rows, cols = s.shape
    tr = _row_tile(rows, cols, budget_bytes=1024 * 1024)

    def body(s_ref, o_ref):
        acc = s_ref[0].astype(F32)
        for k in range(1, n):
            acc = acc + s_ref[k].astype(F32)
        o_ref[...] = acc

    return pl.pallas_call(
        body, name=name, grid=(rows // tr,), in_specs=[pl.BlockSpec((n, tr, cols), lambda i: (0, i, 0))], out_specs=_row_spec(tr, cols),
        out_shape=jax.ShapeDtypeStruct((rows, cols), F32), compiler_params=_cparams())(s)


ADAM_C1 = 1.0 / (1.0 - ADAM_B1 ** ADAM_STEP)
ADAM_C2 = 1.0 / (1.0 - ADAM_B2 ** ADAM_STEP)


def _adamw_math(w, g, m, v):
    mn = ADAM_B1 * m + (1.0 - ADAM_B1) * g
    vn = ADAM_B2 * v + (1.0 - ADAM_B2) * (g * g)
    return -ADAM_LR * ((mn * ADAM_C1) / (jnp.sqrt(vn * ADAM_C2) + ADAM_EPS) + ADAM_WD * w), mn, vn


def _adamw(w, g, m, v, name):
    shape = w.shape
    cols = shape[-1] if w.ndim > 1 else 128
    w2, g2, m2, v2 = (a.reshape(-1, cols) for a in (w, g, m, v))
    rows = w2.shape[0]
    tr = _row_tile(rows, cols, budget_bytes=1024 * 1024)

    def body(w_ref, g_ref, m_ref, v_ref, d_ref, mo_ref, vo_ref):
        d_ref[...], mo_ref[...], vo_ref[...] = _adamw_math(w_ref[...], g_ref[...], m_ref[...], v_ref[...])

    spec = _row_spec(tr, cols)
    outs = pl.pallas_call(
        body, name=name, grid=(rows // tr,), in_specs=[spec] * 4, out_specs=[spec] * 3,
        out_shape=[jax.ShapeDtypeStruct((rows, cols), F32)] * 3, compiler_params=_cparams())(w2, g2, m2, v2)
    return tuple(o.reshape(shape) for o in outs)


def _adamw_sharded(w, g_mine, g_sibling, m, v, core, layer, prev, name, comms=()):
    n_layers, r, c = w.shape
    half = r // 2
    tr = _row_tile(half, c)
    nt = half // tr

    def body(core_ref, w_ref, gm_ref, gs_ref, m_ref, v_ref, *rest):
        g_ref, d_ref, mo_ref, vo_ref = rest[-4:]
        gv = jnp.where(pl.program_id(0) == core_ref[0], gm_ref[...], gs_ref[...])
        g_ref[...] = gv
        d_ref[...], mo_ref[...], vo_ref[...] = _adamw_math(w_ref[...], gv, m_ref[...], v_ref[...])

    full = pl.BlockSpec((None, tr, c), lambda h, i, core: (layer, h * nt + i, 0))
    part = pl.BlockSpec((tr, c), lambda h, i, core: (i, 0))
    args = [w, g_mine, g_sibling, m, v]
    in_specs = [full, part, part, full, full]
    aliases = {}
    if prev is not None:
        aliases = {len(args) + k: k for k in range(4)}
        args += list(prev)
        in_specs += [pl.BlockSpec(memory_space=pl.ANY)] * 4
    return _call(body, name=name, grid=(2, nt), in_specs=in_specs, out_specs=[full] * 4, out_shape=[jax.ShapeDtypeStruct(w.shape, F32)] * 4,
                 args=args, comms=comms, scalar=core, aliases=aliases)


BIG_IN = ("ffn1_w_in", "ffn2_w_in", "ab_w_in", "sgu_w_in")
BIG_OUT = ("ffn1_w_out", "ffn2_w_out", "ab_w_out", "sgu_w_out")
BIG = BIG_IN + BIG_OUT


class _Gatherer:
    def __init__(self, slots):
        self.slots = dict(slots)

    def _stage(self, keys, d2d):
        n = len(keys)

        def plan(ins, outs, place):
            x, y, c = place
            me = 2 * x + y
            remote = []
            for a in range(n):
                rows = ins[a].shape[1] // 2

                def half(ref, q, core, rows=rows):
                    return ref.at[q, pl.ds(core * rows, rows), :]

                for (px, py) in _other_chips(x, y):
                    q = 2 * px + py
                    if d2d:
                        remote.append((half(ins[a], q, c), half(outs[a], q, c), (x, y, 1 - c), half(outs[a], q, 1 - c)))
                    else:
                        remote.append((half(ins[a], me, c), half(outs[a], me, c), (px, py, c), half(outs[a], q, c)))
            return remote

        def finish(outs):
            for k, o in zip(keys, outs):
                self.slots[k] = o

        arrays = [self.slots[k] for k in keys]
        return _Exchange(arrays, [_sds(a) for a in arrays], plan, 3 * n, {a: a for a in range(n)}, finish)

    def ici(self, keys):
        return self._stage(keys, False)

    def d2d(self, keys):
        return self._stage(keys, True)


class _Reducer:
    def __init__(self, me, core):
        self.me, self.core = me, core
        self.views, self.partial, self.staged, self.mine, self.theirs = {}, {}, {}, {}, {}

    def add(self, key, g):
        m, n = g.shape
        if key[0] in BIG_IN:
            self.views[key] = g.reshape(1, 2, m // 2, n)
        else:
            self.views[key] = g.reshape(N_CHIPS, 2, m // (2 * N_CHIPS), n)

    def swap(self, keys):
        views = [self.views[k] for k in keys]

        def plan(ins, outs, place):
            x, y, c = place
            return [(ins[a].at[:, 1 - c], outs[a], (x, y, 1 - c), outs[a]) for a in range(len(keys))]

        def finish(outs):
            for k, v, o in zip(keys, views, outs):
                self.partial[k] = _add_half(v, o, self.core, f"chip_partial_{k[0]}_{k[1]}")

        shapes = [jax.ShapeDtypeStruct((v.shape[0],) + v.shape[2:], v.dtype) for v in views]
        return _Exchange(views, shapes, plan, len(keys), None, finish)

    def scatter(self, keys, part=(0, 1)):
        i, n = part
        n_keys = len(keys)
        parts = [self.partial[k] for k in keys]
        shapes = []
        for k, p in zip(keys, parts):
            q, r, c = p.shape
            shapes.append(jax.ShapeDtypeStruct((N_CHIPS, r, c // N_CHIPS if k[0] in BIG_IN else c), p.dtype))

        def piece(ref, key, q, rows, cols):
            return ref.at[0, rows, pl.ds(q * cols, cols)] if key[0] in BIG_IN else ref.at[q, rows, :]

        def plan(ins, outs, place):
            x, y, c = place
            me = 2 * x + y
            remote = []
            for a, k in enumerate(keys):
                _, r, cols = shapes[a].shape
                rows = pl.ds(i * (r // n), r // n)
                for (px, py) in _other_chips(x, y):
                    q = 2 * px + py
                    remote.append((piece(ins[a], k, q, rows, cols), outs[a].at[me, rows, :], (px, py, c), outs[a].at[q, rows, :]))
            return remote

        def finish(outs):
            for k, p, o in zip(keys, parts, outs):
                self.staged[k] = o
                if i == n - 1:
                    self.mine[k] = _reduce_piece(p, o, self.me, k[0] in BIG_IN, f"reduce_{k[0]}_{k[1]}")

        inputs, aliases = parts, None
        if i > 0:
            inputs = parts + [self.staged[k] for k in keys]
            aliases = {n_keys + a: a for a in range(n_keys)}
        return _Exchange(inputs, shapes, plan, 3 * n_keys, aliases, finish)

    def exchange(self, keys):
        mine = [self.mine[k] for k in keys]

        def plan(ins, outs, place):
            x, y, c = place
            return [(ins[a], outs[a], (x, y, 1 - c), outs[a]) for a in range(len(keys))]

        def finish(outs):
            for k, o in zip(keys, outs):
                self.theirs[k] = o

        return _Exchange(mine, [_sds(a) for a in mine], plan, len(keys), None, finish)


def _all_gather_full(gat, keys):
    n = len(keys)
    arrays = [gat.slots[k] for k in keys]

    def body(*refs):
        ins, outs = refs[:n], refs[n:2 * n]
        send_sems, recv_sems = refs[2 * n:]
        x, y, c = _my_place()
        sibling = (x, y, 1 - c)
        chips = _other_chips(x, y)
        me = 2 * x + y

        def half(ref, q, core):
            rows = ref.shape[1] // 2
            return ref.at[q, pl.ds(core * rows, rows), :]

        def copy(a, k, src, dst, to):
            return pltpu.make_async_remote_copy(src_ref=src, dst_ref=dst, send_sem=send_sems.at[6 * a + k], recv_sem=recv_sems.at[6 * a + k],
                                                device_id=to, device_id_type=MESH)

        first = [copy(a, j, half(ins[a], me, c), half(outs[a], me, c), (*chip, c)) for a in range(n) for j, chip in enumerate(chips)]
        for cp in first:
            cp.start()
        passed = []
        for a in range(n):
            for j, (px, py) in enumerate(chips):
                landed = half(outs[a], 2 * px + py, c)
                copy(a, j, landed, landed, (px, py, c)).wait_recv()
                fwd = copy(a, 3 + j, landed, landed, sibling)
                fwd.start()
                passed.append(fwd)
        for a in range(n):
            for j, (px, py) in enumerate(chips):
                other = half(outs[a], 2 * px + py, 1 - c)
                copy(a, 3 + j, other, other, sibling).wait_recv()
        for cp in first + passed:
            cp.wait_send()

    outs = pl.pallas_call(
        body, name="all_gather_first_weights", in_specs=[HBM_SPEC] * n, out_specs=[HBM_SPEC] * n,
        out_shape=[_sds(a) for a in arrays], input_output_aliases={a: a for a in range(n)},
        scratch_shapes=[pltpu.SemaphoreType.DMA((6 * n,)), pltpu.SemaphoreType.DMA((6 * n,))])(*arrays)
    for k, o in zip(keys, outs):
        gat.slots[k] = o


def _small_all_gather(buf, done):
    peers = [(0, 0, 1), (1, 0, 0), (0, 1, 0), (1, 1, 0), (1, 0, 1), (0, 1, 1), (1, 1, 1)]

    def index(x, y, c):
        return 4 * x + 2 * y + c

    def plan(ins, outs, place):
        x, y, c = place
        remote = []
        for fx, fy, fc in peers:
            px, py, pc = (1 - x if fx else x), (1 - y if fy else y), (1 - c if fc else c)
            remote.append((ins[0], outs[0].at[index(x, y, c)], (px, py, pc), outs[0].at[index(px, py, pc)]))
        return remote

    def local(ins, outs, place):
        return [(ins[0], outs[0].at[index(*place)])]

    return _Exchange([buf], [jax.ShapeDtypeStruct((2 * N_CHIPS,) + buf.shape, buf.dtype)], plan, len(peers), None,
                     lambda outs: done(outs[0]), local, 1)


WEIGHT_NAMES = ("ffn1_norm", "ffn1_w_in", "ffn1_w_out", "mix_norm", "ffn2_norm", "ffn2_w_in", "ffn2_w_out", "ab_w_in", "pool_w", "pool_b",
                "pool_scale", "conv_w", "conv_b", "conv_ln_g", "conv_ln_b", "ab_w_out", "sgu_w_in", "sgu_ln_g", "sgu_ln_b", "sgu_w", "sgu_b",
                "sgu_w_out", "final_norm")
SMALL = tuple(n for n in WEIGHT_NAMES if n not in BIG)
SHARDED_SMALL = ("conv_w", "sgu_ln_g", "sgu_ln_b")
PACK_ROWS = 48
PACK = ("pack", 0)


def _pair(prefix, layer):
    return [(prefix + "_w_in", layer), (prefix + "_w_out", layer)]


def kernel(x, ffn1_norm, ffn1_w_in, ffn1_w_out, mix_norm, ffn2_norm, ffn2_w_in, ffn2_w_out, ab_w_in, pool_w, pool_b, pool_scale, conv_w, conv_b, conv_ln_g, conv_ln_b, ab_w_out, sgu_w_in, sgu_ln_g, sgu_ln_b, sgu_w, sgu_b, sgu_w_out, final_norm, loss_target, m_ffn1_norm, m_ffn1_w_in, m_ffn1_w_out, m_mix_norm, m_ffn2_norm, m_ffn2_w_in, m_ffn2_w_out, m_ab_w_in, m_pool_w, m_pool_b, m_pool_scale, m_conv_w, m_conv_b, m_conv_ln_g, m_conv_ln_b, m_ab_w_out, m_sgu_w_in, m_sgu_ln_g, m_sgu_ln_b, m_sgu_w, m_sgu_b, m_sgu_w_out, m_final_norm, v_ffn1_norm, v_ffn1_w_in, v_ffn1_w_out, v_mix_norm, v_ffn2_norm, v_ffn2_w_in, v_ffn2_w_out, v_ab_w_in, v_pool_w, v_pool_b, v_pool_scale, v_conv_w, v_conv_b, v_conv_ln_g, v_conv_ln_b, v_ab_w_out, v_sgu_w_in, v_sgu_ln_g, v_sgu_ln_b, v_sgu_w, v_sgu_b, v_sgu_w_out, v_final_norm):
    given = dict(locals())
    w = {n: given[n] for n in WEIGHT_NAMES}
    chip = 2 * lax.axis_index("x") + lax.axis_index("y")
    me = chip.astype(jnp.int32).reshape(1)
    core = lax.axis_index("c").astype(jnp.int32).reshape(1)
    row = lambda v: v.reshape(1, -1)
    xin, tgt = x[0], loss_target[0]

    pack = jnp.concatenate([
        w["conv_w"][0], jnp.zeros((1, 128), F32), w["sgu_ln_g"].reshape(2, 128), w["sgu_ln_b"].reshape(2, 128),
        jnp.zeros((PACK_ROWS - 36, 128), F32)], axis=0)
    slots = {PACK: lax.dynamic_update_slice(jnp.zeros((N_CHIPS, PACK_ROWS, 128), F32), pack[None], (me[0], 0, 0))}
    for n in BIG:
        for layer in range(w[n].shape[0]):
            slots[(n, layer)] = _cast_into_slot(w[n], layer, me, f"cast_{n}_{layer}")
    gat = _Gatherer(slots)
    _all_gather_full(gat, _pair("ffn1", 0) + _pair("ab", 0) + [PACK])
    gp = gat.slots[PACK]
    conv_w_full = jnp.transpose(gp[:, 0:CONV_WIDTH], (1, 0, 2)).reshape(CONV_WIDTH, N_CHIPS * 128)
    sgu_ln_g_full = gp[:, 32:34].reshape(1, -1)
    sgu_ln_b_full = gp[:, 34:36].reshape(1, -1)
    gw = lambda n, layer: gat.slots[(n, layer)]

    st = [dict(), dict()]
    st[0]["xa"] = xin
    later = _pair("sgu", 0) + _pair("ffn2", 1)
    cur, st[0]["h1"], st[0]["xn1"] = _ffn_fwd(xin, row(w["ffn1_norm"][0]), gw("ffn1_w_in", 0), gw("ffn1_w_out", 0), "ffn1_fwd_0",
                                              comms=[gat.ici(_pair("ffn2", 0))])
    st[0]["xb"] = cur
    st[0]["h0"], st[0]["xnm"] = _norm_matmul(cur, row(w["mix_norm"][0]), gw("ab_w_in", 0), "mix0_proj_in",
                                             comms=[gat.d2d(_pair("ffn2", 0))])
    pool_args = (w["pool_w"][0], row(w["pool_b"][0]), row(w["pool_scale"][0]), conv_w_full, row(w["conv_b"][0]), row(w["conv_ln_g"][0]),
                 row(w["conv_ln_b"][0]), gw("ab_w_out", 0))
    cur, st[0]["ycat"], st[0]["yconv"] = _mix0_fwd(cur, st[0]["h0"], *pool_args, "mix0_fwd", comms=[gat.ici(_pair("ffn1", 1))])
    st[0]["xc"] = cur
    cur, st[0]["h2"], st[0]["xn2"] = _ffn_fwd(cur, row(w["ffn2_norm"][0]), gw("ffn2_w_in", 0), gw("ffn2_w_out", 0), "ffn2_fwd_0",
                                              comms=[gat.d2d(_pair("ffn1", 1)), gat.ici(later)])
    st[1]["xa"] = cur
    cur, st[1]["h1"], st[1]["xn1"] = _ffn_fwd(cur, row(w["ffn1_norm"][1]), gw("ffn1_w_in", 1), gw("ffn1_w_out", 1), "ffn1_fwd_1",
                                              comms=[gat.d2d(later)])
    st[1]["xb"] = cur
    st[1]["pre"], st[1]["xnm"] = _norm_matmul(cur, row(w["mix_norm"][1]), gw("sgu_w_in", 0), "sgu_proj_in")
    sgu_args = (sgu_ln_g_full, sgu_ln_b_full, w["sgu_w"][0], w["sgu_b"][0].T)
    cur, st[1]["p"] = _sgu_fwd(cur, st[1]["pre"], *sgu_args, gw("sgu_w_out", 0), "sgu_fwd")
    st[1]["xc"] = cur
    cur, st[1]["h2"], st[1]["xn2"] = _ffn_fwd(cur, row(w["ffn2_norm"][1]), gw("ffn2_w_in", 1), gw("ffn2_w_out", 1), "ffn2_fwd_1")
    dy, loss, d_final = _final_loss(cur, tgt, row(w["final_norm"]), "final_loss")

    red = _Reducer(me, core)
    small = {"final_norm": d_final.reshape(-1)}
    norm_grads = {"ffn1_norm": [None] * DEPTH, "mix_norm": [None] * DEPTH, "ffn2_norm": [None] * DEPTH}
    ga, gb, gc, gd, ge, gf = _pair("ffn2", 1), _pair("sgu", 0), _pair("ffn1", 1), _pair("ffn2", 0), _pair("ab", 0), _pair("ffn1", 0)

    def ffn_backward(prefix, layer, xs, hs, xns, dy_in, bwd_comms=(), dwin_comms=(), dwout_comms=()):
        dx, dh, act, norm_grads[prefix + "_norm"][layer] = _ffn_bwd(
            xs, dy_in, hs, row(w[prefix + "_norm"][layer]), gw(prefix + "_w_in", layer), gw(prefix + "_w_out", layer),
            f"{prefix}_bwd_{layer}", comms=bwd_comms)
        red.add((prefix + "_w_in", layer), _tn_matmul(xns, dh, 1.0, 1024, 1408, f"{prefix}_dwin_{layer}", comms=dwin_comms))
        red.add((prefix + "_w_out", layer), _tn_matmul(act, dy_in, 0.5, 1408, 1024, f"{prefix}_dwout_{layer}", comms=dwout_comms))
        return dx

    s1, s0 = st[1], st[0]
    dy = ffn_backward("ffn2", 1, s1["xc"], s1["h2"], s1["xn2"], dy)
    dy_in = dy
    dy, dpre, norm_grads["mix_norm"][1], dlg, dlb, dw, dbt = _sgu_bwd(
        s1["xb"], dy_in, s1["pre"], row(w["mix_norm"][1]), *sgu_args, gw("sgu_w_in", 0), gw("sgu_w_out", 0), "sgu_bwd", comms=[red.swap(ga)])
    red.add(("sgu_w_in", 0), _tn_matmul(s1["xnm"], dpre, 1.0, 1024, 2048, "sgu_dwin"))
    red.add(("sgu_w_out", 0), _tn_matmul(s1["p"], dy_in, 1.0, 1024, 1024, "sgu_dwout"))
    small.update(sgu_ln_g=dlg, sgu_ln_b=dlb, sgu_w=dw[None], sgu_b=dbt.T[None])
    dy = ffn_backward("ffn1", 1, s1["xa"], s1["h1"], s1["xn1"], dy, bwd_comms=[lambda: red.scatter(ga), lambda: red.swap(gb)],
                      dwin_comms=[lambda: red.scatter(gb), lambda: red.exchange(ga)])
    dy = ffn_backward("ffn2", 0, s0["xc"], s0["h2"], s0["xn2"], dy, bwd_comms=[lambda: red.swap(gc), lambda: red.exchange(gb)],
                      dwin_comms=[lambda: red.scatter(gc)])
    dy_in = dy
    dconv, dpc, dpw, rows = _mix0_bwd_a(dy_in, s0["h0"], s0["yconv"], *pool_args, "mix0_bwd_a")
    dy, dh0, norm_grads["mix_norm"][0] = _mix0_bwd_b(s0["xb"], dy_in, s0["h0"], dconv, dpc, row(w["mix_norm"][0]), conv_w_full,
                                                      gw("ab_w_in", 0), "mix0_bwd_b")
    red.add(("ab_w_in", 0), _tn_matmul(s0["xnm"], dh0, 1.0, 1024, 1536, "ab_dwin", comms=[red.swap(gd), red.exchange(gc)]))
    red.add(("ab_w_out", 0), _tn_matmul(s0["ycat"], dy_in, 1.0, 1024, 1024, "ab_dwout"))
    small.update(pool_w=dpw[None], conv_w=rows[None, 0:CONV_WIDTH], conv_b=rows[32:33], conv_ln_g=rows[33:34], conv_ln_b=rows[34:35],
                 pool_scale=rows[35:36], pool_b=rows[36:37].reshape(1, len(POOL_WINDOWS), POOL_GC))

    small_sum = {}

    def small_ready():
        for k, v in norm_grads.items():
            small[k] = jnp.concatenate(v, axis=0)
        flat = [small[n].reshape(-1, 128) for n in SMALL]
        rows = sum(f.shape[0] for f in flat)
        loss_block = jnp.pad(loss, ((0, 8 + (-rows) % 8 - 1), (0, 127)))
        buf = jnp.concatenate(flat + [loss_block], axis=0)

        def done(gathered):
            total, at = _sum_leading(gathered, "reduce_small"), 0
            for n, f in zip(SMALL, flat):
                small_sum[n] = total[at:at + f.shape[0]].reshape(small[n].shape)
                at += f.shape[0]
            small_sum["loss"] = total[at:at + 1, 0:1]

        return _small_all_gather(buf, done)

    dx, dh, act, norm_grads["ffn1_norm"][0] = _ffn_bwd(
        s0["xa"], dy, s0["h1"], row(w["ffn1_norm"][0]), gw("ffn1_w_in", 0), gw("ffn1_w_out", 0), "ffn1_bwd_0",
        comms=[red.scatter(gd), red.swap(ge)])
    red.add(("ffn1_w_in", 0), _tn_matmul(s0["xn1"], dh, 1.0, 1024, 1408, "ffn1_dwin_0",
                                         comms=[red.scatter(ge), red.exchange(gd), small_ready()]))
    red.add(("ffn1_w_out", 0), _tn_matmul(act, dy, 0.5, 1408, 1024, "ffn1_dwout_0", comms=[red.exchange(ge)]))
    grad_x = dx

    big_out = {}

    def adamw_big(n, layer, comms=()):
        big_out[n] = _adamw_sharded(w[n], red.mine[(n, layer)], red.theirs[(n, layer)], given["m_" + n], given["v_" + n], core, layer,
                                    big_out.get(n), f"adamw_{n}_{layer}", comms=comms)

    f_in, f_out = [gf[0]], [gf[1]]
    adamw_big("ffn2_w_in", 1, [red.swap(gf)])
    adamw_big("ffn1_w_in", 1, [red.scatter(f_in, (0, 4))])
    adamw_big("ffn2_w_out", 1, [red.scatter(f_in, (1, 4))])
    adamw_big("ffn1_w_out", 1, [red.scatter(f_in, (2, 4))])
    _exchange_alone("scatter_last_grads", [red.scatter(f_in, (3, 4)), red.scatter(f_out)])
    _exchange_alone("exchange_last_grads", [red.exchange(gf)])
    for n in BIG:
        adamw_big(n, 0)

    loss = small_sum["loss"][0, 0]
    grads, delta, new_m, new_v = {}, {}, {}, {}
    for n in WEIGHT_NAMES:
        mom, var = given["m_" + n], given["v_" + n]
        if n in BIG:
            grads[n], delta[n], new_m[n], new_v[n] = big_out[n]
            continue
        g = small_sum[n]
        if n in SHARDED_SMALL:
            width = w[n].shape[-1]
            g = lax.dynamic_slice_in_dim(g, chip * width, width, axis=g.ndim - 1)
        grads[n] = g
        delta[n], new_m[n], new_v[n] = _adamw(w[n], g, mom, var, f"adamw_{n}")
    return (loss, grad_x[None], *[grads[n] for n in WEIGHT_NAMES], *[delta[n] for n in WEIGHT_NAMES],
            *[new_m[n] for n in WEIGHT_NAMES], *[new_v[n] for n in WEIGHT_NAMES])
```

```python
import jax
import jax.numpy as jnp
from jax import lax
from jax.experimental import pallas as pl
from jax.experimental.pallas import tpu as pltpu

F32, BF16 = jnp.float32, jnp.bfloat16
EPS = 1e-6
N_CHIPS = 4
POOL_WINDOWS = (2, 4, 8, 16)
POOL_GC = 128
POOL_CH = 512
CONV_CH = 512
CONV_WIDTH = 31
HALO = 32
SGU_HEADS = 8
CHUNK = 128
DEPTH = 2
ADAM_LR, ADAM_B1, ADAM_B2, ADAM_EPS, ADAM_WD, ADAM_STEP = 0.001, 0.9, 0.999, 1e-08, 0.01, 10
VMEM_LIMIT_BYTES = 60 * 1024 * 1024
MESH_AXES = ("x", "y", "c")
MESH = pl.DeviceIdType.MESH
HBM_SPEC = pl.BlockSpec(memory_space=pltpu.HBM)


def _sds(a):
    return jax.ShapeDtypeStruct(a.shape, a.dtype)


def _cparams_nd(n):
    return pltpu.CompilerParams(dimension_semantics=("arbitrary",) * n, vmem_limit_bytes=VMEM_LIMIT_BYTES)


def _cparams():
    return _cparams_nd(1)


def _dot(a, b):
    return jnp.dot(a, b, preferred_element_type=F32)


def _dot_nt(a, b):
    return lax.dot_general(a, b, (((1,), (1,)), ((), ())), preferred_element_type=F32)


def _dot_tn(a, b):
    return lax.dot_general(a, b, (((0,), (0,)), ((), ())), preferred_element_type=F32)


def _rms_fwd(x):
    r = lax.rsqrt(jnp.mean(x * x, axis=-1, keepdims=True) + EPS)
    return x * r, r


def _rms_bwd(dxn, xh, r, g):
    dxh = dxn * g
    return r * (dxh - xh * jnp.mean(dxh * xh, axis=-1, keepdims=True))


def _ln_fwd(y):
    mu = jnp.mean(y, axis=-1, keepdims=True)
    yc = y - mu
    rs = lax.rsqrt(jnp.mean(yc * yc, axis=-1, keepdims=True) + EPS)
    return yc * rs, rs


def _ln_bwd(dyhat, yhat, rs):
    return rs * (dyhat - jnp.mean(dyhat, axis=-1, keepdims=True) - yhat * jnp.mean(dyhat * yhat, axis=-1, keepdims=True))


def _sigmoid(x):
    return 0.5 * jnp.tanh(0.5 * x) + 0.5


def _const_spec(shape):
    n = len(shape)
    return pl.BlockSpec(shape, lambda i: (0,) * n)


def _row_spec(tm, cols):
    return pl.BlockSpec((tm, cols), lambda i: (i, 0))


def _my_place():
    return lax.axis_index("x"), lax.axis_index("y"), lax.axis_index("c")


def _other_chips(x, y):
    return [(1 - x, y), (x, 1 - y), (1 - x, 1 - y)]


class _Exchange:
    def __init__(self, inputs, out_shapes, plan, count, aliases=None, finish=None, local=None, n_local=0):
        self.inputs, self.out_shapes, self.plan, self.count = list(inputs), list(out_shapes), plan, count
        self.aliases, self.finish, self.local, self.n_local = dict(aliases or {}), finish, local, n_local


def _call(body, *, name, grid, in_specs, out_specs, out_shape, args, scratch_shapes=(), comms=(), scalar=None, aliases=None):
    comms = [cm if isinstance(cm, _Exchange) else cm() for cm in comms]
    in_specs, out_specs, out_shape, scratch_shapes = list(in_specs), list(out_specs), list(out_shape), list(scratch_shapes)
    n_in, n_out, n_scr = len(in_specs), len(out_specs), len(scratch_shapes)
    c_in = [a for cm in comms for a in cm.inputs]
    c_out = [s for cm in comms for s in cm.out_shapes]
    n_remote = sum(cm.count for cm in comms)
    n_local = sum(cm.n_local for cm in comms)
    n_scalar = 0 if scalar is None else 1
    all_aliases = {n_scalar + i: o for i, o in (aliases or {}).items()}
    at_in, at_out = n_scalar + n_in, n_out
    for cm in comms:
        for i, o in cm.aliases.items():
            all_aliases[at_in + i] = at_out + o
        at_in += len(cm.inputs)
        at_out += len(cm.out_shapes)

    def wrapped(*all_refs):
        scalar_ref, refs = all_refs[:n_scalar], all_refs[n_scalar:]
        ins, ci = refs[:n_in], refs[n_in:n_in + len(c_in)]
        at = n_in + len(c_in)
        outs, co = refs[at:at + n_out], refs[at + n_out:at + n_out + len(c_out)]
        at += n_out + len(c_out)
        scr = refs[at:at + n_scr]

        def run_body():
            body(*scalar_ref, *ins, *outs, *scr)

        if not comms:
            run_body()
            return
        send_sems, recv_sems, local_sems = refs[at + n_scr:]
        place = _my_place()
        sends, arrivals, locals_ = [], [], []
        i0 = o0 = 0
        for cm in comms:
            cm_in, cm_out = ci[i0:i0 + len(cm.inputs)], co[o0:o0 + len(cm.out_shapes)]
            i0 += len(cm.inputs)
            o0 += len(cm.out_shapes)
            for src, dst, dev, incoming in cm.plan(cm_in, cm_out, place):
                k = len(sends)
                sends.append(pltpu.make_async_remote_copy(src_ref=src, dst_ref=dst, send_sem=send_sems.at[k], recv_sem=recv_sems.at[k],
                                                          device_id=dev, device_id_type=MESH))
                arrivals.append(pltpu.make_async_remote_copy(src_ref=src, dst_ref=incoming, send_sem=send_sems.at[k],
                                                             recv_sem=recv_sems.at[k], device_id=dev, device_id_type=MESH))
            if cm.local is not None:
                for src, dst in cm.local(cm_in, cm_out, place):
                    locals_.append(pltpu.make_async_copy(src, dst, local_sems.at[len(locals_)]))

        def start():
            for cp in locals_ + sends:
                cp.start()

        def finish():
            for cp in arrivals:
                cp.wait_recv()
            for cp in sends:
                cp.wait_send()
            for cp in locals_:
                cp.wait()

        if not grid:
            start()
            run_body()
            finish()
            return
        ids = [pl.program_id(a) for a in range(len(grid))]
        first, last = ids[0] == 0, ids[0] == grid[0] - 1
        for a in range(1, len(grid)):
            first = jnp.logical_and(first, ids[a] == 0)
            last = jnp.logical_and(last, ids[a] == grid[a] - 1)
        pl.when(first)(start)
        run_body()
        pl.when(last)(finish)

    sems = []
    if comms:
        sems = [pltpu.SemaphoreType.DMA((max(n_remote, 1),)), pltpu.SemaphoreType.DMA((max(n_remote, 1),)),
                pltpu.SemaphoreType.DMA((max(n_local, 1),))]
    all_in, all_out = in_specs + [HBM_SPEC] * len(c_in), out_specs + [HBM_SPEC] * len(c_out)
    if scalar is None:
        kwargs = dict(grid=grid, compiler_params=_cparams_nd(len(grid))) if grid else {}
        res = pl.pallas_call(
            wrapped, name=name, in_specs=all_in, out_specs=all_out, out_shape=out_shape + c_out, scratch_shapes=scratch_shapes + sems,
            input_output_aliases=all_aliases, **kwargs)(*args, *c_in)
    else:
        spec = pltpu.PrefetchScalarGridSpec(num_scalar_prefetch=1, grid=grid, in_specs=all_in, out_specs=all_out,
                                            scratch_shapes=scratch_shapes + sems)
        res = pl.pallas_call(
            wrapped, name=name, grid_spec=spec, out_shape=out_shape + c_out, input_output_aliases=all_aliases,
            compiler_params=_cparams_nd(len(grid)))(scalar, *args, *c_in)
    at = n_out
    for cm in comms:
        got = res[at:at + len(cm.out_shapes)]
        at += len(cm.out_shapes)
        if cm.finish is not None:
            cm.finish(got)
    return list(res[:n_out])


def _exchange_alone(name, comms):
    _call(lambda: None, name=name, grid=(), in_specs=[], out_specs=[], out_shape=[], args=[], comms=comms)


def _in_weight_copies(w_hbm, w_v, sem, base=0):
    return [pltpu.make_async_copy(w_hbm.at[q], w_v.at[q], sem.at[base + q]) for q in range(N_CHIPS)]


def _out_weight_copies(w_hbm, w_v, sem, base=0):
    rows = w_hbm.shape[1]
    return [pltpu.make_async_copy(w_hbm.at[q], w_v.at[pl.ds(q * rows, rows)], sem.at[base + q]) for q in range(N_CHIPS)]


def _load_at_first_step(copies):
    @pl.when(pl.program_id(0) == 0)
    def _():
        for cp in copies:
            cp.start()
        for cp in copies:
            cp.wait()


def _ffn_fwd(x, g, win_g, wout_g, name, comms=()):
    t, d = x.shape
    c = win_g.shape[-1]
    ff = 2 * c
    tm = min(256, t)

    def body(x_ref, g_ref, win_hbm, wout_hbm, xo_ref, h_ref, xn_ref, win_v, wout_v, sem):
        _load_at_first_step(_in_weight_copies(win_hbm, win_v, sem) + _out_weight_copies(wout_hbm, wout_v, sem, N_CHIPS))
        xv = x_ref[...]
        xh, _ = _rms_fwd(xv)
        xn = (xh * g_ref[...]).astype(BF16)
        xn_ref[...] = xn
        acc = jnp.zeros((tm, d), F32)
        for j in range(2):
            gate = _dot(xn, win_v[j])
            up = _dot(xn, win_v[j + 2])
            h_ref[:, j * c:(j + 1) * c] = gate.astype(BF16)
            h_ref[:, ff + j * c:ff + (j + 1) * c] = up.astype(BF16)
            act = (gate * _sigmoid(gate) * up).astype(BF16)
            acc = acc + _dot(act, wout_v[j * c:(j + 1) * c, :])
        xo_ref[...] = xv + 0.5 * acc

    return _call(
        body, name=name, grid=(t // tm,),
        in_specs=[_row_spec(tm, d), _const_spec((1, d)), HBM_SPEC, HBM_SPEC],
        out_specs=[_row_spec(tm, d), _row_spec(tm, 2 * ff), _row_spec(tm, d)],
        out_shape=[jax.ShapeDtypeStruct((t, d), F32), jax.ShapeDtypeStruct((t, 2 * ff), BF16), jax.ShapeDtypeStruct((t, d), BF16)],
        scratch_shapes=[pltpu.VMEM((N_CHIPS, d, c), BF16), pltpu.VMEM((ff, d), BF16), pltpu.SemaphoreType.DMA((2 * N_CHIPS,))],
        args=(x, g, win_g, wout_g), comms=comms)


def _ffn_bwd(x, dy, h, g, win_g, wout_g, name, comms=()):
    t, d = x.shape
    c = win_g.shape[-1]
    ff = 2 * c
    tm = min(256, t)

    def body(x_ref, dy_ref, h_ref, g_ref, win_hbm, wout_hbm, dx_ref, dh_ref, act_ref, dg_ref, win_v, wout_v, sem):
        _load_at_first_step(_in_weight_copies(win_hbm, win_v, sem) + _out_weight_copies(wout_hbm, wout_v, sem, N_CHIPS))

        @pl.when(pl.program_id(0) == 0)
        def _():
            dg_ref[...] = jnp.zeros_like(dg_ref)

        xv, dyv, gv = x_ref[...], dy_ref[...], g_ref[...]
        xh, r = _rms_fwd(xv)
        dyh = (0.5 * dyv).astype(BF16)
        dxn = jnp.zeros((tm, d), F32)
        for j in range(2):
            gate = h_ref[:, j * c:(j + 1) * c].astype(F32)
            up = h_ref[:, ff + j * c:ff + (j + 1) * c].astype(F32)
            dact = _dot_nt(dyh, wout_v[j * c:(j + 1) * c, :])
            s = _sigmoid(gate)
            sl = gate * s
            act_ref[:, j * c:(j + 1) * c] = (sl * up).astype(BF16)
            dgate = (dact * up * (s * (1.0 + gate * (1.0 - s)))).astype(BF16)
            dup = (dact * sl).astype(BF16)
            dh_ref[:, j * c:(j + 1) * c] = dgate
            dh_ref[:, ff + j * c:ff + (j + 1) * c] = dup
            dxn = dxn + _dot_nt(dgate, win_v[j]) + _dot_nt(dup, win_v[j + 2])
        dg_ref[...] += jnp.sum(dxn * xh, axis=0, keepdims=True)
        dx_ref[...] = dyv + _rms_bwd(dxn, xh, r, gv)

    return _call(
        body, name=name, grid=(t // tm,),
        in_specs=[_row_spec(tm, d), _row_spec(tm, d), _row_spec(tm, 2 * ff), _const_spec((1, d)), HBM_SPEC, HBM_SPEC],
        out_specs=[_row_spec(tm, d), _row_spec(tm, 2 * ff), _row_spec(tm, ff), _const_spec((1, d))],
        out_shape=[jax.ShapeDtypeStruct((t, d), F32), jax.ShapeDtypeStruct((t, 2 * ff), BF16), jax.ShapeDtypeStruct((t, ff), BF16),
                   jax.ShapeDtypeStruct((1, d), F32)],
        scratch_shapes=[pltpu.VMEM((N_CHIPS, d, c), BF16), pltpu.VMEM((ff, d), BF16), pltpu.SemaphoreType.DMA((2 * N_CHIPS,))],
        args=(x, dy, h, g, win_g, wout_g), comms=comms)


def _norm_matmul(x, g, win_g, name, comms=()):
    t, d = x.shape
    c = win_g.shape[-1]
    tm = min(512, t)

    def body(x_ref, g_ref, win_hbm, o_ref, xn_ref, win_v, sem):
        _load_at_first_step(_in_weight_copies(win_hbm, win_v, sem))
        xh, _ = _rms_fwd(x_ref[...])
        xn = (xh * g_ref[...]).astype(BF16)
        xn_ref[...] = xn
        for q in range(N_CHIPS):
            o_ref[:, q * c:(q + 1) * c] = _dot(xn, win_v[q])

    return _call(
        body, name=name, grid=(t // tm,),
        in_specs=[_row_spec(tm, d), _const_spec((1, d)), HBM_SPEC],
        out_specs=[_row_spec(tm, N_CHIPS * c), _row_spec(tm, d)],
        out_shape=[jax.ShapeDtypeStruct((t, N_CHIPS * c), F32), jax.ShapeDtypeStruct((t, d), BF16)],
        scratch_shapes=[pltpu.VMEM((N_CHIPS, d, c), BF16), pltpu.SemaphoreType.DMA((N_CHIPS,))],
        args=(x, g, win_g), comms=comms)


def _proj_in_bwd_tail(dh, win_v, c):
    dxn = _dot_nt(dh[:, 0:c], win_v[0])
    for q in range(1, N_CHIPS):
        dxn = dxn + _dot_nt(dh[:, q * c:(q + 1) * c], win_v[q])
    return dxn


def _prev_halo_spec(tm, cols):
    return pl.BlockSpec((HALO, cols), lambda i: (jnp.maximum(i * (tm // HALO) - 1, 0), 0))


def _next_halo_spec(tm, cols, t):
    last = t // HALO - 1
    return pl.BlockSpec((HALO, cols), lambda i: (jnp.minimum((i + 1) * (tm // HALO), last), 0))


def _shift_down(w, k):
    return w if k == 0 else pltpu.roll(w, k, 0)


def _shift_up(w, k):
    return w if k == 0 else pltpu.roll(w, w.shape[0] - k, 0)


def _pool_counts(i, tm):
    pos = (i * tm + lax.broadcasted_iota(jnp.int32, (tm, POOL_CH), 0) + 1).astype(F32)
    lane = lax.broadcasted_iota(jnp.int32, (tm, POOL_CH), 1)
    win = jnp.where(lane < POOL_GC, 2.0, jnp.where(lane < 2 * POOL_GC, 4.0, jnp.where(lane < 3 * POOL_GC, 8.0, 16.0)))
    return jnp.minimum(pos, win)


def _group_select(parts):
    return jnp.concatenate([p[:, k * POOL_GC:(k + 1) * POOL_GC] for k, p in enumerate(parts)], axis=1)


def _mix0_recompute(i, tm, h_cur, h_prev, conv_w, conv_b, y=None):
    prev = jnp.where(i > 0, h_prev, 0.0)
    win = jnp.concatenate([prev, h_cur], axis=0)
    u_w = win[:, 0:POOL_CH]
    a_w = win[:, POOL_CH:POOL_CH + CONV_CH]
    gt_w = win[:, POOL_CH + CONV_CH:]
    g_w = a_w * _sigmoid(gt_w)
    if y is None:
        y = jnp.zeros((tm, CONV_CH), F32)
        for k in range(CONV_WIDTH):
            y = y + conv_w[k:k + 1, :] * _shift_down(g_w, CONV_WIDTH - 1 - k)[HALO:, :]
        y = y + conv_b
    s2 = u_w + _shift_down(u_w, 1)
    s4 = s2 + _shift_down(s2, 2)
    s8 = s4 + _shift_down(s4, 4)
    s16 = s8 + _shift_down(s8, 8)
    sums = _group_select([s2[HALO:], s4[HALO:], s8[HALO:], s16[HALO:]])
    cnt = _pool_counts(i, tm)
    pooled = sums / cnt - h_cur[:, 0:POOL_CH]
    return g_w, y, pooled, cnt


def _pool_linear(pooled, pw_ref, pb):
    return jnp.concatenate(
        [_dot(pooled[:, k * POOL_GC:(k + 1) * POOL_GC].astype(BF16), pw_ref[k].astype(BF16)) for k in range(len(POOL_WINDOWS))], axis=1) + pb


def _mix0_fwd(x, h0, pool_w, pool_b, pool_scale, conv_w, conv_b, ln_g, ln_b, wout_g, name, comms=()):
    t, d = x.shape
    tm = min(256, t)
    hc = h0.shape[1]

    def body(x_ref, h_ref, hp_ref, pw_ref, pb_ref, ps_ref, cw_ref, cb_ref, lg_ref, lb_ref, wout_hbm, xo_ref, ycat_ref, y_ref, wout_v, sem):
        _load_at_first_step(_out_weight_copies(wout_hbm, wout_v, sem))
        i = pl.program_id(0)
        _, y, pooled, _ = _mix0_recompute(i, tm, h_ref[...], hp_ref[...], cw_ref[...], cb_ref[...])
        y_ref[...] = y
        yhat, _ = _ln_fwd(y)
        yn = yhat * lg_ref[...] + lb_ref[...]
        yb = yn * _sigmoid(yn)
        ya = _pool_linear(pooled, pw_ref, pb_ref[...]) * ps_ref[...]
        ycat = jnp.concatenate([ya, yb], axis=1).astype(BF16)
        ycat_ref[...] = ycat
        xo_ref[...] = x_ref[...] + _dot(ycat, wout_v[...])

    return _call(
        body, name=name, grid=(t // tm,),
        in_specs=[_row_spec(tm, d), _row_spec(tm, hc), _prev_halo_spec(tm, hc), _const_spec(pool_w.shape), _const_spec((1, POOL_CH)),
                  _const_spec((1, POOL_CH)), _const_spec(conv_w.shape), _const_spec((1, CONV_CH)), _const_spec((1, CONV_CH)),
                  _const_spec((1, CONV_CH)), HBM_SPEC],
        out_specs=[_row_spec(tm, d), _row_spec(tm, d), _row_spec(tm, CONV_CH)],
        out_shape=[jax.ShapeDtypeStruct((t, d), F32), jax.ShapeDtypeStruct((t, d), BF16), jax.ShapeDtypeStruct((t, CONV_CH), F32)],
        scratch_shapes=[pltpu.VMEM((d, d), BF16), pltpu.SemaphoreType.DMA((N_CHIPS,))],
        args=(x, h0, h0, pool_w, pool_b, pool_scale, conv_w, conv_b, ln_g, ln_b, wout_g), comms=comms)


def _mix0_bwd_a(dy, h0, y_conv, pool_w, pool_b, pool_scale, conv_w, conv_b, ln_g, ln_b, wout_g, name):
    t, d = dy.shape
    tm = min(256, t)
    hc = h0.shape[1]
    n_small = 40

    def body(dy_ref, h_ref, hp_ref, y_ref, pw_ref, pb_ref, ps_ref, cw_ref, cb_ref, lg_ref, lb_ref, wout_hbm,
             dconv_ref, dpc_ref, dpw_ref, small_ref, wout_v, sem):
        _load_at_first_step(_out_weight_copies(wout_hbm, wout_v, sem))
        i = pl.program_id(0)

        @pl.when(i == 0)
        def _():
            dpw_ref[...] = jnp.zeros_like(dpw_ref)
            small_ref[...] = jnp.zeros_like(small_ref)

        g_w, y, pooled, cnt = _mix0_recompute(i, tm, h_ref[...], hp_ref[...], cw_ref[...], cb_ref[...], y_ref[...])
        yhat, rs = _ln_fwd(y)
        lg = lg_ref[...]
        yn = yhat * lg + lb_ref[...]
        mixed = _pool_linear(pooled, pw_ref, pb_ref[...])
        dycat = _dot_nt(dy_ref[...].astype(BF16), wout_v[...])
        dya, dyb = dycat[:, 0:POOL_CH], dycat[:, POOL_CH:]
        sg = _sigmoid(yn)
        dyn = dyb * (sg * (1.0 + yn * (1.0 - sg)))
        dyc = _ln_bwd(dyn * lg, yhat, rs)
        dconv_ref[...] = dyc

        def add_row(k, value):
            small_ref[k:k + 1, :] += jnp.sum(value, axis=0, keepdims=True)

        for k in range(CONV_WIDTH):
            add_row(k, dyc * _shift_down(g_w, CONV_WIDTH - 1 - k)[HALO:, :])
        add_row(32, dyc)
        add_row(33, dyn * yhat)
        add_row(34, dyn)
        scale = ps_ref[...]
        dmixed = dya * scale
        add_row(35, dya * mixed)
        add_row(36, dmixed)
        dmb = dmixed.astype(BF16)
        dpooled = []
        for k in range(len(POOL_WINDOWS)):
            sl = slice(k * POOL_GC, (k + 1) * POOL_GC)
            dpw_ref[k] += _dot_tn(pooled[:, sl].astype(BF16), dmb[:, sl])
            dpooled.append(_dot_nt(dmb[:, sl], pw_ref[k].astype(BF16)))
        dpc_ref[...] = jnp.concatenate(dpooled, axis=1) / cnt

    return _call(
        body, name=name, grid=(t // tm,),
        in_specs=[_row_spec(tm, d), _row_spec(tm, hc), _prev_halo_spec(tm, hc), _row_spec(tm, CONV_CH), _const_spec(pool_w.shape),
                  _const_spec((1, POOL_CH)), _const_spec((1, POOL_CH)), _const_spec(conv_w.shape), _const_spec((1, CONV_CH)),
                  _const_spec((1, CONV_CH)), _const_spec((1, CONV_CH)), HBM_SPEC],
        out_specs=[_row_spec(tm, CONV_CH), _row_spec(tm, POOL_CH), _const_spec(pool_w.shape), _const_spec((n_small, CONV_CH))],
        out_shape=[jax.ShapeDtypeStruct((t, CONV_CH), F32), jax.ShapeDtypeStruct((t, POOL_CH), F32),
                   jax.ShapeDtypeStruct(pool_w.shape, F32), jax.ShapeDtypeStruct((n_small, CONV_CH), F32)],
        scratch_shapes=[pltpu.VMEM((d, d), BF16), pltpu.SemaphoreType.DMA((N_CHIPS,))],
        args=(dy, h0, h0, y_conv, pool_w, pool_b, pool_scale, conv_w, conv_b, ln_g, ln_b, wout_g))


def _mix0_bwd_b(x, dy, h0, dconv, dpc, g, conv_w, win_g, name):
    t, d = x.shape
    tm = min(256, t)
    hc = h0.shape[1]
    c = win_g.shape[-1]
    n_tiles = t // tm

    def body(x_ref, dy_ref, h_ref, dc_ref, dcn_ref, dp_ref, dpn_ref, g_ref, cw_ref, win_hbm, dx_ref, dh_ref, dg_ref, win_v, sem):
        _load_at_first_step(_in_weight_copies(win_hbm, win_v, sem))
        i = pl.program_id(0)

        @pl.when(i == 0)
        def _():
            dg_ref[...] = jnp.zeros_like(dg_ref)

        not_last = i < n_tiles - 1
        dc_w = jnp.concatenate([dc_ref[...], jnp.where(not_last, dcn_ref[...], 0.0)], axis=0)
        dp_w = jnp.concatenate([dp_ref[...], jnp.where(not_last, dpn_ref[...], 0.0)], axis=0)
        cw = cw_ref[...]
        dg = jnp.zeros((tm, CONV_CH), F32)
        for k in range(CONV_WIDTH):
            dg = dg + cw[k:k + 1, :] * _shift_up(dc_w, CONV_WIDTH - 1 - k)[0:tm, :]
        a2 = dp_w + _shift_up(dp_w, 1)
        a4 = a2 + _shift_up(a2, 2)
        a8 = a4 + _shift_up(a4, 4)
        a16 = a8 + _shift_up(a8, 8)
        back = _group_select([a2[0:tm], a4[0:tm], a8[0:tm], a16[0:tm]])
        du = back - dp_ref[...] * _pool_counts(i, tm)
        hv = h_ref[...]
        a = hv[:, POOL_CH:POOL_CH + CONV_CH]
        sig = _sigmoid(hv[:, POOL_CH + CONV_CH:])
        dh = jnp.concatenate([du, dg * sig, dg * a * sig * (1.0 - sig)], axis=1).astype(BF16)
        dh_ref[...] = dh
        dxn = _proj_in_bwd_tail(dh, win_v, c)
        xh, r = _rms_fwd(x_ref[...])
        dg_ref[...] += jnp.sum(dxn * xh, axis=0, keepdims=True)
        dx_ref[...] = dy_ref[...] + _rms_bwd(dxn, xh, r, g_ref[...])

    return _call(
        body, name=name, grid=(n_tiles,),
        in_specs=[_row_spec(tm, d), _row_spec(tm, d), _row_spec(tm, hc), _row_spec(tm, CONV_CH), _next_halo_spec(tm, CONV_CH, t),
                  _row_spec(tm, POOL_CH), _next_halo_spec(tm, POOL_CH, t), _const_spec((1, d)), _const_spec(conv_w.shape), HBM_SPEC],
        out_specs=[_row_spec(tm, d), _row_spec(tm, hc), _const_spec((1, d))],
        out_shape=[jax.ShapeDtypeStruct((t, d), F32), jax.ShapeDtypeStruct((t, hc), BF16), jax.ShapeDtypeStruct((1, d), F32)],
        scratch_shapes=[pltpu.VMEM((N_CHIPS, d, c), BF16), pltpu.SemaphoreType.DMA((N_CHIPS,))],
        args=(x, dy, h0, dconv, dconv, dpc, dpc, g, conv_w, win_g))


SQRT_HALF = 0.7071067811865476
INV_SQRT_2PI = 0.3989422804014327


def _causal_mask():
    return (lax.broadcasted_iota(jnp.int32, (CHUNK, CHUNK), 1) <= lax.broadcasted_iota(jnp.int32, (CHUNK, CHUNK), 0)).astype(F32)


def _sgu_recompute(pre, lg, lb):
    half = pre.shape[1] // 2
    phi = 0.5 * (1.0 + lax.erf(pre * SQRT_HALF))
    z = pre * phi
    u, v = z[:, 0:half], z[:, half:]
    vhat, rs = _ln_fwd(v)
    return u, vhat, rs, vhat * lg + lb, phi


def _sgu_spatial(vln, w_ref, bt, tm):
    mask = _causal_mask()
    wm = [(w_ref[hd] * mask).astype(BF16) for hd in range(SGU_HEADS)]
    vb = vln.astype(BF16)
    rows = []
    for ch in range(tm // CHUNK):
        blocks = [_dot(wm[hd], vb[ch * CHUNK:(ch + 1) * CHUNK, hd * CHUNK:(hd + 1) * CHUNK]) + bt[:, hd:hd + 1] for hd in range(SGU_HEADS)]
        rows.append(jnp.concatenate(blocks, axis=1))
    return jnp.concatenate(rows, axis=0), wm


def _sgu_fwd(x, pre, ln_g, ln_b, w, bt, wout_g, name):
    t, d = x.shape
    tm = min(256, t)
    pc = pre.shape[1]

    def body(x_ref, pre_ref, lg_ref, lb_ref, w_ref, bt_ref, wout_hbm, xo_ref, p_ref, wout_v, sem):
        _load_at_first_step(_out_weight_copies(wout_hbm, wout_v, sem))
        u, _, _, vln, _ = _sgu_recompute(pre_ref[...], lg_ref[...], lb_ref[...])
        vo, _ = _sgu_spatial(vln, w_ref, bt_ref[...], tm)
        p = (u * vo).astype(BF16)
        p_ref[...] = p
        xo_ref[...] = x_ref[...] + _dot(p, wout_v[...])

    return _call(
        body, name=name, grid=(t // tm,),
        in_specs=[_row_spec(tm, d), _row_spec(tm, pc), _const_spec((1, d)), _const_spec((1, d)), _const_spec(w.shape),
                  _const_spec(bt.shape), HBM_SPEC],
        out_specs=[_row_spec(tm, d), _row_spec(tm, d)],
        out_shape=[jax.ShapeDtypeStruct((t, d), F32), jax.ShapeDtypeStruct((t, d), BF16)],
        scratch_shapes=[pltpu.VMEM((d, d), BF16), pltpu.SemaphoreType.DMA((N_CHIPS,))],
        args=(x, pre, ln_g, ln_b, w, bt, wout_g))


def _sgu_bwd(x, dy, pre, g, ln_g, ln_b, w, bt, win_g, wout_g, name, comms=()):
    t, d = x.shape
    tm = min(256, t)
    pc = pre.shape[1]
    c = win_g.shape[-1]

    def body(x_ref, dy_ref, pre_ref, g_ref, lg_ref, lb_ref, w_ref, bt_ref, win_hbm, wout_hbm,
             dx_ref, dpre_ref, dg_ref, dlg_ref, dlb_ref, dw_ref, dbt_ref, win_v, wout_v, sem):
        _load_at_first_step(_in_weight_copies(win_hbm, win_v, sem) + _out_weight_copies(wout_hbm, wout_v, sem, N_CHIPS))
        i = pl.program_id(0)

        @pl.when(i == 0)
        def _():
            for ref in (dg_ref, dlg_ref, dlb_ref, dw_ref, dbt_ref):
                ref[...] = jnp.zeros_like(ref)

        prev = pre_ref[...]
        lg = lg_ref[...]
        u, vhat, rs, vln, phi = _sgu_recompute(prev, lg, lb_ref[...])
        vo, wm = _sgu_spatial(vln, w_ref, bt_ref[...], tm)
        dp = _dot_nt(dy_ref[...].astype(BF16), wout_v[...])
        du = dp * vo
        dvo = dp * u
        dvob = dvo.astype(BF16)
        vb = vln.astype(BF16)
        head_lane = lax.broadcasted_iota(jnp.int32, (CHUNK, SGU_HEADS), 1)
        dbt = jnp.zeros((CHUNK, SGU_HEADS), F32)
        dw = [jnp.zeros((CHUNK, CHUNK), F32) for _ in range(SGU_HEADS)]
        rows = []
        for ch in range(tm // CHUNK):
            rs_ = slice(ch * CHUNK, (ch + 1) * CHUNK)
            blocks = []
            for hd in range(SGU_HEADS):
                cs = slice(hd * CHUNK, (hd + 1) * CHUNK)
                dbt = dbt + jnp.where(head_lane == hd, jnp.sum(dvo[rs_, cs], axis=1, keepdims=True), 0.0)
                dw[hd] = dw[hd] + _dot_nt(dvob[rs_, cs], vb[rs_, cs])
                blocks.append(_dot_tn(wm[hd], dvob[rs_, cs]))
            rows.append(jnp.concatenate(blocks, axis=1))
        dvln = jnp.concatenate(rows, axis=0)
        mask = _causal_mask()
        for hd in range(SGU_HEADS):
            dw_ref[hd] += dw[hd] * mask
        dbt_ref[...] += dbt
        dlg_ref[...] += jnp.sum(dvln * vhat, axis=0, keepdims=True)
        dlb_ref[...] += jnp.sum(dvln, axis=0, keepdims=True)
        dv = _ln_bwd(dvln * lg, vhat, rs)
        gelu_grad = phi + prev * jnp.exp(-0.5 * prev * prev) * INV_SQRT_2PI
        dpre = (jnp.concatenate([du, dv], axis=1) * gelu_grad).astype(BF16)
        dpre_ref[...] = dpre
        dxn = _proj_in_bwd_tail(dpre, win_v, c)
        xh, r = _rms_fwd(x_ref[...])
        dg_ref[...] += jnp.sum(dxn * xh, axis=0, keepdims=True)
        dx_ref[...] = dy_ref[...] + _rms_bwd(dxn, xh, r, g_ref[...])

    return _call(
        body, name=name, grid=(t // tm,),
        in_specs=[_row_spec(tm, d), _row_spec(tm, d), _row_spec(tm, pc), _const_spec((1, d)), _const_spec((1, d)), _const_spec((1, d)),
                  _const_spec(w.shape), _const_spec(bt.shape), HBM_SPEC, HBM_SPEC],
        out_specs=[_row_spec(tm, d), _row_spec(tm, pc), _const_spec((1, d)), _const_spec((1, d)), _const_spec((1, d)),
                   _const_spec(w.shape), _const_spec(bt.shape)],
        out_shape=[jax.ShapeDtypeStruct((t, d), F32), jax.ShapeDtypeStruct((t, pc), BF16), jax.ShapeDtypeStruct((1, d), F32),
                   jax.ShapeDtypeStruct((1, d), F32), jax.ShapeDtypeStruct((1, d), F32), jax.ShapeDtypeStruct(w.shape, F32),
                   jax.ShapeDtypeStruct(bt.shape, F32)],
        scratch_shapes=[pltpu.VMEM((N_CHIPS, d, c), BF16), pltpu.VMEM((d, d), BF16), pltpu.SemaphoreType.DMA((2 * N_CHIPS,))],
        args=(x, dy, pre, g, ln_g, ln_b, w, bt, win_g, wout_g), comms=comms)


def _final_loss(x, tgt, g, name):
    t, d = x.shape
    tm = min(512, t)

    def body(x_ref, t_ref, g_ref, dx_ref, loss_ref, dg_ref):
        @pl.when(pl.program_id(0) == 0)
        def _():
            loss_ref[...] = jnp.zeros_like(loss_ref)
            dg_ref[...] = jnp.zeros_like(dg_ref)

        gv = g_ref[...]
        xh, r = _rms_fwd(x_ref[...])
        diff = xh * gv - t_ref[...]
        loss_ref[...] += 0.5 * jnp.sum(jnp.sum(diff * diff, axis=1, keepdims=True), axis=0, keepdims=True) / d
        dout = diff / d
        dg_ref[...] += jnp.sum(dout * xh, axis=0, keepdims=True)
        dx_ref[...] = _rms_bwd(dout, xh, r, gv)

    return _call(
        body, name=name, grid=(t // tm,),
        in_specs=[_row_spec(tm, d), _row_spec(tm, d), _const_spec((1, d))],
        out_specs=[_row_spec(tm, d), _const_spec((1, 1)), _const_spec((1, d))],
        out_shape=[jax.ShapeDtypeStruct((t, d), F32), jax.ShapeDtypeStruct((1, 1), F32), jax.ShapeDtypeStruct((1, d), F32)],
        args=(x, tgt, g))


def _tn_matmul(a, b, scale, bm, bn, name, comms=()):
    t, m = a.shape
    n = b.shape[1]
    tk = min(512, t)
    bm, bn = min(bm, m), min(bn, n)
    nk = t // tk

    def body(a_ref, b_ref, o_ref, acc_ref):
        k = pl.program_id(2)

        @pl.when(k == 0)
        def _():
            acc_ref[...] = jnp.zeros_like(acc_ref)

        bv = b_ref[...]
        if bv.dtype != BF16:
            bv = (scale * bv).astype(BF16)
        acc_ref[...] += _dot_tn(a_ref[...], bv)

        @pl.when(k == nk - 1)
        def _():
            o_ref[...] = acc_ref[...].astype(BF16)

    return _call(
        body, name=name, grid=(m // bm, n // bn, nk),
        in_specs=[pl.BlockSpec((tk, bm), lambda i, j, k: (k, i)), pl.BlockSpec((tk, bn), lambda i, j, k: (k, j))],
        out_specs=[pl.BlockSpec((bm, bn), lambda i, j, k: (i, j))],
        out_shape=[jax.ShapeDtypeStruct((m, n), BF16)],
        scratch_shapes=[pltpu.VMEM((bm, bn), F32)],
        args=(a, b), comms=comms)[0]


def _row_tile(rows, cols, budget_bytes=2 * 1024 * 1024):
    best = None
    for cand in range(16, rows + 1, 16):
        if rows % cand == 0 and cand * cols * 4 <= budget_bytes:
            best = cand
    return best or rows


def _scalar_grid(grid, in_specs, out_specs):
    return pltpu.PrefetchScalarGridSpec(num_scalar_prefetch=1, grid=grid, in_specs=in_specs, out_specs=out_specs)


def _cast_into_slot(w, layer, me, name):
    _, rows, cols = w.shape
    tr = _row_tile(rows, cols)

    def body(me_ref, w_ref, o_ref):
        o_ref[...] = w_ref[...].astype(BF16)

    return pl.pallas_call(
        body, name=name,
        grid_spec=_scalar_grid((rows // tr,), [pl.BlockSpec((None, tr, cols), lambda i, me: (layer, i, 0))],
                               pl.BlockSpec((None, tr, cols), lambda i, me: (me[0], i, 0))),
        out_shape=jax.ShapeDtypeStruct((N_CHIPS, rows, cols), BF16), compiler_params=_cparams())(me, w)


def _add_half(view, other, core, name):
    q, _, r, c = view.shape
    tr = _row_tile(r, c)

    def body(core_ref, a_ref, b_ref, o_ref):
        o_ref[...] = (a_ref[...].astype(F32) + b_ref[...].astype(F32)).astype(BF16)

    return pl.pallas_call(
        body, name=name,
        grid_spec=_scalar_grid((q, r // tr), [pl.BlockSpec((None, None, tr, c), lambda k, i, core: (k, core[0], i, 0)),
                                             pl.BlockSpec((None, tr, c), lambda k, i, core: (k, i, 0))],
                               pl.BlockSpec((None, tr, c), lambda k, i, core: (k, i, 0))),
        out_shape=jax.ShapeDtypeStruct((q, r, c), BF16), compiler_params=_cparams_nd(2))(core, view, other)


def _reduce_piece(partial, staged, me, column_sharded, name):
    _, r, c = staged.shape
    tr = _row_tile(r, c, budget_bytes=1024 * 1024)
    nt = r // tr
    if column_sharded:
        own2d = partial.reshape(r, N_CHIPS * c)
        own_spec = pl.BlockSpec((tr, c), lambda i, me: (i, me[0]))
    else:
        own2d = partial.reshape(N_CHIPS * r, c)
        own_spec = pl.BlockSpec((tr, c), lambda i, me: (me[0] * nt + i, 0))
    ring = [pl.BlockSpec((None, tr, c), lambda i, me, k=k: ((me[0] + k) % N_CHIPS, i, 0)) for k in (1, 2, 3)]

    def body(me_ref, own_ref, s1_ref, s2_ref, s3_ref, o_ref):
        o_ref[...] = ((own_ref[...].astype(F32) + s1_ref[...].astype(F32)) + s2_ref[...].astype(F32)) + s3_ref[...].astype(F32)

    return pl.pallas_call(
        body, name=name, grid_spec=_scalar_grid((nt,), [own_spec] + ring, pl.BlockSpec((tr, c), lambda i, me: (i, 0))),
        out_shape=jax.ShapeDtypeStruct((r, c), F32), compiler_params=_cparams())(me, own2d, staged, staged, staged)


def _sum_leading(s, name):
    n, rows, cols = s.shape
    tr = _row_tile(rows, cols, budget_bytes=1024 * 1024)

    def body(s_ref, o_ref):
        acc = s_ref[0].astype(F32)
        for k in range(1, n):
            acc = acc + s_ref[k].astype(F32)
        o_ref[...] = acc

    return pl.pallas_call(
        body, name=name, grid=(rows // tr,), in_specs=[pl.BlockSpec((n, tr, cols), lambda i: (0, i, 0))], out_specs=_row_spec(tr, cols),
        out_shape=jax.ShapeDtypeStruct((rows, cols), F32), compiler_params=_cparams())(s)


ADAM_C1 = 1.0 / (1.0 - ADAM_B1 ** ADAM_STEP)
ADAM_C2 = 1.0 / (1.0 - ADAM_B2 ** ADAM_STEP)


def _adamw_math(w, g, m, v):
    mn = ADAM_B1 * m + (1.0 - ADAM_B1) * g
    vn = ADAM_B2 * v + (1.0 - ADAM_B2) * (g * g)
    return -ADAM_LR * ((mn * ADAM_C1) / (jnp.sqrt(vn * ADAM_C2) + ADAM_EPS) + ADAM_WD * w), mn, vn


def _adamw(w, g, m, v, name):
    shape = w.shape
    cols = shape[-1] if w.ndim > 1 else 128
    w2, g2, m2, v2 = (a.reshape(-1, cols) for a in (w, g, m, v))
    rows = w2.shape[0]
    tr = _row_tile(rows, cols, budget_bytes=1024 * 1024)

    def body(w_ref, g_ref, m_ref, v_ref, d_ref, mo_ref, vo_ref):
        d_ref[...], mo_ref[...], vo_ref[...] = _adamw_math(w_ref[...], g_ref[...], m_ref[...], v_ref[...])

    spec = _row_spec(tr, cols)
    outs = pl.pallas_call(
        body, name=name, grid=(rows // tr,), in_specs=[spec] * 4, out_specs=[spec] * 3,
        out_shape=[jax.ShapeDtypeStruct((rows, cols), F32)] * 3, compiler_params=_cparams())(w2, g2, m2, v2)
    return tuple(o.reshape(shape) for o in outs)


def _adamw_sharded(w, g_mine, g_sibling, m, v, core, layer, prev, name, comms=()):
    n_layers, r, c = w.shape
    half = r // 2
    tr = _row_tile(half, c)
    nt = half // tr

    def body(core_ref, w_ref, gm_ref, gs_ref, m_ref, v_ref, *rest):
        g_ref, d_ref, mo_ref, vo_ref = rest[-4:]
        gv = jnp.where(pl.program_id(0) == core_ref[0], gm_ref[...], gs_ref[...])
        g_ref[...] = gv
        d_ref[...], mo_ref[...], vo_ref[...] = _adamw_math(w_ref[...], gv, m_ref[...], v_ref[...])

    full = pl.BlockSpec((None, tr, c), lambda h, i, core: (layer, h * nt + i, 0))
    part = pl.BlockSpec((tr, c), lambda h, i, core: (i, 0))
    args = [w, g_mine, g_sibling, m, v]
    in_specs = [full, part, part, full, full]
    aliases = {}
    if prev is not None:
        aliases = {len(args) + k: k for k in range(4)}
        args += list(prev)
        in_specs += [pl.BlockSpec(memory_space=pl.ANY)] * 4
    return _call(body, name=name, grid=(2, nt), in_specs=in_specs, out_specs=[full] * 4, out_shape=[jax.ShapeDtypeStruct(w.shape, F32)] * 4,
                 args=args, comms=comms, scalar=core, aliases=aliases)


BIG_IN = ("ffn1_w_in", "ffn2_w_in", "ab_w_in", "sgu_w_in")
BIG_OUT = ("ffn1_w_out", "ffn2_w_out", "ab_w_out", "sgu_w_out")
BIG = BIG_IN + BIG_OUT


class _Gatherer:
    def __init__(self, slots):
        self.slots = dict(slots)

    def _stage(self, keys, d2d):
        n = len(keys)

        def plan(ins, outs, place):
            x, y, c = place
            me = 2 * x + y
            remote = []
            for a in range(n):
                rows = ins[a].shape[1] // 2

                def half(ref, q, core, rows=rows):
                    return ref.at[q, pl.ds(core * rows, rows), :]

                for (px, py) in _other_chips(x, y):
                    q = 2 * px + py
                    if d2d:
                        remote.append((half(ins[a], q, c), half(outs[a], q, c), (x, y, 1 - c), half(outs[a], q, 1 - c)))
                    else:
                        remote.append((half(ins[a], me, c), half(outs[a], me, c), (px, py, c), half(outs[a], q, c)))
            return remote

        def finish(outs):
            for k, o in zip(keys, outs):
                self.slots[k] = o

        arrays = [self.slots[k] for k in keys]
        return _Exchange(arrays, [_sds(a) for a in arrays], plan, 3 * n, {a: a for a in range(n)}, finish)

    def direct(self, keys):
        n = len(keys)

        def plan(ins, outs, place):
            x, y, c = place
            me = 2 * x + y
            return [(ins[a].at[me], outs[a].at[me], (px, py, c), outs[a].at[2 * px + py]) for a in range(n) for (px, py) in _other_chips(x, y)]

        def finish(outs):
            for k, o in zip(keys, outs):
                self.slots[k] = o

        arrays = [self.slots[k] for k in keys]
        return _Exchange(arrays, [_sds(a) for a in arrays], plan, 3 * n, {a: a for a in range(n)}, finish)

    def ici(self, keys):
        return self._stage(keys, False)

    def d2d(self, keys):
        return self._stage(keys, True)


class _Reducer:
    def __init__(self, me, core):
        self.me, self.core = me, core
        self.views, self.partial, self.staged, self.mine, self.theirs = {}, {}, {}, {}, {}

    def add(self, key, g):
        m, n = g.shape
        if key[0] in BIG_IN:
            self.views[key] = g.reshape(1, 2, m // 2, n)
        else:
            self.views[key] = g.reshape(N_CHIPS, 2, m // (2 * N_CHIPS), n)

    def swap(self, keys):
        views = [self.views[k] for k in keys]

        def plan(ins, outs, place):
            x, y, c = place
            return [(ins[a].at[:, 1 - c], outs[a], (x, y, 1 - c), outs[a]) for a in range(len(keys))]

        def finish(outs):
            for k, v, o in zip(keys, views, outs):
                self.partial[k] = _add_half(v, o, self.core, f"chip_partial_{k[0]}_{k[1]}")

        shapes = [jax.ShapeDtypeStruct((v.shape[0],) + v.shape[2:], v.dtype) for v in views]
        return _Exchange(views, shapes, plan, len(keys), None, finish)

    def scatter(self, keys, part=(0, 1)):
        i, n = part
        n_keys = len(keys)
        parts = [self.partial[k] for k in keys]
        shapes = []
        for k, p in zip(keys, parts):
            q, r, c = p.shape
            shapes.append(jax.ShapeDtypeStruct((N_CHIPS, r, c // N_CHIPS if k[0] in BIG_IN else c), p.dtype))

        def piece(ref, key, q, rows, cols):
            return ref.at[0, rows, pl.ds(q * cols, cols)] if key[0] in BIG_IN else ref.at[q, rows, :]

        def plan(ins, outs, place):
            x, y, c = place
            me = 2 * x + y
            remote = []
            for a, k in enumerate(keys):
                _, r, cols = shapes[a].shape
                rows = pl.ds(i * (r // n), r // n)
                for (px, py) in _other_chips(x, y):
                    q = 2 * px + py
                    remote.append((piece(ins[a], k, q, rows, cols), outs[a].at[me, rows, :], (px, py, c), outs[a].at[q, rows, :]))
            return remote

        def finish(outs):
            for k, p, o in zip(keys, parts, outs):
                self.staged[k] = o
                if i == n - 1:
                    self.mine[k] = _reduce_piece(p, o, self.me, k[0] in BIG_IN, f"reduce_{k[0]}_{k[1]}")

        inputs, aliases = parts, None
        if i > 0:
            inputs = parts + [self.staged[k] for k in keys]
            aliases = {n_keys + a: a for a in range(n_keys)}
        return _Exchange(inputs, shapes, plan, 3 * n_keys, aliases, finish)

    def exchange(self, keys):
        mine = [self.mine[k] for k in keys]

        def plan(ins, outs, place):
            x, y, c = place
            return [(ins[a], outs[a], (x, y, 1 - c), outs[a]) for a in range(len(keys))]

        def finish(outs):
            for k, o in zip(keys, outs):
                self.theirs[k] = o

        return _Exchange(mine, [_sds(a) for a in mine], plan, len(keys), None, finish)


def _all_gather_full(gat, keys):
    n = len(keys)
    arrays = [gat.slots[k] for k in keys]

    def body(*refs):
        ins, outs = refs[:n], refs[n:2 * n]
        send_sems, recv_sems = refs[2 * n:]
        x, y, c = _my_place()
        sibling = (x, y, 1 - c)
        chips = _other_chips(x, y)
        me = 2 * x + y

        def half(ref, q, core):
            rows = ref.shape[1] // 2
            return ref.at[q, pl.ds(core * rows, rows), :]

        def copy(a, k, src, dst, to):
            return pltpu.make_async_remote_copy(src_ref=src, dst_ref=dst, send_sem=send_sems.at[6 * a + k], recv_sem=recv_sems.at[6 * a + k],
                                                device_id=to, device_id_type=MESH)

        first = [copy(a, j, half(ins[a], me, c), half(outs[a], me, c), (*chip, c)) for a in range(n) for j, chip in enumerate(chips)]
        for cp in first:
            cp.start()
        passed = []
        for a in range(n):
            for j, (px, py) in enumerate(chips):
                landed = half(outs[a], 2 * px + py, c)
                copy(a, j, landed, landed, (px, py, c)).wait_recv()
                fwd = copy(a, 3 + j, landed, landed, sibling)
                fwd.start()
                passed.append(fwd)
        for a in range(n):
            for j, (px, py) in enumerate(chips):
                other = half(outs[a], 2 * px + py, 1 - c)
                copy(a, 3 + j, other, other, sibling).wait_recv()
        for cp in first + passed:
            cp.wait_send()

    outs = pl.pallas_call(
        body, name="all_gather_first_weights", in_specs=[HBM_SPEC] * n, out_specs=[HBM_SPEC] * n,
        out_shape=[_sds(a) for a in arrays], input_output_aliases={a: a for a in range(n)},
        scratch_shapes=[pltpu.SemaphoreType.DMA((6 * n,)), pltpu.SemaphoreType.DMA((6 * n,))])(*arrays)
    for k, o in zip(keys, outs):
        gat.slots[k] = o


def _small_all_gather(buf, done):
    peers = [(0, 0, 1), (1, 0, 0), (0, 1, 0), (1, 1, 0), (1, 0, 1), (0, 1, 1), (1, 1, 1)]

    def index(x, y, c):
        return 4 * x + 2 * y + c

    def plan(ins, outs, place):
        x, y, c = place
        remote = []
        for fx, fy, fc in peers:
            px, py, pc = (1 - x if fx else x), (1 - y if fy else y), (1 - c if fc else c)
            remote.append((ins[0], outs[0].at[index(x, y, c)], (px, py, pc), outs[0].at[index(px, py, pc)]))
        return remote

    def local(ins, outs, place):
        return [(ins[0], outs[0].at[index(*place)])]

    return _Exchange([buf], [jax.ShapeDtypeStruct((2 * N_CHIPS,) + buf.shape, buf.dtype)], plan, len(peers), None,
                     lambda outs: done(outs[0]), local, 1)


WEIGHT_NAMES = ("ffn1_norm", "ffn1_w_in", "ffn1_w_out", "mix_norm", "ffn2_norm", "ffn2_w_in", "ffn2_w_out", "ab_w_in", "pool_w", "pool_b",
                "pool_scale", "conv_w", "conv_b", "conv_ln_g", "conv_ln_b", "ab_w_out", "sgu_w_in", "sgu_ln_g", "sgu_ln_b", "sgu_w", "sgu_b",
                "sgu_w_out", "final_norm")
SMALL = tuple(n for n in WEIGHT_NAMES if n not in BIG)
SHARDED_SMALL = ("conv_w", "sgu_ln_g", "sgu_ln_b")
PACK_ROWS = 48
PACK = ("pack", 0)


def _pair(prefix, layer):
    return [(prefix + "_w_in", layer), (prefix + "_w_out", layer)]


def kernel(x, ffn1_norm, ffn1_w_in, ffn1_w_out, mix_norm, ffn2_norm, ffn2_w_in, ffn2_w_out, ab_w_in, pool_w, pool_b, pool_scale, conv_w, conv_b, conv_ln_g, conv_ln_b, ab_w_out, sgu_w_in, sgu_ln_g, sgu_ln_b, sgu_w, sgu_b, sgu_w_out, final_norm, loss_target, m_ffn1_norm, m_ffn1_w_in, m_ffn1_w_out, m_mix_norm, m_ffn2_norm, m_ffn2_w_in, m_ffn2_w_out, m_ab_w_in, m_pool_w, m_pool_b, m_pool_scale, m_conv_w, m_conv_b, m_conv_ln_g, m_conv_ln_b, m_ab_w_out, m_sgu_w_in, m_sgu_ln_g, m_sgu_ln_b, m_sgu_w, m_sgu_b, m_sgu_w_out, m_final_norm, v_ffn1_norm, v_ffn1_w_in, v_ffn1_w_out, v_mix_norm, v_ffn2_norm, v_ffn2_w_in, v_ffn2_w_out, v_ab_w_in, v_pool_w, v_pool_b, v_pool_scale, v_conv_w, v_conv_b, v_conv_ln_g, v_conv_ln_b, v_ab_w_out, v_sgu_w_in, v_sgu_ln_g, v_sgu_ln_b, v_sgu_w, v_sgu_b, v_sgu_w_out, v_final_norm):
    given = dict(locals())
    w = {n: given[n] for n in WEIGHT_NAMES}
    chip = 2 * lax.axis_index("x") + lax.axis_index("y")
    me = chip.astype(jnp.int32).reshape(1)
    core = lax.axis_index("c").astype(jnp.int32).reshape(1)
    row = lambda v: v.reshape(1, -1)
    xin, tgt = x[0], loss_target[0]

    pack = jnp.concatenate([
        w["conv_w"][0], jnp.zeros((1, 128), F32), w["sgu_ln_g"].reshape(2, 128), w["sgu_ln_b"].reshape(2, 128),
        jnp.zeros((PACK_ROWS - 36, 128), F32)], axis=0)
    slots = {PACK: lax.dynamic_update_slice(jnp.zeros((N_CHIPS, PACK_ROWS, 128), F32), pack[None], (me[0], 0, 0))}
    for n in BIG:
        for layer in range(w[n].shape[0]):
            slots[(n, layer)] = _cast_into_slot(w[n], layer, me, f"cast_{n}_{layer}")
    gat = _Gatherer(slots)
    _all_gather_full(gat, _pair("ffn1", 0) + [PACK])
    gp = gat.slots[PACK]
    conv_w_full = jnp.transpose(gp[:, 0:CONV_WIDTH], (1, 0, 2)).reshape(CONV_WIDTH, N_CHIPS * 128)
    sgu_ln_g_full = gp[:, 32:34].reshape(1, -1)
    sgu_ln_b_full = gp[:, 34:36].reshape(1, -1)
    gw = lambda n, layer: gat.slots[(n, layer)]

    st = [dict(), dict()]
    st[0]["xa"] = xin
    later = _pair("sgu", 0) + _pair("ffn2", 1)
    cur, st[0]["h1"], st[0]["xn1"] = _ffn_fwd(xin, row(w["ffn1_norm"][0]), gw("ffn1_w_in", 0), gw("ffn1_w_out", 0), "ffn1_fwd_0",
                                              comms=[gat.direct(_pair("ab", 0)), gat.ici(_pair("ffn2", 0))])
    st[0]["xb"] = cur
    st[0]["h0"], st[0]["xnm"] = _norm_matmul(cur, row(w["mix_norm"][0]), gw("ab_w_in", 0), "mix0_proj_in",
                                             comms=[gat.d2d(_pair("ffn2", 0))])
    pool_args = (w["pool_w"][0], row(w["pool_b"][0]), row(w["pool_scale"][0]), conv_w_full, row(w["conv_b"][0]), row(w["conv_ln_g"][0]),
                 row(w["conv_ln_b"][0]), gw("ab_w_out", 0))
    cur, st[0]["ycat"], st[0]["yconv"] = _mix0_fwd(cur, st[0]["h0"], *pool_args, "mix0_fwd", comms=[gat.ici(_pair("ffn1", 1))])
    st[0]["xc"] = cur
    cur, st[0]["h2"], st[0]["xn2"] = _ffn_fwd(cur, row(w["ffn2_norm"][0]), gw("ffn2_w_in", 0), gw("ffn2_w_out", 0), "ffn2_fwd_0",
                                              comms=[gat.d2d(_pair("ffn1", 1)), gat.ici(later)])
    st[1]["xa"] = cur
    cur, st[1]["h1"], st[1]["xn1"] = _ffn_fwd(cur, row(w["ffn1_norm"][1]), gw("ffn1_w_in", 1), gw("ffn1_w_out", 1), "ffn1_fwd_1",
                                              comms=[gat.d2d(later)])
    st[1]["xb"] = cur
    st[1]["pre"], st[1]["xnm"] = _norm_matmul(cur, row(w["mix_norm"][1]), gw("sgu_w_in", 0), "sgu_proj_in")
    sgu_args = (sgu_ln_g_full, sgu_ln_b_full, w["sgu_w"][0], w["sgu_b"][0].T)
    cur, st[1]["p"] = _sgu_fwd(cur, st[1]["pre"], *sgu_args, gw("sgu_w_out", 0), "sgu_fwd")
    st[1]["xc"] = cur
    cur, st[1]["h2"], st[1]["xn2"] = _ffn_fwd(cur, row(w["ffn2_norm"][1]), gw("ffn2_w_in", 1), gw("ffn2_w_out", 1), "ffn2_fwd_1")
    dy, loss, d_final = _final_loss(cur, tgt, row(w["final_norm"]), "final_loss")

    red = _Reducer(me, core)
    small = {"final_norm": d_final.reshape(-1)}
    norm_grads = {"ffn1_norm": [None] * DEPTH, "mix_norm": [None] * DEPTH, "ffn2_norm": [None] * DEPTH}
    ga, gb, gc, gd, ge, gf = _pair("ffn2", 1), _pair("sgu", 0), _pair("ffn1", 1), _pair("ffn2", 0), _pair("ab", 0), _pair("ffn1", 0)

    def ffn_backward(prefix, layer, xs, hs, xns, dy_in, bwd_comms=(), dwin_comms=(), dwout_comms=()):
        dx, dh, act, norm_grads[prefix + "_norm"][layer] = _ffn_bwd(
            xs, dy_in, hs, row(w[prefix + "_norm"][layer]), gw(prefix + "_w_in", layer), gw(prefix + "_w_out", layer),
            f"{prefix}_bwd_{layer}", comms=bwd_comms)
        red.add((prefix + "_w_in", layer), _tn_matmul(xns, dh, 1.0, 1024, 1408, f"{prefix}_dwin_{layer}", comms=dwin_comms))
        red.add((prefix + "_w_out", layer), _tn_matmul(act, dy_in, 0.5, 1408, 1024, f"{prefix}_dwout_{layer}", comms=dwout_comms))
        return dx

    s1, s0 = st[1], st[0]
    dy = ffn_backward("ffn2", 1, s1["xc"], s1["h2"], s1["xn2"], dy)
    dy_in = dy
    dy, dpre, norm_grads["mix_norm"][1], dlg, dlb, dw, dbt = _sgu_bwd(
        s1["xb"], dy_in, s1["pre"], row(w["mix_norm"][1]), *sgu_args, gw("sgu_w_in", 0), gw("sgu_w_out", 0), "sgu_bwd", comms=[red.swap(ga)])
    red.add(("sgu_w_in", 0), _tn_matmul(s1["xnm"], dpre, 1.0, 1024, 2048, "sgu_dwin"))
    red.add(("sgu_w_out", 0), _tn_matmul(s1["p"], dy_in, 1.0, 1024, 1024, "sgu_dwout"))
    small.update(sgu_ln_g=dlg, sgu_ln_b=dlb, sgu_w=dw[None], sgu_b=dbt.T[None])
    dy = ffn_backward("ffn1", 1, s1["xa"], s1["h1"], s1["xn1"], dy, bwd_comms=[lambda: red.scatter(ga), lambda: red.swap(gb)],
                      dwin_comms=[lambda: red.scatter(gb), lambda: red.exchange(ga)])
    dy = ffn_backward("ffn2", 0, s0["xc"], s0["h2"], s0["xn2"], dy, bwd_comms=[lambda: red.swap(gc), lambda: red.exchange(gb)],
                      dwin_comms=[lambda: red.scatter(gc)])
    dy_in = dy
    dconv, dpc, dpw, rows = _mix0_bwd_a(dy_in, s0["h0"], s0["yconv"], *pool_args, "mix0_bwd_a")
    dy, dh0, norm_grads["mix_norm"][0] = _mix0_bwd_b(s0["xb"], dy_in, s0["h0"], dconv, dpc, row(w["mix_norm"][0]), conv_w_full,
                                                      gw("ab_w_in", 0), "mix0_bwd_b")
    red.add(("ab_w_in", 0), _tn_matmul(s0["xnm"], dh0, 1.0, 1024, 1536, "ab_dwin", comms=[red.swap(gd), red.exchange(gc)]))
    red.add(("ab_w_out", 0), _tn_matmul(s0["ycat"], dy_in, 1.0, 1024, 1024, "ab_dwout"))
    small.update(pool_w=dpw[None], conv_w=rows[None, 0:CONV_WIDTH], conv_b=rows[32:33], conv_ln_g=rows[33:34], conv_ln_b=rows[34:35],
                 pool_scale=rows[35:36], pool_b=rows[36:37].reshape(1, len(POOL_WINDOWS), POOL_GC))

    small_sum = {}

    def small_ready():
        for k, v in norm_grads.items():
            small[k] = jnp.concatenate(v, axis=0)
        flat = [small[n].reshape(-1, 128) for n in SMALL]
        rows = sum(f.shape[0] for f in flat)
        loss_block = jnp.pad(loss, ((0, 8 + (-rows) % 8 - 1), (0, 127)))
        buf = jnp.concatenate(flat + [loss_block], axis=0)

        def done(gathered):
            total, at = _sum_leading(gathered, "reduce_small"), 0
            for n, f in zip(SMALL, flat):
                small_sum[n] = total[at:at + f.shape[0]].reshape(small[n].shape)
                at += f.shape[0]
            small_sum["loss"] = total[at:at + 1, 0:1]

        return _small_all_gather(buf, done)

    dx, dh, act, norm_grads["ffn1_norm"][0] = _ffn_bwd(
        s0["xa"], dy, s0["h1"], row(w["ffn1_norm"][0]), gw("ffn1_w_in", 0), gw("ffn1_w_out", 0), "ffn1_bwd_0")
    red.add(("ffn1_w_in", 0), _tn_matmul(s0["xn1"], dh, 1.0, 1024, 1408, "ffn1_dwin_0",
                                         comms=[red.scatter(gd), red.swap(ge), small_ready()]))
    red.add(("ffn1_w_out", 0), _tn_matmul(act, dy, 0.5, 1408, 1024, "ffn1_dwout_0", comms=[red.scatter(ge), red.exchange(gd)]))
    grad_x = dx

    big_out = {}

    def adamw_big(n, layer, comms=()):
        big_out[n] = _adamw_sharded(w[n], red.mine[(n, layer)], red.theirs[(n, layer)], given["m_" + n], given["v_" + n], core, layer,
                                    big_out.get(n), f"adamw_{n}_{layer}", comms=comms)

    _exchange_alone("swap_last_grads", [red.swap(gf), red.exchange(ge)])
    _exchange_alone("scatter_last_grads", [red.scatter(gf)])
    _exchange_alone("exchange_last_grads", [red.exchange(gf)])
    for n in BIG:
        for layer in reversed(range(w[n].shape[0])):
            adamw_big(n, layer)

    loss = small_sum["loss"][0, 0]
    grads, delta, new_m, new_v = {}, {}, {}, {}
    for n in WEIGHT_NAMES:
        mom, var = given["m_" + n], given["v_" + n]
        if n in BIG:
            grads[n], delta[n], new_m[n], new_v[n] = big_out[n]
            continue
        g = small_sum[n]
        if n in SHARDED_SMALL:
            width = w[n].shape[-1]
            g = lax.dynamic_slice_in_dim(g, chip * width, width, axis=g.ndim - 1)
        grads[n] = g
        delta[n], new_m[n], new_v[n] = _adamw(w[n], g, mom, var, f"adamw_{n}")
    return (loss, grad_x[None], *[grads[n] for n in WEIGHT_NAMES], *[delta[n] for n in WEIGHT_NAMES],
            *[new_m[n] for n in WEIGHT_NAMES], *[new_v[n] for n in WEIGHT_NAMES])
```

```python
import jax
import jax.numpy as jnp
from jax import lax
from jax.experimental import pallas as pl
from jax.experimental.pallas import tpu as pltpu

F32, BF16 = jnp.float32, jnp.bfloat16
EPS = 1e-6
N_CHIPS = 4
POOL_WINDOWS = (2, 4, 8, 16)
POOL_GC = 128
POOL_CH = 512
CONV_CH = 512
CONV_WIDTH = 31
HALO = 32
SGU_HEADS = 8
CHUNK = 128
DEPTH = 2
ADAM_LR, ADAM_B1, ADAM_B2, ADAM_EPS, ADAM_WD, ADAM_STEP = 0.001, 0.9, 0.999, 1e-08, 0.01, 10
VMEM_LIMIT_BYTES = 60 * 1024 * 1024
MESH_AXES = ("x", "y", "c")
MESH = pl.DeviceIdType.MESH
HBM_SPEC = pl.BlockSpec(memory_space=pltpu.HBM)


def _sds(a):
    return jax.ShapeDtypeStruct(a.shape, a.dtype)


def _cparams_nd(n):
    return pltpu.CompilerParams(dimension_semantics=("arbitrary",) * n, vmem_limit_bytes=VMEM_LIMIT_BYTES)


def _cparams():
    return _cparams_nd(1)


def _dot(a, b):
    return jnp.dot(a, b, preferred_element_type=F32)


def _dot_nt(a, b):
    return lax.dot_general(a, b, (((1,), (1,)), ((), ())), preferred_element_type=F32)


def _dot_tn(a, b):
    return lax.dot_general(a, b, (((0,), (0,)), ((), ())), preferred_element_type=F32)


def _rms_fwd(x):
    r = lax.rsqrt(jnp.mean(x * x, axis=-1, keepdims=True) + EPS)
    return x * r, r


def _rms_bwd(dxn, xh, r, g):
    dxh = dxn * g
    return r * (dxh - xh * jnp.mean(dxh * xh, axis=-1, keepdims=True))


def _ln_fwd(y):
    mu = jnp.mean(y, axis=-1, keepdims=True)
    yc = y - mu
    rs = lax.rsqrt(jnp.mean(yc * yc, axis=-1, keepdims=True) + EPS)
    return yc * rs, rs


def _ln_bwd(dyhat, yhat, rs):
    return rs * (dyhat - jnp.mean(dyhat, axis=-1, keepdims=True) - yhat * jnp.mean(dyhat * yhat, axis=-1, keepdims=True))


def _sigmoid(x):
    return 0.5 * jnp.tanh(0.5 * x) + 0.5


def _const_spec(shape):
    n = len(shape)
    return pl.BlockSpec(shape, lambda i: (0,) * n)


def _row_spec(tm, cols):
    return pl.BlockSpec((tm, cols), lambda i: (i, 0))


def _my_place():
    return lax.axis_index("x"), lax.axis_index("y"), lax.axis_index("c")


def _other_chips(x, y):
    return [(1 - x, y), (x, 1 - y), (1 - x, 1 - y)]


class _Exchange:
    def __init__(self, inputs, out_shapes, plan, count, aliases=None, finish=None, local=None, n_local=0):
        self.inputs, self.out_shapes, self.plan, self.count = list(inputs), list(out_shapes), plan, count
        self.aliases, self.finish, self.local, self.n_local = dict(aliases or {}), finish, local, n_local


def _call(body, *, name, grid, in_specs, out_specs, out_shape, args, scratch_shapes=(), comms=(), scalar=None, aliases=None):
    comms = [cm if isinstance(cm, _Exchange) else cm() for cm in comms]
    in_specs, out_specs, out_shape, scratch_shapes = list(in_specs), list(out_specs), list(out_shape), list(scratch_shapes)
    n_in, n_out, n_scr = len(in_specs), len(out_specs), len(scratch_shapes)
    c_in = [a for cm in comms for a in cm.inputs]
    c_out = [s for cm in comms for s in cm.out_shapes]
    n_remote = sum(cm.count for cm in comms)
    n_local = sum(cm.n_local for cm in comms)
    n_scalar = 0 if scalar is None else 1
    all_aliases = {n_scalar + i: o for i, o in (aliases or {}).items()}
    at_in, at_out = n_scalar + n_in, n_out
    for cm in comms:
        for i, o in cm.aliases.items():
            all_aliases[at_in + i] = at_out + o
        at_in += len(cm.inputs)
        at_out += len(cm.out_shapes)

    def wrapped(*all_refs):
        scalar_ref, refs = all_refs[:n_scalar], all_refs[n_scalar:]
        ins, ci = refs[:n_in], refs[n_in:n_in + len(c_in)]
        at = n_in + len(c_in)
        outs, co = refs[at:at + n_out], refs[at + n_out:at + n_out + len(c_out)]
        at += n_out + len(c_out)
        scr = refs[at:at + n_scr]

        def run_body():
            body(*scalar_ref, *ins, *outs, *scr)

        if not comms:
            run_body()
            return
        send_sems, recv_sems, local_sems = refs[at + n_scr:]
        place = _my_place()
        sends, arrivals, locals_ = [], [], []
        i0 = o0 = 0
        for cm in comms:
            cm_in, cm_out = ci[i0:i0 + len(cm.inputs)], co[o0:o0 + len(cm.out_shapes)]
            i0 += len(cm.inputs)
            o0 += len(cm.out_shapes)
            for src, dst, dev, incoming in cm.plan(cm_in, cm_out, place):
                k = len(sends)
                sends.append(pltpu.make_async_remote_copy(src_ref=src, dst_ref=dst, send_sem=send_sems.at[k], recv_sem=recv_sems.at[k],
                                                          device_id=dev, device_id_type=MESH))
                arrivals.append(pltpu.make_async_remote_copy(src_ref=src, dst_ref=incoming, send_sem=send_sems.at[k],
                                                             recv_sem=recv_sems.at[k], device_id=dev, device_id_type=MESH))
            if cm.local is not None:
                for src, dst in cm.local(cm_in, cm_out, place):
                    locals_.append(pltpu.make_async_copy(src, dst, local_sems.at[len(locals_)]))

        def start():
            for cp in locals_ + sends:
                cp.start()

        def finish():
            for cp in arrivals:
                cp.wait_recv()
            for cp in sends:
                cp.wait_send()
            for cp in locals_:
                cp.wait()

        if not grid:
            start()
            run_body()
            finish()
            return
        ids = [pl.program_id(a) for a in range(len(grid))]
        first, last = ids[0] == 0, ids[0] == grid[0] - 1
        for a in range(1, len(grid)):
            first = jnp.logical_and(first, ids[a] == 0)
            last = jnp.logical_and(last, ids[a] == grid[a] - 1)
        pl.when(first)(start)
        run_body()
        pl.when(last)(finish)

    sems = []
    if comms:
        sems = [pltpu.SemaphoreType.DMA((max(n_remote, 1),)), pltpu.SemaphoreType.DMA((max(n_remote, 1),)),
                pltpu.SemaphoreType.DMA((max(n_local, 1),))]
    all_in, all_out = in_specs + [HBM_SPEC] * len(c_in), out_specs + [HBM_SPEC] * len(c_out)
    if scalar is None:
        kwargs = dict(grid=grid, compiler_params=_cparams_nd(len(grid))) if grid else {}
        res = pl.pallas_call(
            wrapped, name=name, in_specs=all_in, out_specs=all_out, out_shape=out_shape + c_out, scratch_shapes=scratch_shapes + sems,
            input_output_aliases=all_aliases, **kwargs)(*args, *c_in)
    else:
        spec = pltpu.PrefetchScalarGridSpec(num_scalar_prefetch=1, grid=grid, in_specs=all_in, out_specs=all_out,
                                            scratch_shapes=scratch_shapes + sems)
        res = pl.pallas_call(
            wrapped, name=name, grid_spec=spec, out_shape=out_shape + c_out, input_output_aliases=all_aliases,
            compiler_params=_cparams_nd(len(grid)))(scalar, *args, *c_in)
    at = n_out
    for cm in comms:
        got = res[at:at + len(cm.out_shapes)]
        at += len(cm.out_shapes)
        if cm.finish is not None:
            cm.finish(got)
    return list(res[:n_out])


def _exchange_alone(name, comms):
    _call(lambda: None, name=name, grid=(), in_specs=[], out_specs=[], out_shape=[], args=[], comms=comms)


def _in_weight_copies(w_hbm, w_v, sem, base=0):
    return [pltpu.make_async_copy(w_hbm.at[q], w_v.at[q], sem.at[base + q]) for q in range(N_CHIPS)]


def _out_weight_copies(w_hbm, w_v, sem, base=0):
    rows = w_hbm.shape[1]
    return [pltpu.make_async_copy(w_hbm.at[q], w_v.at[pl.ds(q * rows, rows)], sem.at[base + q]) for q in range(N_CHIPS)]


def _load_at_first_step(copies):
    @pl.when(pl.program_id(0) == 0)
    def _():
        for cp in copies:
            cp.start()
        for cp in copies:
            cp.wait()


def _ffn_fwd(x, g, win_g, wout_g, name, comms=()):
    t, d = x.shape
    c = win_g.shape[-1]
    ff = 2 * c
    tm = min(512, t)

    def body(x_ref, g_ref, win_hbm, wout_hbm, xo_ref, h_ref, xn_ref, win_v, wout_v, sem):
        _load_at_first_step(_in_weight_copies(win_hbm, win_v, sem) + _out_weight_copies(wout_hbm, wout_v, sem, N_CHIPS))
        xv = x_ref[...]
        xh, _ = _rms_fwd(xv)
        xn = (xh * g_ref[...]).astype(BF16)
        xn_ref[...] = xn
        acc = jnp.zeros((tm, d), F32)
        for j in range(2):
            gate = _dot(xn, win_v[j])
            up = _dot(xn, win_v[j + 2])
            h_ref[:, j * c:(j + 1) * c] = gate.astype(BF16)
            h_ref[:, ff + j * c:ff + (j + 1) * c] = up.astype(BF16)
            act = (gate * _sigmoid(gate) * up).astype(BF16)
            acc = acc + _dot(act, wout_v[j * c:(j + 1) * c, :])
        xo_ref[...] = xv + 0.5 * acc

    return _call(
        body, name=name, grid=(t // tm,),
        in_specs=[_row_spec(tm, d), _const_spec((1, d)), HBM_SPEC, HBM_SPEC],
        out_specs=[_row_spec(tm, d), _row_spec(tm, 2 * ff), _row_spec(tm, d)],
        out_shape=[jax.ShapeDtypeStruct((t, d), F32), jax.ShapeDtypeStruct((t, 2 * ff), BF16), jax.ShapeDtypeStruct((t, d), BF16)],
        scratch_shapes=[pltpu.VMEM((N_CHIPS, d, c), BF16), pltpu.VMEM((ff, d), BF16), pltpu.SemaphoreType.DMA((2 * N_CHIPS,))],
        args=(x, g, win_g, wout_g), comms=comms)


def _ffn_bwd(x, dy, h, g, win_g, wout_g, name, comms=()):
    t, d = x.shape
    c = win_g.shape[-1]
    ff = 2 * c
    tm = min(256, t)

    def body(x_ref, dy_ref, h_ref, g_ref, win_hbm, wout_hbm, dx_ref, dh_ref, act_ref, dg_ref, win_v, wout_v, sem):
        _load_at_first_step(_in_weight_copies(win_hbm, win_v, sem) + _out_weight_copies(wout_hbm, wout_v, sem, N_CHIPS))

        @pl.when(pl.program_id(0) == 0)
        def _():
            dg_ref[...] = jnp.zeros_like(dg_ref)

        xv, dyv, gv = x_ref[...], dy_ref[...], g_ref[...]
        xh, r = _rms_fwd(xv)
        dyh = (0.5 * dyv).astype(BF16)
        dxn = jnp.zeros((tm, d), F32)
        for j in range(2):
            gate = h_ref[:, j * c:(j + 1) * c].astype(F32)
            up = h_ref[:, ff + j * c:ff + (j + 1) * c].astype(F32)
            dact = _dot_nt(dyh, wout_v[j * c:(j + 1) * c, :])
            s = _sigmoid(gate)
            sl = gate * s
            act_ref[:, j * c:(j + 1) * c] = (sl * up).astype(BF16)
            dgate = (dact * up * (s * (1.0 + gate * (1.0 - s)))).astype(BF16)
            dup = (dact * sl).astype(BF16)
            dh_ref[:, j * c:(j + 1) * c] = dgate
            dh_ref[:, ff + j * c:ff + (j + 1) * c] = dup
            dxn = dxn + _dot_nt(dgate, win_v[j]) + _dot_nt(dup, win_v[j + 2])
        dg_ref[...] += jnp.sum(dxn * xh, axis=0, keepdims=True)
        dx_ref[...] = dyv + _rms_bwd(dxn, xh, r, gv)

    return _call(
        body, name=name, grid=(t // tm,),
        in_specs=[_row_spec(tm, d), _row_spec(tm, d), _row_spec(tm, 2 * ff), _const_spec((1, d)), HBM_SPEC, HBM_SPEC],
        out_specs=[_row_spec(tm, d), _row_spec(tm, 2 * ff), _row_spec(tm, ff), _const_spec((1, d))],
        out_shape=[jax.ShapeDtypeStruct((t, d), F32), jax.ShapeDtypeStruct((t, 2 * ff), BF16), jax.ShapeDtypeStruct((t, ff), BF16),
                   jax.ShapeDtypeStruct((1, d), F32)],
        scratch_shapes=[pltpu.VMEM((N_CHIPS, d, c), BF16), pltpu.VMEM((ff, d), BF16), pltpu.SemaphoreType.DMA((2 * N_CHIPS,))],
        args=(x, dy, h, g, win_g, wout_g), comms=comms)


def _norm_matmul(x, g, win_g, name, comms=()):
    t, d = x.shape
    c = win_g.shape[-1]
    tm = min(512, t)

    def body(x_ref, g_ref, win_hbm, o_ref, xn_ref, win_v, sem):
        _load_at_first_step(_in_weight_copies(win_hbm, win_v, sem))
        xh, _ = _rms_fwd(x_ref[...])
        xn = (xh * g_ref[...]).astype(BF16)
        xn_ref[...] = xn
        for q in range(N_CHIPS):
            o_ref[:, q * c:(q + 1) * c] = _dot(xn, win_v[q])

    return _call(
        body, name=name, grid=(t // tm,),
        in_specs=[_row_spec(tm, d), _const_spec((1, d)), HBM_SPEC],
        out_specs=[_row_spec(tm, N_CHIPS * c), _row_spec(tm, d)],
        out_shape=[jax.ShapeDtypeStruct((t, N_CHIPS * c), F32), jax.ShapeDtypeStruct((t, d), BF16)],
        scratch_shapes=[pltpu.VMEM((N_CHIPS, d, c), BF16), pltpu.SemaphoreType.DMA((N_CHIPS,))],
        args=(x, g, win_g), comms=comms)


def _proj_in_bwd_tail(dh, win_v, c):
    dxn = _dot_nt(dh[:, 0:c], win_v[0])
    for q in range(1, N_CHIPS):
        dxn = dxn + _dot_nt(dh[:, q * c:(q + 1) * c], win_v[q])
    return dxn


def _prev_halo_spec(tm, cols):
    return pl.BlockSpec((HALO, cols), lambda i: (jnp.maximum(i * (tm // HALO) - 1, 0), 0))


def _next_halo_spec(tm, cols, t):
    last = t // HALO - 1
    return pl.BlockSpec((HALO, cols), lambda i: (jnp.minimum((i + 1) * (tm // HALO), last), 0))


def _shift_down(w, k):
    return w if k == 0 else pltpu.roll(w, k, 0)


def _shift_up(w, k):
    return w if k == 0 else pltpu.roll(w, w.shape[0] - k, 0)


def _pool_counts(i, tm):
    pos = (i * tm + lax.broadcasted_iota(jnp.int32, (tm, POOL_CH), 0) + 1).astype(F32)
    lane = lax.broadcasted_iota(jnp.int32, (tm, POOL_CH), 1)
    win = jnp.where(lane < POOL_GC, 2.0, jnp.where(lane < 2 * POOL_GC, 4.0, jnp.where(lane < 3 * POOL_GC, 8.0, 16.0)))
    return jnp.minimum(pos, win)


def _group_select(parts):
    return jnp.concatenate([p[:, k * POOL_GC:(k + 1) * POOL_GC] for k, p in enumerate(parts)], axis=1)


def _mix0_recompute(i, tm, h_cur, h_prev, conv_w, conv_b, y=None):
    prev = jnp.where(i > 0, h_prev, 0.0)
    win = jnp.concatenate([prev, h_cur], axis=0)
    u_w = win[:, 0:POOL_CH]
    a_w = win[:, POOL_CH:POOL_CH + CONV_CH]
    gt_w = win[:, POOL_CH + CONV_CH:]
    g_w = a_w * _sigmoid(gt_w)
    if y is None:
        y = jnp.zeros((tm, CONV_CH), F32)
        for k in range(CONV_WIDTH):
            y = y + conv_w[k:k + 1, :] * _shift_down(g_w, CONV_WIDTH - 1 - k)[HALO:, :]
        y = y + conv_b
    s2 = u_w + _shift_down(u_w, 1)
    s4 = s2 + _shift_down(s2, 2)
    s8 = s4 + _shift_down(s4, 4)
    s16 = s8 + _shift_down(s8, 8)
    sums = _group_select([s2[HALO:], s4[HALO:], s8[HALO:], s16[HALO:]])
    cnt = _pool_counts(i, tm)
    pooled = sums / cnt - h_cur[:, 0:POOL_CH]
    return g_w, y, pooled, cnt


def _pool_linear(pooled, pw_ref, pb):
    return jnp.concatenate(
        [_dot(pooled[:, k * POOL_GC:(k + 1) * POOL_GC].astype(BF16), pw_ref[k].astype(BF16)) for k in range(len(POOL_WINDOWS))], axis=1) + pb


def _mix0_fwd(x, h0, pool_w, pool_b, pool_scale, conv_w, conv_b, ln_g, ln_b, wout_g, name, comms=()):
    t, d = x.shape
    tm = min(256, t)
    hc = h0.shape[1]

    def body(x_ref, h_ref, hp_ref, pw_ref, pb_ref, ps_ref, cw_ref, cb_ref, lg_ref, lb_ref, wout_hbm, xo_ref, ycat_ref, y_ref, wout_v, sem):
        _load_at_first_step(_out_weight_copies(wout_hbm, wout_v, sem))
        i = pl.program_id(0)
        _, y, pooled, _ = _mix0_recompute(i, tm, h_ref[...], hp_ref[...], cw_ref[...], cb_ref[...])
        y_ref[...] = y
        yhat, _ = _ln_fwd(y)
        yn = yhat * lg_ref[...] + lb_ref[...]
        yb = yn * _sigmoid(yn)
        ya = _pool_linear(pooled, pw_ref, pb_ref[...]) * ps_ref[...]
        ycat = jnp.concatenate([ya, yb], axis=1).astype(BF16)
        ycat_ref[...] = ycat
        xo_ref[...] = x_ref[...] + _dot(ycat, wout_v[...])

    return _call(
        body, name=name, grid=(t // tm,),
        in_specs=[_row_spec(tm, d), _row_spec(tm, hc), _prev_halo_spec(tm, hc), _const_spec(pool_w.shape), _const_spec((1, POOL_CH)),
                  _const_spec((1, POOL_CH)), _const_spec(conv_w.shape), _const_spec((1, CONV_CH)), _const_spec((1, CONV_CH)),
                  _const_spec((1, CONV_CH)), HBM_SPEC],
        out_specs=[_row_spec(tm, d), _row_spec(tm, d), _row_spec(tm, CONV_CH)],
        out_shape=[jax.ShapeDtypeStruct((t, d), F32), jax.ShapeDtypeStruct((t, d), BF16), jax.ShapeDtypeStruct((t, CONV_CH), F32)],
        scratch_shapes=[pltpu.VMEM((d, d), BF16), pltpu.SemaphoreType.DMA((N_CHIPS,))],
        args=(x, h0, h0, pool_w, pool_b, pool_scale, conv_w, conv_b, ln_g, ln_b, wout_g), comms=comms)


def _mix0_bwd_a(dy, h0, y_conv, pool_w, pool_b, pool_scale, conv_w, conv_b, ln_g, ln_b, wout_g, name):
    t, d = dy.shape
    tm = min(256, t)
    hc = h0.shape[1]
    n_small = 40

    def body(dy_ref, h_ref, hp_ref, y_ref, pw_ref, pb_ref, ps_ref, cw_ref, cb_ref, lg_ref, lb_ref, wout_hbm,
             dconv_ref, dpc_ref, dpw_ref, small_ref, wout_v, sem):
        _load_at_first_step(_out_weight_copies(wout_hbm, wout_v, sem))
        i = pl.program_id(0)

        @pl.when(i == 0)
        def _():
            dpw_ref[...] = jnp.zeros_like(dpw_ref)
            small_ref[...] = jnp.zeros_like(small_ref)

        g_w, y, pooled, cnt = _mix0_recompute(i, tm, h_ref[...], hp_ref[...], cw_ref[...], cb_ref[...], y_ref[...])
        yhat, rs = _ln_fwd(y)
        lg = lg_ref[...]
        yn = yhat * lg + lb_ref[...]
        mixed = _pool_linear(pooled, pw_ref, pb_ref[...])
        dycat = _dot_nt(dy_ref[...].astype(BF16), wout_v[...])
        dya, dyb = dycat[:, 0:POOL_CH], dycat[:, POOL_CH:]
        sg = _sigmoid(yn)
        dyn = dyb * (sg * (1.0 + yn * (1.0 - sg)))
        dyc = _ln_bwd(dyn * lg, yhat, rs)
        dconv_ref[...] = dyc

        def add_row(k, value):
            small_ref[k:k + 1, :] += jnp.sum(value, axis=0, keepdims=True)

        for k in range(CONV_WIDTH):
            add_row(k, dyc * _shift_down(g_w, CONV_WIDTH - 1 - k)[HALO:, :])
        add_row(32, dyc)
        add_row(33, dyn * yhat)
        add_row(34, dyn)
        scale = ps_ref[...]
        dmixed = dya * scale
        add_row(35, dya * mixed)
        add_row(36, dmixed)
        dmb = dmixed.astype(BF16)
        dpooled = []
        for k in range(len(POOL_WINDOWS)):
            sl = slice(k * POOL_GC, (k + 1) * POOL_GC)
            dpw_ref[k] += _dot_tn(pooled[:, sl].astype(BF16), dmb[:, sl])
            dpooled.append(_dot_nt(dmb[:, sl], pw_ref[k].astype(BF16)))
        dpc_ref[...] = jnp.concatenate(dpooled, axis=1) / cnt

    return _call(
        body, name=name, grid=(t // tm,),
        in_specs=[_row_spec(tm, d), _row_spec(tm, hc), _prev_halo_spec(tm, hc), _row_spec(tm, CONV_CH), _const_spec(pool_w.shape),
                  _const_spec((1, POOL_CH)), _const_spec((1, POOL_CH)), _const_spec(conv_w.shape), _const_spec((1, CONV_CH)),
                  _const_spec((1, CONV_CH)), _const_spec((1, CONV_CH)), HBM_SPEC],
        out_specs=[_row_spec(tm, CONV_CH), _row_spec(tm, POOL_CH), _const_spec(pool_w.shape), _const_spec((n_small, CONV_CH))],
        out_shape=[jax.ShapeDtypeStruct((t, CONV_CH), F32), jax.ShapeDtypeStruct((t, POOL_CH), F32),
                   jax.ShapeDtypeStruct(pool_w.shape, F32), jax.ShapeDtypeStruct((n_small, CONV_CH), F32)],
        scratch_shapes=[pltpu.VMEM((d, d), BF16), pltpu.SemaphoreType.DMA((N_CHIPS,))],
        args=(dy, h0, h0, y_conv, pool_w, pool_b, pool_scale, conv_w, conv_b, ln_g, ln_b, wout_g))


def _mix0_bwd_b(x, dy, h0, dconv, dpc, g, conv_w, win_g, name):
    t, d = x.shape
    tm = min(256, t)
    hc = h0.shape[1]
    c = win_g.shape[-1]
    n_tiles = t // tm

    def body(x_ref, dy_ref, h_ref, dc_ref, dcn_ref, dp_ref, dpn_ref, g_ref, cw_ref, win_hbm, dx_ref, dh_ref, dg_ref, win_v, sem):
        _load_at_first_step(_in_weight_copies(win_hbm, win_v, sem))
        i = pl.program_id(0)

        @pl.when(i == 0)
        def _():
            dg_ref[...] = jnp.zeros_like(dg_ref)

        not_last = i < n_tiles - 1
        dc_w = jnp.concatenate([dc_ref[...], jnp.where(not_last, dcn_ref[...], 0.0)], axis=0)
        dp_w = jnp.concatenate([dp_ref[...], jnp.where(not_last, dpn_ref[...], 0.0)], axis=0)
        cw = cw_ref[...]
        dg = jnp.zeros((tm, CONV_CH), F32)
        for k in range(CONV_WIDTH):
            dg = dg + cw[k:k + 1, :] * _shift_up(dc_w, CONV_WIDTH - 1 - k)[0:tm, :]
        a2 = dp_w + _shift_up(dp_w, 1)
        a4 = a2 + _shift_up(a2, 2)
        a8 = a4 + _shift_up(a4, 4)
        a16 = a8 + _shift_up(a8, 8)
        back = _group_select([a2[0:tm], a4[0:tm], a8[0:tm], a16[0:tm]])
        du = back - dp_ref[...] * _pool_counts(i, tm)
        hv = h_ref[...]
        a = hv[:, POOL_CH:POOL_CH + CONV_CH]
        sig = _sigmoid(hv[:, POOL_CH + CONV_CH:])
        dh = jnp.concatenate([du, dg * sig, dg * a * sig * (1.0 - sig)], axis=1).astype(BF16)
        dh_ref[...] = dh
        dxn = _proj_in_bwd_tail(dh, win_v, c)
        xh, r = _rms_fwd(x_ref[...])
        dg_ref[...] += jnp.sum(dxn * xh, axis=0, keepdims=True)
        dx_ref[...] = dy_ref[...] + _rms_bwd(dxn, xh, r, g_ref[...])

    return _call(
        body, name=name, grid=(n_tiles,),
        in_specs=[_row_spec(tm, d), _row_spec(tm, d), _row_spec(tm, hc), _row_spec(tm, CONV_CH), _next_halo_spec(tm, CONV_CH, t),
                  _row_spec(tm, POOL_CH), _next_halo_spec(tm, POOL_CH, t), _const_spec((1, d)), _const_spec(conv_w.shape), HBM_SPEC],
        out_specs=[_row_spec(tm, d), _row_spec(tm, hc), _const_spec((1, d))],
        out_shape=[jax.ShapeDtypeStruct((t, d), F32), jax.ShapeDtypeStruct((t, hc), BF16), jax.ShapeDtypeStruct((1, d), F32)],
        scratch_shapes=[pltpu.VMEM((N_CHIPS, d, c), BF16), pltpu.SemaphoreType.DMA((N_CHIPS,))],
        args=(x, dy, h0, dconv, dconv, dpc, dpc, g, conv_w, win_g))


SQRT_HALF = 0.7071067811865476
INV_SQRT_2PI = 0.3989422804014327


def _causal_mask():
    return (lax.broadcasted_iota(jnp.int32, (CHUNK, CHUNK), 1) <= lax.broadcasted_iota(jnp.int32, (CHUNK, CHUNK), 0)).astype(F32)


def _sgu_recompute(pre, lg, lb):
    half = pre.shape[1] // 2
    phi = 0.5 * (1.0 + lax.erf(pre * SQRT_HALF))
    z = pre * phi
    u, v = z[:, 0:half], z[:, half:]
    vhat, rs = _ln_fwd(v)
    return u, vhat, rs, vhat * lg + lb, phi


def _sgu_spatial(vln, w_ref, bt, tm):
    mask = _causal_mask()
    wm = [(w_ref[hd] * mask).astype(BF16) for hd in range(SGU_HEADS)]
    vb = vln.astype(BF16)
    rows = []
    for ch in range(tm // CHUNK):
        blocks = [_dot(wm[hd], vb[ch * CHUNK:(ch + 1) * CHUNK, hd * CHUNK:(hd + 1) * CHUNK]) + bt[:, hd:hd + 1] for hd in range(SGU_HEADS)]
        rows.append(jnp.concatenate(blocks, axis=1))
    return jnp.concatenate(rows, axis=0), wm


def _sgu_fwd(x, pre, ln_g, ln_b, w, bt, wout_g, name):
    t, d = x.shape
    tm = min(256, t)
    pc = pre.shape[1]

    def body(x_ref, pre_ref, lg_ref, lb_ref, w_ref, bt_ref, wout_hbm, xo_ref, p_ref, wout_v, sem):
        _load_at_first_step(_out_weight_copies(wout_hbm, wout_v, sem))
        u, _, _, vln, _ = _sgu_recompute(pre_ref[...], lg_ref[...], lb_ref[...])
        vo, _ = _sgu_spatial(vln, w_ref, bt_ref[...], tm)
        p = (u * vo).astype(BF16)
        p_ref[...] = p
        xo_ref[...] = x_ref[...] + _dot(p, wout_v[...])

    return _call(
        body, name=name, grid=(t // tm,),
        in_specs=[_row_spec(tm, d), _row_spec(tm, pc), _const_spec((1, d)), _const_spec((1, d)), _const_spec(w.shape),
                  _const_spec(bt.shape), HBM_SPEC],
        out_specs=[_row_spec(tm, d), _row_spec(tm, d)],
        out_shape=[jax.ShapeDtypeStruct((t, d), F32), jax.ShapeDtypeStruct((t, d), BF16)],
        scratch_shapes=[pltpu.VMEM((d, d), BF16), pltpu.SemaphoreType.DMA((N_CHIPS,))],
        args=(x, pre, ln_g, ln_b, w, bt, wout_g))


def _sgu_bwd(x, dy, pre, g, ln_g, ln_b, w, bt, win_g, wout_g, name, comms=()):
    t, d = x.shape
    tm = min(256, t)
    pc = pre.shape[1]
    c = win_g.shape[-1]

    def body(x_ref, dy_ref, pre_ref, g_ref, lg_ref, lb_ref, w_ref, bt_ref, win_hbm, wout_hbm,
             dx_ref, dpre_ref, dg_ref, dlg_ref, dlb_ref, dw_ref, dbt_ref, win_v, wout_v, sem):
        _load_at_first_step(_in_weight_copies(win_hbm, win_v, sem) + _out_weight_copies(wout_hbm, wout_v, sem, N_CHIPS))
        i = pl.program_id(0)

        @pl.when(i == 0)
        def _():
            for ref in (dg_ref, dlg_ref, dlb_ref, dw_ref, dbt_ref):
                ref[...] = jnp.zeros_like(ref)

        prev = pre_ref[...]
        lg = lg_ref[...]
        u, vhat, rs, vln, phi = _sgu_recompute(prev, lg, lb_ref[...])
        vo, wm = _sgu_spatial(vln, w_ref, bt_ref[...], tm)
        dp = _dot_nt(dy_ref[...].astype(BF16), wout_v[...])
        du = dp * vo
        dvo = dp * u
        dvob = dvo.astype(BF16)
        vb = vln.astype(BF16)
        head_lane = lax.broadcasted_iota(jnp.int32, (CHUNK, SGU_HEADS), 1)
        dbt = jnp.zeros((CHUNK, SGU_HEADS), F32)
        dw = [jnp.zeros((CHUNK, CHUNK), F32) for _ in range(SGU_HEADS)]
        rows = []
        for ch in range(tm // CHUNK):
            rs_ = slice(ch * CHUNK, (ch + 1) * CHUNK)
            blocks = []
            for hd in range(SGU_HEADS):
                cs = slice(hd * CHUNK, (hd + 1) * CHUNK)
                dbt = dbt + jnp.where(head_lane == hd, jnp.sum(dvo[rs_, cs], axis=1, keepdims=True), 0.0)
                dw[hd] = dw[hd] + _dot_nt(dvob[rs_, cs], vb[rs_, cs])
                blocks.append(_dot_tn(wm[hd], dvob[rs_, cs]))
            rows.append(jnp.concatenate(blocks, axis=1))
        dvln = jnp.concatenate(rows, axis=0)
        mask = _causal_mask()
        for hd in range(SGU_HEADS):
            dw_ref[hd] += dw[hd] * mask
        dbt_ref[...] += dbt
        dlg_ref[...] += jnp.sum(dvln * vhat, axis=0, keepdims=True)
        dlb_ref[...] += jnp.sum(dvln, axis=0, keepdims=True)
        dv = _ln_bwd(dvln * lg, vhat, rs)
        gelu_grad = phi + prev * jnp.exp(-0.5 * prev * prev) * INV_SQRT_2PI
        dpre = (jnp.concatenate([du, dv], axis=1) * gelu_grad).astype(BF16)
        dpre_ref[...] = dpre
        dxn = _proj_in_bwd_tail(dpre, win_v, c)
        xh, r = _rms_fwd(x_ref[...])
        dg_ref[...] += jnp.sum(dxn * xh, axis=0, keepdims=True)
        dx_ref[...] = dy_ref[...] + _rms_bwd(dxn, xh, r, g_ref[...])

    return _call(
        body, name=name, grid=(t // tm,),
        in_specs=[_row_spec(tm, d), _row_spec(tm, d), _row_spec(tm, pc), _const_spec((1, d)), _const_spec((1, d)), _const_spec((1, d)),
                  _const_spec(w.shape), _const_spec(bt.shape), HBM_SPEC, HBM_SPEC],
        out_specs=[_row_spec(tm, d), _row_spec(tm, pc), _const_spec((1, d)), _const_spec((1, d)), _const_spec((1, d)),
                   _const_spec(w.shape), _const_spec(bt.shape)],
        out_shape=[jax.ShapeDtypeStruct((t, d), F32), jax.ShapeDtypeStruct((t, pc), BF16), jax.ShapeDtypeStruct((1, d), F32),
                   jax.ShapeDtypeStruct((1, d), F32), jax.ShapeDtypeStruct((1, d), F32), jax.ShapeDtypeStruct(w.shape, F32),
                   jax.ShapeDtypeStruct(bt.shape, F32)],
        scratch_shapes=[pltpu.VMEM((N_CHIPS, d, c), BF16), pltpu.VMEM((d, d), BF16), pltpu.SemaphoreType.DMA((2 * N_CHIPS,))],
        args=(x, dy, pre, g, ln_g, ln_b, w, bt, win_g, wout_g), comms=comms)


def _final_loss(x, tgt, g, name):
    t, d = x.shape
    tm = min(512, t)

    def body(x_ref, t_ref, g_ref, dx_ref, loss_ref, dg_ref):
        @pl.when(pl.program_id(0) == 0)
        def _():
            loss_ref[...] = jnp.zeros_like(loss_ref)
            dg_ref[...] = jnp.zeros_like(dg_ref)

        gv = g_ref[...]
        xh, r = _rms_fwd(x_ref[...])
        diff = xh * gv - t_ref[...]
        loss_ref[...] += 0.5 * jnp.sum(jnp.sum(diff * diff, axis=1, keepdims=True), axis=0, keepdims=True) / d
        dout = diff / d
        dg_ref[...] += jnp.sum(dout * xh, axis=0, keepdims=True)
        dx_ref[...] = _rms_bwd(dout, xh, r, gv)

    return _call(
        body, name=name, grid=(t // tm,),
        in_specs=[_row_spec(tm, d), _row_spec(tm, d), _const_spec((1, d))],
        out_specs=[_row_spec(tm, d), _const_spec((1, 1)), _const_spec((1, d))],
        out_shape=[jax.ShapeDtypeStruct((t, d), F32), jax.ShapeDtypeStruct((1, 1), F32), jax.ShapeDtypeStruct((1, d), F32)],
        args=(x, tgt, g))


def _tn_matmul(a, b, scale, bm, bn, name, comms=()):
    t, m = a.shape
    n = b.shape[1]
    tk = min(1024, t)
    bm, bn = min(bm, m), min(bn, n)
    nk = t // tk

    def body(a_ref, b_ref, o_ref, acc_ref):
        k = pl.program_id(2)

        @pl.when(k == 0)
        def _():
            acc_ref[...] = jnp.zeros_like(acc_ref)

        bv = b_ref[...]
        if bv.dtype != BF16:
            bv = (scale * bv).astype(BF16)
        acc_ref[...] += _dot_tn(a_ref[...], bv)

        @pl.when(k == nk - 1)
        def _():
            o_ref[...] = acc_ref[...].astype(BF16)

    return _call(
        body, name=name, grid=(m // bm, n // bn, nk),
        in_specs=[pl.BlockSpec((tk, bm), lambda i, j, k: (k, i)), pl.BlockSpec((tk, bn), lambda i, j, k: (k, j))],
        out_specs=[pl.BlockSpec((bm, bn), lambda i, j, k: (i, j))],
        out_shape=[jax.ShapeDtypeStruct((m, n), BF16)],
        scratch_shapes=[pltpu.VMEM((bm, bn), F32)],
        args=(a, b), comms=comms)[0]


def _row_tile(rows, cols, budget_bytes=2 * 1024 * 1024):
    best = None
    for cand in range(16, rows + 1, 16):
        if rows % cand == 0 and cand * cols * 4 <= budget_bytes:
            best = cand
    return best or rows


def _scalar_grid(grid, in_specs, out_specs):
    return pltpu.PrefetchScalarGridSpec(num_scalar_prefetch=1, grid=grid, in_specs=in_specs, out_specs=out_specs)


def _cast_into_slot(w, layer, me, name):
    _, rows, cols = w.shape
    tr = _row_tile(rows, cols)

    def body(me_ref, w_ref, o_ref):
        o_ref[...] = w_ref[...].astype(BF16)

    return pl.pallas_call(
        body, name=name,
        grid_spec=_scalar_grid((rows // tr,), [pl.BlockSpec((None, tr, cols), lambda i, me: (layer, i, 0))],
                               pl.BlockSpec((None, tr, cols), lambda i, me: (me[0], i, 0))),
        out_shape=jax.ShapeDtypeStruct((N_CHIPS, rows, cols), BF16), compiler_params=_cparams())(me, w)


def _add_half(view, other, core, name):
    q, _, r, c = view.shape
    tr = _row_tile(r, c)

    def body(core_ref, a_ref, b_ref, o_ref):
        o_ref[...] = (a_ref[...].astype(F32) + b_ref[...].astype(F32)).astype(BF16)

    return pl.pallas_call(
        body, name=name,
        grid_spec=_scalar_grid((q, r // tr), [pl.BlockSpec((None, None, tr, c), lambda k, i, core: (k, core[0], i, 0)),
                                             pl.BlockSpec((None, tr, c), lambda k, i, core: (k, i, 0))],
                               pl.BlockSpec((None, tr, c), lambda k, i, core: (k, i, 0))),
        out_shape=jax.ShapeDtypeStruct((q, r, c), BF16), compiler_params=_cparams_nd(2))(core, view, other)


def _reduce_piece(partial, staged, me, column_sharded, name):
    _, r, c = staged.shape
    tr = _row_tile(r, c, budget_bytes=1024 * 1024)
    nt = r // tr
    if column_sharded:
        own2d = partial.reshape(r, N_CHIPS * c)
        own_spec = pl.BlockSpec((tr, c), lambda i, me: (i, me[0]))
    else:
        own2d = partial.reshape(N_CHIPS * r, c)
        own_spec = pl.BlockSpec((tr, c), lambda i, me: (me[0] * nt + i, 0))
    ring = [pl.BlockSpec((None, tr, c), lambda i, me, k=k: ((me[0] + k) % N_CHIPS, i, 0)) for k in (1, 2, 3)]

    def body(me_ref, own_ref, s1_ref, s2_ref, s3_ref, o_ref):
        o_ref[...] = ((own_ref[...].astype(F32) + s1_ref[...].astype(F32)) + s2_ref[...].astype(F32)) + s3_ref[...].astype(F32)

    return pl.pallas_call(
        body, name=name, grid_spec=_scalar_grid((nt,), [own_spec] + ring, pl.BlockSpec((tr, c), lambda i, me: (i, 0))),
        out_shape=jax.ShapeDtypeStruct((r, c), F32), compiler_params=_cparams())(me, own2d, staged, staged, staged)


def _sum_leading(s, name):
    n, rows, cols = s.shape
    tr = _row_tile(rows, cols, budget_bytes=1024 * 1024)

    def body(s_ref, o_ref):
        acc = s_ref[0].astype(F32)
        for k in range(1, n):
            acc = acc + s_ref[k].astype(F32)
        o_ref[...] = acc

    return pl.pallas_call(
        body, name=name, grid=(rows // tr,), in_specs=[pl.BlockSpec((n, tr, cols), lambda i: (0, i, 0))], out_specs=_row_spec(tr, cols),
        out_shape=jax.ShapeDtypeStruct((rows, cols), F32), compiler_params=_cparams())(s)


ADAM_C1 = 1.0 / (1.0 - ADAM_B1 ** ADAM_STEP)
ADAM_C2 = 1.0 / (1.0 - ADAM_B2 ** ADAM_STEP)


def _adamw_math(w, g, m, v):
    mn = ADAM_B1 * m + (1.0 - ADAM_B1) * g
    vn = ADAM_B2 * v + (1.0 - ADAM_B2) * (g * g)
    return -ADAM_LR * ((mn * ADAM_C1) / (jnp.sqrt(vn * ADAM_C2) + ADAM_EPS) + ADAM_WD * w), mn, vn


def _adamw(w, g, m, v, name):
    shape = w.shape
    cols = shape[-1] if w.ndim > 1 else 128
    w2, g2, m2, v2 = (a.reshape(-1, cols) for a in (w, g, m, v))
    rows = w2.shape[0]
    tr = _row_tile(rows, cols, budget_bytes=1024 * 1024)

    def body(w_ref, g_ref, m_ref, v_ref, d_ref, mo_ref, vo_ref):
        d_ref[...], mo_ref[...], vo_ref[...] = _adamw_math(w_ref[...], g_ref[...], m_ref[...], v_ref[...])

    spec = _row_spec(tr, cols)
    outs = pl.pallas_call(
        body, name=name, grid=(rows // tr,), in_specs=[spec] * 4, out_specs=[spec] * 3,
        out_shape=[jax.ShapeDtypeStruct((rows, cols), F32)] * 3, compiler_params=_cparams())(w2, g2, m2, v2)
    return tuple(o.reshape(shape) for o in outs)


def _adamw_sharded(w, g_mine, g_sibling, m, v, core, layer, prev, name, comms=()):
    n_layers, r, c = w.shape
    half = r // 2
    tr = _row_tile(half, c)
    nt = half // tr

    def body(core_ref, w_ref, gm_ref, gs_ref, m_ref, v_ref, *rest):
        g_ref, d_ref, mo_ref, vo_ref = rest[-4:]
        gv = jnp.where(pl.program_id(0) == core_ref[0], gm_ref[...], gs_ref[...])
        g_ref[...] = gv
        d_ref[...], mo_ref[...], vo_ref[...] = _adamw_math(w_ref[...], gv, m_ref[...], v_ref[...])

    full = pl.BlockSpec((None, tr, c), lambda h, i, core: (layer, h * nt + i, 0))
    part = pl.BlockSpec((tr, c), lambda h, i, core: (i, 0))
    args = [w, g_mine, g_sibling, m, v]
    in_specs = [full, part, part, full, full]
    aliases = {}
    if prev is not None:
        aliases = {len(args) + k: k for k in range(4)}
        args += list(prev)
        in_specs += [pl.BlockSpec(memory_space=pl.ANY)] * 4
    return _call(body, name=name, grid=(2, nt), in_specs=in_specs, out_specs=[full] * 4, out_shape=[jax.ShapeDtypeStruct(w.shape, F32)] * 4,
                 args=args, comms=comms, scalar=core, aliases=aliases)


BIG_IN = ("ffn1_w_in", "ffn2_w_in", "ab_w_in", "sgu_w_in")
BIG_OUT = ("ffn1_w_out", "ffn2_w_out", "ab_w_out", "sgu_w_out")
BIG = BIG_IN + BIG_OUT


class _Gatherer:
    def __init__(self, slots):
        self.slots = dict(slots)

    def _stage(self, keys, d2d):
        n = len(keys)

        def plan(ins, outs, place):
            x, y, c = place
            me = 2 * x + y
            remote = []
            for a in range(n):
                rows = ins[a].shape[1] // 2

                def half(ref, q, core, rows=rows):
                    return ref.at[q, pl.ds(core * rows, rows), :]

                for (px, py) in _other_chips(x, y):
                    q = 2 * px + py
                    if d2d:
                        remote.append((half(ins[a], q, c), half(outs[a], q, c), (x, y, 1 - c), half(outs[a], q, 1 - c)))
                    else:
                        remote.append((half(ins[a], me, c), half(outs[a], me, c), (px, py, c), half(outs[a], q, c)))
            return remote

        def finish(outs):
            for k, o in zip(keys, outs):
                self.slots[k] = o

        arrays = [self.slots[k] for k in keys]
        return _Exchange(arrays, [_sds(a) for a in arrays], plan, 3 * n, {a: a for a in range(n)}, finish)

    def direct(self, keys):
        n = len(keys)

        def plan(ins, outs, place):
            x, y, c = place
            me = 2 * x + y
            return [(ins[a].at[me], outs[a].at[me], (px, py, c), outs[a].at[2 * px + py]) for a in range(n) for (px, py) in _other_chips(x, y)]

        def finish(outs):
            for k, o in zip(keys, outs):
                self.slots[k] = o

        arrays = [self.slots[k] for k in keys]
        return _Exchange(arrays, [_sds(a) for a in arrays], plan, 3 * n, {a: a for a in range(n)}, finish)

    def ici(self, keys):
        return self._stage(keys, False)

    def d2d(self, keys):
        return self._stage(keys, True)


class _Reducer:
    def __init__(self, me, core):
        self.me, self.core = me, core
        self.views, self.partial, self.staged, self.mine, self.theirs = {}, {}, {}, {}, {}

    def add(self, key, g):
        m, n = g.shape
        if key[0] in BIG_IN:
            self.views[key] = g.reshape(1, 2, m // 2, n)
        else:
            self.views[key] = g.reshape(N_CHIPS, 2, m // (2 * N_CHIPS), n)

    def swap(self, keys):
        views = [self.views[k] for k in keys]

        def plan(ins, outs, place):
            x, y, c = place
            return [(ins[a].at[:, 1 - c], outs[a], (x, y, 1 - c), outs[a]) for a in range(len(keys))]

        def finish(outs):
            for k, v, o in zip(keys, views, outs):
                self.partial[k] = _add_half(v, o, self.core, f"chip_partial_{k[0]}_{k[1]}")

        shapes = [jax.ShapeDtypeStruct((v.shape[0],) + v.shape[2:], v.dtype) for v in views]
        return _Exchange(views, shapes, plan, len(keys), None, finish)

    def scatter(self, keys, part=(0, 1)):
        i, n = part
        n_keys = len(keys)
        parts = [self.partial[k] for k in keys]
        shapes = []
        for k, p in zip(keys, parts):
            q, r, c = p.shape
            shapes.append(jax.ShapeDtypeStruct((N_CHIPS, r, c // N_CHIPS if k[0] in BIG_IN else c), p.dtype))

        def piece(ref, key, q, rows, cols):
            return ref.at[0, rows, pl.ds(q * cols, cols)] if key[0] in BIG_IN else ref.at[q, rows, :]

        def plan(ins, outs, place):
            x, y, c = place
            me = 2 * x + y
            remote = []
            for a, k in enumerate(keys):
                _, r, cols = shapes[a].shape
                rows = pl.ds(i * (r // n), r // n)
                for (px, py) in _other_chips(x, y):
                    q = 2 * px + py
                    remote.append((piece(ins[a], k, q, rows, cols), outs[a].at[me, rows, :], (px, py, c), outs[a].at[q, rows, :]))
            return remote

        def finish(outs):
            for k, p, o in zip(keys, parts, outs):
                self.staged[k] = o
                if i == n - 1:
                    self.mine[k] = _reduce_piece(p, o, self.me, k[0] in BIG_IN, f"reduce_{k[0]}_{k[1]}")

        inputs, aliases = parts, None
        if i > 0:
            inputs = parts + [self.staged[k] for k in keys]
            aliases = {n_keys + a: a for a in range(n_keys)}
        return _Exchange(inputs, shapes, plan, 3 * n_keys, aliases, finish)

    def exchange(self, keys):
        mine = [self.mine[k] for k in keys]

        def plan(ins, outs, place):
            x, y, c = place
            return [(ins[a], outs[a], (x, y, 1 - c), outs[a]) for a in range(len(keys))]

        def finish(outs):
            for k, o in zip(keys, outs):
                self.theirs[k] = o

        return _Exchange(mine, [_sds(a) for a in mine], plan, len(keys), None, finish)


def _all_gather_full(gat, keys):
    n = len(keys)
    arrays = [gat.slots[k] for k in keys]

    def body(*refs):
        ins, outs = refs[:n], refs[n:2 * n]
        send_sems, recv_sems = refs[2 * n:]
        x, y, c = _my_place()
        sibling = (x, y, 1 - c)
        chips = _other_chips(x, y)
        me = 2 * x + y

        def half(ref, q, core):
            rows = ref.shape[1] // 2
            return ref.at[q, pl.ds(core * rows, rows), :]

        def copy(a, k, src, dst, to):
            return pltpu.make_async_remote_copy(src_ref=src, dst_ref=dst, send_sem=send_sems.at[6 * a + k], recv_sem=recv_sems.at[6 * a + k],
                                                device_id=to, device_id_type=MESH)

        first = [copy(a, j, half(ins[a], me, c), half(outs[a], me, c), (*chip, c)) for a in range(n) for j, chip in enumerate(chips)]
        for cp in first:
            cp.start()
        passed = []
        for a in range(n):
            for j, (px, py) in enumerate(chips):
                landed = half(outs[a], 2 * px + py, c)
                copy(a, j, landed, landed, (px, py, c)).wait_recv()
                fwd = copy(a, 3 + j, landed, landed, sibling)
                fwd.start()
                passed.append(fwd)
        for a in range(n):
            for j, (px, py) in enumerate(chips):
                other = half(outs[a], 2 * px + py, 1 - c)
                copy(a, 3 + j, other, other, sibling).wait_recv()
        for cp in first + passed:
            cp.wait_send()

    outs = pl.pallas_call(
        body, name="all_gather_first_weights", in_specs=[HBM_SPEC] * n, out_specs=[HBM_SPEC] * n,
        out_shape=[_sds(a) for a in arrays], input_output_aliases={a: a for a in range(n)},
        scratch_shapes=[pltpu.SemaphoreType.DMA((6 * n,)), pltpu.SemaphoreType.DMA((6 * n,))])(*arrays)
    for k, o in zip(keys, outs):
        gat.slots[k] = o


def _small_all_gather(buf, done):
    state = {}

    def index(x, y, c):
        return 4 * x + 2 * y + c

    def plan_ici(ins, outs, place):
        x, y, c = place
        return [(ins[0], outs[0].at[index(x, y, c)], (px, py, c), outs[0].at[index(px, py, c)]) for (px, py) in _other_chips(x, y)]

    def local(ins, outs, place):
        return [(ins[0], outs[0].at[index(*place)])]

    def plan_d2d(ins, outs, place):
        x, y, c = place
        return [(ins[0].at[index(px, py, c)], outs[0].at[index(px, py, c)], (x, y, 1 - c), outs[0].at[index(px, py, 1 - c)])
                for (px, py) in [(x, y)] + _other_chips(x, y)]

    def second():
        return _Exchange([state["blocks"]], [_sds(state["blocks"])], plan_d2d, N_CHIPS, {0: 0}, lambda outs: done(outs[0]))

    first = _Exchange([buf], [jax.ShapeDtypeStruct((2 * N_CHIPS,) + buf.shape, buf.dtype)], plan_ici, 3, None,
                      lambda outs: state.update(blocks=outs[0]), local, 1)
    return first, second


WEIGHT_NAMES = ("ffn1_norm", "ffn1_w_in", "ffn1_w_out", "mix_norm", "ffn2_norm", "ffn2_w_in", "ffn2_w_out", "ab_w_in", "pool_w", "pool_b",
                "pool_scale", "conv_w", "conv_b", "conv_ln_g", "conv_ln_b", "ab_w_out", "sgu_w_in", "sgu_ln_g", "sgu_ln_b", "sgu_w", "sgu_b",
                "sgu_w_out", "final_norm")
SMALL = tuple(n for n in WEIGHT_NAMES if n not in BIG)
SHARDED_SMALL = ("conv_w", "sgu_ln_g", "sgu_ln_b")
PACK_ROWS = 48
PACK = ("pack", 0)


def _pair(prefix, layer):
    return [(prefix + "_w_in", layer), (prefix + "_w_out", layer)]


def kernel(x, ffn1_norm, ffn1_w_in, ffn1_w_out, mix_norm, ffn2_norm, ffn2_w_in, ffn2_w_out, ab_w_in, pool_w, pool_b, pool_scale, conv_w, conv_b, conv_ln_g, conv_ln_b, ab_w_out, sgu_w_in, sgu_ln_g, sgu_ln_b, sgu_w, sgu_b, sgu_w_out, final_norm, loss_target, m_ffn1_norm, m_ffn1_w_in, m_ffn1_w_out, m_mix_norm, m_ffn2_norm, m_ffn2_w_in, m_ffn2_w_out, m_ab_w_in, m_pool_w, m_pool_b, m_pool_scale, m_conv_w, m_conv_b, m_conv_ln_g, m_conv_ln_b, m_ab_w_out, m_sgu_w_in, m_sgu_ln_g, m_sgu_ln_b, m_sgu_w, m_sgu_b, m_sgu_w_out, m_final_norm, v_ffn1_norm, v_ffn1_w_in, v_ffn1_w_out, v_mix_norm, v_ffn2_norm, v_ffn2_w_in, v_ffn2_w_out, v_ab_w_in, v_pool_w, v_pool_b, v_pool_scale, v_conv_w, v_conv_b, v_conv_ln_g, v_conv_ln_b, v_ab_w_out, v_sgu_w_in, v_sgu_ln_g, v_sgu_ln_b, v_sgu_w, v_sgu_b, v_sgu_w_out, v_final_norm):
    given = dict(locals())
    w = {n: given[n] for n in WEIGHT_NAMES}
    chip = 2 * lax.axis_index("x") + lax.axis_index("y")
    me = chip.astype(jnp.int32).reshape(1)
    core = lax.axis_index("c").astype(jnp.int32).reshape(1)
    row = lambda v: v.reshape(1, -1)
    xin, tgt = x[0], loss_target[0]

    pack = jnp.concatenate([
        w["conv_w"][0], jnp.zeros((1, 128), F32), w["sgu_ln_g"].reshape(2, 128), w["sgu_ln_b"].reshape(2, 128),
        jnp.zeros((PACK_ROWS - 36, 128), F32)], axis=0)
    slots = {PACK: lax.dynamic_update_slice(jnp.zeros((N_CHIPS, PACK_ROWS, 128), F32), pack[None], (me[0], 0, 0))}
    for n in BIG:
        for layer in range(w[n].shape[0]):
            slots[(n, layer)] = _cast_into_slot(w[n], layer, me, f"cast_{n}_{layer}")
    gat = _Gatherer(slots)
    _all_gather_full(gat, _pair("ffn1", 0) + [PACK])
    gp = gat.slots[PACK]
    conv_w_full = jnp.transpose(gp[:, 0:CONV_WIDTH], (1, 0, 2)).reshape(CONV_WIDTH, N_CHIPS * 128)
    sgu_ln_g_full = gp[:, 32:34].reshape(1, -1)
    sgu_ln_b_full = gp[:, 34:36].reshape(1, -1)
    gw = lambda n, layer: gat.slots[(n, layer)]

    st = [dict(), dict()]
    st[0]["xa"] = xin
    later = _pair("sgu", 0) + _pair("ffn2", 1)
    cur, st[0]["h1"], st[0]["xn1"] = _ffn_fwd(xin, row(w["ffn1_norm"][0]), gw("ffn1_w_in", 0), gw("ffn1_w_out", 0), "ffn1_fwd_0",
                                              comms=[gat.direct(_pair("ab", 0)), gat.ici(_pair("ffn2", 0))])
    st[0]["xb"] = cur
    st[0]["h0"], st[0]["xnm"] = _norm_matmul(cur, row(w["mix_norm"][0]), gw("ab_w_in", 0), "mix0_proj_in",
                                             comms=[gat.d2d(_pair("ffn2", 0))])
    pool_args = (w["pool_w"][0], row(w["pool_b"][0]), row(w["pool_scale"][0]), conv_w_full, row(w["conv_b"][0]), row(w["conv_ln_g"][0]),
                 row(w["conv_ln_b"][0]), gw("ab_w_out", 0))
    cur, st[0]["ycat"], st[0]["yconv"] = _mix0_fwd(cur, st[0]["h0"], *pool_args, "mix0_fwd", comms=[gat.ici(_pair("ffn1", 1))])
    st[0]["xc"] = cur
    cur, st[0]["h2"], st[0]["xn2"] = _ffn_fwd(cur, row(w["ffn2_norm"][0]), gw("ffn2_w_in", 0), gw("ffn2_w_out", 0), "ffn2_fwd_0",
                                              comms=[gat.d2d(_pair("ffn1", 1)), gat.ici(later)])
    st[1]["xa"] = cur
    cur, st[1]["h1"], st[1]["xn1"] = _ffn_fwd(cur, row(w["ffn1_norm"][1]), gw("ffn1_w_in", 1), gw("ffn1_w_out", 1), "ffn1_fwd_1",
                                              comms=[gat.d2d(later)])
    st[1]["xb"] = cur
    st[1]["pre"], st[1]["xnm"] = _norm_matmul(cur, row(w["mix_norm"][1]), gw("sgu_w_in", 0), "sgu_proj_in")
    sgu_args = (sgu_ln_g_full, sgu_ln_b_full, w["sgu_w"][0], w["sgu_b"][0].T)
    cur, st[1]["p"] = _sgu_fwd(cur, st[1]["pre"], *sgu_args, gw("sgu_w_out", 0), "sgu_fwd")
    st[1]["xc"] = cur
    cur, st[1]["h2"], st[1]["xn2"] = _ffn_fwd(cur, row(w["ffn2_norm"][1]), gw("ffn2_w_in", 1), gw("ffn2_w_out", 1), "ffn2_fwd_1")
    dy, loss, d_final = _final_loss(cur, tgt, row(w["final_norm"]), "final_loss")

    red = _Reducer(me, core)
    small = {"final_norm": d_final.reshape(-1)}
    norm_grads = {"ffn1_norm": [None] * DEPTH, "mix_norm": [None] * DEPTH, "ffn2_norm": [None] * DEPTH}
    ga, gb, gc, gd, ge, gf = _pair("ffn2", 1), _pair("sgu", 0), _pair("ffn1", 1), _pair("ffn2", 0), _pair("ab", 0), _pair("ffn1", 0)

    def ffn_backward(prefix, layer, xs, hs, xns, dy_in, bwd_comms=(), dwin_comms=(), dwout_comms=()):
        dx, dh, act, norm_grads[prefix + "_norm"][layer] = _ffn_bwd(
            xs, dy_in, hs, row(w[prefix + "_norm"][layer]), gw(prefix + "_w_in", layer), gw(prefix + "_w_out", layer),
            f"{prefix}_bwd_{layer}", comms=bwd_comms)
        red.add((prefix + "_w_in", layer), _tn_matmul(xns, dh, 1.0, 1024, 1408, f"{prefix}_dwin_{layer}", comms=dwin_comms))
        red.add((prefix + "_w_out", layer), _tn_matmul(act, dy_in, 0.5, 1408, 1024, f"{prefix}_dwout_{layer}", comms=dwout_comms))
        return dx

    s1, s0 = st[1], st[0]
    dy = ffn_backward("ffn2", 1, s1["xc"], s1["h2"], s1["xn2"], dy)
    dy_in = dy
    dy, dpre, norm_grads["mix_norm"][1], dlg, dlb, dw, dbt = _sgu_bwd(
        s1["xb"], dy_in, s1["pre"], row(w["mix_norm"][1]), *sgu_args, gw("sgu_w_in", 0), gw("sgu_w_out", 0), "sgu_bwd", comms=[red.swap(ga)])
    red.add(("sgu_w_in", 0), _tn_matmul(s1["xnm"], dpre, 1.0, 1024, 2048, "sgu_dwin"))
    red.add(("sgu_w_out", 0), _tn_matmul(s1["p"], dy_in, 1.0, 1024, 1024, "sgu_dwout"))
    small.update(sgu_ln_g=dlg, sgu_ln_b=dlb, sgu_w=dw[None], sgu_b=dbt.T[None])
    dy = ffn_backward("ffn1", 1, s1["xa"], s1["h1"], s1["xn1"], dy, bwd_comms=[lambda: red.scatter(ga), lambda: red.swap(gb)],
                      dwin_comms=[lambda: red.scatter(gb), lambda: red.exchange(ga)])
    dy = ffn_backward("ffn2", 0, s0["xc"], s0["h2"], s0["xn2"], dy, bwd_comms=[lambda: red.swap(gc), lambda: red.exchange(gb)],
                      dwin_comms=[lambda: red.scatter(gc)])
    dy_in = dy
    dconv, dpc, dpw, rows = _mix0_bwd_a(dy_in, s0["h0"], s0["yconv"], *pool_args, "mix0_bwd_a")
    dy, dh0, norm_grads["mix_norm"][0] = _mix0_bwd_b(s0["xb"], dy_in, s0["h0"], dconv, dpc, row(w["mix_norm"][0]), conv_w_full,
                                                      gw("ab_w_in", 0), "mix0_bwd_b")
    red.add(("ab_w_in", 0), _tn_matmul(s0["xnm"], dh0, 1.0, 1024, 1536, "ab_dwin", comms=[red.swap(gd), red.exchange(gc)]))
    red.add(("ab_w_out", 0), _tn_matmul(s0["ycat"], dy_in, 1.0, 1024, 1024, "ab_dwout"))
    small.update(pool_w=dpw[None], conv_w=rows[None, 0:CONV_WIDTH], conv_b=rows[32:33], conv_ln_g=rows[33:34], conv_ln_b=rows[34:35],
                 pool_scale=rows[35:36], pool_b=rows[36:37].reshape(1, len(POOL_WINDOWS), POOL_GC))

    small_sum = {}

    def small_ready():
        for k, v in norm_grads.items():
            small[k] = jnp.concatenate(v, axis=0)
        flat = [small[n].reshape(-1, 128) for n in SMALL]
        rows = sum(f.shape[0] for f in flat)
        loss_block = jnp.pad(loss, ((0, 8 + (-rows) % 8 - 1), (0, 127)))
        buf = jnp.concatenate(flat + [loss_block], axis=0)

        def done(gathered):
            total, at = _sum_leading(gathered, "reduce_small"), 0
            for n, f in zip(SMALL, flat):
                small_sum[n] = total[at:at + f.shape[0]].reshape(small[n].shape)
                at += f.shape[0]
            small_sum["loss"] = total[at:at + 1, 0:1]

        return _small_all_gather(buf, done)

    dx, dh, act, norm_grads["ffn1_norm"][0] = _ffn_bwd(
        s0["xa"], dy, s0["h1"], row(w["ffn1_norm"][0]), gw("ffn1_w_in", 0), gw("ffn1_w_out", 0), "ffn1_bwd_0")
    small_first, small_second = small_ready()
    red.add(("ffn1_w_in", 0), _tn_matmul(s0["xn1"], dh, 1.0, 1024, 1408, "ffn1_dwin_0",
                                         comms=[red.scatter(gd), red.swap(ge), small_first]))
    red.add(("ffn1_w_out", 0), _tn_matmul(act, dy, 0.5, 1408, 1024, "ffn1_dwout_0",
                                          comms=[red.scatter(ge), red.exchange(gd), small_second]))
    grad_x = dx

    big_out = {}

    def adamw_big(n, layer, comms=()):
        big_out[n] = _adamw_sharded(w[n], red.mine[(n, layer)], red.theirs[(n, layer)], given["m_" + n], given["v_" + n], core, layer,
                                    big_out.get(n), f"adamw_{n}_{layer}", comms=comms)

    _exchange_alone("swap_last_grads", [red.swap(gf), red.exchange(ge)])
    _exchange_alone("scatter_last_grads", [red.scatter(gf)])
    _exchange_alone("exchange_last_grads", [red.exchange(gf)])
    for n in BIG:
        for layer in reversed(range(w[n].shape[0])):
            adamw_big(n, layer)

    loss = small_sum["loss"][0, 0]
    grads, delta, new_m, new_v = {}, {}, {}, {}
    for n in WEIGHT_NAMES:
        mom, var = given["m_" + n], given["v_" + n]
        if n in BIG:
            grads[n], delta[n], new_m[n], new_v[n] = big_out[n]
            continue
        g = small_sum[n]
        if n in SHARDED_SMALL:
            width = w[n].shape[-1]
            g = lax.dynamic_slice_in_dim(g, chip * width, width, axis=g.ndim - 1)
        grads[n] = g
        delta[n], new_m[n], new_v[n] = _adamw(w[n], g, mom, var, f"adamw_{n}")
    return (loss, grad_x[None], *[grads[n] for n in WEIGHT_NAMES], *[delta[n] for n in WEIGHT_NAMES],
            *[new_m[n] for n in WEIGHT_NAMES], *[new_v[n] for n in WEIGHT_NAMES])
```

```python
import jax
import jax.numpy as jnp
from jax import lax
from jax.experimental import pallas as pl
from jax.experimental.pallas import tpu as pltpu

F32, BF16 = jnp.float32, jnp.bfloat16
EPS = 1e-6
N_CHIPS = 4
POOL_WINDOWS = (2, 4, 8, 16)
POOL_GC = 128
POOL_CH = 512
CONV_CH = 512
CONV_WIDTH = 31
HALO = 32
SGU_HEADS = 8
CHUNK = 128
DEPTH = 2
ADAM_LR, ADAM_B1, ADAM_B2, ADAM_EPS, ADAM_WD, ADAM_STEP = 0.001, 0.9, 0.999, 1e-08, 0.01, 10
VMEM_LIMIT_BYTES = 60 * 1024 * 1024
MESH_AXES = ("x", "y", "c")
MESH = pl.DeviceIdType.MESH
HBM_SPEC = pl.BlockSpec(memory_space=pltpu.HBM)


def _sds(a):
    return jax.ShapeDtypeStruct(a.shape, a.dtype)


def _cparams_nd(n):
    return pltpu.CompilerParams(dimension_semantics=("arbitrary",) * n, vmem_limit_bytes=VMEM_LIMIT_BYTES)


def _cparams():
    return _cparams_nd(1)


def _dot(a, b):
    return jnp.dot(a, b, preferred_element_type=F32)


def _dot_nt(a, b):
    return lax.dot_general(a, b, (((1,), (1,)), ((), ())), preferred_element_type=F32)


def _dot_tn(a, b):
    return lax.dot_general(a, b, (((0,), (0,)), ((), ())), preferred_element_type=F32)


def _rms_fwd(x):
    r = lax.rsqrt(jnp.mean(x * x, axis=-1, keepdims=True) + EPS)
    return x * r, r


def _rms_bwd(dxn, xh, r, g):
    dxh = dxn * g
    return r * (dxh - xh * jnp.mean(dxh * xh, axis=-1, keepdims=True))


def _ln_fwd(y):
    mu = jnp.mean(y, axis=-1, keepdims=True)
    yc = y - mu
    rs = lax.rsqrt(jnp.mean(yc * yc, axis=-1, keepdims=True) + EPS)
    return yc * rs, rs


def _ln_bwd(dyhat, yhat, rs):
    return rs * (dyhat - jnp.mean(dyhat, axis=-1, keepdims=True) - yhat * jnp.mean(dyhat * yhat, axis=-1, keepdims=True))


def _sigmoid(x):
    return 0.5 * jnp.tanh(0.5 * x) + 0.5


def _const_spec(shape):
    n = len(shape)
    return pl.BlockSpec(shape, lambda i: (0,) * n)


def _row_spec(tm, cols):
    return pl.BlockSpec((tm, cols), lambda i: (i, 0))


def _my_place():
    return lax.axis_index("x"), lax.axis_index("y"), lax.axis_index("c")


def _other_chips(x, y):
    return [(1 - x, y), (x, 1 - y), (1 - x, 1 - y)]


class _Exchange:
    def __init__(self, inputs, out_shapes, plan, count, aliases=None, finish=None, local=None, n_local=0):
        self.inputs, self.out_shapes, self.plan, self.count = list(inputs), list(out_shapes), plan, count
        self.aliases, self.finish, self.local, self.n_local = dict(aliases or {}), finish, local, n_local


def _call(body, *, name, grid, in_specs, out_specs, out_shape, args, scratch_shapes=(), comms=(), scalar=None, aliases=None):
    comms = [cm if isinstance(cm, _Exchange) else cm() for cm in comms]
    in_specs, out_specs, out_shape, scratch_shapes = list(in_specs), list(out_specs), list(out_shape), list(scratch_shapes)
    n_in, n_out, n_scr = len(in_specs), len(out_specs), len(scratch_shapes)
    c_in = [a for cm in comms for a in cm.inputs]
    c_out = [s for cm in comms for s in cm.out_shapes]
    n_remote = sum(cm.count for cm in comms)
    n_local = sum(cm.n_local for cm in comms)
    n_scalar = 0 if scalar is None else 1
    all_aliases = {n_scalar + i: o for i, o in (aliases or {}).items()}
    at_in, at_out = n_scalar + n_in, n_out
    for cm in comms:
        for i, o in cm.aliases.items():
            all_aliases[at_in + i] = at_out + o
        at_in += len(cm.inputs)
        at_out += len(cm.out_shapes)

    def wrapped(*all_refs):
        scalar_ref, refs = all_refs[:n_scalar], all_refs[n_scalar:]
        ins, ci = refs[:n_in], refs[n_in:n_in + len(c_in)]
        at = n_in + len(c_in)
        outs, co = refs[at:at + n_out], refs[at + n_out:at + n_out + len(c_out)]
        at += n_out + len(c_out)
        scr = refs[at:at + n_scr]

        def run_body():
            body(*scalar_ref, *ins, *outs, *scr)

        if not comms:
            run_body()
            return
        send_sems, recv_sems, local_sems = refs[at + n_scr:]
        place = _my_place()
        sends, arrivals, locals_ = [], [], []
        i0 = o0 = 0
        for cm in comms:
            cm_in, cm_out = ci[i0:i0 + len(cm.inputs)], co[o0:o0 + len(cm.out_shapes)]
            i0 += len(cm.inputs)
            o0 += len(cm.out_shapes)
            for src, dst, dev, incoming in cm.plan(cm_in, cm_out, place):
                k = len(sends)
                sends.append(pltpu.make_async_remote_copy(src_ref=src, dst_ref=dst, send_sem=send_sems.at[k], recv_sem=recv_sems.at[k],
                                                          device_id=dev, device_id_type=MESH))
                arrivals.append(pltpu.make_async_remote_copy(src_ref=src, dst_ref=incoming, send_sem=send_sems.at[k],
                                                             recv_sem=recv_sems.at[k], device_id=dev, device_id_type=MESH))
            if cm.local is not None:
                for src, dst in cm.local(cm_in, cm_out, place):
                    locals_.append(pltpu.make_async_copy(src, dst, local_sems.at[len(locals_)]))

        def start():
            for cp in locals_ + sends:
                cp.start()

        def finish():
            for cp in arrivals:
                cp.wait_recv()
            for cp in sends:
                cp.wait_send()
            for cp in locals_:
                cp.wait()

        if not grid:
            start()
            run_body()
            finish()
            return
        ids = [pl.program_id(a) for a in range(len(grid))]
        first, last = ids[0] == 0, ids[0] == grid[0] - 1
        for a in range(1, len(grid)):
            first = jnp.logical_and(first, ids[a] == 0)
            last = jnp.logical_and(last, ids[a] == grid[a] - 1)
        pl.when(first)(start)
        run_body()
        pl.when(last)(finish)

    sems = []
    if comms:
        sems = [pltpu.SemaphoreType.DMA((max(n_remote, 1),)), pltpu.SemaphoreType.DMA((max(n_remote, 1),)),
                pltpu.SemaphoreType.DMA((max(n_local, 1),))]
    all_in, all_out = in_specs + [HBM_SPEC] * len(c_in), out_specs + [HBM_SPEC] * len(c_out)
    if scalar is None:
        kwargs = dict(grid=grid, compiler_params=_cparams_nd(len(grid))) if grid else {}
        res = pl.pallas_call(
            wrapped, name=name, in_specs=all_in, out_specs=all_out, out_shape=out_shape + c_out, scratch_shapes=scratch_shapes + sems,
            input_output_aliases=all_aliases, **kwargs)(*args, *c_in)
    else:
        spec = pltpu.PrefetchScalarGridSpec(num_scalar_prefetch=1, grid=grid, in_specs=all_in, out_specs=all_out,
                                            scratch_shapes=scratch_shapes + sems)
        res = pl.pallas_call(
            wrapped, name=name, grid_spec=spec, out_shape=out_shape + c_out, input_output_aliases=all_aliases,
            compiler_params=_cparams_nd(len(grid)))(scalar, *args, *c_in)
    at = n_out
    for cm in comms:
        got = res[at:at + len(cm.out_shapes)]
        at += len(cm.out_shapes)
        if cm.finish is not None:
            cm.finish(got)
    return list(res[:n_out])


def _exchange_alone(name, comms):
    _call(lambda: None, name=name, grid=(), in_specs=[], out_specs=[], out_shape=[], args=[], comms=comms)


def _in_weight_copies(w_hbm, w_v, sem, base=0):
    return [pltpu.make_async_copy(w_hbm.at[q], w_v.at[q], sem.at[base + q]) for q in range(N_CHIPS)]


def _out_weight_copies(w_hbm, w_v, sem, base=0):
    rows = w_hbm.shape[1]
    return [pltpu.make_async_copy(w_hbm.at[q], w_v.at[pl.ds(q * rows, rows)], sem.at[base + q]) for q in range(N_CHIPS)]


def _load_at_first_step(copies):
    @pl.when(pl.program_id(0) == 0)
    def _():
        for cp in copies:
            cp.start()
        for cp in copies:
            cp.wait()


def _ffn_fwd(x, g, win_g, wout_g, name, comms=()):
    t, d = x.shape
    c = win_g.shape[-1]
    ff = 2 * c
    tm = min(512, t)

    def body(x_ref, g_ref, win_hbm, wout_hbm, xo_ref, h_ref, xn_ref, win_v, wout_v, sem):
        _load_at_first_step(_in_weight_copies(win_hbm, win_v, sem) + _out_weight_copies(wout_hbm, wout_v, sem, N_CHIPS))
        xv = x_ref[...]
        xh, _ = _rms_fwd(xv)
        xn = (xh * g_ref[...]).astype(BF16)
        xn_ref[...] = xn
        acc = jnp.zeros((tm, d), F32)
        for j in range(2):
            gate = _dot(xn, win_v[j])
            up = _dot(xn, win_v[j + 2])
            h_ref[:, j * c:(j + 1) * c] = gate.astype(BF16)
            h_ref[:, ff + j * c:ff + (j + 1) * c] = up.astype(BF16)
            act = (gate * _sigmoid(gate) * up).astype(BF16)
            acc = acc + _dot(act, wout_v[j * c:(j + 1) * c, :])
        xo_ref[...] = xv + 0.5 * acc

    return _call(
        body, name=name, grid=(t // tm,),
        in_specs=[_row_spec(tm, d), _const_spec((1, d)), HBM_SPEC, HBM_SPEC],
        out_specs=[_row_spec(tm, d), _row_spec(tm, 2 * ff), _row_spec(tm, d)],
        out_shape=[jax.ShapeDtypeStruct((t, d), F32), jax.ShapeDtypeStruct((t, 2 * ff), BF16), jax.ShapeDtypeStruct((t, d), BF16)],
        scratch_shapes=[pltpu.VMEM((N_CHIPS, d, c), BF16), pltpu.VMEM((ff, d), BF16), pltpu.SemaphoreType.DMA((2 * N_CHIPS,))],
        args=(x, g, win_g, wout_g), comms=comms)


def _ffn_bwd(x, dy, h, g, win_g, wout_g, name, comms=()):
    t, d = x.shape
    c = win_g.shape[-1]
    ff = 2 * c
    tm = min(256, t)

    def body(x_ref, dy_ref, h_ref, g_ref, win_hbm, wout_hbm, dx_ref, dh_ref, act_ref, dg_ref, win_v, wout_v, sem):
        _load_at_first_step(_in_weight_copies(win_hbm, win_v, sem) + _out_weight_copies(wout_hbm, wout_v, sem, N_CHIPS))

        @pl.when(pl.program_id(0) == 0)
        def _():
            dg_ref[...] = jnp.zeros_like(dg_ref)

        xv, dyv, gv = x_ref[...], dy_ref[...], g_ref[...]
        xh, r = _rms_fwd(xv)
        dyh = (0.5 * dyv).astype(BF16)
        dxn = jnp.zeros((tm, d), F32)
        for j in range(2):
            gate = h_ref[:, j * c:(j + 1) * c].astype(F32)
            up = h_ref[:, ff + j * c:ff + (j + 1) * c].astype(F32)
            dact = _dot_nt(dyh, wout_v[j * c:(j + 1) * c, :])
            s = _sigmoid(gate)
            sl = gate * s
            act_ref[:, j * c:(j + 1) * c] = (sl * up).astype(BF16)
            dgate = (dact * up * (s + sl * (1.0 - s))).astype(BF16)
            dup = (dact * sl).astype(BF16)
            dh_ref[:, j * c:(j + 1) * c] = dgate
            dh_ref[:, ff + j * c:ff + (j + 1) * c] = dup
            dxn = dxn + _dot_nt(dgate, win_v[j]) + _dot_nt(dup, win_v[j + 2])
        dg_ref[...] += jnp.sum(dxn * xh, axis=0, keepdims=True)
        dx_ref[...] = dyv + _rms_bwd(dxn, xh, r, gv)

    return _call(
        body, name=name, grid=(t // tm,),
        in_specs=[_row_spec(tm, d), _row_spec(tm, d), _row_spec(tm, 2 * ff), _const_spec((1, d)), HBM_SPEC, HBM_SPEC],
        out_specs=[_row_spec(tm, d), _row_spec(tm, 2 * ff), _row_spec(tm, ff), _const_spec((1, d))],
        out_shape=[jax.ShapeDtypeStruct((t, d), F32), jax.ShapeDtypeStruct((t, 2 * ff), BF16), jax.ShapeDtypeStruct((t, ff), BF16),
                   jax.ShapeDtypeStruct((1, d), F32)],
        scratch_shapes=[pltpu.VMEM((N_CHIPS, d, c), BF16), pltpu.VMEM((ff, d), BF16), pltpu.SemaphoreType.DMA((2 * N_CHIPS,))],
        args=(x, dy, h, g, win_g, wout_g), comms=comms)


def _norm_matmul(x, g, win_g, name, comms=()):
    t, d = x.shape
    c = win_g.shape[-1]
    tm = min(512, t)

    def body(x_ref, g_ref, win_hbm, o_ref, xn_ref, win_v, sem):
        _load_at_first_step(_in_weight_copies(win_hbm, win_v, sem))
        xh, _ = _rms_fwd(x_ref[...])
        xn = (xh * g_ref[...]).astype(BF16)
        xn_ref[...] = xn
        for q in range(N_CHIPS):
            o_ref[:, q * c:(q + 1) * c] = _dot(xn, win_v[q])

    return _call(
        body, name=name, grid=(t // tm,),
        in_specs=[_row_spec(tm, d), _const_spec((1, d)), HBM_SPEC],
        out_specs=[_row_spec(tm, N_CHIPS * c), _row_spec(tm, d)],
        out_shape=[jax.ShapeDtypeStruct((t, N_CHIPS * c), F32), jax.ShapeDtypeStruct((t, d), BF16)],
        scratch_shapes=[pltpu.VMEM((N_CHIPS, d, c), BF16), pltpu.SemaphoreType.DMA((N_CHIPS,))],
        args=(x, g, win_g), comms=comms)


def _proj_in_bwd_tail(dh, win_v, c):
    dxn = _dot_nt(dh[:, 0:c], win_v[0])
    for q in range(1, N_CHIPS):
        dxn = dxn + _dot_nt(dh[:, q * c:(q + 1) * c], win_v[q])
    return dxn


def _prev_halo_spec(tm, cols):
    return pl.BlockSpec((HALO, cols), lambda i: (jnp.maximum(i * (tm // HALO) - 1, 0), 0))


def _next_halo_spec(tm, cols, t):
    last = t // HALO - 1
    return pl.BlockSpec((HALO, cols), lambda i: (jnp.minimum((i + 1) * (tm // HALO), last), 0))


def _shift_down(w, k):
    return w if k == 0 else pltpu.roll(w, k, 0)


def _shift_up(w, k):
    return w if k == 0 else pltpu.roll(w, w.shape[0] - k, 0)


def _pool_counts(i, tm):
    pos = (i * tm + lax.broadcasted_iota(jnp.int32, (tm, POOL_CH), 0) + 1).astype(F32)
    lane = lax.broadcasted_iota(jnp.int32, (tm, POOL_CH), 1)
    win = jnp.where(lane < POOL_GC, 2.0, jnp.where(lane < 2 * POOL_GC, 4.0, jnp.where(lane < 3 * POOL_GC, 8.0, 16.0)))
    return jnp.minimum(pos, win)


def _group_select(parts):
    return jnp.concatenate([p[:, k * POOL_GC:(k + 1) * POOL_GC] for k, p in enumerate(parts)], axis=1)


def _mix0_recompute(i, tm, h_cur, h_prev, conv_w, conv_b, y=None):
    prev = jnp.where(i > 0, h_prev, 0.0)
    win = jnp.concatenate([prev, h_cur], axis=0)
    u_w = win[:, 0:POOL_CH]
    a_w = win[:, POOL_CH:POOL_CH + CONV_CH]
    gt_w = win[:, POOL_CH + CONV_CH:]
    g_w = a_w * _sigmoid(gt_w)
    if y is None:
        y = jnp.zeros((tm, CONV_CH), F32)
        for k in range(CONV_WIDTH):
            y = y + conv_w[k:k + 1, :] * _shift_down(g_w, CONV_WIDTH - 1 - k)[HALO:, :]
        y = y + conv_b
    s2 = u_w + _shift_down(u_w, 1)
    s4 = s2 + _shift_down(s2, 2)
    s8 = s4 + _shift_down(s4, 4)
    s16 = s8 + _shift_down(s8, 8)
    sums = _group_select([s2[HALO:], s4[HALO:], s8[HALO:], s16[HALO:]])
    cnt = _pool_counts(i, tm)
    pooled = sums / cnt - h_cur[:, 0:POOL_CH]
    return g_w, y, pooled, cnt


def _pool_linear(pooled, pw_ref, pb):
    return jnp.concatenate(
        [_dot(pooled[:, k * POOL_GC:(k + 1) * POOL_GC].astype(BF16), pw_ref[k].astype(BF16)) for k in range(len(POOL_WINDOWS))], axis=1) + pb


def _mix0_fwd(x, h0, pool_w, pool_b, pool_scale, conv_w, conv_b, ln_g, ln_b, wout_g, name, comms=()):
    t, d = x.shape
    tm = min(256, t)
    hc = h0.shape[1]

    def body(x_ref, h_ref, hp_ref, pw_ref, pb_ref, ps_ref, cw_ref, cb_ref, lg_ref, lb_ref, wout_hbm, xo_ref, ycat_ref, y_ref, wout_v, sem):
        _load_at_first_step(_out_weight_copies(wout_hbm, wout_v, sem))
        i = pl.program_id(0)
        _, y, pooled, _ = _mix0_recompute(i, tm, h_ref[...], hp_ref[...], cw_ref[...], cb_ref[...])
        y_ref[...] = y
        yhat, _ = _ln_fwd(y)
        yn = yhat * lg_ref[...] + lb_ref[...]
        yb = yn * _sigmoid(yn)
        ya = _pool_linear(pooled, pw_ref, pb_ref[...]) * ps_ref[...]
        ycat = jnp.concatenate([ya, yb], axis=1).astype(BF16)
        ycat_ref[...] = ycat
        xo_ref[...] = x_ref[...] + _dot(ycat, wout_v[...])

    return _call(
        body, name=name, grid=(t // tm,),
        in_specs=[_row_spec(tm, d), _row_spec(tm, hc), _prev_halo_spec(tm, hc), _const_spec(pool_w.shape), _const_spec((1, POOL_CH)),
                  _const_spec((1, POOL_CH)), _const_spec(conv_w.shape), _const_spec((1, CONV_CH)), _const_spec((1, CONV_CH)),
                  _const_spec((1, CONV_CH)), HBM_SPEC],
        out_specs=[_row_spec(tm, d), _row_spec(tm, d), _row_spec(tm, CONV_CH)],
        out_shape=[jax.ShapeDtypeStruct((t, d), F32), jax.ShapeDtypeStruct((t, d), BF16), jax.ShapeDtypeStruct((t, CONV_CH), F32)],
        scratch_shapes=[pltpu.VMEM((d, d), BF16), pltpu.SemaphoreType.DMA((N_CHIPS,))],
        args=(x, h0, h0, pool_w, pool_b, pool_scale, conv_w, conv_b, ln_g, ln_b, wout_g), comms=comms)


def _mix0_bwd_a(dy, h0, y_conv, pool_w, pool_b, pool_scale, conv_w, conv_b, ln_g, ln_b, wout_g, name):
    t, d = dy.shape
    tm = min(256, t)
    hc = h0.shape[1]
    n_small = 40

    def body(dy_ref, h_ref, hp_ref, y_ref, pw_ref, pb_ref, ps_ref, cw_ref, cb_ref, lg_ref, lb_ref, wout_hbm,
             dconv_ref, dpc_ref, dpw_ref, small_ref, wout_v, sem):
        _load_at_first_step(_out_weight_copies(wout_hbm, wout_v, sem))
        i = pl.program_id(0)

        @pl.when(i == 0)
        def _():
            dpw_ref[...] = jnp.zeros_like(dpw_ref)
            small_ref[...] = jnp.zeros_like(small_ref)

        g_w, y, pooled, cnt = _mix0_recompute(i, tm, h_ref[...], hp_ref[...], cw_ref[...], cb_ref[...], y_ref[...])
        yhat, rs = _ln_fwd(y)
        lg = lg_ref[...]
        yn = yhat * lg + lb_ref[...]
        mixed = _pool_linear(pooled, pw_ref, pb_ref[...])
        dycat = _dot_nt(dy_ref[...].astype(BF16), wout_v[...])
        dya, dyb = dycat[:, 0:POOL_CH], dycat[:, POOL_CH:]
        sg = _sigmoid(yn)
        dyn = dyb * (sg * (1.0 + yn * (1.0 - sg)))
        dyc = _ln_bwd(dyn * lg, yhat, rs)
        dconv_ref[...] = dyc

        def add_row(k, value):
            small_ref[k:k + 1, :] += jnp.sum(value, axis=0, keepdims=True)

        for k in range(CONV_WIDTH):
            add_row(k, dyc * _shift_down(g_w, CONV_WIDTH - 1 - k)[HALO:, :])
        add_row(32, dyc)
        add_row(33, dyn * yhat)
        add_row(34, dyn)
        scale = ps_ref[...]
        dmixed = dya * scale
        add_row(35, dya * mixed)
        add_row(36, dmixed)
        dmb = dmixed.astype(BF16)
        dpooled = []
        for k in range(len(POOL_WINDOWS)):
            sl = slice(k * POOL_GC, (k + 1) * POOL_GC)
            dpw_ref[k] += _dot_tn(pooled[:, sl].astype(BF16), dmb[:, sl])
            dpooled.append(_dot_nt(dmb[:, sl], pw_ref[k].astype(BF16)))
        dpc_ref[...] = jnp.concatenate(dpooled, axis=1) / cnt

    return _call(
        body, name=name, grid=(t // tm,),
        in_specs=[_row_spec(tm, d), _row_spec(tm, hc), _prev_halo_spec(tm, hc), _row_spec(tm, CONV_CH), _const_spec(pool_w.shape),
                  _const_spec((1, POOL_CH)), _const_spec((1, POOL_CH)), _const_spec(conv_w.shape), _const_spec((1, CONV_CH)),
                  _const_spec((1, CONV_CH)), _const_spec((1, CONV_CH)), HBM_SPEC],
        out_specs=[_row_spec(tm, CONV_CH), _row_spec(tm, POOL_CH), _const_spec(pool_w.shape), _const_spec((n_small, CONV_CH))],
        out_shape=[jax.ShapeDtypeStruct((t, CONV_CH), F32), jax.ShapeDtypeStruct((t, POOL_CH), F32),
                   jax.ShapeDtypeStruct(pool_w.shape, F32), jax.ShapeDtypeStruct((n_small, CONV_CH), F32)],
        scratch_shapes=[pltpu.VMEM((d, d), BF16), pltpu.SemaphoreType.DMA((N_CHIPS,))],
        args=(dy, h0, h0, y_conv, pool_w, pool_b, pool_scale, conv_w, conv_b, ln_g, ln_b, wout_g))


def _mix0_bwd_b(x, dy, h0, dconv, dpc, g, conv_w, win_g, name):
    t, d = x.shape
    tm = min(256, t)
    hc = h0.shape[1]
    c = win_g.shape[-1]
    n_tiles = t // tm

    def body(x_ref, dy_ref, h_ref, dc_ref, dcn_ref, dp_ref, dpn_ref, g_ref, cw_ref, win_hbm, dx_ref, dh_ref, dg_ref, win_v, sem):
        _load_at_first_step(_in_weight_copies(win_hbm, win_v, sem))
        i = pl.program_id(0)

        @pl.when(i == 0)
        def _():
            dg_ref[...] = jnp.zeros_like(dg_ref)

        not_last = i < n_tiles - 1
        dc_w = jnp.concatenate([dc_ref[...], jnp.where(not_last, dcn_ref[...], 0.0)], axis=0)
        dp_w = jnp.concatenate([dp_ref[...], jnp.where(not_last, dpn_ref[...], 0.0)], axis=0)
        cw = cw_ref[...]
        dg = jnp.zeros((tm, CONV_CH), F32)
        for k in range(CONV_WIDTH):
            dg = dg + cw[k:k + 1, :] * _shift_up(dc_w, CONV_WIDTH - 1 - k)[0:tm, :]
        a2 = dp_w + _shift_up(dp_w, 1)
        a4 = a2 + _shift_up(a2, 2)
        a8 = a4 + _shift_up(a4, 4)
        a16 = a8 + _shift_up(a8, 8)
        back = _group_select([a2[0:tm], a4[0:tm], a8[0:tm], a16[0:tm]])
        du = back - dp_ref[...] * _pool_counts(i, tm)
        hv = h_ref[...]
        a = hv[:, POOL_CH:POOL_CH + CONV_CH]
        sig = _sigmoid(hv[:, POOL_CH + CONV_CH:])
        dh = jnp.concatenate([du, dg * sig, dg * a * sig * (1.0 - sig)], axis=1).astype(BF16)
        dh_ref[...] = dh
        dxn = _proj_in_bwd_tail(dh, win_v, c)
        xh, r = _rms_fwd(x_ref[...])
        dg_ref[...] += jnp.sum(dxn * xh, axis=0, keepdims=True)
        dx_ref[...] = dy_ref[...] + _rms_bwd(dxn, xh, r, g_ref[...])

    return _call(
        body, name=name, grid=(n_tiles,),
        in_specs=[_row_spec(tm, d), _row_spec(tm, d), _row_spec(tm, hc), _row_spec(tm, CONV_CH), _next_halo_spec(tm, CONV_CH, t),
                  _row_spec(tm, POOL_CH), _next_halo_spec(tm, POOL_CH, t), _const_spec((1, d)), _const_spec(conv_w.shape), HBM_SPEC],
        out_specs=[_row_spec(tm, d), _row_spec(tm, hc), _const_spec((1, d))],
        out_shape=[jax.ShapeDtypeStruct((t, d), F32), jax.ShapeDtypeStruct((t, hc), BF16), jax.ShapeDtypeStruct((1, d), F32)],
        scratch_shapes=[pltpu.VMEM((N_CHIPS, d, c), BF16), pltpu.SemaphoreType.DMA((N_CHIPS,))],
        args=(x, dy, h0, dconv, dconv, dpc, dpc, g, conv_w, win_g))


SQRT_HALF = 0.7071067811865476
INV_SQRT_2PI = 0.3989422804014327


def _causal_mask():
    return (lax.broadcasted_iota(jnp.int32, (CHUNK, CHUNK), 1) <= lax.broadcasted_iota(jnp.int32, (CHUNK, CHUNK), 0)).astype(F32)


def _sgu_recompute(pre, lg, lb):
    half = pre.shape[1] // 2
    phi = 0.5 * (1.0 + lax.erf(pre * SQRT_HALF))
    z = pre * phi
    u, v = z[:, 0:half], z[:, half:]
    vhat, rs = _ln_fwd(v)
    return u, vhat, rs, vhat * lg + lb, phi


def _sgu_spatial(vln, w_ref, bt, tm):
    mask = _causal_mask()
    wm = [(w_ref[hd] * mask).astype(BF16) for hd in range(SGU_HEADS)]
    vb = vln.astype(BF16)
    rows = []
    for ch in range(tm // CHUNK):
        blocks = [_dot(wm[hd], vb[ch * CHUNK:(ch + 1) * CHUNK, hd * CHUNK:(hd + 1) * CHUNK]) + bt[:, hd:hd + 1] for hd in range(SGU_HEADS)]
        rows.append(jnp.concatenate(blocks, axis=1))
    return jnp.concatenate(rows, axis=0), wm


def _sgu_fwd(x, pre, ln_g, ln_b, w, bt, wout_g, name):
    t, d = x.shape
    tm = min(256, t)
    pc = pre.shape[1]

    def body(x_ref, pre_ref, lg_ref, lb_ref, w_ref, bt_ref, wout_hbm, xo_ref, p_ref, wout_v, sem):
        _load_at_first_step(_out_weight_copies(wout_hbm, wout_v, sem))
        u, _, _, vln, _ = _sgu_recompute(pre_ref[...], lg_ref[...], lb_ref[...])
        vo, _ = _sgu_spatial(vln, w_ref, bt_ref[...], tm)
        p = (u * vo).astype(BF16)
        p_ref[...] = p
        xo_ref[...] = x_ref[...] + _dot(p, wout_v[...])

    return _call(
        body, name=name, grid=(t // tm,),
        in_specs=[_row_spec(tm, d), _row_spec(tm, pc), _const_spec((1, d)), _const_spec((1, d)), _const_spec(w.shape),
                  _const_spec(bt.shape), HBM_SPEC],
        out_specs=[_row_spec(tm, d), _row_spec(tm, d)],
        out_shape=[jax.ShapeDtypeStruct((t, d), F32), jax.ShapeDtypeStruct((t, d), BF16)],
        scratch_shapes=[pltpu.VMEM((d, d), BF16), pltpu.SemaphoreType.DMA((N_CHIPS,))],
        args=(x, pre, ln_g, ln_b, w, bt, wout_g))


def _sgu_bwd(x, dy, pre, g, ln_g, ln_b, w, bt, win_g, wout_g, name, comms=()):
    t, d = x.shape
    tm = min(256, t)
    pc = pre.shape[1]
    c = win_g.shape[-1]

    def body(x_ref, dy_ref, pre_ref, g_ref, lg_ref, lb_ref, w_ref, bt_ref, win_hbm, wout_hbm,
             dx_ref, dpre_ref, dg_ref, dlg_ref, dlb_ref, dw_ref, dbt_ref, win_v, wout_v, sem):
        _load_at_first_step(_in_weight_copies(win_hbm, win_v, sem) + _out_weight_copies(wout_hbm, wout_v, sem, N_CHIPS))
        i = pl.program_id(0)

        @pl.when(i == 0)
        def _():
            for ref in (dg_ref, dlg_ref, dlb_ref, dw_ref, dbt_ref):
                ref[...] = jnp.zeros_like(ref)

        prev = pre_ref[...]
        lg = lg_ref[...]
        u, vhat, rs, vln, phi = _sgu_recompute(prev, lg, lb_ref[...])
        vo, wm = _sgu_spatial(vln, w_ref, bt_ref[...], tm)
        dp = _dot_nt(dy_ref[...].astype(BF16), wout_v[...])
        du = dp * vo
        dvo = dp * u
        dvob = dvo.astype(BF16)
        vb = vln.astype(BF16)
        head_lane = lax.broadcasted_iota(jnp.int32, (CHUNK, SGU_HEADS), 1)
        dbt = jnp.zeros((CHUNK, SGU_HEADS), F32)
        dw = [jnp.zeros((CHUNK, CHUNK), F32) for _ in range(SGU_HEADS)]
        rows = []
        for ch in range(tm // CHUNK):
            rs_ = slice(ch * CHUNK, (ch + 1) * CHUNK)
            blocks = []
            for hd in range(SGU_HEADS):
                cs = slice(hd * CHUNK, (hd + 1) * CHUNK)
                dbt = dbt + jnp.where(head_lane == hd, jnp.sum(dvo[rs_, cs], axis=1, keepdims=True), 0.0)
                dw[hd] = dw[hd] + _dot_nt(dvob[rs_, cs], vb[rs_, cs])
                blocks.append(_dot_tn(wm[hd], dvob[rs_, cs]))
            rows.append(jnp.concatenate(blocks, axis=1))
        dvln = jnp.concatenate(rows, axis=0)
        mask = _causal_mask()
        for hd in range(SGU_HEADS):
            dw_ref[hd] += dw[hd] * mask
        dbt_ref[...] += dbt
        dlg_ref[...] += jnp.sum(dvln * vhat, axis=0, keepdims=True)
        dlb_ref[...] += jnp.sum(dvln, axis=0, keepdims=True)
        dv = _ln_bwd(dvln * lg, vhat, rs)
        gelu_grad = phi + prev * jnp.exp(-0.5 * prev * prev) * INV_SQRT_2PI
        dpre = (jnp.concatenate([du, dv], axis=1) * gelu_grad).astype(BF16)
        dpre_ref[...] = dpre
        dxn = _proj_in_bwd_tail(dpre, win_v, c)
        xh, r = _rms_fwd(x_ref[...])
        dg_ref[...] += jnp.sum(dxn * xh, axis=0, keepdims=True)
        dx_ref[...] = dy_ref[...] + _rms_bwd(dxn, xh, r, g_ref[...])

    return _call(
        body, name=name, grid=(t // tm,),
        in_specs=[_row_spec(tm, d), _row_spec(tm, d), _row_spec(tm, pc), _const_spec((1, d)), _const_spec((1, d)), _const_spec((1, d)),
                  _const_spec(w.shape), _const_spec(bt.shape), HBM_SPEC, HBM_SPEC],
        out_specs=[_row_spec(tm, d), _row_spec(tm, pc), _const_spec((1, d)), _const_spec((1, d)), _const_spec((1, d)),
                   _const_spec(w.shape), _const_spec(bt.shape)],
        out_shape=[jax.ShapeDtypeStruct((t, d), F32), jax.ShapeDtypeStruct((t, pc), BF16), jax.ShapeDtypeStruct((1, d), F32),
                   jax.ShapeDtypeStruct((1, d), F32), jax.ShapeDtypeStruct((1, d), F32), jax.ShapeDtypeStruct(w.shape, F32),
                   jax.ShapeDtypeStruct(bt.shape, F32)],
        scratch_shapes=[pltpu.VMEM((N_CHIPS, d, c), BF16), pltpu.VMEM((d, d), BF16), pltpu.SemaphoreType.DMA((2 * N_CHIPS,))],
        args=(x, dy, pre, g, ln_g, ln_b, w, bt, win_g, wout_g), comms=comms)


def _final_loss(x, tgt, g, name):
    t, d = x.shape
    tm = min(512, t)

    def body(x_ref, t_ref, g_ref, dx_ref, loss_ref, dg_ref):
        @pl.when(pl.program_id(0) == 0)
        def _():
            loss_ref[...] = jnp.zeros_like(loss_ref)
            dg_ref[...] = jnp.zeros_like(dg_ref)

        gv = g_ref[...]
        xh, r = _rms_fwd(x_ref[...])
        diff = xh * gv - t_ref[...]
        loss_ref[...] += 0.5 * jnp.sum(jnp.sum(diff * diff, axis=1, keepdims=True), axis=0, keepdims=True) / d
        dout = diff / d
        dg_ref[...] += jnp.sum(dout * xh, axis=0, keepdims=True)
        dx_ref[...] = _rms_bwd(dout, xh, r, gv)

    return _call(
        body, name=name, grid=(t // tm,),
        in_specs=[_row_spec(tm, d), _row_spec(tm, d), _const_spec((1, d))],
        out_specs=[_row_spec(tm, d), _const_spec((1, 1)), _const_spec((1, d))],
        out_shape=[jax.ShapeDtypeStruct((t, d), F32), jax.ShapeDtypeStruct((1, 1), F32), jax.ShapeDtypeStruct((1, d), F32)],
        args=(x, tgt, g))


def _tn_matmul(a, b, scale, bm, bn, name, comms=()):
    t, m = a.shape
    n = b.shape[1]
    tk = min(2048, t)
    bm, bn = min(bm, m), min(bn, n)
    nk = t // tk

    def body(a_ref, b_ref, o_ref, acc_ref):
        k = pl.program_id(2)

        @pl.when(k == 0)
        def _():
            acc_ref[...] = jnp.zeros_like(acc_ref)

        bv = b_ref[...]
        if bv.dtype != BF16:
            bv = (scale * bv).astype(BF16)
        acc_ref[...] += _dot_tn(a_ref[...], bv)

        @pl.when(k == nk - 1)
        def _():
            o_ref[...] = acc_ref[...].astype(BF16)

    return _call(
        body, name=name, grid=(m // bm, n // bn, nk),
        in_specs=[pl.BlockSpec((tk, bm), lambda i, j, k: (k, i)), pl.BlockSpec((tk, bn), lambda i, j, k: (k, j))],
        out_specs=[pl.BlockSpec((bm, bn), lambda i, j, k: (i, j))],
        out_shape=[jax.ShapeDtypeStruct((m, n), BF16)],
        scratch_shapes=[pltpu.VMEM((bm, bn), F32)],
        args=(a, b), comms=comms)[0]


def _row_tile(rows, cols, budget_bytes=2 * 1024 * 1024):
    best = None
    for cand in range(16, rows + 1, 16):
        if rows % cand == 0 and cand * cols * 4 <= budget_bytes:
            best = cand
    return best or rows


def _scalar_grid(grid, in_specs, out_specs):
    return pltpu.PrefetchScalarGridSpec(num_scalar_prefetch=1, grid=grid, in_specs=in_specs, out_specs=out_specs)


def _cast_into_slot(w, layer, me, name):
    _, rows, cols = w.shape
    tr = _row_tile(rows, cols)

    def body(me_ref, w_ref, o_ref):
        o_ref[...] = w_ref[...].astype(BF16)

    return pl.pallas_call(
        body, name=name,
        grid_spec=_scalar_grid((rows // tr,), [pl.BlockSpec((None, tr, cols), lambda i, me: (layer, i, 0))],
                               pl.BlockSpec((None, tr, cols), lambda i, me: (me[0], i, 0))),
        out_shape=jax.ShapeDtypeStruct((N_CHIPS, rows, cols), BF16), compiler_params=_cparams())(me, w)


def _add_half(view, other, core, name):
    q, _, r, c = view.shape
    tr = _row_tile(r, c)

    def body(core_ref, a_ref, b_ref, o_ref):
        o_ref[...] = (a_ref[...].astype(F32) + b_ref[...].astype(F32)).astype(BF16)

    return pl.pallas_call(
        body, name=name,
        grid_spec=_scalar_grid((q, r // tr), [pl.BlockSpec((None, None, tr, c), lambda k, i, core: (k, core[0], i, 0)),
                                             pl.BlockSpec((None, tr, c), lambda k, i, core: (k, i, 0))],
                               pl.BlockSpec((None, tr, c), lambda k, i, core: (k, i, 0))),
        out_shape=jax.ShapeDtypeStruct((q, r, c), BF16), compiler_params=_cparams_nd(2))(core, view, other)


def _reduce_piece(partial, staged, me, column_sharded, name):
    _, r, c = staged.shape
    tr = _row_tile(r, c, budget_bytes=1024 * 1024)
    nt = r // tr
    if column_sharded:
        own2d = partial.reshape(r, N_CHIPS * c)
        own_spec = pl.BlockSpec((tr, c), lambda i, me: (i, me[0]))
    else:
        own2d = partial.reshape(N_CHIPS * r, c)
        own_spec = pl.BlockSpec((tr, c), lambda i, me: (me[0] * nt + i, 0))
    ring = [pl.BlockSpec((None, tr, c), lambda i, me, k=k: ((me[0] + k) % N_CHIPS, i, 0)) for k in (1, 2, 3)]

    def body(me_ref, own_ref, s1_ref, s2_ref, s3_ref, o_ref):
        o_ref[...] = ((own_ref[...].astype(F32) + s1_ref[...].astype(F32)) + s2_ref[...].astype(F32)) + s3_ref[...].astype(F32)

    return pl.pallas_call(
        body, name=name, grid_spec=_scalar_grid((nt,), [own_spec] + ring, pl.BlockSpec((tr, c), lambda i, me: (i, 0))),
        out_shape=jax.ShapeDtypeStruct((r, c), F32), compiler_params=_cparams())(me, own2d, staged, staged, staged)


def _sum_leading(s, name):
    n, rows, cols = s.shape
    tr = _row_tile(rows, cols, budget_bytes=1024 * 1024)

    def body(s_ref, o_ref):
        acc = s_ref[0].astype(F32)
        for k in range(1, n):
            acc = acc + s_ref[k].astype(F32)
        o_ref[...] = acc

    return pl.pallas_call(
        body, name=name, grid=(rows // tr,), in_specs=[pl.BlockSpec((n, tr, cols), lambda i: (0, i, 0))], out_specs=_row_spec(tr, cols),
        out_shape=jax.ShapeDtypeStruct((rows, cols), F32), compiler_params=_cparams())(s)


ADAM_C1 = 1.0 / (1.0 - ADAM_B1 ** ADAM_STEP)
ADAM_C2 = 1.0 / (1.0 - ADAM_B2 ** ADAM_STEP)


def _adamw_math(w, g, m, v):
    mn = ADAM_B1 * m + (1.0 - ADAM_B1) * g
    vn = ADAM_B2 * v + (1.0 - ADAM_B2) * (g * g)
    return -ADAM_LR * ((mn * ADAM_C1) / (jnp.sqrt(vn * ADAM_C2) + ADAM_EPS) + ADAM_WD * w), mn, vn


def _adamw(w, g, m, v, name):
    shape = w.shape
    cols = shape[-1] if w.ndim > 1 else 128
    w2, g2, m2, v2 = (a.reshape(-1, cols) for a in (w, g, m, v))
    rows = w2.shape[0]
    tr = _row_tile(rows, cols, budget_bytes=1024 * 1024)

    def body(w_ref, g_ref, m_ref, v_ref, d_ref, mo_ref, vo_ref):
        d_ref[...], mo_ref[...], vo_ref[...] = _adamw_math(w_ref[...], g_ref[...], m_ref[...], v_ref[...])

    spec = _row_spec(tr, cols)
    outs = pl.pallas_call(
        body, name=name, grid=(rows // tr,), in_specs=[spec] * 4, out_specs=[spec] * 3,
        out_shape=[jax.ShapeDtypeStruct((rows, cols), F32)] * 3, compiler_params=_cparams())(w2, g2, m2, v2)
    return tuple(o.reshape(shape) for o in outs)


def _adamw_sharded(w, g_mine, g_sibling, m, v, core, layer, prev, name, comms=()):
    n_layers, r, c = w.shape
    half = r // 2
    tr = _row_tile(half, c)
    nt = half // tr

    def body(core_ref, w_ref, gm_ref, gs_ref, m_ref, v_ref, *rest):
        g_ref, d_ref, mo_ref, vo_ref = rest[-4:]
        gv = jnp.where(pl.program_id(0) == core_ref[0], gm_ref[...], gs_ref[...])
        g_ref[...] = gv
        d_ref[...], mo_ref[...], vo_ref[...] = _adamw_math(w_ref[...], gv, m_ref[...], v_ref[...])

    full = pl.BlockSpec((None, tr, c), lambda h, i, core: (layer, h * nt + i, 0))
    part = pl.BlockSpec((tr, c), lambda h, i, core: (i, 0))
    args = [w, g_mine, g_sibling, m, v]
    in_specs = [full, part, part, full, full]
    aliases = {}
    if prev is not None:
        aliases = {len(args) + k: k for k in range(4)}
        args += list(prev)
        in_specs += [pl.BlockSpec(memory_space=pl.ANY)] * 4
    return _call(body, name=name, grid=(2, nt), in_specs=in_specs, out_specs=[full] * 4, out_shape=[jax.ShapeDtypeStruct(w.shape, F32)] * 4,
                 args=args, comms=comms, scalar=core, aliases=aliases)


BIG_IN = ("ffn1_w_in", "ffn2_w_in", "ab_w_in", "sgu_w_in")
BIG_OUT = ("ffn1_w_out", "ffn2_w_out", "ab_w_out", "sgu_w_out")
BIG = BIG_IN + BIG_OUT


class _Gatherer:
    def __init__(self, slots):
        self.slots = dict(slots)

    def _stage(self, keys, d2d):
        n = len(keys)

        def plan(ins, outs, place):
            x, y, c = place
            me = 2 * x + y
            remote = []
            for a in range(n):
                rows = ins[a].shape[1] // 2

                def half(ref, q, core, rows=rows):
                    return ref.at[q, pl.ds(core * rows, rows), :]

                for (px, py) in _other_chips(x, y):
                    q = 2 * px + py
                    if d2d:
                        remote.append((half(ins[a], q, c), half(outs[a], q, c), (x, y, 1 - c), half(outs[a], q, 1 - c)))
                    else:
                        remote.append((half(ins[a], me, c), half(outs[a], me, c), (px, py, c), half(outs[a], q, c)))
            return remote

        def finish(outs):
            for k, o in zip(keys, outs):
                self.slots[k] = o

        arrays = [self.slots[k] for k in keys]
        return _Exchange(arrays, [_sds(a) for a in arrays], plan, 3 * n, {a: a for a in range(n)}, finish)

    def direct(self, keys):
        n = len(keys)

        def plan(ins, outs, place):
            x, y, c = place
            me = 2 * x + y
            return [(ins[a].at[me], outs[a].at[me], (px, py, c), outs[a].at[2 * px + py]) for a in range(n) for (px, py) in _other_chips(x, y)]

        def finish(outs):
            for k, o in zip(keys, outs):
                self.slots[k] = o

        arrays = [self.slots[k] for k in keys]
        return _Exchange(arrays, [_sds(a) for a in arrays], plan, 3 * n, {a: a for a in range(n)}, finish)

    def ici(self, keys):
        return self._stage(keys, False)

    def d2d(self, keys):
        return self._stage(keys, True)


class _Reducer:
    def __init__(self, me, core):
        self.me, self.core = me, core
        self.views, self.partial, self.staged, self.mine, self.theirs = {}, {}, {}, {}, {}

    def add(self, key, g):
        m, n = g.shape
        if key[0] in BIG_IN:
            self.views[key] = g.reshape(1, 2, m // 2, n)
        else:
            self.views[key] = g.reshape(N_CHIPS, 2, m // (2 * N_CHIPS), n)

    def swap(self, keys):
        views = [self.views[k] for k in keys]

        def plan(ins, outs, place):
            x, y, c = place
            return [(ins[a].at[:, 1 - c], outs[a], (x, y, 1 - c), outs[a]) for a in range(len(keys))]

        def finish(outs):
            for k, v, o in zip(keys, views, outs):
                self.partial[k] = _add_half(v, o, self.core, f"chip_partial_{k[0]}_{k[1]}")

        shapes = [jax.ShapeDtypeStruct((v.shape[0],) + v.shape[2:], v.dtype) for v in views]
        return _Exchange(views, shapes, plan, len(keys), None, finish)

    def scatter(self, keys, part=(0, 1)):
        i, n = part
        n_keys = len(keys)
        parts = [self.partial[k] for k in keys]
        shapes = []
        for k, p in zip(keys, parts):
            q, r, c = p.shape
            shapes.append(jax.ShapeDtypeStruct((N_CHIPS, r, c // N_CHIPS if k[0] in BIG_IN else c), p.dtype))

        def piece(ref, key, q, rows, cols):
            return ref.at[0, rows, pl.ds(q * cols, cols)] if key[0] in BIG_IN else ref.at[q, rows, :]

        def plan(ins, outs, place):
            x, y, c = place
            me = 2 * x + y
            remote = []
            for a, k in enumerate(keys):
                _, r, cols = shapes[a].shape
                rows = pl.ds(i * (r // n), r // n)
                for (px, py) in _other_chips(x, y):
                    q = 2 * px + py
                    remote.append((piece(ins[a], k, q, rows, cols), outs[a].at[me, rows, :], (px, py, c), outs[a].at[q, rows, :]))
            return remote

        def finish(outs):
            for k, p, o in zip(keys, parts, outs):
                self.staged[k] = o
                if i == n - 1:
                    self.mine[k] = _reduce_piece(p, o, self.me, k[0] in BIG_IN, f"reduce_{k[0]}_{k[1]}")

        inputs, aliases = parts, None
        if i > 0:
            inputs = parts + [self.staged[k] for k in keys]
            aliases = {n_keys + a: a for a in range(n_keys)}
        return _Exchange(inputs, shapes, plan, 3 * n_keys, aliases, finish)

    def exchange(self, keys):
        mine = [self.mine[k] for k in keys]

        def plan(ins, outs, place):
            x, y, c = place
            return [(ins[a], outs[a], (x, y, 1 - c), outs[a]) for a in range(len(keys))]

        def finish(outs):
            for k, o in zip(keys, outs):
                self.theirs[k] = o

        return _Exchange(mine, [_sds(a) for a in mine], plan, len(keys), None, finish)


def _all_gather_full(gat, keys):
    n = len(keys)
    arrays = [gat.slots[k] for k in keys]
    per = 7

    def body(*refs):
        ins, outs = refs[:n], refs[n:2 * n]
        send_sems, recv_sems = refs[2 * n:]
        x, y, c = _my_place()
        sibling, x_nbr, y_nbr = (x, y, 1 - c), (1 - x, y, c), (x, 1 - y, c)
        me, qx, qy, qd = 2 * x + y, 2 * (1 - x) + y, 2 * x + (1 - y), 2 * (1 - x) + (1 - y)

        def half(ref, q, core):
            rows = ref.shape[1] // 2
            return ref.at[q, pl.ds(core * rows, rows), :]

        def quarter(ref, q, core, k):
            rows = ref.shape[1] // 4
            return ref.at[q, pl.ds((2 * core + k) * rows, rows), :]

        def copy(a, k, src, dst, to):
            return pltpu.make_async_remote_copy(src_ref=src, dst_ref=dst, send_sem=send_sems.at[per * a + k],
                                                recv_sem=recv_sems.at[per * a + k], device_id=to, device_id_type=MESH)

        sent = []

        def send(a, k, part, to):
            cp = copy(a, k, part, part, to)
            cp.start()
            sent.append(cp)

        def landed(a, k, part):
            copy(a, k, part, part, sibling).wait_recv()

        for a in range(n):
            mine_in, mine_out = half(ins[a], me, c), half(outs[a], me, c)
            for k, to in ((0, x_nbr), (1, y_nbr)):
                cp = copy(a, k, mine_in, mine_out, to)
                cp.start()
                sent.append(cp)
        for a in range(n):
            landed(a, 1, half(outs[a], qy, c))
            send(a, 2, quarter(outs[a], qy, c, 0), x_nbr)
            send(a, 5, half(outs[a], qy, c), sibling)
            landed(a, 0, half(outs[a], qx, c))
            send(a, 3, quarter(outs[a], qx, c, 1), y_nbr)
            send(a, 4, half(outs[a], qx, c), sibling)
        for a in range(n):
            landed(a, 2, quarter(outs[a], qd, c, 0))
            landed(a, 3, quarter(outs[a], qd, c, 1))
            send(a, 6, half(outs[a], qd, c), sibling)
        for a in range(n):
            for k, q in ((4, qx), (5, qy), (6, qd)):
                landed(a, k, half(outs[a], q, 1 - c))
        for cp in sent:
            cp.wait_send()

    outs = pl.pallas_call(
        body, name="all_gather_first_weights", in_specs=[HBM_SPEC] * n, out_specs=[HBM_SPEC] * n,
        out_shape=[_sds(a) for a in arrays], input_output_aliases={a: a for a in range(n)},
        scratch_shapes=[pltpu.SemaphoreType.DMA((per * n,)), pltpu.SemaphoreType.DMA((per * n,))])(*arrays)
    for k, o in zip(keys, outs):
        gat.slots[k] = o


def _small_all_gather(buf, done):
    state = {}

    def index(x, y, c):
        return 4 * x + 2 * y + c

    def plan_ici(ins, outs, place):
        x, y, c = place
        return [(ins[0], outs[0].at[index(x, y, c)], (px, py, c), outs[0].at[index(px, py, c)]) for (px, py) in _other_chips(x, y)]

    def local(ins, outs, place):
        return [(ins[0], outs[0].at[index(*place)])]

    def plan_d2d(ins, outs, place):
        x, y, c = place
        return [(ins[0].at[index(px, py, c)], outs[0].at[index(px, py, c)], (x, y, 1 - c), outs[0].at[index(px, py, 1 - c)])
                for (px, py) in [(x, y)] + _other_chips(x, y)]

    def second():
        return _Exchange([state["blocks"]], [_sds(state["blocks"])], plan_d2d, N_CHIPS, {0: 0}, lambda outs: done(outs[0]))

    first = _Exchange([buf], [jax.ShapeDtypeStruct((2 * N_CHIPS,) + buf.shape, buf.dtype)], plan_ici, 3, None,
                      lambda outs: state.update(blocks=outs[0]), local, 1)
    return first, second


WEIGHT_NAMES = ("ffn1_norm", "ffn1_w_in", "ffn1_w_out", "mix_norm", "ffn2_norm", "ffn2_w_in", "ffn2_w_out", "ab_w_in", "pool_w", "pool_b",
                "pool_scale", "conv_w", "conv_b", "conv_ln_g", "conv_ln_b", "ab_w_out", "sgu_w_in", "sgu_ln_g", "sgu_ln_b", "sgu_w", "sgu_b",
                "sgu_w_out", "final_norm")
SMALL = tuple(n for n in WEIGHT_NAMES if n not in BIG)
SHARDED_SMALL = ("conv_w", "sgu_ln_g", "sgu_ln_b")
PACK_ROWS = 64
PACK = ("pack", 0)


def _pair(prefix, layer):
    return [(prefix + "_w_in", layer), (prefix + "_w_out", layer)]


def kernel(x, ffn1_norm, ffn1_w_in, ffn1_w_out, mix_norm, ffn2_norm, ffn2_w_in, ffn2_w_out, ab_w_in, pool_w, pool_b, pool_scale, conv_w, conv_b, conv_ln_g, conv_ln_b, ab_w_out, sgu_w_in, sgu_ln_g, sgu_ln_b, sgu_w, sgu_b, sgu_w_out, final_norm, loss_target, m_ffn1_norm, m_ffn1_w_in, m_ffn1_w_out, m_mix_norm, m_ffn2_norm, m_ffn2_w_in, m_ffn2_w_out, m_ab_w_in, m_pool_w, m_pool_b, m_pool_scale, m_conv_w, m_conv_b, m_conv_ln_g, m_conv_ln_b, m_ab_w_out, m_sgu_w_in, m_sgu_ln_g, m_sgu_ln_b, m_sgu_w, m_sgu_b, m_sgu_w_out, m_final_norm, v_ffn1_norm, v_ffn1_w_in, v_ffn1_w_out, v_mix_norm, v_ffn2_norm, v_ffn2_w_in, v_ffn2_w_out, v_ab_w_in, v_pool_w, v_pool_b, v_pool_scale, v_conv_w, v_conv_b, v_conv_ln_g, v_conv_ln_b, v_ab_w_out, v_sgu_w_in, v_sgu_ln_g, v_sgu_ln_b, v_sgu_w, v_sgu_b, v_sgu_w_out, v_final_norm):
    given = dict(locals())
    w = {n: given[n] for n in WEIGHT_NAMES}
    chip = 2 * lax.axis_index("x") + lax.axis_index("y")
    me = chip.astype(jnp.int32).reshape(1)
    core = lax.axis_index("c").astype(jnp.int32).reshape(1)
    row = lambda v: v.reshape(1, -1)
    xin, tgt = x[0], loss_target[0]

    pack = jnp.concatenate([
        w["conv_w"][0], jnp.zeros((1, 128), F32), w["sgu_ln_g"].reshape(2, 128), w["sgu_ln_b"].reshape(2, 128),
        jnp.zeros((PACK_ROWS - 36, 128), F32)], axis=0)
    slots = {PACK: lax.dynamic_update_slice(jnp.zeros((N_CHIPS, PACK_ROWS, 128), F32), pack[None], (me[0], 0, 0))}
    for n in BIG:
        for layer in range(w[n].shape[0]):
            slots[(n, layer)] = _cast_into_slot(w[n], layer, me, f"cast_{n}_{layer}")
    gat = _Gatherer(slots)
    _all_gather_full(gat, _pair("ffn1", 0) + [PACK])
    gp = gat.slots[PACK]
    conv_w_full = jnp.transpose(gp[:, 0:CONV_WIDTH], (1, 0, 2)).reshape(CONV_WIDTH, N_CHIPS * 128)
    sgu_ln_g_full = gp[:, 32:34].reshape(1, -1)
    sgu_ln_b_full = gp[:, 34:36].reshape(1, -1)
    gw = lambda n, layer: gat.slots[(n, layer)]

    st = [dict(), dict()]
    st[0]["xa"] = xin
    later = _pair("sgu", 0) + _pair("ffn2", 1)
    cur, st[0]["h1"], st[0]["xn1"] = _ffn_fwd(xin, row(w["ffn1_norm"][0]), gw("ffn1_w_in", 0), gw("ffn1_w_out", 0), "ffn1_fwd_0",
                                              comms=[gat.direct(_pair("ab", 0)), gat.ici(_pair("ffn2", 0))])
    st[0]["xb"] = cur
    st[0]["h0"], st[0]["xnm"] = _norm_matmul(cur, row(w["mix_norm"][0]), gw("ab_w_in", 0), "mix0_proj_in",
                                             comms=[gat.d2d(_pair("ffn2", 0)), gat.ici([("ffn1_w_out", 1)])])
    pool_args = (w["pool_w"][0], row(w["pool_b"][0]), row(w["pool_scale"][0]), conv_w_full, row(w["conv_b"][0]), row(w["conv_ln_g"][0]),
                 row(w["conv_ln_b"][0]), gw("ab_w_out", 0))
    cur, st[0]["ycat"], st[0]["yconv"] = _mix0_fwd(cur, st[0]["h0"], *pool_args, "mix0_fwd", comms=[gat.ici([("ffn1_w_in", 1)])])
    st[0]["xc"] = cur
    cur, st[0]["h2"], st[0]["xn2"] = _ffn_fwd(cur, row(w["ffn2_norm"][0]), gw("ffn2_w_in", 0), gw("ffn2_w_out", 0), "ffn2_fwd_0",
                                              comms=[gat.d2d(_pair("ffn1", 1)), gat.ici(later)])
    st[1]["xa"] = cur
    cur, st[1]["h1"], st[1]["xn1"] = _ffn_fwd(cur, row(w["ffn1_norm"][1]), gw("ffn1_w_in", 1), gw("ffn1_w_out", 1), "ffn1_fwd_1",
                                              comms=[gat.d2d(later)])
    st[1]["xb"] = cur
    st[1]["pre"], st[1]["xnm"] = _norm_matmul(cur, row(w["mix_norm"][1]), gw("sgu_w_in", 0), "sgu_proj_in")
    sgu_args = (sgu_ln_g_full, sgu_ln_b_full, w["sgu_w"][0], w["sgu_b"][0].T)
    cur, st[1]["p"] = _sgu_fwd(cur, st[1]["pre"], *sgu_args, gw("sgu_w_out", 0), "sgu_fwd")
    st[1]["xc"] = cur
    cur, st[1]["h2"], st[1]["xn2"] = _ffn_fwd(cur, row(w["ffn2_norm"][1]), gw("ffn2_w_in", 1), gw("ffn2_w_out", 1), "ffn2_fwd_1")
    dy, loss, d_final = _final_loss(cur, tgt, row(w["final_norm"]), "final_loss")

    red = _Reducer(me, core)
    small = {"final_norm": d_final.reshape(-1)}
    norm_grads = {"ffn1_norm": [None] * DEPTH, "mix_norm": [None] * DEPTH, "ffn2_norm": [None] * DEPTH}
    ga, gb, gc, gd, ge, gf = _pair("ffn2", 1), _pair("sgu", 0), _pair("ffn1", 1), _pair("ffn2", 0), _pair("ab", 0), _pair("ffn1", 0)

    def ffn_backward(prefix, layer, xs, hs, xns, dy_in, bwd_comms=(), dwin_comms=(), dwout_comms=()):
        dx, dh, act, norm_grads[prefix + "_norm"][layer] = _ffn_bwd(
            xs, dy_in, hs, row(w[prefix + "_norm"][layer]), gw(prefix + "_w_in", layer), gw(prefix + "_w_out", layer),
            f"{prefix}_bwd_{layer}", comms=bwd_comms)
        red.add((prefix + "_w_in", layer), _tn_matmul(xns, dh, 1.0, 1024, 1408, f"{prefix}_dwin_{layer}", comms=dwin_comms))
        red.add((prefix + "_w_out", layer), _tn_matmul(act, dy_in, 0.5, 1408, 1024, f"{prefix}_dwout_{layer}", comms=dwout_comms))
        return dx

    s1, s0 = st[1], st[0]
    dy = ffn_backward("ffn2", 1, s1["xc"], s1["h2"], s1["xn2"], dy)
    dy_in = dy
    dy, dpre, norm_grads["mix_norm"][1], dlg, dlb, dw, dbt = _sgu_bwd(
        s1["xb"], dy_in, s1["pre"], row(w["mix_norm"][1]), *sgu_args, gw("sgu_w_in", 0), gw("sgu_w_out", 0), "sgu_bwd", comms=[red.swap(ga)])
    red.add(("sgu_w_in", 0), _tn_matmul(s1["xnm"], dpre, 1.0, 1024, 2048, "sgu_dwin"))
    red.add(("sgu_w_out", 0), _tn_matmul(s1["p"], dy_in, 1.0, 1024, 1024, "sgu_dwout"))
    small.update(sgu_ln_g=dlg, sgu_ln_b=dlb, sgu_w=dw[None], sgu_b=dbt.T[None])
    dy = ffn_backward("ffn1", 1, s1["xa"], s1["h1"], s1["xn1"], dy, bwd_comms=[lambda: red.scatter(ga), lambda: red.swap(gb)],
                      dwin_comms=[lambda: red.scatter(gb), lambda: red.exchange(ga)])
    dy = ffn_backward("ffn2", 0, s0["xc"], s0["h2"], s0["xn2"], dy, bwd_comms=[lambda: red.swap(gc), lambda: red.exchange(gb)],
                      dwin_comms=[lambda: red.scatter(gc)])
    dy_in = dy
    dconv, dpc, dpw, rows = _mix0_bwd_a(dy_in, s0["h0"], s0["yconv"], *pool_args, "mix0_bwd_a")
    dy, dh0, norm_grads["mix_norm"][0] = _mix0_bwd_b(s0["xb"], dy_in, s0["h0"], dconv, dpc, row(w["mix_norm"][0]), conv_w_full,
                                                      gw("ab_w_in", 0), "mix0_bwd_b")
    red.add(("ab_w_in", 0), _tn_matmul(s0["xnm"], dh0, 1.0, 1024, 1536, "ab_dwin", comms=[red.swap(gd), red.exchange(gc)]))
    red.add(("ab_w_out", 0), _tn_matmul(s0["ycat"], dy_in, 1.0, 1024, 1024, "ab_dwout"))
    small.update(pool_w=dpw[None], conv_w=rows[None, 0:CONV_WIDTH], conv_b=rows[32:33], conv_ln_g=rows[33:34], conv_ln_b=rows[34:35],
                 pool_scale=rows[35:36], pool_b=rows[36:37].reshape(1, len(POOL_WINDOWS), POOL_GC))

    small_sum = {}

    def small_ready():
        for k, v in norm_grads.items():
            small[k] = jnp.concatenate(v, axis=0)
        flat = [small[n].reshape(-1, 128) for n in SMALL]
        rows = sum(f.shape[0] for f in flat)
        loss_block = jnp.pad(loss, ((0, 8 + (-rows) % 8 - 1), (0, 127)))
        buf = jnp.concatenate(flat + [loss_block], axis=0)

        def done(gathered):
            total, at = _sum_leading(gathered, "reduce_small"), 0
            for n, f in zip(SMALL, flat):
                small_sum[n] = total[at:at + f.shape[0]].reshape(small[n].shape)
                at += f.shape[0]
            small_sum["loss"] = total[at:at + 1, 0:1]

        return _small_all_gather(buf, done)

    dx, dh, act, norm_grads["ffn1_norm"][0] = _ffn_bwd(
        s0["xa"], dy, s0["h1"], row(w["ffn1_norm"][0]), gw("ffn1_w_in", 0), gw("ffn1_w_out", 0), "ffn1_bwd_0")
    small_first, small_second = small_ready()
    red.add(("ffn1_w_in", 0), _tn_matmul(s0["xn1"], dh, 1.0, 1024, 1408, "ffn1_dwin_0",
                                         comms=[red.scatter(gd), red.swap(ge), small_first]))
    red.add(("ffn1_w_out", 0), _tn_matmul(act, dy, 0.5, 1408, 1024, "ffn1_dwout_0",
                                          comms=[red.scatter(ge), red.exchange(gd), small_second]))
    grad_x = dx

    big_out = {}

    def adamw_big(n, layer, comms=()):
        big_out[n] = _adamw_sharded(w[n], red.mine[(n, layer)], red.theirs[(n, layer)], given["m_" + n], given["v_" + n], core, layer,
                                    big_out.get(n), f"adamw_{n}_{layer}", comms=comms)

    _exchange_alone("swap_last_grads", [red.swap(gf), red.exchange(ge)])
    _exchange_alone("scatter_last_grads", [red.scatter(gf)])
    _exchange_alone("exchange_last_grads", [red.exchange(gf)])
    for n in BIG:
        for layer in reversed(range(w[n].shape[0])):
            adamw_big(n, layer)

    loss = small_sum["loss"][0, 0]
    grads, delta, new_m, new_v = {}, {}, {}, {}
    for n in WEIGHT_NAMES:
        mom, var = given["m_" + n], given["v_" + n]
        if n in BIG:
            grads[n], delta[n], new_m[n], new_v[n] = big_out[n]
            continue
        g = small_sum[n]
        if n in SHARDED_SMALL:
            width = w[n].shape[-1]
            g = lax.dynamic_slice_in_dim(g, chip * width, width, axis=g.ndim - 1)
        grads[n] = g
        delta[n], new_m[n], new_v[n] = _adamw(w[n], g, mom, var, f"adamw_{n}")
    return (loss, grad_x[None], *[grads[n] for n in WEIGHT_NAMES], *[delta[n] for n in WEIGHT_NAMES],
            *[new_m[n] for n in WEIGHT_NAMES], *[new_v[n] for n in WEIGHT_NAMES])
```

```python
import jax
import jax.numpy as jnp
from jax import lax
from jax.experimental import pallas as pl
from jax.experimental.pallas import tpu as pltpu

F32, BF16 = jnp.float32, jnp.bfloat16
EPS = 1e-6
N_CHIPS = 4
POOL_WINDOWS = (2, 4, 8, 16)
POOL_GC = 128
POOL_CH = 512
CONV_CH = 512
CONV_WIDTH = 31
HALO = 32
SGU_HEADS = 8
CHUNK = 128
DEPTH = 2
ADAM_LR, ADAM_B1, ADAM_B2, ADAM_EPS, ADAM_WD, ADAM_STEP = 0.001, 0.9, 0.999, 1e-08, 0.01, 10
VMEM_LIMIT_BYTES = 60 * 1024 * 1024
MESH_AXES = ("x", "y", "c")
MESH = pl.DeviceIdType.MESH
HBM_SPEC = pl.BlockSpec(memory_space=pltpu.HBM)


def _sds(a):
    return jax.ShapeDtypeStruct(a.shape, a.dtype)


def _cparams_nd(n):
    return pltpu.CompilerParams(dimension_semantics=("arbitrary",) * n, vmem_limit_bytes=VMEM_LIMIT_BYTES)


def _cparams():
    return _cparams_nd(1)


def _dot(a, b):
    return jnp.dot(a, b, preferred_element_type=F32)


def _dot_nt(a, b):
    return lax.dot_general(a, b, (((1,), (1,)), ((), ())), preferred_element_type=F32)


def _dot_tn(a, b):
    return lax.dot_general(a, b, (((0,), (0,)), ((), ())), preferred_element_type=F32)


def _rms_fwd(x):
    r = lax.rsqrt(jnp.mean(x * x, axis=-1, keepdims=True) + EPS)
    return x * r, r


def _rms_bwd(dxn, xh, r, g):
    dxh = dxn * g
    return r * (dxh - xh * jnp.mean(dxh * xh, axis=-1, keepdims=True))


def _ln_fwd(y):
    mu = jnp.mean(y, axis=-1, keepdims=True)
    yc = y - mu
    rs = lax.rsqrt(jnp.mean(yc * yc, axis=-1, keepdims=True) + EPS)
    return yc * rs, rs


def _ln_bwd(dyhat, yhat, rs):
    return rs * (dyhat - jnp.mean(dyhat, axis=-1, keepdims=True) - yhat * jnp.mean(dyhat * yhat, axis=-1, keepdims=True))


def _sigmoid(x):
    return 0.5 * jnp.tanh(0.5 * x) + 0.5


def _const_spec(shape):
    n = len(shape)
    return pl.BlockSpec(shape, lambda i: (0,) * n)


def _row_spec(tm, cols):
    return pl.BlockSpec((tm, cols), lambda i: (i, 0))


def _my_place():
    return lax.axis_index("x"), lax.axis_index("y"), lax.axis_index("c")


def _other_chips(x, y):
    return [(1 - x, y), (x, 1 - y), (1 - x, 1 - y)]


class _Exchange:
    def __init__(self, inputs, out_shapes, plan, count, aliases=None, finish=None, local=None, n_local=0):
        self.inputs, self.out_shapes, self.plan, self.count = list(inputs), list(out_shapes), plan, count
        self.aliases, self.finish, self.local, self.n_local = dict(aliases or {}), finish, local, n_local


def _call(body, *, name, grid, in_specs, out_specs, out_shape, args, scratch_shapes=(), comms=(), scalar=None, aliases=None):
    comms = [cm if isinstance(cm, _Exchange) else cm() for cm in comms]
    in_specs, out_specs, out_shape, scratch_shapes = list(in_specs), list(out_specs), list(out_shape), list(scratch_shapes)
    n_in, n_out, n_scr = len(in_specs), len(out_specs), len(scratch_shapes)
    c_in = [a for cm in comms for a in cm.inputs]
    c_out = [s for cm in comms for s in cm.out_shapes]
    n_remote = sum(cm.count for cm in comms)
    n_local = sum(cm.n_local for cm in comms)
    n_scalar = 0 if scalar is None else 1
    all_aliases = {n_scalar + i: o for i, o in (aliases or {}).items()}
    at_in, at_out = n_scalar + n_in, n_out
    for cm in comms:
        for i, o in cm.aliases.items():
            all_aliases[at_in + i] = at_out + o
        at_in += len(cm.inputs)
        at_out += len(cm.out_shapes)

    def wrapped(*all_refs):
        scalar_ref, refs = all_refs[:n_scalar], all_refs[n_scalar:]
        ins, ci = refs[:n_in], refs[n_in:n_in + len(c_in)]
        at = n_in + len(c_in)
        outs, co = refs[at:at + n_out], refs[at + n_out:at + n_out + len(c_out)]
        at += n_out + len(c_out)
        scr = refs[at:at + n_scr]

        def run_body():
            body(*scalar_ref, *ins, *outs, *scr)

        if not comms:
            run_body()
            return
        send_sems, recv_sems, local_sems = refs[at + n_scr:]
        place = _my_place()
        sends, arrivals, locals_ = [], [], []
        i0 = o0 = 0
        for cm in comms:
            cm_in, cm_out = ci[i0:i0 + len(cm.inputs)], co[o0:o0 + len(cm.out_shapes)]
            i0 += len(cm.inputs)
            o0 += len(cm.out_shapes)
            for src, dst, dev, incoming in cm.plan(cm_in, cm_out, place):
                k = len(sends)
                sends.append(pltpu.make_async_remote_copy(src_ref=src, dst_ref=dst, send_sem=send_sems.at[k], recv_sem=recv_sems.at[k],
                                                          device_id=dev, device_id_type=MESH))
                arrivals.append(pltpu.make_async_remote_copy(src_ref=src, dst_ref=incoming, send_sem=send_sems.at[k],
                                                             recv_sem=recv_sems.at[k], device_id=dev, device_id_type=MESH))
            if cm.local is not None:
                for src, dst in cm.local(cm_in, cm_out, place):
                    locals_.append(pltpu.make_async_copy(src, dst, local_sems.at[len(locals_)]))

        def start():
            for cp in locals_ + sends:
                cp.start()

        def finish():
            for cp in arrivals:
                cp.wait_recv()
            for cp in sends:
                cp.wait_send()
            for cp in locals_:
                cp.wait()

        if not grid:
            start()
            run_body()
            finish()
            return
        ids = [pl.program_id(a) for a in range(len(grid))]
        first, last = ids[0] == 0, ids[0] == grid[0] - 1
        for a in range(1, len(grid)):
            first = jnp.logical_and(first, ids[a] == 0)
            last = jnp.logical_and(last, ids[a] == grid[a] - 1)
        pl.when(first)(start)
        run_body()
        pl.when(last)(finish)

    sems = []
    if comms:
        sems = [pltpu.SemaphoreType.DMA((max(n_remote, 1),)), pltpu.SemaphoreType.DMA((max(n_remote, 1),)),
                pltpu.SemaphoreType.DMA((max(n_local, 1),))]
    all_in, all_out = in_specs + [HBM_SPEC] * len(c_in), out_specs + [HBM_SPEC] * len(c_out)
    if scalar is None:
        kwargs = dict(grid=grid, compiler_params=_cparams_nd(len(grid))) if grid else {}
        res = pl.pallas_call(
            wrapped, name=name, in_specs=all_in, out_specs=all_out, out_shape=out_shape + c_out, scratch_shapes=scratch_shapes + sems,
            input_output_aliases=all_aliases, **kwargs)(*args, *c_in)
    else:
        spec = pltpu.PrefetchScalarGridSpec(num_scalar_prefetch=1, grid=grid, in_specs=all_in, out_specs=all_out,
                                            scratch_shapes=scratch_shapes + sems)
        res = pl.pallas_call(
            wrapped, name=name, grid_spec=spec, out_shape=out_shape + c_out, input_output_aliases=all_aliases,
            compiler_params=_cparams_nd(len(grid)))(scalar, *args, *c_in)
    at = n_out
    for cm in comms:
        got = res[at:at + len(cm.out_shapes)]
        at += len(cm.out_shapes)
        if cm.finish is not None:
            cm.finish(got)
    return list(res[:n_out])


def _exchange_alone(name, comms):
    _call(lambda: None, name=name, grid=(), in_specs=[], out_specs=[], out_shape=[], args=[], comms=comms)


def _in_weight_copies(w_hbm, w_v, sem, base=0):
    return [pltpu.make_async_copy(w_hbm.at[q], w_v.at[q], sem.at[base + q]) for q in range(N_CHIPS)]


def _out_weight_copies(w_hbm, w_v, sem, base=0):
    rows = w_hbm.shape[1]
    return [pltpu.make_async_copy(w_hbm.at[q], w_v.at[pl.ds(q * rows, rows)], sem.at[base + q]) for q in range(N_CHIPS)]


def _load_at_first_step(copies):
    @pl.when(pl.program_id(0) == 0)
    def _():
        for cp in copies:
            cp.start()
        for cp in copies:
            cp.wait()


def _ffn_fwd(x, g, win_g, wout_g, name, comms=()):
    t, d = x.shape
    c = win_g.shape[-1]
    ff = 2 * c
    tm = min(512, t)

    def body(x_ref, g_ref, win_hbm, wout_hbm, xo_ref, h_ref, xn_ref, win_v, wout_v, sem):
        _load_at_first_step(_in_weight_copies(win_hbm, win_v, sem) + _out_weight_copies(wout_hbm, wout_v, sem, N_CHIPS))
        xv = x_ref[...]
        xh, _ = _rms_fwd(xv)
        xn = (xh * g_ref[...]).astype(BF16)
        xn_ref[...] = xn
        acc = jnp.zeros((tm, d), F32)
        for j in range(2):
            gate = _dot(xn, win_v[j])
            up = _dot(xn, win_v[j + 2])
            h_ref[:, j * c:(j + 1) * c] = gate.astype(BF16)
            h_ref[:, ff + j * c:ff + (j + 1) * c] = up.astype(BF16)
            act = (gate * _sigmoid(gate) * up).astype(BF16)
            acc = acc + _dot(act, wout_v[j * c:(j + 1) * c, :])
        xo_ref[...] = xv + 0.5 * acc

    return _call(
        body, name=name, grid=(t // tm,),
        in_specs=[_row_spec(tm, d), _const_spec((1, d)), HBM_SPEC, HBM_SPEC],
        out_specs=[_row_spec(tm, d), _row_spec(tm, 2 * ff), _row_spec(tm, d)],
        out_shape=[jax.ShapeDtypeStruct((t, d), F32), jax.ShapeDtypeStruct((t, 2 * ff), BF16), jax.ShapeDtypeStruct((t, d), BF16)],
        scratch_shapes=[pltpu.VMEM((N_CHIPS, d, c), BF16), pltpu.VMEM((ff, d), BF16), pltpu.SemaphoreType.DMA((2 * N_CHIPS,))],
        args=(x, g, win_g, wout_g), comms=comms)


def _ffn_bwd(x, dy, h, g, win_g, wout_g, name, comms=()):
    t, d = x.shape
    c = win_g.shape[-1]
    ff = 2 * c
    tm = min(256, t)

    def body(x_ref, dy_ref, h_ref, g_ref, win_hbm, wout_hbm, dx_ref, dh_ref, act_ref, dg_ref, win_v, wout_v, sem):
        _load_at_first_step(_in_weight_copies(win_hbm, win_v, sem) + _out_weight_copies(wout_hbm, wout_v, sem, N_CHIPS))

        @pl.when(pl.program_id(0) == 0)
        def _():
            dg_ref[...] = jnp.zeros_like(dg_ref)

        xv, dyv, gv = x_ref[...], dy_ref[...], g_ref[...]
        xh, r = _rms_fwd(xv)
        dyh = (0.5 * dyv).astype(BF16)
        dxn = jnp.zeros((tm, d), F32)
        for j in range(2):
            gate = h_ref[:, j * c:(j + 1) * c].astype(F32)
            up = h_ref[:, ff + j * c:ff + (j + 1) * c].astype(F32)
            dact = _dot_nt(dyh, wout_v[j * c:(j + 1) * c, :])
            s = _sigmoid(gate)
            sl = gate * s
            act_ref[:, j * c:(j + 1) * c] = (sl * up).astype(BF16)
            dgate = (dact * up * (s + sl * (1.0 - s))).astype(BF16)
            dup = (dact * sl).astype(BF16)
            dh_ref[:, j * c:(j + 1) * c] = dgate
            dh_ref[:, ff + j * c:ff + (j + 1) * c] = dup
            dxn = dxn + _dot_nt(dgate, win_v[j]) + _dot_nt(dup, win_v[j + 2])
        dg_ref[...] += jnp.sum(dxn * xh, axis=0, keepdims=True)
        dx_ref[...] = dyv + _rms_bwd(dxn, xh, r, gv)

    return _call(
        body, name=name, grid=(t // tm,),
        in_specs=[_row_spec(tm, d), _row_spec(tm, d), _row_spec(tm, 2 * ff), _const_spec((1, d)), HBM_SPEC, HBM_SPEC],
        out_specs=[_row_spec(tm, d), _row_spec(tm, 2 * ff), _row_spec(tm, ff), _const_spec((1, d))],
        out_shape=[jax.ShapeDtypeStruct((t, d), F32), jax.ShapeDtypeStruct((t, 2 * ff), BF16), jax.ShapeDtypeStruct((t, ff), BF16),
                   jax.ShapeDtypeStruct((1, d), F32)],
        scratch_shapes=[pltpu.VMEM((N_CHIPS, d, c), BF16), pltpu.VMEM((ff, d), BF16), pltpu.SemaphoreType.DMA((2 * N_CHIPS,))],
        args=(x, dy, h, g, win_g, wout_g), comms=comms)


def _norm_matmul(x, g, win_g, name, comms=()):
    t, d = x.shape
    c = win_g.shape[-1]
    tm = min(512, t)

    def body(x_ref, g_ref, win_hbm, o_ref, xn_ref, win_v, sem):
        _load_at_first_step(_in_weight_copies(win_hbm, win_v, sem))
        xh, _ = _rms_fwd(x_ref[...])
        xn = (xh * g_ref[...]).astype(BF16)
        xn_ref[...] = xn
        for q in range(N_CHIPS):
            o_ref[:, q * c:(q + 1) * c] = _dot(xn, win_v[q])

    return _call(
        body, name=name, grid=(t // tm,),
        in_specs=[_row_spec(tm, d), _const_spec((1, d)), HBM_SPEC],
        out_specs=[_row_spec(tm, N_CHIPS * c), _row_spec(tm, d)],
        out_shape=[jax.ShapeDtypeStruct((t, N_CHIPS * c), F32), jax.ShapeDtypeStruct((t, d), BF16)],
        scratch_shapes=[pltpu.VMEM((N_CHIPS, d, c), BF16), pltpu.SemaphoreType.DMA((N_CHIPS,))],
        args=(x, g, win_g), comms=comms)


def _proj_in_bwd_tail(dh, win_v, c):
    dxn = _dot_nt(dh[:, 0:c], win_v[0])
    for q in range(1, N_CHIPS):
        dxn = dxn + _dot_nt(dh[:, q * c:(q + 1) * c], win_v[q])
    return dxn


def _prev_halo_spec(tm, cols):
    return pl.BlockSpec((HALO, cols), lambda i: (jnp.maximum(i * (tm // HALO) - 1, 0), 0))


def _next_halo_spec(tm, cols, t):
    last = t // HALO - 1
    return pl.BlockSpec((HALO, cols), lambda i: (jnp.minimum((i + 1) * (tm // HALO), last), 0))


def _shift_down(w, k):
    return w if k == 0 else pltpu.roll(w, k, 0)


def _shift_up(w, k):
    return w if k == 0 else pltpu.roll(w, w.shape[0] - k, 0)


def _pool_counts(i, tm):
    pos = (i * tm + lax.broadcasted_iota(jnp.int32, (tm, POOL_CH), 0) + 1).astype(F32)
    lane = lax.broadcasted_iota(jnp.int32, (tm, POOL_CH), 1)
    win = jnp.where(lane < POOL_GC, 2.0, jnp.where(lane < 2 * POOL_GC, 4.0, jnp.where(lane < 3 * POOL_GC, 8.0, 16.0)))
    return jnp.minimum(pos, win)


def _group_select(parts):
    return jnp.concatenate([p[:, k * POOL_GC:(k + 1) * POOL_GC] for k, p in enumerate(parts)], axis=1)


def _mix0_recompute(i, tm, h_cur, h_prev, conv_w, conv_b, y=None):
    prev = jnp.where(i > 0, h_prev, 0.0)
    win = jnp.concatenate([prev, h_cur], axis=0)
    u_w = win[:, 0:POOL_CH]
    a_w = win[:, POOL_CH:POOL_CH + CONV_CH]
    gt_w = win[:, POOL_CH + CONV_CH:]
    g_w = a_w * _sigmoid(gt_w)
    if y is None:
        y = jnp.zeros((tm, CONV_CH), F32)
        for k in range(CONV_WIDTH):
            y = y + conv_w[k:k + 1, :] * _shift_down(g_w, CONV_WIDTH - 1 - k)[HALO:, :]
        y = y + conv_b
    s2 = u_w + _shift_down(u_w, 1)
    s4 = s2 + _shift_down(s2, 2)
    s8 = s4 + _shift_down(s4, 4)
    s16 = s8 + _shift_down(s8, 8)
    sums = _group_select([s2[HALO:], s4[HALO:], s8[HALO:], s16[HALO:]])
    cnt = _pool_counts(i, tm)
    pooled = sums / cnt - h_cur[:, 0:POOL_CH]
    return g_w, y, pooled, cnt


def _pool_linear(pooled, pw_ref, pb):
    return jnp.concatenate(
        [_dot(pooled[:, k * POOL_GC:(k + 1) * POOL_GC].astype(BF16), pw_ref[k].astype(BF16)) for k in range(len(POOL_WINDOWS))], axis=1) + pb


def _mix0_fwd(x, h0, pool_w, pool_b, pool_scale, conv_w, conv_b, ln_g, ln_b, wout_g, name, comms=()):
    t, d = x.shape
    tm = min(256, t)
    hc = h0.shape[1]

    def body(x_ref, h_ref, hp_ref, pw_ref, pb_ref, ps_ref, cw_ref, cb_ref, lg_ref, lb_ref, wout_hbm, xo_ref, ycat_ref, y_ref, wout_v, sem):
        _load_at_first_step(_out_weight_copies(wout_hbm, wout_v, sem))
        i = pl.program_id(0)
        _, y, pooled, _ = _mix0_recompute(i, tm, h_ref[...], hp_ref[...], cw_ref[...], cb_ref[...])
        y_ref[...] = y
        yhat, _ = _ln_fwd(y)
        yn = yhat * lg_ref[...] + lb_ref[...]
        yb = yn * _sigmoid(yn)
        ya = _pool_linear(pooled, pw_ref, pb_ref[...]) * ps_ref[...]
        ycat = jnp.concatenate([ya, yb], axis=1).astype(BF16)
        ycat_ref[...] = ycat
        xo_ref[...] = x_ref[...] + _dot(ycat, wout_v[...])

    return _call(
        body, name=name, grid=(t // tm,),
        in_specs=[_row_spec(tm, d), _row_spec(tm, hc), _prev_halo_spec(tm, hc), _const_spec(pool_w.shape), _const_spec((1, POOL_CH)),
                  _const_spec((1, POOL_CH)), _const_spec(conv_w.shape), _const_spec((1, CONV_CH)), _const_spec((1, CONV_CH)),
                  _const_spec((1, CONV_CH)), HBM_SPEC],
        out_specs=[_row_spec(tm, d), _row_spec(tm, d), _row_spec(tm, CONV_CH)],
        out_shape=[jax.ShapeDtypeStruct((t, d), F32), jax.ShapeDtypeStruct((t, d), BF16), jax.ShapeDtypeStruct((t, CONV_CH), F32)],
        scratch_shapes=[pltpu.VMEM((d, d), BF16), pltpu.SemaphoreType.DMA((N_CHIPS,))],
        args=(x, h0, h0, pool_w, pool_b, pool_scale, conv_w, conv_b, ln_g, ln_b, wout_g), comms=comms)


def _mix0_bwd_a(dy, h0, y_conv, pool_w, pool_b, pool_scale, conv_w, conv_b, ln_g, ln_b, wout_g, name):
    t, d = dy.shape
    tm = min(256, t)
    hc = h0.shape[1]
    n_small = 40

    def body(dy_ref, h_ref, hp_ref, y_ref, pw_ref, pb_ref, ps_ref, cw_ref, cb_ref, lg_ref, lb_ref, wout_hbm,
             dconv_ref, dpc_ref, dpw_ref, small_ref, wout_v, sem):
        _load_at_first_step(_out_weight_copies(wout_hbm, wout_v, sem))
        i = pl.program_id(0)

        @pl.when(i == 0)
        def _():
            dpw_ref[...] = jnp.zeros_like(dpw_ref)
            small_ref[...] = jnp.zeros_like(small_ref)

        g_w, y, pooled, cnt = _mix0_recompute(i, tm, h_ref[...], hp_ref[...], cw_ref[...], cb_ref[...], y_ref[...])
        yhat, rs = _ln_fwd(y)
        lg = lg_ref[...]
        yn = yhat * lg + lb_ref[...]
        mixed = _pool_linear(pooled, pw_ref, pb_ref[...])
        dycat = _dot_nt(dy_ref[...].astype(BF16), wout_v[...])
        dya, dyb = dycat[:, 0:POOL_CH], dycat[:, POOL_CH:]
        sg = _sigmoid(yn)
        dyn = dyb * (sg * (1.0 + yn * (1.0 - sg)))
        dyc = _ln_bwd(dyn * lg, yhat, rs)
        dconv_ref[...] = dyc

        def add_row(k, value):
            small_ref[k:k + 1, :] += jnp.sum(value, axis=0, keepdims=True)

        for k in range(CONV_WIDTH):
            add_row(k, dyc * _shift_down(g_w, CONV_WIDTH - 1 - k)[HALO:, :])
        add_row(32, dyc)
        add_row(33, dyn * yhat)
        add_row(34, dyn)
        scale = ps_ref[...]
        dmixed = dya * scale
        add_row(35, dya * mixed)
        add_row(36, dmixed)
        dmb = dmixed.astype(BF16)
        dpooled = []
        for k in range(len(POOL_WINDOWS)):
            sl = slice(k * POOL_GC, (k + 1) * POOL_GC)
            dpw_ref[k] += _dot_tn(pooled[:, sl].astype(BF16), dmb[:, sl])
            dpooled.append(_dot_nt(dmb[:, sl], pw_ref[k].astype(BF16)))
        dpc_ref[...] = jnp.concatenate(dpooled, axis=1) / cnt

    return _call(
        body, name=name, grid=(t // tm,),
        in_specs=[_row_spec(tm, d), _row_spec(tm, hc), _prev_halo_spec(tm, hc), _row_spec(tm, CONV_CH), _const_spec(pool_w.shape),
                  _const_spec((1, POOL_CH)), _const_spec((1, POOL_CH)), _const_spec(conv_w.shape), _const_spec((1, CONV_CH)),
                  _const_spec((1, CONV_CH)), _const_spec((1, CONV_CH)), HBM_SPEC],
        out_specs=[_row_spec(tm, CONV_CH), _row_spec(tm, POOL_CH), _const_spec(pool_w.shape), _const_spec((n_small, CONV_CH))],
        out_shape=[jax.ShapeDtypeStruct((t, CONV_CH), F32), jax.ShapeDtypeStruct((t, POOL_CH), F32),
                   jax.ShapeDtypeStruct(pool_w.shape, F32), jax.ShapeDtypeStruct((n_small, CONV_CH), F32)],
        scratch_shapes=[pltpu.VMEM((d, d), BF16), pltpu.SemaphoreType.DMA((N_CHIPS,))],
        args=(dy, h0, h0, y_conv, pool_w, pool_b, pool_scale, conv_w, conv_b, ln_g, ln_b, wout_g))


def _mix0_bwd_b(x, dy, h0, dconv, dpc, g, conv_w, win_g, name):
    t, d = x.shape
    tm = min(256, t)
    hc = h0.shape[1]
    c = win_g.shape[-1]
    n_tiles = t // tm

    def body(x_ref, dy_ref, h_ref, dc_ref, dcn_ref, dp_ref, dpn_ref, g_ref, cw_ref, win_hbm, dx_ref, dh_ref, dg_ref, win_v, sem):
        _load_at_first_step(_in_weight_copies(win_hbm, win_v, sem))
        i = pl.program_id(0)

        @pl.when(i == 0)
        def _():
            dg_ref[...] = jnp.zeros_like(dg_ref)

        not_last = i < n_tiles - 1
        dc_w = jnp.concatenate([dc_ref[...], jnp.where(not_last, dcn_ref[...], 0.0)], axis=0)
        dp_w = jnp.concatenate([dp_ref[...], jnp.where(not_last, dpn_ref[...], 0.0)], axis=0)
        cw = cw_ref[...]
        dg = jnp.zeros((tm, CONV_CH), F32)
        for k in range(CONV_WIDTH):
            dg = dg + cw[k:k + 1, :] * _shift_up(dc_w, CONV_WIDTH - 1 - k)[0:tm, :]
        a2 = dp_w + _shift_up(dp_w, 1)
        a4 = a2 + _shift_up(a2, 2)
        a8 = a4 + _shift_up(a4, 4)
        a16 = a8 + _shift_up(a8, 8)
        back = _group_select([a2[0:tm], a4[0:tm], a8[0:tm], a16[0:tm]])
        du = back - dp_ref[...] * _pool_counts(i, tm)
        hv = h_ref[...]
        a = hv[:, POOL_CH:POOL_CH + CONV_CH]
        sig = _sigmoid(hv[:, POOL_CH + CONV_CH:])
        dh = jnp.concatenate([du, dg * sig, dg * a * sig * (1.0 - sig)], axis=1).astype(BF16)
        dh_ref[...] = dh
        dxn = _proj_in_bwd_tail(dh, win_v, c)
        xh, r = _rms_fwd(x_ref[...])
        dg_ref[...] += jnp.sum(dxn * xh, axis=0, keepdims=True)
        dx_ref[...] = dy_ref[...] + _rms_bwd(dxn, xh, r, g_ref[...])

    return _call(
        body, name=name, grid=(n_tiles,),
        in_specs=[_row_spec(tm, d), _row_spec(tm, d), _row_spec(tm, hc), _row_spec(tm, CONV_CH), _next_halo_spec(tm, CONV_CH, t),
                  _row_spec(tm, POOL_CH), _next_halo_spec(tm, POOL_CH, t), _const_spec((1, d)), _const_spec(conv_w.shape), HBM_SPEC],
        out_specs=[_row_spec(tm, d), _row_spec(tm, hc), _const_spec((1, d))],
        out_shape=[jax.ShapeDtypeStruct((t, d), F32), jax.ShapeDtypeStruct((t, hc), BF16), jax.ShapeDtypeStruct((1, d), F32)],
        scratch_shapes=[pltpu.VMEM((N_CHIPS, d, c), BF16), pltpu.SemaphoreType.DMA((N_CHIPS,))],
        args=(x, dy, h0, dconv, dconv, dpc, dpc, g, conv_w, win_g))


SQRT_HALF = 0.7071067811865476
INV_SQRT_2PI = 0.3989422804014327


def _causal_mask():
    return (lax.broadcasted_iota(jnp.int32, (CHUNK, CHUNK), 1) <= lax.broadcasted_iota(jnp.int32, (CHUNK, CHUNK), 0)).astype(F32)


def _sgu_recompute(pre, lg, lb):
    half = pre.shape[1] // 2
    phi = 0.5 * (1.0 + lax.erf(pre * SQRT_HALF))
    z = pre * phi
    u, v = z[:, 0:half], z[:, half:]
    vhat, rs = _ln_fwd(v)
    return u, vhat, rs, vhat * lg + lb, phi


def _sgu_spatial(vln, w_ref, bt, tm):
    mask = _causal_mask()
    wm = [(w_ref[hd] * mask).astype(BF16) for hd in range(SGU_HEADS)]
    vb = vln.astype(BF16)
    rows = []
    for ch in range(tm // CHUNK):
        blocks = [_dot(wm[hd], vb[ch * CHUNK:(ch + 1) * CHUNK, hd * CHUNK:(hd + 1) * CHUNK]) + bt[:, hd:hd + 1] for hd in range(SGU_HEADS)]
        rows.append(jnp.concatenate(blocks, axis=1))
    return jnp.concatenate(rows, axis=0), wm


def _sgu_fwd(x, pre, ln_g, ln_b, w, bt, wout_g, name):
    t, d = x.shape
    tm = min(256, t)
    pc = pre.shape[1]

    def body(x_ref, pre_ref, lg_ref, lb_ref, w_ref, bt_ref, wout_hbm, xo_ref, p_ref, wout_v, sem):
        _load_at_first_step(_out_weight_copies(wout_hbm, wout_v, sem))
        u, _, _, vln, _ = _sgu_recompute(pre_ref[...], lg_ref[...], lb_ref[...])
        vo, _ = _sgu_spatial(vln, w_ref, bt_ref[...], tm)
        p = (u * vo).astype(BF16)
        p_ref[...] = p
        xo_ref[...] = x_ref[...] + _dot(p, wout_v[...])

    return _call(
        body, name=name, grid=(t // tm,),
        in_specs=[_row_spec(tm, d), _row_spec(tm, pc), _const_spec((1, d)), _const_spec((1, d)), _const_spec(w.shape),
                  _const_spec(bt.shape), HBM_SPEC],
        out_specs=[_row_spec(tm, d), _row_spec(tm, d)],
        out_shape=[jax.ShapeDtypeStruct((t, d), F32), jax.ShapeDtypeStruct((t, d), BF16)],
        scratch_shapes=[pltpu.VMEM((d, d), BF16), pltpu.SemaphoreType.DMA((N_CHIPS,))],
        args=(x, pre, ln_g, ln_b, w, bt, wout_g))


def _sgu_bwd(x, dy, pre, g, ln_g, ln_b, w, bt, win_g, wout_g, name, comms=()):
    t, d = x.shape
    tm = min(256, t)
    pc = pre.shape[1]
    c = win_g.shape[-1]

    def body(x_ref, dy_ref, pre_ref, g_ref, lg_ref, lb_ref, w_ref, bt_ref, win_hbm, wout_hbm,
             dx_ref, dpre_ref, dg_ref, dlg_ref, dlb_ref, dw_ref, dbt_ref, win_v, wout_v, sem):
        _load_at_first_step(_in_weight_copies(win_hbm, win_v, sem) + _out_weight_copies(wout_hbm, wout_v, sem, N_CHIPS))
        i = pl.program_id(0)

        @pl.when(i == 0)
        def _():
            for ref in (dg_ref, dlg_ref, dlb_ref, dw_ref, dbt_ref):
                ref[...] = jnp.zeros_like(ref)

        prev = pre_ref[...]
        lg = lg_ref[...]
        u, vhat, rs, vln, phi = _sgu_recompute(prev, lg, lb_ref[...])
        vo, wm = _sgu_spatial(vln, w_ref, bt_ref[...], tm)
        dp = _dot_nt(dy_ref[...].astype(BF16), wout_v[...])
        du = dp * vo
        dvo = dp * u
        dvob = dvo.astype(BF16)
        vb = vln.astype(BF16)
        head_lane = lax.broadcasted_iota(jnp.int32, (CHUNK, SGU_HEADS), 1)
        dbt = jnp.zeros((CHUNK, SGU_HEADS), F32)
        dw = [jnp.zeros((CHUNK, CHUNK), F32) for _ in range(SGU_HEADS)]
        rows = []
        for ch in range(tm // CHUNK):
            rs_ = slice(ch * CHUNK, (ch + 1) * CHUNK)
            blocks = []
            for hd in range(SGU_HEADS):
                cs = slice(hd * CHUNK, (hd + 1) * CHUNK)
                dbt = dbt + jnp.where(head_lane == hd, jnp.sum(dvo[rs_, cs], axis=1, keepdims=True), 0.0)
                dw[hd] = dw[hd] + _dot_nt(dvob[rs_, cs], vb[rs_, cs])
                blocks.append(_dot_tn(wm[hd], dvob[rs_, cs]))
            rows.append(jnp.concatenate(blocks, axis=1))
        dvln = jnp.concatenate(rows, axis=0)
        mask = _causal_mask()
        for hd in range(SGU_HEADS):
            dw_ref[hd] += dw[hd] * mask
        dbt_ref[...] += dbt
        dlg_ref[...] += jnp.sum(dvln * vhat, axis=0, keepdims=True)
        dlb_ref[...] += jnp.sum(dvln, axis=0, keepdims=True)
        dv = _ln_bwd(dvln * lg, vhat, rs)
        gelu_grad = phi + prev * jnp.exp(-0.5 * prev * prev) * INV_SQRT_2PI
        dpre = (jnp.concatenate([du, dv], axis=1) * gelu_grad).astype(BF16)
        dpre_ref[...] = dpre
        dxn = _proj_in_bwd_tail(dpre, win_v, c)
        xh, r = _rms_fwd(x_ref[...])
        dg_ref[...] += jnp.sum(dxn * xh, axis=0, keepdims=True)
        dx_ref[...] = dy_ref[...] + _rms_bwd(dxn, xh, r, g_ref[...])

    return _call(
        body, name=name, grid=(t // tm,),
        in_specs=[_row_spec(tm, d), _row_spec(tm, d), _row_spec(tm, pc), _const_spec((1, d)), _const_spec((1, d)), _const_spec((1, d)),
                  _const_spec(w.shape), _const_spec(bt.shape), HBM_SPEC, HBM_SPEC],
        out_specs=[_row_spec(tm, d), _row_spec(tm, pc), _const_spec((1, d)), _const_spec((1, d)), _const_spec((1, d)),
                   _const_spec(w.shape), _const_spec(bt.shape)],
        out_shape=[jax.ShapeDtypeStruct((t, d), F32), jax.ShapeDtypeStruct((t, pc), BF16), jax.ShapeDtypeStruct((1, d), F32),
                   jax.ShapeDtypeStruct((1, d), F32), jax.ShapeDtypeStruct((1, d), F32), jax.ShapeDtypeStruct(w.shape, F32),
                   jax.ShapeDtypeStruct(bt.shape, F32)],
        scratch_shapes=[pltpu.VMEM((N_CHIPS, d, c), BF16), pltpu.VMEM((d, d), BF16), pltpu.SemaphoreType.DMA((2 * N_CHIPS,))],
        args=(x, dy, pre, g, ln_g, ln_b, w, bt, win_g, wout_g), comms=comms)


def _final_loss(x, tgt, g, name):
    t, d = x.shape
    tm = min(512, t)

    def body(x_ref, t_ref, g_ref, dx_ref, loss_ref, dg_ref):
        @pl.when(pl.program_id(0) == 0)
        def _():
            loss_ref[...] = jnp.zeros_like(loss_ref)
            dg_ref[...] = jnp.zeros_like(dg_ref)

        gv = g_ref[...]
        xh, r = _rms_fwd(x_ref[...])
        diff = xh * gv - t_ref[...]
        loss_ref[...] += 0.5 * jnp.sum(jnp.sum(diff * diff, axis=1, keepdims=True), axis=0, keepdims=True) / d
        dout = diff / d
        dg_ref[...] += jnp.sum(dout * xh, axis=0, keepdims=True)
        dx_ref[...] = _rms_bwd(dout, xh, r, gv)

    return _call(
        body, name=name, grid=(t // tm,),
        in_specs=[_row_spec(tm, d), _row_spec(tm, d), _const_spec((1, d))],
        out_specs=[_row_spec(tm, d), _const_spec((1, 1)), _const_spec((1, d))],
        out_shape=[jax.ShapeDtypeStruct((t, d), F32), jax.ShapeDtypeStruct((1, 1), F32), jax.ShapeDtypeStruct((1, d), F32)],
        args=(x, tgt, g))


def _tn_matmul(a, b, scale, bm, bn, name, comms=()):
    t, m = a.shape
    n = b.shape[1]
    tk = min(2048, t)
    bm, bn = min(bm, m), min(bn, n)
    nk = t // tk

    def body(a_ref, b_ref, o_ref, acc_ref):
        k = pl.program_id(2)

        @pl.when(k == 0)
        def _():
            acc_ref[...] = jnp.zeros_like(acc_ref)

        bv = b_ref[...]
        if bv.dtype != BF16:
            bv = (scale * bv).astype(BF16)
        acc_ref[...] += _dot_tn(a_ref[...], bv)

        @pl.when(k == nk - 1)
        def _():
            o_ref[...] = acc_ref[...].astype(BF16)

    return _call(
        body, name=name, grid=(m // bm, n // bn, nk),
        in_specs=[pl.BlockSpec((tk, bm), lambda i, j, k: (k, i)), pl.BlockSpec((tk, bn), lambda i, j, k: (k, j))],
        out_specs=[pl.BlockSpec((bm, bn), lambda i, j, k: (i, j))],
        out_shape=[jax.ShapeDtypeStruct((m, n), BF16)],
        scratch_shapes=[pltpu.VMEM((bm, bn), F32)],
        args=(a, b), comms=comms)[0]


def _row_tile(rows, cols, budget_bytes=2 * 1024 * 1024):
    best = None
    for cand in range(16, rows + 1, 16):
        if rows % cand == 0 and cand * cols * 4 <= budget_bytes:
            best = cand
    return best or rows


def _scalar_grid(grid, in_specs, out_specs):
    return pltpu.PrefetchScalarGridSpec(num_scalar_prefetch=1, grid=grid, in_specs=in_specs, out_specs=out_specs)


def _cast_into_slot(w, layer, me, name):
    _, rows, cols = w.shape
    tr = _row_tile(rows, cols)

    def body(me_ref, w_ref, o_ref):
        o_ref[...] = w_ref[...].astype(BF16)

    return pl.pallas_call(
        body, name=name,
        grid_spec=_scalar_grid((rows // tr,), [pl.BlockSpec((None, tr, cols), lambda i, me: (layer, i, 0))],
                               pl.BlockSpec((None, tr, cols), lambda i, me: (me[0], i, 0))),
        out_shape=jax.ShapeDtypeStruct((N_CHIPS, rows, cols), BF16), compiler_params=_cparams())(me, w)


def _add_half(view, other, core, name):
    q, _, r, c = view.shape
    tr = _row_tile(r, c)

    def body(core_ref, a_ref, b_ref, o_ref):
        o_ref[...] = (a_ref[...].astype(F32) + b_ref[...].astype(F32)).astype(BF16)

    return pl.pallas_call(
        body, name=name,
        grid_spec=_scalar_grid((q, r // tr), [pl.BlockSpec((None, None, tr, c), lambda k, i, core: (k, core[0], i, 0)),
                                             pl.BlockSpec((None, tr, c), lambda k, i, core: (k, i, 0))],
                               pl.BlockSpec((None, tr, c), lambda k, i, core: (k, i, 0))),
        out_shape=jax.ShapeDtypeStruct((q, r, c), BF16), compiler_params=_cparams_nd(2))(core, view, other)


def _reduce_piece(partial, staged, me, column_sharded, name):
    _, r, c = staged.shape
    tr = _row_tile(r, c, budget_bytes=1024 * 1024)
    nt = r // tr
    if column_sharded:
        own2d = partial.reshape(r, N_CHIPS * c)
        own_spec = pl.BlockSpec((tr, c), lambda i, me: (i, me[0]))
    else:
        own2d = partial.reshape(N_CHIPS * r, c)
        own_spec = pl.BlockSpec((tr, c), lambda i, me: (me[0] * nt + i, 0))
    ring = [pl.BlockSpec((None, tr, c), lambda i, me, k=k: ((me[0] + k) % N_CHIPS, i, 0)) for k in (1, 2, 3)]

    def body(me_ref, own_ref, s1_ref, s2_ref, s3_ref, o_ref):
        o_ref[...] = ((own_ref[...].astype(F32) + s1_ref[...].astype(F32)) + s2_ref[...].astype(F32)) + s3_ref[...].astype(F32)

    return pl.pallas_call(
        body, name=name, grid_spec=_scalar_grid((nt,), [own_spec] + ring, pl.BlockSpec((tr, c), lambda i, me: (i, 0))),
        out_shape=jax.ShapeDtypeStruct((r, c), F32), compiler_params=_cparams())(me, own2d, staged, staged, staged)


def _sum_leading(s, name):
    n, rows, cols = s.shape
    tr = _row_tile(rows, cols, budget_bytes=1024 * 1024)

    def body(s_ref, o_ref):
        acc = s_ref[0].astype(F32)
        for k in range(1, n):
            acc = acc + s_ref[k].astype(F32)
        o_ref[...] = acc

    return pl.pallas_call(
        body, name=name, grid=(rows // tr,), in_specs=[pl.BlockSpec((n, tr, cols), lambda i: (0, i, 0))], out_specs=_row_spec(tr, cols),
        out_shape=jax.ShapeDtypeStruct((rows, cols), F32), compiler_params=_cparams())(s)


ADAM_C1 = 1.0 / (1.0 - ADAM_B1 ** ADAM_STEP)
ADAM_C2 = 1.0 / (1.0 - ADAM_B2 ** ADAM_STEP)


def _adamw_math(w, g, m, v):
    mn = ADAM_B1 * m + (1.0 - ADAM_B1) * g
    vn = ADAM_B2 * v + (1.0 - ADAM_B2) * (g * g)
    return -ADAM_LR * ((mn * ADAM_C1) / (jnp.sqrt(vn * ADAM_C2) + ADAM_EPS) + ADAM_WD * w), mn, vn


def _adamw(w, g, m, v, name):
    shape = w.shape
    cols = shape[-1] if w.ndim > 1 else 128
    w2, g2, m2, v2 = (a.reshape(-1, cols) for a in (w, g, m, v))
    rows = w2.shape[0]
    tr = _row_tile(rows, cols, budget_bytes=1024 * 1024)

    def body(w_ref, g_ref, m_ref, v_ref, d_ref, mo_ref, vo_ref):
        d_ref[...], mo_ref[...], vo_ref[...] = _adamw_math(w_ref[...], g_ref[...], m_ref[...], v_ref[...])

    spec = _row_spec(tr, cols)
    outs = pl.pallas_call(
        body, name=name, grid=(rows // tr,), in_specs=[spec] * 4, out_specs=[spec] * 3,
        out_shape=[jax.ShapeDtypeStruct((rows, cols), F32)] * 3, compiler_params=_cparams())(w2, g2, m2, v2)
    return tuple(o.reshape(shape) for o in outs)


def _adamw_sharded(w, g_mine, g_sibling, m, v, core, layer, prev, name, comms=(), after=()):
    n_layers, r, c = w.shape
    half = r // 2
    tr = _row_tile(half, c)
    nt = half // tr

    def body(core_ref, w_ref, gm_ref, gs_ref, m_ref, v_ref, *rest):
        g_ref, d_ref, mo_ref, vo_ref = rest[-4:]
        gv = jnp.where(pl.program_id(0) == core_ref[0], gm_ref[...], gs_ref[...])
        g_ref[...] = gv
        d_ref[...], mo_ref[...], vo_ref[...] = _adamw_math(w_ref[...], gv, m_ref[...], v_ref[...])

    full = pl.BlockSpec((None, tr, c), lambda h, i, core: (layer, h * nt + i, 0))
    part = pl.BlockSpec((tr, c), lambda h, i, core: (i, 0))
    args = [w, g_mine, g_sibling, m, v]
    in_specs = [full, part, part, full, full]
    aliases = {}
    if prev is not None:
        aliases = {len(args) + k: k for k in range(4)}
        args += list(prev)
        in_specs += [pl.BlockSpec(memory_space=pl.ANY)] * 4
    args += list(after)
    in_specs += [pl.BlockSpec(memory_space=pl.ANY)] * len(after)
    return _call(body, name=name, grid=(2, nt), in_specs=in_specs, out_specs=[full] * 4, out_shape=[jax.ShapeDtypeStruct(w.shape, F32)] * 4,
                 args=args, comms=comms, scalar=core, aliases=aliases)


BIG_IN = ("ffn1_w_in", "ffn2_w_in", "ab_w_in", "sgu_w_in")
BIG_OUT = ("ffn1_w_out", "ffn2_w_out", "ab_w_out", "sgu_w_out")
BIG = BIG_IN + BIG_OUT


class _Gatherer:
    def __init__(self, slots):
        self.slots = dict(slots)

    def _stage(self, keys, d2d):
        n = len(keys)

        def plan(ins, outs, place):
            x, y, c = place
            me = 2 * x + y
            remote = []
            for a in range(n):
                rows = ins[a].shape[1] // 2

                def half(ref, q, core, rows=rows):
                    return ref.at[q, pl.ds(core * rows, rows), :]

                for (px, py) in _other_chips(x, y):
                    q = 2 * px + py
                    if d2d:
                        remote.append((half(ins[a], q, c), half(outs[a], q, c), (x, y, 1 - c), half(outs[a], q, 1 - c)))
                    else:
                        remote.append((half(ins[a], me, c), half(outs[a], me, c), (px, py, c), half(outs[a], q, c)))
            return remote

        def finish(outs):
            for k, o in zip(keys, outs):
                self.slots[k] = o

        arrays = [self.slots[k] for k in keys]
        return _Exchange(arrays, [_sds(a) for a in arrays], plan, 3 * n, {a: a for a in range(n)}, finish)

    def direct(self, keys):
        n = len(keys)

        def plan(ins, outs, place):
            x, y, c = place
            me = 2 * x + y
            return [(ins[a].at[me], outs[a].at[me], (px, py, c), outs[a].at[2 * px + py]) for a in range(n) for (px, py) in _other_chips(x, y)]

        def finish(outs):
            for k, o in zip(keys, outs):
                self.slots[k] = o

        arrays = [self.slots[k] for k in keys]
        return _Exchange(arrays, [_sds(a) for a in arrays], plan, 3 * n, {a: a for a in range(n)}, finish)

    def ici(self, keys):
        return self._stage(keys, False)

    def d2d(self, keys):
        return self._stage(keys, True)


class _Reducer:
    def __init__(self, me, core):
        self.me, self.core = me, core
        self.views, self.partial, self.staged, self.mine, self.theirs = {}, {}, {}, {}, {}

    def add(self, key, g):
        m, n = g.shape
        if key[0] in BIG_IN:
            self.views[key] = g.reshape(1, 2, m // 2, n)
        else:
            self.views[key] = g.reshape(N_CHIPS, 2, m // (2 * N_CHIPS), n)

    def swap(self, keys):
        views = [self.views[k] for k in keys]

        def plan(ins, outs, place):
            x, y, c = place
            return [(ins[a].at[:, 1 - c], outs[a], (x, y, 1 - c), outs[a]) for a in range(len(keys))]

        def finish(outs):
            for k, v, o in zip(keys, views, outs):
                self.partial[k] = _add_half(v, o, self.core, f"chip_partial_{k[0]}_{k[1]}")

        shapes = [jax.ShapeDtypeStruct((v.shape[0],) + v.shape[2:], v.dtype) for v in views]
        return _Exchange(views, shapes, plan, len(keys), None, finish)

    def scatter(self, keys, part=(0, 1)):
        i, n = part
        n_keys = len(keys)
        parts = [self.partial[k] for k in keys]
        shapes = []
        for k, p in zip(keys, parts):
            q, r, c = p.shape
            shapes.append(jax.ShapeDtypeStruct((N_CHIPS, r, c // N_CHIPS if k[0] in BIG_IN else c), p.dtype))

        def piece(ref, key, q, rows, cols):
            return ref.at[0, rows, pl.ds(q * cols, cols)] if key[0] in BIG_IN else ref.at[q, rows, :]

        def plan(ins, outs, place):
            x, y, c = place
            me = 2 * x + y
            remote = []
            for a, k in enumerate(keys):
                _, r, cols = shapes[a].shape
                rows = pl.ds(i * (r // n), r // n)
                for (px, py) in _other_chips(x, y):
                    q = 2 * px + py
                    remote.append((piece(ins[a], k, q, rows, cols), outs[a].at[me, rows, :], (px, py, c), outs[a].at[q, rows, :]))
            return remote

        def finish(outs):
            for k, p, o in zip(keys, parts, outs):
                self.staged[k] = o
                if i == n - 1:
                    self.mine[k] = _reduce_piece(p, o, self.me, k[0] in BIG_IN, f"reduce_{k[0]}_{k[1]}")

        inputs, aliases = parts, None
        if i > 0:
            inputs = parts + [self.staged[k] for k in keys]
            aliases = {n_keys + a: a for a in range(n_keys)}
        return _Exchange(inputs, shapes, plan, 3 * n_keys, aliases, finish)

    def scatter_behind(self, keys, work):
        n = len(keys)
        parts = [self.partial[k] for k in keys]
        lands = []
        for k, p in zip(keys, parts):
            _, r, c = p.shape
            lands.append(jax.ShapeDtypeStruct((N_CHIPS, r, c // N_CHIPS if k[0] in BIG_IN else c), p.dtype))
        sem_spec = pl.BlockSpec(memory_space=pltpu.SEMAPHORE)
        effect = pltpu.CompilerParams(has_side_effects=pltpu.SideEffectType.DATAFLOW_SIDE_EFFECTING)

        def copies(part_refs, land_refs, send_sems, recv_sems):
            x, y, c = _my_place()
            me = 2 * x + y
            out = []
            for a, k in enumerate(keys):
                cols = lands[a].shape[2]
                for j, (px, py) in enumerate(_other_chips(x, y)):
                    q = 2 * px + py
                    src = part_refs[a].at[0, :, pl.ds(q * cols, cols)] if k[0] in BIG_IN else part_refs[a].at[q]
                    sems = dict(send_sem=send_sems.at[3 * a + j], recv_sem=recv_sems.at[3 * a + j], device_id=(px, py, c), device_id_type=MESH)
                    out.append((pltpu.make_async_remote_copy(src_ref=src, dst_ref=land_refs[a].at[me], **sems),
                                pltpu.make_async_remote_copy(src_ref=src, dst_ref=land_refs[a].at[q], **sems)))
            return out

        def start_body(*refs):
            part_refs, land_refs, send_sems, recv_sems, token = refs[:n], refs[n:2 * n], refs[2 * n], refs[2 * n + 1], refs[-1]
            for send, _ in copies(part_refs, land_refs, send_sems, recv_sems):
                send.start()
            token[...] = jnp.zeros_like(token)

        def wait_body(*refs):
            part_refs, land_refs, send_sems, recv_sems = refs[:n], refs[n:2 * n], refs[2 * n], refs[2 * n + 1]
            for send, arrive in copies(part_refs, land_refs, send_sems, recv_sems):
                send.wait_send()
                arrive.wait_recv()

        in_hbm = [pltpu.with_memory_space_constraint(p, pltpu.HBM) for p in parts]
        in_hbm += [pltpu.with_memory_space_constraint(lax.empty(s.shape, s.dtype), pltpu.HBM) for s in lands]
        thru_shapes = [pltpu.HBM(p.shape, p.dtype) for p in parts] + [pltpu.HBM(s.shape, s.dtype) for s in lands]
        started = pl.pallas_call(
            start_body, name="scatter_last_start", in_specs=[HBM_SPEC] * (2 * n),
            out_shape=(pltpu.SemaphoreType.DMA((3 * n,)), pltpu.SemaphoreType.DMA((3 * n,)), *thru_shapes, jax.ShapeDtypeStruct((8, 128), F32)),
            out_specs=(sem_spec, sem_spec, *[HBM_SPEC] * (2 * n), pl.BlockSpec(memory_space=pltpu.VMEM)),
            input_output_aliases={i: 2 + i for i in range(2 * n)}, compiler_params=effect)(*in_hbm)
        send_sems, recv_sems, thru, token = started[0], started[1], started[2:2 + 2 * n], started[-1]
        after = work(token)
        done = pl.pallas_call(
            wait_body, name="scatter_last_wait", in_specs=[HBM_SPEC] * (2 * n) + [sem_spec, sem_spec] + [pl.BlockSpec(memory_space=pl.ANY)] * len(after),
            out_shape=tuple(thru_shapes), out_specs=tuple([HBM_SPEC] * (2 * n)), input_output_aliases={i: i for i in range(2 * n)},
            compiler_params=effect)(*thru, send_sems, recv_sems, *after)
        for a, k in enumerate(keys):
            self.staged[k] = done[n + a]
            self.mine[k] = _reduce_piece(done[a], done[n + a], self.me, k[0] in BIG_IN, f"reduce_{k[0]}_{k[1]}")

    def exchange(self, keys):
        mine = [self.mine[k] for k in keys]

        def plan(ins, outs, place):
            x, y, c = place
            return [(ins[a], outs[a], (x, y, 1 - c), outs[a]) for a in range(len(keys))]

        def finish(outs):
            for k, o in zip(keys, outs):
                self.theirs[k] = o

        return _Exchange(mine, [_sds(a) for a in mine], plan, len(keys), None, finish)


def _all_gather_full(gat, keys):
    n = len(keys)
    arrays = [gat.slots[k] for k in keys]
    per = 7

    def body(*refs):
        ins, outs = refs[:n], refs[n:2 * n]
        send_sems, recv_sems = refs[2 * n:]
        x, y, c = _my_place()
        sibling, x_nbr, y_nbr = (x, y, 1 - c), (1 - x, y, c), (x, 1 - y, c)
        me, qx, qy, qd = 2 * x + y, 2 * (1 - x) + y, 2 * x + (1 - y), 2 * (1 - x) + (1 - y)

        def half(ref, q, core):
            rows = ref.shape[1] // 2
            return ref.at[q, pl.ds(core * rows, rows), :]

        def quarter(ref, q, core, k):
            rows = ref.shape[1] // 4
            return ref.at[q, pl.ds((2 * core + k) * rows, rows), :]

        def copy(a, k, src, dst, to):
            return pltpu.make_async_remote_copy(src_ref=src, dst_ref=dst, send_sem=send_sems.at[per * a + k],
                                                recv_sem=recv_sems.at[per * a + k], device_id=to, device_id_type=MESH)

        sent = []

        def send(a, k, part, to):
            cp = copy(a, k, part, part, to)
            cp.start()
            sent.append(cp)

        def landed(a, k, part):
            copy(a, k, part, part, sibling).wait_recv()

        for a in range(n):
            mine_in, mine_out = half(ins[a], me, c), half(outs[a], me, c)
            for k, to in ((0, x_nbr), (1, y_nbr)):
                cp = copy(a, k, mine_in, mine_out, to)
                cp.start()
                sent.append(cp)
        for a in range(n):
            landed(a, 1, half(outs[a], qy, c))
            send(a, 2, quarter(outs[a], qy, c, 0), x_nbr)
            send(a, 5, half(outs[a], qy, c), sibling)
            landed(a, 0, half(outs[a], qx, c))
            send(a, 3, quarter(outs[a], qx, c, 1), y_nbr)
            send(a, 4, half(outs[a], qx, c), sibling)
        for a in range(n):
            landed(a, 2, quarter(outs[a], qd, c, 0))
            landed(a, 3, quarter(outs[a], qd, c, 1))
            send(a, 6, half(outs[a], qd, c), sibling)
        for a in range(n):
            for k, q in ((4, qx), (5, qy), (6, qd)):
                landed(a, k, half(outs[a], q, 1 - c))
        for cp in sent:
            cp.wait_send()

    outs = pl.pallas_call(
        body, name="all_gather_first_weights", in_specs=[HBM_SPEC] * n, out_specs=[HBM_SPEC] * n,
        out_shape=[_sds(a) for a in arrays], input_output_aliases={a: a for a in range(n)},
        scratch_shapes=[pltpu.SemaphoreType.DMA((per * n,)), pltpu.SemaphoreType.DMA((per * n,))])(*arrays)
    for k, o in zip(keys, outs):
        gat.slots[k] = o


def _small_all_gather(buf, done):
    state = {}

    def index(x, y, c):
        return 4 * x + 2 * y + c

    def plan_ici(ins, outs, place):
        x, y, c = place
        return [(ins[0], outs[0].at[index(x, y, c)], (px, py, c), outs[0].at[index(px, py, c)]) for (px, py) in _other_chips(x, y)]

    def local(ins, outs, place):
        return [(ins[0], outs[0].at[index(*place)])]

    def plan_d2d(ins, outs, place):
        x, y, c = place
        return [(ins[0].at[index(px, py, c)], outs[0].at[index(px, py, c)], (x, y, 1 - c), outs[0].at[index(px, py, 1 - c)])
                for (px, py) in [(x, y)] + _other_chips(x, y)]

    def second():
        return _Exchange([state["blocks"]], [_sds(state["blocks"])], plan_d2d, N_CHIPS, {0: 0}, lambda outs: done(outs[0]))

    first = _Exchange([buf], [jax.ShapeDtypeStruct((2 * N_CHIPS,) + buf.shape, buf.dtype)], plan_ici, 3, None,
                      lambda outs: state.update(blocks=outs[0]), local, 1)
    return first, second


WEIGHT_NAMES = ("ffn1_norm", "ffn1_w_in", "ffn1_w_out", "mix_norm", "ffn2_norm", "ffn2_w_in", "ffn2_w_out", "ab_w_in", "pool_w", "pool_b",
                "pool_scale", "conv_w", "conv_b", "conv_ln_g", "conv_ln_b", "ab_w_out", "sgu_w_in", "sgu_ln_g", "sgu_ln_b", "sgu_w", "sgu_b",
                "sgu_w_out", "final_norm")
SMALL = tuple(n for n in WEIGHT_NAMES if n not in BIG)
SHARDED_SMALL = ("conv_w", "sgu_ln_g", "sgu_ln_b")
PACK_ROWS = 64
PACK = ("pack", 0)


def _pair(prefix, layer):
    return [(prefix + "_w_in", layer), (prefix + "_w_out", layer)]


def kernel(x, ffn1_norm, ffn1_w_in, ffn1_w_out, mix_norm, ffn2_norm, ffn2_w_in, ffn2_w_out, ab_w_in, pool_w, pool_b, pool_scale, conv_w, conv_b, conv_ln_g, conv_ln_b, ab_w_out, sgu_w_in, sgu_ln_g, sgu_ln_b, sgu_w, sgu_b, sgu_w_out, final_norm, loss_target, m_ffn1_norm, m_ffn1_w_in, m_ffn1_w_out, m_mix_norm, m_ffn2_norm, m_ffn2_w_in, m_ffn2_w_out, m_ab_w_in, m_pool_w, m_pool_b, m_pool_scale, m_conv_w, m_conv_b, m_conv_ln_g, m_conv_ln_b, m_ab_w_out, m_sgu_w_in, m_sgu_ln_g, m_sgu_ln_b, m_sgu_w, m_sgu_b, m_sgu_w_out, m_final_norm, v_ffn1_norm, v_ffn1_w_in, v_ffn1_w_out, v_mix_norm, v_ffn2_norm, v_ffn2_w_in, v_ffn2_w_out, v_ab_w_in, v_pool_w, v_pool_b, v_pool_scale, v_conv_w, v_conv_b, v_conv_ln_g, v_conv_ln_b, v_ab_w_out, v_sgu_w_in, v_sgu_ln_g, v_sgu_ln_b, v_sgu_w, v_sgu_b, v_sgu_w_out, v_final_norm):
    given = dict(locals())
    w = {n: given[n] for n in WEIGHT_NAMES}
    chip = 2 * lax.axis_index("x") + lax.axis_index("y")
    me = chip.astype(jnp.int32).reshape(1)
    core = lax.axis_index("c").astype(jnp.int32).reshape(1)
    row = lambda v: v.reshape(1, -1)
    xin, tgt = x[0], loss_target[0]

    pack = jnp.concatenate([
        w["conv_w"][0], jnp.zeros((1, 128), F32), w["sgu_ln_g"].reshape(2, 128), w["sgu_ln_b"].reshape(2, 128),
        jnp.zeros((PACK_ROWS - 36, 128), F32)], axis=0)
    slots = {PACK: lax.dynamic_update_slice(jnp.zeros((N_CHIPS, PACK_ROWS, 128), F32), pack[None], (me[0], 0, 0))}
    for n in BIG:
        for layer in range(w[n].shape[0]):
            slots[(n, layer)] = _cast_into_slot(w[n], layer, me, f"cast_{n}_{layer}")
    gat = _Gatherer(slots)
    _all_gather_full(gat, _pair("ffn1", 0) + [PACK])
    gp = gat.slots[PACK]
    conv_w_full = jnp.transpose(gp[:, 0:CONV_WIDTH], (1, 0, 2)).reshape(CONV_WIDTH, N_CHIPS * 128)
    sgu_ln_g_full = gp[:, 32:34].reshape(1, -1)
    sgu_ln_b_full = gp[:, 34:36].reshape(1, -1)
    gw = lambda n, layer: gat.slots[(n, layer)]

    st = [dict(), dict()]
    st[0]["xa"] = xin
    later = _pair("sgu", 0) + _pair("ffn2", 1)
    cur, st[0]["h1"], st[0]["xn1"] = _ffn_fwd(xin, row(w["ffn1_norm"][0]), gw("ffn1_w_in", 0), gw("ffn1_w_out", 0), "ffn1_fwd_0",
                                              comms=[gat.direct(_pair("ab", 0)), gat.ici(_pair("ffn2", 0))])
    st[0]["xb"] = cur
    st[0]["h0"], st[0]["xnm"] = _norm_matmul(cur, row(w["mix_norm"][0]), gw("ab_w_in", 0), "mix0_proj_in",
                                             comms=[gat.d2d(_pair("ffn2", 0)), gat.ici([("ffn1_w_out", 1)])])
    pool_args = (w["pool_w"][0], row(w["pool_b"][0]), row(w["pool_scale"][0]), conv_w_full, row(w["conv_b"][0]), row(w["conv_ln_g"][0]),
                 row(w["conv_ln_b"][0]), gw("ab_w_out", 0))
    cur, st[0]["ycat"], st[0]["yconv"] = _mix0_fwd(cur, st[0]["h0"], *pool_args, "mix0_fwd", comms=[gat.ici([("ffn1_w_in", 1)])])
    st[0]["xc"] = cur
    cur, st[0]["h2"], st[0]["xn2"] = _ffn_fwd(cur, row(w["ffn2_norm"][0]), gw("ffn2_w_in", 0), gw("ffn2_w_out", 0), "ffn2_fwd_0",
                                              comms=[gat.d2d(_pair("ffn1", 1)), gat.ici(later)])
    st[1]["xa"] = cur
    cur, st[1]["h1"], st[1]["xn1"] = _ffn_fwd(cur, row(w["ffn1_norm"][1]), gw("ffn1_w_in", 1), gw("ffn1_w_out", 1), "ffn1_fwd_1",
                                              comms=[gat.d2d(later)])
    st[1]["xb"] = cur
    st[1]["pre"], st[1]["xnm"] = _norm_matmul(cur, row(w["mix_norm"][1]), gw("sgu_w_in", 0), "sgu_proj_in")
    sgu_args = (sgu_ln_g_full, sgu_ln_b_full, w["sgu_w"][0], w["sgu_b"][0].T)
    cur, st[1]["p"] = _sgu_fwd(cur, st[1]["pre"], *sgu_args, gw("sgu_w_out", 0), "sgu_fwd")
    st[1]["xc"] = cur
    cur, st[1]["h2"], st[1]["xn2"] = _ffn_fwd(cur, row(w["ffn2_norm"][1]), gw("ffn2_w_in", 1), gw("ffn2_w_out", 1), "ffn2_fwd_1")
    dy, loss, d_final = _final_loss(cur, tgt, row(w["final_norm"]), "final_loss")

    red = _Reducer(me, core)
    small = {"final_norm": d_final.reshape(-1)}
    norm_grads = {"ffn1_norm": [None] * DEPTH, "mix_norm": [None] * DEPTH, "ffn2_norm": [None] * DEPTH}
    ga, gb, gc, gd, ge, gf = _pair("ffn2", 1), _pair("sgu", 0), _pair("ffn1", 1), _pair("ffn2", 0), _pair("ab", 0), _pair("ffn1", 0)

    def ffn_backward(prefix, layer, xs, hs, xns, dy_in, bwd_comms=(), dwin_comms=(), dwout_comms=()):
        dx, dh, act, norm_grads[prefix + "_norm"][layer] = _ffn_bwd(
            xs, dy_in, hs, row(w[prefix + "_norm"][layer]), gw(prefix + "_w_in", layer), gw(prefix + "_w_out", layer),
            f"{prefix}_bwd_{layer}", comms=bwd_comms)
        red.add((prefix + "_w_in", layer), _tn_matmul(xns, dh, 1.0, 1024, 1408, f"{prefix}_dwin_{layer}", comms=dwin_comms))
        red.add((prefix + "_w_out", layer), _tn_matmul(act, dy_in, 0.5, 1408, 1024, f"{prefix}_dwout_{layer}", comms=dwout_comms))
        return dx

    s1, s0 = st[1], st[0]
    dy = ffn_backward("ffn2", 1, s1["xc"], s1["h2"], s1["xn2"], dy)
    dy_in = dy
    dy, dpre, norm_grads["mix_norm"][1], dlg, dlb, dw, dbt = _sgu_bwd(
        s1["xb"], dy_in, s1["pre"], row(w["mix_norm"][1]), *sgu_args, gw("sgu_w_in", 0), gw("sgu_w_out", 0), "sgu_bwd", comms=[red.swap(ga)])
    red.add(("sgu_w_in", 0), _tn_matmul(s1["xnm"], dpre, 1.0, 1024, 2048, "sgu_dwin"))
    red.add(("sgu_w_out", 0), _tn_matmul(s1["p"], dy_in, 1.0, 1024, 1024, "sgu_dwout"))
    small.update(sgu_ln_g=dlg, sgu_ln_b=dlb, sgu_w=dw[None], sgu_b=dbt.T[None])
    dy = ffn_backward("ffn1", 1, s1["xa"], s1["h1"], s1["xn1"], dy, bwd_comms=[lambda: red.scatter(ga), lambda: red.swap(gb)],
                      dwin_comms=[lambda: red.scatter(gb), lambda: red.exchange(ga)])
    dy = ffn_backward("ffn2", 0, s0["xc"], s0["h2"], s0["xn2"], dy, bwd_comms=[lambda: red.swap(gc), lambda: red.exchange(gb)],
                      dwin_comms=[lambda: red.scatter(gc)])
    dy_in = dy
    dconv, dpc, dpw, rows = _mix0_bwd_a(dy_in, s0["h0"], s0["yconv"], *pool_args, "mix0_bwd_a")
    dy, dh0, norm_grads["mix_norm"][0] = _mix0_bwd_b(s0["xb"], dy_in, s0["h0"], dconv, dpc, row(w["mix_norm"][0]), conv_w_full,
                                                      gw("ab_w_in", 0), "mix0_bwd_b")
    red.add(("ab_w_in", 0), _tn_matmul(s0["xnm"], dh0, 1.0, 1024, 1536, "ab_dwin", comms=[red.swap(gd), red.exchange(gc)]))
    red.add(("ab_w_out", 0), _tn_matmul(s0["ycat"], dy_in, 1.0, 1024, 1024, "ab_dwout"))
    small.update(pool_w=dpw[None], conv_w=rows[None, 0:CONV_WIDTH], conv_b=rows[32:33], conv_ln_g=rows[33:34], conv_ln_b=rows[34:35],
                 pool_scale=rows[35:36], pool_b=rows[36:37].reshape(1, len(POOL_WINDOWS), POOL_GC))

    small_sum = {}

    def small_ready():
        for k, v in norm_grads.items():
            small[k] = jnp.concatenate(v, axis=0)
        flat = [small[n].reshape(-1, 128) for n in SMALL]
        rows = sum(f.shape[0] for f in flat)
        loss_block = jnp.pad(loss, ((0, 8 + (-rows) % 8 - 1), (0, 127)))
        buf = jnp.concatenate(flat + [loss_block], axis=0)

        def done(gathered):
            total, at = _sum_leading(gathered, "reduce_small"), 0
            for n, f in zip(SMALL, flat):
                small_sum[n] = total[at:at + f.shape[0]].reshape(small[n].shape)
                at += f.shape[0]
            small_sum["loss"] = total[at:at + 1, 0:1]

        return _small_all_gather(buf, done)

    dx, dh, act, norm_grads["ffn1_norm"][0] = _ffn_bwd(
        s0["xa"], dy, s0["h1"], row(w["ffn1_norm"][0]), gw("ffn1_w_in", 0), gw("ffn1_w_out", 0), "ffn1_bwd_0")
    small_first, small_second = small_ready()
    red.add(("ffn1_w_in", 0), _tn_matmul(s0["xn1"], dh, 1.0, 1024, 1408, "ffn1_dwin_0",
                                         comms=[red.scatter(gd), red.swap(ge), small_first]))
    red.add(("ffn1_w_out", 0), _tn_matmul(act, dy, 0.5, 1408, 1024, "ffn1_dwout_0",
                                          comms=[red.scatter(ge), red.exchange(gd), small_second]))
    grad_x = dx

    big_out = {}

    def adamw_big(n, layer, after=()):
        big_out[n] = _adamw_sharded(w[n], red.mine[(n, layer)], red.theirs[(n, layer)], given["m_" + n], given["v_" + n], core, layer,
                                    big_out.get(n), f"adamw_{n}_{layer}", after=after)

    _exchange_alone("swap_last_grads", [red.swap(gf), red.exchange(ge)])

    def other_updates(token):
        for n in BIG:
            for layer in reversed(range(w[n].shape[0])):
                if (n, layer) not in gf:
                    adamw_big(n, layer, after=[token])
        return [big_out[n][0] for n in BIG]

    red.scatter_behind(gf, other_updates)
    _exchange_alone("exchange_last_grads", [red.exchange(gf)])
    for key in gf:
        adamw_big(*key)

    loss = small_sum["loss"][0, 0]
    grads, delta, new_m, new_v = {}, {}, {}, {}
    for n in WEIGHT_NAMES:
        mom, var = given["m_" + n], given["v_" + n]
        if n in BIG:
            grads[n], delta[n], new_m[n], new_v[n] = big_out[n]
            continue
        g = small_sum[n]
        if n in SHARDED_SMALL:
            width = w[n].shape[-1]
            g = lax.dynamic_slice_in_dim(g, chip * width, width, axis=g.ndim - 1)
        grads[n] = g
        delta[n], new_m[n], new_v[n] = _adamw(w[n], g, mom, var, f"adamw_{n}")
    return (loss, grad_x[None], *[grads[n] for n in WEIGHT_NAMES], *[delta[n] for n in WEIGHT_NAMES],
            *[new_m[n] for n in WEIGHT_NAMES], *[new_v[n] for n in WEIGHT_NAMES])
```

```python
import jax
import jax.numpy as jnp
from jax import lax
from jax.experimental import pallas as pl
from jax.experimental.pallas import tpu as pltpu

F32, BF16 = jnp.float32, jnp.bfloat16
EPS = 1e-6
N_CHIPS = 4
POOL_WINDOWS = (2, 4, 8, 16)
POOL_GC = 128
POOL_CH = 512
CONV_CH = 512
CONV_WIDTH = 31
HALO = 32
SGU_HEADS = 8
CHUNK = 128
DEPTH = 2
ADAM_LR, ADAM_B1, ADAM_B2, ADAM_EPS, ADAM_WD, ADAM_STEP = 0.001, 0.9, 0.999, 1e-08, 0.01, 10
VMEM_LIMIT_BYTES = 60 * 1024 * 1024
MESH_AXES = ("x", "y", "c")
MESH = pl.DeviceIdType.MESH
HBM_SPEC = pl.BlockSpec(memory_space=pltpu.HBM)


def _sds(a):
    return jax.ShapeDtypeStruct(a.shape, a.dtype)


def _cparams_nd(n):
    return pltpu.CompilerParams(dimension_semantics=("arbitrary",) * n, vmem_limit_bytes=VMEM_LIMIT_BYTES)


def _cparams():
    return _cparams_nd(1)


def _dot(a, b):
    return jnp.dot(a, b, preferred_element_type=F32)


def _dot_nt(a, b):
    return lax.dot_general(a, b, (((1,), (1,)), ((), ())), preferred_element_type=F32)


def _dot_tn(a, b):
    return lax.dot_general(a, b, (((0,), (0,)), ((), ())), preferred_element_type=F32)


def _rms_fwd(x):
    r = lax.rsqrt(jnp.mean(x * x, axis=-1, keepdims=True) + EPS)
    return x * r, r


def _rms_bwd(dxn, xh, r, g):
    dxh = dxn * g
    return r * (dxh - xh * jnp.mean(dxh * xh, axis=-1, keepdims=True))


def _ln_fwd(y):
    mu = jnp.mean(y, axis=-1, keepdims=True)
    yc = y - mu
    rs = lax.rsqrt(jnp.mean(yc * yc, axis=-1, keepdims=True) + EPS)
    return yc * rs, rs


def _ln_bwd(dyhat, yhat, rs):
    return rs * (dyhat - jnp.mean(dyhat, axis=-1, keepdims=True) - yhat * jnp.mean(dyhat * yhat, axis=-1, keepdims=True))


def _sigmoid(x):
    return 0.5 * jnp.tanh(0.5 * x) + 0.5


def _const_spec(shape):
    n = len(shape)
    return pl.BlockSpec(shape, lambda i: (0,) * n)


def _row_spec(tm, cols):
    return pl.BlockSpec((tm, cols), lambda i: (i, 0))


def _my_place():
    return lax.axis_index("x"), lax.axis_index("y"), lax.axis_index("c")


def _other_chips(x, y):
    return [(1 - x, y), (x, 1 - y), (1 - x, 1 - y)]


class _Exchange:
    def __init__(self, inputs, out_shapes, plan, count, aliases=None, finish=None, local=None, n_local=0):
        self.inputs, self.out_shapes, self.plan, self.count = list(inputs), list(out_shapes), plan, count
        self.aliases, self.finish, self.local, self.n_local = dict(aliases or {}), finish, local, n_local


def _call(body, *, name, grid, in_specs, out_specs, out_shape, args, scratch_shapes=(), comms=(), scalar=None, aliases=None):
    comms = [cm if isinstance(cm, _Exchange) else cm() for cm in comms]
    in_specs, out_specs, out_shape, scratch_shapes = list(in_specs), list(out_specs), list(out_shape), list(scratch_shapes)
    n_in, n_out, n_scr = len(in_specs), len(out_specs), len(scratch_shapes)
    c_in = [a for cm in comms for a in cm.inputs]
    c_out = [s for cm in comms for s in cm.out_shapes]
    n_remote = sum(cm.count for cm in comms)
    n_local = sum(cm.n_local for cm in comms)
    n_scalar = 0 if scalar is None else 1
    all_aliases = {n_scalar + i: o for i, o in (aliases or {}).items()}
    at_in, at_out = n_scalar + n_in, n_out
    for cm in comms:
        for i, o in cm.aliases.items():
            all_aliases[at_in + i] = at_out + o
        at_in += len(cm.inputs)
        at_out += len(cm.out_shapes)

    def wrapped(*all_refs):
        scalar_ref, refs = all_refs[:n_scalar], all_refs[n_scalar:]
        ins, ci = refs[:n_in], refs[n_in:n_in + len(c_in)]
        at = n_in + len(c_in)
        outs, co = refs[at:at + n_out], refs[at + n_out:at + n_out + len(c_out)]
        at += n_out + len(c_out)
        scr = refs[at:at + n_scr]

        def run_body():
            body(*scalar_ref, *ins, *outs, *scr)

        if not comms:
            run_body()
            return
        send_sems, recv_sems, local_sems = refs[at + n_scr:]
        place = _my_place()
        sends, arrivals, locals_ = [], [], []
        i0 = o0 = 0
        for cm in comms:
            cm_in, cm_out = ci[i0:i0 + len(cm.inputs)], co[o0:o0 + len(cm.out_shapes)]
            i0 += len(cm.inputs)
            o0 += len(cm.out_shapes)
            for src, dst, dev, incoming in cm.plan(cm_in, cm_out, place):
                k = len(sends)
                sends.append(pltpu.make_async_remote_copy(src_ref=src, dst_ref=dst, send_sem=send_sems.at[k], recv_sem=recv_sems.at[k],
                                                          device_id=dev, device_id_type=MESH))
                arrivals.append(pltpu.make_async_remote_copy(src_ref=src, dst_ref=incoming, send_sem=send_sems.at[k],
                                                             recv_sem=recv_sems.at[k], device_id=dev, device_id_type=MESH))
            if cm.local is not None:
                for src, dst in cm.local(cm_in, cm_out, place):
                    locals_.append(pltpu.make_async_copy(src, dst, local_sems.at[len(locals_)]))

        def start():
            for cp in locals_ + sends:
                cp.start()

        def finish():
            for cp in arrivals:
                cp.wait_recv()
            for cp in sends:
                cp.wait_send()
            for cp in locals_:
                cp.wait()

        if not grid:
            start()
            run_body()
            finish()
            return
        ids = [pl.program_id(a) for a in range(len(grid))]
        first, last = ids[0] == 0, ids[0] == grid[0] - 1
        for a in range(1, len(grid)):
            first = jnp.logical_and(first, ids[a] == 0)
            last = jnp.logical_and(last, ids[a] == grid[a] - 1)
        pl.when(first)(start)
        run_body()
        pl.when(last)(finish)

    sems = []
    if comms:
        sems = [pltpu.SemaphoreType.DMA((max(n_remote, 1),)), pltpu.SemaphoreType.DMA((max(n_remote, 1),)),
                pltpu.SemaphoreType.DMA((max(n_local, 1),))]
    all_in, all_out = in_specs + [HBM_SPEC] * len(c_in), out_specs + [HBM_SPEC] * len(c_out)
    if scalar is None:
        kwargs = dict(grid=grid, compiler_params=_cparams_nd(len(grid))) if grid else {}
        res = pl.pallas_call(
            wrapped, name=name, in_specs=all_in, out_specs=all_out, out_shape=out_shape + c_out, scratch_shapes=scratch_shapes + sems,
            input_output_aliases=all_aliases, **kwargs)(*args, *c_in)
    else:
        spec = pltpu.PrefetchScalarGridSpec(num_scalar_prefetch=1, grid=grid, in_specs=all_in, out_specs=all_out,
                                            scratch_shapes=scratch_shapes + sems)
        res = pl.pallas_call(
            wrapped, name=name, grid_spec=spec, out_shape=out_shape + c_out, input_output_aliases=all_aliases,
            compiler_params=_cparams_nd(len(grid)))(scalar, *args, *c_in)
    at = n_out
    for cm in comms:
        got = res[at:at + len(cm.out_shapes)]
        at += len(cm.out_shapes)
        if cm.finish is not None:
            cm.finish(got)
    return list(res[:n_out])


def _exchange_alone(name, comms):
    _call(lambda: None, name=name, grid=(), in_specs=[], out_specs=[], out_shape=[], args=[], comms=comms)


def _in_weight_copies(w_hbm, w_v, sem, base=0):
    return [pltpu.make_async_copy(w_hbm.at[q], w_v.at[q], sem.at[base + q]) for q in range(N_CHIPS)]


def _out_weight_copies(w_hbm, w_v, sem, base=0):
    rows = w_hbm.shape[1]
    return [pltpu.make_async_copy(w_hbm.at[q], w_v.at[pl.ds(q * rows, rows)], sem.at[base + q]) for q in range(N_CHIPS)]


def _load_at_first_step(copies):
    @pl.when(pl.program_id(0) == 0)
    def _():
        for cp in copies:
            cp.start()
        for cp in copies:
            cp.wait()


def _ffn_fwd(x, g, win_g, wout_g, name, comms=()):
    t, d = x.shape
    c = win_g.shape[-1]
    ff = 2 * c
    tm = min(512, t)

    def body(x_ref, g_ref, win_hbm, wout_hbm, xo_ref, h_ref, xn_ref, win_v, wout_v, sem):
        _load_at_first_step(_in_weight_copies(win_hbm, win_v, sem) + _out_weight_copies(wout_hbm, wout_v, sem, N_CHIPS))
        xv = x_ref[...]
        xh, _ = _rms_fwd(xv)
        xn = (xh * g_ref[...]).astype(BF16)
        xn_ref[...] = xn
        acc = jnp.zeros((tm, d), F32)
        for j in range(2):
            gate = _dot(xn, win_v[j])
            up = _dot(xn, win_v[j + 2])
            h_ref[:, j * c:(j + 1) * c] = gate.astype(BF16)
            h_ref[:, ff + j * c:ff + (j + 1) * c] = up.astype(BF16)
            act = (gate * _sigmoid(gate) * up).astype(BF16)
            acc = acc + _dot(act, wout_v[j * c:(j + 1) * c, :])
        xo_ref[...] = xv + 0.5 * acc

    return _call(
        body, name=name, grid=(t // tm,),
        in_specs=[_row_spec(tm, d), _const_spec((1, d)), HBM_SPEC, HBM_SPEC],
        out_specs=[_row_spec(tm, d), _row_spec(tm, 2 * ff), _row_spec(tm, d)],
        out_shape=[jax.ShapeDtypeStruct((t, d), F32), jax.ShapeDtypeStruct((t, 2 * ff), BF16), jax.ShapeDtypeStruct((t, d), BF16)],
        scratch_shapes=[pltpu.VMEM((N_CHIPS, d, c), BF16), pltpu.VMEM((ff, d), BF16), pltpu.SemaphoreType.DMA((2 * N_CHIPS,))],
        args=(x, g, win_g, wout_g), comms=comms)


def _ffn_bwd(x, dy, h, g, win_g, wout_g, name, comms=()):
    t, d = x.shape
    c = win_g.shape[-1]
    ff = 2 * c
    tm = min(256, t)

    def body(x_ref, dy_ref, h_ref, g_ref, win_hbm, wout_hbm, dx_ref, dh_ref, act_ref, dg_ref, win_v, wout_v, sem):
        _load_at_first_step(_in_weight_copies(win_hbm, win_v, sem) + _out_weight_copies(wout_hbm, wout_v, sem, N_CHIPS))

        @pl.when(pl.program_id(0) == 0)
        def _():
            dg_ref[...] = jnp.zeros_like(dg_ref)

        xv, dyv, gv = x_ref[...], dy_ref[...], g_ref[...]
        xh, r = _rms_fwd(xv)
        dyh = (0.5 * dyv).astype(BF16)
        dxn = jnp.zeros((tm, d), F32)
        for j in range(2):
            gate = h_ref[:, j * c:(j + 1) * c].astype(F32)
            up = h_ref[:, ff + j * c:ff + (j + 1) * c].astype(F32)
            dact = _dot_nt(dyh, wout_v[j * c:(j + 1) * c, :])
            s = _sigmoid(gate)
            sl = gate * s
            act_ref[:, j * c:(j + 1) * c] = (sl * up).astype(BF16)
            dgate = (dact * up * (s + sl * (1.0 - s))).astype(BF16)
            dup = (dact * sl).astype(BF16)
            dh_ref[:, j * c:(j + 1) * c] = dgate
            dh_ref[:, ff + j * c:ff + (j + 1) * c] = dup
            dxn = dxn + _dot_nt(dgate, win_v[j]) + _dot_nt(dup, win_v[j + 2])
        dg_ref[...] += jnp.sum(dxn * xh, axis=0, keepdims=True)
        dx_ref[...] = dyv + _rms_bwd(dxn, xh, r, gv)

    return _call(
        body, name=name, grid=(t // tm,),
        in_specs=[_row_spec(tm, d), _row_spec(tm, d), _row_spec(tm, 2 * ff), _const_spec((1, d)), HBM_SPEC, HBM_SPEC],
        out_specs=[_row_spec(tm, d), _row_spec(tm, 2 * ff), _row_spec(tm, ff), _const_spec((1, d))],
        out_shape=[jax.ShapeDtypeStruct((t, d), F32), jax.ShapeDtypeStruct((t, 2 * ff), BF16), jax.ShapeDtypeStruct((t, ff), BF16),
                   jax.ShapeDtypeStruct((1, d), F32)],
        scratch_shapes=[pltpu.VMEM((N_CHIPS, d, c), BF16), pltpu.VMEM((ff, d), BF16), pltpu.SemaphoreType.DMA((2 * N_CHIPS,))],
        args=(x, dy, h, g, win_g, wout_g), comms=comms)


def _norm_matmul(x, g, win_g, name, comms=()):
    t, d = x.shape
    c = win_g.shape[-1]
    tm = min(512, t)

    def body(x_ref, g_ref, win_hbm, o_ref, xn_ref, win_v, sem):
        _load_at_first_step(_in_weight_copies(win_hbm, win_v, sem))
        xh, _ = _rms_fwd(x_ref[...])
        xn = (xh * g_ref[...]).astype(BF16)
        xn_ref[...] = xn
        for q in range(N_CHIPS):
            o_ref[:, q * c:(q + 1) * c] = _dot(xn, win_v[q])

    return _call(
        body, name=name, grid=(t // tm,),
        in_specs=[_row_spec(tm, d), _const_spec((1, d)), HBM_SPEC],
        out_specs=[_row_spec(tm, N_CHIPS * c), _row_spec(tm, d)],
        out_shape=[jax.ShapeDtypeStruct((t, N_CHIPS * c), F32), jax.ShapeDtypeStruct((t, d), BF16)],
        scratch_shapes=[pltpu.VMEM((N_CHIPS, d, c), BF16), pltpu.SemaphoreType.DMA((N_CHIPS,))],
        args=(x, g, win_g), comms=comms)


def _proj_in_bwd_tail(dh, win_v, c):
    dxn = _dot_nt(dh[:, 0:c], win_v[0])
    for q in range(1, N_CHIPS):
        dxn = dxn + _dot_nt(dh[:, q * c:(q + 1) * c], win_v[q])
    return dxn


def _prev_halo_spec(tm, cols):
    return pl.BlockSpec((HALO, cols), lambda i: (jnp.maximum(i * (tm // HALO) - 1, 0), 0))


def _next_halo_spec(tm, cols, t):
    last = t // HALO - 1
    return pl.BlockSpec((HALO, cols), lambda i: (jnp.minimum((i + 1) * (tm // HALO), last), 0))


def _shift_down(w, k):
    return w if k == 0 else pltpu.roll(w, k, 0)


def _shift_up(w, k):
    return w if k == 0 else pltpu.roll(w, w.shape[0] - k, 0)


def _pool_counts(i, tm):
    pos = (i * tm + lax.broadcasted_iota(jnp.int32, (tm, POOL_CH), 0) + 1).astype(F32)
    lane = lax.broadcasted_iota(jnp.int32, (tm, POOL_CH), 1)
    win = jnp.where(lane < POOL_GC, 2.0, jnp.where(lane < 2 * POOL_GC, 4.0, jnp.where(lane < 3 * POOL_GC, 8.0, 16.0)))
    return jnp.minimum(pos, win)


def _group_select(parts):
    return jnp.concatenate([p[:, k * POOL_GC:(k + 1) * POOL_GC] for k, p in enumerate(parts)], axis=1)


def _mix0_recompute(i, tm, h_cur, h_prev, conv_w, conv_b, y=None):
    prev = jnp.where(i > 0, h_prev, 0.0)
    win = jnp.concatenate([prev, h_cur], axis=0)
    u_w = win[:, 0:POOL_CH]
    a_w = win[:, POOL_CH:POOL_CH + CONV_CH]
    gt_w = win[:, POOL_CH + CONV_CH:]
    g_w = a_w * _sigmoid(gt_w)
    if y is None:
        y = jnp.zeros((tm, CONV_CH), F32)
        for k in range(CONV_WIDTH):
            y = y + conv_w[k:k + 1, :] * _shift_down(g_w, CONV_WIDTH - 1 - k)[HALO:, :]
        y = y + conv_b
    s2 = u_w + _shift_down(u_w, 1)
    s4 = s2 + _shift_down(s2, 2)
    s8 = s4 + _shift_down(s4, 4)
    s16 = s8 + _shift_down(s8, 8)
    sums = _group_select([s2[HALO:], s4[HALO:], s8[HALO:], s16[HALO:]])
    cnt = _pool_counts(i, tm)
    pooled = sums / cnt - h_cur[:, 0:POOL_CH]
    return g_w, y, pooled, cnt


def _pool_linear(pooled, pw_ref, pb):
    return jnp.concatenate(
        [_dot(pooled[:, k * POOL_GC:(k + 1) * POOL_GC].astype(BF16), pw_ref[k].astype(BF16)) for k in range(len(POOL_WINDOWS))], axis=1) + pb


def _mix0_fwd(x, h0, pool_w, pool_b, pool_scale, conv_w, conv_b, ln_g, ln_b, wout_g, name, comms=()):
    t, d = x.shape
    tm = min(256, t)
    hc = h0.shape[1]

    def body(x_ref, h_ref, hp_ref, pw_ref, pb_ref, ps_ref, cw_ref, cb_ref, lg_ref, lb_ref, wout_hbm, xo_ref, ycat_ref, y_ref, wout_v, sem):
        _load_at_first_step(_out_weight_copies(wout_hbm, wout_v, sem))
        i = pl.program_id(0)
        _, y, pooled, _ = _mix0_recompute(i, tm, h_ref[...], hp_ref[...], cw_ref[...], cb_ref[...])
        y_ref[...] = y
        yhat, _ = _ln_fwd(y)
        yn = yhat * lg_ref[...] + lb_ref[...]
        yb = yn * _sigmoid(yn)
        ya = _pool_linear(pooled, pw_ref, pb_ref[...]) * ps_ref[...]
        ycat = jnp.concatenate([ya, yb], axis=1).astype(BF16)
        ycat_ref[...] = ycat
        xo_ref[...] = x_ref[...] + _dot(ycat, wout_v[...])

    return _call(
        body, name=name, grid=(t // tm,),
        in_specs=[_row_spec(tm, d), _row_spec(tm, hc), _prev_halo_spec(tm, hc), _const_spec(pool_w.shape), _const_spec((1, POOL_CH)),
                  _const_spec((1, POOL_CH)), _const_spec(conv_w.shape), _const_spec((1, CONV_CH)), _const_spec((1, CONV_CH)),
                  _const_spec((1, CONV_CH)), HBM_SPEC],
        out_specs=[_row_spec(tm, d), _row_spec(tm, d), _row_spec(tm, CONV_CH)],
        out_shape=[jax.ShapeDtypeStruct((t, d), F32), jax.ShapeDtypeStruct((t, d), BF16), jax.ShapeDtypeStruct((t, CONV_CH), F32)],
        scratch_shapes=[pltpu.VMEM((d, d), BF16), pltpu.SemaphoreType.DMA((N_CHIPS,))],
        args=(x, h0, h0, pool_w, pool_b, pool_scale, conv_w, conv_b, ln_g, ln_b, wout_g), comms=comms)


def _mix0_bwd_a(dy, h0, y_conv, pool_w, pool_b, pool_scale, conv_w, conv_b, ln_g, ln_b, wout_g, name):
    t, d = dy.shape
    tm = min(256, t)
    hc = h0.shape[1]
    n_small = 40

    def body(dy_ref, h_ref, hp_ref, y_ref, pw_ref, pb_ref, ps_ref, cw_ref, cb_ref, lg_ref, lb_ref, wout_hbm,
             dconv_ref, dpc_ref, dpw_ref, small_ref, wout_v, sem):
        _load_at_first_step(_out_weight_copies(wout_hbm, wout_v, sem))
        i = pl.program_id(0)

        @pl.when(i == 0)
        def _():
            dpw_ref[...] = jnp.zeros_like(dpw_ref)
            small_ref[...] = jnp.zeros_like(small_ref)

        g_w, y, pooled, cnt = _mix0_recompute(i, tm, h_ref[...], hp_ref[...], cw_ref[...], cb_ref[...], y_ref[...])
        yhat, rs = _ln_fwd(y)
        lg = lg_ref[...]
        yn = yhat * lg + lb_ref[...]
        mixed = _pool_linear(pooled, pw_ref, pb_ref[...])
        dycat = _dot_nt(dy_ref[...].astype(BF16), wout_v[...])
        dya, dyb = dycat[:, 0:POOL_CH], dycat[:, POOL_CH:]
        sg = _sigmoid(yn)
        dyn = dyb * (sg * (1.0 + yn * (1.0 - sg)))
        dyc = _ln_bwd(dyn * lg, yhat, rs)
        dconv_ref[...] = dyc

        def add_row(k, value):
            small_ref[k:k + 1, :] += jnp.sum(value, axis=0, keepdims=True)

        for k in range(CONV_WIDTH):
            add_row(k, dyc * _shift_down(g_w, CONV_WIDTH - 1 - k)[HALO:, :])
        add_row(32, dyc)
        add_row(33, dyn * yhat)
        add_row(34, dyn)
        scale = ps_ref[...]
        dmixed = dya * scale
        add_row(35, dya * mixed)
        add_row(36, dmixed)
        dmb = dmixed.astype(BF16)
        dpooled = []
        for k in range(len(POOL_WINDOWS)):
            sl = slice(k * POOL_GC, (k + 1) * POOL_GC)
            dpw_ref[k] += _dot_tn(pooled[:, sl].astype(BF16), dmb[:, sl])
            dpooled.append(_dot_nt(dmb[:, sl], pw_ref[k].astype(BF16)))
        dpc_ref[...] = jnp.concatenate(dpooled, axis=1) / cnt

    return _call(
        body, name=name, grid=(t // tm,),
        in_specs=[_row_spec(tm, d), _row_spec(tm, hc), _prev_halo_spec(tm, hc), _row_spec(tm, CONV_CH), _const_spec(pool_w.shape),
                  _const_spec((1, POOL_CH)), _const_spec((1, POOL_CH)), _const_spec(conv_w.shape), _const_spec((1, CONV_CH)),
                  _const_spec((1, CONV_CH)), _const_spec((1, CONV_CH)), HBM_SPEC],
        out_specs=[_row_spec(tm, CONV_CH), _row_spec(tm, POOL_CH), _const_spec(pool_w.shape), _const_spec((n_small, CONV_CH))],
        out_shape=[jax.ShapeDtypeStruct((t, CONV_CH), F32), jax.ShapeDtypeStruct((t, POOL_CH), F32),
                   jax.ShapeDtypeStruct(pool_w.shape, F32), jax.ShapeDtypeStruct((n_small, CONV_CH), F32)],
        scratch_shapes=[pltpu.VMEM((d, d), BF16), pltpu.SemaphoreType.DMA((N_CHIPS,))],
        args=(dy, h0, h0, y_conv, pool_w, pool_b, pool_scale, conv_w, conv_b, ln_g, ln_b, wout_g))


def _mix0_bwd_b(x, dy, h0, dconv, dpc, g, conv_w, win_g, name):
    t, d = x.shape
    tm = min(256, t)
    hc = h0.shape[1]
    c = win_g.shape[-1]
    n_tiles = t // tm

    def body(x_ref, dy_ref, h_ref, dc_ref, dcn_ref, dp_ref, dpn_ref, g_ref, cw_ref, win_hbm, dx_ref, dh_ref, dg_ref, win_v, sem):
        _load_at_first_step(_in_weight_copies(win_hbm, win_v, sem))
        i = pl.program_id(0)

        @pl.when(i == 0)
        def _():
            dg_ref[...] = jnp.zeros_like(dg_ref)

        not_last = i < n_tiles - 1
        dc_w = jnp.concatenate([dc_ref[...], jnp.where(not_last, dcn_ref[...], 0.0)], axis=0)
        dp_w = jnp.concatenate([dp_ref[...], jnp.where(not_last, dpn_ref[...], 0.0)], axis=0)
        cw = cw_ref[...]
        dg = jnp.zeros((tm, CONV_CH), F32)
        for k in range(CONV_WIDTH):
            dg = dg + cw[k:k + 1, :] * _shift_up(dc_w, CONV_WIDTH - 1 - k)[0:tm, :]
        a2 = dp_w + _shift_up(dp_w, 1)
        a4 = a2 + _shift_up(a2, 2)
        a8 = a4 + _shift_up(a4, 4)
        a16 = a8 + _shift_up(a8, 8)
        back = _group_select([a2[0:tm], a4[0:tm], a8[0:tm], a16[0:tm]])
        du = back - dp_ref[...] * _pool_counts(i, tm)
        hv = h_ref[...]
        a = hv[:, POOL_CH:POOL_CH + CONV_CH]
        sig = _sigmoid(hv[:, POOL_CH + CONV_CH:])
        dh = jnp.concatenate([du, dg * sig, dg * a * sig * (1.0 - sig)], axis=1).astype(BF16)
        dh_ref[...] = dh
        dxn = _proj_in_bwd_tail(dh, win_v, c)
        xh, r = _rms_fwd(x_ref[...])
        dg_ref[...] += jnp.sum(dxn * xh, axis=0, keepdims=True)
        dx_ref[...] = dy_ref[...] + _rms_bwd(dxn, xh, r, g_ref[...])

    return _call(
        body, name=name, grid=(n_tiles,),
        in_specs=[_row_spec(tm, d), _row_spec(tm, d), _row_spec(tm, hc), _row_spec(tm, CONV_CH), _next_halo_spec(tm, CONV_CH, t),
                  _row_spec(tm, POOL_CH), _next_halo_spec(tm, POOL_CH, t), _const_spec((1, d)), _const_spec(conv_w.shape), HBM_SPEC],
        out_specs=[_row_spec(tm, d), _row_spec(tm, hc), _const_spec((1, d))],
        out_shape=[jax.ShapeDtypeStruct((t, d), F32), jax.ShapeDtypeStruct((t, hc), BF16), jax.ShapeDtypeStruct((1, d), F32)],
        scratch_shapes=[pltpu.VMEM((N_CHIPS, d, c), BF16), pltpu.SemaphoreType.DMA((N_CHIPS,))],
        args=(x, dy, h0, dconv, dconv, dpc, dpc, g, conv_w, win_g))


SQRT_HALF = 0.7071067811865476
INV_SQRT_2PI = 0.3989422804014327


def _causal_mask():
    return (lax.broadcasted_iota(jnp.int32, (CHUNK, CHUNK), 1) <= lax.broadcasted_iota(jnp.int32, (CHUNK, CHUNK), 0)).astype(F32)


def _sgu_recompute(pre, lg, lb):
    half = pre.shape[1] // 2
    phi = 0.5 * (1.0 + lax.erf(pre * SQRT_HALF))
    z = pre * phi
    u, v = z[:, 0:half], z[:, half:]
    vhat, rs = _ln_fwd(v)
    return u, vhat, rs, vhat * lg + lb, phi


def _sgu_spatial(vln, w_ref, bt, tm):
    mask = _causal_mask()
    wm = [(w_ref[hd] * mask).astype(BF16) for hd in range(SGU_HEADS)]
    vb = vln.astype(BF16)
    rows = []
    for ch in range(tm // CHUNK):
        blocks = [_dot(wm[hd], vb[ch * CHUNK:(ch + 1) * CHUNK, hd * CHUNK:(hd + 1) * CHUNK]) + bt[:, hd:hd + 1] for hd in range(SGU_HEADS)]
        rows.append(jnp.concatenate(blocks, axis=1))
    return jnp.concatenate(rows, axis=0), wm


def _sgu_fwd(x, pre, ln_g, ln_b, w, bt, wout_g, name):
    t, d = x.shape
    tm = min(256, t)
    pc = pre.shape[1]

    def body(x_ref, pre_ref, lg_ref, lb_ref, w_ref, bt_ref, wout_hbm, xo_ref, p_ref, wout_v, sem):
        _load_at_first_step(_out_weight_copies(wout_hbm, wout_v, sem))
        u, _, _, vln, _ = _sgu_recompute(pre_ref[...], lg_ref[...], lb_ref[...])
        vo, _ = _sgu_spatial(vln, w_ref, bt_ref[...], tm)
        p = (u * vo).astype(BF16)
        p_ref[...] = p
        xo_ref[...] = x_ref[...] + _dot(p, wout_v[...])

    return _call(
        body, name=name, grid=(t // tm,),
        in_specs=[_row_spec(tm, d), _row_spec(tm, pc), _const_spec((1, d)), _const_spec((1, d)), _const_spec(w.shape),
                  _const_spec(bt.shape), HBM_SPEC],
        out_specs=[_row_spec(tm, d), _row_spec(tm, d)],
        out_shape=[jax.ShapeDtypeStruct((t, d), F32), jax.ShapeDtypeStruct((t, d), BF16)],
        scratch_shapes=[pltpu.VMEM((d, d), BF16), pltpu.SemaphoreType.DMA((N_CHIPS,))],
        args=(x, pre, ln_g, ln_b, w, bt, wout_g))


def _sgu_bwd(x, dy, pre, g, ln_g, ln_b, w, bt, win_g, wout_g, name, comms=()):
    t, d = x.shape
    tm = min(256, t)
    pc = pre.shape[1]
    c = win_g.shape[-1]

    def body(x_ref, dy_ref, pre_ref, g_ref, lg_ref, lb_ref, w_ref, bt_ref, win_hbm, wout_hbm,
             dx_ref, dpre_ref, dg_ref, dlg_ref, dlb_ref, dw_ref, dbt_ref, win_v, wout_v, sem):
        _load_at_first_step(_in_weight_copies(win_hbm, win_v, sem) + _out_weight_copies(wout_hbm, wout_v, sem, N_CHIPS))
        i = pl.program_id(0)

        @pl.when(i == 0)
        def _():
            for ref in (dg_ref, dlg_ref, dlb_ref, dw_ref, dbt_ref):
                ref[...] = jnp.zeros_like(ref)

        prev = pre_ref[...]
        lg = lg_ref[...]
        u, vhat, rs, vln, phi = _sgu_recompute(prev, lg, lb_ref[...])
        vo, wm = _sgu_spatial(vln, w_ref, bt_ref[...], tm)
        dp = _dot_nt(dy_ref[...].astype(BF16), wout_v[...])
        du = dp * vo
        dvo = dp * u
        dvob = dvo.astype(BF16)
        vb = vln.astype(BF16)
        head_lane = lax.broadcasted_iota(jnp.int32, (CHUNK, SGU_HEADS), 1)
        dbt = jnp.zeros((CHUNK, SGU_HEADS), F32)
        dw = [jnp.zeros((CHUNK, CHUNK), F32) for _ in range(SGU_HEADS)]
        rows = []
        for ch in range(tm // CHUNK):
            rs_ = slice(ch * CHUNK, (ch + 1) * CHUNK)
            blocks = []
            for hd in range(SGU_HEADS):
                cs = slice(hd * CHUNK, (hd + 1) * CHUNK)
                dbt = dbt + jnp.where(head_lane == hd, jnp.sum(dvo[rs_, cs], axis=1, keepdims=True), 0.0)
                dw[hd] = dw[hd] + _dot_nt(dvob[rs_, cs], vb[rs_, cs])
                blocks.append(_dot_tn(wm[hd], dvob[rs_, cs]))
            rows.append(jnp.concatenate(blocks, axis=1))
        dvln = jnp.concatenate(rows, axis=0)
        mask = _causal_mask()
        for hd in range(SGU_HEADS):
            dw_ref[hd] += dw[hd] * mask
        dbt_ref[...] += dbt
        dlg_ref[...] += jnp.sum(dvln * vhat, axis=0, keepdims=True)
        dlb_ref[...] += jnp.sum(dvln, axis=0, keepdims=True)
        dv = _ln_bwd(dvln * lg, vhat, rs)
        gelu_grad = phi + prev * jnp.exp(-0.5 * prev * prev) * INV_SQRT_2PI
        dpre = (jnp.concatenate([du, dv], axis=1) * gelu_grad).astype(BF16)
        dpre_ref[...] = dpre
        dxn = _proj_in_bwd_tail(dpre, win_v, c)
        xh, r = _rms_fwd(x_ref[...])
        dg_ref[...] += jnp.sum(dxn * xh, axis=0, keepdims=True)
        dx_ref[...] = dy_ref[...] + _rms_bwd(dxn, xh, r, g_ref[...])

    return _call(
        body, name=name, grid=(t // tm,),
        in_specs=[_row_spec(tm, d), _row_spec(tm, d), _row_spec(tm, pc), _const_spec((1, d)), _const_spec((1, d)), _const_spec((1, d)),
                  _const_spec(w.shape), _const_spec(bt.shape), HBM_SPEC, HBM_SPEC],
        out_specs=[_row_spec(tm, d), _row_spec(tm, pc), _const_spec((1, d)), _const_spec((1, d)), _const_spec((1, d)),
                   _const_spec(w.shape), _const_spec(bt.shape)],
        out_shape=[jax.ShapeDtypeStruct((t, d), F32), jax.ShapeDtypeStruct((t, pc), BF16), jax.ShapeDtypeStruct((1, d), F32),
                   jax.ShapeDtypeStruct((1, d), F32), jax.ShapeDtypeStruct((1, d), F32), jax.ShapeDtypeStruct(w.shape, F32),
                   jax.ShapeDtypeStruct(bt.shape, F32)],
        scratch_shapes=[pltpu.VMEM((N_CHIPS, d, c), BF16), pltpu.VMEM((d, d), BF16), pltpu.SemaphoreType.DMA((2 * N_CHIPS,))],
        args=(x, dy, pre, g, ln_g, ln_b, w, bt, win_g, wout_g), comms=comms)


def _final_loss(x, tgt, g, name):
    t, d = x.shape
    tm = min(512, t)

    def body(x_ref, t_ref, g_ref, dx_ref, loss_ref, dg_ref):
        @pl.when(pl.program_id(0) == 0)
        def _():
            loss_ref[...] = jnp.zeros_like(loss_ref)
            dg_ref[...] = jnp.zeros_like(dg_ref)

        gv = g_ref[...]
        xh, r = _rms_fwd(x_ref[...])
        diff = xh * gv - t_ref[...]
        loss_ref[...] += 0.5 * jnp.sum(jnp.sum(diff * diff, axis=1, keepdims=True), axis=0, keepdims=True) / d
        dout = diff / d
        dg_ref[...] += jnp.sum(dout * xh, axis=0, keepdims=True)
        dx_ref[...] = _rms_bwd(dout, xh, r, gv)

    return _call(
        body, name=name, grid=(t // tm,),
        in_specs=[_row_spec(tm, d), _row_spec(tm, d), _const_spec((1, d))],
        out_specs=[_row_spec(tm, d), _const_spec((1, 1)), _const_spec((1, d))],
        out_shape=[jax.ShapeDtypeStruct((t, d), F32), jax.ShapeDtypeStruct((1, 1), F32), jax.ShapeDtypeStruct((1, d), F32)],
        args=(x, tgt, g))


def _tn_matmul(a, b, scale, bm, bn, name, comms=()):
    t, m = a.shape
    n = b.shape[1]
    tk = min(2048, t)
    bm, bn = min(bm, m), min(bn, n)
    nk = t // tk

    def body(a_ref, b_ref, o_ref, acc_ref):
        k = pl.program_id(2)

        @pl.when(k == 0)
        def _():
            acc_ref[...] = jnp.zeros_like(acc_ref)

        bv = b_ref[...]
        if bv.dtype != BF16:
            bv = (scale * bv).astype(BF16)
        acc_ref[...] += _dot_tn(a_ref[...], bv)

        @pl.when(k == nk - 1)
        def _():
            o_ref[...] = acc_ref[...].astype(BF16)

    return _call(
        body, name=name, grid=(m // bm, n // bn, nk),
        in_specs=[pl.BlockSpec((tk, bm), lambda i, j, k: (k, i)), pl.BlockSpec((tk, bn), lambda i, j, k: (k, j))],
        out_specs=[pl.BlockSpec((bm, bn), lambda i, j, k: (i, j))],
        out_shape=[jax.ShapeDtypeStruct((m, n), BF16)],
        scratch_shapes=[pltpu.VMEM((bm, bn), F32)],
        args=(a, b), comms=comms)[0]


def _row_tile(rows, cols, budget_bytes=2 * 1024 * 1024):
    best = None
    for cand in range(16, rows + 1, 16):
        if rows % cand == 0 and cand * cols * 4 <= budget_bytes:
            best = cand
    return best or rows


def _scalar_grid(grid, in_specs, out_specs):
    return pltpu.PrefetchScalarGridSpec(num_scalar_prefetch=1, grid=grid, in_specs=in_specs, out_specs=out_specs)


def _cast_into_slot(w, layer, me, name):
    _, rows, cols = w.shape
    tr = _row_tile(rows, cols)

    def body(me_ref, w_ref, o_ref):
        o_ref[...] = w_ref[...].astype(BF16)

    return pl.pallas_call(
        body, name=name,
        grid_spec=_scalar_grid((rows // tr,), [pl.BlockSpec((None, tr, cols), lambda i, me: (layer, i, 0))],
                               pl.BlockSpec((None, tr, cols), lambda i, me: (me[0], i, 0))),
        out_shape=jax.ShapeDtypeStruct((N_CHIPS, rows, cols), BF16), compiler_params=_cparams())(me, w)


def _add_half(view, other, core, name):
    q, _, r, c = view.shape
    tr = _row_tile(r, c)

    def body(core_ref, a_ref, b_ref, o_ref):
        o_ref[...] = (a_ref[...].astype(F32) + b_ref[...].astype(F32)).astype(BF16)

    return pl.pallas_call(
        body, name=name,
        grid_spec=_scalar_grid((q, r // tr), [pl.BlockSpec((None, None, tr, c), lambda k, i, core: (k, core[0], i, 0)),
                                             pl.BlockSpec((None, tr, c), lambda k, i, core: (k, i, 0))],
                               pl.BlockSpec((None, tr, c), lambda k, i, core: (k, i, 0))),
        out_shape=jax.ShapeDtypeStruct((q, r, c), BF16), compiler_params=_cparams_nd(2))(core, view, other)


def _reduce_piece(partial, staged, me, column_sharded, name):
    _, r, c = staged.shape
    tr = _row_tile(r, c, budget_bytes=1024 * 1024)
    nt = r // tr
    if column_sharded:
        own2d = partial.reshape(r, N_CHIPS * c)
        own_spec = pl.BlockSpec((tr, c), lambda i, me: (i, me[0]))
    else:
        own2d = partial.reshape(N_CHIPS * r, c)
        own_spec = pl.BlockSpec((tr, c), lambda i, me: (me[0] * nt + i, 0))
    ring = [pl.BlockSpec((None, tr, c), lambda i, me, k=k: ((me[0] + k) % N_CHIPS, i, 0)) for k in (1, 2, 3)]

    def body(me_ref, own_ref, s1_ref, s2_ref, s3_ref, o_ref):
        o_ref[...] = ((own_ref[...].astype(F32) + s1_ref[...].astype(F32)) + s2_ref[...].astype(F32)) + s3_ref[...].astype(F32)

    return pl.pallas_call(
        body, name=name, grid_spec=_scalar_grid((nt,), [own_spec] + ring, pl.BlockSpec((tr, c), lambda i, me: (i, 0))),
        out_shape=jax.ShapeDtypeStruct((r, c), F32), compiler_params=_cparams())(me, own2d, staged, staged, staged)


def _sum_leading(s, name):
    n, rows, cols = s.shape
    tr = _row_tile(rows, cols, budget_bytes=1024 * 1024)

    def body(s_ref, o_ref):
        acc = s_ref[0].astype(F32)
        for k in range(1, n):
            acc = acc + s_ref[k].astype(F32)
        o_ref[...] = acc

    return pl.pallas_call(
        body, name=name, grid=(rows // tr,), in_specs=[pl.BlockSpec((n, tr, cols), lambda i: (0, i, 0))], out_specs=_row_spec(tr, cols),
        out_shape=jax.ShapeDtypeStruct((rows, cols), F32), compiler_params=_cparams())(s)


ADAM_C1 = 1.0 / (1.0 - ADAM_B1 ** ADAM_STEP)
ADAM_C2 = 1.0 / (1.0 - ADAM_B2 ** ADAM_STEP)


def _adamw_math(w, g, m, v):
    mn = ADAM_B1 * m + (1.0 - ADAM_B1) * g
    vn = ADAM_B2 * v + (1.0 - ADAM_B2) * (g * g)
    return -ADAM_LR * ((mn * ADAM_C1) / (jnp.sqrt(vn * ADAM_C2) + ADAM_EPS) + ADAM_WD * w), mn, vn


def _adamw(w, g, m, v, name):
    shape = w.shape
    cols = shape[-1] if w.ndim > 1 else 128
    w2, g2, m2, v2 = (a.reshape(-1, cols) for a in (w, g, m, v))
    rows = w2.shape[0]
    tr = _row_tile(rows, cols, budget_bytes=1024 * 1024)

    def body(w_ref, g_ref, m_ref, v_ref, d_ref, mo_ref, vo_ref):
        d_ref[...], mo_ref[...], vo_ref[...] = _adamw_math(w_ref[...], g_ref[...], m_ref[...], v_ref[...])

    spec = _row_spec(tr, cols)
    outs = pl.pallas_call(
        body, name=name, grid=(rows // tr,), in_specs=[spec] * 4, out_specs=[spec] * 3,
        out_shape=[jax.ShapeDtypeStruct((rows, cols), F32)] * 3, compiler_params=_cparams())(w2, g2, m2, v2)
    return tuple(o.reshape(shape) for o in outs)


def _adamw_sharded(w, g_mine, g_sibling, m, v, core, layer, prev, name, comms=(), after=()):
    n_layers, r, c = w.shape
    half = r // 2
    tr = _row_tile(half, c)
    nt = half // tr

    def body(core_ref, w_ref, gm_ref, gs_ref, m_ref, v_ref, *rest):
        g_ref, d_ref, mo_ref, vo_ref = rest[-4:]
        gv = jnp.where(pl.program_id(0) == core_ref[0], gm_ref[...], gs_ref[...])
        g_ref[...] = gv
        d_ref[...], mo_ref[...], vo_ref[...] = _adamw_math(w_ref[...], gv, m_ref[...], v_ref[...])

    full = pl.BlockSpec((None, tr, c), lambda h, i, core: (layer, h * nt + i, 0))
    part = pl.BlockSpec((tr, c), lambda h, i, core: (i, 0))
    args = [w, g_mine, g_sibling, m, v]
    in_specs = [full, part, part, full, full]
    aliases = {}
    if prev is not None:
        aliases = {len(args) + k: k for k in range(4)}
        args += list(prev)
        in_specs += [pl.BlockSpec(memory_space=pl.ANY)] * 4
    args += list(after)
    in_specs += [pl.BlockSpec(memory_space=pl.ANY)] * len(after)
    return _call(body, name=name, grid=(2, nt), in_specs=in_specs, out_specs=[full] * 4, out_shape=[jax.ShapeDtypeStruct(w.shape, F32)] * 4,
                 args=args, comms=comms, scalar=core, aliases=aliases)


BIG_IN = ("ffn1_w_in", "ffn2_w_in", "ab_w_in", "sgu_w_in")
BIG_OUT = ("ffn1_w_out", "ffn2_w_out", "ab_w_out", "sgu_w_out")
BIG = BIG_IN + BIG_OUT


class _Gatherer:
    def __init__(self, slots):
        self.slots = dict(slots)

    def _stage(self, keys, d2d):
        n = len(keys)

        def plan(ins, outs, place):
            x, y, c = place
            me = 2 * x + y
            remote = []
            for a in range(n):
                rows = ins[a].shape[1] // 2

                def half(ref, q, core, rows=rows):
                    return ref.at[q, pl.ds(core * rows, rows), :]

                for (px, py) in _other_chips(x, y):
                    q = 2 * px + py
                    if d2d:
                        remote.append((half(ins[a], q, c), half(outs[a], q, c), (x, y, 1 - c), half(outs[a], q, 1 - c)))
                    else:
                        remote.append((half(ins[a], me, c), half(outs[a], me, c), (px, py, c), half(outs[a], q, c)))
            return remote

        def finish(outs):
            for k, o in zip(keys, outs):
                self.slots[k] = o

        arrays = [self.slots[k] for k in keys]
        return _Exchange(arrays, [_sds(a) for a in arrays], plan, 3 * n, {a: a for a in range(n)}, finish)

    def direct(self, keys):
        n = len(keys)

        def plan(ins, outs, place):
            x, y, c = place
            me = 2 * x + y
            return [(ins[a].at[me], outs[a].at[me], (px, py, c), outs[a].at[2 * px + py]) for a in range(n) for (px, py) in _other_chips(x, y)]

        def finish(outs):
            for k, o in zip(keys, outs):
                self.slots[k] = o

        arrays = [self.slots[k] for k in keys]
        return _Exchange(arrays, [_sds(a) for a in arrays], plan, 3 * n, {a: a for a in range(n)}, finish)

    def ici(self, keys):
        return self._stage(keys, False)

    def d2d(self, keys):
        return self._stage(keys, True)


class _Reducer:
    def __init__(self, me, core):
        self.me, self.core = me, core
        self.views, self.partial, self.staged, self.mine, self.theirs = {}, {}, {}, {}, {}

    def add(self, key, g):
        m, n = g.shape
        if key[0] in BIG_IN:
            self.views[key] = g.reshape(1, 2, m // 2, n)
        else:
            self.views[key] = g.reshape(N_CHIPS, 2, m // (2 * N_CHIPS), n)

    def swap(self, keys):
        views = [self.views[k] for k in keys]

        def plan(ins, outs, place):
            x, y, c = place
            return [(ins[a].at[:, 1 - c], outs[a], (x, y, 1 - c), outs[a]) for a in range(len(keys))]

        def finish(outs):
            for k, v, o in zip(keys, views, outs):
                self.partial[k] = _add_half(v, o, self.core, f"chip_partial_{k[0]}_{k[1]}")

        shapes = [jax.ShapeDtypeStruct((v.shape[0],) + v.shape[2:], v.dtype) for v in views]
        return _Exchange(views, shapes, plan, len(keys), None, finish)

    def scatter(self, keys, part=(0, 1)):
        i, n = part
        n_keys = len(keys)
        parts = [self.partial[k] for k in keys]
        shapes = []
        for k, p in zip(keys, parts):
            q, r, c = p.shape
            shapes.append(jax.ShapeDtypeStruct((N_CHIPS, r, c // N_CHIPS if k[0] in BIG_IN else c), p.dtype))

        def piece(ref, key, q, rows, cols):
            return ref.at[0, rows, pl.ds(q * cols, cols)] if key[0] in BIG_IN else ref.at[q, rows, :]

        def plan(ins, outs, place):
            x, y, c = place
            me = 2 * x + y
            remote = []
            for a, k in enumerate(keys):
                _, r, cols = shapes[a].shape
                rows = pl.ds(i * (r // n), r // n)
                for (px, py) in _other_chips(x, y):
                    q = 2 * px + py
                    remote.append((piece(ins[a], k, q, rows, cols), outs[a].at[me, rows, :], (px, py, c), outs[a].at[q, rows, :]))
            return remote

        def finish(outs):
            for k, p, o in zip(keys, parts, outs):
                self.staged[k] = o
                if i == n - 1:
                    self.mine[k] = _reduce_piece(p, o, self.me, k[0] in BIG_IN, f"reduce_{k[0]}_{k[1]}")

        inputs, aliases = parts, None
        if i > 0:
            inputs = parts + [self.staged[k] for k in keys]
            aliases = {n_keys + a: a for a in range(n_keys)}
        return _Exchange(inputs, shapes, plan, 3 * n_keys, aliases, finish)

    def scatter_behind(self, keys, work):
        n = len(keys)
        parts = [self.partial[k] for k in keys]
        lands = []
        for k, p in zip(keys, parts):
            _, r, c = p.shape
            lands.append(jax.ShapeDtypeStruct((N_CHIPS, r, c // N_CHIPS if k[0] in BIG_IN else c), p.dtype))
        sem_spec = pl.BlockSpec(memory_space=pltpu.SEMAPHORE)
        effect = pltpu.CompilerParams(has_side_effects=pltpu.SideEffectType.DATAFLOW_SIDE_EFFECTING)

        def copies(part_refs, land_refs, send_sems, recv_sems):
            x, y, c = _my_place()
            me = 2 * x + y
            out = []
            for a, k in enumerate(keys):
                cols = lands[a].shape[2]
                for j, (px, py) in enumerate(_other_chips(x, y)):
                    q = 2 * px + py
                    src = part_refs[a].at[0, :, pl.ds(q * cols, cols)] if k[0] in BIG_IN else part_refs[a].at[q]
                    sems = dict(send_sem=send_sems.at[3 * a + j], recv_sem=recv_sems.at[3 * a + j], device_id=(px, py, c), device_id_type=MESH)
                    out.append((pltpu.make_async_remote_copy(src_ref=src, dst_ref=land_refs[a].at[me], **sems),
                                pltpu.make_async_remote_copy(src_ref=src, dst_ref=land_refs[a].at[q], **sems)))
            return out

        def start_body(*refs):
            part_refs, land_refs, send_sems, recv_sems, token = refs[:n], refs[n:2 * n], refs[2 * n], refs[2 * n + 1], refs[-1]
            for send, _ in copies(part_refs, land_refs, send_sems, recv_sems):
                send.start()
            token[...] = jnp.zeros_like(token)

        def wait_body(*refs):
            part_refs, land_refs, send_sems, recv_sems = refs[:n], refs[n:2 * n], refs[2 * n], refs[2 * n + 1]
            for send, arrive in copies(part_refs, land_refs, send_sems, recv_sems):
                send.wait_send()
                arrive.wait_recv()

        in_hbm = [pltpu.with_memory_space_constraint(p, pltpu.HBM) for p in parts]
        in_hbm += [pltpu.with_memory_space_constraint(lax.empty(s.shape, s.dtype), pltpu.HBM) for s in lands]
        thru_shapes = [pltpu.HBM(p.shape, p.dtype) for p in parts] + [pltpu.HBM(s.shape, s.dtype) for s in lands]
        started = pl.pallas_call(
            start_body, name="scatter_last_start", in_specs=[HBM_SPEC] * (2 * n),
            out_shape=(pltpu.SemaphoreType.DMA((3 * n,)), pltpu.SemaphoreType.DMA((3 * n,)), *thru_shapes, jax.ShapeDtypeStruct((8, 128), F32)),
            out_specs=(sem_spec, sem_spec, *[HBM_SPEC] * (2 * n), pl.BlockSpec(memory_space=pltpu.VMEM)),
            input_output_aliases={i: 2 + i for i in range(2 * n)}, compiler_params=effect)(*in_hbm)
        send_sems, recv_sems, thru, token = started[0], started[1], started[2:2 + 2 * n], started[-1]
        after = work(token)
        done = pl.pallas_call(
            wait_body, name="scatter_last_wait", in_specs=[HBM_SPEC] * (2 * n) + [sem_spec, sem_spec] + [pl.BlockSpec(memory_space=pl.ANY)] * len(after),
            out_shape=tuple(thru_shapes), out_specs=tuple([HBM_SPEC] * (2 * n)), input_output_aliases={i: i for i in range(2 * n)},
            compiler_params=effect)(*thru, send_sems, recv_sems, *after)
        for a, k in enumerate(keys):
            self.staged[k] = done[n + a]
            self.mine[k] = _reduce_piece(done[a], done[n + a], self.me, k[0] in BIG_IN, f"reduce_{k[0]}_{k[1]}")

    def exchange(self, keys):
        mine = [self.mine[k] for k in keys]

        def plan(ins, outs, place):
            x, y, c = place
            return [(ins[a], outs[a], (x, y, 1 - c), outs[a]) for a in range(len(keys))]

        def finish(outs):
            for k, o in zip(keys, outs):
                self.theirs[k] = o

        return _Exchange(mine, [_sds(a) for a in mine], plan, len(keys), None, finish)


def _all_gather_full(gat, keys):
    n = len(keys)
    arrays = [gat.slots[k] for k in keys]
    per = 7

    def body(*refs):
        ins, outs = refs[:n], refs[n:2 * n]
        send_sems, recv_sems = refs[2 * n:]
        x, y, c = _my_place()
        sibling, x_nbr, y_nbr = (x, y, 1 - c), (1 - x, y, c), (x, 1 - y, c)
        me, qx, qy, qd = 2 * x + y, 2 * (1 - x) + y, 2 * x + (1 - y), 2 * (1 - x) + (1 - y)

        def half(ref, q, core):
            rows = ref.shape[1] // 2
            return ref.at[q, pl.ds(core * rows, rows), :]

        def quarter(ref, q, core, k):
            rows = ref.shape[1] // 4
            return ref.at[q, pl.ds((2 * core + k) * rows, rows), :]

        def copy(a, k, src, dst, to):
            return pltpu.make_async_remote_copy(src_ref=src, dst_ref=dst, send_sem=send_sems.at[per * a + k],
                                                recv_sem=recv_sems.at[per * a + k], device_id=to, device_id_type=MESH)

        sent = []

        def send(a, k, part, to):
            cp = copy(a, k, part, part, to)
            cp.start()
            sent.append(cp)

        def landed(a, k, part):
            copy(a, k, part, part, sibling).wait_recv()

        for a in range(n):
            mine_in, mine_out = half(ins[a], me, c), half(outs[a], me, c)
            for k, to in ((0, x_nbr), (1, y_nbr)):
                cp = copy(a, k, mine_in, mine_out, to)
                cp.start()
                sent.append(cp)
        for a in range(n):
            landed(a, 1, half(outs[a], qy, c))
            send(a, 2, quarter(outs[a], qy, c, 0), x_nbr)
            send(a, 5, half(outs[a], qy, c), sibling)
            landed(a, 0, half(outs[a], qx, c))
            send(a, 3, quarter(outs[a], qx, c, 1), y_nbr)
            send(a, 4, half(outs[a], qx, c), sibling)
        for a in range(n):
            landed(a, 2, quarter(outs[a], qd, c, 0))
            landed(a, 3, quarter(outs[a], qd, c, 1))
            send(a, 6, half(outs[a], qd, c), sibling)
        for a in range(n):
            for k, q in ((4, qx), (5, qy), (6, qd)):
                landed(a, k, half(outs[a], q, 1 - c))
        for cp in sent:
            cp.wait_send()

    outs = pl.pallas_call(
        body, name="all_gather_first_weights", in_specs=[HBM_SPEC] * n, out_specs=[HBM_SPEC] * n,
        out_shape=[_sds(a) for a in arrays], input_output_aliases={a: a for a in range(n)},
        scratch_shapes=[pltpu.SemaphoreType.DMA((per * n,)), pltpu.SemaphoreType.DMA((per * n,))])(*arrays)
    for k, o in zip(keys, outs):
        gat.slots[k] = o


def _small_all_gather(buf, done):
    state = {}

    def index(x, y, c):
        return 4 * x + 2 * y + c

    def plan_ici(ins, outs, place):
        x, y, c = place
        return [(ins[0], outs[0].at[index(x, y, c)], (px, py, c), outs[0].at[index(px, py, c)]) for (px, py) in _other_chips(x, y)]

    def local(ins, outs, place):
        return [(ins[0], outs[0].at[index(*place)])]

    def plan_d2d(ins, outs, place):
        x, y, c = place
        return [(ins[0].at[index(px, py, c)], outs[0].at[index(px, py, c)], (x, y, 1 - c), outs[0].at[index(px, py, 1 - c)])
                for (px, py) in [(x, y)] + _other_chips(x, y)]

    def second():
        return _Exchange([state["blocks"]], [_sds(state["blocks"])], plan_d2d, N_CHIPS, {0: 0}, lambda outs: done(outs[0]))

    first = _Exchange([buf], [jax.ShapeDtypeStruct((2 * N_CHIPS,) + buf.shape, buf.dtype)], plan_ici, 3, None,
                      lambda outs: state.update(blocks=outs[0]), local, 1)
    return first, second


WEIGHT_NAMES = ("ffn1_norm", "ffn1_w_in", "ffn1_w_out", "mix_norm", "ffn2_norm", "ffn2_w_in", "ffn2_w_out", "ab_w_in", "pool_w", "pool_b",
                "pool_scale", "conv_w", "conv_b", "conv_ln_g", "conv_ln_b", "ab_w_out", "sgu_w_in", "sgu_ln_g", "sgu_ln_b", "sgu_w", "sgu_b",
                "sgu_w_out", "final_norm")
SMALL = tuple(n for n in WEIGHT_NAMES if n not in BIG)
SHARDED_SMALL = ("conv_w", "sgu_ln_g", "sgu_ln_b")
PACK_ROWS = 64
PACK = ("pack", 0)


def _pair(prefix, layer):
    return [(prefix + "_w_in", layer), (prefix + "_w_out", layer)]


def kernel(x, ffn1_norm, ffn1_w_in, ffn1_w_out, mix_norm, ffn2_norm, ffn2_w_in, ffn2_w_out, ab_w_in, pool_w, pool_b, pool_scale, conv_w, conv_b, conv_ln_g, conv_ln_b, ab_w_out, sgu_w_in, sgu_ln_g, sgu_ln_b, sgu_w, sgu_b, sgu_w_out, final_norm, loss_target, m_ffn1_norm, m_ffn1_w_in, m_ffn1_w_out, m_mix_norm, m_ffn2_norm, m_ffn2_w_in, m_ffn2_w_out, m_ab_w_in, m_pool_w, m_pool_b, m_pool_scale, m_conv_w, m_conv_b, m_conv_ln_g, m_conv_ln_b, m_ab_w_out, m_sgu_w_in, m_sgu_ln_g, m_sgu_ln_b, m_sgu_w, m_sgu_b, m_sgu_w_out, m_final_norm, v_ffn1_norm, v_ffn1_w_in, v_ffn1_w_out, v_mix_norm, v_ffn2_norm, v_ffn2_w_in, v_ffn2_w_out, v_ab_w_in, v_pool_w, v_pool_b, v_pool_scale, v_conv_w, v_conv_b, v_conv_ln_g, v_conv_ln_b, v_ab_w_out, v_sgu_w_in, v_sgu_ln_g, v_sgu_ln_b, v_sgu_w, v_sgu_b, v_sgu_w_out, v_final_norm):
    given = dict(locals())
    w = {n: given[n] for n in WEIGHT_NAMES}
    chip = 2 * lax.axis_index("x") + lax.axis_index("y")
    me = chip.astype(jnp.int32).reshape(1)
    core = lax.axis_index("c").astype(jnp.int32).reshape(1)
    row = lambda v: v.reshape(1, -1)
    xin, tgt = x[0], loss_target[0]

    pack = jnp.concatenate([
        w["conv_w"][0], jnp.zeros((1, 128), F32), w["sgu_ln_g"].reshape(2, 128), w["sgu_ln_b"].reshape(2, 128),
        jnp.zeros((PACK_ROWS - 36, 128), F32)], axis=0)
    slots = {PACK: lax.dynamic_update_slice(jnp.zeros((N_CHIPS, PACK_ROWS, 128), F32), pack[None], (me[0], 0, 0))}
    for n in BIG:
        for layer in range(w[n].shape[0]):
            slots[(n, layer)] = _cast_into_slot(w[n], layer, me, f"cast_{n}_{layer}")
    gat = _Gatherer(slots)
    _all_gather_full(gat, _pair("ffn1", 0) + [PACK])
    gp = gat.slots[PACK]
    conv_w_full = jnp.transpose(gp[:, 0:CONV_WIDTH], (1, 0, 2)).reshape(CONV_WIDTH, N_CHIPS * 128)
    sgu_ln_g_full = gp[:, 32:34].reshape(1, -1)
    sgu_ln_b_full = gp[:, 34:36].reshape(1, -1)
    gw = lambda n, layer: gat.slots[(n, layer)]

    st = [dict(), dict()]
    st[0]["xa"] = xin
    later = _pair("sgu", 0) + _pair("ffn2", 1)
    cur, st[0]["h1"], st[0]["xn1"] = _ffn_fwd(xin, row(w["ffn1_norm"][0]), gw("ffn1_w_in", 0), gw("ffn1_w_out", 0), "ffn1_fwd_0",
                                              comms=[gat.direct(_pair("ab", 0)), gat.ici(_pair("ffn2", 0))])
    st[0]["xb"] = cur
    st[0]["h0"], st[0]["xnm"] = _norm_matmul(cur, row(w["mix_norm"][0]), gw("ab_w_in", 0), "mix0_proj_in",
                                             comms=[gat.d2d(_pair("ffn2", 0)), gat.ici([("ffn1_w_out", 1)])])
    pool_args = (w["pool_w"][0], row(w["pool_b"][0]), row(w["pool_scale"][0]), conv_w_full, row(w["conv_b"][0]), row(w["conv_ln_g"][0]),
                 row(w["conv_ln_b"][0]), gw("ab_w_out", 0))
    cur, st[0]["ycat"], st[0]["yconv"] = _mix0_fwd(cur, st[0]["h0"], *pool_args, "mix0_fwd", comms=[gat.ici([("ffn1_w_in", 1)])])
    st[0]["xc"] = cur
    cur, st[0]["h2"], st[0]["xn2"] = _ffn_fwd(cur, row(w["ffn2_norm"][0]), gw("ffn2_w_in", 0), gw("ffn2_w_out", 0), "ffn2_fwd_0",
                                              comms=[gat.d2d(_pair("ffn1", 1)), gat.ici(later)])
    st[1]["xa"] = cur
    cur, st[1]["h1"], st[1]["xn1"] = _ffn_fwd(cur, row(w["ffn1_norm"][1]), gw("ffn1_w_in", 1), gw("ffn1_w_out", 1), "ffn1_fwd_1",
                                              comms=[gat.d2d(later)])
    st[1]["xb"] = cur
    st[1]["pre"], st[1]["xnm"] = _norm_matmul(cur, row(w["mix_norm"][1]), gw("sgu_w_in", 0), "sgu_proj_in")
    sgu_args = (sgu_ln_g_full, sgu_ln_b_full, w["sgu_w"][0], w["sgu_b"][0].T)
    cur, st[1]["p"] = _sgu_fwd(cur, st[1]["pre"], *sgu_args, gw("sgu_w_out", 0), "sgu_fwd")
    st[1]["xc"] = cur
    cur, st[1]["h2"], st[1]["xn2"] = _ffn_fwd(cur, row(w["ffn2_norm"][1]), gw("ffn2_w_in", 1), gw("ffn2_w_out", 1), "ffn2_fwd_1")
    dy, loss, d_final = _final_loss(cur, tgt, row(w["final_norm"]), "final_loss")

    red = _Reducer(me, core)
    small = {"final_norm": d_final.reshape(-1)}
    norm_grads = {"ffn1_norm": [None] * DEPTH, "mix_norm": [None] * DEPTH, "ffn2_norm": [None] * DEPTH}
    ga, gb, gc, gd, ge, gf = _pair("ffn2", 1), _pair("sgu", 0), _pair("ffn1", 1), _pair("ffn2", 0), _pair("ab", 0), _pair("ffn1", 0)

    def ffn_backward(prefix, layer, xs, hs, xns, dy_in, bwd_comms=(), dwin_comms=(), dwout_comms=()):
        dx, dh, act, norm_grads[prefix + "_norm"][layer] = _ffn_bwd(
            xs, dy_in, hs, row(w[prefix + "_norm"][layer]), gw(prefix + "_w_in", layer), gw(prefix + "_w_out", layer),
            f"{prefix}_bwd_{layer}", comms=bwd_comms)
        red.add((prefix + "_w_in", layer), _tn_matmul(xns, dh, 1.0, 1024, 1408, f"{prefix}_dwin_{layer}", comms=dwin_comms))
        red.add((prefix + "_w_out", layer), _tn_matmul(act, dy_in, 0.5, 1408, 1024, f"{prefix}_dwout_{layer}", comms=dwout_comms))
        return dx

    s1, s0 = st[1], st[0]
    dy = ffn_backward("ffn2", 1, s1["xc"], s1["h2"], s1["xn2"], dy)
    dy_in = dy
    dy, dpre, norm_grads["mix_norm"][1], dlg, dlb, dw, dbt = _sgu_bwd(
        s1["xb"], dy_in, s1["pre"], row(w["mix_norm"][1]), *sgu_args, gw("sgu_w_in", 0), gw("sgu_w_out", 0), "sgu_bwd", comms=[red.swap(ga)])
    red.add(("sgu_w_in", 0), _tn_matmul(s1["xnm"], dpre, 1.0, 1024, 2048, "sgu_dwin"))
    red.add(("sgu_w_out", 0), _tn_matmul(s1["p"], dy_in, 1.0, 1024, 1024, "sgu_dwout"))
    small.update(sgu_ln_g=dlg, sgu_ln_b=dlb, sgu_w=dw[None], sgu_b=dbt.T[None])
    dy = ffn_backward("ffn1", 1, s1["xa"], s1["h1"], s1["xn1"], dy, bwd_comms=[lambda: red.scatter(ga), lambda: red.swap(gb)],
                      dwin_comms=[lambda: red.scatter(gb), lambda: red.exchange(ga)])
    dy = ffn_backward("ffn2", 0, s0["xc"], s0["h2"], s0["xn2"], dy, bwd_comms=[lambda: red.swap(gc), lambda: red.exchange(gb)],
                      dwin_comms=[lambda: red.scatter(gc)])
    dy_in = dy
    dconv, dpc, dpw, rows = _mix0_bwd_a(dy_in, s0["h0"], s0["yconv"], *pool_args, "mix0_bwd_a")
    dy, dh0, norm_grads["mix_norm"][0] = _mix0_bwd_b(s0["xb"], dy_in, s0["h0"], dconv, dpc, row(w["mix_norm"][0]), conv_w_full,
                                                      gw("ab_w_in", 0), "mix0_bwd_b")
    red.add(("ab_w_in", 0), _tn_matmul(s0["xnm"], dh0, 1.0, 1024, 1536, "ab_dwin", comms=[red.swap(gd), red.exchange(gc)]))
    red.add(("ab_w_out", 0), _tn_matmul(s0["ycat"], dy_in, 1.0, 1024, 1024, "ab_dwout"))
    small.update(pool_w=dpw[None], conv_w=rows[None, 0:CONV_WIDTH], conv_b=rows[32:33], conv_ln_g=rows[33:34], conv_ln_b=rows[34:35],
                 pool_scale=rows[35:36], pool_b=rows[36:37].reshape(1, len(POOL_WINDOWS), POOL_GC))

    small_sum = {}

    def small_ready():
        for k, v in norm_grads.items():
            small[k] = jnp.concatenate(v, axis=0)
        flat = [small[n].reshape(-1, 128) for n in SMALL]
        rows = sum(f.shape[0] for f in flat)
        loss_block = jnp.pad(loss, ((0, 8 + (-rows) % 8 - 1), (0, 127)))
        buf = jnp.concatenate(flat + [loss_block], axis=0)

        def done(gathered):
            total, at = _sum_leading(gathered, "reduce_small"), 0
            for n, f in zip(SMALL, flat):
                small_sum[n] = total[at:at + f.shape[0]].reshape(small[n].shape)
                at += f.shape[0]
            small_sum["loss"] = total[at:at + 1, 0:1]

        return _small_all_gather(buf, done)

    dx, dh, act, norm_grads["ffn1_norm"][0] = _ffn_bwd(
        s0["xa"], dy, s0["h1"], row(w["ffn1_norm"][0]), gw("ffn1_w_in", 0), gw("ffn1_w_out", 0), "ffn1_bwd_0")
    small_first, small_second = small_ready()
    red.add(("ffn1_w_in", 0), _tn_matmul(s0["xn1"], dh, 1.0, 1024, 1408, "ffn1_dwin_0", comms=[red.scatter(gd), red.swap(ge)]))
    red.add(("ffn1_w_out", 0), _tn_matmul(act, dy, 0.5, 1408, 1024, "ffn1_dwout_0",
                                          comms=[red.scatter(ge), red.exchange(gd), small_first]))
    grad_x = dx

    big_out = {}

    def adamw_big(n, layer, after=()):
        big_out[n] = _adamw_sharded(w[n], red.mine[(n, layer)], red.theirs[(n, layer)], given["m_" + n], given["v_" + n], core, layer,
                                    big_out.get(n), f"adamw_{n}_{layer}", after=after)

    _exchange_alone("swap_last_grads", [red.swap(gf), red.exchange(ge), small_second])

    def other_updates(token):
        for n in BIG:
            for layer in reversed(range(w[n].shape[0])):
                if (n, layer) not in gf:
                    adamw_big(n, layer, after=[token])
        return [big_out[n][0] for n in BIG]

    red.scatter_behind(gf, other_updates)
    _exchange_alone("exchange_last_grads", [red.exchange(gf)])
    for key in gf:
        adamw_big(*key)

    loss = small_sum["loss"][0, 0]
    grads, delta, new_m, new_v = {}, {}, {}, {}
    for n in WEIGHT_NAMES:
        mom, var = given["m_" + n], given["v_" + n]
        if n in BIG:
            grads[n], delta[n], new_m[n], new_v[n] = big_out[n]
            continue
        g = small_sum[n]
        if n in SHARDED_SMALL:
            width = w[n].shape[-1]
            g = lax.dynamic_slice_in_dim(g, chip * width, width, axis=g.ndim - 1)
        grads[n] = g
        delta[n], new_m[n], new_v[n] = _adamw(w[n], g, mom, var, f"adamw_{n}")
    return (loss, grad_x[None], *[grads[n] for n in WEIGHT_NAMES], *[delta[n] for n in WEIGHT_NAMES],
            *[new_m[n] for n in WEIGHT_NAMES], *[new_v[n] for n in WEIGHT_NAMES])
```

```python
import jax
import jax.numpy as jnp
from jax import lax
from jax.experimental import pallas as pl
from jax.experimental.pallas import tpu as pltpu

F32, BF16 = jnp.float32, jnp.bfloat16
EPS = 1e-6
N_CHIPS = 4
POOL_WINDOWS = (2, 4, 8, 16)
POOL_GC = 128
POOL_CH = 512
CONV_CH = 512
CONV_WIDTH = 31
HALO = 32
SGU_HEADS = 8
CHUNK = 128
DEPTH = 2
ADAM_LR, ADAM_B1, ADAM_B2, ADAM_EPS, ADAM_WD, ADAM_STEP = 0.001, 0.9, 0.999, 1e-08, 0.01, 10
VMEM_LIMIT_BYTES = 60 * 1024 * 1024
MESH_AXES = ("x", "y", "c")
MESH = pl.DeviceIdType.MESH
HBM_SPEC = pl.BlockSpec(memory_space=pltpu.HBM)


def _sds(a):
    return jax.ShapeDtypeStruct(a.shape, a.dtype)


def _cparams_nd(n):
    return pltpu.CompilerParams(dimension_semantics=("arbitrary",) * n, vmem_limit_bytes=VMEM_LIMIT_BYTES)


def _cparams():
    return _cparams_nd(1)


def _dot(a, b):
    return jnp.dot(a, b, preferred_element_type=F32)


def _dot_nt(a, b):
    return lax.dot_general(a, b, (((1,), (1,)), ((), ())), preferred_element_type=F32)


def _dot_tn(a, b):
    return lax.dot_general(a, b, (((0,), (0,)), ((), ())), preferred_element_type=F32)


def _rms_fwd(x):
    r = lax.rsqrt(jnp.mean(x * x, axis=-1, keepdims=True) + EPS)
    return x * r, r


def _rms_bwd(dxn, xh, r, g):
    dxh = dxn * g
    return r * (dxh - xh * jnp.mean(dxh * xh, axis=-1, keepdims=True))


def _ln_fwd(y):
    mu = jnp.mean(y, axis=-1, keepdims=True)
    yc = y - mu
    rs = lax.rsqrt(jnp.mean(yc * yc, axis=-1, keepdims=True) + EPS)
    return yc * rs, rs


def _ln_bwd(dyhat, yhat, rs):
    return rs * (dyhat - jnp.mean(dyhat, axis=-1, keepdims=True) - yhat * jnp.mean(dyhat * yhat, axis=-1, keepdims=True))


def _sigmoid(x):
    return 0.5 * jnp.tanh(0.5 * x) + 0.5


def _const_spec(shape):
    n = len(shape)
    return pl.BlockSpec(shape, lambda i: (0,) * n)


def _row_spec(tm, cols):
    return pl.BlockSpec((tm, cols), lambda i: (i, 0))


def _my_place():
    return lax.axis_index("x"), lax.axis_index("y"), lax.axis_index("c")


def _other_chips(x, y):
    return [(1 - x, y), (x, 1 - y), (1 - x, 1 - y)]


class _Exchange:
    def __init__(self, inputs, out_shapes, plan, count, aliases=None, finish=None, local=None, n_local=0):
        self.inputs, self.out_shapes, self.plan, self.count = list(inputs), list(out_shapes), plan, count
        self.aliases, self.finish, self.local, self.n_local = dict(aliases or {}), finish, local, n_local


def _call(body, *, name, grid, in_specs, out_specs, out_shape, args, scratch_shapes=(), comms=(), scalar=None, aliases=None):
    comms = [cm if isinstance(cm, _Exchange) else cm() for cm in comms]
    in_specs, out_specs, out_shape, scratch_shapes = list(in_specs), list(out_specs), list(out_shape), list(scratch_shapes)
    n_in, n_out, n_scr = len(in_specs), len(out_specs), len(scratch_shapes)
    c_in = [a for cm in comms for a in cm.inputs]
    c_out = [s for cm in comms for s in cm.out_shapes]
    n_remote = sum(cm.count for cm in comms)
    n_local = sum(cm.n_local for cm in comms)
    n_scalar = 0 if scalar is None else 1
    all_aliases = {n_scalar + i: o for i, o in (aliases or {}).items()}
    at_in, at_out = n_scalar + n_in, n_out
    for cm in comms:
        for i, o in cm.aliases.items():
            all_aliases[at_in + i] = at_out + o
        at_in += len(cm.inputs)
        at_out += len(cm.out_shapes)

    def wrapped(*all_refs):
        scalar_ref, refs = all_refs[:n_scalar], all_refs[n_scalar:]
        ins, ci = refs[:n_in], refs[n_in:n_in + len(c_in)]
        at = n_in + len(c_in)
        outs, co = refs[at:at + n_out], refs[at + n_out:at + n_out + len(c_out)]
        at += n_out + len(c_out)
        scr = refs[at:at + n_scr]

        def run_body():
            body(*scalar_ref, *ins, *outs, *scr)

        if not comms:
            run_body()
            return
        send_sems, recv_sems, local_sems = refs[at + n_scr:]
        place = _my_place()
        sends, arrivals, locals_ = [], [], []
        i0 = o0 = 0
        for cm in comms:
            cm_in, cm_out = ci[i0:i0 + len(cm.inputs)], co[o0:o0 + len(cm.out_shapes)]
            i0 += len(cm.inputs)
            o0 += len(cm.out_shapes)
            for src, dst, dev, incoming in cm.plan(cm_in, cm_out, place):
                k = len(sends)
                sends.append(pltpu.make_async_remote_copy(src_ref=src, dst_ref=dst, send_sem=send_sems.at[k], recv_sem=recv_sems.at[k],
                                                          device_id=dev, device_id_type=MESH))
                arrivals.append(pltpu.make_async_remote_copy(src_ref=src, dst_ref=incoming, send_sem=send_sems.at[k],
                                                             recv_sem=recv_sems.at[k], device_id=dev, device_id_type=MESH))
            if cm.local is not None:
                for src, dst in cm.local(cm_in, cm_out, place):
                    locals_.append(pltpu.make_async_copy(src, dst, local_sems.at[len(locals_)]))

        def start():
            for cp in locals_ + sends:
                cp.start()

        def finish():
            for cp in arrivals:
                cp.wait_recv()
            for cp in sends:
                cp.wait_send()
            for cp in locals_:
                cp.wait()

        if not grid:
            start()
            run_body()
            finish()
            return
        ids = [pl.program_id(a) for a in range(len(grid))]
        first, last = ids[0] == 0, ids[0] == grid[0] - 1
        for a in range(1, len(grid)):
            first = jnp.logical_and(first, ids[a] == 0)
            last = jnp.logical_and(last, ids[a] == grid[a] - 1)
        pl.when(first)(start)
        run_body()
        pl.when(last)(finish)

    sems = []
    if comms:
        sems = [pltpu.SemaphoreType.DMA((max(n_remote, 1),)), pltpu.SemaphoreType.DMA((max(n_remote, 1),)),
                pltpu.SemaphoreType.DMA((max(n_local, 1),))]
    all_in, all_out = in_specs + [HBM_SPEC] * len(c_in), out_specs + [HBM_SPEC] * len(c_out)
    if scalar is None:
        kwargs = dict(grid=grid, compiler_params=_cparams_nd(len(grid))) if grid else {}
        res = pl.pallas_call(
            wrapped, name=name, in_specs=all_in, out_specs=all_out, out_shape=out_shape + c_out, scratch_shapes=scratch_shapes + sems,
            input_output_aliases=all_aliases, **kwargs)(*args, *c_in)
    else:
        spec = pltpu.PrefetchScalarGridSpec(num_scalar_prefetch=1, grid=grid, in_specs=all_in, out_specs=all_out,
                                            scratch_shapes=scratch_shapes + sems)
        res = pl.pallas_call(
            wrapped, name=name, grid_spec=spec, out_shape=out_shape + c_out, input_output_aliases=all_aliases,
            compiler_params=_cparams_nd(len(grid)))(scalar, *args, *c_in)
    at = n_out
    for cm in comms:
        got = res[at:at + len(cm.out_shapes)]
        at += len(cm.out_shapes)
        if cm.finish is not None:
            cm.finish(got)
    return list(res[:n_out])


def _exchange_alone(name, comms):
    _call(lambda: None, name=name, grid=(), in_specs=[], out_specs=[], out_shape=[], args=[], comms=comms)


def _in_weight_copies(w_hbm, w_v, sem, base=0):
    return [pltpu.make_async_copy(w_hbm.at[q], w_v.at[q], sem.at[base + q]) for q in range(N_CHIPS)]


def _out_weight_copies(w_hbm, w_v, sem, base=0):
    rows = w_hbm.shape[1]
    return [pltpu.make_async_copy(w_hbm.at[q], w_v.at[pl.ds(q * rows, rows)], sem.at[base + q]) for q in range(N_CHIPS)]


def _load_at_first_step(copies):
    @pl.when(pl.program_id(0) == 0)
    def _():
        for cp in copies:
            cp.start()
        for cp in copies:
            cp.wait()


def _ffn_fwd(x, g, win_g, wout_g, name, comms=()):
    t, d = x.shape
    c = win_g.shape[-1]
    ff = 2 * c
    tm = min(512, t)

    def body(x_ref, g_ref, win_hbm, wout_hbm, xo_ref, h_ref, xn_ref, win_v, wout_v, sem):
        _load_at_first_step(_in_weight_copies(win_hbm, win_v, sem) + _out_weight_copies(wout_hbm, wout_v, sem, N_CHIPS))
        xv = x_ref[...]
        xh, _ = _rms_fwd(xv)
        xn = (xh * g_ref[...]).astype(BF16)
        xn_ref[...] = xn
        acc = jnp.zeros((tm, d), F32)
        for j in range(2):
            gate = _dot(xn, win_v[j])
            up = _dot(xn, win_v[j + 2])
            h_ref[:, j * c:(j + 1) * c] = gate.astype(BF16)
            h_ref[:, ff + j * c:ff + (j + 1) * c] = up.astype(BF16)
            act = (gate * _sigmoid(gate) * up).astype(BF16)
            acc = acc + _dot(act, wout_v[j * c:(j + 1) * c, :])
        xo_ref[...] = xv + 0.5 * acc

    return _call(
        body, name=name, grid=(t // tm,),
        in_specs=[_row_spec(tm, d), _const_spec((1, d)), HBM_SPEC, HBM_SPEC],
        out_specs=[_row_spec(tm, d), _row_spec(tm, 2 * ff), _row_spec(tm, d)],
        out_shape=[jax.ShapeDtypeStruct((t, d), F32), jax.ShapeDtypeStruct((t, 2 * ff), BF16), jax.ShapeDtypeStruct((t, d), BF16)],
        scratch_shapes=[pltpu.VMEM((N_CHIPS, d, c), BF16), pltpu.VMEM((ff, d), BF16), pltpu.SemaphoreType.DMA((2 * N_CHIPS,))],
        args=(x, g, win_g, wout_g), comms=comms)


def _ffn_bwd(x, dy, h, g, win_g, wout_g, name, comms=()):
    t, d = x.shape
    c = win_g.shape[-1]
    ff = 2 * c
    tm = min(256, t)

    def body(x_ref, dy_ref, h_ref, g_ref, win_hbm, wout_hbm, dx_ref, dh_ref, act_ref, dg_ref, win_v, wout_v, sem):
        _load_at_first_step(_in_weight_copies(win_hbm, win_v, sem) + _out_weight_copies(wout_hbm, wout_v, sem, N_CHIPS))

        @pl.when(pl.program_id(0) == 0)
        def _():
            dg_ref[...] = jnp.zeros_like(dg_ref)

        xv, dyv, gv = x_ref[...], dy_ref[...], g_ref[...]
        xh, r = _rms_fwd(xv)
        dyh = (0.5 * dyv).astype(BF16)
        dxn = jnp.zeros((tm, d), F32)
        for j in range(2):
            gate = h_ref[:, j * c:(j + 1) * c].astype(F32)
            up = h_ref[:, ff + j * c:ff + (j + 1) * c].astype(F32)
            dact = _dot_nt(dyh, wout_v[j * c:(j + 1) * c, :])
            s = _sigmoid(gate)
            sl = gate * s
            act_ref[:, j * c:(j + 1) * c] = (sl * up).astype(BF16)
            dgate = (dact * up * (s + sl * (1.0 - s))).astype(BF16)
            dup = (dact * sl).astype(BF16)
            dh_ref[:, j * c:(j + 1) * c] = dgate
            dh_ref[:, ff + j * c:ff + (j + 1) * c] = dup
            dxn = dxn + _dot_nt(dgate, win_v[j]) + _dot_nt(dup, win_v[j + 2])
        dg_ref[...] += jnp.sum(dxn * xh, axis=0, keepdims=True)
        dx_ref[...] = dyv + _rms_bwd(dxn, xh, r, gv)

    return _call(
        body, name=name, grid=(t // tm,),
        in_specs=[_row_spec(tm, d), _row_spec(tm, d), _row_spec(tm, 2 * ff), _const_spec((1, d)), HBM_SPEC, HBM_SPEC],
        out_specs=[_row_spec(tm, d), _row_spec(tm, 2 * ff), _row_spec(tm, ff), _const_spec((1, d))],
        out_shape=[jax.ShapeDtypeStruct((t, d), F32), jax.ShapeDtypeStruct((t, 2 * ff), BF16), jax.ShapeDtypeStruct((t, ff), BF16),
                   jax.ShapeDtypeStruct((1, d), F32)],
        scratch_shapes=[pltpu.VMEM((N_CHIPS, d, c), BF16), pltpu.VMEM((ff, d), BF16), pltpu.SemaphoreType.DMA((2 * N_CHIPS,))],
        args=(x, dy, h, g, win_g, wout_g), comms=comms)


def _norm_matmul(x, g, win_g, name, comms=()):
    t, d = x.shape
    c = win_g.shape[-1]
    tm = min(512, t)

    def body(x_ref, g_ref, win_hbm, o_ref, xn_ref, win_v, sem):
        _load_at_first_step(_in_weight_copies(win_hbm, win_v, sem))
        xh, _ = _rms_fwd(x_ref[...])
        xn = (xh * g_ref[...]).astype(BF16)
        xn_ref[...] = xn
        for q in range(N_CHIPS):
            o_ref[:, q * c:(q + 1) * c] = _dot(xn, win_v[q])

    return _call(
        body, name=name, grid=(t // tm,),
        in_specs=[_row_spec(tm, d), _const_spec((1, d)), HBM_SPEC],
        out_specs=[_row_spec(tm, N_CHIPS * c), _row_spec(tm, d)],
        out_shape=[jax.ShapeDtypeStruct((t, N_CHIPS * c), F32), jax.ShapeDtypeStruct((t, d), BF16)],
        scratch_shapes=[pltpu.VMEM((N_CHIPS, d, c), BF16), pltpu.SemaphoreType.DMA((N_CHIPS,))],
        args=(x, g, win_g), comms=comms)


def _proj_in_bwd_tail(dh, win_v, c):
    dxn = _dot_nt(dh[:, 0:c], win_v[0])
    for q in range(1, N_CHIPS):
        dxn = dxn + _dot_nt(dh[:, q * c:(q + 1) * c], win_v[q])
    return dxn


def _prev_halo_spec(tm, cols):
    return pl.BlockSpec((HALO, cols), lambda i: (jnp.maximum(i * (tm // HALO) - 1, 0), 0))


def _next_halo_spec(tm, cols, t):
    last = t // HALO - 1
    return pl.BlockSpec((HALO, cols), lambda i: (jnp.minimum((i + 1) * (tm // HALO), last), 0))


def _shift_down(w, k):
    return w if k == 0 else pltpu.roll(w, k, 0)


def _shift_up(w, k):
    return w if k == 0 else pltpu.roll(w, w.shape[0] - k, 0)


def _pool_counts(i, tm):
    pos = (i * tm + lax.broadcasted_iota(jnp.int32, (tm, POOL_CH), 0) + 1).astype(F32)
    lane = lax.broadcasted_iota(jnp.int32, (tm, POOL_CH), 1)
    win = jnp.where(lane < POOL_GC, 2.0, jnp.where(lane < 2 * POOL_GC, 4.0, jnp.where(lane < 3 * POOL_GC, 8.0, 16.0)))
    return jnp.minimum(pos, win)


def _group_select(parts):
    return jnp.concatenate([p[:, k * POOL_GC:(k + 1) * POOL_GC] for k, p in enumerate(parts)], axis=1)


TAP_ROWS = 128


def _tap_blocks(tm, block_fn):
    cols = []
    for ch in range(CONV_CH // POOL_GC):
        lanes = slice(ch * POOL_GC, (ch + 1) * POOL_GC)
        cols.append(jnp.concatenate([block_fn(r, lanes) for r in range(tm // TAP_ROWS)], axis=0))
    return jnp.concatenate(cols, axis=1)


def _conv_block(win, taps):
    acc = jnp.zeros((TAP_ROWS, win.shape[1]), F32)
    for k in range(CONV_WIDTH):
        acc = acc + taps[k:k + 1, :] * _shift_down(win, CONV_WIDTH - 1 - k)[HALO:, :]
    return acc


def _conv_block_transposed(win, taps):
    acc = jnp.zeros((TAP_ROWS, win.shape[1]), F32)
    for k in range(CONV_WIDTH):
        acc = acc + taps[k:k + 1, :] * _shift_up(win, CONV_WIDTH - 1 - k)[0:TAP_ROWS, :]
    return acc


def _mix0_recompute(i, tm, h_cur, h_prev, conv_w, conv_b, y=None):
    prev = jnp.where(i > 0, h_prev, 0.0)
    win = jnp.concatenate([prev, h_cur], axis=0)
    u_w = win[:, 0:POOL_CH]
    a_w = win[:, POOL_CH:POOL_CH + CONV_CH]
    gt_w = win[:, POOL_CH + CONV_CH:]
    g_w = a_w * _sigmoid(gt_w)
    if y is None:
        y = _tap_blocks(tm, lambda r, lanes: _conv_block(g_w[r * TAP_ROWS:(r + 1) * TAP_ROWS + HALO, lanes], conv_w[:, lanes])) + conv_b
    s2 = u_w + _shift_down(u_w, 1)
    s4 = s2 + _shift_down(s2, 2)
    s8 = s4 + _shift_down(s4, 4)
    s16 = s8 + _shift_down(s8, 8)
    sums = _group_select([s2[HALO:], s4[HALO:], s8[HALO:], s16[HALO:]])
    cnt = _pool_counts(i, tm)
    pooled = sums / cnt - h_cur[:, 0:POOL_CH]
    return g_w, y, pooled, cnt


def _pool_linear(pooled, pw_ref, pb):
    return jnp.concatenate(
        [_dot(pooled[:, k * POOL_GC:(k + 1) * POOL_GC].astype(BF16), pw_ref[k].astype(BF16)) for k in range(len(POOL_WINDOWS))], axis=1) + pb


def _mix0_fwd(x, h0, pool_w, pool_b, pool_scale, conv_w, conv_b, ln_g, ln_b, wout_g, name, comms=()):
    t, d = x.shape
    tm = min(256, t)
    hc = h0.shape[1]

    def body(x_ref, h_ref, hp_ref, pw_ref, pb_ref, ps_ref, cw_ref, cb_ref, lg_ref, lb_ref, wout_hbm, xo_ref, ycat_ref, y_ref, wout_v, sem):
        _load_at_first_step(_out_weight_copies(wout_hbm, wout_v, sem))
        i = pl.program_id(0)
        _, y, pooled, _ = _mix0_recompute(i, tm, h_ref[...], hp_ref[...], cw_ref[...], cb_ref[...])
        y_ref[...] = y
        yhat, _ = _ln_fwd(y)
        yn = yhat * lg_ref[...] + lb_ref[...]
        yb = yn * _sigmoid(yn)
        ya = _pool_linear(pooled, pw_ref, pb_ref[...]) * ps_ref[...]
        ycat = jnp.concatenate([ya, yb], axis=1).astype(BF16)
        ycat_ref[...] = ycat
        xo_ref[...] = x_ref[...] + _dot(ycat, wout_v[...])

    return _call(
        body, name=name, grid=(t // tm,),
        in_specs=[_row_spec(tm, d), _row_spec(tm, hc), _prev_halo_spec(tm, hc), _const_spec(pool_w.shape), _const_spec((1, POOL_CH)),
                  _const_spec((1, POOL_CH)), _const_spec(conv_w.shape), _const_spec((1, CONV_CH)), _const_spec((1, CONV_CH)),
                  _const_spec((1, CONV_CH)), HBM_SPEC],
        out_specs=[_row_spec(tm, d), _row_spec(tm, d), _row_spec(tm, CONV_CH)],
        out_shape=[jax.ShapeDtypeStruct((t, d), F32), jax.ShapeDtypeStruct((t, d), BF16), jax.ShapeDtypeStruct((t, CONV_CH), F32)],
        scratch_shapes=[pltpu.VMEM((d, d), BF16), pltpu.SemaphoreType.DMA((N_CHIPS,))],
        args=(x, h0, h0, pool_w, pool_b, pool_scale, conv_w, conv_b, ln_g, ln_b, wout_g), comms=comms)


def _mix0_bwd_a(dy, h0, y_conv, pool_w, pool_b, pool_scale, conv_w, conv_b, ln_g, ln_b, wout_g, name):
    t, d = dy.shape
    tm = min(256, t)
    hc = h0.shape[1]
    n_small = 40

    def body(dy_ref, h_ref, hp_ref, y_ref, pw_ref, pb_ref, ps_ref, cw_ref, cb_ref, lg_ref, lb_ref, wout_hbm,
             dconv_ref, dpc_ref, dpw_ref, small_ref, wout_v, sem):
        _load_at_first_step(_out_weight_copies(wout_hbm, wout_v, sem))
        i = pl.program_id(0)

        @pl.when(i == 0)
        def _():
            dpw_ref[...] = jnp.zeros_like(dpw_ref)
            small_ref[...] = jnp.zeros_like(small_ref)

        g_w, y, pooled, cnt = _mix0_recompute(i, tm, h_ref[...], hp_ref[...], cw_ref[...], cb_ref[...], y_ref[...])
        yhat, rs = _ln_fwd(y)
        lg = lg_ref[...]
        yn = yhat * lg + lb_ref[...]
        mixed = _pool_linear(pooled, pw_ref, pb_ref[...])
        dycat = _dot_nt(dy_ref[...].astype(BF16), wout_v[...])
        dya, dyb = dycat[:, 0:POOL_CH], dycat[:, POOL_CH:]
        sg = _sigmoid(yn)
        dyn = dyb * (sg * (1.0 + yn * (1.0 - sg)))
        dyc = _ln_bwd(dyn * lg, yhat, rs)
        dconv_ref[...] = dyc

        def add_row(k, value):
            small_ref[k:k + 1, :] += jnp.sum(value, axis=0, keepdims=True)

        for ch in range(CONV_CH // POOL_GC):
            lanes = slice(ch * POOL_GC, (ch + 1) * POOL_GC)
            taps = [jnp.zeros((1, POOL_GC), F32)] * CONV_WIDTH
            for r in range(tm // TAP_ROWS):
                win = g_w[r * TAP_ROWS:(r + 1) * TAP_ROWS + HALO, lanes]
                d = dyc[r * TAP_ROWS:(r + 1) * TAP_ROWS, lanes]
                for k in range(CONV_WIDTH):
                    taps[k] = taps[k] + jnp.sum(d * _shift_down(win, CONV_WIDTH - 1 - k)[HALO:, :], axis=0, keepdims=True)
            for k in range(CONV_WIDTH):
                small_ref[k:k + 1, lanes] += taps[k]
        add_row(32, dyc)
        add_row(33, dyn * yhat)
        add_row(34, dyn)
        scale = ps_ref[...]
        dmixed = dya * scale
        add_row(35, dya * mixed)
        add_row(36, dmixed)
        dmb = dmixed.astype(BF16)
        dpooled = []
        for k in range(len(POOL_WINDOWS)):
            sl = slice(k * POOL_GC, (k + 1) * POOL_GC)
            dpw_ref[k] += _dot_tn(pooled[:, sl].astype(BF16), dmb[:, sl])
            dpooled.append(_dot_nt(dmb[:, sl], pw_ref[k].astype(BF16)))
        dpc_ref[...] = jnp.concatenate(dpooled, axis=1) / cnt

    return _call(
        body, name=name, grid=(t // tm,),
        in_specs=[_row_spec(tm, d), _row_spec(tm, hc), _prev_halo_spec(tm, hc), _row_spec(tm, CONV_CH), _const_spec(pool_w.shape),
                  _const_spec((1, POOL_CH)), _const_spec((1, POOL_CH)), _const_spec(conv_w.shape), _const_spec((1, CONV_CH)),
                  _const_spec((1, CONV_CH)), _const_spec((1, CONV_CH)), HBM_SPEC],
        out_specs=[_row_spec(tm, CONV_CH), _row_spec(tm, POOL_CH), _const_spec(pool_w.shape), _const_spec((n_small, CONV_CH))],
        out_shape=[jax.ShapeDtypeStruct((t, CONV_CH), F32), jax.ShapeDtypeStruct((t, POOL_CH), F32),
                   jax.ShapeDtypeStruct(pool_w.shape, F32), jax.ShapeDtypeStruct((n_small, CONV_CH), F32)],
        scratch_shapes=[pltpu.VMEM((d, d), BF16), pltpu.SemaphoreType.DMA((N_CHIPS,))],
        args=(dy, h0, h0, y_conv, pool_w, pool_b, pool_scale, conv_w, conv_b, ln_g, ln_b, wout_g))


def _mix0_bwd_b(x, dy, h0, dconv, dpc, g, conv_w, win_g, name):
    t, d = x.shape
    tm = min(256, t)
    hc = h0.shape[1]
    c = win_g.shape[-1]
    n_tiles = t // tm

    def body(x_ref, dy_ref, h_ref, dc_ref, dcn_ref, dp_ref, dpn_ref, g_ref, cw_ref, win_hbm, dx_ref, dh_ref, dg_ref, win_v, sem):
        _load_at_first_step(_in_weight_copies(win_hbm, win_v, sem))
        i = pl.program_id(0)

        @pl.when(i == 0)
        def _():
            dg_ref[...] = jnp.zeros_like(dg_ref)

        not_last = i < n_tiles - 1
        dc_w = jnp.concatenate([dc_ref[...], jnp.where(not_last, dcn_ref[...], 0.0)], axis=0)
        dp_w = jnp.concatenate([dp_ref[...], jnp.where(not_last, dpn_ref[...], 0.0)], axis=0)
        cw = cw_ref[...]
        dg = _tap_blocks(tm, lambda r, lanes: _conv_block_transposed(dc_w[r * TAP_ROWS:(r + 1) * TAP_ROWS + HALO, lanes], cw[:, lanes]))
        a2 = dp_w + _shift_up(dp_w, 1)
        a4 = a2 + _shift_up(a2, 2)
        a8 = a4 + _shift_up(a4, 4)
        a16 = a8 + _shift_up(a8, 8)
        back = _group_select([a2[0:tm], a4[0:tm], a8[0:tm], a16[0:tm]])
        du = back - dp_ref[...] * _pool_counts(i, tm)
        hv = h_ref[...]
        a = hv[:, POOL_CH:POOL_CH + CONV_CH]
        sig = _sigmoid(hv[:, POOL_CH + CONV_CH:])
        dh = jnp.concatenate([du, dg * sig, dg * a * sig * (1.0 - sig)], axis=1).astype(BF16)
        dh_ref[...] = dh
        dxn = _proj_in_bwd_tail(dh, win_v, c)
        xh, r = _rms_fwd(x_ref[...])
        dg_ref[...] += jnp.sum(dxn * xh, axis=0, keepdims=True)
        dx_ref[...] = dy_ref[...] + _rms_bwd(dxn, xh, r, g_ref[...])

    return _call(
        body, name=name, grid=(n_tiles,),
        in_specs=[_row_spec(tm, d), _row_spec(tm, d), _row_spec(tm, hc), _row_spec(tm, CONV_CH), _next_halo_spec(tm, CONV_CH, t),
                  _row_spec(tm, POOL_CH), _next_halo_spec(tm, POOL_CH, t), _const_spec((1, d)), _const_spec(conv_w.shape), HBM_SPEC],
        out_specs=[_row_spec(tm, d), _row_spec(tm, hc), _const_spec((1, d))],
        out_shape=[jax.ShapeDtypeStruct((t, d), F32), jax.ShapeDtypeStruct((t, hc), BF16), jax.ShapeDtypeStruct((1, d), F32)],
        scratch_shapes=[pltpu.VMEM((N_CHIPS, d, c), BF16), pltpu.SemaphoreType.DMA((N_CHIPS,))],
        args=(x, dy, h0, dconv, dconv, dpc, dpc, g, conv_w, win_g))


SQRT_HALF = 0.7071067811865476
INV_SQRT_2PI = 0.3989422804014327


def _causal_mask():
    return (lax.broadcasted_iota(jnp.int32, (CHUNK, CHUNK), 1) <= lax.broadcasted_iota(jnp.int32, (CHUNK, CHUNK), 0)).astype(F32)


def _sgu_recompute(pre, lg, lb):
    half = pre.shape[1] // 2
    phi = 0.5 * (1.0 + lax.erf(pre * SQRT_HALF))
    z = pre * phi
    u, v = z[:, 0:half], z[:, half:]
    vhat, rs = _ln_fwd(v)
    return u, vhat, rs, vhat * lg + lb, phi


def _sgu_spatial(vln, w_ref, bt, tm):
    mask = _causal_mask()
    wm = [(w_ref[hd] * mask).astype(BF16) for hd in range(SGU_HEADS)]
    vb = vln.astype(BF16)
    rows = []
    for ch in range(tm // CHUNK):
        blocks = [_dot(wm[hd], vb[ch * CHUNK:(ch + 1) * CHUNK, hd * CHUNK:(hd + 1) * CHUNK]) + bt[:, hd:hd + 1] for hd in range(SGU_HEADS)]
        rows.append(jnp.concatenate(blocks, axis=1))
    return jnp.concatenate(rows, axis=0), wm


def _sgu_fwd(x, pre, ln_g, ln_b, w, bt, wout_g, name):
    t, d = x.shape
    tm = min(256, t)
    pc = pre.shape[1]

    def body(x_ref, pre_ref, lg_ref, lb_ref, w_ref, bt_ref, wout_hbm, xo_ref, p_ref, wout_v, sem):
        _load_at_first_step(_out_weight_copies(wout_hbm, wout_v, sem))
        u, _, _, vln, _ = _sgu_recompute(pre_ref[...], lg_ref[...], lb_ref[...])
        vo, _ = _sgu_spatial(vln, w_ref, bt_ref[...], tm)
        p = (u * vo).astype(BF16)
        p_ref[...] = p
        xo_ref[...] = x_ref[...] + _dot(p, wout_v[...])

    return _call(
        body, name=name, grid=(t // tm,),
        in_specs=[_row_spec(tm, d), _row_spec(tm, pc), _const_spec((1, d)), _const_spec((1, d)), _const_spec(w.shape),
                  _const_spec(bt.shape), HBM_SPEC],
        out_specs=[_row_spec(tm, d), _row_spec(tm, d)],
        out_shape=[jax.ShapeDtypeStruct((t, d), F32), jax.ShapeDtypeStruct((t, d), BF16)],
        scratch_shapes=[pltpu.VMEM((d, d), BF16), pltpu.SemaphoreType.DMA((N_CHIPS,))],
        args=(x, pre, ln_g, ln_b, w, bt, wout_g))


def _sgu_bwd(x, dy, pre, g, ln_g, ln_b, w, bt, win_g, wout_g, name, comms=()):
    t, d = x.shape
    tm = min(256, t)
    pc = pre.shape[1]
    c = win_g.shape[-1]

    def body(x_ref, dy_ref, pre_ref, g_ref, lg_ref, lb_ref, w_ref, bt_ref, win_hbm, wout_hbm,
             dx_ref, dpre_ref, dg_ref, dlg_ref, dlb_ref, dw_ref, dbt_ref, win_v, wout_v, sem):
        _load_at_first_step(_in_weight_copies(win_hbm, win_v, sem) + _out_weight_copies(wout_hbm, wout_v, sem, N_CHIPS))
        i = pl.program_id(0)

        @pl.when(i == 0)
        def _():
            for ref in (dg_ref, dlg_ref, dlb_ref, dw_ref, dbt_ref):
                ref[...] = jnp.zeros_like(ref)

        prev = pre_ref[...]
        lg = lg_ref[...]
        u, vhat, rs, vln, phi = _sgu_recompute(prev, lg, lb_ref[...])
        vo, wm = _sgu_spatial(vln, w_ref, bt_ref[...], tm)
        dp = _dot_nt(dy_ref[...].astype(BF16), wout_v[...])
        du = dp * vo
        dvo = dp * u
        dvob = dvo.astype(BF16)
        vb = vln.astype(BF16)
        head_lane = lax.broadcasted_iota(jnp.int32, (CHUNK, SGU_HEADS), 1)
        dbt = jnp.zeros((CHUNK, SGU_HEADS), F32)
        dw = [jnp.zeros((CHUNK, CHUNK), F32) for _ in range(SGU_HEADS)]
        rows = []
        for ch in range(tm // CHUNK):
            rs_ = slice(ch * CHUNK, (ch + 1) * CHUNK)
            blocks = []
            for hd in range(SGU_HEADS):
                cs = slice(hd * CHUNK, (hd + 1) * CHUNK)
                dbt = dbt + jnp.where(head_lane == hd, jnp.sum(dvo[rs_, cs], axis=1, keepdims=True), 0.0)
                dw[hd] = dw[hd] + _dot_nt(dvob[rs_, cs], vb[rs_, cs])
                blocks.append(_dot_tn(wm[hd], dvob[rs_, cs]))
            rows.append(jnp.concatenate(blocks, axis=1))
        dvln = jnp.concatenate(rows, axis=0)
        mask = _causal_mask()
        for hd in range(SGU_HEADS):
            dw_ref[hd] += dw[hd] * mask
        dbt_ref[...] += dbt
        dlg_ref[...] += jnp.sum(dvln * vhat, axis=0, keepdims=True)
        dlb_ref[...] += jnp.sum(dvln, axis=0, keepdims=True)
        dv = _ln_bwd(dvln * lg, vhat, rs)
        gelu_grad = phi + prev * jnp.exp(-0.5 * prev * prev) * INV_SQRT_2PI
        dpre = (jnp.concatenate([du, dv], axis=1) * gelu_grad).astype(BF16)
        dpre_ref[...] = dpre
        dxn = _proj_in_bwd_tail(dpre, win_v, c)
        xh, r = _rms_fwd(x_ref[...])
        dg_ref[...] += jnp.sum(dxn * xh, axis=0, keepdims=True)
        dx_ref[...] = dy_ref[...] + _rms_bwd(dxn, xh, r, g_ref[...])

    return _call(
        body, name=name, grid=(t // tm,),
        in_specs=[_row_spec(tm, d), _row_spec(tm, d), _row_spec(tm, pc), _const_spec((1, d)), _const_spec((1, d)), _const_spec((1, d)),
                  _const_spec(w.shape), _const_spec(bt.shape), HBM_SPEC, HBM_SPEC],
        out_specs=[_row_spec(tm, d), _row_spec(tm, pc), _const_spec((1, d)), _const_spec((1, d)), _const_spec((1, d)),
                   _const_spec(w.shape), _const_spec(bt.shape)],
        out_shape=[jax.ShapeDtypeStruct((t, d), F32), jax.ShapeDtypeStruct((t, pc), BF16), jax.ShapeDtypeStruct((1, d), F32),
                   jax.ShapeDtypeStruct((1, d), F32), jax.ShapeDtypeStruct((1, d), F32), jax.ShapeDtypeStruct(w.shape, F32),
                   jax.ShapeDtypeStruct(bt.shape, F32)],
        scratch_shapes=[pltpu.VMEM((N_CHIPS, d, c), BF16), pltpu.VMEM((d, d), BF16), pltpu.SemaphoreType.DMA((2 * N_CHIPS,))],
        args=(x, dy, pre, g, ln_g, ln_b, w, bt, win_g, wout_g), comms=comms)


def _final_loss(x, tgt, g, name):
    t, d = x.shape
    tm = min(512, t)

    def body(x_ref, t_ref, g_ref, dx_ref, loss_ref, dg_ref):
        @pl.when(pl.program_id(0) == 0)
        def _():
            loss_ref[...] = jnp.zeros_like(loss_ref)
            dg_ref[...] = jnp.zeros_like(dg_ref)

        gv = g_ref[...]
        xh, r = _rms_fwd(x_ref[...])
        diff = xh * gv - t_ref[...]
        loss_ref[...] += 0.5 * jnp.sum(jnp.sum(diff * diff, axis=1, keepdims=True), axis=0, keepdims=True) / d
        dout = diff / d
        dg_ref[...] += jnp.sum(dout * xh, axis=0, keepdims=True)
        dx_ref[...] = _rms_bwd(dout, xh, r, gv)

    return _call(
        body, name=name, grid=(t // tm,),
        in_specs=[_row_spec(tm, d), _row_spec(tm, d), _const_spec((1, d))],
        out_specs=[_row_spec(tm, d), _const_spec((1, 1)), _const_spec((1, d))],
        out_shape=[jax.ShapeDtypeStruct((t, d), F32), jax.ShapeDtypeStruct((1, 1), F32), jax.ShapeDtypeStruct((1, d), F32)],
        args=(x, tgt, g))


def _tn_matmul(a, b, scale, bm, bn, name, comms=()):
    t, m = a.shape
    n = b.shape[1]
    tk = min(2048, t)
    bm, bn = min(bm, m), min(bn, n)
    nk = t // tk

    def body(a_ref, b_ref, o_ref, acc_ref):
        k = pl.program_id(2)

        @pl.when(k == 0)
        def _():
            acc_ref[...] = jnp.zeros_like(acc_ref)

        bv = b_ref[...]
        if bv.dtype != BF16:
            bv = (scale * bv).astype(BF16)
        acc_ref[...] += _dot_tn(a_ref[...], bv)

        @pl.when(k == nk - 1)
        def _():
            o_ref[...] = acc_ref[...].astype(BF16)

    return _call(
        body, name=name, grid=(m // bm, n // bn, nk),
        in_specs=[pl.BlockSpec((tk, bm), lambda i, j, k: (k, i)), pl.BlockSpec((tk, bn), lambda i, j, k: (k, j))],
        out_specs=[pl.BlockSpec((bm, bn), lambda i, j, k: (i, j))],
        out_shape=[jax.ShapeDtypeStruct((m, n), BF16)],
        scratch_shapes=[pltpu.VMEM((bm, bn), F32)],
        args=(a, b), comms=comms)[0]


def _row_tile(rows, cols, budget_bytes=2 * 1024 * 1024):
    best = None
    for cand in range(16, rows + 1, 16):
        if rows % cand == 0 and cand * cols * 4 <= budget_bytes:
            best = cand
    return best or rows


def _scalar_grid(grid, in_specs, out_specs):
    return pltpu.PrefetchScalarGridSpec(num_scalar_prefetch=1, grid=grid, in_specs=in_specs, out_specs=out_specs)


def _cast_into_slot(w, layer, me, name):
    _, rows, cols = w.shape
    tr = _row_tile(rows, cols)

    def body(me_ref, w_ref, o_ref):
        o_ref[...] = w_ref[...].astype(BF16)

    return pl.pallas_call(
        body, name=name,
        grid_spec=_scalar_grid((rows // tr,), [pl.BlockSpec((None, tr, cols), lambda i, me: (layer, i, 0))],
                               pl.BlockSpec((None, tr, cols), lambda i, me: (me[0], i, 0))),
        out_shape=jax.ShapeDtypeStruct((N_CHIPS, rows, cols), BF16), compiler_params=_cparams())(me, w)


def _add_half(view, other, core, name):
    q, _, r, c = view.shape
    tr = _row_tile(r, c)

    def body(core_ref, a_ref, b_ref, o_ref):
        o_ref[...] = (a_ref[...].astype(F32) + b_ref[...].astype(F32)).astype(BF16)

    return pl.pallas_call(
        body, name=name,
        grid_spec=_scalar_grid((q, r // tr), [pl.BlockSpec((None, None, tr, c), lambda k, i, core: (k, core[0], i, 0)),
                                             pl.BlockSpec((None, tr, c), lambda k, i, core: (k, i, 0))],
                               pl.BlockSpec((None, tr, c), lambda k, i, core: (k, i, 0))),
        out_shape=jax.ShapeDtypeStruct((q, r, c), BF16), compiler_params=_cparams_nd(2))(core, view, other)


def _reduce_piece(partial, staged, me, column_sharded, name):
    _, r, c = staged.shape
    tr = _row_tile(r, c, budget_bytes=1024 * 1024)
    nt = r // tr
    if column_sharded:
        own2d = partial.reshape(r, N_CHIPS * c)
        own_spec = pl.BlockSpec((tr, c), lambda i, me: (i, me[0]))
    else:
        own2d = partial.reshape(N_CHIPS * r, c)
        own_spec = pl.BlockSpec((tr, c), lambda i, me: (me[0] * nt + i, 0))
    ring = [pl.BlockSpec((None, tr, c), lambda i, me, k=k: ((me[0] + k) % N_CHIPS, i, 0)) for k in (1, 2, 3)]

    def body(me_ref, own_ref, s1_ref, s2_ref, s3_ref, o_ref):
        o_ref[...] = ((own_ref[...].astype(F32) + s1_ref[...].astype(F32)) + s2_ref[...].astype(F32)) + s3_ref[...].astype(F32)

    return pl.pallas_call(
        body, name=name, grid_spec=_scalar_grid((nt,), [own_spec] + ring, pl.BlockSpec((tr, c), lambda i, me: (i, 0))),
        out_shape=jax.ShapeDtypeStruct((r, c), F32), compiler_params=_cparams())(me, own2d, staged, staged, staged)


def _sum_leading(s, name):
    n, rows, cols = s.shape
    tr = _row_tile(rows, cols, budget_bytes=1024 * 1024)

    def body(s_ref, o_ref):
        acc = s_ref[0].astype(F32)
        for k in range(1, n):
            acc = acc + s_ref[k].astype(F32)
        o_ref[...] = acc

    return pl.pallas_call(
        body, name=name, grid=(rows // tr,), in_specs=[pl.BlockSpec((n, tr, cols), lambda i: (0, i, 0))], out_specs=_row_spec(tr, cols),
        out_shape=jax.ShapeDtypeStruct((rows, cols), F32), compiler_params=_cparams())(s)


ADAM_C1 = 1.0 / (1.0 - ADAM_B1 ** ADAM_STEP)
ADAM_C2 = 1.0 / (1.0 - ADAM_B2 ** ADAM_STEP)


def _adamw_math(w, g, m, v):
    mn = ADAM_B1 * m + (1.0 - ADAM_B1) * g
    vn = ADAM_B2 * v + (1.0 - ADAM_B2) * (g * g)
    return -ADAM_LR * ((mn * ADAM_C1) / (jnp.sqrt(vn * ADAM_C2) + ADAM_EPS) + ADAM_WD * w), mn, vn


def _adamw(w, g, m, v, name):
    shape = w.shape
    cols = shape[-1] if w.ndim > 1 else 128
    w2, g2, m2, v2 = (a.reshape(-1, cols) for a in (w, g, m, v))
    rows = w2.shape[0]
    tr = _row_tile(rows, cols, budget_bytes=1024 * 1024)

    def body(w_ref, g_ref, m_ref, v_ref, d_ref, mo_ref, vo_ref):
        d_ref[...], mo_ref[...], vo_ref[...] = _adamw_math(w_ref[...], g_ref[...], m_ref[...], v_ref[...])

    spec = _row_spec(tr, cols)
    outs = pl.pallas_call(
        body, name=name, grid=(rows // tr,), in_specs=[spec] * 4, out_specs=[spec] * 3,
        out_shape=[jax.ShapeDtypeStruct((rows, cols), F32)] * 3, compiler_params=_cparams())(w2, g2, m2, v2)
    return tuple(o.reshape(shape) for o in outs)


def _adamw_sharded(w, g_mine, g_sibling, m, v, core, layer, prev, name, comms=(), after=()):
    n_layers, r, c = w.shape
    half = r // 2
    tr = _row_tile(half, c)
    nt = half // tr

    def body(core_ref, w_ref, gm_ref, gs_ref, m_ref, v_ref, *rest):
        g_ref, d_ref, mo_ref, vo_ref = rest[-4:]
        gv = jnp.where(pl.program_id(0) == core_ref[0], gm_ref[...], gs_ref[...])
        g_ref[...] = gv
        d_ref[...], mo_ref[...], vo_ref[...] = _adamw_math(w_ref[...], gv, m_ref[...], v_ref[...])

    full = pl.BlockSpec((None, tr, c), lambda h, i, core: (layer, h * nt + i, 0))
    part = pl.BlockSpec((tr, c), lambda h, i, core: (i, 0))
    args = [w, g_mine, g_sibling, m, v]
    in_specs = [full, part, part, full, full]
    aliases = {}
    if prev is not None:
        aliases = {len(args) + k: k for k in range(4)}
        args += list(prev)
        in_specs += [pl.BlockSpec(memory_space=pl.ANY)] * 4
    args += list(after)
    in_specs += [pl.BlockSpec(memory_space=pl.ANY)] * len(after)
    return _call(body, name=name, grid=(2, nt), in_specs=in_specs, out_specs=[full] * 4, out_shape=[jax.ShapeDtypeStruct(w.shape, F32)] * 4,
                 args=args, comms=comms, scalar=core, aliases=aliases)


BIG_IN = ("ffn1_w_in", "ffn2_w_in", "ab_w_in", "sgu_w_in")
BIG_OUT = ("ffn1_w_out", "ffn2_w_out", "ab_w_out", "sgu_w_out")
BIG = BIG_IN + BIG_OUT


class _Gatherer:
    def __init__(self, slots):
        self.slots = dict(slots)

    def _stage(self, keys, d2d):
        n = len(keys)

        def plan(ins, outs, place):
            x, y, c = place
            me = 2 * x + y
            remote = []
            for a in range(n):
                rows = ins[a].shape[1] // 2

                def half(ref, q, core, rows=rows):
                    return ref.at[q, pl.ds(core * rows, rows), :]

                for (px, py) in _other_chips(x, y):
                    q = 2 * px + py
                    if d2d:
                        remote.append((half(ins[a], q, c), half(outs[a], q, c), (x, y, 1 - c), half(outs[a], q, 1 - c)))
                    else:
                        remote.append((half(ins[a], me, c), half(outs[a], me, c), (px, py, c), half(outs[a], q, c)))
            return remote

        def finish(outs):
            for k, o in zip(keys, outs):
                self.slots[k] = o

        arrays = [self.slots[k] for k in keys]
        return _Exchange(arrays, [_sds(a) for a in arrays], plan, 3 * n, {a: a for a in range(n)}, finish)

    def direct(self, keys):
        n = len(keys)

        def plan(ins, outs, place):
            x, y, c = place
            me = 2 * x + y
            return [(ins[a].at[me], outs[a].at[me], (px, py, c), outs[a].at[2 * px + py]) for a in range(n) for (px, py) in _other_chips(x, y)]

        def finish(outs):
            for k, o in zip(keys, outs):
                self.slots[k] = o

        arrays = [self.slots[k] for k in keys]
        return _Exchange(arrays, [_sds(a) for a in arrays], plan, 3 * n, {a: a for a in range(n)}, finish)

    def ici(self, keys):
        return self._stage(keys, False)

    def d2d(self, keys):
        return self._stage(keys, True)


class _Reducer:
    def __init__(self, me, core):
        self.me, self.core = me, core
        self.views, self.partial, self.staged, self.mine, self.theirs = {}, {}, {}, {}, {}

    def add(self, key, g):
        m, n = g.shape
        if key[0] in BIG_IN:
            self.views[key] = g.reshape(1, 2, m // 2, n)
        else:
            self.views[key] = g.reshape(N_CHIPS, 2, m // (2 * N_CHIPS), n)

    def swap(self, keys):
        views = [self.views[k] for k in keys]

        def plan(ins, outs, place):
            x, y, c = place
            return [(ins[a].at[:, 1 - c], outs[a], (x, y, 1 - c), outs[a]) for a in range(len(keys))]

        def finish(outs):
            for k, v, o in zip(keys, views, outs):
                self.partial[k] = _add_half(v, o, self.core, f"chip_partial_{k[0]}_{k[1]}")

        shapes = [jax.ShapeDtypeStruct((v.shape[0],) + v.shape[2:], v.dtype) for v in views]
        return _Exchange(views, shapes, plan, len(keys), None, finish)

    def scatter(self, keys, part=(0, 1)):
        i, n = part
        n_keys = len(keys)
        parts = [self.partial[k] for k in keys]
        shapes = []
        for k, p in zip(keys, parts):
            q, r, c = p.shape
            shapes.append(jax.ShapeDtypeStruct((N_CHIPS, r, c // N_CHIPS if k[0] in BIG_IN else c), p.dtype))

        def piece(ref, key, q, rows, cols):
            return ref.at[0, rows, pl.ds(q * cols, cols)] if key[0] in BIG_IN else ref.at[q, rows, :]

        def plan(ins, outs, place):
            x, y, c = place
            me = 2 * x + y
            remote = []
            for a, k in enumerate(keys):
                _, r, cols = shapes[a].shape
                rows = pl.ds(i * (r // n), r // n)
                for (px, py) in _other_chips(x, y):
                    q = 2 * px + py
                    remote.append((piece(ins[a], k, q, rows, cols), outs[a].at[me, rows, :], (px, py, c), outs[a].at[q, rows, :]))
            return remote

        def finish(outs):
            for k, p, o in zip(keys, parts, outs):
                self.staged[k] = o
                if i == n - 1:
                    self.mine[k] = _reduce_piece(p, o, self.me, k[0] in BIG_IN, f"reduce_{k[0]}_{k[1]}")

        inputs, aliases = parts, None
        if i > 0:
            inputs = parts + [self.staged[k] for k in keys]
            aliases = {n_keys + a: a for a in range(n_keys)}
        return _Exchange(inputs, shapes, plan, 3 * n_keys, aliases, finish)

    def scatter_behind(self, keys, work):
        n = len(keys)
        parts = [self.partial[k] for k in keys]
        lands = []
        for k, p in zip(keys, parts):
            _, r, c = p.shape
            lands.append(jax.ShapeDtypeStruct((N_CHIPS, r, c // N_CHIPS if k[0] in BIG_IN else c), p.dtype))
        sem_spec = pl.BlockSpec(memory_space=pltpu.SEMAPHORE)
        effect = pltpu.CompilerParams(has_side_effects=pltpu.SideEffectType.DATAFLOW_SIDE_EFFECTING)

        def copies(part_refs, land_refs, send_sems, recv_sems):
            x, y, c = _my_place()
            me = 2 * x + y
            out = []
            for a, k in enumerate(keys):
                cols = lands[a].shape[2]
                for j, (px, py) in enumerate(_other_chips(x, y)):
                    q = 2 * px + py
                    src = part_refs[a].at[0, :, pl.ds(q * cols, cols)] if k[0] in BIG_IN else part_refs[a].at[q]
                    sems = dict(send_sem=send_sems.at[3 * a + j], recv_sem=recv_sems.at[3 * a + j], device_id=(px, py, c), device_id_type=MESH)
                    out.append((pltpu.make_async_remote_copy(src_ref=src, dst_ref=land_refs[a].at[me], **sems),
                                pltpu.make_async_remote_copy(src_ref=src, dst_ref=land_refs[a].at[q], **sems)))
            return out

        def start_body(*refs):
            part_refs, land_refs, send_sems, recv_sems, token = refs[:n], refs[n:2 * n], refs[2 * n], refs[2 * n + 1], refs[-1]
            for send, _ in copies(part_refs, land_refs, send_sems, recv_sems):
                send.start()
            token[...] = jnp.zeros_like(token)

        def wait_body(*refs):
            part_refs, land_refs, send_sems, recv_sems = refs[:n], refs[n:2 * n], refs[2 * n], refs[2 * n + 1]
            for send, arrive in copies(part_refs, land_refs, send_sems, recv_sems):
                send.wait_send()
                arrive.wait_recv()

        in_hbm = [pltpu.with_memory_space_constraint(p, pltpu.HBM) for p in parts]
        in_hbm += [pltpu.with_memory_space_constraint(lax.empty(s.shape, s.dtype), pltpu.HBM) for s in lands]
        thru_shapes = [pltpu.HBM(p.shape, p.dtype) for p in parts] + [pltpu.HBM(s.shape, s.dtype) for s in lands]
        started = pl.pallas_call(
            start_body, name="scatter_last_start", in_specs=[HBM_SPEC] * (2 * n),
            out_shape=(pltpu.SemaphoreType.DMA((3 * n,)), pltpu.SemaphoreType.DMA((3 * n,)), *thru_shapes, jax.ShapeDtypeStruct((8, 128), F32)),
            out_specs=(sem_spec, sem_spec, *[HBM_SPEC] * (2 * n), pl.BlockSpec(memory_space=pltpu.VMEM)),
            input_output_aliases={i: 2 + i for i in range(2 * n)}, compiler_params=effect)(*in_hbm)
        send_sems, recv_sems, thru, token = started[0], started[1], started[2:2 + 2 * n], started[-1]
        after = work(token)
        done = pl.pallas_call(
            wait_body, name="scatter_last_wait", in_specs=[HBM_SPEC] * (2 * n) + [sem_spec, sem_spec] + [pl.BlockSpec(memory_space=pl.ANY)] * len(after),
            out_shape=tuple(thru_shapes), out_specs=tuple([HBM_SPEC] * (2 * n)), input_output_aliases={i: i for i in range(2 * n)},
            compiler_params=effect)(*thru, send_sems, recv_sems, *after)
        for a, k in enumerate(keys):
            self.staged[k] = done[n + a]
            self.mine[k] = _reduce_piece(done[a], done[n + a], self.me, k[0] in BIG_IN, f"reduce_{k[0]}_{k[1]}")

    def exchange(self, keys):
        mine = [self.mine[k] for k in keys]

        def plan(ins, outs, place):
            x, y, c = place
            return [(ins[a], outs[a], (x, y, 1 - c), outs[a]) for a in range(len(keys))]

        def finish(outs):
            for k, o in zip(keys, outs):
                self.theirs[k] = o

        return _Exchange(mine, [_sds(a) for a in mine], plan, len(keys), None, finish)


def _all_gather_full(gat, keys):
    n = len(keys)
    arrays = [gat.slots[k] for k in keys]
    per = 7

    def body(*refs):
        ins, outs = refs[:n], refs[n:2 * n]
        send_sems, recv_sems = refs[2 * n:]
        x, y, c = _my_place()
        sibling, x_nbr, y_nbr = (x, y, 1 - c), (1 - x, y, c), (x, 1 - y, c)
        me, qx, qy, qd = 2 * x + y, 2 * (1 - x) + y, 2 * x + (1 - y), 2 * (1 - x) + (1 - y)

        def half(ref, q, core):
            rows = ref.shape[1] // 2
            return ref.at[q, pl.ds(core * rows, rows), :]

        def quarter(ref, q, core, k):
            rows = ref.shape[1] // 4
            return ref.at[q, pl.ds((2 * core + k) * rows, rows), :]

        def copy(a, k, src, dst, to):
            return pltpu.make_async_remote_copy(src_ref=src, dst_ref=dst, send_sem=send_sems.at[per * a + k],
                                                recv_sem=recv_sems.at[per * a + k], device_id=to, device_id_type=MESH)

        sent = []

        def send(a, k, part, to):
            cp = copy(a, k, part, part, to)
            cp.start()
            sent.append(cp)

        def landed(a, k, part):
            copy(a, k, part, part, sibling).wait_recv()

        for a in range(n):
            mine_in, mine_out = half(ins[a], me, c), half(outs[a], me, c)
            for k, to in ((0, x_nbr), (1, y_nbr)):
                cp = copy(a, k, mine_in, mine_out, to)
                cp.start()
                sent.append(cp)
        for a in range(n):
            landed(a, 1, half(outs[a], qy, c))
            send(a, 2, quarter(outs[a], qy, c, 0), x_nbr)
            send(a, 5, half(outs[a], qy, c), sibling)
            landed(a, 0, half(outs[a], qx, c))
            send(a, 3, quarter(outs[a], qx, c, 1), y_nbr)
            send(a, 4, half(outs[a], qx, c), sibling)
        for a in range(n):
            landed(a, 2, quarter(outs[a], qd, c, 0))
            landed(a, 3, quarter(outs[a], qd, c, 1))
            send(a, 6, half(outs[a], qd, c), sibling)
        for a in range(n):
            for k, q in ((4, qx), (5, qy), (6, qd)):
                landed(a, k, half(outs[a], q, 1 - c))
        for cp in sent:
            cp.wait_send()

    outs = pl.pallas_call(
        body, name="all_gather_first_weights", in_specs=[HBM_SPEC] * n, out_specs=[HBM_SPEC] * n,
        out_shape=[_sds(a) for a in arrays], input_output_aliases={a: a for a in range(n)},
        scratch_shapes=[pltpu.SemaphoreType.DMA((per * n,)), pltpu.SemaphoreType.DMA((per * n,))])(*arrays)
    for k, o in zip(keys, outs):
        gat.slots[k] = o


def _small_all_gather(buf, done):
    state = {}

    def index(x, y, c):
        return 4 * x + 2 * y + c

    def plan_ici(ins, outs, place):
        x, y, c = place
        return [(ins[0], outs[0].at[index(x, y, c)], (px, py, c), outs[0].at[index(px, py, c)]) for (px, py) in _other_chips(x, y)]

    def local(ins, outs, place):
        return [(ins[0], outs[0].at[index(*place)])]

    def plan_d2d(ins, outs, place):
        x, y, c = place
        return [(ins[0].at[index(px, py, c)], outs[0].at[index(px, py, c)], (x, y, 1 - c), outs[0].at[index(px, py, 1 - c)])
                for (px, py) in [(x, y)] + _other_chips(x, y)]

    def second():
        return _Exchange([state["blocks"]], [_sds(state["blocks"])], plan_d2d, N_CHIPS, {0: 0}, lambda outs: done(outs[0]))

    first = _Exchange([buf], [jax.ShapeDtypeStruct((2 * N_CHIPS,) + buf.shape, buf.dtype)], plan_ici, 3, None,
                      lambda outs: state.update(blocks=outs[0]), local, 1)
    return first, second


WEIGHT_NAMES = ("ffn1_norm", "ffn1_w_in", "ffn1_w_out", "mix_norm", "ffn2_norm", "ffn2_w_in", "ffn2_w_out", "ab_w_in", "pool_w", "pool_b",
                "pool_scale", "conv_w", "conv_b", "conv_ln_g", "conv_ln_b", "ab_w_out", "sgu_w_in", "sgu_ln_g", "sgu_ln_b", "sgu_w", "sgu_b",
                "sgu_w_out", "final_norm")
SMALL = tuple(n for n in WEIGHT_NAMES if n not in BIG)
SHARDED_SMALL = ("conv_w", "sgu_ln_g", "sgu_ln_b")
PACK_ROWS = 64
PACK = ("pack", 0)


def _pair(prefix, layer):
    return [(prefix + "_w_in", layer), (prefix + "_w_out", layer)]


def kernel(x, ffn1_norm, ffn1_w_in, ffn1_w_out, mix_norm, ffn2_norm, ffn2_w_in, ffn2_w_out, ab_w_in, pool_w, pool_b, pool_scale, conv_w, conv_b, conv_ln_g, conv_ln_b, ab_w_out, sgu_w_in, sgu_ln_g, sgu_ln_b, sgu_w, sgu_b, sgu_w_out, final_norm, loss_target, m_ffn1_norm, m_ffn1_w_in, m_ffn1_w_out, m_mix_norm, m_ffn2_norm, m_ffn2_w_in, m_ffn2_w_out, m_ab_w_in, m_pool_w, m_pool_b, m_pool_scale, m_conv_w, m_conv_b, m_conv_ln_g, m_conv_ln_b, m_ab_w_out, m_sgu_w_in, m_sgu_ln_g, m_sgu_ln_b, m_sgu_w, m_sgu_b, m_sgu_w_out, m_final_norm, v_ffn1_norm, v_ffn1_w_in, v_ffn1_w_out, v_mix_norm, v_ffn2_norm, v_ffn2_w_in, v_ffn2_w_out, v_ab_w_in, v_pool_w, v_pool_b, v_pool_scale, v_conv_w, v_conv_b, v_conv_ln_g, v_conv_ln_b, v_ab_w_out, v_sgu_w_in, v_sgu_ln_g, v_sgu_ln_b, v_sgu_w, v_sgu_b, v_sgu_w_out, v_final_norm):
    given = dict(locals())
    w = {n: given[n] for n in WEIGHT_NAMES}
    chip = 2 * lax.axis_index("x") + lax.axis_index("y")
    me = chip.astype(jnp.int32).reshape(1)
    core = lax.axis_index("c").astype(jnp.int32).reshape(1)
    row = lambda v: v.reshape(1, -1)
    xin, tgt = x[0], loss_target[0]

    pack = jnp.concatenate([
        w["conv_w"][0], jnp.zeros((1, 128), F32), w["sgu_ln_g"].reshape(2, 128), w["sgu_ln_b"].reshape(2, 128),
        jnp.zeros((PACK_ROWS - 36, 128), F32)], axis=0)
    slots = {PACK: lax.dynamic_update_slice(jnp.zeros((N_CHIPS, PACK_ROWS, 128), F32), pack[None], (me[0], 0, 0))}
    for n in BIG:
        for layer in range(w[n].shape[0]):
            slots[(n, layer)] = _cast_into_slot(w[n], layer, me, f"cast_{n}_{layer}")
    gat = _Gatherer(slots)
    _all_gather_full(gat, _pair("ffn1", 0) + [PACK])
    gp = gat.slots[PACK]
    conv_w_full = jnp.transpose(gp[:, 0:CONV_WIDTH], (1, 0, 2)).reshape(CONV_WIDTH, N_CHIPS * 128)
    sgu_ln_g_full = gp[:, 32:34].reshape(1, -1)
    sgu_ln_b_full = gp[:, 34:36].reshape(1, -1)
    gw = lambda n, layer: gat.slots[(n, layer)]

    st = [dict(), dict()]
    st[0]["xa"] = xin
    later = _pair("sgu", 0) + _pair("ffn2", 1)
    cur, st[0]["h1"], st[0]["xn1"] = _ffn_fwd(xin, row(w["ffn1_norm"][0]), gw("ffn1_w_in", 0), gw("ffn1_w_out", 0), "ffn1_fwd_0",
                                              comms=[gat.direct(_pair("ab", 0)), gat.ici(_pair("ffn2", 0))])
    st[0]["xb"] = cur
    st[0]["h0"], st[0]["xnm"] = _norm_matmul(cur, row(w["mix_norm"][0]), gw("ab_w_in", 0), "mix0_proj_in",
                                             comms=[gat.d2d(_pair("ffn2", 0)), gat.ici([("ffn1_w_out", 1)])])
    pool_args = (w["pool_w"][0], row(w["pool_b"][0]), row(w["pool_scale"][0]), conv_w_full, row(w["conv_b"][0]), row(w["conv_ln_g"][0]),
                 row(w["conv_ln_b"][0]), gw("ab_w_out", 0))
    cur, st[0]["ycat"], st[0]["yconv"] = _mix0_fwd(cur, st[0]["h0"], *pool_args, "mix0_fwd", comms=[gat.ici([("ffn1_w_in", 1)])])
    st[0]["xc"] = cur
    cur, st[0]["h2"], st[0]["xn2"] = _ffn_fwd(cur, row(w["ffn2_norm"][0]), gw("ffn2_w_in", 0), gw("ffn2_w_out", 0), "ffn2_fwd_0",
                                              comms=[gat.d2d(_pair("ffn1", 1)), gat.ici(later)])
    st[1]["xa"] = cur
    cur, st[1]["h1"], st[1]["xn1"] = _ffn_fwd(cur, row(w["ffn1_norm"][1]), gw("ffn1_w_in", 1), gw("ffn1_w_out", 1), "ffn1_fwd_1",
                                              comms=[gat.d2d(later)])
    st[1]["xb"] = cur
    st[1]["pre"], st[1]["xnm"] = _norm_matmul(cur, row(w["mix_norm"][1]), gw("sgu_w_in", 0), "sgu_proj_in")
    sgu_args = (sgu_ln_g_full, sgu_ln_b_full, w["sgu_w"][0], w["sgu_b"][0].T)
    cur, st[1]["p"] = _sgu_fwd(cur, st[1]["pre"], *sgu_args, gw("sgu_w_out", 0), "sgu_fwd")
    st[1]["xc"] = cur
    cur, st[1]["h2"], st[1]["xn2"] = _ffn_fwd(cur, row(w["ffn2_norm"][1]), gw("ffn2_w_in", 1), gw("ffn2_w_out", 1), "ffn2_fwd_1")
    dy, loss, d_final = _final_loss(cur, tgt, row(w["final_norm"]), "final_loss")

    red = _Reducer(me, core)
    small = {"final_norm": d_final.reshape(-1)}
    norm_grads = {"ffn1_norm": [None] * DEPTH, "mix_norm": [None] * DEPTH, "ffn2_norm": [None] * DEPTH}
    ga, gb, gc, gd, ge, gf = _pair("ffn2", 1), _pair("sgu", 0), _pair("ffn1", 1), _pair("ffn2", 0), _pair("ab", 0), _pair("ffn1", 0)

    def ffn_backward(prefix, layer, xs, hs, xns, dy_in, bwd_comms=(), dwin_comms=(), dwout_comms=()):
        dx, dh, act, norm_grads[prefix + "_norm"][layer] = _ffn_bwd(
            xs, dy_in, hs, row(w[prefix + "_norm"][layer]), gw(prefix + "_w_in", layer), gw(prefix + "_w_out", layer),
            f"{prefix}_bwd_{layer}", comms=bwd_comms)
        red.add((prefix + "_w_in", layer), _tn_matmul(xns, dh, 1.0, 1024, 1408, f"{prefix}_dwin_{layer}", comms=dwin_comms))
        red.add((prefix + "_w_out", layer), _tn_matmul(act, dy_in, 0.5, 1408, 1024, f"{prefix}_dwout_{layer}", comms=dwout_comms))
        return dx

    s1, s0 = st[1], st[0]
    dy = ffn_backward("ffn2", 1, s1["xc"], s1["h2"], s1["xn2"], dy)
    dy_in = dy
    dy, dpre, norm_grads["mix_norm"][1], dlg, dlb, dw, dbt = _sgu_bwd(
        s1["xb"], dy_in, s1["pre"], row(w["mix_norm"][1]), *sgu_args, gw("sgu_w_in", 0), gw("sgu_w_out", 0), "sgu_bwd", comms=[red.swap(ga)])
    red.add(("sgu_w_in", 0), _tn_matmul(s1["xnm"], dpre, 1.0, 1024, 2048, "sgu_dwin"))
    red.add(("sgu_w_out", 0), _tn_matmul(s1["p"], dy_in, 1.0, 1024, 1024, "sgu_dwout"))
    small.update(sgu_ln_g=dlg, sgu_ln_b=dlb, sgu_w=dw[None], sgu_b=dbt.T[None])
    dy = ffn_backward("ffn1", 1, s1["xa"], s1["h1"], s1["xn1"], dy, bwd_comms=[lambda: red.scatter(ga), lambda: red.swap(gb)],
                      dwin_comms=[lambda: red.scatter(gb), lambda: red.exchange(ga)])
    dy = ffn_backward("ffn2", 0, s0["xc"], s0["h2"], s0["xn2"], dy, bwd_comms=[lambda: red.swap(gc), lambda: red.exchange(gb)],
                      dwin_comms=[lambda: red.scatter(gc)])
    dy_in = dy
    dconv, dpc, dpw, rows = _mix0_bwd_a(dy_in, s0["h0"], s0["yconv"], *pool_args, "mix0_bwd_a")
    dy, dh0, norm_grads["mix_norm"][0] = _mix0_bwd_b(s0["xb"], dy_in, s0["h0"], dconv, dpc, row(w["mix_norm"][0]), conv_w_full,
                                                      gw("ab_w_in", 0), "mix0_bwd_b")
    red.add(("ab_w_in", 0), _tn_matmul(s0["xnm"], dh0, 1.0, 1024, 1536, "ab_dwin", comms=[red.swap(gd), red.exchange(gc)]))
    red.add(("ab_w_out", 0), _tn_matmul(s0["ycat"], dy_in, 1.0, 1024, 1024, "ab_dwout"))
    small.update(pool_w=dpw[None], conv_w=rows[None, 0:CONV_WIDTH], conv_b=rows[32:33], conv_ln_g=rows[33:34], conv_ln_b=rows[34:35],
                 pool_scale=rows[35:36], pool_b=rows[36:37].reshape(1, len(POOL_WINDOWS), POOL_GC))

    small_sum = {}

    def small_ready():
        for k, v in norm_grads.items():
            small[k] = jnp.concatenate(v, axis=0)
        flat = [small[n].reshape(-1, 128) for n in SMALL]
        rows = sum(f.shape[0] for f in flat)
        loss_block = jnp.pad(loss, ((0, 8 + (-rows) % 8 - 1), (0, 127)))
        buf = jnp.concatenate(flat + [loss_block], axis=0)

        def done(gathered):
            total, at = _sum_leading(gathered, "reduce_small"), 0
            for n, f in zip(SMALL, flat):
                small_sum[n] = total[at:at + f.shape[0]].reshape(small[n].shape)
                at += f.shape[0]
            small_sum["loss"] = total[at:at + 1, 0:1]

        return _small_all_gather(buf, done)

    dx, dh, act, norm_grads["ffn1_norm"][0] = _ffn_bwd(
        s0["xa"], dy, s0["h1"], row(w["ffn1_norm"][0]), gw("ffn1_w_in", 0), gw("ffn1_w_out", 0), "ffn1_bwd_0")
    small_first, small_second = small_ready()
    red.add(("ffn1_w_in", 0), _tn_matmul(s0["xn1"], dh, 1.0, 1024, 1408, "ffn1_dwin_0", comms=[red.scatter(gd), red.swap(ge)]))
    red.add(("ffn1_w_out", 0), _tn_matmul(act, dy, 0.5, 1408, 1024, "ffn1_dwout_0",
                                          comms=[red.scatter(ge), red.exchange(gd), small_first]))
    grad_x = dx

    big_out = {}

    def adamw_big(n, layer, after=()):
        big_out[n] = _adamw_sharded(w[n], red.mine[(n, layer)], red.theirs[(n, layer)], given["m_" + n], given["v_" + n], core, layer,
                                    big_out.get(n), f"adamw_{n}_{layer}", after=after)

    _exchange_alone("swap_last_grads", [red.swap(gf), red.exchange(ge), small_second])

    def other_updates(token):
        for n in BIG:
            for layer in reversed(range(w[n].shape[0])):
                if (n, layer) not in gf:
                    adamw_big(n, layer, after=[token])
        return [big_out[n][0] for n in BIG]

    red.scatter_behind(gf, other_updates)
    _exchange_alone("exchange_last_grads", [red.exchange(gf)])
    for key in gf:
        adamw_big(*key)

    loss = small_sum["loss"][0, 0]
    grads, delta, new_m, new_v = {}, {}, {}, {}
    for n in WEIGHT_NAMES:
        mom, var = given["m_" + n], given["v_" + n]
        if n in BIG:
            grads[n], delta[n], new_m[n], new_v[n] = big_out[n]
            continue
        g = small_sum[n]
        if n in SHARDED_SMALL:
            width = w[n].shape[-1]
            g = lax.dynamic_slice_in_dim(g, chip * width, width, axis=g.ndim - 1)
        grads[n] = g
        delta[n], new_m[n], new_v[n] = _adamw(w[n], g, mom, var, f"adamw_{n}")
    return (loss, grad_x[None], *[grads[n] for n in WEIGHT_NAMES], *[delta[n] for n in WEIGHT_NAMES],
            *[new_m[n] for n in WEIGHT_NAMES], *[new_v[n] for n in WEIGHT_NAMES])
```

```python
import jax
import jax.numpy as jnp
from jax import lax
from jax.experimental import pallas as pl
from jax.experimental.pallas import tpu as pltpu

F32, BF16 = jnp.float32, jnp.bfloat16
EPS = 1e-6
N_CHIPS = 4
POOL_WINDOWS = (2, 4, 8, 16)
POOL_GC = 128
POOL_CH = 512
CONV_CH = 512
CONV_WIDTH = 31
HALO = 32
SGU_HEADS = 8
CHUNK = 128
DEPTH = 2
ADAM_LR, ADAM_B1, ADAM_B2, ADAM_EPS, ADAM_WD, ADAM_STEP = 0.001, 0.9, 0.999, 1e-08, 0.01, 10
VMEM_LIMIT_BYTES = 60 * 1024 * 1024
MESH_AXES = ("x", "y", "c")
MESH = pl.DeviceIdType.MESH
HBM_SPEC = pl.BlockSpec(memory_space=pltpu.HBM)


def _sds(a):
    return jax.ShapeDtypeStruct(a.shape, a.dtype)


def _cparams_nd(n):
    return pltpu.CompilerParams(dimension_semantics=("arbitrary",) * n, vmem_limit_bytes=VMEM_LIMIT_BYTES)


def _cparams():
    return _cparams_nd(1)


def _dot(a, b):
    return jnp.dot(a, b, preferred_element_type=F32)


def _dot_nt(a, b):
    return lax.dot_general(a, b, (((1,), (1,)), ((), ())), preferred_element_type=F32)


def _dot_tn(a, b):
    return lax.dot_general(a, b, (((0,), (0,)), ((), ())), preferred_element_type=F32)


def _rms_fwd(x):
    r = lax.rsqrt(jnp.mean(x * x, axis=-1, keepdims=True) + EPS)
    return x * r, r


def _rms_bwd(dxn, xh, r, g):
    dxh = dxn * g
    return r * (dxh - xh * jnp.mean(dxh * xh, axis=-1, keepdims=True))


def _ln_fwd(y):
    mu = jnp.mean(y, axis=-1, keepdims=True)
    yc = y - mu
    rs = lax.rsqrt(jnp.mean(yc * yc, axis=-1, keepdims=True) + EPS)
    return yc * rs, rs


def _ln_bwd(dyhat, yhat, rs):
    return rs * (dyhat - jnp.mean(dyhat, axis=-1, keepdims=True) - yhat * jnp.mean(dyhat * yhat, axis=-1, keepdims=True))


def _sigmoid(x):
    return 0.5 * jnp.tanh(0.5 * x) + 0.5


def _const_spec(shape):
    n = len(shape)
    return pl.BlockSpec(shape, lambda i: (0,) * n)


def _row_spec(tm, cols):
    return pl.BlockSpec((tm, cols), lambda i: (i, 0))


def _my_place():
    return lax.axis_index("x"), lax.axis_index("y"), lax.axis_index("c")


def _other_chips(x, y):
    return [(1 - x, y), (x, 1 - y), (1 - x, 1 - y)]


class _Exchange:
    def __init__(self, inputs, out_shapes, plan, count, aliases=None, finish=None, local=None, n_local=0):
        self.inputs, self.out_shapes, self.plan, self.count = list(inputs), list(out_shapes), plan, count
        self.aliases, self.finish, self.local, self.n_local = dict(aliases or {}), finish, local, n_local


def _call(body, *, name, grid, in_specs, out_specs, out_shape, args, scratch_shapes=(), comms=(), scalar=None, aliases=None):
    comms = [cm if isinstance(cm, _Exchange) else cm() for cm in comms]
    in_specs, out_specs, out_shape, scratch_shapes = list(in_specs), list(out_specs), list(out_shape), list(scratch_shapes)
    n_in, n_out, n_scr = len(in_specs), len(out_specs), len(scratch_shapes)
    c_in = [a for cm in comms for a in cm.inputs]
    c_out = [s for cm in comms for s in cm.out_shapes]
    n_remote = sum(cm.count for cm in comms)
    n_local = sum(cm.n_local for cm in comms)
    n_scalar = 0 if scalar is None else 1
    all_aliases = {n_scalar + i: o for i, o in (aliases or {}).items()}
    at_in, at_out = n_scalar + n_in, n_out
    for cm in comms:
        for i, o in cm.aliases.items():
            all_aliases[at_in + i] = at_out + o
        at_in += len(cm.inputs)
        at_out += len(cm.out_shapes)

    def wrapped(*all_refs):
        scalar_ref, refs = all_refs[:n_scalar], all_refs[n_scalar:]
        ins, ci = refs[:n_in], refs[n_in:n_in + len(c_in)]
        at = n_in + len(c_in)
        outs, co = refs[at:at + n_out], refs[at + n_out:at + n_out + len(c_out)]
        at += n_out + len(c_out)
        scr = refs[at:at + n_scr]

        def run_body():
            body(*scalar_ref, *ins, *outs, *scr)

        if not comms:
            run_body()
            return
        send_sems, recv_sems, local_sems = refs[at + n_scr:]
        place = _my_place()
        sends, arrivals, locals_ = [], [], []
        i0 = o0 = 0
        for cm in comms:
            cm_in, cm_out = ci[i0:i0 + len(cm.inputs)], co[o0:o0 + len(cm.out_shapes)]
            i0 += len(cm.inputs)
            o0 += len(cm.out_shapes)
            for src, dst, dev, incoming in cm.plan(cm_in, cm_out, place):
                k = len(sends)
                sends.append(pltpu.make_async_remote_copy(src_ref=src, dst_ref=dst, send_sem=send_sems.at[k], recv_sem=recv_sems.at[k],
                                                          device_id=dev, device_id_type=MESH))
                arrivals.append(pltpu.make_async_remote_copy(src_ref=src, dst_ref=incoming, send_sem=send_sems.at[k],
                                                             recv_sem=recv_sems.at[k], device_id=dev, device_id_type=MESH))
            if cm.local is not None:
                for src, dst in cm.local(cm_in, cm_out, place):
                    locals_.append(pltpu.make_async_copy(src, dst, local_sems.at[len(locals_)]))

        def start():
            for cp in locals_ + sends:
                cp.start()

        def finish():
            for cp in arrivals:
                cp.wait_recv()
            for cp in sends:
                cp.wait_send()
            for cp in locals_:
                cp.wait()

        if not grid:
            start()
            run_body()
            finish()
            return
        ids = [pl.program_id(a) for a in range(len(grid))]
        first, last = ids[0] == 0, ids[0] == grid[0] - 1
        for a in range(1, len(grid)):
            first = jnp.logical_and(first, ids[a] == 0)
            last = jnp.logical_and(last, ids[a] == grid[a] - 1)
        pl.when(first)(start)
        run_body()
        pl.when(last)(finish)

    sems = []
    if comms:
        sems = [pltpu.SemaphoreType.DMA((max(n_remote, 1),)), pltpu.SemaphoreType.DMA((max(n_remote, 1),)),
                pltpu.SemaphoreType.DMA((max(n_local, 1),))]
    all_in, all_out = in_specs + [HBM_SPEC] * len(c_in), out_specs + [HBM_SPEC] * len(c_out)
    if scalar is None:
        kwargs = dict(grid=grid, compiler_params=_cparams_nd(len(grid))) if grid else {}
        res = pl.pallas_call(
            wrapped, name=name, in_specs=all_in, out_specs=all_out, out_shape=out_shape + c_out, scratch_shapes=scratch_shapes + sems,
            input_output_aliases=all_aliases, **kwargs)(*args, *c_in)
    else:
        spec = pltpu.PrefetchScalarGridSpec(num_scalar_prefetch=1, grid=grid, in_specs=all_in, out_specs=all_out,
                                            scratch_shapes=scratch_shapes + sems)
        res = pl.pallas_call(
            wrapped, name=name, grid_spec=spec, out_shape=out_shape + c_out, input_output_aliases=all_aliases,
            compiler_params=_cparams_nd(len(grid)))(scalar, *args, *c_in)
    at = n_out
    for cm in comms:
        got = res[at:at + len(cm.out_shapes)]
        at += len(cm.out_shapes)
        if cm.finish is not None:
            cm.finish(got)
    return list(res[:n_out])


def _exchange_alone(name, comms):
    _call(lambda: None, name=name, grid=(), in_specs=[], out_specs=[], out_shape=[], args=[], comms=comms)


def _in_weight_copies(w_hbm, w_v, sem, base=0):
    return [pltpu.make_async_copy(w_hbm.at[q], w_v.at[q], sem.at[base + q]) for q in range(N_CHIPS)]


def _out_weight_copies(w_hbm, w_v, sem, base=0):
    rows = w_hbm.shape[1]
    return [pltpu.make_async_copy(w_hbm.at[q], w_v.at[pl.ds(q * rows, rows)], sem.at[base + q]) for q in range(N_CHIPS)]


def _load_at_first_step(copies):
    @pl.when(pl.program_id(0) == 0)
    def _():
        for cp in copies:
            cp.start()
        for cp in copies:
            cp.wait()


def _ffn_fwd(x, g, win_g, wout_g, name, comms=()):
    t, d = x.shape
    c = win_g.shape[-1]
    ff = 2 * c
    tm = min(512, t)

    def body(x_ref, g_ref, win_hbm, wout_hbm, xo_ref, h_ref, xn_ref, win_v, wout_v, sem):
        _load_at_first_step(_in_weight_copies(win_hbm, win_v, sem) + _out_weight_copies(wout_hbm, wout_v, sem, N_CHIPS))
        xv = x_ref[...]
        xh, _ = _rms_fwd(xv)
        xn = (xh * g_ref[...]).astype(BF16)
        xn_ref[...] = xn
        acc = jnp.zeros((tm, d), F32)
        for j in range(2):
            gate = _dot(xn, win_v[j])
            up = _dot(xn, win_v[j + 2])
            h_ref[:, j * c:(j + 1) * c] = gate.astype(BF16)
            h_ref[:, ff + j * c:ff + (j + 1) * c] = up.astype(BF16)
            act = (gate * _sigmoid(gate) * up).astype(BF16)
            acc = acc + _dot(act, wout_v[j * c:(j + 1) * c, :])
        xo_ref[...] = xv + 0.5 * acc

    return _call(
        body, name=name, grid=(t // tm,),
        in_specs=[_row_spec(tm, d), _const_spec((1, d)), HBM_SPEC, HBM_SPEC],
        out_specs=[_row_spec(tm, d), _row_spec(tm, 2 * ff), _row_spec(tm, d)],
        out_shape=[jax.ShapeDtypeStruct((t, d), F32), jax.ShapeDtypeStruct((t, 2 * ff), BF16), jax.ShapeDtypeStruct((t, d), BF16)],
        scratch_shapes=[pltpu.VMEM((N_CHIPS, d, c), BF16), pltpu.VMEM((ff, d), BF16), pltpu.SemaphoreType.DMA((2 * N_CHIPS,))],
        args=(x, g, win_g, wout_g), comms=comms)


def _ffn_bwd(x, dy, h, g, win_g, wout_g, name, comms=()):
    t, d = x.shape
    c = win_g.shape[-1]
    ff = 2 * c
    tm = min(256, t)

    def body(x_ref, dy_ref, h_ref, g_ref, win_hbm, wout_hbm, dx_ref, dh_ref, act_ref, dg_ref, win_v, wout_v, sem):
        _load_at_first_step(_in_weight_copies(win_hbm, win_v, sem) + _out_weight_copies(wout_hbm, wout_v, sem, N_CHIPS))

        @pl.when(pl.program_id(0) == 0)
        def _():
            dg_ref[...] = jnp.zeros_like(dg_ref)

        xv, dyv, gv = x_ref[...], dy_ref[...], g_ref[...]
        xh, r = _rms_fwd(xv)
        dyh = (0.5 * dyv).astype(BF16)
        dxn = jnp.zeros((tm, d), F32)
        for j in range(2):
            gate = h_ref[:, j * c:(j + 1) * c].astype(F32)
            up = h_ref[:, ff + j * c:ff + (j + 1) * c].astype(F32)
            dact = _dot_nt(dyh, wout_v[j * c:(j + 1) * c, :])
            s = _sigmoid(gate)
            sl = gate * s
            act_ref[:, j * c:(j + 1) * c] = (sl * up).astype(BF16)
            dgate = (dact * up * (s + sl * (1.0 - s))).astype(BF16)
            dup = (dact * sl).astype(BF16)
            dh_ref[:, j * c:(j + 1) * c] = dgate
            dh_ref[:, ff + j * c:ff + (j + 1) * c] = dup
            dxn = dxn + _dot_nt(dgate, win_v[j]) + _dot_nt(dup, win_v[j + 2])
        dg_ref[...] += jnp.sum(dxn * xh, axis=0, keepdims=True)
        dx_ref[...] = dyv + _rms_bwd(dxn, xh, r, gv)

    return _call(
        body, name=name, grid=(t // tm,),
        in_specs=[_row_spec(tm, d), _row_spec(tm, d), _row_spec(tm, 2 * ff), _const_spec((1, d)), HBM_SPEC, HBM_SPEC],
        out_specs=[_row_spec(tm, d), _row_spec(tm, 2 * ff), _row_spec(tm, ff), _const_spec((1, d))],
        out_shape=[jax.ShapeDtypeStruct((t, d), F32), jax.ShapeDtypeStruct((t, 2 * ff), BF16), jax.ShapeDtypeStruct((t, ff), BF16),
                   jax.ShapeDtypeStruct((1, d), F32)],
        scratch_shapes=[pltpu.VMEM((N_CHIPS, d, c), BF16), pltpu.VMEM((ff, d), BF16), pltpu.SemaphoreType.DMA((2 * N_CHIPS,))],
        args=(x, dy, h, g, win_g, wout_g), comms=comms)


def _norm_matmul(x, g, win_g, name, comms=()):
    t, d = x.shape
    c = win_g.shape[-1]
    tm = min(512, t)

    def body(x_ref, g_ref, win_hbm, o_ref, xn_ref, win_v, sem):
        _load_at_first_step(_in_weight_copies(win_hbm, win_v, sem))
        xh, _ = _rms_fwd(x_ref[...])
        xn = (xh * g_ref[...]).astype(BF16)
        xn_ref[...] = xn
        for q in range(N_CHIPS):
            o_ref[:, q * c:(q + 1) * c] = _dot(xn, win_v[q])

    return _call(
        body, name=name, grid=(t // tm,),
        in_specs=[_row_spec(tm, d), _const_spec((1, d)), HBM_SPEC],
        out_specs=[_row_spec(tm, N_CHIPS * c), _row_spec(tm, d)],
        out_shape=[jax.ShapeDtypeStruct((t, N_CHIPS * c), F32), jax.ShapeDtypeStruct((t, d), BF16)],
        scratch_shapes=[pltpu.VMEM((N_CHIPS, d, c), BF16), pltpu.SemaphoreType.DMA((N_CHIPS,))],
        args=(x, g, win_g), comms=comms)


def _proj_in_bwd_tail(dh, win_v, c):
    dxn = _dot_nt(dh[:, 0:c], win_v[0])
    for q in range(1, N_CHIPS):
        dxn = dxn + _dot_nt(dh[:, q * c:(q + 1) * c], win_v[q])
    return dxn


def _prev_halo_spec(tm, cols):
    return pl.BlockSpec((HALO, cols), lambda i: (jnp.maximum(i * (tm // HALO) - 1, 0), 0))


def _next_halo_spec(tm, cols, t):
    last = t // HALO - 1
    return pl.BlockSpec((HALO, cols), lambda i: (jnp.minimum((i + 1) * (tm // HALO), last), 0))


def _shift_down(w, k):
    return w if k == 0 else pltpu.roll(w, k, 0)


def _shift_up(w, k):
    return w if k == 0 else pltpu.roll(w, w.shape[0] - k, 0)


def _pool_counts(i, tm):
    pos = (i * tm + lax.broadcasted_iota(jnp.int32, (tm, POOL_CH), 0) + 1).astype(F32)
    lane = lax.broadcasted_iota(jnp.int32, (tm, POOL_CH), 1)
    win = jnp.where(lane < POOL_GC, 2.0, jnp.where(lane < 2 * POOL_GC, 4.0, jnp.where(lane < 3 * POOL_GC, 8.0, 16.0)))
    return jnp.minimum(pos, win)


def _group_select(parts):
    return jnp.concatenate([p[:, k * POOL_GC:(k + 1) * POOL_GC] for k, p in enumerate(parts)], axis=1)


TAP_ROWS = 128


def _tap_blocks(tm, block_fn):
    cols = []
    for ch in range(CONV_CH // POOL_GC):
        lanes = slice(ch * POOL_GC, (ch + 1) * POOL_GC)
        cols.append(jnp.concatenate([block_fn(r, lanes) for r in range(tm // TAP_ROWS)], axis=0))
    return jnp.concatenate(cols, axis=1)


def _conv_block(win, taps):
    acc = jnp.zeros((TAP_ROWS, win.shape[1]), F32)
    for k in range(CONV_WIDTH):
        acc = acc + taps[k:k + 1, :] * _shift_down(win, CONV_WIDTH - 1 - k)[HALO:, :]
    return acc


def _conv_block_transposed(win, taps):
    acc = jnp.zeros((TAP_ROWS, win.shape[1]), F32)
    for k in range(CONV_WIDTH):
        acc = acc + taps[k:k + 1, :] * _shift_up(win, CONV_WIDTH - 1 - k)[0:TAP_ROWS, :]
    return acc


def _mix0_recompute(i, tm, h_cur, h_prev, conv_w, conv_b, y=None):
    prev = jnp.where(i > 0, h_prev, 0.0)
    win = jnp.concatenate([prev, h_cur], axis=0)
    u_w = win[:, 0:POOL_CH]
    a_w = win[:, POOL_CH:POOL_CH + CONV_CH]
    gt_w = win[:, POOL_CH + CONV_CH:]
    g_w = a_w * _sigmoid(gt_w)
    if y is None:
        y = _tap_blocks(tm, lambda r, lanes: _conv_block(g_w[r * TAP_ROWS:(r + 1) * TAP_ROWS + HALO, lanes], conv_w[:, lanes])) + conv_b
    s2 = u_w + _shift_down(u_w, 1)
    s4 = s2 + _shift_down(s2, 2)
    s8 = s4 + _shift_down(s4, 4)
    s16 = s8 + _shift_down(s8, 8)
    sums = _group_select([s2[HALO:], s4[HALO:], s8[HALO:], s16[HALO:]])
    cnt = _pool_counts(i, tm)
    pooled = sums / cnt - h_cur[:, 0:POOL_CH]
    return g_w, y, pooled, cnt


def _pool_linear(pooled, pw_ref, pb):
    return jnp.concatenate(
        [_dot(pooled[:, k * POOL_GC:(k + 1) * POOL_GC].astype(BF16), pw_ref[k].astype(BF16)) for k in range(len(POOL_WINDOWS))], axis=1) + pb


def _mix0_fwd(x, h0, pool_w, pool_b, pool_scale, conv_w, conv_b, ln_g, ln_b, wout_g, name, comms=()):
    t, d = x.shape
    tm = min(512, t)
    hc = h0.shape[1]

    def body(x_ref, h_ref, hp_ref, pw_ref, pb_ref, ps_ref, cw_ref, cb_ref, lg_ref, lb_ref, wout_hbm, xo_ref, ycat_ref, y_ref, wout_v, sem):
        _load_at_first_step(_out_weight_copies(wout_hbm, wout_v, sem))
        i = pl.program_id(0)
        _, y, pooled, _ = _mix0_recompute(i, tm, h_ref[...], hp_ref[...], cw_ref[...], cb_ref[...])
        y_ref[...] = y
        yhat, _ = _ln_fwd(y)
        yn = yhat * lg_ref[...] + lb_ref[...]
        yb = yn * _sigmoid(yn)
        ya = _pool_linear(pooled, pw_ref, pb_ref[...]) * ps_ref[...]
        ycat = jnp.concatenate([ya, yb], axis=1).astype(BF16)
        ycat_ref[...] = ycat
        xo_ref[...] = x_ref[...] + _dot(ycat, wout_v[...])

    return _call(
        body, name=name, grid=(t // tm,),
        in_specs=[_row_spec(tm, d), _row_spec(tm, hc), _prev_halo_spec(tm, hc), _const_spec(pool_w.shape), _const_spec((1, POOL_CH)),
                  _const_spec((1, POOL_CH)), _const_spec(conv_w.shape), _const_spec((1, CONV_CH)), _const_spec((1, CONV_CH)),
                  _const_spec((1, CONV_CH)), HBM_SPEC],
        out_specs=[_row_spec(tm, d), _row_spec(tm, d), _row_spec(tm, CONV_CH)],
        out_shape=[jax.ShapeDtypeStruct((t, d), F32), jax.ShapeDtypeStruct((t, d), BF16), jax.ShapeDtypeStruct((t, CONV_CH), F32)],
        scratch_shapes=[pltpu.VMEM((d, d), BF16), pltpu.SemaphoreType.DMA((N_CHIPS,))],
        args=(x, h0, h0, pool_w, pool_b, pool_scale, conv_w, conv_b, ln_g, ln_b, wout_g), comms=comms)


def _mix0_bwd_a(dy, h0, y_conv, pool_w, pool_b, pool_scale, conv_w, conv_b, ln_g, ln_b, wout_g, name):
    t, d = dy.shape
    tm = min(512, t)
    hc = h0.shape[1]
    n_small = 40

    def body(dy_ref, h_ref, hp_ref, y_ref, pw_ref, pb_ref, ps_ref, cw_ref, cb_ref, lg_ref, lb_ref, wout_hbm,
             dconv_ref, dpc_ref, dpw_ref, small_ref, wout_v, sem):
        _load_at_first_step(_out_weight_copies(wout_hbm, wout_v, sem))
        i = pl.program_id(0)

        @pl.when(i == 0)
        def _():
            dpw_ref[...] = jnp.zeros_like(dpw_ref)
            small_ref[...] = jnp.zeros_like(small_ref)

        g_w, y, pooled, cnt = _mix0_recompute(i, tm, h_ref[...], hp_ref[...], cw_ref[...], cb_ref[...], y_ref[...])
        yhat, rs = _ln_fwd(y)
        lg = lg_ref[...]
        yn = yhat * lg + lb_ref[...]
        mixed = _pool_linear(pooled, pw_ref, pb_ref[...])
        dycat = _dot_nt(dy_ref[...].astype(BF16), wout_v[...])
        dya, dyb = dycat[:, 0:POOL_CH], dycat[:, POOL_CH:]
        sg = _sigmoid(yn)
        dyn = dyb * (sg * (1.0 + yn * (1.0 - sg)))
        dyc = _ln_bwd(dyn * lg, yhat, rs)
        dconv_ref[...] = dyc

        def add_row(k, value):
            small_ref[k:k + 1, :] += jnp.sum(value, axis=0, keepdims=True)

        for ch in range(CONV_CH // POOL_GC):
            lanes = slice(ch * POOL_GC, (ch + 1) * POOL_GC)
            taps = [jnp.zeros((1, POOL_GC), F32)] * CONV_WIDTH
            for r in range(tm // TAP_ROWS):
                win = g_w[r * TAP_ROWS:(r + 1) * TAP_ROWS + HALO, lanes]
                d = dyc[r * TAP_ROWS:(r + 1) * TAP_ROWS, lanes]
                for k in range(CONV_WIDTH):
                    taps[k] = taps[k] + jnp.sum(d * _shift_down(win, CONV_WIDTH - 1 - k)[HALO:, :], axis=0, keepdims=True)
            for k in range(CONV_WIDTH):
                small_ref[k:k + 1, lanes] += taps[k]
        add_row(32, dyc)
        add_row(33, dyn * yhat)
        add_row(34, dyn)
        scale = ps_ref[...]
        dmixed = dya * scale
        add_row(35, dya * mixed)
        add_row(36, dmixed)
        dmb = dmixed.astype(BF16)
        dpooled = []
        for k in range(len(POOL_WINDOWS)):
            sl = slice(k * POOL_GC, (k + 1) * POOL_GC)
            dpw_ref[k] += _dot_tn(pooled[:, sl].astype(BF16), dmb[:, sl])
            dpooled.append(_dot_nt(dmb[:, sl], pw_ref[k].astype(BF16)))
        dpc_ref[...] = jnp.concatenate(dpooled, axis=1) / cnt

    return _call(
        body, name=name, grid=(t // tm,),
        in_specs=[_row_spec(tm, d), _row_spec(tm, hc), _prev_halo_spec(tm, hc), _row_spec(tm, CONV_CH), _const_spec(pool_w.shape),
                  _const_spec((1, POOL_CH)), _const_spec((1, POOL_CH)), _const_spec(conv_w.shape), _const_spec((1, CONV_CH)),
                  _const_spec((1, CONV_CH)), _const_spec((1, CONV_CH)), HBM_SPEC],
        out_specs=[_row_spec(tm, CONV_CH), _row_spec(tm, POOL_CH), _const_spec(pool_w.shape), _const_spec((n_small, CONV_CH))],
        out_shape=[jax.ShapeDtypeStruct((t, CONV_CH), F32), jax.ShapeDtypeStruct((t, POOL_CH), F32),
                   jax.ShapeDtypeStruct(pool_w.shape, F32), jax.ShapeDtypeStruct((n_small, CONV_CH), F32)],
        scratch_shapes=[pltpu.VMEM((d, d), BF16), pltpu.SemaphoreType.DMA((N_CHIPS,))],
        args=(dy, h0, h0, y_conv, pool_w, pool_b, pool_scale, conv_w, conv_b, ln_g, ln_b, wout_g))


def _mix0_bwd_b(x, dy, h0, dconv, dpc, g, conv_w, win_g, name):
    t, d = x.shape
    tm = min(512, t)
    hc = h0.shape[1]
    c = win_g.shape[-1]
    n_tiles = t // tm

    def body(x_ref, dy_ref, h_ref, dc_ref, dcn_ref, dp_ref, dpn_ref, g_ref, cw_ref, win_hbm, dx_ref, dh_ref, dg_ref, win_v, sem):
        _load_at_first_step(_in_weight_copies(win_hbm, win_v, sem))
        i = pl.program_id(0)

        @pl.when(i == 0)
        def _():
            dg_ref[...] = jnp.zeros_like(dg_ref)

        not_last = i < n_tiles - 1
        dc_w = jnp.concatenate([dc_ref[...], jnp.where(not_last, dcn_ref[...], 0.0)], axis=0)
        dp_w = jnp.concatenate([dp_ref[...], jnp.where(not_last, dpn_ref[...], 0.0)], axis=0)
        cw = cw_ref[...]
        dg = _tap_blocks(tm, lambda r, lanes: _conv_block_transposed(dc_w[r * TAP_ROWS:(r + 1) * TAP_ROWS + HALO, lanes], cw[:, lanes]))
        a2 = dp_w + _shift_up(dp_w, 1)
        a4 = a2 + _shift_up(a2, 2)
        a8 = a4 + _shift_up(a4, 4)
        a16 = a8 + _shift_up(a8, 8)
        back = _group_select([a2[0:tm], a4[0:tm], a8[0:tm], a16[0:tm]])
        du = back - dp_ref[...] * _pool_counts(i, tm)
        hv = h_ref[...]
        a = hv[:, POOL_CH:POOL_CH + CONV_CH]
        sig = _sigmoid(hv[:, POOL_CH + CONV_CH:])
        dh = jnp.concatenate([du, dg * sig, dg * a * sig * (1.0 - sig)], axis=1).astype(BF16)
        dh_ref[...] = dh
        dxn = _proj_in_bwd_tail(dh, win_v, c)
        xh, r = _rms_fwd(x_ref[...])
        dg_ref[...] += jnp.sum(dxn * xh, axis=0, keepdims=True)
        dx_ref[...] = dy_ref[...] + _rms_bwd(dxn, xh, r, g_ref[...])

    return _call(
        body, name=name, grid=(n_tiles,),
        in_specs=[_row_spec(tm, d), _row_spec(tm, d), _row_spec(tm, hc), _row_spec(tm, CONV_CH), _next_halo_spec(tm, CONV_CH, t),
                  _row_spec(tm, POOL_CH), _next_halo_spec(tm, POOL_CH, t), _const_spec((1, d)), _const_spec(conv_w.shape), HBM_SPEC],
        out_specs=[_row_spec(tm, d), _row_spec(tm, hc), _const_spec((1, d))],
        out_shape=[jax.ShapeDtypeStruct((t, d), F32), jax.ShapeDtypeStruct((t, hc), BF16), jax.ShapeDtypeStruct((1, d), F32)],
        scratch_shapes=[pltpu.VMEM((N_CHIPS, d, c), BF16), pltpu.SemaphoreType.DMA((N_CHIPS,))],
        args=(x, dy, h0, dconv, dconv, dpc, dpc, g, conv_w, win_g))


SQRT_HALF = 0.7071067811865476
INV_SQRT_2PI = 0.3989422804014327


def _causal_mask():
    return (lax.broadcasted_iota(jnp.int32, (CHUNK, CHUNK), 1) <= lax.broadcasted_iota(jnp.int32, (CHUNK, CHUNK), 0)).astype(F32)


def _sgu_recompute(pre, lg, lb):
    half = pre.shape[1] // 2
    phi = 0.5 * (1.0 + lax.erf(pre * SQRT_HALF))
    z = pre * phi
    u, v = z[:, 0:half], z[:, half:]
    vhat, rs = _ln_fwd(v)
    return u, vhat, rs, vhat * lg + lb, phi


def _sgu_spatial(vln, w_ref, bt, tm):
    mask = _causal_mask()
    wm = [(w_ref[hd] * mask).astype(BF16) for hd in range(SGU_HEADS)]
    vb = vln.astype(BF16)
    rows = []
    for ch in range(tm // CHUNK):
        blocks = [_dot(wm[hd], vb[ch * CHUNK:(ch + 1) * CHUNK, hd * CHUNK:(hd + 1) * CHUNK]) + bt[:, hd:hd + 1] for hd in range(SGU_HEADS)]
        rows.append(jnp.concatenate(blocks, axis=1))
    return jnp.concatenate(rows, axis=0), wm


def _sgu_fwd(x, pre, ln_g, ln_b, w, bt, wout_g, name):
    t, d = x.shape
    tm = min(512, t)
    pc = pre.shape[1]

    def body(x_ref, pre_ref, lg_ref, lb_ref, w_ref, bt_ref, wout_hbm, xo_ref, p_ref, wout_v, sem):
        _load_at_first_step(_out_weight_copies(wout_hbm, wout_v, sem))
        u, _, _, vln, _ = _sgu_recompute(pre_ref[...], lg_ref[...], lb_ref[...])
        vo, _ = _sgu_spatial(vln, w_ref, bt_ref[...], tm)
        p = (u * vo).astype(BF16)
        p_ref[...] = p
        xo_ref[...] = x_ref[...] + _dot(p, wout_v[...])

    return _call(
        body, name=name, grid=(t // tm,),
        in_specs=[_row_spec(tm, d), _row_spec(tm, pc), _const_spec((1, d)), _const_spec((1, d)), _const_spec(w.shape),
                  _const_spec(bt.shape), HBM_SPEC],
        out_specs=[_row_spec(tm, d), _row_spec(tm, d)],
        out_shape=[jax.ShapeDtypeStruct((t, d), F32), jax.ShapeDtypeStruct((t, d), BF16)],
        scratch_shapes=[pltpu.VMEM((d, d), BF16), pltpu.SemaphoreType.DMA((N_CHIPS,))],
        args=(x, pre, ln_g, ln_b, w, bt, wout_g))


def _sgu_bwd(x, dy, pre, g, ln_g, ln_b, w, bt, win_g, wout_g, name, comms=()):
    t, d = x.shape
    tm = min(512, t)
    pc = pre.shape[1]
    c = win_g.shape[-1]

    def body(x_ref, dy_ref, pre_ref, g_ref, lg_ref, lb_ref, w_ref, bt_ref, win_hbm, wout_hbm,
             dx_ref, dpre_ref, dg_ref, dlg_ref, dlb_ref, dw_ref, dbt_ref, win_v, wout_v, sem):
        _load_at_first_step(_in_weight_copies(win_hbm, win_v, sem) + _out_weight_copies(wout_hbm, wout_v, sem, N_CHIPS))
        i = pl.program_id(0)

        @pl.when(i == 0)
        def _():
            for ref in (dg_ref, dlg_ref, dlb_ref, dw_ref, dbt_ref):
                ref[...] = jnp.zeros_like(ref)

        prev = pre_ref[...]
        lg = lg_ref[...]
        u, vhat, rs, vln, phi = _sgu_recompute(prev, lg, lb_ref[...])
        vo, wm = _sgu_spatial(vln, w_ref, bt_ref[...], tm)
        dp = _dot_nt(dy_ref[...].astype(BF16), wout_v[...])
        du = dp * vo
        dvo = dp * u
        dvob = dvo.astype(BF16)
        vb = vln.astype(BF16)
        head_lane = lax.broadcasted_iota(jnp.int32, (CHUNK, SGU_HEADS), 1)
        dbt = jnp.zeros((CHUNK, SGU_HEADS), F32)
        dw = [jnp.zeros((CHUNK, CHUNK), F32) for _ in range(SGU_HEADS)]
        rows = []
        for ch in range(tm // CHUNK):
            rs_ = slice(ch * CHUNK, (ch + 1) * CHUNK)
            blocks = []
            for hd in range(SGU_HEADS):
                cs = slice(hd * CHUNK, (hd + 1) * CHUNK)
                dbt = dbt + jnp.where(head_lane == hd, jnp.sum(dvo[rs_, cs], axis=1, keepdims=True), 0.0)
                dw[hd] = dw[hd] + _dot_nt(dvob[rs_, cs], vb[rs_, cs])
                blocks.append(_dot_tn(wm[hd], dvob[rs_, cs]))
            rows.append(jnp.concatenate(blocks, axis=1))
        dvln = jnp.concatenate(rows, axis=0)
        mask = _causal_mask()
        for hd in range(SGU_HEADS):
            dw_ref[hd] += dw[hd] * mask
        dbt_ref[...] += dbt
        dlg_ref[...] += jnp.sum(dvln * vhat, axis=0, keepdims=True)
        dlb_ref[...] += jnp.sum(dvln, axis=0, keepdims=True)
        dv = _ln_bwd(dvln * lg, vhat, rs)
        gelu_grad = phi + prev * jnp.exp(-0.5 * prev * prev) * INV_SQRT_2PI
        dpre = (jnp.concatenate([du, dv], axis=1) * gelu_grad).astype(BF16)
        dpre_ref[...] = dpre
        dxn = _proj_in_bwd_tail(dpre, win_v, c)
        xh, r = _rms_fwd(x_ref[...])
        dg_ref[...] += jnp.sum(dxn * xh, axis=0, keepdims=True)
        dx_ref[...] = dy_ref[...] + _rms_bwd(dxn, xh, r, g_ref[...])

    return _call(
        body, name=name, grid=(t // tm,),
        in_specs=[_row_spec(tm, d), _row_spec(tm, d), _row_spec(tm, pc), _const_spec((1, d)), _const_spec((1, d)), _const_spec((1, d)),
                  _const_spec(w.shape), _const_spec(bt.shape), HBM_SPEC, HBM_SPEC],
        out_specs=[_row_spec(tm, d), _row_spec(tm, pc), _const_spec((1, d)), _const_spec((1, d)), _const_spec((1, d)),
                   _const_spec(w.shape), _const_spec(bt.shape)],
        out_shape=[jax.ShapeDtypeStruct((t, d), F32), jax.ShapeDtypeStruct((t, pc), BF16), jax.ShapeDtypeStruct((1, d), F32),
                   jax.ShapeDtypeStruct((1, d), F32), jax.ShapeDtypeStruct((1, d), F32), jax.ShapeDtypeStruct(w.shape, F32),
                   jax.ShapeDtypeStruct(bt.shape, F32)],
        scratch_shapes=[pltpu.VMEM((N_CHIPS, d, c), BF16), pltpu.VMEM((d, d), BF16), pltpu.SemaphoreType.DMA((2 * N_CHIPS,))],
        args=(x, dy, pre, g, ln_g, ln_b, w, bt, win_g, wout_g), comms=comms)


def _final_loss(x, tgt, g, name):
    t, d = x.shape
    tm = min(512, t)

    def body(x_ref, t_ref, g_ref, dx_ref, loss_ref, dg_ref):
        @pl.when(pl.program_id(0) == 0)
        def _():
            loss_ref[...] = jnp.zeros_like(loss_ref)
            dg_ref[...] = jnp.zeros_like(dg_ref)

        gv = g_ref[...]
        xh, r = _rms_fwd(x_ref[...])
        diff = xh * gv - t_ref[...]
        loss_ref[...] += 0.5 * jnp.sum(jnp.sum(diff * diff, axis=1, keepdims=True), axis=0, keepdims=True) / d
        dout = diff / d
        dg_ref[...] += jnp.sum(dout * xh, axis=0, keepdims=True)
        dx_ref[...] = _rms_bwd(dout, xh, r, gv)

    return _call(
        body, name=name, grid=(t // tm,),
        in_specs=[_row_spec(tm, d), _row_spec(tm, d), _const_spec((1, d))],
        out_specs=[_row_spec(tm, d), _const_spec((1, 1)), _const_spec((1, d))],
        out_shape=[jax.ShapeDtypeStruct((t, d), F32), jax.ShapeDtypeStruct((1, 1), F32), jax.ShapeDtypeStruct((1, d), F32)],
        args=(x, tgt, g))


def _tn_matmul(a, b, scale, bm, bn, name, comms=()):
    t, m = a.shape
    n = b.shape[1]
    tk = min(2048, t)
    bm, bn = min(bm, m), min(bn, n)
    nk = t // tk

    def body(a_ref, b_ref, o_ref, acc_ref):
        k = pl.program_id(2)

        @pl.when(k == 0)
        def _():
            acc_ref[...] = jnp.zeros_like(acc_ref)

        bv = b_ref[...]
        if bv.dtype != BF16:
            bv = (scale * bv).astype(BF16)
        acc_ref[...] += _dot_tn(a_ref[...], bv)

        @pl.when(k == nk - 1)
        def _():
            o_ref[...] = acc_ref[...].astype(BF16)

    return _call(
        body, name=name, grid=(m // bm, n // bn, nk),
        in_specs=[pl.BlockSpec((tk, bm), lambda i, j, k: (k, i)), pl.BlockSpec((tk, bn), lambda i, j, k: (k, j))],
        out_specs=[pl.BlockSpec((bm, bn), lambda i, j, k: (i, j))],
        out_shape=[jax.ShapeDtypeStruct((m, n), BF16)],
        scratch_shapes=[pltpu.VMEM((bm, bn), F32)],
        args=(a, b), comms=comms)[0]


def _row_tile(rows, cols, budget_bytes=2 * 1024 * 1024):
    best = None
    for cand in range(16, rows + 1, 16):
        if rows % cand == 0 and cand * cols * 4 <= budget_bytes:
            best = cand
    return best or rows


def _scalar_grid(grid, in_specs, out_specs):
    return pltpu.PrefetchScalarGridSpec(num_scalar_prefetch=1, grid=grid, in_specs=in_specs, out_specs=out_specs)


def _cast_into_slot(w, layer, me, name):
    _, rows, cols = w.shape
    tr = _row_tile(rows, cols)

    def body(me_ref, w_ref, o_ref):
        o_ref[...] = w_ref[...].astype(BF16)

    return pl.pallas_call(
        body, name=name,
        grid_spec=_scalar_grid((rows // tr,), [pl.BlockSpec((None, tr, cols), lambda i, me: (layer, i, 0))],
                               pl.BlockSpec((None, tr, cols), lambda i, me: (me[0], i, 0))),
        out_shape=jax.ShapeDtypeStruct((N_CHIPS, rows, cols), BF16), compiler_params=_cparams())(me, w)


def _add_half(view, other, core, name):
    q, _, r, c = view.shape
    tr = _row_tile(r, c)

    def body(core_ref, a_ref, b_ref, o_ref):
        o_ref[...] = (a_ref[...].astype(F32) + b_ref[...].astype(F32)).astype(BF16)

    return pl.pallas_call(
        body, name=name,
        grid_spec=_scalar_grid((q, r // tr), [pl.BlockSpec((None, None, tr, c), lambda k, i, core: (k, core[0], i, 0)),
                                             pl.BlockSpec((None, tr, c), lambda k, i, core: (k, i, 0))],
                               pl.BlockSpec((None, tr, c), lambda k, i, core: (k, i, 0))),
        out_shape=jax.ShapeDtypeStruct((q, r, c), BF16), compiler_params=_cparams_nd(2))(core, view, other)


def _reduce_piece(partial, staged, me, column_sharded, name):
    _, r, c = staged.shape
    tr = _row_tile(r, c, budget_bytes=1024 * 1024)
    nt = r // tr
    if column_sharded:
        own2d = partial.reshape(r, N_CHIPS * c)
        own_spec = pl.BlockSpec((tr, c), lambda i, me: (i, me[0]))
    else:
        own2d = partial.reshape(N_CHIPS * r, c)
        own_spec = pl.BlockSpec((tr, c), lambda i, me: (me[0] * nt + i, 0))
    ring = [pl.BlockSpec((None, tr, c), lambda i, me, k=k: ((me[0] + k) % N_CHIPS, i, 0)) for k in (1, 2, 3)]

    def body(me_ref, own_ref, s1_ref, s2_ref, s3_ref, o_ref):
        o_ref[...] = ((own_ref[...].astype(F32) + s1_ref[...].astype(F32)) + s2_ref[...].astype(F32)) + s3_ref[...].astype(F32)

    return pl.pallas_call(
        body, name=name, grid_spec=_scalar_grid((nt,), [own_spec] + ring, pl.BlockSpec((tr, c), lambda i, me: (i, 0))),
        out_shape=jax.ShapeDtypeStruct((r, c), F32), compiler_params=_cparams())(me, own2d, staged, staged, staged)


def _sum_leading(s, name):
    n, rows, cols = s.shape
    tr = _row_tile(rows, cols, budget_bytes=1024 * 1024)

    def body(s_ref, o_ref):
        acc = s_ref[0].astype(F32)
        for k in range(1, n):
            acc = acc + s_ref[k].astype(F32)
        o_ref[...] = acc

    return pl.pallas_call(
        body, name=name, grid=(rows // tr,), in_specs=[pl.BlockSpec((n, tr, cols), lambda i: (0, i, 0))], out_specs=_row_spec(tr, cols),
        out_shape=jax.ShapeDtypeStruct((rows, cols), F32), compiler_params=_cparams())(s)


ADAM_C1 = 1.0 / (1.0 - ADAM_B1 ** ADAM_STEP)
ADAM_C2 = 1.0 / (1.0 - ADAM_B2 ** ADAM_STEP)


def _adamw_math(w, g, m, v):
    mn = ADAM_B1 * m + (1.0 - ADAM_B1) * g
    vn = ADAM_B2 * v + (1.0 - ADAM_B2) * (g * g)
    return -ADAM_LR * ((mn * ADAM_C1) / (jnp.sqrt(vn * ADAM_C2) + ADAM_EPS) + ADAM_WD * w), mn, vn


def _adamw(w, g, m, v, name):
    shape = w.shape
    cols = shape[-1] if w.ndim > 1 else 128
    w2, g2, m2, v2 = (a.reshape(-1, cols) for a in (w, g, m, v))
    rows = w2.shape[0]
    tr = _row_tile(rows, cols, budget_bytes=1024 * 1024)

    def body(w_ref, g_ref, m_ref, v_ref, d_ref, mo_ref, vo_ref):
        d_ref[...], mo_ref[...], vo_ref[...] = _adamw_math(w_ref[...], g_ref[...], m_ref[...], v_ref[...])

    spec = _row_spec(tr, cols)
    outs = pl.pallas_call(
        body, name=name, grid=(rows // tr,), in_specs=[spec] * 4, out_specs=[spec] * 3,
        out_shape=[jax.ShapeDtypeStruct((rows, cols), F32)] * 3, compiler_params=_cparams())(w2, g2, m2, v2)
    return tuple(o.reshape(shape) for o in outs)


def _adamw_sharded(w, g_mine, g_sibling, m, v, core, layer, prev, name, comms=(), after=()):
    n_layers, r, c = w.shape
    half = r // 2
    tr = _row_tile(half, c)
    nt = half // tr

    def body(core_ref, w_ref, gm_ref, gs_ref, m_ref, v_ref, *rest):
        g_ref, d_ref, mo_ref, vo_ref = rest[-4:]
        gv = jnp.where(pl.program_id(0) == core_ref[0], gm_ref[...], gs_ref[...])
        g_ref[...] = gv
        d_ref[...], mo_ref[...], vo_ref[...] = _adamw_math(w_ref[...], gv, m_ref[...], v_ref[...])

    full = pl.BlockSpec((None, tr, c), lambda h, i, core: (layer, h * nt + i, 0))
    part = pl.BlockSpec((tr, c), lambda h, i, core: (i, 0))
    args = [w, g_mine, g_sibling, m, v]
    in_specs = [full, part, part, full, full]
    aliases = {}
    if prev is not None:
        aliases = {len(args) + k: k for k in range(4)}
        args += list(prev)
        in_specs += [pl.BlockSpec(memory_space=pl.ANY)] * 4
    args += list(after)
    in_specs += [pl.BlockSpec(memory_space=pl.ANY)] * len(after)
    return _call(body, name=name, grid=(2, nt), in_specs=in_specs, out_specs=[full] * 4, out_shape=[jax.ShapeDtypeStruct(w.shape, F32)] * 4,
                 args=args, comms=comms, scalar=core, aliases=aliases)


BIG_IN = ("ffn1_w_in", "ffn2_w_in", "ab_w_in", "sgu_w_in")
BIG_OUT = ("ffn1_w_out", "ffn2_w_out", "ab_w_out", "sgu_w_out")
BIG = BIG_IN + BIG_OUT


class _Gatherer:
    def __init__(self, slots):
        self.slots = dict(slots)

    def _stage(self, keys, d2d):
        n = len(keys)

        def plan(ins, outs, place):
            x, y, c = place
            me = 2 * x + y
            remote = []
            for a in range(n):
                rows = ins[a].shape[1] // 2

                def half(ref, q, core, rows=rows):
                    return ref.at[q, pl.ds(core * rows, rows), :]

                for (px, py) in _other_chips(x, y):
                    q = 2 * px + py
                    if d2d:
                        remote.append((half(ins[a], q, c), half(outs[a], q, c), (x, y, 1 - c), half(outs[a], q, 1 - c)))
                    else:
                        remote.append((half(ins[a], me, c), half(outs[a], me, c), (px, py, c), half(outs[a], q, c)))
            return remote

        def finish(outs):
            for k, o in zip(keys, outs):
                self.slots[k] = o

        arrays = [self.slots[k] for k in keys]
        return _Exchange(arrays, [_sds(a) for a in arrays], plan, 3 * n, {a: a for a in range(n)}, finish)

    def direct(self, keys):
        n = len(keys)

        def plan(ins, outs, place):
            x, y, c = place
            me = 2 * x + y
            return [(ins[a].at[me], outs[a].at[me], (px, py, c), outs[a].at[2 * px + py]) for a in range(n) for (px, py) in _other_chips(x, y)]

        def finish(outs):
            for k, o in zip(keys, outs):
                self.slots[k] = o

        arrays = [self.slots[k] for k in keys]
        return _Exchange(arrays, [_sds(a) for a in arrays], plan, 3 * n, {a: a for a in range(n)}, finish)

    def ici(self, keys):
        return self._stage(keys, False)

    def d2d(self, keys):
        return self._stage(keys, True)


class _Reducer:
    def __init__(self, me, core):
        self.me, self.core = me, core
        self.views, self.partial, self.staged, self.mine, self.theirs = {}, {}, {}, {}, {}

    def add(self, key, g):
        m, n = g.shape
        if key[0] in BIG_IN:
            self.views[key] = g.reshape(1, 2, m // 2, n)
        else:
            self.views[key] = g.reshape(N_CHIPS, 2, m // (2 * N_CHIPS), n)

    def swap(self, keys):
        views = [self.views[k] for k in keys]

        def plan(ins, outs, place):
            x, y, c = place
            return [(ins[a].at[:, 1 - c], outs[a], (x, y, 1 - c), outs[a]) for a in range(len(keys))]

        def finish(outs):
            for k, v, o in zip(keys, views, outs):
                self.partial[k] = _add_half(v, o, self.core, f"chip_partial_{k[0]}_{k[1]}")

        shapes = [jax.ShapeDtypeStruct((v.shape[0],) + v.shape[2:], v.dtype) for v in views]
        return _Exchange(views, shapes, plan, len(keys), None, finish)

    def scatter(self, keys, part=(0, 1)):
        i, n = part
        n_keys = len(keys)
        parts = [self.partial[k] for k in keys]
        shapes = []
        for k, p in zip(keys, parts):
            q, r, c = p.shape
            shapes.append(jax.ShapeDtypeStruct((N_CHIPS, r, c // N_CHIPS if k[0] in BIG_IN else c), p.dtype))

        def piece(ref, key, q, rows, cols):
            return ref.at[0, rows, pl.ds(q * cols, cols)] if key[0] in BIG_IN else ref.at[q, rows, :]

        def plan(ins, outs, place):
            x, y, c = place
            me = 2 * x + y
            remote = []
            for a, k in enumerate(keys):
                _, r, cols = shapes[a].shape
                rows = pl.ds(i * (r // n), r // n)
                for (px, py) in _other_chips(x, y):
                    q = 2 * px + py
                    remote.append((piece(ins[a], k, q, rows, cols), outs[a].at[me, rows, :], (px, py, c), outs[a].at[q, rows, :]))
            return remote

        def finish(outs):
            for k, p, o in zip(keys, parts, outs):
                self.staged[k] = o
                if i == n - 1:
                    self.mine[k] = _reduce_piece(p, o, self.me, k[0] in BIG_IN, f"reduce_{k[0]}_{k[1]}")

        inputs, aliases = parts, None
        if i > 0:
            inputs = parts + [self.staged[k] for k in keys]
            aliases = {n_keys + a: a for a in range(n_keys)}
        return _Exchange(inputs, shapes, plan, 3 * n_keys, aliases, finish)

    def scatter_behind(self, keys, work):
        n = len(keys)
        parts = [self.partial[k] for k in keys]
        lands = []
        for k, p in zip(keys, parts):
            _, r, c = p.shape
            lands.append(jax.ShapeDtypeStruct((N_CHIPS, r, c // N_CHIPS if k[0] in BIG_IN else c), p.dtype))
        sem_spec = pl.BlockSpec(memory_space=pltpu.SEMAPHORE)
        effect = pltpu.CompilerParams(has_side_effects=pltpu.SideEffectType.DATAFLOW_SIDE_EFFECTING)

        def copies(part_refs, land_refs, send_sems, recv_sems):
            x, y, c = _my_place()
            me = 2 * x + y
            out = []
            for a, k in enumerate(keys):
                cols = lands[a].shape[2]
                for j, (px, py) in enumerate(_other_chips(x, y)):
                    q = 2 * px + py
                    src = part_refs[a].at[0, :, pl.ds(q * cols, cols)] if k[0] in BIG_IN else part_refs[a].at[q]
                    sems = dict(send_sem=send_sems.at[3 * a + j], recv_sem=recv_sems.at[3 * a + j], device_id=(px, py, c), device_id_type=MESH)
                    out.append((pltpu.make_async_remote_copy(src_ref=src, dst_ref=land_refs[a].at[me], **sems),
                                pltpu.make_async_remote_copy(src_ref=src, dst_ref=land_refs[a].at[q], **sems)))
            return out

        def start_body(*refs):
            part_refs, land_refs, send_sems, recv_sems, token = refs[:n], refs[n:2 * n], refs[2 * n], refs[2 * n + 1], refs[-1]
            for send, _ in copies(part_refs, land_refs, send_sems, recv_sems):
                send.start()
            token[...] = jnp.zeros_like(token)

        def wait_body(*refs):
            part_refs, land_refs, send_sems, recv_sems = refs[:n], refs[n:2 * n], refs[2 * n], refs[2 * n + 1]
            for send, arrive in copies(part_refs, land_refs, send_sems, recv_sems):
                send.wait_send()
                arrive.wait_recv()

        in_hbm = [pltpu.with_memory_space_constraint(p, pltpu.HBM) for p in parts]
        in_hbm += [pltpu.with_memory_space_constraint(lax.empty(s.shape, s.dtype), pltpu.HBM) for s in lands]
        thru_shapes = [pltpu.HBM(p.shape, p.dtype) for p in parts] + [pltpu.HBM(s.shape, s.dtype) for s in lands]
        started = pl.pallas_call(
            start_body, name="scatter_last_start", in_specs=[HBM_SPEC] * (2 * n),
            out_shape=(pltpu.SemaphoreType.DMA((3 * n,)), pltpu.SemaphoreType.DMA((3 * n,)), *thru_shapes, jax.ShapeDtypeStruct((8, 128), F32)),
            out_specs=(sem_spec, sem_spec, *[HBM_SPEC] * (2 * n), pl.BlockSpec(memory_space=pltpu.VMEM)),
            input_output_aliases={i: 2 + i for i in range(2 * n)}, compiler_params=effect)(*in_hbm)
        send_sems, recv_sems, thru, token = started[0], started[1], started[2:2 + 2 * n], started[-1]
        after = work(token)
        done = pl.pallas_call(
            wait_body, name="scatter_last_wait", in_specs=[HBM_SPEC] * (2 * n) + [sem_spec, sem_spec] + [pl.BlockSpec(memory_space=pl.ANY)] * len(after),
            out_shape=tuple(thru_shapes), out_specs=tuple([HBM_SPEC] * (2 * n)), input_output_aliases={i: i for i in range(2 * n)},
            compiler_params=effect)(*thru, send_sems, recv_sems, *after)
        for a, k in enumerate(keys):
            self.staged[k] = done[n + a]
            self.mine[k] = _reduce_piece(done[a], done[n + a], self.me, k[0] in BIG_IN, f"reduce_{k[0]}_{k[1]}")

    def exchange(self, keys):
        mine = [self.mine[k] for k in keys]

        def plan(ins, outs, place):
            x, y, c = place
            return [(ins[a], outs[a], (x, y, 1 - c), outs[a]) for a in range(len(keys))]

        def finish(outs):
            for k, o in zip(keys, outs):
                self.theirs[k] = o

        return _Exchange(mine, [_sds(a) for a in mine], plan, len(keys), None, finish)


def _all_gather_full(gat, keys):
    n = len(keys)
    arrays = [gat.slots[k] for k in keys]
    per = 7

    def body(*refs):
        ins, outs = refs[:n], refs[n:2 * n]
        send_sems, recv_sems = refs[2 * n:]
        x, y, c = _my_place()
        sibling, x_nbr, y_nbr = (x, y, 1 - c), (1 - x, y, c), (x, 1 - y, c)
        me, qx, qy, qd = 2 * x + y, 2 * (1 - x) + y, 2 * x + (1 - y), 2 * (1 - x) + (1 - y)

        def half(ref, q, core):
            rows = ref.shape[1] // 2
            return ref.at[q, pl.ds(core * rows, rows), :]

        def quarter(ref, q, core, k):
            rows = ref.shape[1] // 4
            return ref.at[q, pl.ds((2 * core + k) * rows, rows), :]

        def copy(a, k, src, dst, to):
            return pltpu.make_async_remote_copy(src_ref=src, dst_ref=dst, send_sem=send_sems.at[per * a + k],
                                                recv_sem=recv_sems.at[per * a + k], device_id=to, device_id_type=MESH)

        sent = []

        def send(a, k, part, to):
            cp = copy(a, k, part, part, to)
            cp.start()
            sent.append(cp)

        def landed(a, k, part):
            copy(a, k, part, part, sibling).wait_recv()

        for a in range(n):
            mine_in, mine_out = half(ins[a], me, c), half(outs[a], me, c)
            for k, to in ((0, x_nbr), (1, y_nbr)):
                cp = copy(a, k, mine_in, mine_out, to)
                cp.start()
                sent.append(cp)
        for a in range(n):
            landed(a, 1, half(outs[a], qy, c))
            send(a, 2, quarter(outs[a], qy, c, 0), x_nbr)
            send(a, 5, half(outs[a], qy, c), sibling)
            landed(a, 0, half(outs[a], qx, c))
            send(a, 3, quarter(outs[a], qx, c, 1), y_nbr)
            send(a, 4, half(outs[a], qx, c), sibling)
        for a in range(n):
            landed(a, 2, quarter(outs[a], qd, c, 0))
            landed(a, 3, quarter(outs[a], qd, c, 1))
            send(a, 6, half(outs[a], qd, c), sibling)
        for a in range(n):
            for k, q in ((4, qx), (5, qy), (6, qd)):
                landed(a, k, half(outs[a], q, 1 - c))
        for cp in sent:
            cp.wait_send()

    outs = pl.pallas_call(
        body, name="all_gather_first_weights", in_specs=[HBM_SPEC] * n, out_specs=[HBM_SPEC] * n,
        out_shape=[_sds(a) for a in arrays], input_output_aliases={a: a for a in range(n)},
        scratch_shapes=[pltpu.SemaphoreType.DMA((per * n,)), pltpu.SemaphoreType.DMA((per * n,))])(*arrays)
    for k, o in zip(keys, outs):
        gat.slots[k] = o


def _small_all_gather(buf, done):
    state = {}

    def index(x, y, c):
        return 4 * x + 2 * y + c

    def plan_ici(ins, outs, place):
        x, y, c = place
        return [(ins[0], outs[0].at[index(x, y, c)], (px, py, c), outs[0].at[index(px, py, c)]) for (px, py) in _other_chips(x, y)]

    def local(ins, outs, place):
        return [(ins[0], outs[0].at[index(*place)])]

    def plan_d2d(ins, outs, place):
        x, y, c = place
        return [(ins[0].at[index(px, py, c)], outs[0].at[index(px, py, c)], (x, y, 1 - c), outs[0].at[index(px, py, 1 - c)])
                for (px, py) in [(x, y)] + _other_chips(x, y)]

    def second():
        return _Exchange([state["blocks"]], [_sds(state["blocks"])], plan_d2d, N_CHIPS, {0: 0}, lambda outs: done(outs[0]))

    first = _Exchange([buf], [jax.ShapeDtypeStruct((2 * N_CHIPS,) + buf.shape, buf.dtype)], plan_ici, 3, None,
                      lambda outs: state.update(blocks=outs[0]), local, 1)
    return first, second


WEIGHT_NAMES = ("ffn1_norm", "ffn1_w_in", "ffn1_w_out", "mix_norm", "ffn2_norm", "ffn2_w_in", "ffn2_w_out", "ab_w_in", "pool_w", "pool_b",
                "pool_scale", "conv_w", "conv_b", "conv_ln_g", "conv_ln_b", "ab_w_out", "sgu_w_in", "sgu_ln_g", "sgu_ln_b", "sgu_w", "sgu_b",
                "sgu_w_out", "final_norm")
SMALL = tuple(n for n in WEIGHT_NAMES if n not in BIG)
SHARDED_SMALL = ("conv_w", "sgu_ln_g", "sgu_ln_b")
PACK_ROWS = 64
PACK = ("pack", 0)


def _pair(prefix, layer):
    return [(prefix + "_w_in", layer), (prefix + "_w_out", layer)]


def kernel(x, ffn1_norm, ffn1_w_in, ffn1_w_out, mix_norm, ffn2_norm, ffn2_w_in, ffn2_w_out, ab_w_in, pool_w, pool_b, pool_scale, conv_w, conv_b, conv_ln_g, conv_ln_b, ab_w_out, sgu_w_in, sgu_ln_g, sgu_ln_b, sgu_w, sgu_b, sgu_w_out, final_norm, loss_target, m_ffn1_norm, m_ffn1_w_in, m_ffn1_w_out, m_mix_norm, m_ffn2_norm, m_ffn2_w_in, m_ffn2_w_out, m_ab_w_in, m_pool_w, m_pool_b, m_pool_scale, m_conv_w, m_conv_b, m_conv_ln_g, m_conv_ln_b, m_ab_w_out, m_sgu_w_in, m_sgu_ln_g, m_sgu_ln_b, m_sgu_w, m_sgu_b, m_sgu_w_out, m_final_norm, v_ffn1_norm, v_ffn1_w_in, v_ffn1_w_out, v_mix_norm, v_ffn2_norm, v_ffn2_w_in, v_ffn2_w_out, v_ab_w_in, v_pool_w, v_pool_b, v_pool_scale, v_conv_w, v_conv_b, v_conv_ln_g, v_conv_ln_b, v_ab_w_out, v_sgu_w_in, v_sgu_ln_g, v_sgu_ln_b, v_sgu_w, v_sgu_b, v_sgu_w_out, v_final_norm):
    given = dict(locals())
    w = {n: given[n] for n in WEIGHT_NAMES}
    chip = 2 * lax.axis_index("x") + lax.axis_index("y")
    me = chip.astype(jnp.int32).reshape(1)
    core = lax.axis_index("c").astype(jnp.int32).reshape(1)
    row = lambda v: v.reshape(1, -1)
    xin, tgt = x[0], loss_target[0]

    pack = jnp.concatenate([
        w["conv_w"][0], jnp.zeros((1, 128), F32), w["sgu_ln_g"].reshape(2, 128), w["sgu_ln_b"].reshape(2, 128),
        jnp.zeros((PACK_ROWS - 36, 128), F32)], axis=0)
    slots = {PACK: lax.dynamic_update_slice(jnp.zeros((N_CHIPS, PACK_ROWS, 128), F32), pack[None], (me[0], 0, 0))}
    for n in BIG:
        for layer in range(w[n].shape[0]):
            slots[(n, layer)] = _cast_into_slot(w[n], layer, me, f"cast_{n}_{layer}")
    gat = _Gatherer(slots)
    _all_gather_full(gat, _pair("ffn1", 0) + [PACK])
    gp = gat.slots[PACK]
    conv_w_full = jnp.transpose(gp[:, 0:CONV_WIDTH], (1, 0, 2)).reshape(CONV_WIDTH, N_CHIPS * 128)
    sgu_ln_g_full = gp[:, 32:34].reshape(1, -1)
    sgu_ln_b_full = gp[:, 34:36].reshape(1, -1)
    gw = lambda n, layer: gat.slots[(n, layer)]

    st = [dict(), dict()]
    st[0]["xa"] = xin
    later = _pair("sgu", 0) + _pair("ffn2", 1)
    cur, st[0]["h1"], st[0]["xn1"] = _ffn_fwd(xin, row(w["ffn1_norm"][0]), gw("ffn1_w_in", 0), gw("ffn1_w_out", 0), "ffn1_fwd_0",
                                              comms=[gat.direct(_pair("ab", 0)), gat.ici(_pair("ffn2", 0))])
    st[0]["xb"] = cur
    st[0]["h0"], st[0]["xnm"] = _norm_matmul(cur, row(w["mix_norm"][0]), gw("ab_w_in", 0), "mix0_proj_in",
                                             comms=[gat.d2d(_pair("ffn2", 0)), gat.ici([("ffn1_w_out", 1)])])
    pool_args = (w["pool_w"][0], row(w["pool_b"][0]), row(w["pool_scale"][0]), conv_w_full, row(w["conv_b"][0]), row(w["conv_ln_g"][0]),
                 row(w["conv_ln_b"][0]), gw("ab_w_out", 0))
    cur, st[0]["ycat"], st[0]["yconv"] = _mix0_fwd(cur, st[0]["h0"], *pool_args, "mix0_fwd", comms=[gat.ici([("ffn1_w_in", 1)])])
    st[0]["xc"] = cur
    cur, st[0]["h2"], st[0]["xn2"] = _ffn_fwd(cur, row(w["ffn2_norm"][0]), gw("ffn2_w_in", 0), gw("ffn2_w_out", 0), "ffn2_fwd_0",
                                              comms=[gat.d2d(_pair("ffn1", 1)), gat.ici(later)])
    st[1]["xa"] = cur
    cur, st[1]["h1"], st[1]["xn1"] = _ffn_fwd(cur, row(w["ffn1_norm"][1]), gw("ffn1_w_in", 1), gw("ffn1_w_out", 1), "ffn1_fwd_1",
                                              comms=[gat.d2d(later)])
    st[1]["xb"] = cur
    st[1]["pre"], st[1]["xnm"] = _norm_matmul(cur, row(w["mix_norm"][1]), gw("sgu_w_in", 0), "sgu_proj_in")
    sgu_args = (sgu_ln_g_full, sgu_ln_b_full, w["sgu_w"][0], w["sgu_b"][0].T)
    cur, st[1]["p"] = _sgu_fwd(cur, st[1]["pre"], *sgu_args, gw("sgu_w_out", 0), "sgu_fwd")
    st[1]["xc"] = cur
    cur, st[1]["h2"], st[1]["xn2"] = _ffn_fwd(cur, row(w["ffn2_norm"][1]), gw("ffn2_w_in", 1), gw("ffn2_w_out", 1), "ffn2_fwd_1")
    dy, loss, d_final = _final_loss(cur, tgt, row(w["final_norm"]), "final_loss")

    red = _Reducer(me, core)
    small = {"final_norm": d_final.reshape(-1)}
    norm_grads = {"ffn1_norm": [None] * DEPTH, "mix_norm": [None] * DEPTH, "ffn2_norm": [None] * DEPTH}
    ga, gb, gc, gd, ge, gf = _pair("ffn2", 1), _pair("sgu", 0), _pair("ffn1", 1), _pair("ffn2", 0), _pair("ab", 0), _pair("ffn1", 0)

    def ffn_backward(prefix, layer, xs, hs, xns, dy_in, bwd_comms=(), dwin_comms=(), dwout_comms=()):
        dx, dh, act, norm_grads[prefix + "_norm"][layer] = _ffn_bwd(
            xs, dy_in, hs, row(w[prefix + "_norm"][layer]), gw(prefix + "_w_in", layer), gw(prefix + "_w_out", layer),
            f"{prefix}_bwd_{layer}", comms=bwd_comms)
        red.add((prefix + "_w_in", layer), _tn_matmul(xns, dh, 1.0, 1024, 1408, f"{prefix}_dwin_{layer}", comms=dwin_comms))
        red.add((prefix + "_w_out", layer), _tn_matmul(act, dy_in, 0.5, 1408, 1024, f"{prefix}_dwout_{layer}", comms=dwout_comms))
        return dx

    s1, s0 = st[1], st[0]
    dy = ffn_backward("ffn2", 1, s1["xc"], s1["h2"], s1["xn2"], dy)
    dy_in = dy
    dy, dpre, norm_grads["mix_norm"][1], dlg, dlb, dw, dbt = _sgu_bwd(
        s1["xb"], dy_in, s1["pre"], row(w["mix_norm"][1]), *sgu_args, gw("sgu_w_in", 0), gw("sgu_w_out", 0), "sgu_bwd", comms=[red.swap(ga)])
    red.add(("sgu_w_in", 0), _tn_matmul(s1["xnm"], dpre, 1.0, 1024, 2048, "sgu_dwin"))
    red.add(("sgu_w_out", 0), _tn_matmul(s1["p"], dy_in, 1.0, 1024, 1024, "sgu_dwout"))
    small.update(sgu_ln_g=dlg, sgu_ln_b=dlb, sgu_w=dw[None], sgu_b=dbt.T[None])
    dy = ffn_backward("ffn1", 1, s1["xa"], s1["h1"], s1["xn1"], dy, bwd_comms=[lambda: red.scatter(ga), lambda: red.swap(gb)],
                      dwin_comms=[lambda: red.scatter(gb), lambda: red.exchange(ga)])
    dy = ffn_backward("ffn2", 0, s0["xc"], s0["h2"], s0["xn2"], dy, bwd_comms=[lambda: red.swap(gc), lambda: red.exchange(gb)],
                      dwin_comms=[lambda: red.scatter(gc)])
    dy_in = dy
    dconv, dpc, dpw, rows = _mix0_bwd_a(dy_in, s0["h0"], s0["yconv"], *pool_args, "mix0_bwd_a")
    dy, dh0, norm_grads["mix_norm"][0] = _mix0_bwd_b(s0["xb"], dy_in, s0["h0"], dconv, dpc, row(w["mix_norm"][0]), conv_w_full,
                                                      gw("ab_w_in", 0), "mix0_bwd_b")
    red.add(("ab_w_in", 0), _tn_matmul(s0["xnm"], dh0, 1.0, 1024, 1536, "ab_dwin", comms=[red.swap(gd), red.exchange(gc)]))
    red.add(("ab_w_out", 0), _tn_matmul(s0["ycat"], dy_in, 1.0, 1024, 1024, "ab_dwout"))
    small.update(pool_w=dpw[None], conv_w=rows[None, 0:CONV_WIDTH], conv_b=rows[32:33], conv_ln_g=rows[33:34], conv_ln_b=rows[34:35],
                 pool_scale=rows[35:36], pool_b=rows[36:37].reshape(1, len(POOL_WINDOWS), POOL_GC))

    small_sum = {}

    def small_ready():
        for k, v in norm_grads.items():
            small[k] = jnp.concatenate(v, axis=0)
        flat = [small[n].reshape(-1, 128) for n in SMALL]
        rows = sum(f.shape[0] for f in flat)
        loss_block = jnp.pad(loss, ((0, 8 + (-rows) % 8 - 1), (0, 127)))
        buf = jnp.concatenate(flat + [loss_block], axis=0)

        def done(gathered):
            total, at = _sum_leading(gathered, "reduce_small"), 0
            for n, f in zip(SMALL, flat):
                small_sum[n] = total[at:at + f.shape[0]].reshape(small[n].shape)
                at += f.shape[0]
            small_sum["loss"] = total[at:at + 1, 0:1]

        return _small_all_gather(buf, done)

    dx, dh, act, norm_grads["ffn1_norm"][0] = _ffn_bwd(
        s0["xa"], dy, s0["h1"], row(w["ffn1_norm"][0]), gw("ffn1_w_in", 0), gw("ffn1_w_out", 0), "ffn1_bwd_0")
    small_first, small_second = small_ready()
    red.add(("ffn1_w_in", 0), _tn_matmul(s0["xn1"], dh, 1.0, 1024, 1408, "ffn1_dwin_0", comms=[red.scatter(gd), red.swap(ge)]))
    red.add(("ffn1_w_out", 0), _tn_matmul(act, dy, 0.5, 1408, 1024, "ffn1_dwout_0",
                                          comms=[red.scatter(ge), red.exchange(gd), small_first]))
    grad_x = dx

    big_out = {}

    def adamw_big(n, layer, after=()):
        big_out[n] = _adamw_sharded(w[n], red.mine[(n, layer)], red.theirs[(n, layer)], given["m_" + n], given["v_" + n], core, layer,
                                    big_out.get(n), f"adamw_{n}_{layer}", after=after)

    _exchange_alone("swap_last_grads", [red.swap(gf), red.exchange(ge), small_second])

    def other_updates(token):
        for n in BIG:
            for layer in reversed(range(w[n].shape[0])):
                if (n, layer) not in gf:
                    adamw_big(n, layer, after=[token])
        return [big_out[n][0] for n in BIG]

    red.scatter_behind(gf, other_updates)
    _exchange_alone("exchange_last_grads", [red.exchange(gf)])
    for key in gf:
        adamw_big(*key)

    loss = small_sum["loss"][0, 0]
    grads, delta, new_m, new_v = {}, {}, {}, {}
    for n in WEIGHT_NAMES:
        mom, var = given["m_" + n], given["v_" + n]
        if n in BIG:
            grads[n], delta[n], new_m[n], new_v[n] = big_out[n]
            continue
        g = small_sum[n]
        if n in SHARDED_SMALL:
            width = w[n].shape[-1]
            g = lax.dynamic_slice_in_dim(g, chip * width, width, axis=g.ndim - 1)
        grads[n] = g
        delta[n], new_m[n], new_v[n] = _adamw(w[n], g, mom, var, f"adamw_{n}")
    return (loss, grad_x[None], *[grads[n] for n in WEIGHT_NAMES], *[delta[n] for n in WEIGHT_NAMES],
            *[new_m[n] for n in WEIGHT_NAMES], *[new_v[n] for n in WEIGHT_NAMES])
```

```python
import jax
import jax.numpy as jnp
from jax import lax
from jax.experimental import pallas as pl
from jax.experimental.pallas import tpu as pltpu

F32, BF16 = jnp.float32, jnp.bfloat16
EPS = 1e-6
N_CHIPS = 4
POOL_WINDOWS = (2, 4, 8, 16)
POOL_GC = 128
POOL_CH = 512
CONV_CH = 512
CONV_WIDTH = 31
HALO = 32
SGU_HEADS = 8
CHUNK = 128
DEPTH = 2
ADAM_LR, ADAM_B1, ADAM_B2, ADAM_EPS, ADAM_WD, ADAM_STEP = 0.001, 0.9, 0.999, 1e-08, 0.01, 10
VMEM_LIMIT_BYTES = 60 * 1024 * 1024
MESH_AXES = ("x", "y", "c")
MESH = pl.DeviceIdType.MESH
HBM_SPEC = pl.BlockSpec(memory_space=pltpu.HBM)


def _sds(a):
    return jax.ShapeDtypeStruct(a.shape, a.dtype)


def _cparams_nd(n):
    return pltpu.CompilerParams(dimension_semantics=("arbitrary",) * n, vmem_limit_bytes=VMEM_LIMIT_BYTES)


def _cparams():
    return _cparams_nd(1)


def _dot(a, b):
    return jnp.dot(a, b, preferred_element_type=F32)


def _dot_nt(a, b):
    return lax.dot_general(a, b, (((1,), (1,)), ((), ())), preferred_element_type=F32)


def _dot_tn(a, b):
    return lax.dot_general(a, b, (((0,), (0,)), ((), ())), preferred_element_type=F32)


def _rms_fwd(x):
    r = lax.rsqrt(jnp.mean(x * x, axis=-1, keepdims=True) + EPS)
    return x * r, r


def _rms_bwd(dxn, xh, r, g):
    dxh = dxn * g
    return r * (dxh - xh * jnp.mean(dxh * xh, axis=-1, keepdims=True))


def _ln_fwd(y):
    mu = jnp.mean(y, axis=-1, keepdims=True)
    yc = y - mu
    rs = lax.rsqrt(jnp.mean(yc * yc, axis=-1, keepdims=True) + EPS)
    return yc * rs, rs


def _ln_bwd(dyhat, yhat, rs):
    return rs * (dyhat - jnp.mean(dyhat, axis=-1, keepdims=True) - yhat * jnp.mean(dyhat * yhat, axis=-1, keepdims=True))


def _sigmoid(x):
    return 0.5 * jnp.tanh(0.5 * x) + 0.5


def _const_spec(shape):
    n = len(shape)
    return pl.BlockSpec(shape, lambda i: (0,) * n)


def _row_spec(tm, cols):
    return pl.BlockSpec((tm, cols), lambda i: (i, 0))


def _my_place():
    return lax.axis_index("x"), lax.axis_index("y"), lax.axis_index("c")


def _other_chips(x, y):
    return [(1 - x, y), (x, 1 - y), (1 - x, 1 - y)]


class _Exchange:
    def __init__(self, inputs, out_shapes, plan, count, aliases=None, finish=None, local=None, n_local=0):
        self.inputs, self.out_shapes, self.plan, self.count = list(inputs), list(out_shapes), plan, count
        self.aliases, self.finish, self.local, self.n_local = dict(aliases or {}), finish, local, n_local


def _call(body, *, name, grid, in_specs, out_specs, out_shape, args, scratch_shapes=(), comms=(), scalar=None, aliases=None, after=()):
    comms = [cm if isinstance(cm, _Exchange) else cm() for cm in comms]
    in_specs, out_specs, out_shape, scratch_shapes = list(in_specs), list(out_specs), list(out_shape), list(scratch_shapes)
    n_body_in = len(in_specs)
    in_specs += [pl.BlockSpec(memory_space=pl.ANY)] * len(after)
    args = list(args) + list(after)
    n_in, n_out, n_scr = len(in_specs), len(out_specs), len(scratch_shapes)
    c_in = [a for cm in comms for a in cm.inputs]
    c_out = [s for cm in comms for s in cm.out_shapes]
    n_remote = sum(cm.count for cm in comms)
    n_local = sum(cm.n_local for cm in comms)
    n_scalar = 0 if scalar is None else 1
    all_aliases = {n_scalar + i: o for i, o in (aliases or {}).items()}
    at_in, at_out = n_scalar + n_in, n_out
    for cm in comms:
        for i, o in cm.aliases.items():
            all_aliases[at_in + i] = at_out + o
        at_in += len(cm.inputs)
        at_out += len(cm.out_shapes)

    def wrapped(*all_refs):
        scalar_ref, refs = all_refs[:n_scalar], all_refs[n_scalar:]
        ins, ci = refs[:n_in], refs[n_in:n_in + len(c_in)]
        at = n_in + len(c_in)
        outs, co = refs[at:at + n_out], refs[at + n_out:at + n_out + len(c_out)]
        at += n_out + len(c_out)
        scr = refs[at:at + n_scr]

        def run_body():
            body(*scalar_ref, *ins[:n_body_in], *outs, *scr)

        if not comms:
            run_body()
            return
        send_sems, recv_sems, local_sems = refs[at + n_scr:]
        place = _my_place()
        sends, arrivals, locals_ = [], [], []
        i0 = o0 = 0
        for cm in comms:
            cm_in, cm_out = ci[i0:i0 + len(cm.inputs)], co[o0:o0 + len(cm.out_shapes)]
            i0 += len(cm.inputs)
            o0 += len(cm.out_shapes)
            for src, dst, dev, incoming in cm.plan(cm_in, cm_out, place):
                k = len(sends)
                sends.append(pltpu.make_async_remote_copy(src_ref=src, dst_ref=dst, send_sem=send_sems.at[k], recv_sem=recv_sems.at[k],
                                                          device_id=dev, device_id_type=MESH))
                arrivals.append(pltpu.make_async_remote_copy(src_ref=src, dst_ref=incoming, send_sem=send_sems.at[k],
                                                             recv_sem=recv_sems.at[k], device_id=dev, device_id_type=MESH))
            if cm.local is not None:
                for src, dst in cm.local(cm_in, cm_out, place):
                    locals_.append(pltpu.make_async_copy(src, dst, local_sems.at[len(locals_)]))

        def start():
            for cp in locals_ + sends:
                cp.start()

        def finish():
            for cp in arrivals:
                cp.wait_recv()
            for cp in sends:
                cp.wait_send()
            for cp in locals_:
                cp.wait()

        if not grid:
            start()
            run_body()
            finish()
            return
        ids = [pl.program_id(a) for a in range(len(grid))]
        first, last = ids[0] == 0, ids[0] == grid[0] - 1
        for a in range(1, len(grid)):
            first = jnp.logical_and(first, ids[a] == 0)
            last = jnp.logical_and(last, ids[a] == grid[a] - 1)
        pl.when(first)(start)
        run_body()
        pl.when(last)(finish)

    sems = []
    if comms:
        sems = [pltpu.SemaphoreType.DMA((max(n_remote, 1),)), pltpu.SemaphoreType.DMA((max(n_remote, 1),)),
                pltpu.SemaphoreType.DMA((max(n_local, 1),))]
    all_in, all_out = in_specs + [HBM_SPEC] * len(c_in), out_specs + [HBM_SPEC] * len(c_out)
    if scalar is None:
        kwargs = dict(grid=grid, compiler_params=_cparams_nd(len(grid))) if grid else {}
        res = pl.pallas_call(
            wrapped, name=name, in_specs=all_in, out_specs=all_out, out_shape=out_shape + c_out, scratch_shapes=scratch_shapes + sems,
            input_output_aliases=all_aliases, **kwargs)(*args, *c_in)
    else:
        spec = pltpu.PrefetchScalarGridSpec(num_scalar_prefetch=1, grid=grid, in_specs=all_in, out_specs=all_out,
                                            scratch_shapes=scratch_shapes + sems)
        res = pl.pallas_call(
            wrapped, name=name, grid_spec=spec, out_shape=out_shape + c_out, input_output_aliases=all_aliases,
            compiler_params=_cparams_nd(len(grid)))(scalar, *args, *c_in)
    at = n_out
    for cm in comms:
        got = res[at:at + len(cm.out_shapes)]
        at += len(cm.out_shapes)
        if cm.finish is not None:
            cm.finish(got)
    return list(res[:n_out])


def _exchange_alone(name, comms):
    _call(lambda: None, name=name, grid=(), in_specs=[], out_specs=[], out_shape=[], args=[], comms=comms)


def _in_weight_copies(w_hbm, w_v, sem, base=0):
    return [pltpu.make_async_copy(w_hbm.at[q], w_v.at[q], sem.at[base + q]) for q in range(N_CHIPS)]


def _out_weight_copies(w_hbm, w_v, sem, base=0):
    rows = w_hbm.shape[1]
    return [pltpu.make_async_copy(w_hbm.at[q], w_v.at[pl.ds(q * rows, rows)], sem.at[base + q]) for q in range(N_CHIPS)]


def _load_at_first_step(copies):
    @pl.when(pl.program_id(0) == 0)
    def _():
        for cp in copies:
            cp.start()
        for cp in copies:
            cp.wait()


def _ffn_fwd(x, g, win_g, wout_g, name, comms=()):
    t, d = x.shape
    c = win_g.shape[-1]
    ff = 2 * c
    tm = min(512, t)

    def body(x_ref, g_ref, win_hbm, wout_hbm, xo_ref, h_ref, xn_ref, win_v, wout_v, sem):
        _load_at_first_step(_in_weight_copies(win_hbm, win_v, sem) + _out_weight_copies(wout_hbm, wout_v, sem, N_CHIPS))
        xv = x_ref[...]
        xh, _ = _rms_fwd(xv)
        xn = (xh * g_ref[...]).astype(BF16)
        xn_ref[...] = xn
        acc = jnp.zeros((tm, d), F32)
        for j in range(2):
            gate = _dot(xn, win_v[j])
            up = _dot(xn, win_v[j + 2])
            h_ref[:, j * c:(j + 1) * c] = gate.astype(BF16)
            h_ref[:, ff + j * c:ff + (j + 1) * c] = up.astype(BF16)
            act = (gate * _sigmoid(gate) * up).astype(BF16)
            acc = acc + _dot(act, wout_v[j * c:(j + 1) * c, :])
        xo_ref[...] = xv + 0.5 * acc

    return _call(
        body, name=name, grid=(t // tm,),
        in_specs=[_row_spec(tm, d), _const_spec((1, d)), HBM_SPEC, HBM_SPEC],
        out_specs=[_row_spec(tm, d), _row_spec(tm, 2 * ff), _row_spec(tm, d)],
        out_shape=[jax.ShapeDtypeStruct((t, d), F32), jax.ShapeDtypeStruct((t, 2 * ff), BF16), jax.ShapeDtypeStruct((t, d), BF16)],
        scratch_shapes=[pltpu.VMEM((N_CHIPS, d, c), BF16), pltpu.VMEM((ff, d), BF16), pltpu.SemaphoreType.DMA((2 * N_CHIPS,))],
        args=(x, g, win_g, wout_g), comms=comms)


def _ffn_bwd(x, dy, h, g, win_g, wout_g, name, comms=(), after=()):
    t, d = x.shape
    c = win_g.shape[-1]
    ff = 2 * c
    tm = min(256, t)

    def body(x_ref, dy_ref, h_ref, g_ref, win_hbm, wout_hbm, dx_ref, dh_ref, act_ref, dg_ref, win_v, wout_v, sem):
        _load_at_first_step(_in_weight_copies(win_hbm, win_v, sem) + _out_weight_copies(wout_hbm, wout_v, sem, N_CHIPS))

        @pl.when(pl.program_id(0) == 0)
        def _():
            dg_ref[...] = jnp.zeros_like(dg_ref)

        xv, dyv, gv = x_ref[...], dy_ref[...], g_ref[...]
        xh, r = _rms_fwd(xv)
        dyh = (0.5 * dyv).astype(BF16)
        dxn = jnp.zeros((tm, d), F32)
        for j in range(2):
            gate = h_ref[:, j * c:(j + 1) * c].astype(F32)
            up = h_ref[:, ff + j * c:ff + (j + 1) * c].astype(F32)
            dact = _dot_nt(dyh, wout_v[j * c:(j + 1) * c, :])
            s = _sigmoid(gate)
            sl = gate * s
            act_ref[:, j * c:(j + 1) * c] = (sl * up).astype(BF16)
            dgate = (dact * up * (s + sl * (1.0 - s))).astype(BF16)
            dup = (dact * sl).astype(BF16)
            dh_ref[:, j * c:(j + 1) * c] = dgate
            dh_ref[:, ff + j * c:ff + (j + 1) * c] = dup
            dxn = dxn + _dot_nt(dgate, win_v[j]) + _dot_nt(dup, win_v[j + 2])
        dg_ref[...] += jnp.sum(dxn * xh, axis=0, keepdims=True)
        dx_ref[...] = dyv + _rms_bwd(dxn, xh, r, gv)

    return _call(
        body, name=name, grid=(t // tm,),
        in_specs=[_row_spec(tm, d), _row_spec(tm, d), _row_spec(tm, 2 * ff), _const_spec((1, d)), HBM_SPEC, HBM_SPEC],
        out_specs=[_row_spec(tm, d), _row_spec(tm, 2 * ff), _row_spec(tm, ff), _const_spec((1, d))],
        out_shape=[jax.ShapeDtypeStruct((t, d), F32), jax.ShapeDtypeStruct((t, 2 * ff), BF16), jax.ShapeDtypeStruct((t, ff), BF16),
                   jax.ShapeDtypeStruct((1, d), F32)],
        scratch_shapes=[pltpu.VMEM((N_CHIPS, d, c), BF16), pltpu.VMEM((ff, d), BF16), pltpu.SemaphoreType.DMA((2 * N_CHIPS,))],
        args=(x, dy, h, g, win_g, wout_g), comms=comms, after=after)


def _norm_matmul(x, g, win_g, name, comms=()):
    t, d = x.shape
    c = win_g.shape[-1]
    tm = min(512, t)

    def body(x_ref, g_ref, win_hbm, o_ref, xn_ref, win_v, sem):
        _load_at_first_step(_in_weight_copies(win_hbm, win_v, sem))
        xh, _ = _rms_fwd(x_ref[...])
        xn = (xh * g_ref[...]).astype(BF16)
        xn_ref[...] = xn
        for q in range(N_CHIPS):
            o_ref[:, q * c:(q + 1) * c] = _dot(xn, win_v[q])

    return _call(
        body, name=name, grid=(t // tm,),
        in_specs=[_row_spec(tm, d), _const_spec((1, d)), HBM_SPEC],
        out_specs=[_row_spec(tm, N_CHIPS * c), _row_spec(tm, d)],
        out_shape=[jax.ShapeDtypeStruct((t, N_CHIPS * c), F32), jax.ShapeDtypeStruct((t, d), BF16)],
        scratch_shapes=[pltpu.VMEM((N_CHIPS, d, c), BF16), pltpu.SemaphoreType.DMA((N_CHIPS,))],
        args=(x, g, win_g), comms=comms)


def _proj_in_bwd_tail(dh, win_v, c):
    dxn = _dot_nt(dh[:, 0:c], win_v[0])
    for q in range(1, N_CHIPS):
        dxn = dxn + _dot_nt(dh[:, q * c:(q + 1) * c], win_v[q])
    return dxn


def _prev_halo_spec(tm, cols):
    return pl.BlockSpec((HALO, cols), lambda i: (jnp.maximum(i * (tm // HALO) - 1, 0), 0))


def _next_halo_spec(tm, cols, t):
    last = t // HALO - 1
    return pl.BlockSpec((HALO, cols), lambda i: (jnp.minimum((i + 1) * (tm // HALO), last), 0))


def _shift_down(w, k):
    return w if k == 0 else pltpu.roll(w, k, 0)


def _shift_up(w, k):
    return w if k == 0 else pltpu.roll(w, w.shape[0] - k, 0)


def _pool_counts(i, tm):
    pos = (i * tm + lax.broadcasted_iota(jnp.int32, (tm, POOL_CH), 0) + 1).astype(F32)
    lane = lax.broadcasted_iota(jnp.int32, (tm, POOL_CH), 1)
    win = jnp.where(lane < POOL_GC, 2.0, jnp.where(lane < 2 * POOL_GC, 4.0, jnp.where(lane < 3 * POOL_GC, 8.0, 16.0)))
    return jnp.minimum(pos, win)


def _group_select(parts):
    return jnp.concatenate([p[:, k * POOL_GC:(k + 1) * POOL_GC] for k, p in enumerate(parts)], axis=1)


TAP_ROWS = 128


def _tap_blocks(tm, block_fn):
    cols = []
    for ch in range(CONV_CH // POOL_GC):
        lanes = slice(ch * POOL_GC, (ch + 1) * POOL_GC)
        cols.append(jnp.concatenate([block_fn(r, lanes) for r in range(tm // TAP_ROWS)], axis=0))
    return jnp.concatenate(cols, axis=1)


def _conv_block(win, taps):
    acc = jnp.zeros((TAP_ROWS, win.shape[1]), F32)
    for k in range(CONV_WIDTH):
        acc = acc + taps[k:k + 1, :] * _shift_down(win, CONV_WIDTH - 1 - k)[HALO:, :]
    return acc


def _conv_block_transposed(win, taps):
    acc = jnp.zeros((TAP_ROWS, win.shape[1]), F32)
    for k in range(CONV_WIDTH):
        acc = acc + taps[k:k + 1, :] * _shift_up(win, CONV_WIDTH - 1 - k)[0:TAP_ROWS, :]
    return acc


def _mix0_recompute(i, tm, h_cur, h_prev, conv_w, conv_b, y=None):
    prev = jnp.where(i > 0, h_prev, 0.0)
    win = jnp.concatenate([prev, h_cur], axis=0)
    u_w = win[:, 0:POOL_CH]
    a_w = win[:, POOL_CH:POOL_CH + CONV_CH]
    gt_w = win[:, POOL_CH + CONV_CH:]
    g_w = a_w * _sigmoid(gt_w)
    if y is None:
        y = _tap_blocks(tm, lambda r, lanes: _conv_block(g_w[r * TAP_ROWS:(r + 1) * TAP_ROWS + HALO, lanes], conv_w[:, lanes])) + conv_b
    s2 = u_w + _shift_down(u_w, 1)
    s4 = s2 + _shift_down(s2, 2)
    s8 = s4 + _shift_down(s4, 4)
    s16 = s8 + _shift_down(s8, 8)
    sums = _group_select([s2[HALO:], s4[HALO:], s8[HALO:], s16[HALO:]])
    cnt = _pool_counts(i, tm)
    pooled = sums / cnt - h_cur[:, 0:POOL_CH]
    return g_w, y, pooled, cnt


def _pool_linear(pooled, pw_ref, pb):
    return jnp.concatenate(
        [_dot(pooled[:, k * POOL_GC:(k + 1) * POOL_GC].astype(BF16), pw_ref[k].astype(BF16)) for k in range(len(POOL_WINDOWS))], axis=1) + pb


def _mix0_fwd(x, h0, pool_w, pool_b, pool_scale, conv_w, conv_b, ln_g, ln_b, wout_g, name, comms=()):
    t, d = x.shape
    tm = min(512, t)
    hc = h0.shape[1]

    def body(x_ref, h_ref, hp_ref, pw_ref, pb_ref, ps_ref, cw_ref, cb_ref, lg_ref, lb_ref, wout_hbm, xo_ref, ycat_ref, y_ref, wout_v, sem):
        _load_at_first_step(_out_weight_copies(wout_hbm, wout_v, sem))
        i = pl.program_id(0)
        _, y, pooled, _ = _mix0_recompute(i, tm, h_ref[...], hp_ref[...], cw_ref[...], cb_ref[...])
        y_ref[...] = y
        yhat, _ = _ln_fwd(y)
        yn = yhat * lg_ref[...] + lb_ref[...]
        yb = yn * _sigmoid(yn)
        ya = _pool_linear(pooled, pw_ref, pb_ref[...]) * ps_ref[...]
        ycat = jnp.concatenate([ya, yb], axis=1).astype(BF16)
        ycat_ref[...] = ycat
        xo_ref[...] = x_ref[...] + _dot(ycat, wout_v[...])

    return _call(
        body, name=name, grid=(t // tm,),
        in_specs=[_row_spec(tm, d), _row_spec(tm, hc), _prev_halo_spec(tm, hc), _const_spec(pool_w.shape), _const_spec((1, POOL_CH)),
                  _const_spec((1, POOL_CH)), _const_spec(conv_w.shape), _const_spec((1, CONV_CH)), _const_spec((1, CONV_CH)),
                  _const_spec((1, CONV_CH)), HBM_SPEC],
        out_specs=[_row_spec(tm, d), _row_spec(tm, d), _row_spec(tm, CONV_CH)],
        out_shape=[jax.ShapeDtypeStruct((t, d), F32), jax.ShapeDtypeStruct((t, d), BF16), jax.ShapeDtypeStruct((t, CONV_CH), F32)],
        scratch_shapes=[pltpu.VMEM((d, d), BF16), pltpu.SemaphoreType.DMA((N_CHIPS,))],
        args=(x, h0, h0, pool_w, pool_b, pool_scale, conv_w, conv_b, ln_g, ln_b, wout_g), comms=comms)


def _mix0_bwd_a(dy, h0, y_conv, pool_w, pool_b, pool_scale, conv_w, conv_b, ln_g, ln_b, wout_g, name):
    t, d = dy.shape
    tm = min(512, t)
    hc = h0.shape[1]
    n_small = 40

    def body(dy_ref, h_ref, hp_ref, y_ref, pw_ref, pb_ref, ps_ref, cw_ref, cb_ref, lg_ref, lb_ref, wout_hbm,
             dconv_ref, dpc_ref, dpw_ref, small_ref, wout_v, sem):
        _load_at_first_step(_out_weight_copies(wout_hbm, wout_v, sem))
        i = pl.program_id(0)

        @pl.when(i == 0)
        def _():
            dpw_ref[...] = jnp.zeros_like(dpw_ref)
            small_ref[...] = jnp.zeros_like(small_ref)

        g_w, y, pooled, cnt = _mix0_recompute(i, tm, h_ref[...], hp_ref[...], cw_ref[...], cb_ref[...], y_ref[...])
        yhat, rs = _ln_fwd(y)
        lg = lg_ref[...]
        yn = yhat * lg + lb_ref[...]
        mixed = _pool_linear(pooled, pw_ref, pb_ref[...])
        dycat = _dot_nt(dy_ref[...].astype(BF16), wout_v[...])
        dya, dyb = dycat[:, 0:POOL_CH], dycat[:, POOL_CH:]
        sg = _sigmoid(yn)
        dyn = dyb * (sg * (1.0 + yn * (1.0 - sg)))
        dyc = _ln_bwd(dyn * lg, yhat, rs)
        dconv_ref[...] = dyc

        def add_row(k, value):
            small_ref[k:k + 1, :] += jnp.sum(value, axis=0, keepdims=True)

        for ch in range(CONV_CH // POOL_GC):
            lanes = slice(ch * POOL_GC, (ch + 1) * POOL_GC)
            taps = [jnp.zeros((1, POOL_GC), F32)] * CONV_WIDTH
            for r in range(tm // TAP_ROWS):
                win = g_w[r * TAP_ROWS:(r + 1) * TAP_ROWS + HALO, lanes]
                d = dyc[r * TAP_ROWS:(r + 1) * TAP_ROWS, lanes]
                for k in range(CONV_WIDTH):
                    taps[k] = taps[k] + jnp.sum(d * _shift_down(win, CONV_WIDTH - 1 - k)[HALO:, :], axis=0, keepdims=True)
            for k in range(CONV_WIDTH):
                small_ref[k:k + 1, lanes] += taps[k]
        add_row(32, dyc)
        add_row(33, dyn * yhat)
        add_row(34, dyn)
        scale = ps_ref[...]
        dmixed = dya * scale
        add_row(35, dya * mixed)
        add_row(36, dmixed)
        dmb = dmixed.astype(BF16)
        dpooled = []
        for k in range(len(POOL_WINDOWS)):
            sl = slice(k * POOL_GC, (k + 1) * POOL_GC)
            dpw_ref[k] += _dot_tn(pooled[:, sl].astype(BF16), dmb[:, sl])
            dpooled.append(_dot_nt(dmb[:, sl], pw_ref[k].astype(BF16)))
        dpc_ref[...] = jnp.concatenate(dpooled, axis=1) / cnt

    return _call(
        body, name=name, grid=(t // tm,),
        in_specs=[_row_spec(tm, d), _row_spec(tm, hc), _prev_halo_spec(tm, hc), _row_spec(tm, CONV_CH), _const_spec(pool_w.shape),
                  _const_spec((1, POOL_CH)), _const_spec((1, POOL_CH)), _const_spec(conv_w.shape), _const_spec((1, CONV_CH)),
                  _const_spec((1, CONV_CH)), _const_spec((1, CONV_CH)), HBM_SPEC],
        out_specs=[_row_spec(tm, CONV_CH), _row_spec(tm, POOL_CH), _const_spec(pool_w.shape), _const_spec((n_small, CONV_CH))],
        out_shape=[jax.ShapeDtypeStruct((t, CONV_CH), F32), jax.ShapeDtypeStruct((t, POOL_CH), F32),
                   jax.ShapeDtypeStruct(pool_w.shape, F32), jax.ShapeDtypeStruct((n_small, CONV_CH), F32)],
        scratch_shapes=[pltpu.VMEM((d, d), BF16), pltpu.SemaphoreType.DMA((N_CHIPS,))],
        args=(dy, h0, h0, y_conv, pool_w, pool_b, pool_scale, conv_w, conv_b, ln_g, ln_b, wout_g))


def _mix0_bwd_b(x, dy, h0, dconv, dpc, g, conv_w, win_g, name):
    t, d = x.shape
    tm = min(512, t)
    hc = h0.shape[1]
    c = win_g.shape[-1]
    n_tiles = t // tm

    def body(x_ref, dy_ref, h_ref, dc_ref, dcn_ref, dp_ref, dpn_ref, g_ref, cw_ref, win_hbm, dx_ref, dh_ref, dg_ref, win_v, sem):
        _load_at_first_step(_in_weight_copies(win_hbm, win_v, sem))
        i = pl.program_id(0)

        @pl.when(i == 0)
        def _():
            dg_ref[...] = jnp.zeros_like(dg_ref)

        not_last = i < n_tiles - 1
        dc_w = jnp.concatenate([dc_ref[...], jnp.where(not_last, dcn_ref[...], 0.0)], axis=0)
        dp_w = jnp.concatenate([dp_ref[...], jnp.where(not_last, dpn_ref[...], 0.0)], axis=0)
        cw = cw_ref[...]
        dg = _tap_blocks(tm, lambda r, lanes: _conv_block_transposed(dc_w[r * TAP_ROWS:(r + 1) * TAP_ROWS + HALO, lanes], cw[:, lanes]))
        a2 = dp_w + _shift_up(dp_w, 1)
        a4 = a2 + _shift_up(a2, 2)
        a8 = a4 + _shift_up(a4, 4)
        a16 = a8 + _shift_up(a8, 8)
        back = _group_select([a2[0:tm], a4[0:tm], a8[0:tm], a16[0:tm]])
        du = back - dp_ref[...] * _pool_counts(i, tm)
        hv = h_ref[...]
        a = hv[:, POOL_CH:POOL_CH + CONV_CH]
        sig = _sigmoid(hv[:, POOL_CH + CONV_CH:])
        dh = jnp.concatenate([du, dg * sig, dg * a * sig * (1.0 - sig)], axis=1).astype(BF16)
        dh_ref[...] = dh
        dxn = _proj_in_bwd_tail(dh, win_v, c)
        xh, r = _rms_fwd(x_ref[...])
        dg_ref[...] += jnp.sum(dxn * xh, axis=0, keepdims=True)
        dx_ref[...] = dy_ref[...] + _rms_bwd(dxn, xh, r, g_ref[...])

    return _call(
        body, name=name, grid=(n_tiles,),
        in_specs=[_row_spec(tm, d), _row_spec(tm, d), _row_spec(tm, hc), _row_spec(tm, CONV_CH), _next_halo_spec(tm, CONV_CH, t),
                  _row_spec(tm, POOL_CH), _next_halo_spec(tm, POOL_CH, t), _const_spec((1, d)), _const_spec(conv_w.shape), HBM_SPEC],
        out_specs=[_row_spec(tm, d), _row_spec(tm, hc), _const_spec((1, d))],
        out_shape=[jax.ShapeDtypeStruct((t, d), F32), jax.ShapeDtypeStruct((t, hc), BF16), jax.ShapeDtypeStruct((1, d), F32)],
        scratch_shapes=[pltpu.VMEM((N_CHIPS, d, c), BF16), pltpu.SemaphoreType.DMA((N_CHIPS,))],
        args=(x, dy, h0, dconv, dconv, dpc, dpc, g, conv_w, win_g))


SQRT_HALF = 0.7071067811865476
INV_SQRT_2PI = 0.3989422804014327


def _causal_mask():
    return (lax.broadcasted_iota(jnp.int32, (CHUNK, CHUNK), 1) <= lax.broadcasted_iota(jnp.int32, (CHUNK, CHUNK), 0)).astype(F32)


def _sgu_recompute(pre, lg, lb):
    half = pre.shape[1] // 2
    phi = 0.5 * (1.0 + lax.erf(pre * SQRT_HALF))
    z = pre * phi
    u, v = z[:, 0:half], z[:, half:]
    vhat, rs = _ln_fwd(v)
    return u, vhat, rs, vhat * lg + lb, phi


def _sgu_spatial(vln, w_ref, bt, tm):
    mask = _causal_mask()
    wm = [(w_ref[hd] * mask).astype(BF16) for hd in range(SGU_HEADS)]
    vb = vln.astype(BF16)
    rows = []
    for ch in range(tm // CHUNK):
        blocks = [_dot(wm[hd], vb[ch * CHUNK:(ch + 1) * CHUNK, hd * CHUNK:(hd + 1) * CHUNK]) + bt[:, hd:hd + 1] for hd in range(SGU_HEADS)]
        rows.append(jnp.concatenate(blocks, axis=1))
    return jnp.concatenate(rows, axis=0), wm


def _sgu_fwd(x, pre, ln_g, ln_b, w, bt, wout_g, name):
    t, d = x.shape
    tm = min(512, t)
    pc = pre.shape[1]

    def body(x_ref, pre_ref, lg_ref, lb_ref, w_ref, bt_ref, wout_hbm, xo_ref, p_ref, wout_v, sem):
        _load_at_first_step(_out_weight_copies(wout_hbm, wout_v, sem))
        u, _, _, vln, _ = _sgu_recompute(pre_ref[...], lg_ref[...], lb_ref[...])
        vo, _ = _sgu_spatial(vln, w_ref, bt_ref[...], tm)
        p = (u * vo).astype(BF16)
        p_ref[...] = p
        xo_ref[...] = x_ref[...] + _dot(p, wout_v[...])

    return _call(
        body, name=name, grid=(t // tm,),
        in_specs=[_row_spec(tm, d), _row_spec(tm, pc), _const_spec((1, d)), _const_spec((1, d)), _const_spec(w.shape),
                  _const_spec(bt.shape), HBM_SPEC],
        out_specs=[_row_spec(tm, d), _row_spec(tm, d)],
        out_shape=[jax.ShapeDtypeStruct((t, d), F32), jax.ShapeDtypeStruct((t, d), BF16)],
        scratch_shapes=[pltpu.VMEM((d, d), BF16), pltpu.SemaphoreType.DMA((N_CHIPS,))],
        args=(x, pre, ln_g, ln_b, w, bt, wout_g))


def _sgu_bwd(x, dy, pre, g, ln_g, ln_b, w, bt, win_g, wout_g, name, comms=()):
    t, d = x.shape
    tm = min(512, t)
    pc = pre.shape[1]
    c = win_g.shape[-1]

    def body(x_ref, dy_ref, pre_ref, g_ref, lg_ref, lb_ref, w_ref, bt_ref, win_hbm, wout_hbm,
             dx_ref, dpre_ref, dg_ref, dlg_ref, dlb_ref, dw_ref, dbt_ref, win_v, wout_v, sem):
        _load_at_first_step(_in_weight_copies(win_hbm, win_v, sem) + _out_weight_copies(wout_hbm, wout_v, sem, N_CHIPS))
        i = pl.program_id(0)

        @pl.when(i == 0)
        def _():
            for ref in (dg_ref, dlg_ref, dlb_ref, dw_ref, dbt_ref):
                ref[...] = jnp.zeros_like(ref)

        prev = pre_ref[...]
        lg = lg_ref[...]
        u, vhat, rs, vln, phi = _sgu_recompute(prev, lg, lb_ref[...])
        vo, wm = _sgu_spatial(vln, w_ref, bt_ref[...], tm)
        dp = _dot_nt(dy_ref[...].astype(BF16), wout_v[...])
        du = dp * vo
        dvo = dp * u
        dvob = dvo.astype(BF16)
        vb = vln.astype(BF16)
        head_lane = lax.broadcasted_iota(jnp.int32, (CHUNK, SGU_HEADS), 1)
        dbt = jnp.zeros((CHUNK, SGU_HEADS), F32)
        dw = [jnp.zeros((CHUNK, CHUNK), F32) for _ in range(SGU_HEADS)]
        rows = []
        for ch in range(tm // CHUNK):
            rs_ = slice(ch * CHUNK, (ch + 1) * CHUNK)
            blocks = []
            for hd in range(SGU_HEADS):
                cs = slice(hd * CHUNK, (hd + 1) * CHUNK)
                dbt = dbt + jnp.where(head_lane == hd, jnp.sum(dvo[rs_, cs], axis=1, keepdims=True), 0.0)
                dw[hd] = dw[hd] + _dot_nt(dvob[rs_, cs], vb[rs_, cs])
                blocks.append(_dot_tn(wm[hd], dvob[rs_, cs]))
            rows.append(jnp.concatenate(blocks, axis=1))
        dvln = jnp.concatenate(rows, axis=0)
        mask = _causal_mask()
        for hd in range(SGU_HEADS):
            dw_ref[hd] += dw[hd] * mask
        dbt_ref[...] += dbt
        dlg_ref[...] += jnp.sum(dvln * vhat, axis=0, keepdims=True)
        dlb_ref[...] += jnp.sum(dvln, axis=0, keepdims=True)
        dv = _ln_bwd(dvln * lg, vhat, rs)
        gelu_grad = phi + prev * jnp.exp(-0.5 * prev * prev) * INV_SQRT_2PI
        dpre = (jnp.concatenate([du, dv], axis=1) * gelu_grad).astype(BF16)
        dpre_ref[...] = dpre
        dxn = _proj_in_bwd_tail(dpre, win_v, c)
        xh, r = _rms_fwd(x_ref[...])
        dg_ref[...] += jnp.sum(dxn * xh, axis=0, keepdims=True)
        dx_ref[...] = dy_ref[...] + _rms_bwd(dxn, xh, r, g_ref[...])

    return _call(
        body, name=name, grid=(t // tm,),
        in_specs=[_row_spec(tm, d), _row_spec(tm, d), _row_spec(tm, pc), _const_spec((1, d)), _const_spec((1, d)), _const_spec((1, d)),
                  _const_spec(w.shape), _const_spec(bt.shape), HBM_SPEC, HBM_SPEC],
        out_specs=[_row_spec(tm, d), _row_spec(tm, pc), _const_spec((1, d)), _const_spec((1, d)), _const_spec((1, d)),
                   _const_spec(w.shape), _const_spec(bt.shape)],
        out_shape=[jax.ShapeDtypeStruct((t, d), F32), jax.ShapeDtypeStruct((t, pc), BF16), jax.ShapeDtypeStruct((1, d), F32),
                   jax.ShapeDtypeStruct((1, d), F32), jax.ShapeDtypeStruct((1, d), F32), jax.ShapeDtypeStruct(w.shape, F32),
                   jax.ShapeDtypeStruct(bt.shape, F32)],
        scratch_shapes=[pltpu.VMEM((N_CHIPS, d, c), BF16), pltpu.VMEM((d, d), BF16), pltpu.SemaphoreType.DMA((2 * N_CHIPS,))],
        args=(x, dy, pre, g, ln_g, ln_b, w, bt, win_g, wout_g), comms=comms)


def _final_loss(x, tgt, g, name):
    t, d = x.shape
    tm = min(512, t)

    def body(x_ref, t_ref, g_ref, dx_ref, loss_ref, dg_ref):
        @pl.when(pl.program_id(0) == 0)
        def _():
            loss_ref[...] = jnp.zeros_like(loss_ref)
            dg_ref[...] = jnp.zeros_like(dg_ref)

        gv = g_ref[...]
        xh, r = _rms_fwd(x_ref[...])
        diff = xh * gv - t_ref[...]
        loss_ref[...] += 0.5 * jnp.sum(jnp.sum(diff * diff, axis=1, keepdims=True), axis=0, keepdims=True) / d
        dout = diff / d
        dg_ref[...] += jnp.sum(dout * xh, axis=0, keepdims=True)
        dx_ref[...] = _rms_bwd(dout, xh, r, gv)

    return _call(
        body, name=name, grid=(t // tm,),
        in_specs=[_row_spec(tm, d), _row_spec(tm, d), _const_spec((1, d))],
        out_specs=[_row_spec(tm, d), _const_spec((1, 1)), _const_spec((1, d))],
        out_shape=[jax.ShapeDtypeStruct((t, d), F32), jax.ShapeDtypeStruct((1, 1), F32), jax.ShapeDtypeStruct((1, d), F32)],
        args=(x, tgt, g))


def _tn_matmul(a, b, scale, bm, bn, name, comms=(), after=()):
    t, m = a.shape
    n = b.shape[1]
    tk = min(2048, t)
    bm, bn = min(bm, m), min(bn, n)
    nk = t // tk

    def body(a_ref, b_ref, o_ref, acc_ref):
        k = pl.program_id(2)

        @pl.when(k == 0)
        def _():
            acc_ref[...] = jnp.zeros_like(acc_ref)

        bv = b_ref[...]
        if bv.dtype != BF16:
            bv = (scale * bv).astype(BF16)
        acc_ref[...] += _dot_tn(a_ref[...], bv)

        @pl.when(k == nk - 1)
        def _():
            o_ref[...] = acc_ref[...].astype(BF16)

    return _call(
        body, name=name, grid=(m // bm, n // bn, nk),
        in_specs=[pl.BlockSpec((tk, bm), lambda i, j, k: (k, i)), pl.BlockSpec((tk, bn), lambda i, j, k: (k, j))],
        out_specs=[pl.BlockSpec((bm, bn), lambda i, j, k: (i, j))],
        out_shape=[jax.ShapeDtypeStruct((m, n), BF16)],
        scratch_shapes=[pltpu.VMEM((bm, bn), F32)],
        args=(a, b), comms=comms, after=after)[0]


def _row_tile(rows, cols, budget_bytes=2 * 1024 * 1024):
    best = None
    for cand in range(16, rows + 1, 16):
        if rows % cand == 0 and cand * cols * 4 <= budget_bytes:
            best = cand
    return best or rows


def _scalar_grid(grid, in_specs, out_specs):
    return pltpu.PrefetchScalarGridSpec(num_scalar_prefetch=1, grid=grid, in_specs=in_specs, out_specs=out_specs)


def _cast_into_slot(w, layer, me, name):
    _, rows, cols = w.shape
    tr = _row_tile(rows, cols)

    def body(me_ref, w_ref, o_ref):
        o_ref[...] = w_ref[...].astype(BF16)

    return pl.pallas_call(
        body, name=name,
        grid_spec=_scalar_grid((rows // tr,), [pl.BlockSpec((None, tr, cols), lambda i, me: (layer, i, 0))],
                               pl.BlockSpec((None, tr, cols), lambda i, me: (me[0], i, 0))),
        out_shape=jax.ShapeDtypeStruct((N_CHIPS, rows, cols), BF16), compiler_params=_cparams())(me, w)


def _add_half(view, other, core, name):
    q, _, r, c = view.shape
    tr = _row_tile(r, c)

    def body(core_ref, a_ref, b_ref, o_ref):
        o_ref[...] = (a_ref[...].astype(F32) + b_ref[...].astype(F32)).astype(BF16)

    return pl.pallas_call(
        body, name=name,
        grid_spec=_scalar_grid((q, r // tr), [pl.BlockSpec((None, None, tr, c), lambda k, i, core: (k, core[0], i, 0)),
                                             pl.BlockSpec((None, tr, c), lambda k, i, core: (k, i, 0))],
                               pl.BlockSpec((None, tr, c), lambda k, i, core: (k, i, 0))),
        out_shape=jax.ShapeDtypeStruct((q, r, c), BF16), compiler_params=_cparams_nd(2))(core, view, other)


def _reduce_piece(partial, staged, me, column_sharded, name):
    _, r, c = staged.shape
    tr = _row_tile(r, c, budget_bytes=1024 * 1024)
    nt = r // tr
    if column_sharded:
        own2d = partial.reshape(r, N_CHIPS * c)
        own_spec = pl.BlockSpec((tr, c), lambda i, me: (i, me[0]))
    else:
        own2d = partial.reshape(N_CHIPS * r, c)
        own_spec = pl.BlockSpec((tr, c), lambda i, me: (me[0] * nt + i, 0))
    ring = [pl.BlockSpec((None, tr, c), lambda i, me, k=k: ((me[0] + k) % N_CHIPS, i, 0)) for k in (1, 2, 3)]

    def body(me_ref, own_ref, s1_ref, s2_ref, s3_ref, o_ref):
        o_ref[...] = ((own_ref[...].astype(F32) + s1_ref[...].astype(F32)) + s2_ref[...].astype(F32)) + s3_ref[...].astype(F32)

    return pl.pallas_call(
        body, name=name, grid_spec=_scalar_grid((nt,), [own_spec] + ring, pl.BlockSpec((tr, c), lambda i, me: (i, 0))),
        out_shape=jax.ShapeDtypeStruct((r, c), F32), compiler_params=_cparams())(me, own2d, staged, staged, staged)


def _sum_leading(s, name):
    n, rows, cols = s.shape
    tr = _row_tile(rows, cols, budget_bytes=1024 * 1024)

    def body(s_ref, o_ref):
        acc = s_ref[0].astype(F32)
        for k in range(1, n):
            acc = acc + s_ref[k].astype(F32)
        o_ref[...] = acc

    return pl.pallas_call(
        body, name=name, grid=(rows // tr,), in_specs=[pl.BlockSpec((n, tr, cols), lambda i: (0, i, 0))], out_specs=_row_spec(tr, cols),
        out_shape=jax.ShapeDtypeStruct((rows, cols), F32), compiler_params=_cparams())(s)


ADAM_C1 = 1.0 / (1.0 - ADAM_B1 ** ADAM_STEP)
ADAM_C2 = 1.0 / (1.0 - ADAM_B2 ** ADAM_STEP)


def _adamw_math(w, g, m, v):
    mn = ADAM_B1 * m + (1.0 - ADAM_B1) * g
    vn = ADAM_B2 * v + (1.0 - ADAM_B2) * (g * g)
    return -ADAM_LR * ((mn * ADAM_C1) / (jnp.sqrt(vn * ADAM_C2) + ADAM_EPS) + ADAM_WD * w), mn, vn


def _adamw(w, g, m, v, name):
    shape = w.shape
    cols = shape[-1] if w.ndim > 1 else 128
    w2, g2, m2, v2 = (a.reshape(-1, cols) for a in (w, g, m, v))
    rows = w2.shape[0]
    tr = _row_tile(rows, cols, budget_bytes=1024 * 1024)

    def body(w_ref, g_ref, m_ref, v_ref, d_ref, mo_ref, vo_ref):
        d_ref[...], mo_ref[...], vo_ref[...] = _adamw_math(w_ref[...], g_ref[...], m_ref[...], v_ref[...])

    spec = _row_spec(tr, cols)
    outs = pl.pallas_call(
        body, name=name, grid=(rows // tr,), in_specs=[spec] * 4, out_specs=[spec] * 3,
        out_shape=[jax.ShapeDtypeStruct((rows, cols), F32)] * 3, compiler_params=_cparams())(w2, g2, m2, v2)
    return tuple(o.reshape(shape) for o in outs)


def _adamw_sharded(w, g_mine, g_sibling, m, v, core, layer, prev, name, comms=(), after=()):
    n_layers, r, c = w.shape
    half = r // 2
    tr = _row_tile(half, c)
    nt = half // tr

    def body(core_ref, w_ref, gm_ref, gs_ref, m_ref, v_ref, *rest):
        g_ref, d_ref, mo_ref, vo_ref = rest[-4:]
        gv = jnp.where(pl.program_id(0) == core_ref[0], gm_ref[...], gs_ref[...])
        g_ref[...] = gv
        d_ref[...], mo_ref[...], vo_ref[...] = _adamw_math(w_ref[...], gv, m_ref[...], v_ref[...])

    full = pl.BlockSpec((None, tr, c), lambda h, i, core: (layer, h * nt + i, 0))
    part = pl.BlockSpec((tr, c), lambda h, i, core: (i, 0))
    args = [w, g_mine, g_sibling, m, v]
    in_specs = [full, part, part, full, full]
    aliases = {}
    if prev is not None:
        aliases = {len(args) + k: k for k in range(4)}
        args += list(prev)
        in_specs += [pl.BlockSpec(memory_space=pl.ANY)] * 4
    args += list(after)
    in_specs += [pl.BlockSpec(memory_space=pl.ANY)] * len(after)
    return _call(body, name=name, grid=(2, nt), in_specs=in_specs, out_specs=[full] * 4, out_shape=[jax.ShapeDtypeStruct(w.shape, F32)] * 4,
                 args=args, comms=comms, scalar=core, aliases=aliases)


BIG_IN = ("ffn1_w_in", "ffn2_w_in", "ab_w_in", "sgu_w_in")
BIG_OUT = ("ffn1_w_out", "ffn2_w_out", "ab_w_out", "sgu_w_out")
BIG = BIG_IN + BIG_OUT


class _Gatherer:
    def __init__(self, slots):
        self.slots = dict(slots)

    def _stage(self, keys, d2d):
        n = len(keys)

        def plan(ins, outs, place):
            x, y, c = place
            me = 2 * x + y
            remote = []
            for a in range(n):
                rows = ins[a].shape[1] // 2

                def half(ref, q, core, rows=rows):
                    return ref.at[q, pl.ds(core * rows, rows), :]

                for (px, py) in _other_chips(x, y):
                    q = 2 * px + py
                    if d2d:
                        remote.append((half(ins[a], q, c), half(outs[a], q, c), (x, y, 1 - c), half(outs[a], q, 1 - c)))
                    else:
                        remote.append((half(ins[a], me, c), half(outs[a], me, c), (px, py, c), half(outs[a], q, c)))
            return remote

        def finish(outs):
            for k, o in zip(keys, outs):
                self.slots[k] = o

        arrays = [self.slots[k] for k in keys]
        return _Exchange(arrays, [_sds(a) for a in arrays], plan, 3 * n, {a: a for a in range(n)}, finish)

    def direct(self, keys):
        n = len(keys)

        def plan(ins, outs, place):
            x, y, c = place
            me = 2 * x + y
            return [(ins[a].at[me], outs[a].at[me], (px, py, c), outs[a].at[2 * px + py]) for a in range(n) for (px, py) in _other_chips(x, y)]

        def finish(outs):
            for k, o in zip(keys, outs):
                self.slots[k] = o

        arrays = [self.slots[k] for k in keys]
        return _Exchange(arrays, [_sds(a) for a in arrays], plan, 3 * n, {a: a for a in range(n)}, finish)

    def ici(self, keys):
        return self._stage(keys, False)

    def d2d(self, keys):
        return self._stage(keys, True)


class _Reducer:
    def __init__(self, me, core):
        self.me, self.core = me, core
        self.views, self.partial, self.staged, self.mine, self.theirs = {}, {}, {}, {}, {}

    def add(self, key, g):
        m, n = g.shape
        if key[0] in BIG_IN:
            self.views[key] = g.reshape(1, 2, m // 2, n)
        else:
            self.views[key] = g.reshape(N_CHIPS, 2, m // (2 * N_CHIPS), n)

    def swap(self, keys):
        views = [self.views[k] for k in keys]

        def plan(ins, outs, place):
            x, y, c = place
            return [(ins[a].at[:, 1 - c], outs[a], (x, y, 1 - c), outs[a]) for a in range(len(keys))]

        def finish(outs):
            for k, v, o in zip(keys, views, outs):
                self.partial[k] = _add_half(v, o, self.core, f"chip_partial_{k[0]}_{k[1]}")

        shapes = [jax.ShapeDtypeStruct((v.shape[0],) + v.shape[2:], v.dtype) for v in views]
        return _Exchange(views, shapes, plan, len(keys), None, finish)

    def scatter(self, keys, part=(0, 1)):
        i, n = part
        n_keys = len(keys)
        parts = [self.partial[k] for k in keys]
        shapes = []
        for k, p in zip(keys, parts):
            q, r, c = p.shape
            shapes.append(jax.ShapeDtypeStruct((N_CHIPS, r, c // N_CHIPS if k[0] in BIG_IN else c), p.dtype))

        def piece(ref, key, q, rows, cols):
            return ref.at[0, rows, pl.ds(q * cols, cols)] if key[0] in BIG_IN else ref.at[q, rows, :]

        def plan(ins, outs, place):
            x, y, c = place
            me = 2 * x + y
            remote = []
            for a, k in enumerate(keys):
                _, r, cols = shapes[a].shape
                rows = pl.ds(i * (r // n), r // n)
                for (px, py) in _other_chips(x, y):
                    q = 2 * px + py
                    remote.append((piece(ins[a], k, q, rows, cols), outs[a].at[me, rows, :], (px, py, c), outs[a].at[q, rows, :]))
            return remote

        def finish(outs):
            for k, p, o in zip(keys, parts, outs):
                self.staged[k] = o
                if i == n - 1:
                    self.mine[k] = _reduce_piece(p, o, self.me, k[0] in BIG_IN, f"reduce_{k[0]}_{k[1]}")

        inputs, aliases = parts, None
        if i > 0:
            inputs = parts + [self.staged[k] for k in keys]
            aliases = {n_keys + a: a for a in range(n_keys)}
        return _Exchange(inputs, shapes, plan, 3 * n_keys, aliases, finish)

    def scatter_behind(self, keys, work, tag):
        n = len(keys)
        parts = [self.partial[k] for k in keys]
        lands = []
        for k, p in zip(keys, parts):
            _, r, c = p.shape
            lands.append(jax.ShapeDtypeStruct((N_CHIPS, r, c // N_CHIPS if k[0] in BIG_IN else c), p.dtype))
        sem_spec = pl.BlockSpec(memory_space=pltpu.SEMAPHORE)
        effect = pltpu.CompilerParams(has_side_effects=pltpu.SideEffectType.DATAFLOW_SIDE_EFFECTING)

        def copies(part_refs, land_refs, send_sems, recv_sems):
            x, y, c = _my_place()
            me = 2 * x + y
            out = []
            for a, k in enumerate(keys):
                cols = lands[a].shape[2]
                for j, (px, py) in enumerate(_other_chips(x, y)):
                    q = 2 * px + py
                    src = part_refs[a].at[0, :, pl.ds(q * cols, cols)] if k[0] in BIG_IN else part_refs[a].at[q]
                    sems = dict(send_sem=send_sems.at[3 * a + j], recv_sem=recv_sems.at[3 * a + j], device_id=(px, py, c), device_id_type=MESH)
                    out.append((pltpu.make_async_remote_copy(src_ref=src, dst_ref=land_refs[a].at[me], **sems),
                                pltpu.make_async_remote_copy(src_ref=src, dst_ref=land_refs[a].at[q], **sems)))
            return out

        def start_body(*refs):
            part_refs, land_refs, send_sems, recv_sems, token = refs[:n], refs[n:2 * n], refs[2 * n], refs[2 * n + 1], refs[-1]
            for send, _ in copies(part_refs, land_refs, send_sems, recv_sems):
                send.start()
            token[...] = jnp.zeros_like(token)

        def wait_body(*refs):
            part_refs, land_refs, send_sems, recv_sems = refs[:n], refs[n:2 * n], refs[2 * n], refs[2 * n + 1]
            for send, arrive in copies(part_refs, land_refs, send_sems, recv_sems):
                send.wait_send()
                arrive.wait_recv()

        in_hbm = [pltpu.with_memory_space_constraint(p, pltpu.HBM) for p in parts]
        in_hbm += [pltpu.with_memory_space_constraint(lax.empty(s.shape, s.dtype), pltpu.HBM) for s in lands]
        thru_shapes = [pltpu.HBM(p.shape, p.dtype) for p in parts] + [pltpu.HBM(s.shape, s.dtype) for s in lands]
        started = pl.pallas_call(
            start_body, name=f"scatter_{tag}_start", in_specs=[HBM_SPEC] * (2 * n),
            out_shape=(pltpu.SemaphoreType.DMA((3 * n,)), pltpu.SemaphoreType.DMA((3 * n,)), *thru_shapes, jax.ShapeDtypeStruct((8, 128), F32)),
            out_specs=(sem_spec, sem_spec, *[HBM_SPEC] * (2 * n), pl.BlockSpec(memory_space=pltpu.VMEM)),
            input_output_aliases={i: 2 + i for i in range(2 * n)}, compiler_params=effect)(*in_hbm)
        send_sems, recv_sems, thru, token = started[0], started[1], started[2:2 + 2 * n], started[-1]
        after = work(token)
        done = pl.pallas_call(
            wait_body, name=f"scatter_{tag}_wait", in_specs=[HBM_SPEC] * (2 * n) + [sem_spec, sem_spec] + [pl.BlockSpec(memory_space=pl.ANY)] * len(after),
            out_shape=tuple(thru_shapes), out_specs=tuple([HBM_SPEC] * (2 * n)), input_output_aliases={i: i for i in range(2 * n)},
            compiler_params=effect)(*thru, send_sems, recv_sems, *after)
        for a, k in enumerate(keys):
            self.staged[k] = done[n + a]
            self.mine[k] = _reduce_piece(done[a], done[n + a], self.me, k[0] in BIG_IN, f"reduce_{k[0]}_{k[1]}")

    def exchange(self, keys):
        mine = [self.mine[k] for k in keys]

        def plan(ins, outs, place):
            x, y, c = place
            return [(ins[a], outs[a], (x, y, 1 - c), outs[a]) for a in range(len(keys))]

        def finish(outs):
            for k, o in zip(keys, outs):
                self.theirs[k] = o

        return _Exchange(mine, [_sds(a) for a in mine], plan, len(keys), None, finish)


def _all_gather_full(gat, keys):
    n = len(keys)
    arrays = [gat.slots[k] for k in keys]
    per = 7

    def body(*refs):
        ins, outs = refs[:n], refs[n:2 * n]
        send_sems, recv_sems = refs[2 * n:]
        x, y, c = _my_place()
        sibling, x_nbr, y_nbr = (x, y, 1 - c), (1 - x, y, c), (x, 1 - y, c)
        me, qx, qy, qd = 2 * x + y, 2 * (1 - x) + y, 2 * x + (1 - y), 2 * (1 - x) + (1 - y)

        def half(ref, q, core):
            rows = ref.shape[1] // 2
            return ref.at[q, pl.ds(core * rows, rows), :]

        def quarter(ref, q, core, k):
            rows = ref.shape[1] // 4
            return ref.at[q, pl.ds((2 * core + k) * rows, rows), :]

        def copy(a, k, src, dst, to):
            return pltpu.make_async_remote_copy(src_ref=src, dst_ref=dst, send_sem=send_sems.at[per * a + k],
                                                recv_sem=recv_sems.at[per * a + k], device_id=to, device_id_type=MESH)

        sent = []

        def send(a, k, part, to):
            cp = copy(a, k, part, part, to)
            cp.start()
            sent.append(cp)

        def landed(a, k, part):
            copy(a, k, part, part, sibling).wait_recv()

        for a in range(n):
            mine_in, mine_out = half(ins[a], me, c), half(outs[a], me, c)
            for k, to in ((0, x_nbr), (1, y_nbr)):
                cp = copy(a, k, mine_in, mine_out, to)
                cp.start()
                sent.append(cp)
        for a in range(n):
            landed(a, 1, half(outs[a], qy, c))
            send(a, 2, quarter(outs[a], qy, c, 0), x_nbr)
            send(a, 5, half(outs[a], qy, c), sibling)
            landed(a, 0, half(outs[a], qx, c))
            send(a, 3, quarter(outs[a], qx, c, 1), y_nbr)
            send(a, 4, half(outs[a], qx, c), sibling)
        for a in range(n):
            landed(a, 2, quarter(outs[a], qd, c, 0))
            landed(a, 3, quarter(outs[a], qd, c, 1))
            send(a, 6, half(outs[a], qd, c), sibling)
        for a in range(n):
            for k, q in ((4, qx), (5, qy), (6, qd)):
                landed(a, k, half(outs[a], q, 1 - c))
        for cp in sent:
            cp.wait_send()

    outs = pl.pallas_call(
        body, name="all_gather_first_weights", in_specs=[HBM_SPEC] * n, out_specs=[HBM_SPEC] * n,
        out_shape=[_sds(a) for a in arrays], input_output_aliases={a: a for a in range(n)},
        scratch_shapes=[pltpu.SemaphoreType.DMA((per * n,)), pltpu.SemaphoreType.DMA((per * n,))])(*arrays)
    for k, o in zip(keys, outs):
        gat.slots[k] = o


def _small_all_gather(buf, done):
    state = {}

    def index(x, y, c):
        return 4 * x + 2 * y + c

    def plan_ici(ins, outs, place):
        x, y, c = place
        return [(ins[0], outs[0].at[index(x, y, c)], (px, py, c), outs[0].at[index(px, py, c)]) for (px, py) in _other_chips(x, y)]

    def local(ins, outs, place):
        return [(ins[0], outs[0].at[index(*place)])]

    def plan_d2d(ins, outs, place):
        x, y, c = place
        return [(ins[0].at[index(px, py, c)], outs[0].at[index(px, py, c)], (x, y, 1 - c), outs[0].at[index(px, py, 1 - c)])
                for (px, py) in [(x, y)] + _other_chips(x, y)]

    def second():
        return _Exchange([state["blocks"]], [_sds(state["blocks"])], plan_d2d, N_CHIPS, {0: 0}, lambda outs: done(outs[0]))

    first = _Exchange([buf], [jax.ShapeDtypeStruct((2 * N_CHIPS,) + buf.shape, buf.dtype)], plan_ici, 3, None,
                      lambda outs: state.update(blocks=outs[0]), local, 1)
    return first, second


WEIGHT_NAMES = ("ffn1_norm", "ffn1_w_in", "ffn1_w_out", "mix_norm", "ffn2_norm", "ffn2_w_in", "ffn2_w_out", "ab_w_in", "pool_w", "pool_b",
                "pool_scale", "conv_w", "conv_b", "conv_ln_g", "conv_ln_b", "ab_w_out", "sgu_w_in", "sgu_ln_g", "sgu_ln_b", "sgu_w", "sgu_b",
                "sgu_w_out", "final_norm")
SMALL = tuple(n for n in WEIGHT_NAMES if n not in BIG)
SHARDED_SMALL = ("conv_w", "sgu_ln_g", "sgu_ln_b")
PACK_ROWS = 64
PACK = ("pack", 0)


def _pair(prefix, layer):
    return [(prefix + "_w_in", layer), (prefix + "_w_out", layer)]


def kernel(x, ffn1_norm, ffn1_w_in, ffn1_w_out, mix_norm, ffn2_norm, ffn2_w_in, ffn2_w_out, ab_w_in, pool_w, pool_b, pool_scale, conv_w, conv_b, conv_ln_g, conv_ln_b, ab_w_out, sgu_w_in, sgu_ln_g, sgu_ln_b, sgu_w, sgu_b, sgu_w_out, final_norm, loss_target, m_ffn1_norm, m_ffn1_w_in, m_ffn1_w_out, m_mix_norm, m_ffn2_norm, m_ffn2_w_in, m_ffn2_w_out, m_ab_w_in, m_pool_w, m_pool_b, m_pool_scale, m_conv_w, m_conv_b, m_conv_ln_g, m_conv_ln_b, m_ab_w_out, m_sgu_w_in, m_sgu_ln_g, m_sgu_ln_b, m_sgu_w, m_sgu_b, m_sgu_w_out, m_final_norm, v_ffn1_norm, v_ffn1_w_in, v_ffn1_w_out, v_mix_norm, v_ffn2_norm, v_ffn2_w_in, v_ffn2_w_out, v_ab_w_in, v_pool_w, v_pool_b, v_pool_scale, v_conv_w, v_conv_b, v_conv_ln_g, v_conv_ln_b, v_ab_w_out, v_sgu_w_in, v_sgu_ln_g, v_sgu_ln_b, v_sgu_w, v_sgu_b, v_sgu_w_out, v_final_norm):
    given = dict(locals())
    w = {n: given[n] for n in WEIGHT_NAMES}
    chip = 2 * lax.axis_index("x") + lax.axis_index("y")
    me = chip.astype(jnp.int32).reshape(1)
    core = lax.axis_index("c").astype(jnp.int32).reshape(1)
    row = lambda v: v.reshape(1, -1)
    xin, tgt = x[0], loss_target[0]

    pack = jnp.concatenate([
        w["conv_w"][0], jnp.zeros((1, 128), F32), w["sgu_ln_g"].reshape(2, 128), w["sgu_ln_b"].reshape(2, 128),
        jnp.zeros((PACK_ROWS - 36, 128), F32)], axis=0)
    slots = {PACK: lax.dynamic_update_slice(jnp.zeros((N_CHIPS, PACK_ROWS, 128), F32), pack[None], (me[0], 0, 0))}
    for n in BIG:
        for layer in range(w[n].shape[0]):
            slots[(n, layer)] = _cast_into_slot(w[n], layer, me, f"cast_{n}_{layer}")
    gat = _Gatherer(slots)
    _all_gather_full(gat, _pair("ffn1", 0) + [PACK])
    gp = gat.slots[PACK]
    conv_w_full = jnp.transpose(gp[:, 0:CONV_WIDTH], (1, 0, 2)).reshape(CONV_WIDTH, N_CHIPS * 128)
    sgu_ln_g_full = gp[:, 32:34].reshape(1, -1)
    sgu_ln_b_full = gp[:, 34:36].reshape(1, -1)
    gw = lambda n, layer: gat.slots[(n, layer)]

    st = [dict(), dict()]
    st[0]["xa"] = xin
    later = _pair("sgu", 0) + _pair("ffn2", 1)
    cur, st[0]["h1"], st[0]["xn1"] = _ffn_fwd(xin, row(w["ffn1_norm"][0]), gw("ffn1_w_in", 0), gw("ffn1_w_out", 0), "ffn1_fwd_0",
                                              comms=[gat.direct(_pair("ab", 0)), gat.ici(_pair("ffn2", 0))])
    st[0]["xb"] = cur
    st[0]["h0"], st[0]["xnm"] = _norm_matmul(cur, row(w["mix_norm"][0]), gw("ab_w_in", 0), "mix0_proj_in",
                                             comms=[gat.d2d(_pair("ffn2", 0)), gat.ici([("ffn1_w_out", 1)])])
    pool_args = (w["pool_w"][0], row(w["pool_b"][0]), row(w["pool_scale"][0]), conv_w_full, row(w["conv_b"][0]), row(w["conv_ln_g"][0]),
                 row(w["conv_ln_b"][0]), gw("ab_w_out", 0))
    cur, st[0]["ycat"], st[0]["yconv"] = _mix0_fwd(cur, st[0]["h0"], *pool_args, "mix0_fwd", comms=[gat.ici([("ffn1_w_in", 1)])])
    st[0]["xc"] = cur
    cur, st[0]["h2"], st[0]["xn2"] = _ffn_fwd(cur, row(w["ffn2_norm"][0]), gw("ffn2_w_in", 0), gw("ffn2_w_out", 0), "ffn2_fwd_0",
                                              comms=[gat.d2d(_pair("ffn1", 1)), gat.ici(later)])
    st[1]["xa"] = cur
    cur, st[1]["h1"], st[1]["xn1"] = _ffn_fwd(cur, row(w["ffn1_norm"][1]), gw("ffn1_w_in", 1), gw("ffn1_w_out", 1), "ffn1_fwd_1",
                                              comms=[gat.d2d(later)])
    st[1]["xb"] = cur
    st[1]["pre"], st[1]["xnm"] = _norm_matmul(cur, row(w["mix_norm"][1]), gw("sgu_w_in", 0), "sgu_proj_in")
    sgu_args = (sgu_ln_g_full, sgu_ln_b_full, w["sgu_w"][0], w["sgu_b"][0].T)
    cur, st[1]["p"] = _sgu_fwd(cur, st[1]["pre"], *sgu_args, gw("sgu_w_out", 0), "sgu_fwd")
    st[1]["xc"] = cur
    cur, st[1]["h2"], st[1]["xn2"] = _ffn_fwd(cur, row(w["ffn2_norm"][1]), gw("ffn2_w_in", 1), gw("ffn2_w_out", 1), "ffn2_fwd_1")
    dy, loss, d_final = _final_loss(cur, tgt, row(w["final_norm"]), "final_loss")

    red = _Reducer(me, core)
    small = {"final_norm": d_final.reshape(-1)}
    norm_grads = {"ffn1_norm": [None] * DEPTH, "mix_norm": [None] * DEPTH, "ffn2_norm": [None] * DEPTH}
    ga, gb, gc, gd, ge, gf = _pair("ffn2", 1), _pair("sgu", 0), _pair("ffn1", 1), _pair("ffn2", 0), _pair("ab", 0), _pair("ffn1", 0)

    def ffn_backward(prefix, layer, xs, hs, xns, dy_in, bwd_comms=(), dwin_comms=(), dwout_comms=()):
        dx, dh, act, norm_grads[prefix + "_norm"][layer] = _ffn_bwd(
            xs, dy_in, hs, row(w[prefix + "_norm"][layer]), gw(prefix + "_w_in", layer), gw(prefix + "_w_out", layer),
            f"{prefix}_bwd_{layer}", comms=bwd_comms)
        red.add((prefix + "_w_in", layer), _tn_matmul(xns, dh, 1.0, 1024, 1408, f"{prefix}_dwin_{layer}", comms=dwin_comms))
        red.add((prefix + "_w_out", layer), _tn_matmul(act, dy_in, 0.5, 1408, 1024, f"{prefix}_dwout_{layer}", comms=dwout_comms))
        return dx

    s1, s0 = st[1], st[0]
    dy = ffn_backward("ffn2", 1, s1["xc"], s1["h2"], s1["xn2"], dy)
    dy_in = dy
    dy, dpre, norm_grads["mix_norm"][1], dlg, dlb, dw, dbt = _sgu_bwd(
        s1["xb"], dy_in, s1["pre"], row(w["mix_norm"][1]), *sgu_args, gw("sgu_w_in", 0), gw("sgu_w_out", 0), "sgu_bwd", comms=[red.swap(ga)])
    red.add(("sgu_w_in", 0), _tn_matmul(s1["xnm"], dpre, 1.0, 1024, 2048, "sgu_dwin"))
    red.add(("sgu_w_out", 0), _tn_matmul(s1["p"], dy_in, 1.0, 1024, 1024, "sgu_dwout"))
    small.update(sgu_ln_g=dlg, sgu_ln_b=dlb, sgu_w=dw[None], sgu_b=dbt.T[None])
    dy = ffn_backward("ffn1", 1, s1["xa"], s1["h1"], s1["xn1"], dy, bwd_comms=[lambda: red.scatter(ga), lambda: red.swap(gb)],
                      dwin_comms=[lambda: red.scatter(gb), lambda: red.exchange(ga)])
    dy = ffn_backward("ffn2", 0, s0["xc"], s0["h2"], s0["xn2"], dy, bwd_comms=[lambda: red.swap(gc), lambda: red.exchange(gb)],
                      dwin_comms=[lambda: red.scatter(gc)])
    dy_in = dy
    dconv, dpc, dpw, rows = _mix0_bwd_a(dy_in, s0["h0"], s0["yconv"], *pool_args, "mix0_bwd_a")
    dy, dh0, norm_grads["mix_norm"][0] = _mix0_bwd_b(s0["xb"], dy_in, s0["h0"], dconv, dpc, row(w["mix_norm"][0]), conv_w_full,
                                                      gw("ab_w_in", 0), "mix0_bwd_b")
    red.add(("ab_w_in", 0), _tn_matmul(s0["xnm"], dh0, 1.0, 1024, 1536, "ab_dwin", comms=[red.swap(gd), red.exchange(gc)]))
    small.update(pool_w=dpw[None], conv_w=rows[None, 0:CONV_WIDTH], conv_b=rows[32:33], conv_ln_g=rows[33:34], conv_ln_b=rows[34:35],
                 pool_scale=rows[35:36], pool_b=rows[36:37].reshape(1, len(POOL_WINDOWS), POOL_GC))

    last = {}

    def behind_ffn2_scatter(token):
        red.add(("ab_w_out", 0), _tn_matmul(s0["ycat"], dy_in, 1.0, 1024, 1024, "ab_dwout", after=[token]))
        last["dx"], last["dh"], last["act"], norm_grads["ffn1_norm"][0] = _ffn_bwd(
            s0["xa"], dy, s0["h1"], row(w["ffn1_norm"][0]), gw("ffn1_w_in", 0), gw("ffn1_w_out", 0), "ffn1_bwd_0", after=[token])
        return [norm_grads["ffn1_norm"][0]]

    red.scatter_behind(gd, behind_ffn2_scatter, "ffn2_0")

    small_sum = {}

    def small_ready():
        for k, v in norm_grads.items():
            small[k] = jnp.concatenate(v, axis=0)
        flat = [small[n].reshape(-1, 128) for n in SMALL]
        rows = sum(f.shape[0] for f in flat)
        loss_block = jnp.pad(loss, ((0, 8 + (-rows) % 8 - 1), (0, 127)))
        buf = jnp.concatenate(flat + [loss_block], axis=0)

        def done(gathered):
            total, at = _sum_leading(gathered, "reduce_small"), 0
            for n, f in zip(SMALL, flat):
                small_sum[n] = total[at:at + f.shape[0]].reshape(small[n].shape)
                at += f.shape[0]
            small_sum["loss"] = total[at:at + 1, 0:1]

        return _small_all_gather(buf, done)

    small_first, small_second = small_ready()
    red.add(("ffn1_w_in", 0), _tn_matmul(s0["xn1"], last["dh"], 1.0, 1024, 1408, "ffn1_dwin_0", comms=[red.swap(ge), small_first]))
    red.add(("ffn1_w_out", 0), _tn_matmul(last["act"], dy, 0.5, 1408, 1024, "ffn1_dwout_0",
                                          comms=[red.scatter(ge), red.exchange(gd), small_second]))
    grad_x = last["dx"]

    big_out = {}

    def adamw_big(n, layer, after=()):
        big_out[n] = _adamw_sharded(w[n], red.mine[(n, layer)], red.theirs[(n, layer)], given["m_" + n], given["v_" + n], core, layer,
                                    big_out.get(n), f"adamw_{n}_{layer}", after=after)

    _exchange_alone("swap_last_grads", [red.swap(gf), red.exchange(ge)])

    def other_updates(token):
        for n in BIG:
            for layer in reversed(range(w[n].shape[0])):
                if (n, layer) not in gf:
                    adamw_big(n, layer, after=[token])
        return [big_out[n][0] for n in BIG]

    red.scatter_behind(gf, other_updates, "last")
    _exchange_alone("exchange_last_grads", [red.exchange(gf)])
    for key in gf:
        adamw_big(*key)

    loss = small_sum["loss"][0, 0]
    grads, delta, new_m, new_v = {}, {}, {}, {}
    for n in WEIGHT_NAMES:
        mom, var = given["m_" + n], given["v_" + n]
        if n in BIG:
            grads[n], delta[n], new_m[n], new_v[n] = big_out[n]
            continue
        g = small_sum[n]
        if n in SHARDED_SMALL:
            width = w[n].shape[-1]
            g = lax.dynamic_slice_in_dim(g, chip * width, width, axis=g.ndim - 1)
        grads[n] = g
        delta[n], new_m[n], new_v[n] = _adamw(w[n], g, mom, var, f"adamw_{n}")
    return (loss, grad_x[None], *[grads[n] for n in WEIGHT_NAMES], *[delta[n] for n in WEIGHT_NAMES],
            *[new_m[n] for n in WEIGHT_NAMES], *[new_v[n] for n in WEIGHT_NAMES])
```

```python
import jax
import jax.numpy as jnp
from jax import lax
from jax.experimental import pallas as pl
from jax.experimental.pallas import tpu as pltpu

F32, BF16 = jnp.float32, jnp.bfloat16
EPS = 1e-6
N_CHIPS = 4
POOL_WINDOWS = (2, 4, 8, 16)
POOL_GC = 128
POOL_CH = 512
CONV_CH = 512
CONV_WIDTH = 31
HALO = 32
SGU_HEADS = 8
CHUNK = 128
DEPTH = 2
ADAM_LR, ADAM_B1, ADAM_B2, ADAM_EPS, ADAM_WD, ADAM_STEP = 0.001, 0.9, 0.999, 1e-08, 0.01, 10
VMEM_LIMIT_BYTES = 60 * 1024 * 1024
MESH_AXES = ("x", "y", "c")
MESH = pl.DeviceIdType.MESH
HBM_SPEC = pl.BlockSpec(memory_space=pltpu.HBM)


def _sds(a):
    return jax.ShapeDtypeStruct(a.shape, a.dtype)


def _cparams_nd(n):
    return pltpu.CompilerParams(dimension_semantics=("arbitrary",) * n, vmem_limit_bytes=VMEM_LIMIT_BYTES)


def _cparams():
    return _cparams_nd(1)


def _dot(a, b):
    return jnp.dot(a, b, preferred_element_type=F32)


def _dot_nt(a, b):
    return lax.dot_general(a, b, (((1,), (1,)), ((), ())), preferred_element_type=F32)


def _dot_tn(a, b):
    return lax.dot_general(a, b, (((0,), (0,)), ((), ())), preferred_element_type=F32)


def _rms_fwd(x):
    r = lax.rsqrt(jnp.mean(x * x, axis=-1, keepdims=True) + EPS)
    return x * r, r


def _rms_bwd(dxn, xh, r, g):
    dxh = dxn * g
    return r * (dxh - xh * jnp.mean(dxh * xh, axis=-1, keepdims=True))


def _ln_fwd(y):
    mu = jnp.mean(y, axis=-1, keepdims=True)
    yc = y - mu
    rs = lax.rsqrt(jnp.mean(yc * yc, axis=-1, keepdims=True) + EPS)
    return yc * rs, rs


def _ln_bwd(dyhat, yhat, rs):
    return rs * (dyhat - jnp.mean(dyhat, axis=-1, keepdims=True) - yhat * jnp.mean(dyhat * yhat, axis=-1, keepdims=True))


def _sigmoid(x):
    return 0.5 * jnp.tanh(0.5 * x) + 0.5


def _const_spec(shape):
    n = len(shape)
    return pl.BlockSpec(shape, lambda i: (0,) * n)


def _row_spec(tm, cols):
    return pl.BlockSpec((tm, cols), lambda i: (i, 0))


def _my_place():
    return lax.axis_index("x"), lax.axis_index("y"), lax.axis_index("c")


def _other_chips(x, y):
    return [(1 - x, y), (x, 1 - y), (1 - x, 1 - y)]


class _Exchange:
    def __init__(self, inputs, out_shapes, plan, count, aliases=None, finish=None, local=None, n_local=0):
        self.inputs, self.out_shapes, self.plan, self.count = list(inputs), list(out_shapes), plan, count
        self.aliases, self.finish, self.local, self.n_local = dict(aliases or {}), finish, local, n_local


def _call(body, *, name, grid, in_specs, out_specs, out_shape, args, scratch_shapes=(), comms=(), scalar=None, aliases=None, after=()):
    comms = [cm if isinstance(cm, _Exchange) else cm() for cm in comms]
    in_specs, out_specs, out_shape, scratch_shapes = list(in_specs), list(out_specs), list(out_shape), list(scratch_shapes)
    n_body_in = len(in_specs)
    in_specs += [pl.BlockSpec(memory_space=pl.ANY)] * len(after)
    args = list(args) + list(after)
    n_in, n_out, n_scr = len(in_specs), len(out_specs), len(scratch_shapes)
    c_in = [a for cm in comms for a in cm.inputs]
    c_out = [s for cm in comms for s in cm.out_shapes]
    n_remote = sum(cm.count for cm in comms)
    n_local = sum(cm.n_local for cm in comms)
    n_scalar = 0 if scalar is None else 1
    all_aliases = {n_scalar + i: o for i, o in (aliases or {}).items()}
    at_in, at_out = n_scalar + n_in, n_out
    for cm in comms:
        for i, o in cm.aliases.items():
            all_aliases[at_in + i] = at_out + o
        at_in += len(cm.inputs)
        at_out += len(cm.out_shapes)

    def wrapped(*all_refs):
        scalar_ref, refs = all_refs[:n_scalar], all_refs[n_scalar:]
        ins, ci = refs[:n_in], refs[n_in:n_in + len(c_in)]
        at = n_in + len(c_in)
        outs, co = refs[at:at + n_out], refs[at + n_out:at + n_out + len(c_out)]
        at += n_out + len(c_out)
        scr = refs[at:at + n_scr]

        def run_body():
            body(*scalar_ref, *ins[:n_body_in], *outs, *scr)

        if not comms:
            run_body()
            return
        send_sems, recv_sems, local_sems = refs[at + n_scr:]
        place = _my_place()
        sends, arrivals, locals_ = [], [], []
        i0 = o0 = 0
        for cm in comms:
            cm_in, cm_out = ci[i0:i0 + len(cm.inputs)], co[o0:o0 + len(cm.out_shapes)]
            i0 += len(cm.inputs)
            o0 += len(cm.out_shapes)
            for src, dst, dev, incoming in cm.plan(cm_in, cm_out, place):
                k = len(sends)
                sends.append(pltpu.make_async_remote_copy(src_ref=src, dst_ref=dst, send_sem=send_sems.at[k], recv_sem=recv_sems.at[k],
                                                          device_id=dev, device_id_type=MESH))
                arrivals.append(pltpu.make_async_remote_copy(src_ref=src, dst_ref=incoming, send_sem=send_sems.at[k],
                                                             recv_sem=recv_sems.at[k], device_id=dev, device_id_type=MESH))
            if cm.local is not None:
                for src, dst in cm.local(cm_in, cm_out, place):
                    locals_.append(pltpu.make_async_copy(src, dst, local_sems.at[len(locals_)]))

        def start():
            for cp in locals_ + sends:
                cp.start()

        def finish():
            for cp in arrivals:
                cp.wait_recv()
            for cp in sends:
                cp.wait_send()
            for cp in locals_:
                cp.wait()

        if not grid:
            start()
            run_body()
            finish()
            return
        ids = [pl.program_id(a) for a in range(len(grid))]
        first, last = ids[0] == 0, ids[0] == grid[0] - 1
        for a in range(1, len(grid)):
            first = jnp.logical_and(first, ids[a] == 0)
            last = jnp.logical_and(last, ids[a] == grid[a] - 1)
        pl.when(first)(start)
        run_body()
        pl.when(last)(finish)

    sems = []
    if comms:
        sems = [pltpu.SemaphoreType.DMA((max(n_remote, 1),)), pltpu.SemaphoreType.DMA((max(n_remote, 1),)),
                pltpu.SemaphoreType.DMA((max(n_local, 1),))]
    all_in, all_out = in_specs + [HBM_SPEC] * len(c_in), out_specs + [HBM_SPEC] * len(c_out)
    if scalar is None:
        kwargs = dict(grid=grid, compiler_params=_cparams_nd(len(grid))) if grid else {}
        res = pl.pallas_call(
            wrapped, name=name, in_specs=all_in, out_specs=all_out, out_shape=out_shape + c_out, scratch_shapes=scratch_shapes + sems,
            input_output_aliases=all_aliases, **kwargs)(*args, *c_in)
    else:
        spec = pltpu.PrefetchScalarGridSpec(num_scalar_prefetch=1, grid=grid, in_specs=all_in, out_specs=all_out,
                                            scratch_shapes=scratch_shapes + sems)
        res = pl.pallas_call(
            wrapped, name=name, grid_spec=spec, out_shape=out_shape + c_out, input_output_aliases=all_aliases,
            compiler_params=_cparams_nd(len(grid)))(scalar, *args, *c_in)
    at = n_out
    for cm in comms:
        got = res[at:at + len(cm.out_shapes)]
        at += len(cm.out_shapes)
        if cm.finish is not None:
            cm.finish(got)
    return list(res[:n_out])


def _exchange_alone(name, comms):
    _call(lambda: None, name=name, grid=(), in_specs=[], out_specs=[], out_shape=[], args=[], comms=comms)


def _in_weight_copies(w_hbm, w_v, sem, base=0):
    return [pltpu.make_async_copy(w_hbm.at[q], w_v.at[q], sem.at[base + q]) for q in range(N_CHIPS)]


def _out_weight_copies(w_hbm, w_v, sem, base=0):
    rows = w_hbm.shape[1]
    return [pltpu.make_async_copy(w_hbm.at[q], w_v.at[pl.ds(q * rows, rows)], sem.at[base + q]) for q in range(N_CHIPS)]


def _load_at_first_step(copies):
    @pl.when(pl.program_id(0) == 0)
    def _():
        for cp in copies:
            cp.start()
        for cp in copies:
            cp.wait()


def _loss_head(xo, tgt, gv, loss_ref, dg_ref):
    d = xo.shape[1]
    xh, r = _rms_fwd(xo)
    diff = xh * gv - tgt
    loss_ref[...] += 0.5 * jnp.sum(jnp.sum(diff * diff, axis=1, keepdims=True), axis=0, keepdims=True) / d
    dout = diff / d
    dg_ref[...] += jnp.sum(dout * xh, axis=0, keepdims=True)
    return _rms_bwd(dout, xh, r, gv)


def _ffn_fwd(x, g, win_g, wout_g, name, comms=(), loss_head=None):
    t, d = x.shape
    c = win_g.shape[-1]
    ff = 2 * c
    tm = min(512, t)
    n_head = 0 if loss_head is None else 2

    def body(x_ref, g_ref, win_hbm, wout_hbm, *rest):
        head_in, (xo_ref, h_ref, xn_ref), rest = rest[:n_head], rest[n_head:n_head + 3], rest[n_head + 3:]
        head_out, (win_v, wout_v, sem) = rest[:n_head], rest[n_head:]
        _load_at_first_step(_in_weight_copies(win_hbm, win_v, sem) + _out_weight_copies(wout_hbm, wout_v, sem, N_CHIPS))
        xv = x_ref[...]
        xh, _ = _rms_fwd(xv)
        xn = (xh * g_ref[...]).astype(BF16)
        xn_ref[...] = xn
        acc = jnp.zeros((tm, d), F32)
        for j in range(2):
            gate = _dot(xn, win_v[j])
            up = _dot(xn, win_v[j + 2])
            h_ref[:, j * c:(j + 1) * c] = gate.astype(BF16)
            h_ref[:, ff + j * c:ff + (j + 1) * c] = up.astype(BF16)
            act = (gate * _sigmoid(gate) * up).astype(BF16)
            acc = acc + _dot(act, wout_v[j * c:(j + 1) * c, :])
        xo = xv + 0.5 * acc
        if loss_head is None:
            xo_ref[...] = xo
            return

        @pl.when(pl.program_id(0) == 0)
        def _():
            for ref in head_out:
                ref[...] = jnp.zeros_like(ref)

        xo_ref[...] = _loss_head(xo, head_in[0][...], head_in[1][...], *head_out)

    head_specs = [] if loss_head is None else [_row_spec(tm, d), _const_spec((1, d))]
    head_out_specs = [] if loss_head is None else [_const_spec((1, 1)), _const_spec((1, d))]
    head_out_shape = [] if loss_head is None else [jax.ShapeDtypeStruct((1, 1), F32), jax.ShapeDtypeStruct((1, d), F32)]
    return _call(
        body, name=name, grid=(t // tm,),
        in_specs=[_row_spec(tm, d), _const_spec((1, d)), HBM_SPEC, HBM_SPEC] + head_specs,
        out_specs=[_row_spec(tm, d), _row_spec(tm, 2 * ff), _row_spec(tm, d)] + head_out_specs,
        out_shape=[jax.ShapeDtypeStruct((t, d), F32), jax.ShapeDtypeStruct((t, 2 * ff), BF16), jax.ShapeDtypeStruct((t, d), BF16)]
        + head_out_shape,
        scratch_shapes=[pltpu.VMEM((N_CHIPS, d, c), BF16), pltpu.VMEM((ff, d), BF16), pltpu.SemaphoreType.DMA((2 * N_CHIPS,))],
        args=(x, g, win_g, wout_g) + tuple(loss_head or ()), comms=comms)


def _ffn_bwd(x, dy, h, g, win_g, wout_g, name, comms=(), after=()):
    t, d = x.shape
    c = win_g.shape[-1]
    ff = 2 * c
    tm = min(256, t)

    def body(x_ref, dy_ref, h_ref, g_ref, win_hbm, wout_hbm, dx_ref, dh_ref, act_ref, dg_ref, win_v, wout_v, sem):
        _load_at_first_step(_in_weight_copies(win_hbm, win_v, sem) + _out_weight_copies(wout_hbm, wout_v, sem, N_CHIPS))

        @pl.when(pl.program_id(0) == 0)
        def _():
            dg_ref[...] = jnp.zeros_like(dg_ref)

        xv, dyv, gv = x_ref[...], dy_ref[...], g_ref[...]
        xh, r = _rms_fwd(xv)
        dyh = (0.5 * dyv).astype(BF16)
        dxn = jnp.zeros((tm, d), F32)
        for j in range(2):
            gate = h_ref[:, j * c:(j + 1) * c].astype(F32)
            up = h_ref[:, ff + j * c:ff + (j + 1) * c].astype(F32)
            dact = _dot_nt(dyh, wout_v[j * c:(j + 1) * c, :])
            s = _sigmoid(gate)
            sl = gate * s
            act_ref[:, j * c:(j + 1) * c] = (sl * up).astype(BF16)
            dgate = (dact * up * (s + sl * (1.0 - s))).astype(BF16)
            dup = (dact * sl).astype(BF16)
            dh_ref[:, j * c:(j + 1) * c] = dgate
            dh_ref[:, ff + j * c:ff + (j + 1) * c] = dup
            dxn = dxn + _dot_nt(dgate, win_v[j]) + _dot_nt(dup, win_v[j + 2])
        dg_ref[...] += jnp.sum(dxn * xh, axis=0, keepdims=True)
        dx_ref[...] = dyv + _rms_bwd(dxn, xh, r, gv)

    return _call(
        body, name=name, grid=(t // tm,),
        in_specs=[_row_spec(tm, d), _row_spec(tm, d), _row_spec(tm, 2 * ff), _const_spec((1, d)), HBM_SPEC, HBM_SPEC],
        out_specs=[_row_spec(tm, d), _row_spec(tm, 2 * ff), _row_spec(tm, ff), _const_spec((1, d))],
        out_shape=[jax.ShapeDtypeStruct((t, d), F32), jax.ShapeDtypeStruct((t, 2 * ff), BF16), jax.ShapeDtypeStruct((t, ff), BF16),
                   jax.ShapeDtypeStruct((1, d), F32)],
        scratch_shapes=[pltpu.VMEM((N_CHIPS, d, c), BF16), pltpu.VMEM((ff, d), BF16), pltpu.SemaphoreType.DMA((2 * N_CHIPS,))],
        args=(x, dy, h, g, win_g, wout_g), comms=comms, after=after)


def _norm_matmul(x, g, win_g, name, comms=()):
    t, d = x.shape
    c = win_g.shape[-1]
    tm = min(512, t)

    def body(x_ref, g_ref, win_hbm, o_ref, xn_ref, win_v, sem):
        _load_at_first_step(_in_weight_copies(win_hbm, win_v, sem))
        xh, _ = _rms_fwd(x_ref[...])
        xn = (xh * g_ref[...]).astype(BF16)
        xn_ref[...] = xn
        for q in range(N_CHIPS):
            o_ref[:, q * c:(q + 1) * c] = _dot(xn, win_v[q])

    return _call(
        body, name=name, grid=(t // tm,),
        in_specs=[_row_spec(tm, d), _const_spec((1, d)), HBM_SPEC],
        out_specs=[_row_spec(tm, N_CHIPS * c), _row_spec(tm, d)],
        out_shape=[jax.ShapeDtypeStruct((t, N_CHIPS * c), F32), jax.ShapeDtypeStruct((t, d), BF16)],
        scratch_shapes=[pltpu.VMEM((N_CHIPS, d, c), BF16), pltpu.SemaphoreType.DMA((N_CHIPS,))],
        args=(x, g, win_g), comms=comms)


def _proj_in_bwd_tail(dh, win_v, c):
    dxn = _dot_nt(dh[:, 0:c], win_v[0])
    for q in range(1, N_CHIPS):
        dxn = dxn + _dot_nt(dh[:, q * c:(q + 1) * c], win_v[q])
    return dxn


def _prev_halo_spec(tm, cols):
    return pl.BlockSpec((HALO, cols), lambda i: (jnp.maximum(i * (tm // HALO) - 1, 0), 0))


def _next_halo_spec(tm, cols, t):
    last = t // HALO - 1
    return pl.BlockSpec((HALO, cols), lambda i: (jnp.minimum((i + 1) * (tm // HALO), last), 0))


def _shift_down(w, k):
    return w if k == 0 else pltpu.roll(w, k, 0)


def _shift_up(w, k):
    return w if k == 0 else pltpu.roll(w, w.shape[0] - k, 0)


def _pool_counts(i, tm):
    pos = (i * tm + lax.broadcasted_iota(jnp.int32, (tm, POOL_CH), 0) + 1).astype(F32)
    lane = lax.broadcasted_iota(jnp.int32, (tm, POOL_CH), 1)
    win = jnp.where(lane < POOL_GC, 2.0, jnp.where(lane < 2 * POOL_GC, 4.0, jnp.where(lane < 3 * POOL_GC, 8.0, 16.0)))
    return jnp.minimum(pos, win)


def _group_select(parts):
    return jnp.concatenate([p[:, k * POOL_GC:(k + 1) * POOL_GC] for k, p in enumerate(parts)], axis=1)


TAP_ROWS = 128


def _tap_blocks(tm, block_fn):
    cols = []
    for ch in range(CONV_CH // POOL_GC):
        lanes = slice(ch * POOL_GC, (ch + 1) * POOL_GC)
        cols.append(jnp.concatenate([block_fn(r, lanes) for r in range(tm // TAP_ROWS)], axis=0))
    return jnp.concatenate(cols, axis=1)


def _conv_block(win, taps):
    acc = jnp.zeros((TAP_ROWS, win.shape[1]), F32)
    for k in range(CONV_WIDTH):
        acc = acc + taps[k:k + 1, :] * _shift_down(win, CONV_WIDTH - 1 - k)[HALO:, :]
    return acc


def _conv_block_transposed(win, taps):
    acc = jnp.zeros((TAP_ROWS, win.shape[1]), F32)
    for k in range(CONV_WIDTH):
        acc = acc + taps[k:k + 1, :] * _shift_up(win, CONV_WIDTH - 1 - k)[0:TAP_ROWS, :]
    return acc


def _mix0_recompute(i, tm, h_cur, h_prev, conv_w, conv_b, y=None):
    prev = jnp.where(i > 0, h_prev, 0.0)
    win = jnp.concatenate([prev, h_cur], axis=0)
    u_w = win[:, 0:POOL_CH]
    a_w = win[:, POOL_CH:POOL_CH + CONV_CH]
    gt_w = win[:, POOL_CH + CONV_CH:]
    g_w = a_w * _sigmoid(gt_w)
    if y is None:
        y = _tap_blocks(tm, lambda r, lanes: _conv_block(g_w[r * TAP_ROWS:(r + 1) * TAP_ROWS + HALO, lanes], conv_w[:, lanes])) + conv_b
    s2 = u_w + _shift_down(u_w, 1)
    s4 = s2 + _shift_down(s2, 2)
    s8 = s4 + _shift_down(s4, 4)
    s16 = s8 + _shift_down(s8, 8)
    sums = _group_select([s2[HALO:], s4[HALO:], s8[HALO:], s16[HALO:]])
    cnt = _pool_counts(i, tm)
    pooled = sums / cnt - h_cur[:, 0:POOL_CH]
    return g_w, y, pooled, cnt


def _pool_linear(pooled, pw_ref, pb):
    return jnp.concatenate(
        [_dot(pooled[:, k * POOL_GC:(k + 1) * POOL_GC].astype(BF16), pw_ref[k].astype(BF16)) for k in range(len(POOL_WINDOWS))], axis=1) + pb


def _mix0_fwd(x, h0, pool_w, pool_b, pool_scale, conv_w, conv_b, ln_g, ln_b, wout_g, name, comms=()):
    t, d = x.shape
    tm = min(512, t)
    hc = h0.shape[1]

    def body(x_ref, h_ref, hp_ref, pw_ref, pb_ref, ps_ref, cw_ref, cb_ref, lg_ref, lb_ref, wout_hbm, xo_ref, ycat_ref, y_ref, wout_v, sem):
        _load_at_first_step(_out_weight_copies(wout_hbm, wout_v, sem))
        i = pl.program_id(0)
        _, y, pooled, _ = _mix0_recompute(i, tm, h_ref[...], hp_ref[...], cw_ref[...], cb_ref[...])
        y_ref[...] = y
        yhat, _ = _ln_fwd(y)
        yn = yhat * lg_ref[...] + lb_ref[...]
        yb = yn * _sigmoid(yn)
        ya = _pool_linear(pooled, pw_ref, pb_ref[...]) * ps_ref[...]
        ycat = jnp.concatenate([ya, yb], axis=1).astype(BF16)
        ycat_ref[...] = ycat
        xo_ref[...] = x_ref[...] + _dot(ycat, wout_v[...])

    return _call(
        body, name=name, grid=(t // tm,),
        in_specs=[_row_spec(tm, d), _row_spec(tm, hc), _prev_halo_spec(tm, hc), _const_spec(pool_w.shape), _const_spec((1, POOL_CH)),
                  _const_spec((1, POOL_CH)), _const_spec(conv_w.shape), _const_spec((1, CONV_CH)), _const_spec((1, CONV_CH)),
                  _const_spec((1, CONV_CH)), HBM_SPEC],
        out_specs=[_row_spec(tm, d), _row_spec(tm, d), _row_spec(tm, CONV_CH)],
        out_shape=[jax.ShapeDtypeStruct((t, d), F32), jax.ShapeDtypeStruct((t, d), BF16), jax.ShapeDtypeStruct((t, CONV_CH), F32)],
        scratch_shapes=[pltpu.VMEM((d, d), BF16), pltpu.SemaphoreType.DMA((N_CHIPS,))],
        args=(x, h0, h0, pool_w, pool_b, pool_scale, conv_w, conv_b, ln_g, ln_b, wout_g), comms=comms)


def _mix0_bwd_a(dy, h0, y_conv, pool_w, pool_b, pool_scale, conv_w, conv_b, ln_g, ln_b, wout_g, name):
    t, d = dy.shape
    tm = min(512, t)
    hc = h0.shape[1]
    n_small = 40

    def body(dy_ref, h_ref, hp_ref, y_ref, pw_ref, pb_ref, ps_ref, cw_ref, cb_ref, lg_ref, lb_ref, wout_hbm,
             dconv_ref, dpc_ref, dpw_ref, small_ref, wout_v, sem):
        _load_at_first_step(_out_weight_copies(wout_hbm, wout_v, sem))
        i = pl.program_id(0)

        @pl.when(i == 0)
        def _():
            dpw_ref[...] = jnp.zeros_like(dpw_ref)
            small_ref[...] = jnp.zeros_like(small_ref)

        g_w, y, pooled, cnt = _mix0_recompute(i, tm, h_ref[...], hp_ref[...], cw_ref[...], cb_ref[...], y_ref[...])
        yhat, rs = _ln_fwd(y)
        lg = lg_ref[...]
        yn = yhat * lg + lb_ref[...]
        mixed = _pool_linear(pooled, pw_ref, pb_ref[...])
        dycat = _dot_nt(dy_ref[...].astype(BF16), wout_v[...])
        dya, dyb = dycat[:, 0:POOL_CH], dycat[:, POOL_CH:]
        sg = _sigmoid(yn)
        dyn = dyb * (sg * (1.0 + yn * (1.0 - sg)))
        dyc = _ln_bwd(dyn * lg, yhat, rs)
        dconv_ref[...] = dyc

        def add_row(k, value):
            small_ref[k:k + 1, :] += jnp.sum(value, axis=0, keepdims=True)

        for ch in range(CONV_CH // POOL_GC):
            lanes = slice(ch * POOL_GC, (ch + 1) * POOL_GC)
            taps = [jnp.zeros((1, POOL_GC), F32)] * CONV_WIDTH
            for r in range(tm // TAP_ROWS):
                win = g_w[r * TAP_ROWS:(r + 1) * TAP_ROWS + HALO, lanes]
                d = dyc[r * TAP_ROWS:(r + 1) * TAP_ROWS, lanes]
                for k in range(CONV_WIDTH):
                    taps[k] = taps[k] + jnp.sum(d * _shift_down(win, CONV_WIDTH - 1 - k)[HALO:, :], axis=0, keepdims=True)
            for k in range(CONV_WIDTH):
                small_ref[k:k + 1, lanes] += taps[k]
        add_row(32, dyc)
        add_row(33, dyn * yhat)
        add_row(34, dyn)
        scale = ps_ref[...]
        dmixed = dya * scale
        add_row(35, dya * mixed)
        add_row(36, dmixed)
        dmb = dmixed.astype(BF16)
        dpooled = []
        for k in range(len(POOL_WINDOWS)):
            sl = slice(k * POOL_GC, (k + 1) * POOL_GC)
            dpw_ref[k] += _dot_tn(pooled[:, sl].astype(BF16), dmb[:, sl])
            dpooled.append(_dot_nt(dmb[:, sl], pw_ref[k].astype(BF16)))
        dpc_ref[...] = jnp.concatenate(dpooled, axis=1) / cnt

    return _call(
        body, name=name, grid=(t // tm,),
        in_specs=[_row_spec(tm, d), _row_spec(tm, hc), _prev_halo_spec(tm, hc), _row_spec(tm, CONV_CH), _const_spec(pool_w.shape),
                  _const_spec((1, POOL_CH)), _const_spec((1, POOL_CH)), _const_spec(conv_w.shape), _const_spec((1, CONV_CH)),
                  _const_spec((1, CONV_CH)), _const_spec((1, CONV_CH)), HBM_SPEC],
        out_specs=[_row_spec(tm, CONV_CH), _row_spec(tm, POOL_CH), _const_spec(pool_w.shape), _const_spec((n_small, CONV_CH))],
        out_shape=[jax.ShapeDtypeStruct((t, CONV_CH), F32), jax.ShapeDtypeStruct((t, POOL_CH), F32),
                   jax.ShapeDtypeStruct(pool_w.shape, F32), jax.ShapeDtypeStruct((n_small, CONV_CH), F32)],
        scratch_shapes=[pltpu.VMEM((d, d), BF16), pltpu.SemaphoreType.DMA((N_CHIPS,))],
        args=(dy, h0, h0, y_conv, pool_w, pool_b, pool_scale, conv_w, conv_b, ln_g, ln_b, wout_g))


def _mix0_bwd_b(x, dy, h0, dconv, dpc, g, conv_w, win_g, name):
    t, d = x.shape
    tm = min(512, t)
    hc = h0.shape[1]
    c = win_g.shape[-1]
    n_tiles = t // tm

    def body(x_ref, dy_ref, h_ref, dc_ref, dcn_ref, dp_ref, dpn_ref, g_ref, cw_ref, win_hbm, dx_ref, dh_ref, dg_ref, win_v, sem):
        _load_at_first_step(_in_weight_copies(win_hbm, win_v, sem))
        i = pl.program_id(0)

        @pl.when(i == 0)
        def _():
            dg_ref[...] = jnp.zeros_like(dg_ref)

        not_last = i < n_tiles - 1
        dc_w = jnp.concatenate([dc_ref[...], jnp.where(not_last, dcn_ref[...], 0.0)], axis=0)
        dp_w = jnp.concatenate([dp_ref[...], jnp.where(not_last, dpn_ref[...], 0.0)], axis=0)
        cw = cw_ref[...]
        dg = _tap_blocks(tm, lambda r, lanes: _conv_block_transposed(dc_w[r * TAP_ROWS:(r + 1) * TAP_ROWS + HALO, lanes], cw[:, lanes]))
        a2 = dp_w + _shift_up(dp_w, 1)
        a4 = a2 + _shift_up(a2, 2)
        a8 = a4 + _shift_up(a4, 4)
        a16 = a8 + _shift_up(a8, 8)
        back = _group_select([a2[0:tm], a4[0:tm], a8[0:tm], a16[0:tm]])
        du = back - dp_ref[...] * _pool_counts(i, tm)
        hv = h_ref[...]
        a = hv[:, POOL_CH:POOL_CH + CONV_CH]
        sig = _sigmoid(hv[:, POOL_CH + CONV_CH:])
        dh = jnp.concatenate([du, dg * sig, dg * a * sig * (1.0 - sig)], axis=1).astype(BF16)
        dh_ref[...] = dh
        dxn = _proj_in_bwd_tail(dh, win_v, c)
        xh, r = _rms_fwd(x_ref[...])
        dg_ref[...] += jnp.sum(dxn * xh, axis=0, keepdims=True)
        dx_ref[...] = dy_ref[...] + _rms_bwd(dxn, xh, r, g_ref[...])

    return _call(
        body, name=name, grid=(n_tiles,),
        in_specs=[_row_spec(tm, d), _row_spec(tm, d), _row_spec(tm, hc), _row_spec(tm, CONV_CH), _next_halo_spec(tm, CONV_CH, t),
                  _row_spec(tm, POOL_CH), _next_halo_spec(tm, POOL_CH, t), _const_spec((1, d)), _const_spec(conv_w.shape), HBM_SPEC],
        out_specs=[_row_spec(tm, d), _row_spec(tm, hc), _const_spec((1, d))],
        out_shape=[jax.ShapeDtypeStruct((t, d), F32), jax.ShapeDtypeStruct((t, hc), BF16), jax.ShapeDtypeStruct((1, d), F32)],
        scratch_shapes=[pltpu.VMEM((N_CHIPS, d, c), BF16), pltpu.SemaphoreType.DMA((N_CHIPS,))],
        args=(x, dy, h0, dconv, dconv, dpc, dpc, g, conv_w, win_g))


SQRT_HALF = 0.7071067811865476
INV_SQRT_2PI = 0.3989422804014327


def _causal_mask():
    return (lax.broadcasted_iota(jnp.int32, (CHUNK, CHUNK), 1) <= lax.broadcasted_iota(jnp.int32, (CHUNK, CHUNK), 0)).astype(F32)


def _sgu_recompute(pre, lg, lb):
    half = pre.shape[1] // 2
    phi = 0.5 * (1.0 + lax.erf(pre * SQRT_HALF))
    z = pre * phi
    u, v = z[:, 0:half], z[:, half:]
    vhat, rs = _ln_fwd(v)
    return u, vhat, rs, vhat * lg + lb, phi


def _sgu_spatial(vln, w_ref, bt, tm):
    mask = _causal_mask()
    wm = [(w_ref[hd] * mask).astype(BF16) for hd in range(SGU_HEADS)]
    vb = vln.astype(BF16)
    rows = []
    for ch in range(tm // CHUNK):
        blocks = [_dot(wm[hd], vb[ch * CHUNK:(ch + 1) * CHUNK, hd * CHUNK:(hd + 1) * CHUNK]) + bt[:, hd:hd + 1] for hd in range(SGU_HEADS)]
        rows.append(jnp.concatenate(blocks, axis=1))
    return jnp.concatenate(rows, axis=0), wm


def _sgu_fwd(x, pre, ln_g, ln_b, w, bt, wout_g, name):
    t, d = x.shape
    tm = min(512, t)
    pc = pre.shape[1]

    def body(x_ref, pre_ref, lg_ref, lb_ref, w_ref, bt_ref, wout_hbm, xo_ref, p_ref, wout_v, sem):
        _load_at_first_step(_out_weight_copies(wout_hbm, wout_v, sem))
        u, _, _, vln, _ = _sgu_recompute(pre_ref[...], lg_ref[...], lb_ref[...])
        vo, _ = _sgu_spatial(vln, w_ref, bt_ref[...], tm)
        p = (u * vo).astype(BF16)
        p_ref[...] = p
        xo_ref[...] = x_ref[...] + _dot(p, wout_v[...])

    return _call(
        body, name=name, grid=(t // tm,),
        in_specs=[_row_spec(tm, d), _row_spec(tm, pc), _const_spec((1, d)), _const_spec((1, d)), _const_spec(w.shape),
                  _const_spec(bt.shape), HBM_SPEC],
        out_specs=[_row_spec(tm, d), _row_spec(tm, d)],
        out_shape=[jax.ShapeDtypeStruct((t, d), F32), jax.ShapeDtypeStruct((t, d), BF16)],
        scratch_shapes=[pltpu.VMEM((d, d), BF16), pltpu.SemaphoreType.DMA((N_CHIPS,))],
        args=(x, pre, ln_g, ln_b, w, bt, wout_g))


def _sgu_bwd(x, dy, pre, g, ln_g, ln_b, w, bt, win_g, wout_g, name, comms=()):
    t, d = x.shape
    tm = min(512, t)
    pc = pre.shape[1]
    c = win_g.shape[-1]

    def body(x_ref, dy_ref, pre_ref, g_ref, lg_ref, lb_ref, w_ref, bt_ref, win_hbm, wout_hbm,
             dx_ref, dpre_ref, dg_ref, dlg_ref, dlb_ref, dw_ref, dbt_ref, win_v, wout_v, sem):
        _load_at_first_step(_in_weight_copies(win_hbm, win_v, sem) + _out_weight_copies(wout_hbm, wout_v, sem, N_CHIPS))
        i = pl.program_id(0)

        @pl.when(i == 0)
        def _():
            for ref in (dg_ref, dlg_ref, dlb_ref, dw_ref, dbt_ref):
                ref[...] = jnp.zeros_like(ref)

        prev = pre_ref[...]
        lg = lg_ref[...]
        u, vhat, rs, vln, phi = _sgu_recompute(prev, lg, lb_ref[...])
        vo, wm = _sgu_spatial(vln, w_ref, bt_ref[...], tm)
        dp = _dot_nt(dy_ref[...].astype(BF16), wout_v[...])
        du = dp * vo
        dvo = dp * u
        dvob = dvo.astype(BF16)
        vb = vln.astype(BF16)
        head_lane = lax.broadcasted_iota(jnp.int32, (CHUNK, SGU_HEADS), 1)
        dbt = jnp.zeros((CHUNK, SGU_HEADS), F32)
        dw = [jnp.zeros((CHUNK, CHUNK), F32) for _ in range(SGU_HEADS)]
        rows = []
        for ch in range(tm // CHUNK):
            rs_ = slice(ch * CHUNK, (ch + 1) * CHUNK)
            blocks = []
            for hd in range(SGU_HEADS):
                cs = slice(hd * CHUNK, (hd + 1) * CHUNK)
                dbt = dbt + jnp.where(head_lane == hd, jnp.sum(dvo[rs_, cs], axis=1, keepdims=True), 0.0)
                dw[hd] = dw[hd] + _dot_nt(dvob[rs_, cs], vb[rs_, cs])
                blocks.append(_dot_tn(wm[hd], dvob[rs_, cs]))
            rows.append(jnp.concatenate(blocks, axis=1))
        dvln = jnp.concatenate(rows, axis=0)
        mask = _causal_mask()
        for hd in range(SGU_HEADS):
            dw_ref[hd] += dw[hd] * mask
        dbt_ref[...] += dbt
        dlg_ref[...] += jnp.sum(dvln * vhat, axis=0, keepdims=True)
        dlb_ref[...] += jnp.sum(dvln, axis=0, keepdims=True)
        dv = _ln_bwd(dvln * lg, vhat, rs)
        gelu_grad = phi + prev * jnp.exp(-0.5 * prev * prev) * INV_SQRT_2PI
        dpre = (jnp.concatenate([du, dv], axis=1) * gelu_grad).astype(BF16)
        dpre_ref[...] = dpre
        dxn = _proj_in_bwd_tail(dpre, win_v, c)
        xh, r = _rms_fwd(x_ref[...])
        dg_ref[...] += jnp.sum(dxn * xh, axis=0, keepdims=True)
        dx_ref[...] = dy_ref[...] + _rms_bwd(dxn, xh, r, g_ref[...])

    return _call(
        body, name=name, grid=(t // tm,),
        in_specs=[_row_spec(tm, d), _row_spec(tm, d), _row_spec(tm, pc), _const_spec((1, d)), _const_spec((1, d)), _const_spec((1, d)),
                  _const_spec(w.shape), _const_spec(bt.shape), HBM_SPEC, HBM_SPEC],
        out_specs=[_row_spec(tm, d), _row_spec(tm, pc), _const_spec((1, d)), _const_spec((1, d)), _const_spec((1, d)),
                   _const_spec(w.shape), _const_spec(bt.shape)],
        out_shape=[jax.ShapeDtypeStruct((t, d), F32), jax.ShapeDtypeStruct((t, pc), BF16), jax.ShapeDtypeStruct((1, d), F32),
                   jax.ShapeDtypeStruct((1, d), F32), jax.ShapeDtypeStruct((1, d), F32), jax.ShapeDtypeStruct(w.shape, F32),
                   jax.ShapeDtypeStruct(bt.shape, F32)],
        scratch_shapes=[pltpu.VMEM((N_CHIPS, d, c), BF16), pltpu.VMEM((d, d), BF16), pltpu.SemaphoreType.DMA((2 * N_CHIPS,))],
        args=(x, dy, pre, g, ln_g, ln_b, w, bt, win_g, wout_g), comms=comms)


def _tn_matmul(a, b, scale, bm, bn, name, comms=(), after=()):
    t, m = a.shape
    n = b.shape[1]
    tk = min(2048, t)
    bm, bn = min(bm, m), min(bn, n)
    nk = t // tk

    def body(a_ref, b_ref, o_ref, acc_ref):
        k = pl.program_id(2)

        @pl.when(k == 0)
        def _():
            acc_ref[...] = jnp.zeros_like(acc_ref)

        bv = b_ref[...]
        if bv.dtype != BF16:
            bv = (scale * bv).astype(BF16)
        acc_ref[...] += _dot_tn(a_ref[...], bv)

        @pl.when(k == nk - 1)
        def _():
            o_ref[...] = acc_ref[...].astype(BF16)

    return _call(
        body, name=name, grid=(m // bm, n // bn, nk),
        in_specs=[pl.BlockSpec((tk, bm), lambda i, j, k: (k, i)), pl.BlockSpec((tk, bn), lambda i, j, k: (k, j))],
        out_specs=[pl.BlockSpec((bm, bn), lambda i, j, k: (i, j))],
        out_shape=[jax.ShapeDtypeStruct((m, n), BF16)],
        scratch_shapes=[pltpu.VMEM((bm, bn), F32)],
        args=(a, b), comms=comms, after=after)[0]


def _row_tile(rows, cols, budget_bytes=2 * 1024 * 1024):
    best = None
    for cand in range(16, rows + 1, 16):
        if rows % cand == 0 and cand * cols * 4 <= budget_bytes:
            best = cand
    return best or rows


def _scalar_grid(grid, in_specs, out_specs):
    return pltpu.PrefetchScalarGridSpec(num_scalar_prefetch=1, grid=grid, in_specs=in_specs, out_specs=out_specs)


def _cast_into_slot(w, layer, me, name):
    _, rows, cols = w.shape
    tr = _row_tile(rows, cols)

    def body(me_ref, w_ref, o_ref):
        o_ref[...] = w_ref[...].astype(BF16)

    return pl.pallas_call(
        body, name=name,
        grid_spec=_scalar_grid((rows // tr,), [pl.BlockSpec((None, tr, cols), lambda i, me: (layer, i, 0))],
                               pl.BlockSpec((None, tr, cols), lambda i, me: (me[0], i, 0))),
        out_shape=jax.ShapeDtypeStruct((N_CHIPS, rows, cols), BF16), compiler_params=_cparams())(me, w)


def _add_half(view, other, core, name):
    q, _, r, c = view.shape
    tr = _row_tile(r, c)

    def body(core_ref, a_ref, b_ref, o_ref):
        o_ref[...] = (a_ref[...].astype(F32) + b_ref[...].astype(F32)).astype(BF16)

    return pl.pallas_call(
        body, name=name,
        grid_spec=_scalar_grid((q, r // tr), [pl.BlockSpec((None, None, tr, c), lambda k, i, core: (k, core[0], i, 0)),
                                             pl.BlockSpec((None, tr, c), lambda k, i, core: (k, i, 0))],
                               pl.BlockSpec((None, tr, c), lambda k, i, core: (k, i, 0))),
        out_shape=jax.ShapeDtypeStruct((q, r, c), BF16), compiler_params=_cparams_nd(2))(core, view, other)


def _reduce_piece(partial, staged, me, column_sharded, name):
    _, r, c = staged.shape
    tr = _row_tile(r, c, budget_bytes=1024 * 1024)
    nt = r // tr
    if column_sharded:
        own2d = partial.reshape(r, N_CHIPS * c)
        own_spec = pl.BlockSpec((tr, c), lambda i, me: (i, me[0]))
    else:
        own2d = partial.reshape(N_CHIPS * r, c)
        own_spec = pl.BlockSpec((tr, c), lambda i, me: (me[0] * nt + i, 0))
    ring = [pl.BlockSpec((None, tr, c), lambda i, me, k=k: ((me[0] + k) % N_CHIPS, i, 0)) for k in (1, 2, 3)]

    def body(me_ref, own_ref, s1_ref, s2_ref, s3_ref, o_ref):
        o_ref[...] = ((own_ref[...].astype(F32) + s1_ref[...].astype(F32)) + s2_ref[...].astype(F32)) + s3_ref[...].astype(F32)

    return pl.pallas_call(
        body, name=name, grid_spec=_scalar_grid((nt,), [own_spec] + ring, pl.BlockSpec((tr, c), lambda i, me: (i, 0))),
        out_shape=jax.ShapeDtypeStruct((r, c), F32), compiler_params=_cparams())(me, own2d, staged, staged, staged)


def _sum_leading(s, name):
    n, rows, cols = s.shape
    tr = _row_tile(rows, cols, budget_bytes=1024 * 1024)

    def body(s_ref, o_ref):
        acc = s_ref[0].astype(F32)
        for k in range(1, n):
            acc = acc + s_ref[k].astype(F32)
        o_ref[...] = acc

    return pl.pallas_call(
        body, name=name, grid=(rows // tr,), in_specs=[pl.BlockSpec((n, tr, cols), lambda i: (0, i, 0))], out_specs=_row_spec(tr, cols),
        out_shape=jax.ShapeDtypeStruct((rows, cols), F32), compiler_params=_cparams())(s)


ADAM_C1 = 1.0 / (1.0 - ADAM_B1 ** ADAM_STEP)
ADAM_C2 = 1.0 / (1.0 - ADAM_B2 ** ADAM_STEP)


def _adamw_math(w, g, m, v):
    mn = ADAM_B1 * m + (1.0 - ADAM_B1) * g
    vn = ADAM_B2 * v + (1.0 - ADAM_B2) * (g * g)
    return -ADAM_LR * ((mn * ADAM_C1) / (jnp.sqrt(vn * ADAM_C2) + ADAM_EPS) + ADAM_WD * w), mn, vn


def _adamw(w, g, m, v, name):
    shape = w.shape
    cols = shape[-1] if w.ndim > 1 else 128
    w2, g2, m2, v2 = (a.reshape(-1, cols) for a in (w, g, m, v))
    rows = w2.shape[0]
    tr = _row_tile(rows, cols, budget_bytes=1024 * 1024)

    def body(w_ref, g_ref, m_ref, v_ref, d_ref, mo_ref, vo_ref):
        d_ref[...], mo_ref[...], vo_ref[...] = _adamw_math(w_ref[...], g_ref[...], m_ref[...], v_ref[...])

    spec = _row_spec(tr, cols)
    outs = pl.pallas_call(
        body, name=name, grid=(rows // tr,), in_specs=[spec] * 4, out_specs=[spec] * 3,
        out_shape=[jax.ShapeDtypeStruct((rows, cols), F32)] * 3, compiler_params=_cparams())(w2, g2, m2, v2)
    return tuple(o.reshape(shape) for o in outs)


def _adamw_sharded(w, g_mine, g_sibling, m, v, core, layer, prev, name, comms=(), after=()):
    n_layers, r, c = w.shape
    half = r // 2
    tr = _row_tile(half, c)
    nt = half // tr

    def body(core_ref, w_ref, gm_ref, gs_ref, m_ref, v_ref, *rest):
        g_ref, d_ref, mo_ref, vo_ref = rest[-4:]
        gv = jnp.where(pl.program_id(0) == core_ref[0], gm_ref[...], gs_ref[...])
        g_ref[...] = gv
        d_ref[...], mo_ref[...], vo_ref[...] = _adamw_math(w_ref[...], gv, m_ref[...], v_ref[...])

    full = pl.BlockSpec((None, tr, c), lambda h, i, core: (layer, h * nt + i, 0))
    part = pl.BlockSpec((tr, c), lambda h, i, core: (i, 0))
    args = [w, g_mine, g_sibling, m, v]
    in_specs = [full, part, part, full, full]
    aliases = {}
    if prev is not None:
        aliases = {len(args) + k: k for k in range(4)}
        args += list(prev)
        in_specs += [pl.BlockSpec(memory_space=pl.ANY)] * 4
    args += list(after)
    in_specs += [pl.BlockSpec(memory_space=pl.ANY)] * len(after)
    return _call(body, name=name, grid=(2, nt), in_specs=in_specs, out_specs=[full] * 4, out_shape=[jax.ShapeDtypeStruct(w.shape, F32)] * 4,
                 args=args, comms=comms, scalar=core, aliases=aliases)


BIG_IN = ("ffn1_w_in", "ffn2_w_in", "ab_w_in", "sgu_w_in")
BIG_OUT = ("ffn1_w_out", "ffn2_w_out", "ab_w_out", "sgu_w_out")
BIG = BIG_IN + BIG_OUT


class _Gatherer:
    def __init__(self, slots):
        self.slots = dict(slots)

    def _stage(self, keys, d2d):
        n = len(keys)

        def plan(ins, outs, place):
            x, y, c = place
            me = 2 * x + y
            remote = []
            for a in range(n):
                rows = ins[a].shape[1] // 2

                def half(ref, q, core, rows=rows):
                    return ref.at[q, pl.ds(core * rows, rows), :]

                for (px, py) in _other_chips(x, y):
                    q = 2 * px + py
                    if d2d:
                        remote.append((half(ins[a], q, c), half(outs[a], q, c), (x, y, 1 - c), half(outs[a], q, 1 - c)))
                    else:
                        remote.append((half(ins[a], me, c), half(outs[a], me, c), (px, py, c), half(outs[a], q, c)))
            return remote

        def finish(outs):
            for k, o in zip(keys, outs):
                self.slots[k] = o

        arrays = [self.slots[k] for k in keys]
        return _Exchange(arrays, [_sds(a) for a in arrays], plan, 3 * n, {a: a for a in range(n)}, finish)

    def direct(self, keys):
        n = len(keys)

        def plan(ins, outs, place):
            x, y, c = place
            me = 2 * x + y
            return [(ins[a].at[me], outs[a].at[me], (px, py, c), outs[a].at[2 * px + py]) for a in range(n) for (px, py) in _other_chips(x, y)]

        def finish(outs):
            for k, o in zip(keys, outs):
                self.slots[k] = o

        arrays = [self.slots[k] for k in keys]
        return _Exchange(arrays, [_sds(a) for a in arrays], plan, 3 * n, {a: a for a in range(n)}, finish)

    def ici(self, keys):
        return self._stage(keys, False)

    def d2d(self, keys):
        return self._stage(keys, True)


class _Reducer:
    def __init__(self, me, core):
        self.me, self.core = me, core
        self.views, self.partial, self.staged, self.mine, self.theirs = {}, {}, {}, {}, {}

    def add(self, key, g):
        m, n = g.shape
        if key[0] in BIG_IN:
            self.views[key] = g.reshape(1, 2, m // 2, n)
        else:
            self.views[key] = g.reshape(N_CHIPS, 2, m // (2 * N_CHIPS), n)

    def swap(self, keys):
        views = [self.views[k] for k in keys]

        def plan(ins, outs, place):
            x, y, c = place
            return [(ins[a].at[:, 1 - c], outs[a], (x, y, 1 - c), outs[a]) for a in range(len(keys))]

        def finish(outs):
            for k, v, o in zip(keys, views, outs):
                self.partial[k] = _add_half(v, o, self.core, f"chip_partial_{k[0]}_{k[1]}")

        shapes = [jax.ShapeDtypeStruct((v.shape[0],) + v.shape[2:], v.dtype) for v in views]
        return _Exchange(views, shapes, plan, len(keys), None, finish)

    def scatter(self, keys, part=(0, 1)):
        i, n = part
        n_keys = len(keys)
        parts = [self.partial[k] for k in keys]
        shapes = []
        for k, p in zip(keys, parts):
            q, r, c = p.shape
            shapes.append(jax.ShapeDtypeStruct((N_CHIPS, r, c // N_CHIPS if k[0] in BIG_IN else c), p.dtype))

        def piece(ref, key, q, rows, cols):
            return ref.at[0, rows, pl.ds(q * cols, cols)] if key[0] in BIG_IN else ref.at[q, rows, :]

        def plan(ins, outs, place):
            x, y, c = place
            me = 2 * x + y
            remote = []
            for a, k in enumerate(keys):
                _, r, cols = shapes[a].shape
                rows = pl.ds(i * (r // n), r // n)
                for (px, py) in _other_chips(x, y):
                    q = 2 * px + py
                    remote.append((piece(ins[a], k, q, rows, cols), outs[a].at[me, rows, :], (px, py, c), outs[a].at[q, rows, :]))
            return remote

        def finish(outs):
            for k, p, o in zip(keys, parts, outs):
                self.staged[k] = o
                if i == n - 1:
                    self.mine[k] = _reduce_piece(p, o, self.me, k[0] in BIG_IN, f"reduce_{k[0]}_{k[1]}")

        inputs, aliases = parts, None
        if i > 0:
            inputs = parts + [self.staged[k] for k in keys]
            aliases = {n_keys + a: a for a in range(n_keys)}
        return _Exchange(inputs, shapes, plan, 3 * n_keys, aliases, finish)

    def scatter_behind(self, keys, work, tag):
        n = len(keys)
        parts = [self.partial[k] for k in keys]
        lands = []
        for k, p in zip(keys, parts):
            _, r, c = p.shape
            lands.append(jax.ShapeDtypeStruct((N_CHIPS, r, c // N_CHIPS if k[0] in BIG_IN else c), p.dtype))
        sem_spec = pl.BlockSpec(memory_space=pltpu.SEMAPHORE)
        effect = pltpu.CompilerParams(has_side_effects=pltpu.SideEffectType.DATAFLOW_SIDE_EFFECTING)

        def copies(part_refs, land_refs, send_sems, recv_sems):
            x, y, c = _my_place()
            me = 2 * x + y
            out = []
            for a, k in enumerate(keys):
                cols = lands[a].shape[2]
                for j, (px, py) in enumerate(_other_chips(x, y)):
                    q = 2 * px + py
                    src = part_refs[a].at[0, :, pl.ds(q * cols, cols)] if k[0] in BIG_IN else part_refs[a].at[q]
                    sems = dict(send_sem=send_sems.at[3 * a + j], recv_sem=recv_sems.at[3 * a + j], device_id=(px, py, c), device_id_type=MESH)
                    out.append((pltpu.make_async_remote_copy(src_ref=src, dst_ref=land_refs[a].at[me], **sems),
                                pltpu.make_async_remote_copy(src_ref=src, dst_ref=land_refs[a].at[q], **sems)))
            return out

        def start_body(*refs):
            part_refs, land_refs, send_sems, recv_sems, token = refs[:n], refs[n:2 * n], refs[2 * n], refs[2 * n + 1], refs[-1]
            for send, _ in copies(part_refs, land_refs, send_sems, recv_sems):
                send.start()
            token[...] = jnp.zeros_like(token)

        def wait_body(*refs):
            part_refs, land_refs, send_sems, recv_sems = refs[:n], refs[n:2 * n], refs[2 * n], refs[2 * n + 1]
            for send, arrive in copies(part_refs, land_refs, send_sems, recv_sems):
                send.wait_send()
                arrive.wait_recv()

        in_hbm = [pltpu.with_memory_space_constraint(p, pltpu.HBM) for p in parts]
        in_hbm += [pltpu.with_memory_space_constraint(lax.empty(s.shape, s.dtype), pltpu.HBM) for s in lands]
        thru_shapes = [pltpu.HBM(p.shape, p.dtype) for p in parts] + [pltpu.HBM(s.shape, s.dtype) for s in lands]
        started = pl.pallas_call(
            start_body, name=f"scatter_{tag}_start", in_specs=[HBM_SPEC] * (2 * n),
            out_shape=(pltpu.SemaphoreType.DMA((3 * n,)), pltpu.SemaphoreType.DMA((3 * n,)), *thru_shapes, jax.ShapeDtypeStruct((8, 128), F32)),
            out_specs=(sem_spec, sem_spec, *[HBM_SPEC] * (2 * n), pl.BlockSpec(memory_space=pltpu.VMEM)),
            input_output_aliases={i: 2 + i for i in range(2 * n)}, compiler_params=effect)(*in_hbm)
        send_sems, recv_sems, thru, token = started[0], started[1], started[2:2 + 2 * n], started[-1]
        after = work(token)
        done = pl.pallas_call(
            wait_body, name=f"scatter_{tag}_wait", in_specs=[HBM_SPEC] * (2 * n) + [sem_spec, sem_spec] + [pl.BlockSpec(memory_space=pl.ANY)] * len(after),
            out_shape=tuple(thru_shapes), out_specs=tuple([HBM_SPEC] * (2 * n)), input_output_aliases={i: i for i in range(2 * n)},
            compiler_params=effect)(*thru, send_sems, recv_sems, *after)
        for a, k in enumerate(keys):
            self.staged[k] = done[n + a]
            self.mine[k] = _reduce_piece(done[a], done[n + a], self.me, k[0] in BIG_IN, f"reduce_{k[0]}_{k[1]}")

    def exchange(self, keys):
        mine = [self.mine[k] for k in keys]

        def plan(ins, outs, place):
            x, y, c = place
            return [(ins[a], outs[a], (x, y, 1 - c), outs[a]) for a in range(len(keys))]

        def finish(outs):
            for k, o in zip(keys, outs):
                self.theirs[k] = o

        return _Exchange(mine, [_sds(a) for a in mine], plan, len(keys), None, finish)


def _all_gather_full(gat, keys):
    n = len(keys)
    arrays = [gat.slots[k] for k in keys]
    per = 7

    def body(*refs):
        ins, outs = refs[:n], refs[n:2 * n]
        send_sems, recv_sems = refs[2 * n:]
        x, y, c = _my_place()
        sibling, x_nbr, y_nbr = (x, y, 1 - c), (1 - x, y, c), (x, 1 - y, c)
        me, qx, qy, qd = 2 * x + y, 2 * (1 - x) + y, 2 * x + (1 - y), 2 * (1 - x) + (1 - y)

        def half(ref, q, core):
            rows = ref.shape[1] // 2
            return ref.at[q, pl.ds(core * rows, rows), :]

        def quarter(ref, q, core, k):
            rows = ref.shape[1] // 4
            return ref.at[q, pl.ds((2 * core + k) * rows, rows), :]

        def copy(a, k, src, dst, to):
            return pltpu.make_async_remote_copy(src_ref=src, dst_ref=dst, send_sem=send_sems.at[per * a + k],
                                                recv_sem=recv_sems.at[per * a + k], device_id=to, device_id_type=MESH)

        sent = []

        def send(a, k, part, to):
            cp = copy(a, k, part, part, to)
            cp.start()
            sent.append(cp)

        def landed(a, k, part):
            copy(a, k, part, part, sibling).wait_recv()

        for a in range(n):
            mine_in, mine_out = half(ins[a], me, c), half(outs[a], me, c)
            for k, to in ((0, x_nbr), (1, y_nbr)):
                cp = copy(a, k, mine_in, mine_out, to)
                cp.start()
                sent.append(cp)
        for a in range(n):
            landed(a, 1, half(outs[a], qy, c))
            send(a, 2, quarter(outs[a], qy, c, 0), x_nbr)
            send(a, 5, half(outs[a], qy, c), sibling)
            landed(a, 0, half(outs[a], qx, c))
            send(a, 3, quarter(outs[a], qx, c, 1), y_nbr)
            send(a, 4, half(outs[a], qx, c), sibling)
        for a in range(n):
            landed(a, 2, quarter(outs[a], qd, c, 0))
            landed(a, 3, quarter(outs[a], qd, c, 1))
            send(a, 6, half(outs[a], qd, c), sibling)
        for a in range(n):
            for k, q in ((4, qx), (5, qy), (6, qd)):
                landed(a, k, half(outs[a], q, 1 - c))
        for cp in sent:
            cp.wait_send()

    outs = pl.pallas_call(
        body, name="all_gather_first_weights", in_specs=[HBM_SPEC] * n, out_specs=[HBM_SPEC] * n,
        out_shape=[_sds(a) for a in arrays], input_output_aliases={a: a for a in range(n)},
        scratch_shapes=[pltpu.SemaphoreType.DMA((per * n,)), pltpu.SemaphoreType.DMA((per * n,))])(*arrays)
    for k, o in zip(keys, outs):
        gat.slots[k] = o


def _small_all_gather(buf, done):
    state = {}

    def index(x, y, c):
        return 4 * x + 2 * y + c

    def plan_ici(ins, outs, place):
        x, y, c = place
        return [(ins[0], outs[0].at[index(x, y, c)], (px, py, c), outs[0].at[index(px, py, c)]) for (px, py) in _other_chips(x, y)]

    def local(ins, outs, place):
        return [(ins[0], outs[0].at[index(*place)])]

    def plan_d2d(ins, outs, place):
        x, y, c = place
        return [(ins[0].at[index(px, py, c)], outs[0].at[index(px, py, c)], (x, y, 1 - c), outs[0].at[index(px, py, 1 - c)])
                for (px, py) in [(x, y)] + _other_chips(x, y)]

    def second():
        return _Exchange([state["blocks"]], [_sds(state["blocks"])], plan_d2d, N_CHIPS, {0: 0}, lambda outs: done(outs[0]))

    first = _Exchange([buf], [jax.ShapeDtypeStruct((2 * N_CHIPS,) + buf.shape, buf.dtype)], plan_ici, 3, None,
                      lambda outs: state.update(blocks=outs[0]), local, 1)
    return first, second


WEIGHT_NAMES = ("ffn1_norm", "ffn1_w_in", "ffn1_w_out", "mix_norm", "ffn2_norm", "ffn2_w_in", "ffn2_w_out", "ab_w_in", "pool_w", "pool_b",
                "pool_scale", "conv_w", "conv_b", "conv_ln_g", "conv_ln_b", "ab_w_out", "sgu_w_in", "sgu_ln_g", "sgu_ln_b", "sgu_w", "sgu_b",
                "sgu_w_out", "final_norm")
SMALL = tuple(n for n in WEIGHT_NAMES if n not in BIG)
SHARDED_SMALL = ("conv_w", "sgu_ln_g", "sgu_ln_b")
PACK_ROWS = 64
PACK = ("pack", 0)


def _pair(prefix, layer):
    return [(prefix + "_w_in", layer), (prefix + "_w_out", layer)]


def kernel(x, ffn1_norm, ffn1_w_in, ffn1_w_out, mix_norm, ffn2_norm, ffn2_w_in, ffn2_w_out, ab_w_in, pool_w, pool_b, pool_scale, conv_w, conv_b, conv_ln_g, conv_ln_b, ab_w_out, sgu_w_in, sgu_ln_g, sgu_ln_b, sgu_w, sgu_b, sgu_w_out, final_norm, loss_target, m_ffn1_norm, m_ffn1_w_in, m_ffn1_w_out, m_mix_norm, m_ffn2_norm, m_ffn2_w_in, m_ffn2_w_out, m_ab_w_in, m_pool_w, m_pool_b, m_pool_scale, m_conv_w, m_conv_b, m_conv_ln_g, m_conv_ln_b, m_ab_w_out, m_sgu_w_in, m_sgu_ln_g, m_sgu_ln_b, m_sgu_w, m_sgu_b, m_sgu_w_out, m_final_norm, v_ffn1_norm, v_ffn1_w_in, v_ffn1_w_out, v_mix_norm, v_ffn2_norm, v_ffn2_w_in, v_ffn2_w_out, v_ab_w_in, v_pool_w, v_pool_b, v_pool_scale, v_conv_w, v_conv_b, v_conv_ln_g, v_conv_ln_b, v_ab_w_out, v_sgu_w_in, v_sgu_ln_g, v_sgu_ln_b, v_sgu_w, v_sgu_b, v_sgu_w_out, v_final_norm):
    given = dict(locals())
    w = {n: given[n] for n in WEIGHT_NAMES}
    chip = 2 * lax.axis_index("x") + lax.axis_index("y")
    me = chip.astype(jnp.int32).reshape(1)
    core = lax.axis_index("c").astype(jnp.int32).reshape(1)
    row = lambda v: v.reshape(1, -1)
    xin, tgt = x[0], loss_target[0]

    pack = jnp.concatenate([
        w["conv_w"][0], jnp.zeros((1, 128), F32), w["sgu_ln_g"].reshape(2, 128), w["sgu_ln_b"].reshape(2, 128),
        jnp.zeros((PACK_ROWS - 36, 128), F32)], axis=0)
    slots = {PACK: lax.dynamic_update_slice(jnp.zeros((N_CHIPS, PACK_ROWS, 128), F32), pack[None], (me[0], 0, 0))}
    for n in BIG:
        for layer in range(w[n].shape[0]):
            slots[(n, layer)] = _cast_into_slot(w[n], layer, me, f"cast_{n}_{layer}")
    gat = _Gatherer(slots)
    _all_gather_full(gat, _pair("ffn1", 0) + [PACK])
    gp = gat.slots[PACK]
    conv_w_full = jnp.transpose(gp[:, 0:CONV_WIDTH], (1, 0, 2)).reshape(CONV_WIDTH, N_CHIPS * 128)
    sgu_ln_g_full = gp[:, 32:34].reshape(1, -1)
    sgu_ln_b_full = gp[:, 34:36].reshape(1, -1)
    gw = lambda n, layer: gat.slots[(n, layer)]

    st = [dict(), dict()]
    st[0]["xa"] = xin
    later = _pair("sgu", 0) + _pair("ffn2", 1)
    cur, st[0]["h1"], st[0]["xn1"] = _ffn_fwd(xin, row(w["ffn1_norm"][0]), gw("ffn1_w_in", 0), gw("ffn1_w_out", 0), "ffn1_fwd_0",
                                              comms=[gat.direct(_pair("ab", 0)), gat.ici(_pair("ffn2", 0))])
    st[0]["xb"] = cur
    st[0]["h0"], st[0]["xnm"] = _norm_matmul(cur, row(w["mix_norm"][0]), gw("ab_w_in", 0), "mix0_proj_in",
                                             comms=[gat.d2d(_pair("ffn2", 0)), gat.ici([("ffn1_w_out", 1)])])
    pool_args = (w["pool_w"][0], row(w["pool_b"][0]), row(w["pool_scale"][0]), conv_w_full, row(w["conv_b"][0]), row(w["conv_ln_g"][0]),
                 row(w["conv_ln_b"][0]), gw("ab_w_out", 0))
    cur, st[0]["ycat"], st[0]["yconv"] = _mix0_fwd(cur, st[0]["h0"], *pool_args, "mix0_fwd", comms=[gat.ici([("ffn1_w_in", 1)])])
    st[0]["xc"] = cur
    cur, st[0]["h2"], st[0]["xn2"] = _ffn_fwd(cur, row(w["ffn2_norm"][0]), gw("ffn2_w_in", 0), gw("ffn2_w_out", 0), "ffn2_fwd_0",
                                              comms=[gat.d2d(_pair("ffn1", 1)), gat.ici(later)])
    st[1]["xa"] = cur
    cur, st[1]["h1"], st[1]["xn1"] = _ffn_fwd(cur, row(w["ffn1_norm"][1]), gw("ffn1_w_in", 1), gw("ffn1_w_out", 1), "ffn1_fwd_1",
                                              comms=[gat.d2d(later)])
    st[1]["xb"] = cur
    st[1]["pre"], st[1]["xnm"] = _norm_matmul(cur, row(w["mix_norm"][1]), gw("sgu_w_in", 0), "sgu_proj_in")
    sgu_args = (sgu_ln_g_full, sgu_ln_b_full, w["sgu_w"][0], w["sgu_b"][0].T)
    cur, st[1]["p"] = _sgu_fwd(cur, st[1]["pre"], *sgu_args, gw("sgu_w_out", 0), "sgu_fwd")
    st[1]["xc"] = cur
    dy, st[1]["h2"], st[1]["xn2"], loss, d_final = _ffn_fwd(cur, row(w["ffn2_norm"][1]), gw("ffn2_w_in", 1), gw("ffn2_w_out", 1),
                                                            "ffn2_fwd_1_loss", loss_head=(tgt, row(w["final_norm"])))

    red = _Reducer(me, core)
    small = {"final_norm": d_final.reshape(-1)}
    norm_grads = {"ffn1_norm": [None] * DEPTH, "mix_norm": [None] * DEPTH, "ffn2_norm": [None] * DEPTH}
    ga, gb, gc, gd, ge, gf = _pair("ffn2", 1), _pair("sgu", 0), _pair("ffn1", 1), _pair("ffn2", 0), _pair("ab", 0), _pair("ffn1", 0)

    def ffn_backward(prefix, layer, xs, hs, xns, dy_in, bwd_comms=(), dwin_comms=(), dwout_comms=()):
        dx, dh, act, norm_grads[prefix + "_norm"][layer] = _ffn_bwd(
            xs, dy_in, hs, row(w[prefix + "_norm"][layer]), gw(prefix + "_w_in", layer), gw(prefix + "_w_out", layer),
            f"{prefix}_bwd_{layer}", comms=bwd_comms)
        red.add((prefix + "_w_in", layer), _tn_matmul(xns, dh, 1.0, 1024, 1408, f"{prefix}_dwin_{layer}", comms=dwin_comms))
        red.add((prefix + "_w_out", layer), _tn_matmul(act, dy_in, 0.5, 1408, 1024, f"{prefix}_dwout_{layer}", comms=dwout_comms))
        return dx

    s1, s0 = st[1], st[0]
    dy = ffn_backward("ffn2", 1, s1["xc"], s1["h2"], s1["xn2"], dy)
    dy_in = dy
    dy, dpre, norm_grads["mix_norm"][1], dlg, dlb, dw, dbt = _sgu_bwd(
        s1["xb"], dy_in, s1["pre"], row(w["mix_norm"][1]), *sgu_args, gw("sgu_w_in", 0), gw("sgu_w_out", 0), "sgu_bwd", comms=[red.swap(ga)])
    red.add(("sgu_w_in", 0), _tn_matmul(s1["xnm"], dpre, 1.0, 1024, 2048, "sgu_dwin"))
    red.add(("sgu_w_out", 0), _tn_matmul(s1["p"], dy_in, 1.0, 1024, 1024, "sgu_dwout"))
    small.update(sgu_ln_g=dlg, sgu_ln_b=dlb, sgu_w=dw[None], sgu_b=dbt.T[None])
    dy = ffn_backward("ffn1", 1, s1["xa"], s1["h1"], s1["xn1"], dy, bwd_comms=[lambda: red.scatter(ga), lambda: red.swap(gb)],
                      dwin_comms=[lambda: red.scatter(gb), lambda: red.exchange(ga)])
    dy = ffn_backward("ffn2", 0, s0["xc"], s0["h2"], s0["xn2"], dy, bwd_comms=[lambda: red.swap(gc), lambda: red.exchange(gb)],
                      dwin_comms=[lambda: red.scatter(gc)])
    dy_in = dy
    dconv, dpc, dpw, rows = _mix0_bwd_a(dy_in, s0["h0"], s0["yconv"], *pool_args, "mix0_bwd_a")
    dy, dh0, norm_grads["mix_norm"][0] = _mix0_bwd_b(s0["xb"], dy_in, s0["h0"], dconv, dpc, row(w["mix_norm"][0]), conv_w_full,
                                                      gw("ab_w_in", 0), "mix0_bwd_b")
    red.add(("ab_w_in", 0), _tn_matmul(s0["xnm"], dh0, 1.0, 1024, 1536, "ab_dwin", comms=[red.swap(gd), red.exchange(gc)]))
    small.update(pool_w=dpw[None], conv_w=rows[None, 0:CONV_WIDTH], conv_b=rows[32:33], conv_ln_g=rows[33:34], conv_ln_b=rows[34:35],
                 pool_scale=rows[35:36], pool_b=rows[36:37].reshape(1, len(POOL_WINDOWS), POOL_GC))

    last = {}

    def behind_ffn2_scatter(token):
        red.add(("ab_w_out", 0), _tn_matmul(s0["ycat"], dy_in, 1.0, 1024, 1024, "ab_dwout", after=[token]))
        last["dx"], last["dh"], last["act"], norm_grads["ffn1_norm"][0] = _ffn_bwd(
            s0["xa"], dy, s0["h1"], row(w["ffn1_norm"][0]), gw("ffn1_w_in", 0), gw("ffn1_w_out", 0), "ffn1_bwd_0", after=[token])
        return [norm_grads["ffn1_norm"][0]]

    red.scatter_behind(gd, behind_ffn2_scatter, "ffn2_0")

    small_sum = {}

    def small_ready():
        for k, v in norm_grads.items():
            small[k] = jnp.concatenate(v, axis=0)
        flat = [small[n].reshape(-1, 128) for n in SMALL]
        rows = sum(f.shape[0] for f in flat)
        loss_block = jnp.pad(loss, ((0, 8 + (-rows) % 8 - 1), (0, 127)))
        buf = jnp.concatenate(flat + [loss_block], axis=0)

        def done(gathered):
            total, at = _sum_leading(gathered, "reduce_small"), 0
            for n, f in zip(SMALL, flat):
                small_sum[n] = total[at:at + f.shape[0]].reshape(small[n].shape)
                at += f.shape[0]
            small_sum["loss"] = total[at:at + 1, 0:1]

        return _small_all_gather(buf, done)

    small_first, small_second = small_ready()
    red.add(("ffn1_w_in", 0), _tn_matmul(s0["xn1"], last["dh"], 1.0, 1024, 1408, "ffn1_dwin_0", comms=[red.swap(ge), small_first]))
    red.add(("ffn1_w_out", 0), _tn_matmul(last["act"], dy, 0.5, 1408, 1024, "ffn1_dwout_0",
                                          comms=[red.scatter(ge), red.exchange(gd), small_second]))
    grad_x = last["dx"]

    big_out = {}

    def adamw_big(n, layer, after=()):
        big_out[n] = _adamw_sharded(w[n], red.mine[(n, layer)], red.theirs[(n, layer)], given["m_" + n], given["v_" + n], core, layer,
                                    big_out.get(n), f"adamw_{n}_{layer}", after=after)

    _exchange_alone("swap_last_grads", [red.swap(gf), red.exchange(ge)])

    def other_updates(token):
        for n in BIG:
            for layer in reversed(range(w[n].shape[0])):
                if (n, layer) not in gf:
                    adamw_big(n, layer, after=[token])
        return [big_out[n][0] for n in BIG]

    red.scatter_behind(gf, other_updates, "last")
    _exchange_alone("exchange_last_grads", [red.exchange(gf)])
    for key in gf:
        adamw_big(*key)

    loss = small_sum["loss"][0, 0]
    grads, delta, new_m, new_v = {}, {}, {}, {}
    for n in WEIGHT_NAMES:
        mom, var = given["m_" + n], given["v_" + n]
        if n in BIG:
            grads[n], delta[n], new_m[n], new_v[n] = big_out[n]
            continue
        g = small_sum[n]
        if n in SHARDED_SMALL:
            width = w[n].shape[-1]
            g = lax.dynamic_slice_in_dim(g, chip * width, width, axis=g.ndim - 1)
        grads[n] = g
        delta[n], new_m[n], new_v[n] = _adamw(w[n], g, mom, var, f"adamw_{n}")
    return (loss, grad_x[None], *[grads[n] for n in WEIGHT_NAMES], *[delta[n] for n in WEIGHT_NAMES],
            *[new_m[n] for n in WEIGHT_NAMES], *[new_v[n] for n in WEIGHT_NAMES])
```

```python
import jax
import jax.numpy as jnp
from jax import lax
from jax.experimental import pallas as pl
from jax.experimental.pallas import tpu as pltpu

F32, BF16 = jnp.float32, jnp.bfloat16
EPS = 1e-6
N_CHIPS = 4
POOL_WINDOWS = (2, 4, 8, 16)
POOL_GC = 128
POOL_CH = 512
CONV_CH = 512
CONV_WIDTH = 31
HALO = 32
SGU_HEADS = 8
CHUNK = 128
DEPTH = 2
ADAM_LR, ADAM_B1, ADAM_B2, ADAM_EPS, ADAM_WD, ADAM_STEP = 0.001, 0.9, 0.999, 1e-08, 0.01, 10
VMEM_LIMIT_BYTES = 60 * 1024 * 1024
MESH_AXES = ("x", "y", "c")
MESH = pl.DeviceIdType.MESH
HBM_SPEC = pl.BlockSpec(memory_space=pltpu.HBM)


def _sds(a):
    return jax.ShapeDtypeStruct(a.shape, a.dtype)


def _cparams_nd(n):
    return pltpu.CompilerParams(dimension_semantics=("arbitrary",) * n, vmem_limit_bytes=VMEM_LIMIT_BYTES)


def _cparams():
    return _cparams_nd(1)


def _dot(a, b):
    return jnp.dot(a, b, preferred_element_type=F32)


def _dot_nt(a, b):
    return lax.dot_general(a, b, (((1,), (1,)), ((), ())), preferred_element_type=F32)


def _dot_tn(a, b):
    return lax.dot_general(a, b, (((0,), (0,)), ((), ())), preferred_element_type=F32)


def _rms_fwd(x):
    r = lax.rsqrt(jnp.mean(x * x, axis=-1, keepdims=True) + EPS)
    return x * r, r


def _rms_bwd(dxn, xh, r, g):
    dxh = dxn * g
    return r * (dxh - xh * jnp.mean(dxh * xh, axis=-1, keepdims=True))


def _ln_fwd(y):
    mu = jnp.mean(y, axis=-1, keepdims=True)
    yc = y - mu
    rs = lax.rsqrt(jnp.mean(yc * yc, axis=-1, keepdims=True) + EPS)
    return yc * rs, rs


def _ln_bwd(dyhat, yhat, rs):
    return rs * (dyhat - jnp.mean(dyhat, axis=-1, keepdims=True) - yhat * jnp.mean(dyhat * yhat, axis=-1, keepdims=True))


def _sigmoid(x):
    return 0.5 * jnp.tanh(0.5 * x) + 0.5


def _const_spec(shape):
    n = len(shape)
    return pl.BlockSpec(shape, lambda i: (0,) * n)


def _row_spec(tm, cols):
    return pl.BlockSpec((tm, cols), lambda i: (i, 0))


def _my_place():
    return lax.axis_index("x"), lax.axis_index("y"), lax.axis_index("c")


def _other_chips(x, y):
    return [(1 - x, y), (x, 1 - y), (1 - x, 1 - y)]


class _Exchange:
    def __init__(self, inputs, out_shapes, plan, count, aliases=None, finish=None, local=None, n_local=0):
        self.inputs, self.out_shapes, self.plan, self.count = list(inputs), list(out_shapes), plan, count
        self.aliases, self.finish, self.local, self.n_local = dict(aliases or {}), finish, local, n_local


def _call(body, *, name, grid, in_specs, out_specs, out_shape, args, scratch_shapes=(), comms=(), scalar=None, aliases=None, after=()):
    comms = [cm if isinstance(cm, _Exchange) else cm() for cm in comms]
    in_specs, out_specs, out_shape, scratch_shapes = list(in_specs), list(out_specs), list(out_shape), list(scratch_shapes)
    n_body_in = len(in_specs)
    in_specs += [pl.BlockSpec(memory_space=pl.ANY)] * len(after)
    args = list(args) + list(after)
    n_in, n_out, n_scr = len(in_specs), len(out_specs), len(scratch_shapes)
    c_in = [a for cm in comms for a in cm.inputs]
    c_out = [s for cm in comms for s in cm.out_shapes]
    n_remote = sum(cm.count for cm in comms)
    n_local = sum(cm.n_local for cm in comms)
    n_scalar = 0 if scalar is None else 1
    all_aliases = {n_scalar + i: o for i, o in (aliases or {}).items()}
    at_in, at_out = n_scalar + n_in, n_out
    for cm in comms:
        for i, o in cm.aliases.items():
            all_aliases[at_in + i] = at_out + o
        at_in += len(cm.inputs)
        at_out += len(cm.out_shapes)

    def wrapped(*all_refs):
        scalar_ref, refs = all_refs[:n_scalar], all_refs[n_scalar:]
        ins, ci = refs[:n_in], refs[n_in:n_in + len(c_in)]
        at = n_in + len(c_in)
        outs, co = refs[at:at + n_out], refs[at + n_out:at + n_out + len(c_out)]
        at += n_out + len(c_out)
        scr = refs[at:at + n_scr]

        def run_body():
            body(*scalar_ref, *ins[:n_body_in], *outs, *scr)

        if not comms:
            run_body()
            return
        send_sems, recv_sems, local_sems = refs[at + n_scr:]
        place = _my_place()
        sends, arrivals, locals_ = [], [], []
        i0 = o0 = 0
        for cm in comms:
            cm_in, cm_out = ci[i0:i0 + len(cm.inputs)], co[o0:o0 + len(cm.out_shapes)]
            i0 += len(cm.inputs)
            o0 += len(cm.out_shapes)
            for src, dst, dev, incoming in cm.plan(cm_in, cm_out, place):
                k = len(sends)
                sends.append(pltpu.make_async_remote_copy(src_ref=src, dst_ref=dst, send_sem=send_sems.at[k], recv_sem=recv_sems.at[k],
                                                          device_id=dev, device_id_type=MESH))
                arrivals.append(pltpu.make_async_remote_copy(src_ref=src, dst_ref=incoming, send_sem=send_sems.at[k],
                                                             recv_sem=recv_sems.at[k], device_id=dev, device_id_type=MESH))
            if cm.local is not None:
                for src, dst in cm.local(cm_in, cm_out, place):
                    locals_.append(pltpu.make_async_copy(src, dst, local_sems.at[len(locals_)]))

        def start():
            for cp in locals_ + sends:
                cp.start()

        def finish():
            for cp in arrivals:
                cp.wait_recv()
            for cp in sends:
                cp.wait_send()
            for cp in locals_:
                cp.wait()

        if not grid:
            start()
            run_body()
            finish()
            return
        ids = [pl.program_id(a) for a in range(len(grid))]
        first, last = ids[0] == 0, ids[0] == grid[0] - 1
        for a in range(1, len(grid)):
            first = jnp.logical_and(first, ids[a] == 0)
            last = jnp.logical_and(last, ids[a] == grid[a] - 1)
        pl.when(first)(start)
        run_body()
        pl.when(last)(finish)

    sems = []
    if comms:
        sems = [pltpu.SemaphoreType.DMA((max(n_remote, 1),)), pltpu.SemaphoreType.DMA((max(n_remote, 1),)),
                pltpu.SemaphoreType.DMA((max(n_local, 1),))]
    all_in, all_out = in_specs + [HBM_SPEC] * len(c_in), out_specs + [HBM_SPEC] * len(c_out)
    if scalar is None:
        kwargs = dict(grid=grid, compiler_params=_cparams_nd(len(grid))) if grid else {}
        res = pl.pallas_call(
            wrapped, name=name, in_specs=all_in, out_specs=all_out, out_shape=out_shape + c_out, scratch_shapes=scratch_shapes + sems,
            input_output_aliases=all_aliases, **kwargs)(*args, *c_in)
    else:
        spec = pltpu.PrefetchScalarGridSpec(num_scalar_prefetch=1, grid=grid, in_specs=all_in, out_specs=all_out,
                                            scratch_shapes=scratch_shapes + sems)
        res = pl.pallas_call(
            wrapped, name=name, grid_spec=spec, out_shape=out_shape + c_out, input_output_aliases=all_aliases,
            compiler_params=_cparams_nd(len(grid)))(scalar, *args, *c_in)
    at = n_out
    for cm in comms:
        got = res[at:at + len(cm.out_shapes)]
        at += len(cm.out_shapes)
        if cm.finish is not None:
            cm.finish(got)
    return list(res[:n_out])


def _exchange_alone(name, comms):
    _call(lambda: None, name=name, grid=(), in_specs=[], out_specs=[], out_shape=[], args=[], comms=comms)


def _in_weight_copies(w_hbm, w_v, sem, base=0):
    return [pltpu.make_async_copy(w_hbm.at[q], w_v.at[q], sem.at[base + q]) for q in range(N_CHIPS)]


def _out_weight_copies(w_hbm, w_v, sem, base=0):
    rows = w_hbm.shape[1]
    return [pltpu.make_async_copy(w_hbm.at[q], w_v.at[pl.ds(q * rows, rows)], sem.at[base + q]) for q in range(N_CHIPS)]


def _load_at_first_step(copies):
    @pl.when(pl.program_id(0) == 0)
    def _():
        for cp in copies:
            cp.start()
        for cp in copies:
            cp.wait()


def _loss_head(xo, tgt, gv, loss_ref, dg_ref):
    d = xo.shape[1]
    xh, r = _rms_fwd(xo)
    diff = xh * gv - tgt
    loss_ref[...] += 0.5 * jnp.sum(jnp.sum(diff * diff, axis=1, keepdims=True), axis=0, keepdims=True) / d
    dout = diff / d
    dg_ref[...] += jnp.sum(dout * xh, axis=0, keepdims=True)
    return _rms_bwd(dout, xh, r, gv)


def _ffn_fwd(x, g, win_g, wout_g, name, comms=(), loss_head=None):
    t, d = x.shape
    c = win_g.shape[-1]
    ff = 2 * c
    tm = min(512, t)
    n_head = 0 if loss_head is None else 2

    def body(x_ref, g_ref, win_hbm, wout_hbm, *rest):
        head_in, (xo_ref, h_ref, xn_ref), rest = rest[:n_head], rest[n_head:n_head + 3], rest[n_head + 3:]
        head_out, (win_v, wout_v, sem) = rest[:n_head], rest[n_head:]
        _load_at_first_step(_in_weight_copies(win_hbm, win_v, sem) + _out_weight_copies(wout_hbm, wout_v, sem, N_CHIPS))
        xv = x_ref[...]
        xh, _ = _rms_fwd(xv)
        xn = (xh * g_ref[...]).astype(BF16)
        xn_ref[...] = xn
        acc = jnp.zeros((tm, d), F32)
        for j in range(2):
            gate = _dot(xn, win_v[j])
            up = _dot(xn, win_v[j + 2])
            h_ref[:, j * c:(j + 1) * c] = gate.astype(BF16)
            h_ref[:, ff + j * c:ff + (j + 1) * c] = up.astype(BF16)
            act = (gate * _sigmoid(gate) * up).astype(BF16)
            acc = acc + _dot(act, wout_v[j * c:(j + 1) * c, :])
        xo = xv + 0.5 * acc
        if loss_head is None:
            xo_ref[...] = xo
            return

        @pl.when(pl.program_id(0) == 0)
        def _():
            for ref in head_out:
                ref[...] = jnp.zeros_like(ref)

        xo_ref[...] = _loss_head(xo, head_in[0][...], head_in[1][...], *head_out)

    head_specs = [] if loss_head is None else [_row_spec(tm, d), _const_spec((1, d))]
    head_out_specs = [] if loss_head is None else [_const_spec((1, 1)), _const_spec((1, d))]
    head_out_shape = [] if loss_head is None else [jax.ShapeDtypeStruct((1, 1), F32), jax.ShapeDtypeStruct((1, d), F32)]
    return _call(
        body, name=name, grid=(t // tm,),
        in_specs=[_row_spec(tm, d), _const_spec((1, d)), HBM_SPEC, HBM_SPEC] + head_specs,
        out_specs=[_row_spec(tm, d), _row_spec(tm, 2 * ff), _row_spec(tm, d)] + head_out_specs,
        out_shape=[jax.ShapeDtypeStruct((t, d), F32), jax.ShapeDtypeStruct((t, 2 * ff), BF16), jax.ShapeDtypeStruct((t, d), BF16)]
        + head_out_shape,
        scratch_shapes=[pltpu.VMEM((N_CHIPS, d, c), BF16), pltpu.VMEM((ff, d), BF16), pltpu.SemaphoreType.DMA((2 * N_CHIPS,))],
        args=(x, g, win_g, wout_g) + tuple(loss_head or ()), comms=comms)


def _ffn_bwd(x, dy, h, g, win_g, wout_g, name, comms=(), after=()):
    t, d = x.shape
    c = win_g.shape[-1]
    ff = 2 * c
    tm = min(256, t)

    def body(x_ref, dy_ref, h_ref, g_ref, win_hbm, wout_hbm, dx_ref, dh_ref, act_ref, dg_ref, win_v, wout_v, sem):
        _load_at_first_step(_in_weight_copies(win_hbm, win_v, sem) + _out_weight_copies(wout_hbm, wout_v, sem, N_CHIPS))

        @pl.when(pl.program_id(0) == 0)
        def _():
            dg_ref[...] = jnp.zeros_like(dg_ref)

        xv, dyv, gv = x_ref[...], dy_ref[...], g_ref[...]
        xh, r = _rms_fwd(xv)
        dyh = (0.5 * dyv).astype(BF16)
        dxn = jnp.zeros((tm, d), F32)
        for j in range(2):
            gate = h_ref[:, j * c:(j + 1) * c].astype(F32)
            up = h_ref[:, ff + j * c:ff + (j + 1) * c].astype(F32)
            dact = _dot_nt(dyh, wout_v[j * c:(j + 1) * c, :])
            s = _sigmoid(gate)
            sl = gate * s
            act_ref[:, j * c:(j + 1) * c] = (sl * up).astype(BF16)
            dgate = (dact * up * (s + sl * (1.0 - s))).astype(BF16)
            dup = (dact * sl).astype(BF16)
            dh_ref[:, j * c:(j + 1) * c] = dgate
            dh_ref[:, ff + j * c:ff + (j + 1) * c] = dup
            dxn = dxn + _dot_nt(dgate, win_v[j]) + _dot_nt(dup, win_v[j + 2])
        dg_ref[...] += jnp.sum(dxn * xh, axis=0, keepdims=True)
        dx_ref[...] = dyv + _rms_bwd(dxn, xh, r, gv)

    return _call(
        body, name=name, grid=(t // tm,),
        in_specs=[_row_spec(tm, d), _row_spec(tm, d), _row_spec(tm, 2 * ff), _const_spec((1, d)), HBM_SPEC, HBM_SPEC],
        out_specs=[_row_spec(tm, d), _row_spec(tm, 2 * ff), _row_spec(tm, ff), _const_spec((1, d))],
        out_shape=[jax.ShapeDtypeStruct((t, d), F32), jax.ShapeDtypeStruct((t, 2 * ff), BF16), jax.ShapeDtypeStruct((t, ff), BF16),
                   jax.ShapeDtypeStruct((1, d), F32)],
        scratch_shapes=[pltpu.VMEM((N_CHIPS, d, c), BF16), pltpu.VMEM((ff, d), BF16), pltpu.SemaphoreType.DMA((2 * N_CHIPS,))],
        args=(x, dy, h, g, win_g, wout_g), comms=comms, after=after)


def _norm_matmul(x, g, win_g, name, comms=()):
    t, d = x.shape
    c = win_g.shape[-1]
    tm = min(512, t)

    def body(x_ref, g_ref, win_hbm, o_ref, xn_ref, win_v, sem):
        _load_at_first_step(_in_weight_copies(win_hbm, win_v, sem))
        xh, _ = _rms_fwd(x_ref[...])
        xn = (xh * g_ref[...]).astype(BF16)
        xn_ref[...] = xn
        for q in range(N_CHIPS):
            o_ref[:, q * c:(q + 1) * c] = _dot(xn, win_v[q])

    return _call(
        body, name=name, grid=(t // tm,),
        in_specs=[_row_spec(tm, d), _const_spec((1, d)), HBM_SPEC],
        out_specs=[_row_spec(tm, N_CHIPS * c), _row_spec(tm, d)],
        out_shape=[jax.ShapeDtypeStruct((t, N_CHIPS * c), F32), jax.ShapeDtypeStruct((t, d), BF16)],
        scratch_shapes=[pltpu.VMEM((N_CHIPS, d, c), BF16), pltpu.SemaphoreType.DMA((N_CHIPS,))],
        args=(x, g, win_g), comms=comms)


def _proj_in_bwd_tail(dh, win_v, c):
    dxn = _dot_nt(dh[:, 0:c], win_v[0])
    for q in range(1, N_CHIPS):
        dxn = dxn + _dot_nt(dh[:, q * c:(q + 1) * c], win_v[q])
    return dxn


def _prev_halo_spec(tm, cols):
    return pl.BlockSpec((HALO, cols), lambda i: (jnp.maximum(i * (tm // HALO) - 1, 0), 0))


def _next_halo_spec(tm, cols, t):
    last = t // HALO - 1
    return pl.BlockSpec((HALO, cols), lambda i: (jnp.minimum((i + 1) * (tm // HALO), last), 0))


def _shift_down(w, k):
    return w if k == 0 else pltpu.roll(w, k, 0)


def _shift_up(w, k):
    return w if k == 0 else pltpu.roll(w, w.shape[0] - k, 0)


def _pool_counts(i, tm):
    pos = (i * tm + lax.broadcasted_iota(jnp.int32, (tm, POOL_CH), 0) + 1).astype(F32)
    lane = lax.broadcasted_iota(jnp.int32, (tm, POOL_CH), 1)
    win = jnp.where(lane < POOL_GC, 2.0, jnp.where(lane < 2 * POOL_GC, 4.0, jnp.where(lane < 3 * POOL_GC, 8.0, 16.0)))
    return jnp.minimum(pos, win)


def _group_select(parts):
    return jnp.concatenate([p[:, k * POOL_GC:(k + 1) * POOL_GC] for k, p in enumerate(parts)], axis=1)


TAP_ROWS = 128


def _tap_blocks(tm, block_fn):
    cols = []
    for ch in range(CONV_CH // POOL_GC):
        lanes = slice(ch * POOL_GC, (ch + 1) * POOL_GC)
        cols.append(jnp.concatenate([block_fn(r, lanes) for r in range(tm // TAP_ROWS)], axis=0))
    return jnp.concatenate(cols, axis=1)


def _conv_block(win, taps):
    acc = jnp.zeros((TAP_ROWS, win.shape[1]), F32)
    for k in range(CONV_WIDTH):
        acc = acc + taps[k:k + 1, :] * _shift_down(win, CONV_WIDTH - 1 - k)[HALO:, :]
    return acc


def _conv_block_transposed(win, taps):
    acc = jnp.zeros((TAP_ROWS, win.shape[1]), F32)
    for k in range(CONV_WIDTH):
        acc = acc + taps[k:k + 1, :] * _shift_up(win, CONV_WIDTH - 1 - k)[0:TAP_ROWS, :]
    return acc


def _mix0_recompute(i, tm, h_cur, h_prev, conv_w, conv_b, y=None):
    prev = jnp.where(i > 0, h_prev, 0.0)
    win = jnp.concatenate([prev, h_cur], axis=0)
    u_w = win[:, 0:POOL_CH]
    a_w = win[:, POOL_CH:POOL_CH + CONV_CH]
    gt_w = win[:, POOL_CH + CONV_CH:]
    g_w = a_w * _sigmoid(gt_w)
    if y is None:
        y = _tap_blocks(tm, lambda r, lanes: _conv_block(g_w[r * TAP_ROWS:(r + 1) * TAP_ROWS + HALO, lanes], conv_w[:, lanes])) + conv_b
    s2 = u_w + _shift_down(u_w, 1)
    s4 = s2 + _shift_down(s2, 2)
    s8 = s4 + _shift_down(s4, 4)
    s16 = s8 + _shift_down(s8, 8)
    sums = _group_select([s2[HALO:], s4[HALO:], s8[HALO:], s16[HALO:]])
    cnt = _pool_counts(i, tm)
    pooled = sums / cnt - h_cur[:, 0:POOL_CH]
    return g_w, y, pooled, cnt


def _pool_linear(pooled, pw_ref, pb):
    return jnp.concatenate(
        [_dot(pooled[:, k * POOL_GC:(k + 1) * POOL_GC].astype(BF16), pw_ref[k].astype(BF16)) for k in range(len(POOL_WINDOWS))], axis=1) + pb


def _mix0_fwd(x, h0, pool_w, pool_b, pool_scale, conv_w, conv_b, ln_g, ln_b, wout_g, name, comms=()):
    t, d = x.shape
    tm = min(512, t)
    hc = h0.shape[1]

    def body(x_ref, h_ref, hp_ref, pw_ref, pb_ref, ps_ref, cw_ref, cb_ref, lg_ref, lb_ref, wout_hbm, xo_ref, ycat_ref, y_ref, wout_v, sem):
        _load_at_first_step(_out_weight_copies(wout_hbm, wout_v, sem))
        i = pl.program_id(0)
        _, y, pooled, _ = _mix0_recompute(i, tm, h_ref[...], hp_ref[...], cw_ref[...], cb_ref[...])
        y_ref[...] = y
        yhat, _ = _ln_fwd(y)
        yn = yhat * lg_ref[...] + lb_ref[...]
        yb = yn * _sigmoid(yn)
        ya = _pool_linear(pooled, pw_ref, pb_ref[...]) * ps_ref[...]
        ycat = jnp.concatenate([ya, yb], axis=1).astype(BF16)
        ycat_ref[...] = ycat
        xo_ref[...] = x_ref[...] + _dot(ycat, wout_v[...])

    return _call(
        body, name=name, grid=(t // tm,),
        in_specs=[_row_spec(tm, d), _row_spec(tm, hc), _prev_halo_spec(tm, hc), _const_spec(pool_w.shape), _const_spec((1, POOL_CH)),
                  _const_spec((1, POOL_CH)), _const_spec(conv_w.shape), _const_spec((1, CONV_CH)), _const_spec((1, CONV_CH)),
                  _const_spec((1, CONV_CH)), HBM_SPEC],
        out_specs=[_row_spec(tm, d), _row_spec(tm, d), _row_spec(tm, CONV_CH)],
        out_shape=[jax.ShapeDtypeStruct((t, d), F32), jax.ShapeDtypeStruct((t, d), BF16), jax.ShapeDtypeStruct((t, CONV_CH), F32)],
        scratch_shapes=[pltpu.VMEM((d, d), BF16), pltpu.SemaphoreType.DMA((N_CHIPS,))],
        args=(x, h0, h0, pool_w, pool_b, pool_scale, conv_w, conv_b, ln_g, ln_b, wout_g), comms=comms)


def _mix0_bwd_a(dy, h0, y_conv, pool_w, pool_b, pool_scale, conv_w, conv_b, ln_g, ln_b, wout_g, name):
    t, d = dy.shape
    tm = min(512, t)
    hc = h0.shape[1]
    n_small = 40

    def body(dy_ref, h_ref, hp_ref, y_ref, pw_ref, pb_ref, ps_ref, cw_ref, cb_ref, lg_ref, lb_ref, wout_hbm,
             dconv_ref, dpc_ref, dpw_ref, small_ref, wout_v, sem):
        _load_at_first_step(_out_weight_copies(wout_hbm, wout_v, sem))
        i = pl.program_id(0)

        @pl.when(i == 0)
        def _():
            dpw_ref[...] = jnp.zeros_like(dpw_ref)
            small_ref[...] = jnp.zeros_like(small_ref)

        g_w, y, pooled, cnt = _mix0_recompute(i, tm, h_ref[...], hp_ref[...], cw_ref[...], cb_ref[...], y_ref[...])
        yhat, rs = _ln_fwd(y)
        lg = lg_ref[...]
        yn = yhat * lg + lb_ref[...]
        mixed = _pool_linear(pooled, pw_ref, pb_ref[...])
        dycat = _dot_nt(dy_ref[...].astype(BF16), wout_v[...])
        dya, dyb = dycat[:, 0:POOL_CH], dycat[:, POOL_CH:]
        sg = _sigmoid(yn)
        dyn = dyb * (sg * (1.0 + yn * (1.0 - sg)))
        dyc = _ln_bwd(dyn * lg, yhat, rs)
        dconv_ref[...] = dyc

        def add_row(k, value):
            small_ref[k:k + 1, :] += jnp.sum(value, axis=0, keepdims=True)

        for ch in range(CONV_CH // POOL_GC):
            lanes = slice(ch * POOL_GC, (ch + 1) * POOL_GC)
            taps = [jnp.zeros((1, POOL_GC), F32)] * CONV_WIDTH
            for r in range(tm // TAP_ROWS):
                win = g_w[r * TAP_ROWS:(r + 1) * TAP_ROWS + HALO, lanes]
                d = dyc[r * TAP_ROWS:(r + 1) * TAP_ROWS, lanes]
                for k in range(CONV_WIDTH):
                    taps[k] = taps[k] + jnp.sum(d * _shift_down(win, CONV_WIDTH - 1 - k)[HALO:, :], axis=0, keepdims=True)
            for k in range(CONV_WIDTH):
                small_ref[k:k + 1, lanes] += taps[k]
        add_row(32, dyc)
        add_row(33, dyn * yhat)
        add_row(34, dyn)
        scale = ps_ref[...]
        dmixed = dya * scale
        add_row(35, dya * mixed)
        add_row(36, dmixed)
        dmb = dmixed.astype(BF16)
        dpooled = []
        for k in range(len(POOL_WINDOWS)):
            sl = slice(k * POOL_GC, (k + 1) * POOL_GC)
            dpw_ref[k] += _dot_tn(pooled[:, sl].astype(BF16), dmb[:, sl])
            dpooled.append(_dot_nt(dmb[:, sl], pw_ref[k].astype(BF16)))
        dpc_ref[...] = jnp.concatenate(dpooled, axis=1) / cnt

    return _call(
        body, name=name, grid=(t // tm,),
        in_specs=[_row_spec(tm, d), _row_spec(tm, hc), _prev_halo_spec(tm, hc), _row_spec(tm, CONV_CH), _const_spec(pool_w.shape),
                  _const_spec((1, POOL_CH)), _const_spec((1, POOL_CH)), _const_spec(conv_w.shape), _const_spec((1, CONV_CH)),
                  _const_spec((1, CONV_CH)), _const_spec((1, CONV_CH)), HBM_SPEC],
        out_specs=[_row_spec(tm, CONV_CH), _row_spec(tm, POOL_CH), _const_spec(pool_w.shape), _const_spec((n_small, CONV_CH))],
        out_shape=[jax.ShapeDtypeStruct((t, CONV_CH), F32), jax.ShapeDtypeStruct((t, POOL_CH), F32),
                   jax.ShapeDtypeStruct(pool_w.shape, F32), jax.ShapeDtypeStruct((n_small, CONV_CH), F32)],
        scratch_shapes=[pltpu.VMEM((d, d), BF16), pltpu.SemaphoreType.DMA((N_CHIPS,))],
        args=(dy, h0, h0, y_conv, pool_w, pool_b, pool_scale, conv_w, conv_b, ln_g, ln_b, wout_g))


def _mix0_bwd_b(x, dy, h0, dconv, dpc, g, conv_w, win_g, name):
    t, d = x.shape
    tm = min(512, t)
    hc = h0.shape[1]
    c = win_g.shape[-1]
    n_tiles = t // tm

    def body(x_ref, dy_ref, h_ref, dc_ref, dcn_ref, dp_ref, dpn_ref, g_ref, cw_ref, win_hbm, dx_ref, dh_ref, dg_ref, win_v, sem):
        _load_at_first_step(_in_weight_copies(win_hbm, win_v, sem))
        i = pl.program_id(0)

        @pl.when(i == 0)
        def _():
            dg_ref[...] = jnp.zeros_like(dg_ref)

        not_last = i < n_tiles - 1
        dc_w = jnp.concatenate([dc_ref[...], jnp.where(not_last, dcn_ref[...], 0.0)], axis=0)
        dp_w = jnp.concatenate([dp_ref[...], jnp.where(not_last, dpn_ref[...], 0.0)], axis=0)
        cw = cw_ref[...]
        dg = _tap_blocks(tm, lambda r, lanes: _conv_block_transposed(dc_w[r * TAP_ROWS:(r + 1) * TAP_ROWS + HALO, lanes], cw[:, lanes]))
        a2 = dp_w + _shift_up(dp_w, 1)
        a4 = a2 + _shift_up(a2, 2)
        a8 = a4 + _shift_up(a4, 4)
        a16 = a8 + _shift_up(a8, 8)
        back = _group_select([a2[0:tm], a4[0:tm], a8[0:tm], a16[0:tm]])
        du = back - dp_ref[...] * _pool_counts(i, tm)
        hv = h_ref[...]
        a = hv[:, POOL_CH:POOL_CH + CONV_CH]
        sig = _sigmoid(hv[:, POOL_CH + CONV_CH:])
        dh = jnp.concatenate([du, dg * sig, dg * a * sig * (1.0 - sig)], axis=1).astype(BF16)
        dh_ref[...] = dh
        dxn = _proj_in_bwd_tail(dh, win_v, c)
        xh, r = _rms_fwd(x_ref[...])
        dg_ref[...] += jnp.sum(dxn * xh, axis=0, keepdims=True)
        dx_ref[...] = dy_ref[...] + _rms_bwd(dxn, xh, r, g_ref[...])

    return _call(
        body, name=name, grid=(n_tiles,),
        in_specs=[_row_spec(tm, d), _row_spec(tm, d), _row_spec(tm, hc), _row_spec(tm, CONV_CH), _next_halo_spec(tm, CONV_CH, t),
                  _row_spec(tm, POOL_CH), _next_halo_spec(tm, POOL_CH, t), _const_spec((1, d)), _const_spec(conv_w.shape), HBM_SPEC],
        out_specs=[_row_spec(tm, d), _row_spec(tm, hc), _const_spec((1, d))],
        out_shape=[jax.ShapeDtypeStruct((t, d), F32), jax.ShapeDtypeStruct((t, hc), BF16), jax.ShapeDtypeStruct((1, d), F32)],
        scratch_shapes=[pltpu.VMEM((N_CHIPS, d, c), BF16), pltpu.SemaphoreType.DMA((N_CHIPS,))],
        args=(x, dy, h0, dconv, dconv, dpc, dpc, g, conv_w, win_g))


SQRT_HALF = 0.7071067811865476
INV_SQRT_2PI = 0.3989422804014327


def _causal_mask():
    return (lax.broadcasted_iota(jnp.int32, (CHUNK, CHUNK), 1) <= lax.broadcasted_iota(jnp.int32, (CHUNK, CHUNK), 0)).astype(F32)


def _sgu_recompute(pre, lg, lb):
    half = pre.shape[1] // 2
    phi = 0.5 * (1.0 + lax.erf(pre * SQRT_HALF))
    z = pre * phi
    u, v = z[:, 0:half], z[:, half:]
    vhat, rs = _ln_fwd(v)
    return u, vhat, rs, vhat * lg + lb, phi


def _sgu_spatial(vln, w_ref, bt, tm):
    mask = _causal_mask()
    wm = [(w_ref[hd] * mask).astype(BF16) for hd in range(SGU_HEADS)]
    vb = vln.astype(BF16)
    rows = []
    for ch in range(tm // CHUNK):
        blocks = [_dot(wm[hd], vb[ch * CHUNK:(ch + 1) * CHUNK, hd * CHUNK:(hd + 1) * CHUNK]) + bt[:, hd:hd + 1] for hd in range(SGU_HEADS)]
        rows.append(jnp.concatenate(blocks, axis=1))
    return jnp.concatenate(rows, axis=0), wm


def _sgu_fwd(x, pre, ln_g, ln_b, w, bt, wout_g, name):
    t, d = x.shape
    tm = min(512, t)
    pc = pre.shape[1]

    def body(x_ref, pre_ref, lg_ref, lb_ref, w_ref, bt_ref, wout_hbm, xo_ref, p_ref, wout_v, sem):
        _load_at_first_step(_out_weight_copies(wout_hbm, wout_v, sem))
        u, _, _, vln, _ = _sgu_recompute(pre_ref[...], lg_ref[...], lb_ref[...])
        vo, _ = _sgu_spatial(vln, w_ref, bt_ref[...], tm)
        p = (u * vo).astype(BF16)
        p_ref[...] = p
        xo_ref[...] = x_ref[...] + _dot(p, wout_v[...])

    return _call(
        body, name=name, grid=(t // tm,),
        in_specs=[_row_spec(tm, d), _row_spec(tm, pc), _const_spec((1, d)), _const_spec((1, d)), _const_spec(w.shape),
                  _const_spec(bt.shape), HBM_SPEC],
        out_specs=[_row_spec(tm, d), _row_spec(tm, d)],
        out_shape=[jax.ShapeDtypeStruct((t, d), F32), jax.ShapeDtypeStruct((t, d), BF16)],
        scratch_shapes=[pltpu.VMEM((d, d), BF16), pltpu.SemaphoreType.DMA((N_CHIPS,))],
        args=(x, pre, ln_g, ln_b, w, bt, wout_g))


def _sgu_bwd(x, dy, pre, g, ln_g, ln_b, w, bt, win_g, wout_g, name, comms=()):
    t, d = x.shape
    tm = min(512, t)
    pc = pre.shape[1]
    c = win_g.shape[-1]

    def body(x_ref, dy_ref, pre_ref, g_ref, lg_ref, lb_ref, w_ref, bt_ref, win_hbm, wout_hbm,
             dx_ref, dpre_ref, dg_ref, dlg_ref, dlb_ref, dw_ref, dbt_ref, win_v, wout_v, sem):
        _load_at_first_step(_in_weight_copies(win_hbm, win_v, sem) + _out_weight_copies(wout_hbm, wout_v, sem, N_CHIPS))
        i = pl.program_id(0)

        @pl.when(i == 0)
        def _():
            for ref in (dg_ref, dlg_ref, dlb_ref, dw_ref, dbt_ref):
                ref[...] = jnp.zeros_like(ref)

        prev = pre_ref[...]
        lg = lg_ref[...]
        u, vhat, rs, vln, phi = _sgu_recompute(prev, lg, lb_ref[...])
        vo, wm = _sgu_spatial(vln, w_ref, bt_ref[...], tm)
        dp = _dot_nt(dy_ref[...].astype(BF16), wout_v[...])
        du = dp * vo
        dvo = dp * u
        dvob = dvo.astype(BF16)
        vb = vln.astype(BF16)
        head_lane = lax.broadcasted_iota(jnp.int32, (CHUNK, SGU_HEADS), 1)
        dbt = jnp.zeros((CHUNK, SGU_HEADS), F32)
        dw = [jnp.zeros((CHUNK, CHUNK), F32) for _ in range(SGU_HEADS)]
        rows = []
        for ch in range(tm // CHUNK):
            rs_ = slice(ch * CHUNK, (ch + 1) * CHUNK)
            blocks = []
            for hd in range(SGU_HEADS):
                cs = slice(hd * CHUNK, (hd + 1) * CHUNK)
                dbt = dbt + jnp.where(head_lane == hd, jnp.sum(dvo[rs_, cs], axis=1, keepdims=True), 0.0)
                dw[hd] = dw[hd] + _dot_nt(dvob[rs_, cs], vb[rs_, cs])
                blocks.append(_dot_tn(wm[hd], dvob[rs_, cs]))
            rows.append(jnp.concatenate(blocks, axis=1))
        dvln = jnp.concatenate(rows, axis=0)
        mask = _causal_mask()
        for hd in range(SGU_HEADS):
            dw_ref[hd] += dw[hd] * mask
        dbt_ref[...] += dbt
        dlg_ref[...] += jnp.sum(dvln * vhat, axis=0, keepdims=True)
        dlb_ref[...] += jnp.sum(dvln, axis=0, keepdims=True)
        dv = _ln_bwd(dvln * lg, vhat, rs)
        gelu_grad = phi + prev * jnp.exp(-0.5 * prev * prev) * INV_SQRT_2PI
        dpre = (jnp.concatenate([du, dv], axis=1) * gelu_grad).astype(BF16)
        dpre_ref[...] = dpre
        dxn = _proj_in_bwd_tail(dpre, win_v, c)
        xh, r = _rms_fwd(x_ref[...])
        dg_ref[...] += jnp.sum(dxn * xh, axis=0, keepdims=True)
        dx_ref[...] = dy_ref[...] + _rms_bwd(dxn, xh, r, g_ref[...])

    return _call(
        body, name=name, grid=(t // tm,),
        in_specs=[_row_spec(tm, d), _row_spec(tm, d), _row_spec(tm, pc), _const_spec((1, d)), _const_spec((1, d)), _const_spec((1, d)),
                  _const_spec(w.shape), _const_spec(bt.shape), HBM_SPEC, HBM_SPEC],
        out_specs=[_row_spec(tm, d), _row_spec(tm, pc), _const_spec((1, d)), _const_spec((1, d)), _const_spec((1, d)),
                   _const_spec(w.shape), _const_spec(bt.shape)],
        out_shape=[jax.ShapeDtypeStruct((t, d), F32), jax.ShapeDtypeStruct((t, pc), BF16), jax.ShapeDtypeStruct((1, d), F32),
                   jax.ShapeDtypeStruct((1, d), F32), jax.ShapeDtypeStruct((1, d), F32), jax.ShapeDtypeStruct(w.shape, F32),
                   jax.ShapeDtypeStruct(bt.shape, F32)],
        scratch_shapes=[pltpu.VMEM((N_CHIPS, d, c), BF16), pltpu.VMEM((d, d), BF16), pltpu.SemaphoreType.DMA((2 * N_CHIPS,))],
        args=(x, dy, pre, g, ln_g, ln_b, w, bt, win_g, wout_g), comms=comms)


def _tn_matmul(a, b, scale, bm, bn, name, comms=(), after=()):
    t, m = a.shape
    n = b.shape[1]
    tk = min(2048, t)
    bm, bn = min(bm, m), min(bn, n)
    nk = t // tk

    def body(a_ref, b_ref, o_ref, acc_ref):
        k = pl.program_id(2)

        @pl.when(k == 0)
        def _():
            acc_ref[...] = jnp.zeros_like(acc_ref)

        bv = b_ref[...]
        if bv.dtype != BF16:
            bv = (scale * bv).astype(BF16)
        acc_ref[...] += _dot_tn(a_ref[...], bv)

        @pl.when(k == nk - 1)
        def _():
            o_ref[...] = acc_ref[...].astype(BF16)

    return _call(
        body, name=name, grid=(m // bm, n // bn, nk),
        in_specs=[pl.BlockSpec((tk, bm), lambda i, j, k: (k, i)), pl.BlockSpec((tk, bn), lambda i, j, k: (k, j))],
        out_specs=[pl.BlockSpec((bm, bn), lambda i, j, k: (i, j))],
        out_shape=[jax.ShapeDtypeStruct((m, n), BF16)],
        scratch_shapes=[pltpu.VMEM((bm, bn), F32)],
        args=(a, b), comms=comms, after=after)[0]


def _row_tile(rows, cols, budget_bytes=2 * 1024 * 1024):
    best = None
    for cand in range(16, rows + 1, 16):
        if rows % cand == 0 and cand * cols * 4 <= budget_bytes:
            best = cand
    return best or rows


def _scalar_grid(grid, in_specs, out_specs):
    return pltpu.PrefetchScalarGridSpec(num_scalar_prefetch=1, grid=grid, in_specs=in_specs, out_specs=out_specs)


def _cast_into_slot(w, layer, me, name, after=()):
    _, rows, cols = w.shape
    tr = _row_tile(rows, cols)

    def body(me_ref, w_ref, *rest):
        rest[-1][...] = w_ref[...].astype(BF16)

    return pl.pallas_call(
        body, name=name,
        grid_spec=_scalar_grid((rows // tr,), [pl.BlockSpec((None, tr, cols), lambda i, me: (layer, i, 0))]
                               + [pl.BlockSpec(memory_space=pl.ANY)] * len(after),
                               pl.BlockSpec((None, tr, cols), lambda i, me: (me[0], i, 0))),
        out_shape=jax.ShapeDtypeStruct((N_CHIPS, rows, cols), BF16), compiler_params=_cparams())(me, w, *after)


def _add_half(view, other, core, name):
    q, _, r, c = view.shape
    tr = _row_tile(r, c)

    def body(core_ref, a_ref, b_ref, o_ref):
        o_ref[...] = (a_ref[...].astype(F32) + b_ref[...].astype(F32)).astype(BF16)

    return pl.pallas_call(
        body, name=name,
        grid_spec=_scalar_grid((q, r // tr), [pl.BlockSpec((None, None, tr, c), lambda k, i, core: (k, core[0], i, 0)),
                                             pl.BlockSpec((None, tr, c), lambda k, i, core: (k, i, 0))],
                               pl.BlockSpec((None, tr, c), lambda k, i, core: (k, i, 0))),
        out_shape=jax.ShapeDtypeStruct((q, r, c), BF16), compiler_params=_cparams_nd(2))(core, view, other)


def _reduce_piece(partial, staged, me, column_sharded, name):
    _, r, c = staged.shape
    tr = _row_tile(r, c, budget_bytes=1024 * 1024)
    nt = r // tr
    if column_sharded:
        own2d = partial.reshape(r, N_CHIPS * c)
        own_spec = pl.BlockSpec((tr, c), lambda i, me: (i, me[0]))
    else:
        own2d = partial.reshape(N_CHIPS * r, c)
        own_spec = pl.BlockSpec((tr, c), lambda i, me: (me[0] * nt + i, 0))
    ring = [pl.BlockSpec((None, tr, c), lambda i, me, k=k: ((me[0] + k) % N_CHIPS, i, 0)) for k in (1, 2, 3)]

    def body(me_ref, own_ref, s1_ref, s2_ref, s3_ref, o_ref):
        o_ref[...] = ((own_ref[...].astype(F32) + s1_ref[...].astype(F32)) + s2_ref[...].astype(F32)) + s3_ref[...].astype(F32)

    return pl.pallas_call(
        body, name=name, grid_spec=_scalar_grid((nt,), [own_spec] + ring, pl.BlockSpec((tr, c), lambda i, me: (i, 0))),
        out_shape=jax.ShapeDtypeStruct((r, c), F32), compiler_params=_cparams())(me, own2d, staged, staged, staged)


def _sum_leading(s, name):
    n, rows, cols = s.shape
    tr = _row_tile(rows, cols, budget_bytes=1024 * 1024)

    def body(s_ref, o_ref):
        acc = s_ref[0].astype(F32)
        for k in range(1, n):
            acc = acc + s_ref[k].astype(F32)
        o_ref[...] = acc

    return pl.pallas_call(
        body, name=name, grid=(rows // tr,), in_specs=[pl.BlockSpec((n, tr, cols), lambda i: (0, i, 0))], out_specs=_row_spec(tr, cols),
        out_shape=jax.ShapeDtypeStruct((rows, cols), F32), compiler_params=_cparams())(s)


ADAM_C1 = 1.0 / (1.0 - ADAM_B1 ** ADAM_STEP)
ADAM_C2 = 1.0 / (1.0 - ADAM_B2 ** ADAM_STEP)


def _adamw_math(w, g, m, v):
    mn = ADAM_B1 * m + (1.0 - ADAM_B1) * g
    vn = ADAM_B2 * v + (1.0 - ADAM_B2) * (g * g)
    return -ADAM_LR * ((mn * ADAM_C1) / (jnp.sqrt(vn * ADAM_C2) + ADAM_EPS) + ADAM_WD * w), mn, vn


def _adamw(w, g, m, v, name):
    shape = w.shape
    cols = shape[-1] if w.ndim > 1 else 128
    w2, g2, m2, v2 = (a.reshape(-1, cols) for a in (w, g, m, v))
    rows = w2.shape[0]
    tr = _row_tile(rows, cols, budget_bytes=1024 * 1024)

    def body(w_ref, g_ref, m_ref, v_ref, d_ref, mo_ref, vo_ref):
        d_ref[...], mo_ref[...], vo_ref[...] = _adamw_math(w_ref[...], g_ref[...], m_ref[...], v_ref[...])

    spec = _row_spec(tr, cols)
    outs = pl.pallas_call(
        body, name=name, grid=(rows // tr,), in_specs=[spec] * 4, out_specs=[spec] * 3,
        out_shape=[jax.ShapeDtypeStruct((rows, cols), F32)] * 3, compiler_params=_cparams())(w2, g2, m2, v2)
    return tuple(o.reshape(shape) for o in outs)


def _adamw_sharded(w, g_mine, g_sibling, m, v, core, layer, prev, name, comms=(), after=()):
    n_layers, r, c = w.shape
    half = r // 2
    tr = _row_tile(half, c)
    nt = half // tr

    def body(core_ref, w_ref, gm_ref, gs_ref, m_ref, v_ref, *rest):
        g_ref, d_ref, mo_ref, vo_ref = rest[-4:]
        gv = jnp.where(pl.program_id(0) == core_ref[0], gm_ref[...], gs_ref[...])
        g_ref[...] = gv
        d_ref[...], mo_ref[...], vo_ref[...] = _adamw_math(w_ref[...], gv, m_ref[...], v_ref[...])

    full = pl.BlockSpec((None, tr, c), lambda h, i, core: (layer, h * nt + i, 0))
    part = pl.BlockSpec((tr, c), lambda h, i, core: (i, 0))
    args = [w, g_mine, g_sibling, m, v]
    in_specs = [full, part, part, full, full]
    aliases = {}
    if prev is not None:
        aliases = {len(args) + k: k for k in range(4)}
        args += list(prev)
        in_specs += [pl.BlockSpec(memory_space=pl.ANY)] * 4
    args += list(after)
    in_specs += [pl.BlockSpec(memory_space=pl.ANY)] * len(after)
    return _call(body, name=name, grid=(2, nt), in_specs=in_specs, out_specs=[full] * 4, out_shape=[jax.ShapeDtypeStruct(w.shape, F32)] * 4,
                 args=args, comms=comms, scalar=core, aliases=aliases)


BIG_IN = ("ffn1_w_in", "ffn2_w_in", "ab_w_in", "sgu_w_in")
BIG_OUT = ("ffn1_w_out", "ffn2_w_out", "ab_w_out", "sgu_w_out")
BIG = BIG_IN + BIG_OUT


class _Gatherer:
    def __init__(self, slots):
        self.slots = dict(slots)

    def _stage(self, keys, d2d):
        n = len(keys)

        def plan(ins, outs, place):
            x, y, c = place
            me = 2 * x + y
            remote = []
            for a in range(n):
                rows = ins[a].shape[1] // 2

                def half(ref, q, core, rows=rows):
                    return ref.at[q, pl.ds(core * rows, rows), :]

                for (px, py) in _other_chips(x, y):
                    q = 2 * px + py
                    if d2d:
                        remote.append((half(ins[a], q, c), half(outs[a], q, c), (x, y, 1 - c), half(outs[a], q, 1 - c)))
                    else:
                        remote.append((half(ins[a], me, c), half(outs[a], me, c), (px, py, c), half(outs[a], q, c)))
            return remote

        def finish(outs):
            for k, o in zip(keys, outs):
                self.slots[k] = o

        arrays = [self.slots[k] for k in keys]
        return _Exchange(arrays, [_sds(a) for a in arrays], plan, 3 * n, {a: a for a in range(n)}, finish)

    def direct(self, keys):
        n = len(keys)

        def plan(ins, outs, place):
            x, y, c = place
            me = 2 * x + y
            return [(ins[a].at[me], outs[a].at[me], (px, py, c), outs[a].at[2 * px + py]) for a in range(n) for (px, py) in _other_chips(x, y)]

        def finish(outs):
            for k, o in zip(keys, outs):
                self.slots[k] = o

        arrays = [self.slots[k] for k in keys]
        return _Exchange(arrays, [_sds(a) for a in arrays], plan, 3 * n, {a: a for a in range(n)}, finish)

    def ici(self, keys):
        return self._stage(keys, False)

    def d2d(self, keys):
        return self._stage(keys, True)


class _Reducer:
    def __init__(self, me, core):
        self.me, self.core = me, core
        self.views, self.partial, self.staged, self.mine, self.theirs = {}, {}, {}, {}, {}

    def add(self, key, g):
        m, n = g.shape
        if key[0] in BIG_IN:
            self.views[key] = g.reshape(1, 2, m // 2, n)
        else:
            self.views[key] = g.reshape(N_CHIPS, 2, m // (2 * N_CHIPS), n)

    def swap(self, keys):
        views = [self.views[k] for k in keys]

        def plan(ins, outs, place):
            x, y, c = place
            return [(ins[a].at[:, 1 - c], outs[a], (x, y, 1 - c), outs[a]) for a in range(len(keys))]

        def finish(outs):
            for k, v, o in zip(keys, views, outs):
                self.partial[k] = _add_half(v, o, self.core, f"chip_partial_{k[0]}_{k[1]}")

        shapes = [jax.ShapeDtypeStruct((v.shape[0],) + v.shape[2:], v.dtype) for v in views]
        return _Exchange(views, shapes, plan, len(keys), None, finish)

    def scatter(self, keys, part=(0, 1)):
        i, n = part
        n_keys = len(keys)
        parts = [self.partial[k] for k in keys]
        shapes = []
        for k, p in zip(keys, parts):
            q, r, c = p.shape
            shapes.append(jax.ShapeDtypeStruct((N_CHIPS, r, c // N_CHIPS if k[0] in BIG_IN else c), p.dtype))

        def piece(ref, key, q, rows, cols):
            return ref.at[0, rows, pl.ds(q * cols, cols)] if key[0] in BIG_IN else ref.at[q, rows, :]

        def plan(ins, outs, place):
            x, y, c = place
            me = 2 * x + y
            remote = []
            for a, k in enumerate(keys):
                _, r, cols = shapes[a].shape
                rows = pl.ds(i * (r // n), r // n)
                for (px, py) in _other_chips(x, y):
                    q = 2 * px + py
                    remote.append((piece(ins[a], k, q, rows, cols), outs[a].at[me, rows, :], (px, py, c), outs[a].at[q, rows, :]))
            return remote

        def finish(outs):
            for k, p, o in zip(keys, parts, outs):
                self.staged[k] = o
                if i == n - 1:
                    self.mine[k] = _reduce_piece(p, o, self.me, k[0] in BIG_IN, f"reduce_{k[0]}_{k[1]}")

        inputs, aliases = parts, None
        if i > 0:
            inputs = parts + [self.staged[k] for k in keys]
            aliases = {n_keys + a: a for a in range(n_keys)}
        return _Exchange(inputs, shapes, plan, 3 * n_keys, aliases, finish)

    def scatter_behind(self, keys, work, tag):
        n = len(keys)
        parts = [self.partial[k] for k in keys]
        lands = []
        for k, p in zip(keys, parts):
            _, r, c = p.shape
            lands.append(jax.ShapeDtypeStruct((N_CHIPS, r, c // N_CHIPS if k[0] in BIG_IN else c), p.dtype))
        sem_spec = pl.BlockSpec(memory_space=pltpu.SEMAPHORE)
        effect = pltpu.CompilerParams(has_side_effects=pltpu.SideEffectType.DATAFLOW_SIDE_EFFECTING)

        def copies(part_refs, land_refs, send_sems, recv_sems):
            x, y, c = _my_place()
            me = 2 * x + y
            out = []
            for a, k in enumerate(keys):
                cols = lands[a].shape[2]
                for j, (px, py) in enumerate(_other_chips(x, y)):
                    q = 2 * px + py
                    src = part_refs[a].at[0, :, pl.ds(q * cols, cols)] if k[0] in BIG_IN else part_refs[a].at[q]
                    sems = dict(send_sem=send_sems.at[3 * a + j], recv_sem=recv_sems.at[3 * a + j], device_id=(px, py, c), device_id_type=MESH)
                    out.append((pltpu.make_async_remote_copy(src_ref=src, dst_ref=land_refs[a].at[me], **sems),
                                pltpu.make_async_remote_copy(src_ref=src, dst_ref=land_refs[a].at[q], **sems)))
            return out

        def start_body(*refs):
            part_refs, land_refs, send_sems, recv_sems, token = refs[:n], refs[n:2 * n], refs[2 * n], refs[2 * n + 1], refs[-1]
            for send, _ in copies(part_refs, land_refs, send_sems, recv_sems):
                send.start()
            token[...] = jnp.zeros_like(token)

        def wait_body(*refs):
            part_refs, land_refs, send_sems, recv_sems = refs[:n], refs[n:2 * n], refs[2 * n], refs[2 * n + 1]
            for send, arrive in copies(part_refs, land_refs, send_sems, recv_sems):
                send.wait_send()
                arrive.wait_recv()

        in_hbm = [pltpu.with_memory_space_constraint(p, pltpu.HBM) for p in parts]
        in_hbm += [pltpu.with_memory_space_constraint(lax.empty(s.shape, s.dtype), pltpu.HBM) for s in lands]
        thru_shapes = [pltpu.HBM(p.shape, p.dtype) for p in parts] + [pltpu.HBM(s.shape, s.dtype) for s in lands]
        started = pl.pallas_call(
            start_body, name=f"scatter_{tag}_start", in_specs=[HBM_SPEC] * (2 * n),
            out_shape=(pltpu.SemaphoreType.DMA((3 * n,)), pltpu.SemaphoreType.DMA((3 * n,)), *thru_shapes, jax.ShapeDtypeStruct((8, 128), F32)),
            out_specs=(sem_spec, sem_spec, *[HBM_SPEC] * (2 * n), pl.BlockSpec(memory_space=pltpu.VMEM)),
            input_output_aliases={i: 2 + i for i in range(2 * n)}, compiler_params=effect)(*in_hbm)
        send_sems, recv_sems, thru, token = started[0], started[1], started[2:2 + 2 * n], started[-1]
        after = work(token)
        done = pl.pallas_call(
            wait_body, name=f"scatter_{tag}_wait", in_specs=[HBM_SPEC] * (2 * n) + [sem_spec, sem_spec] + [pl.BlockSpec(memory_space=pl.ANY)] * len(after),
            out_shape=tuple(thru_shapes), out_specs=tuple([HBM_SPEC] * (2 * n)), input_output_aliases={i: i for i in range(2 * n)},
            compiler_params=effect)(*thru, send_sems, recv_sems, *after)
        for a, k in enumerate(keys):
            self.staged[k] = done[n + a]
            self.mine[k] = _reduce_piece(done[a], done[n + a], self.me, k[0] in BIG_IN, f"reduce_{k[0]}_{k[1]}")

    def exchange(self, keys):
        mine = [self.mine[k] for k in keys]

        def plan(ins, outs, place):
            x, y, c = place
            return [(ins[a], outs[a], (x, y, 1 - c), outs[a]) for a in range(len(keys))]

        def finish(outs):
            for k, o in zip(keys, outs):
                self.theirs[k] = o

        return _Exchange(mine, [_sds(a) for a in mine], plan, len(keys), None, finish)


def _half(ref, q, core):
    rows = ref.shape[1] // 2
    return ref.at[q, pl.ds(core * rows, rows), :]


def _all_gather_full(gat, keys, work):
    n = len(keys)
    arrays = [gat.slots[k] for k in keys]
    sem_spec = pl.BlockSpec(memory_space=pltpu.SEMAPHORE)
    effect = pltpu.CompilerParams(has_side_effects=pltpu.SideEffectType.DATAFLOW_SIDE_EFFECTING)

    def places():
        x, y, c = _my_place()
        return c, (x, y, 1 - c), (1 - x, y, c), (x, 1 - y, c), 2 * x + y, 2 * (1 - x) + y, 2 * x + (1 - y), 2 * (1 - x) + (1 - y)

    def first_hop(refs, send_sems, recv_sems):
        c, _, x_nbr, y_nbr, me, qx, qy, _ = places()
        out = []
        for a in range(n):
            for k, (to, q) in enumerate(((x_nbr, qx), (y_nbr, qy))):
                sems = dict(send_sem=send_sems.at[2 * a + k], recv_sem=recv_sems.at[2 * a + k], device_id=to, device_id_type=MESH)
                out.append((pltpu.make_async_remote_copy(src_ref=_half(refs[a], me, c), dst_ref=_half(refs[a], me, c), **sems),
                            pltpu.make_async_remote_copy(src_ref=_half(refs[a], me, c), dst_ref=_half(refs[a], q, c), **sems)))
        return out

    def start_body(*refs):
        for send, _ in first_hop(refs[:n], refs[n], refs[n + 1]):
            send.start()
        refs[-1][...] = jnp.zeros_like(refs[-1])

    def wait_body(*refs):
        for send, arrive in first_hop(refs[:n], refs[n], refs[n + 1]):
            send.wait_send()
            arrive.wait_recv()

    thru_shapes = [pltpu.HBM(a.shape, a.dtype) for a in arrays]
    started = pl.pallas_call(
        start_body, name="gather_first_hop_start", in_specs=[HBM_SPEC] * n,
        out_shape=(pltpu.SemaphoreType.DMA((2 * n,)), pltpu.SemaphoreType.DMA((2 * n,)), *thru_shapes, jax.ShapeDtypeStruct((8, 128), F32)),
        out_specs=(sem_spec, sem_spec, *[HBM_SPEC] * n, pl.BlockSpec(memory_space=pltpu.VMEM)),
        input_output_aliases={i: 2 + i for i in range(n)}, compiler_params=effect,
    )(*[pltpu.with_memory_space_constraint(a, pltpu.HBM) for a in arrays])
    after = work(started[-1])
    landed_first = pl.pallas_call(
        wait_body, name="gather_first_hop_wait", in_specs=[HBM_SPEC] * n + [sem_spec, sem_spec] + [pl.BlockSpec(memory_space=pl.ANY)] * len(after),
        out_shape=tuple(thru_shapes), out_specs=tuple([HBM_SPEC] * n), input_output_aliases={i: i for i in range(n)},
        compiler_params=effect)(*started[2:2 + n], started[0], started[1], *after)
    per = 5

    def body(*refs):
        outs = refs[n:2 * n]
        send_sems, recv_sems = refs[2 * n:]
        c, sibling, x_nbr, y_nbr, _, qx, qy, qd = places()

        def quarter(ref, q, core, k):
            rows = ref.shape[1] // 4
            return ref.at[q, pl.ds((2 * core + k) * rows, rows), :]

        def copy(a, k, part, to):
            return pltpu.make_async_remote_copy(src_ref=part, dst_ref=part, send_sem=send_sems.at[per * a + k],
                                                recv_sem=recv_sems.at[per * a + k], device_id=to, device_id_type=MESH)

        sent = []

        def send(a, k, part, to):
            cp = copy(a, k, part, to)
            cp.start()
            sent.append(cp)

        for a in range(n):
            send(a, 0, quarter(outs[a], qy, c, 0), x_nbr)
            send(a, 1, quarter(outs[a], qx, c, 1), y_nbr)
            send(a, 2, _half(outs[a], qx, c), sibling)
            send(a, 3, _half(outs[a], qy, c), sibling)
        for a in range(n):
            copy(a, 0, quarter(outs[a], qd, c, 0), sibling).wait_recv()
            copy(a, 1, quarter(outs[a], qd, c, 1), sibling).wait_recv()
            send(a, 4, _half(outs[a], qd, c), sibling)
        for a in range(n):
            for k, q in ((2, qx), (3, qy), (4, qd)):
                copy(a, k, _half(outs[a], q, 1 - c), sibling).wait_recv()
        for cp in sent:
            cp.wait_send()

    outs = pl.pallas_call(
        body, name="gather_first_rest", in_specs=[HBM_SPEC] * n, out_specs=[HBM_SPEC] * n,
        out_shape=[_sds(a) for a in arrays], input_output_aliases={a: a for a in range(n)},
        scratch_shapes=[pltpu.SemaphoreType.DMA((per * n,)), pltpu.SemaphoreType.DMA((per * n,))])(*landed_first)
    for k, o in zip(keys, outs):
        gat.slots[k] = o


def _small_all_gather(buf, done):
    state = {}

    def index(x, y, c):
        return 4 * x + 2 * y + c

    def plan_ici(ins, outs, place):
        x, y, c = place
        return [(ins[0], outs[0].at[index(x, y, c)], (px, py, c), outs[0].at[index(px, py, c)]) for (px, py) in _other_chips(x, y)]

    def local(ins, outs, place):
        return [(ins[0], outs[0].at[index(*place)])]

    def plan_d2d(ins, outs, place):
        x, y, c = place
        return [(ins[0].at[index(px, py, c)], outs[0].at[index(px, py, c)], (x, y, 1 - c), outs[0].at[index(px, py, 1 - c)])
                for (px, py) in [(x, y)] + _other_chips(x, y)]

    def second():
        return _Exchange([state["blocks"]], [_sds(state["blocks"])], plan_d2d, N_CHIPS, {0: 0}, lambda outs: done(outs[0]))

    first = _Exchange([buf], [jax.ShapeDtypeStruct((2 * N_CHIPS,) + buf.shape, buf.dtype)], plan_ici, 3, None,
                      lambda outs: state.update(blocks=outs[0]), local, 1)
    return first, second


WEIGHT_NAMES = ("ffn1_norm", "ffn1_w_in", "ffn1_w_out", "mix_norm", "ffn2_norm", "ffn2_w_in", "ffn2_w_out", "ab_w_in", "pool_w", "pool_b",
                "pool_scale", "conv_w", "conv_b", "conv_ln_g", "conv_ln_b", "ab_w_out", "sgu_w_in", "sgu_ln_g", "sgu_ln_b", "sgu_w", "sgu_b",
                "sgu_w_out", "final_norm")
SMALL = tuple(n for n in WEIGHT_NAMES if n not in BIG)
SHARDED_SMALL = ("conv_w", "sgu_ln_g", "sgu_ln_b")
PACK_ROWS = 64
PACK = ("pack", 0)


def _pair(prefix, layer):
    return [(prefix + "_w_in", layer), (prefix + "_w_out", layer)]


def kernel(x, ffn1_norm, ffn1_w_in, ffn1_w_out, mix_norm, ffn2_norm, ffn2_w_in, ffn2_w_out, ab_w_in, pool_w, pool_b, pool_scale, conv_w, conv_b, conv_ln_g, conv_ln_b, ab_w_out, sgu_w_in, sgu_ln_g, sgu_ln_b, sgu_w, sgu_b, sgu_w_out, final_norm, loss_target, m_ffn1_norm, m_ffn1_w_in, m_ffn1_w_out, m_mix_norm, m_ffn2_norm, m_ffn2_w_in, m_ffn2_w_out, m_ab_w_in, m_pool_w, m_pool_b, m_pool_scale, m_conv_w, m_conv_b, m_conv_ln_g, m_conv_ln_b, m_ab_w_out, m_sgu_w_in, m_sgu_ln_g, m_sgu_ln_b, m_sgu_w, m_sgu_b, m_sgu_w_out, m_final_norm, v_ffn1_norm, v_ffn1_w_in, v_ffn1_w_out, v_mix_norm, v_ffn2_norm, v_ffn2_w_in, v_ffn2_w_out, v_ab_w_in, v_pool_w, v_pool_b, v_pool_scale, v_conv_w, v_conv_b, v_conv_ln_g, v_conv_ln_b, v_ab_w_out, v_sgu_w_in, v_sgu_ln_g, v_sgu_ln_b, v_sgu_w, v_sgu_b, v_sgu_w_out, v_final_norm):
    given = dict(locals())
    w = {n: given[n] for n in WEIGHT_NAMES}
    chip = 2 * lax.axis_index("x") + lax.axis_index("y")
    me = chip.astype(jnp.int32).reshape(1)
    core = lax.axis_index("c").astype(jnp.int32).reshape(1)
    row = lambda v: v.reshape(1, -1)
    xin, tgt = x[0], loss_target[0]

    pack = jnp.concatenate([
        w["conv_w"][0], jnp.zeros((1, 128), F32), w["sgu_ln_g"].reshape(2, 128), w["sgu_ln_b"].reshape(2, 128),
        jnp.zeros((PACK_ROWS - 36, 128), F32)], axis=0)
    first = _pair("ffn1", 0)
    slots = {PACK: lax.dynamic_update_slice(jnp.zeros((N_CHIPS, PACK_ROWS, 128), F32), pack[None], (me[0], 0, 0))}
    for n, layer in first:
        slots[(n, layer)] = _cast_into_slot(w[n], layer, me, f"cast_{n}_{layer}")
    gat = _Gatherer(slots)

    def other_casts(token):
        for n in BIG:
            for layer in range(w[n].shape[0]):
                if (n, layer) not in first:
                    token = gat.slots[(n, layer)] = _cast_into_slot(w[n], layer, me, f"cast_{n}_{layer}", after=[token])
        return [token]

    _all_gather_full(gat, first + [PACK], other_casts)
    gp = gat.slots[PACK]
    conv_w_full = jnp.transpose(gp[:, 0:CONV_WIDTH], (1, 0, 2)).reshape(CONV_WIDTH, N_CHIPS * 128)
    sgu_ln_g_full = gp[:, 32:34].reshape(1, -1)
    sgu_ln_b_full = gp[:, 34:36].reshape(1, -1)
    gw = lambda n, layer: gat.slots[(n, layer)]

    st = [dict(), dict()]
    st[0]["xa"] = xin
    later = _pair("sgu", 0) + _pair("ffn2", 1)
    cur, st[0]["h1"], st[0]["xn1"] = _ffn_fwd(xin, row(w["ffn1_norm"][0]), gw("ffn1_w_in", 0), gw("ffn1_w_out", 0), "ffn1_fwd_0",
                                              comms=[gat.direct(_pair("ab", 0)), gat.ici(_pair("ffn2", 0))])
    st[0]["xb"] = cur
    st[0]["h0"], st[0]["xnm"] = _norm_matmul(cur, row(w["mix_norm"][0]), gw("ab_w_in", 0), "mix0_proj_in",
                                             comms=[gat.d2d(_pair("ffn2", 0)), gat.ici([("ffn1_w_out", 1)])])
    pool_args = (w["pool_w"][0], row(w["pool_b"][0]), row(w["pool_scale"][0]), conv_w_full, row(w["conv_b"][0]), row(w["conv_ln_g"][0]),
                 row(w["conv_ln_b"][0]), gw("ab_w_out", 0))
    cur, st[0]["ycat"], st[0]["yconv"] = _mix0_fwd(cur, st[0]["h0"], *pool_args, "mix0_fwd", comms=[gat.ici([("ffn1_w_in", 1)])])
    st[0]["xc"] = cur
    cur, st[0]["h2"], st[0]["xn2"] = _ffn_fwd(cur, row(w["ffn2_norm"][0]), gw("ffn2_w_in", 0), gw("ffn2_w_out", 0), "ffn2_fwd_0",
                                              comms=[gat.d2d(_pair("ffn1", 1)), gat.ici(later)])
    st[1]["xa"] = cur
    cur, st[1]["h1"], st[1]["xn1"] = _ffn_fwd(cur, row(w["ffn1_norm"][1]), gw("ffn1_w_in", 1), gw("ffn1_w_out", 1), "ffn1_fwd_1",
                                              comms=[gat.d2d(later)])
    st[1]["xb"] = cur
    st[1]["pre"], st[1]["xnm"] = _norm_matmul(cur, row(w["mix_norm"][1]), gw("sgu_w_in", 0), "sgu_proj_in")
    sgu_args = (sgu_ln_g_full, sgu_ln_b_full, w["sgu_w"][0], w["sgu_b"][0].T)
    cur, st[1]["p"] = _sgu_fwd(cur, st[1]["pre"], *sgu_args, gw("sgu_w_out", 0), "sgu_fwd")
    st[1]["xc"] = cur
    dy, st[1]["h2"], st[1]["xn2"], loss, d_final = _ffn_fwd(cur, row(w["ffn2_norm"][1]), gw("ffn2_w_in", 1), gw("ffn2_w_out", 1),
                                                            "ffn2_fwd_1_loss", loss_head=(tgt, row(w["final_norm"])))

    red = _Reducer(me, core)
    small = {"final_norm": d_final.reshape(-1)}
    norm_grads = {"ffn1_norm": [None] * DEPTH, "mix_norm": [None] * DEPTH, "ffn2_norm": [None] * DEPTH}
    ga, gb, gc, gd, ge, gf = _pair("ffn2", 1), _pair("sgu", 0), _pair("ffn1", 1), _pair("ffn2", 0), _pair("ab", 0), _pair("ffn1", 0)

    def ffn_backward(prefix, layer, xs, hs, xns, dy_in, bwd_comms=(), dwin_comms=(), dwout_comms=()):
        dx, dh, act, norm_grads[prefix + "_norm"][layer] = _ffn_bwd(
            xs, dy_in, hs, row(w[prefix + "_norm"][layer]), gw(prefix + "_w_in", layer), gw(prefix + "_w_out", layer),
            f"{prefix}_bwd_{layer}", comms=bwd_comms)
        red.add((prefix + "_w_in", layer), _tn_matmul(xns, dh, 1.0, 1024, 1408, f"{prefix}_dwin_{layer}", comms=dwin_comms))
        red.add((prefix + "_w_out", layer), _tn_matmul(act, dy_in, 0.5, 1408, 1024, f"{prefix}_dwout_{layer}", comms=dwout_comms))
        return dx

    s1, s0 = st[1], st[0]
    dy = ffn_backward("ffn2", 1, s1["xc"], s1["h2"], s1["xn2"], dy)
    dy_in = dy
    dy, dpre, norm_grads["mix_norm"][1], dlg, dlb, dw, dbt = _sgu_bwd(
        s1["xb"], dy_in, s1["pre"], row(w["mix_norm"][1]), *sgu_args, gw("sgu_w_in", 0), gw("sgu_w_out", 0), "sgu_bwd", comms=[red.swap(ga)])
    red.add(("sgu_w_in", 0), _tn_matmul(s1["xnm"], dpre, 1.0, 1024, 2048, "sgu_dwin"))
    red.add(("sgu_w_out", 0), _tn_matmul(s1["p"], dy_in, 1.0, 1024, 1024, "sgu_dwout"))
    small.update(sgu_ln_g=dlg, sgu_ln_b=dlb, sgu_w=dw[None], sgu_b=dbt.T[None])
    dy = ffn_backward("ffn1", 1, s1["xa"], s1["h1"], s1["xn1"], dy, bwd_comms=[lambda: red.scatter(ga), lambda: red.swap(gb)],
                      dwin_comms=[lambda: red.scatter(gb), lambda: red.exchange(ga)])
    dy = ffn_backward("ffn2", 0, s0["xc"], s0["h2"], s0["xn2"], dy, bwd_comms=[lambda: red.swap(gc), lambda: red.exchange(gb)],
                      dwin_comms=[lambda: red.scatter(gc)])
    dy_in = dy
    dconv, dpc, dpw, rows = _mix0_bwd_a(dy_in, s0["h0"], s0["yconv"], *pool_args, "mix0_bwd_a")
    dy, dh0, norm_grads["mix_norm"][0] = _mix0_bwd_b(s0["xb"], dy_in, s0["h0"], dconv, dpc, row(w["mix_norm"][0]), conv_w_full,
                                                      gw("ab_w_in", 0), "mix0_bwd_b")
    red.add(("ab_w_in", 0), _tn_matmul(s0["xnm"], dh0, 1.0, 1024, 1536, "ab_dwin", comms=[red.swap(gd), red.exchange(gc)]))
    small.update(pool_w=dpw[None], conv_w=rows[None, 0:CONV_WIDTH], conv_b=rows[32:33], conv_ln_g=rows[33:34], conv_ln_b=rows[34:35],
                 pool_scale=rows[35:36], pool_b=rows[36:37].reshape(1, len(POOL_WINDOWS), POOL_GC))

    last = {}

    def behind_ffn2_scatter(token):
        red.add(("ab_w_out", 0), _tn_matmul(s0["ycat"], dy_in, 1.0, 1024, 1024, "ab_dwout", after=[token]))
        last["dx"], last["dh"], last["act"], norm_grads["ffn1_norm"][0] = _ffn_bwd(
            s0["xa"], dy, s0["h1"], row(w["ffn1_norm"][0]), gw("ffn1_w_in", 0), gw("ffn1_w_out", 0), "ffn1_bwd_0", after=[token])
        return [norm_grads["ffn1_norm"][0]]

    red.scatter_behind(gd, behind_ffn2_scatter, "ffn2_0")

    small_sum = {}

    def small_ready():
        for k, v in norm_grads.items():
            small[k] = jnp.concatenate(v, axis=0)
        flat = [small[n].reshape(-1, 128) for n in SMALL]
        rows = sum(f.shape[0] for f in flat)
        loss_block = jnp.pad(loss, ((0, 8 + (-rows) % 8 - 1), (0, 127)))
        buf = jnp.concatenate(flat + [loss_block], axis=0)

        def done(gathered):
            total, at = _sum_leading(gathered, "reduce_small"), 0
            for n, f in zip(SMALL, flat):
                small_sum[n] = total[at:at + f.shape[0]].reshape(small[n].shape)
                at += f.shape[0]
            small_sum["loss"] = total[at:at + 1, 0:1]

        return _small_all_gather(buf, done)

    small_first, small_second = small_ready()
    red.add(("ffn1_w_in", 0), _tn_matmul(s0["xn1"], last["dh"], 1.0, 1024, 1408, "ffn1_dwin_0", comms=[red.swap(ge), small_first]))
    red.add(("ffn1_w_out", 0), _tn_matmul(last["act"], dy, 0.5, 1408, 1024, "ffn1_dwout_0",
                                          comms=[red.scatter(ge), red.exchange(gd), small_second]))
    grad_x = last["dx"]

    big_out = {}

    def adamw_big(n, layer, after=()):
        big_out[n] = _adamw_sharded(w[n], red.mine[(n, layer)], red.theirs[(n, layer)], given["m_" + n], given["v_" + n], core, layer,
                                    big_out.get(n), f"adamw_{n}_{layer}", after=after)

    _exchange_alone("swap_last_grads", [red.swap(gf), red.exchange(ge)])

    def other_updates(token):
        for n in BIG:
            for layer in reversed(range(w[n].shape[0])):
                if (n, layer) not in gf:
                    adamw_big(n, layer, after=[token])
        return [big_out[n][0] for n in BIG]

    red.scatter_behind(gf, other_updates, "last")
    _exchange_alone("exchange_last_grads", [red.exchange(gf)])
    for key in gf:
        adamw_big(*key)

    loss = small_sum["loss"][0, 0]
    grads, delta, new_m, new_v = {}, {}, {}, {}
    for n in WEIGHT_NAMES:
        mom, var = given["m_" + n], given["v_" + n]
        if n in BIG:
            grads[n], delta[n], new_m[n], new_v[n] = big_out[n]
            continue
        g = small_sum[n]
        if n in SHARDED_SMALL:
            width = w[n].shape[-1]
            g = lax.dynamic_slice_in_dim(g, chip * width, width, axis=g.ndim - 1)
        grads[n] = g
        delta[n], new_m[n], new_v[n] = _adamw(w[n], g, mom, var, f"adamw_{n}")
    return (loss, grad_x[None], *[grads[n] for n in WEIGHT_NAMES], *[delta[n] for n in WEIGHT_NAMES],
            *[new_m[n] for n in WEIGHT_NAMES], *[new_v[n] for n in WEIGHT_NAMES])
```

```python
import jax
import jax.numpy as jnp
from jax import lax
from jax.experimental import pallas as pl
from jax.experimental.pallas import tpu as pltpu

F32, BF16 = jnp.float32, jnp.bfloat16
EPS = 1e-6
N_CHIPS = 4
POOL_WINDOWS = (2, 4, 8, 16)
POOL_GC = 128
POOL_CH = 512
CONV_CH = 512
CONV_WIDTH = 31
HALO = 32
SGU_HEADS = 8
CHUNK = 128
DEPTH = 2
ADAM_LR, ADAM_B1, ADAM_B2, ADAM_EPS, ADAM_WD, ADAM_STEP = 0.001, 0.9, 0.999, 1e-08, 0.01, 10
VMEM_LIMIT_BYTES = 60 * 1024 * 1024
MESH_AXES = ("x", "y", "c")
MESH = pl.DeviceIdType.MESH
HBM_SPEC = pl.BlockSpec(memory_space=pltpu.HBM)


def _sds(a):
    return jax.ShapeDtypeStruct(a.shape, a.dtype)


def _cparams_nd(n):
    return pltpu.CompilerParams(dimension_semantics=("arbitrary",) * n, vmem_limit_bytes=VMEM_LIMIT_BYTES)


def _cparams():
    return _cparams_nd(1)


def _dot(a, b):
    return jnp.dot(a, b, preferred_element_type=F32)


def _dot_nt(a, b):
    return lax.dot_general(a, b, (((1,), (1,)), ((), ())), preferred_element_type=F32)


def _dot_tn(a, b):
    return lax.dot_general(a, b, (((0,), (0,)), ((), ())), preferred_element_type=F32)


def _rms_fwd(x):
    r = lax.rsqrt(jnp.mean(x * x, axis=-1, keepdims=True) + EPS)
    return x * r, r


def _rms_bwd(dxn, xh, r, g):
    dxh = dxn * g
    return r * (dxh - xh * jnp.mean(dxh * xh, axis=-1, keepdims=True))


def _ln_fwd(y):
    mu = jnp.mean(y, axis=-1, keepdims=True)
    yc = y - mu
    rs = lax.rsqrt(jnp.mean(yc * yc, axis=-1, keepdims=True) + EPS)
    return yc * rs, rs


def _ln_bwd(dyhat, yhat, rs):
    return rs * (dyhat - jnp.mean(dyhat, axis=-1, keepdims=True) - yhat * jnp.mean(dyhat * yhat, axis=-1, keepdims=True))


def _sigmoid(x):
    return 0.5 * jnp.tanh(0.5 * x) + 0.5


def _const_spec(shape):
    n = len(shape)
    return pl.BlockSpec(shape, lambda i: (0,) * n)


def _row_spec(tm, cols):
    return pl.BlockSpec((tm, cols), lambda i: (i, 0))


def _my_place():
    return lax.axis_index("x"), lax.axis_index("y"), lax.axis_index("c")


def _other_chips(x, y):
    return [(1 - x, y), (x, 1 - y), (1 - x, 1 - y)]


class _Exchange:
    def __init__(self, inputs, out_shapes, plan, count, aliases=None, finish=None, local=None, n_local=0):
        self.inputs, self.out_shapes, self.plan, self.count = list(inputs), list(out_shapes), plan, count
        self.aliases, self.finish, self.local, self.n_local = dict(aliases or {}), finish, local, n_local


def _call(body, *, name, grid, in_specs, out_specs, out_shape, args, scratch_shapes=(), comms=(), scalar=None, aliases=None, after=()):
    comms = [cm if isinstance(cm, _Exchange) else cm() for cm in comms]
    in_specs, out_specs, out_shape, scratch_shapes = list(in_specs), list(out_specs), list(out_shape), list(scratch_shapes)
    n_body_in = len(in_specs)
    in_specs += [pl.BlockSpec(memory_space=pl.ANY)] * len(after)
    args = list(args) + list(after)
    n_in, n_out, n_scr = len(in_specs), len(out_specs), len(scratch_shapes)
    c_in = [a for cm in comms for a in cm.inputs]
    c_out = [s for cm in comms for s in cm.out_shapes]
    n_remote = sum(cm.count for cm in comms)
    n_local = sum(cm.n_local for cm in comms)
    n_scalar = 0 if scalar is None else 1
    all_aliases = {n_scalar + i: o for i, o in (aliases or {}).items()}
    at_in, at_out = n_scalar + n_in, n_out
    for cm in comms:
        for i, o in cm.aliases.items():
            all_aliases[at_in + i] = at_out + o
        at_in += len(cm.inputs)
        at_out += len(cm.out_shapes)

    def wrapped(*all_refs):
        scalar_ref, refs = all_refs[:n_scalar], all_refs[n_scalar:]
        ins, ci = refs[:n_in], refs[n_in:n_in + len(c_in)]
        at = n_in + len(c_in)
        outs, co = refs[at:at + n_out], refs[at + n_out:at + n_out + len(c_out)]
        at += n_out + len(c_out)
        scr = refs[at:at + n_scr]

        def run_body():
            body(*scalar_ref, *ins[:n_body_in], *outs, *scr)

        if not comms:
            run_body()
            return
        send_sems, recv_sems, local_sems = refs[at + n_scr:]
        place = _my_place()
        sends, arrivals, locals_ = [], [], []
        i0 = o0 = 0
        for cm in comms:
            cm_in, cm_out = ci[i0:i0 + len(cm.inputs)], co[o0:o0 + len(cm.out_shapes)]
            i0 += len(cm.inputs)
            o0 += len(cm.out_shapes)
            for src, dst, dev, incoming in cm.plan(cm_in, cm_out, place):
                k = len(sends)
                sends.append(pltpu.make_async_remote_copy(src_ref=src, dst_ref=dst, send_sem=send_sems.at[k], recv_sem=recv_sems.at[k],
                                                          device_id=dev, device_id_type=MESH))
                arrivals.append(pltpu.make_async_remote_copy(src_ref=src, dst_ref=incoming, send_sem=send_sems.at[k],
                                                             recv_sem=recv_sems.at[k], device_id=dev, device_id_type=MESH))
            if cm.local is not None:
                for src, dst in cm.local(cm_in, cm_out, place):
                    locals_.append(pltpu.make_async_copy(src, dst, local_sems.at[len(locals_)]))

        def start():
            for cp in locals_ + sends:
                cp.start()

        def finish():
            for cp in arrivals:
                cp.wait_recv()
            for cp in sends:
                cp.wait_send()
            for cp in locals_:
                cp.wait()

        if not grid:
            start()
            run_body()
            finish()
            return
        ids = [pl.program_id(a) for a in range(len(grid))]
        first, last = ids[0] == 0, ids[0] == grid[0] - 1
        for a in range(1, len(grid)):
            first = jnp.logical_and(first, ids[a] == 0)
            last = jnp.logical_and(last, ids[a] == grid[a] - 1)
        pl.when(first)(start)
        run_body()
        pl.when(last)(finish)

    sems = []
    if comms:
        sems = [pltpu.SemaphoreType.DMA((max(n_remote, 1),)), pltpu.SemaphoreType.DMA((max(n_remote, 1),)),
                pltpu.SemaphoreType.DMA((max(n_local, 1),))]
    all_in, all_out = in_specs + [HBM_SPEC] * len(c_in), out_specs + [HBM_SPEC] * len(c_out)
    if scalar is None:
        kwargs = dict(grid=grid, compiler_params=_cparams_nd(len(grid))) if grid else {}
        res = pl.pallas_call(
            wrapped, name=name, in_specs=all_in, out_specs=all_out, out_shape=out_shape + c_out, scratch_shapes=scratch_shapes + sems,
            input_output_aliases=all_aliases, **kwargs)(*args, *c_in)
    else:
        spec = pltpu.PrefetchScalarGridSpec(num_scalar_prefetch=1, grid=grid, in_specs=all_in, out_specs=all_out,
                                            scratch_shapes=scratch_shapes + sems)
        res = pl.pallas_call(
            wrapped, name=name, grid_spec=spec, out_shape=out_shape + c_out, input_output_aliases=all_aliases,
            compiler_params=_cparams_nd(len(grid)))(scalar, *args, *c_in)
    at = n_out
    for cm in comms:
        got = res[at:at + len(cm.out_shapes)]
        at += len(cm.out_shapes)
        if cm.finish is not None:
            cm.finish(got)
    return list(res[:n_out])


def _exchange_alone(name, comms):
    _call(lambda: None, name=name, grid=(), in_specs=[], out_specs=[], out_shape=[], args=[], comms=comms)


def _in_weight_copies(w_hbm, w_v, sem, base=0):
    return [pltpu.make_async_copy(w_hbm.at[q], w_v.at[q], sem.at[base + q]) for q in range(N_CHIPS)]


def _out_weight_copies(w_hbm, w_v, sem, base=0):
    rows = w_hbm.shape[1]
    return [pltpu.make_async_copy(w_hbm.at[q], w_v.at[pl.ds(q * rows, rows)], sem.at[base + q]) for q in range(N_CHIPS)]


def _load_at_first_step(copies):
    @pl.when(pl.program_id(0) == 0)
    def _():
        for cp in copies:
            cp.start()
        for cp in copies:
            cp.wait()


def _loss_head(xo, tgt, gv, loss_ref, dg_ref):
    d = xo.shape[1]
    xh, r = _rms_fwd(xo)
    diff = xh * gv - tgt
    loss_ref[...] += 0.5 * jnp.sum(jnp.sum(diff * diff, axis=1, keepdims=True), axis=0, keepdims=True) / d
    dout = diff / d
    dg_ref[...] += jnp.sum(dout * xh, axis=0, keepdims=True)
    return _rms_bwd(dout, xh, r, gv)


def _ffn_fwd(x, g, win_g, wout_g, name, comms=(), loss_head=None):
    t, d = x.shape
    c = win_g.shape[-1]
    ff = 2 * c
    tm = min(512, t)
    n_head = 0 if loss_head is None else 2

    def body(x_ref, g_ref, win_hbm, wout_hbm, *rest):
        head_in, (xo_ref, h_ref, xn_ref), rest = rest[:n_head], rest[n_head:n_head + 3], rest[n_head + 3:]
        head_out, (win_v, wout_v, sem) = rest[:n_head], rest[n_head:]
        _load_at_first_step(_in_weight_copies(win_hbm, win_v, sem) + _out_weight_copies(wout_hbm, wout_v, sem, N_CHIPS))
        xv = x_ref[...]
        xh, _ = _rms_fwd(xv)
        xn = (xh * g_ref[...]).astype(BF16)
        xn_ref[...] = xn
        acc = jnp.zeros((tm, d), F32)
        for j in range(2):
            gate = _dot(xn, win_v[j])
            up = _dot(xn, win_v[j + 2])
            h_ref[:, j * c:(j + 1) * c] = gate.astype(BF16)
            h_ref[:, ff + j * c:ff + (j + 1) * c] = up.astype(BF16)
            act = (gate * _sigmoid(gate) * up).astype(BF16)
            acc = acc + _dot(act, wout_v[j * c:(j + 1) * c, :])
        xo = xv + 0.5 * acc
        if loss_head is None:
            xo_ref[...] = xo
            return

        @pl.when(pl.program_id(0) == 0)
        def _():
            for ref in head_out:
                ref[...] = jnp.zeros_like(ref)

        xo_ref[...] = _loss_head(xo, head_in[0][...], head_in[1][...], *head_out)

    head_specs = [] if loss_head is None else [_row_spec(tm, d), _const_spec((1, d))]
    head_out_specs = [] if loss_head is None else [_const_spec((1, 1)), _const_spec((1, d))]
    head_out_shape = [] if loss_head is None else [jax.ShapeDtypeStruct((1, 1), F32), jax.ShapeDtypeStruct((1, d), F32)]
    return _call(
        body, name=name, grid=(t // tm,),
        in_specs=[_row_spec(tm, d), _const_spec((1, d)), HBM_SPEC, HBM_SPEC] + head_specs,
        out_specs=[_row_spec(tm, d), _row_spec(tm, 2 * ff), _row_spec(tm, d)] + head_out_specs,
        out_shape=[jax.ShapeDtypeStruct((t, d), F32), jax.ShapeDtypeStruct((t, 2 * ff), BF16), jax.ShapeDtypeStruct((t, d), BF16)]
        + head_out_shape,
        scratch_shapes=[pltpu.VMEM((N_CHIPS, d, c), BF16), pltpu.VMEM((ff, d), BF16), pltpu.SemaphoreType.DMA((2 * N_CHIPS,))],
        args=(x, g, win_g, wout_g) + tuple(loss_head or ()), comms=comms)


def _ffn_bwd(x, dy, h, g, win_g, wout_g, name, comms=(), after=()):
    t, d = x.shape
    c = win_g.shape[-1]
    ff = 2 * c
    tm = min(256, t)

    def body(x_ref, dy_ref, h_ref, g_ref, win_hbm, wout_hbm, dx_ref, dh_ref, act_ref, dg_ref, win_v, wout_v, sem):
        _load_at_first_step(_in_weight_copies(win_hbm, win_v, sem) + _out_weight_copies(wout_hbm, wout_v, sem, N_CHIPS))

        @pl.when(pl.program_id(0) == 0)
        def _():
            dg_ref[...] = jnp.zeros_like(dg_ref)

        xv, dyv, gv = x_ref[...], dy_ref[...], g_ref[...]
        xh, r = _rms_fwd(xv)
        dyh = (0.5 * dyv).astype(BF16)
        dxn = jnp.zeros((tm, d), F32)
        for j in range(2):
            gate = h_ref[:, j * c:(j + 1) * c].astype(F32)
            up = h_ref[:, ff + j * c:ff + (j + 1) * c].astype(F32)
            dact = _dot_nt(dyh, wout_v[j * c:(j + 1) * c, :])
            s = _sigmoid(gate)
            sl = gate * s
            act_ref[:, j * c:(j + 1) * c] = (sl * up).astype(BF16)
            dgate = (dact * up * (s + sl * (1.0 - s))).astype(BF16)
            dup = (dact * sl).astype(BF16)
            dh_ref[:, j * c:(j + 1) * c] = dgate
            dh_ref[:, ff + j * c:ff + (j + 1) * c] = dup
            dxn = dxn + _dot_nt(dgate, win_v[j]) + _dot_nt(dup, win_v[j + 2])
        dg_ref[...] += jnp.sum(dxn * xh, axis=0, keepdims=True)
        dx_ref[...] = dyv + _rms_bwd(dxn, xh, r, gv)

    return _call(
        body, name=name, grid=(t // tm,),
        in_specs=[_row_spec(tm, d), _row_spec(tm, d), _row_spec(tm, 2 * ff), _const_spec((1, d)), HBM_SPEC, HBM_SPEC],
        out_specs=[_row_spec(tm, d), _row_spec(tm, 2 * ff), _row_spec(tm, ff), _const_spec((1, d))],
        out_shape=[jax.ShapeDtypeStruct((t, d), F32), jax.ShapeDtypeStruct((t, 2 * ff), BF16), jax.ShapeDtypeStruct((t, ff), BF16),
                   jax.ShapeDtypeStruct((1, d), F32)],
        scratch_shapes=[pltpu.VMEM((N_CHIPS, d, c), BF16), pltpu.VMEM((ff, d), BF16), pltpu.SemaphoreType.DMA((2 * N_CHIPS,))],
        args=(x, dy, h, g, win_g, wout_g), comms=comms, after=after)


def _norm_matmul(x, g, win_g, name, comms=()):
    t, d = x.shape
    c = win_g.shape[-1]
    tm = min(512, t)

    def body(x_ref, g_ref, win_hbm, o_ref, xn_ref, win_v, sem):
        _load_at_first_step(_in_weight_copies(win_hbm, win_v, sem))
        xh, _ = _rms_fwd(x_ref[...])
        xn = (xh * g_ref[...]).astype(BF16)
        xn_ref[...] = xn
        for q in range(N_CHIPS):
            o_ref[:, q * c:(q + 1) * c] = _dot(xn, win_v[q])

    return _call(
        body, name=name, grid=(t // tm,),
        in_specs=[_row_spec(tm, d), _const_spec((1, d)), HBM_SPEC],
        out_specs=[_row_spec(tm, N_CHIPS * c), _row_spec(tm, d)],
        out_shape=[jax.ShapeDtypeStruct((t, N_CHIPS * c), F32), jax.ShapeDtypeStruct((t, d), BF16)],
        scratch_shapes=[pltpu.VMEM((N_CHIPS, d, c), BF16), pltpu.SemaphoreType.DMA((N_CHIPS,))],
        args=(x, g, win_g), comms=comms)


def _proj_in_bwd_tail(dh, win_v, c):
    dxn = _dot_nt(dh[:, 0:c], win_v[0])
    for q in range(1, N_CHIPS):
        dxn = dxn + _dot_nt(dh[:, q * c:(q + 1) * c], win_v[q])
    return dxn


def _prev_halo_spec(tm, cols):
    return pl.BlockSpec((HALO, cols), lambda i: (jnp.maximum(i * (tm // HALO) - 1, 0), 0))


def _next_halo_spec(tm, cols, t):
    last = t // HALO - 1
    return pl.BlockSpec((HALO, cols), lambda i: (jnp.minimum((i + 1) * (tm // HALO), last), 0))


def _shift_down(w, k):
    return w if k == 0 else pltpu.roll(w, k, 0)


def _shift_up(w, k):
    return w if k == 0 else pltpu.roll(w, w.shape[0] - k, 0)


def _pool_counts(i, tm):
    pos = (i * tm + lax.broadcasted_iota(jnp.int32, (tm, POOL_CH), 0) + 1).astype(F32)
    lane = lax.broadcasted_iota(jnp.int32, (tm, POOL_CH), 1)
    win = jnp.where(lane < POOL_GC, 2.0, jnp.where(lane < 2 * POOL_GC, 4.0, jnp.where(lane < 3 * POOL_GC, 8.0, 16.0)))
    return jnp.minimum(pos, win)


def _group_select(parts):
    return jnp.concatenate([p[:, k * POOL_GC:(k + 1) * POOL_GC] for k, p in enumerate(parts)], axis=1)


TAP_ROWS = 128


def _tap_blocks(tm, block_fn):
    cols = []
    for ch in range(CONV_CH // POOL_GC):
        lanes = slice(ch * POOL_GC, (ch + 1) * POOL_GC)
        cols.append(jnp.concatenate([block_fn(r, lanes) for r in range(tm // TAP_ROWS)], axis=0))
    return jnp.concatenate(cols, axis=1)


def _conv_block(win, taps):
    acc = jnp.zeros((TAP_ROWS, win.shape[1]), F32)
    for k in range(CONV_WIDTH):
        acc = acc + taps[k:k + 1, :] * _shift_down(win, CONV_WIDTH - 1 - k)[HALO:, :]
    return acc


def _conv_block_transposed(win, taps):
    acc = jnp.zeros((TAP_ROWS, win.shape[1]), F32)
    for k in range(CONV_WIDTH):
        acc = acc + taps[k:k + 1, :] * _shift_up(win, CONV_WIDTH - 1 - k)[0:TAP_ROWS, :]
    return acc


def _mix0_recompute(i, tm, h_cur, h_prev, conv_w, conv_b, y=None):
    prev = jnp.where(i > 0, h_prev, 0.0)
    win = jnp.concatenate([prev, h_cur], axis=0)
    u_w = win[:, 0:POOL_CH]
    a_w = win[:, POOL_CH:POOL_CH + CONV_CH]
    gt_w = win[:, POOL_CH + CONV_CH:]
    g_w = a_w * _sigmoid(gt_w)
    if y is None:
        y = _tap_blocks(tm, lambda r, lanes: _conv_block(g_w[r * TAP_ROWS:(r + 1) * TAP_ROWS + HALO, lanes], conv_w[:, lanes])) + conv_b
    s2 = u_w + _shift_down(u_w, 1)
    s4 = s2 + _shift_down(s2, 2)
    s8 = s4 + _shift_down(s4, 4)
    s16 = s8 + _shift_down(s8, 8)
    sums = _group_select([s2[HALO:], s4[HALO:], s8[HALO:], s16[HALO:]])
    cnt = _pool_counts(i, tm)
    pooled = sums / cnt - h_cur[:, 0:POOL_CH]
    return g_w, y, pooled, cnt


def _pool_linear(pooled, pw_ref, pb):
    return jnp.concatenate(
        [_dot(pooled[:, k * POOL_GC:(k + 1) * POOL_GC].astype(BF16), pw_ref[k].astype(BF16)) for k in range(len(POOL_WINDOWS))], axis=1) + pb


def _mix0_fwd(x, h0, pool_w, pool_b, pool_scale, conv_w, conv_b, ln_g, ln_b, wout_g, name, comms=()):
    t, d = x.shape
    tm = min(512, t)
    hc = h0.shape[1]

    def body(x_ref, h_ref, hp_ref, pw_ref, pb_ref, ps_ref, cw_ref, cb_ref, lg_ref, lb_ref, wout_hbm, xo_ref, ycat_ref, y_ref, wout_v, sem):
        _load_at_first_step(_out_weight_copies(wout_hbm, wout_v, sem))
        i = pl.program_id(0)
        _, y, pooled, _ = _mix0_recompute(i, tm, h_ref[...], hp_ref[...], cw_ref[...], cb_ref[...])
        y_ref[...] = y
        yhat, _ = _ln_fwd(y)
        yn = yhat * lg_ref[...] + lb_ref[...]
        yb = yn * _sigmoid(yn)
        ya = _pool_linear(pooled, pw_ref, pb_ref[...]) * ps_ref[...]
        ycat = jnp.concatenate([ya, yb], axis=1).astype(BF16)
        ycat_ref[...] = ycat
        xo_ref[...] = x_ref[...] + _dot(ycat, wout_v[...])

    return _call(
        body, name=name, grid=(t // tm,),
        in_specs=[_row_spec(tm, d), _row_spec(tm, hc), _prev_halo_spec(tm, hc), _const_spec(pool_w.shape), _const_spec((1, POOL_CH)),
                  _const_spec((1, POOL_CH)), _const_spec(conv_w.shape), _const_spec((1, CONV_CH)), _const_spec((1, CONV_CH)),
                  _const_spec((1, CONV_CH)), HBM_SPEC],
        out_specs=[_row_spec(tm, d), _row_spec(tm, d), _row_spec(tm, CONV_CH)],
        out_shape=[jax.ShapeDtypeStruct((t, d), F32), jax.ShapeDtypeStruct((t, d), BF16), jax.ShapeDtypeStruct((t, CONV_CH), F32)],
        scratch_shapes=[pltpu.VMEM((d, d), BF16), pltpu.SemaphoreType.DMA((N_CHIPS,))],
        args=(x, h0, h0, pool_w, pool_b, pool_scale, conv_w, conv_b, ln_g, ln_b, wout_g), comms=comms)


def _mix0_bwd_a(dy, h0, y_conv, pool_w, pool_b, pool_scale, conv_w, conv_b, ln_g, ln_b, wout_g, name):
    t, d = dy.shape
    tm = min(512, t)
    hc = h0.shape[1]
    n_small = 40

    def body(dy_ref, h_ref, hp_ref, y_ref, pw_ref, pb_ref, ps_ref, cw_ref, cb_ref, lg_ref, lb_ref, wout_hbm,
             dconv_ref, dpc_ref, dpw_ref, small_ref, wout_v, sem):
        _load_at_first_step(_out_weight_copies(wout_hbm, wout_v, sem))
        i = pl.program_id(0)

        @pl.when(i == 0)
        def _():
            dpw_ref[...] = jnp.zeros_like(dpw_ref)
            small_ref[...] = jnp.zeros_like(small_ref)

        g_w, y, pooled, cnt = _mix0_recompute(i, tm, h_ref[...], hp_ref[...], cw_ref[...], cb_ref[...], y_ref[...])
        yhat, rs = _ln_fwd(y)
        lg = lg_ref[...]
        yn = yhat * lg + lb_ref[...]
        mixed = _pool_linear(pooled, pw_ref, pb_ref[...])
        dycat = _dot_nt(dy_ref[...].astype(BF16), wout_v[...])
        dya, dyb = dycat[:, 0:POOL_CH], dycat[:, POOL_CH:]
        sg = _sigmoid(yn)
        dyn = dyb * (sg * (1.0 + yn * (1.0 - sg)))
        dyc = _ln_bwd(dyn * lg, yhat, rs)
        dconv_ref[...] = dyc

        def add_row(k, value):
            small_ref[k:k + 1, :] += jnp.sum(value, axis=0, keepdims=True)

        for ch in range(CONV_CH // POOL_GC):
            lanes = slice(ch * POOL_GC, (ch + 1) * POOL_GC)
            taps = [jnp.zeros((1, POOL_GC), F32)] * CONV_WIDTH
            for r in range(tm // TAP_ROWS):
                win = g_w[r * TAP_ROWS:(r + 1) * TAP_ROWS + HALO, lanes]
                d = dyc[r * TAP_ROWS:(r + 1) * TAP_ROWS, lanes]
                for k in range(CONV_WIDTH):
                    taps[k] = taps[k] + jnp.sum(d * _shift_down(win, CONV_WIDTH - 1 - k)[HALO:, :], axis=0, keepdims=True)
            for k in range(CONV_WIDTH):
                small_ref[k:k + 1, lanes] += taps[k]
        add_row(32, dyc)
        add_row(33, dyn * yhat)
        add_row(34, dyn)
        scale = ps_ref[...]
        dmixed = dya * scale
        add_row(35, dya * mixed)
        add_row(36, dmixed)
        dmb = dmixed.astype(BF16)
        dpooled = []
        for k in range(len(POOL_WINDOWS)):
            sl = slice(k * POOL_GC, (k + 1) * POOL_GC)
            dpw_ref[k] += _dot_tn(pooled[:, sl].astype(BF16), dmb[:, sl])
            dpooled.append(_dot_nt(dmb[:, sl], pw_ref[k].astype(BF16)))
        dpc_ref[...] = jnp.concatenate(dpooled, axis=1) / cnt

    return _call(
        body, name=name, grid=(t // tm,),
        in_specs=[_row_spec(tm, d), _row_spec(tm, hc), _prev_halo_spec(tm, hc), _row_spec(tm, CONV_CH), _const_spec(pool_w.shape),
                  _const_spec((1, POOL_CH)), _const_spec((1, POOL_CH)), _const_spec(conv_w.shape), _const_spec((1, CONV_CH)),
                  _const_spec((1, CONV_CH)), _const_spec((1, CONV_CH)), HBM_SPEC],
        out_specs=[_row_spec(tm, CONV_CH), _row_spec(tm, POOL_CH), _const_spec(pool_w.shape), _const_spec((n_small, CONV_CH))],
        out_shape=[jax.ShapeDtypeStruct((t, CONV_CH), F32), jax.ShapeDtypeStruct((t, POOL_CH), F32),
                   jax.ShapeDtypeStruct(pool_w.shape, F32), jax.ShapeDtypeStruct((n_small, CONV_CH), F32)],
        scratch_shapes=[pltpu.VMEM((d, d), BF16), pltpu.SemaphoreType.DMA((N_CHIPS,))],
        args=(dy, h0, h0, y_conv, pool_w, pool_b, pool_scale, conv_w, conv_b, ln_g, ln_b, wout_g))


def _mix0_bwd_b(x, dy, h0, dconv, dpc, g, conv_w, win_g, name):
    t, d = x.shape
    tm = min(512, t)
    hc = h0.shape[1]
    c = win_g.shape[-1]
    n_tiles = t // tm

    def body(x_ref, dy_ref, h_ref, dc_ref, dcn_ref, dp_ref, dpn_ref, g_ref, cw_ref, win_hbm, dx_ref, dh_ref, dg_ref, win_v, sem):
        _load_at_first_step(_in_weight_copies(win_hbm, win_v, sem))
        i = pl.program_id(0)

        @pl.when(i == 0)
        def _():
            dg_ref[...] = jnp.zeros_like(dg_ref)

        not_last = i < n_tiles - 1
        dc_w = jnp.concatenate([dc_ref[...], jnp.where(not_last, dcn_ref[...], 0.0)], axis=0)
        dp_w = jnp.concatenate([dp_ref[...], jnp.where(not_last, dpn_ref[...], 0.0)], axis=0)
        cw = cw_ref[...]
        dg = _tap_blocks(tm, lambda r, lanes: _conv_block_transposed(dc_w[r * TAP_ROWS:(r + 1) * TAP_ROWS + HALO, lanes], cw[:, lanes]))
        a2 = dp_w + _shift_up(dp_w, 1)
        a4 = a2 + _shift_up(a2, 2)
        a8 = a4 + _shift_up(a4, 4)
        a16 = a8 + _shift_up(a8, 8)
        back = _group_select([a2[0:tm], a4[0:tm], a8[0:tm], a16[0:tm]])
        du = back - dp_ref[...] * _pool_counts(i, tm)
        hv = h_ref[...]
        a = hv[:, POOL_CH:POOL_CH + CONV_CH]
        sig = _sigmoid(hv[:, POOL_CH + CONV_CH:])
        dh = jnp.concatenate([du, dg * sig, dg * a * sig * (1.0 - sig)], axis=1).astype(BF16)
        dh_ref[...] = dh
        dxn = _proj_in_bwd_tail(dh, win_v, c)
        xh, r = _rms_fwd(x_ref[...])
        dg_ref[...] += jnp.sum(dxn * xh, axis=0, keepdims=True)
        dx_ref[...] = dy_ref[...] + _rms_bwd(dxn, xh, r, g_ref[...])

    return _call(
        body, name=name, grid=(n_tiles,),
        in_specs=[_row_spec(tm, d), _row_spec(tm, d), _row_spec(tm, hc), _row_spec(tm, CONV_CH), _next_halo_spec(tm, CONV_CH, t),
                  _row_spec(tm, POOL_CH), _next_halo_spec(tm, POOL_CH, t), _const_spec((1, d)), _const_spec(conv_w.shape), HBM_SPEC],
        out_specs=[_row_spec(tm, d), _row_spec(tm, hc), _const_spec((1, d))],
        out_shape=[jax.ShapeDtypeStruct((t, d), F32), jax.ShapeDtypeStruct((t, hc), BF16), jax.ShapeDtypeStruct((1, d), F32)],
        scratch_shapes=[pltpu.VMEM((N_CHIPS, d, c), BF16), pltpu.SemaphoreType.DMA((N_CHIPS,))],
        args=(x, dy, h0, dconv, dconv, dpc, dpc, g, conv_w, win_g))


SQRT_HALF = 0.7071067811865476
INV_SQRT_2PI = 0.3989422804014327


def _causal_mask():
    return (lax.broadcasted_iota(jnp.int32, (CHUNK, CHUNK), 1) <= lax.broadcasted_iota(jnp.int32, (CHUNK, CHUNK), 0)).astype(F32)


def _sgu_recompute(pre, lg, lb):
    half = pre.shape[1] // 2
    phi = 0.5 * (1.0 + lax.erf(pre * SQRT_HALF))
    z = pre * phi
    u, v = z[:, 0:half], z[:, half:]
    vhat, rs = _ln_fwd(v)
    return u, vhat, rs, vhat * lg + lb, phi


def _sgu_spatial(vln, w_ref, bt, tm):
    mask = _causal_mask()
    wm = [(w_ref[hd] * mask).astype(BF16) for hd in range(SGU_HEADS)]
    vb = vln.astype(BF16)
    rows = []
    for ch in range(tm // CHUNK):
        blocks = [_dot(wm[hd], vb[ch * CHUNK:(ch + 1) * CHUNK, hd * CHUNK:(hd + 1) * CHUNK]) + bt[:, hd:hd + 1] for hd in range(SGU_HEADS)]
        rows.append(jnp.concatenate(blocks, axis=1))
    return jnp.concatenate(rows, axis=0), wm


def _sgu_fwd(x, pre, ln_g, ln_b, w, bt, wout_g, name):
    t, d = x.shape
    tm = min(512, t)
    pc = pre.shape[1]

    def body(x_ref, pre_ref, lg_ref, lb_ref, w_ref, bt_ref, wout_hbm, xo_ref, p_ref, wout_v, sem):
        _load_at_first_step(_out_weight_copies(wout_hbm, wout_v, sem))
        u, _, _, vln, _ = _sgu_recompute(pre_ref[...], lg_ref[...], lb_ref[...])
        vo, _ = _sgu_spatial(vln, w_ref, bt_ref[...], tm)
        p = (u * vo).astype(BF16)
        p_ref[...] = p
        xo_ref[...] = x_ref[...] + _dot(p, wout_v[...])

    return _call(
        body, name=name, grid=(t // tm,),
        in_specs=[_row_spec(tm, d), _row_spec(tm, pc), _const_spec((1, d)), _const_spec((1, d)), _const_spec(w.shape),
                  _const_spec(bt.shape), HBM_SPEC],
        out_specs=[_row_spec(tm, d), _row_spec(tm, d)],
        out_shape=[jax.ShapeDtypeStruct((t, d), F32), jax.ShapeDtypeStruct((t, d), BF16)],
        scratch_shapes=[pltpu.VMEM((d, d), BF16), pltpu.SemaphoreType.DMA((N_CHIPS,))],
        args=(x, pre, ln_g, ln_b, w, bt, wout_g))


def _sgu_bwd(x, dy, pre, g, ln_g, ln_b, w, bt, win_g, wout_g, name, comms=()):
    t, d = x.shape
    tm = min(512, t)
    pc = pre.shape[1]
    c = win_g.shape[-1]

    def body(x_ref, dy_ref, pre_ref, g_ref, lg_ref, lb_ref, w_ref, bt_ref, win_hbm, wout_hbm,
             dx_ref, dpre_ref, dg_ref, dlg_ref, dlb_ref, dw_ref, dbt_ref, win_v, wout_v, sem):
        _load_at_first_step(_in_weight_copies(win_hbm, win_v, sem) + _out_weight_copies(wout_hbm, wout_v, sem, N_CHIPS))
        i = pl.program_id(0)

        @pl.when(i == 0)
        def _():
            for ref in (dg_ref, dlg_ref, dlb_ref, dw_ref, dbt_ref):
                ref[...] = jnp.zeros_like(ref)

        prev = pre_ref[...]
        lg = lg_ref[...]
        u, vhat, rs, vln, phi = _sgu_recompute(prev, lg, lb_ref[...])
        vo, wm = _sgu_spatial(vln, w_ref, bt_ref[...], tm)
        dp = _dot_nt(dy_ref[...].astype(BF16), wout_v[...])
        du = dp * vo
        dvo = dp * u
        dvob = dvo.astype(BF16)
        vb = vln.astype(BF16)
        head_lane = lax.broadcasted_iota(jnp.int32, (CHUNK, SGU_HEADS), 1)
        dbt = jnp.zeros((CHUNK, SGU_HEADS), F32)
        dw = [jnp.zeros((CHUNK, CHUNK), F32) for _ in range(SGU_HEADS)]
        rows = []
        for ch in range(tm // CHUNK):
            rs_ = slice(ch * CHUNK, (ch + 1) * CHUNK)
            blocks = []
            for hd in range(SGU_HEADS):
                cs = slice(hd * CHUNK, (hd + 1) * CHUNK)
                dbt = dbt + jnp.where(head_lane == hd, jnp.sum(dvo[rs_, cs], axis=1, keepdims=True), 0.0)
                dw[hd] = dw[hd] + _dot_nt(dvob[rs_, cs], vb[rs_, cs])
                blocks.append(_dot_tn(wm[hd], dvob[rs_, cs]))
            rows.append(jnp.concatenate(blocks, axis=1))
        dvln = jnp.concatenate(rows, axis=0)
        mask = _causal_mask()
        for hd in range(SGU_HEADS):
            dw_ref[hd] += dw[hd] * mask
        dbt_ref[...] += dbt
        dlg_ref[...] += jnp.sum(dvln * vhat, axis=0, keepdims=True)
        dlb_ref[...] += jnp.sum(dvln, axis=0, keepdims=True)
        dv = _ln_bwd(dvln * lg, vhat, rs)
        gelu_grad = phi + prev * jnp.exp(-0.5 * prev * prev) * INV_SQRT_2PI
        dpre = (jnp.concatenate([du, dv], axis=1) * gelu_grad).astype(BF16)
        dpre_ref[...] = dpre
        dxn = _proj_in_bwd_tail(dpre, win_v, c)
        xh, r = _rms_fwd(x_ref[...])
        dg_ref[...] += jnp.sum(dxn * xh, axis=0, keepdims=True)
        dx_ref[...] = dy_ref[...] + _rms_bwd(dxn, xh, r, g_ref[...])

    return _call(
        body, name=name, grid=(t // tm,),
        in_specs=[_row_spec(tm, d), _row_spec(tm, d), _row_spec(tm, pc), _const_spec((1, d)), _const_spec((1, d)), _const_spec((1, d)),
                  _const_spec(w.shape), _const_spec(bt.shape), HBM_SPEC, HBM_SPEC],
        out_specs=[_row_spec(tm, d), _row_spec(tm, pc), _const_spec((1, d)), _const_spec((1, d)), _const_spec((1, d)),
                   _const_spec(w.shape), _const_spec(bt.shape)],
        out_shape=[jax.ShapeDtypeStruct((t, d), F32), jax.ShapeDtypeStruct((t, pc), BF16), jax.ShapeDtypeStruct((1, d), F32),
                   jax.ShapeDtypeStruct((1, d), F32), jax.ShapeDtypeStruct((1, d), F32), jax.ShapeDtypeStruct(w.shape, F32),
                   jax.ShapeDtypeStruct(bt.shape, F32)],
        scratch_shapes=[pltpu.VMEM((N_CHIPS, d, c), BF16), pltpu.VMEM((d, d), BF16), pltpu.SemaphoreType.DMA((2 * N_CHIPS,))],
        args=(x, dy, pre, g, ln_g, ln_b, w, bt, win_g, wout_g), comms=comms)


def _tn_matmul(a, b, scale, bm, bn, name, comms=(), after=()):
    t, m = a.shape
    n = b.shape[1]
    tk = min(2048, t)
    bm, bn = min(bm, m), min(bn, n)
    nk = t // tk

    def body(a_ref, b_ref, o_ref, acc_ref):
        k = pl.program_id(2)

        @pl.when(k == 0)
        def _():
            acc_ref[...] = jnp.zeros_like(acc_ref)

        bv = b_ref[...]
        if bv.dtype != BF16:
            bv = (scale * bv).astype(BF16)
        acc_ref[...] += _dot_tn(a_ref[...], bv)

        @pl.when(k == nk - 1)
        def _():
            o_ref[...] = acc_ref[...].astype(BF16)

    return _call(
        body, name=name, grid=(m // bm, n // bn, nk),
        in_specs=[pl.BlockSpec((tk, bm), lambda i, j, k: (k, i)), pl.BlockSpec((tk, bn), lambda i, j, k: (k, j))],
        out_specs=[pl.BlockSpec((bm, bn), lambda i, j, k: (i, j))],
        out_shape=[jax.ShapeDtypeStruct((m, n), BF16)],
        scratch_shapes=[pltpu.VMEM((bm, bn), F32)],
        args=(a, b), comms=comms, after=after)[0]


def _row_tile(rows, cols, budget_bytes=2 * 1024 * 1024):
    best = None
    for cand in range(16, rows + 1, 16):
        if rows % cand == 0 and cand * cols * 4 <= budget_bytes:
            best = cand
    return best or rows


def _scalar_grid(grid, in_specs, out_specs):
    return pltpu.PrefetchScalarGridSpec(num_scalar_prefetch=1, grid=grid, in_specs=in_specs, out_specs=out_specs)


def _cast_into_slot(w, layer, me, name, after=()):
    _, rows, cols = w.shape
    tr = _row_tile(rows, cols)

    def body(me_ref, w_ref, *rest):
        rest[-1][...] = w_ref[...].astype(BF16)

    return pl.pallas_call(
        body, name=name,
        grid_spec=_scalar_grid((rows // tr,), [pl.BlockSpec((None, tr, cols), lambda i, me: (layer, i, 0))]
                               + [pl.BlockSpec(memory_space=pl.ANY)] * len(after),
                               pl.BlockSpec((None, tr, cols), lambda i, me: (me[0], i, 0))),
        out_shape=jax.ShapeDtypeStruct((N_CHIPS, rows, cols), BF16), compiler_params=_cparams())(me, w, *after)


def _add_half(view, other, core, name):
    q, _, r, c = view.shape
    tr = _row_tile(r, c)

    def body(core_ref, a_ref, b_ref, o_ref):
        o_ref[...] = (a_ref[...].astype(F32) + b_ref[...].astype(F32)).astype(BF16)

    return pl.pallas_call(
        body, name=name,
        grid_spec=_scalar_grid((q, r // tr), [pl.BlockSpec((None, None, tr, c), lambda k, i, core: (k, core[0], i, 0)),
                                             pl.BlockSpec((None, tr, c), lambda k, i, core: (k, i, 0))],
                               pl.BlockSpec((None, tr, c), lambda k, i, core: (k, i, 0))),
        out_shape=jax.ShapeDtypeStruct((q, r, c), BF16), compiler_params=_cparams_nd(2))(core, view, other)


def _reduce_piece(partial, staged, me, column_sharded, name):
    _, r, c = staged.shape
    tr = _row_tile(r, c, budget_bytes=1024 * 1024)
    nt = r // tr
    if column_sharded:
        own2d = partial.reshape(r, N_CHIPS * c)
        own_spec = pl.BlockSpec((tr, c), lambda i, me: (i, me[0]))
    else:
        own2d = partial.reshape(N_CHIPS * r, c)
        own_spec = pl.BlockSpec((tr, c), lambda i, me: (me[0] * nt + i, 0))
    ring = [pl.BlockSpec((None, tr, c), lambda i, me, k=k: ((me[0] + k) % N_CHIPS, i, 0)) for k in (1, 2, 3)]

    def body(me_ref, own_ref, s1_ref, s2_ref, s3_ref, o_ref):
        o_ref[...] = ((own_ref[...].astype(F32) + s1_ref[...].astype(F32)) + s2_ref[...].astype(F32)) + s3_ref[...].astype(F32)

    return pl.pallas_call(
        body, name=name, grid_spec=_scalar_grid((nt,), [own_spec] + ring, pl.BlockSpec((tr, c), lambda i, me: (i, 0))),
        out_shape=jax.ShapeDtypeStruct((r, c), F32), compiler_params=_cparams())(me, own2d, staged, staged, staged)


def _sum_leading(s, name):
    n, rows, cols = s.shape
    tr = _row_tile(rows, cols, budget_bytes=1024 * 1024)

    def body(s_ref, o_ref):
        acc = s_ref[0].astype(F32)
        for k in range(1, n):
            acc = acc + s_ref[k].astype(F32)
        o_ref[...] = acc

    return pl.pallas_call(
        body, name=name, grid=(rows // tr,), in_specs=[pl.BlockSpec((n, tr, cols), lambda i: (0, i, 0))], out_specs=_row_spec(tr, cols),
        out_shape=jax.ShapeDtypeStruct((rows, cols), F32), compiler_params=_cparams())(s)


ADAM_C1 = 1.0 / (1.0 - ADAM_B1 ** ADAM_STEP)
ADAM_C2 = 1.0 / (1.0 - ADAM_B2 ** ADAM_STEP)


def _adamw_math(w, g, m, v):
    mn = ADAM_B1 * m + (1.0 - ADAM_B1) * g
    vn = ADAM_B2 * v + (1.0 - ADAM_B2) * (g * g)
    return -ADAM_LR * ((mn * ADAM_C1) / (jnp.sqrt(vn * ADAM_C2) + ADAM_EPS) + ADAM_WD * w), mn, vn


def _adamw(w, g, m, v, name):
    shape = w.shape
    cols = shape[-1] if w.ndim > 1 else 128
    w2, g2, m2, v2 = (a.reshape(-1, cols) for a in (w, g, m, v))
    rows = w2.shape[0]
    tr = _row_tile(rows, cols, budget_bytes=1024 * 1024)

    def body(w_ref, g_ref, m_ref, v_ref, d_ref, mo_ref, vo_ref):
        d_ref[...], mo_ref[...], vo_ref[...] = _adamw_math(w_ref[...], g_ref[...], m_ref[...], v_ref[...])

    spec = _row_spec(tr, cols)
    outs = pl.pallas_call(
        body, name=name, grid=(rows // tr,), in_specs=[spec] * 4, out_specs=[spec] * 3,
        out_shape=[jax.ShapeDtypeStruct((rows, cols), F32)] * 3, compiler_params=_cparams())(w2, g2, m2, v2)
    return tuple(o.reshape(shape) for o in outs)


def _adamw_sharded(w, g_mine, g_sibling, m, v, core, layer, prev, name, comms=(), after=()):
    n_layers, r, c = w.shape
    half = r // 2
    tr = _row_tile(half, c)
    nt = half // tr

    def body(core_ref, w_ref, gm_ref, gs_ref, m_ref, v_ref, *rest):
        g_ref, d_ref, mo_ref, vo_ref = rest[-4:]
        gv = jnp.where(pl.program_id(0) == core_ref[0], gm_ref[...], gs_ref[...])
        g_ref[...] = gv
        d_ref[...], mo_ref[...], vo_ref[...] = _adamw_math(w_ref[...], gv, m_ref[...], v_ref[...])

    full = pl.BlockSpec((None, tr, c), lambda h, i, core: (layer, h * nt + i, 0))
    part = pl.BlockSpec((tr, c), lambda h, i, core: (i, 0))
    args = [w, g_mine, g_sibling, m, v]
    in_specs = [full, part, part, full, full]
    aliases = {}
    if prev is not None:
        aliases = {len(args) + k: k for k in range(4)}
        args += list(prev)
        in_specs += [pl.BlockSpec(memory_space=pl.ANY)] * 4
    args += list(after)
    in_specs += [pl.BlockSpec(memory_space=pl.ANY)] * len(after)
    return _call(body, name=name, grid=(2, nt), in_specs=in_specs, out_specs=[full] * 4, out_shape=[jax.ShapeDtypeStruct(w.shape, F32)] * 4,
                 args=args, comms=comms, scalar=core, aliases=aliases)


BIG_IN = ("ffn1_w_in", "ffn2_w_in", "ab_w_in", "sgu_w_in")
BIG_OUT = ("ffn1_w_out", "ffn2_w_out", "ab_w_out", "sgu_w_out")
BIG = BIG_IN + BIG_OUT


class _Gatherer:
    def __init__(self, slots):
        self.slots = dict(slots)

    def _stage(self, keys, d2d):
        n = len(keys)

        def plan(ins, outs, place):
            x, y, c = place
            me = 2 * x + y
            remote = []
            for a in range(n):
                rows = ins[a].shape[1] // 2

                def half(ref, q, core, rows=rows):
                    return ref.at[q, pl.ds(core * rows, rows), :]

                for (px, py) in _other_chips(x, y):
                    q = 2 * px + py
                    if d2d:
                        remote.append((half(ins[a], q, c), half(outs[a], q, c), (x, y, 1 - c), half(outs[a], q, 1 - c)))
                    else:
                        remote.append((half(ins[a], me, c), half(outs[a], me, c), (px, py, c), half(outs[a], q, c)))
            return remote

        def finish(outs):
            for k, o in zip(keys, outs):
                self.slots[k] = o

        arrays = [self.slots[k] for k in keys]
        return _Exchange(arrays, [_sds(a) for a in arrays], plan, 3 * n, {a: a for a in range(n)}, finish)

    def direct(self, keys):
        n = len(keys)

        def plan(ins, outs, place):
            x, y, c = place
            me = 2 * x + y
            return [(ins[a].at[me], outs[a].at[me], (px, py, c), outs[a].at[2 * px + py]) for a in range(n) for (px, py) in _other_chips(x, y)]

        def finish(outs):
            for k, o in zip(keys, outs):
                self.slots[k] = o

        arrays = [self.slots[k] for k in keys]
        return _Exchange(arrays, [_sds(a) for a in arrays], plan, 3 * n, {a: a for a in range(n)}, finish)

    def ici(self, keys):
        return self._stage(keys, False)

    def d2d(self, keys):
        return self._stage(keys, True)


class _Reducer:
    def __init__(self, me, core):
        self.me, self.core = me, core
        self.views, self.partial, self.staged, self.mine, self.theirs = {}, {}, {}, {}, {}

    def add(self, key, g):
        m, n = g.shape
        if key[0] in BIG_IN:
            self.views[key] = g.reshape(1, 2, m // 2, n)
        else:
            self.views[key] = g.reshape(N_CHIPS, 2, m // (2 * N_CHIPS), n)

    def swap(self, keys):
        views = [self.views[k] for k in keys]

        def plan(ins, outs, place):
            x, y, c = place
            return [(ins[a].at[:, 1 - c], outs[a], (x, y, 1 - c), outs[a]) for a in range(len(keys))]

        def finish(outs):
            for k, v, o in zip(keys, views, outs):
                self.partial[k] = _add_half(v, o, self.core, f"chip_partial_{k[0]}_{k[1]}")

        shapes = [jax.ShapeDtypeStruct((v.shape[0],) + v.shape[2:], v.dtype) for v in views]
        return _Exchange(views, shapes, plan, len(keys), None, finish)

    def scatter(self, keys, part=(0, 1)):
        i, n = part
        n_keys = len(keys)
        parts = [self.partial[k] for k in keys]
        shapes = []
        for k, p in zip(keys, parts):
            q, r, c = p.shape
            shapes.append(jax.ShapeDtypeStruct((N_CHIPS, r, c // N_CHIPS if k[0] in BIG_IN else c), p.dtype))

        def piece(ref, key, q, rows, cols):
            return ref.at[0, rows, pl.ds(q * cols, cols)] if key[0] in BIG_IN else ref.at[q, rows, :]

        def plan(ins, outs, place):
            x, y, c = place
            me = 2 * x + y
            remote = []
            for a, k in enumerate(keys):
                _, r, cols = shapes[a].shape
                rows = pl.ds(i * (r // n), r // n)
                for (px, py) in _other_chips(x, y):
                    q = 2 * px + py
                    remote.append((piece(ins[a], k, q, rows, cols), outs[a].at[me, rows, :], (px, py, c), outs[a].at[q, rows, :]))
            return remote

        def finish(outs):
            for k, p, o in zip(keys, parts, outs):
                self.staged[k] = o
                if i == n - 1:
                    self.mine[k] = _reduce_piece(p, o, self.me, k[0] in BIG_IN, f"reduce_{k[0]}_{k[1]}")

        inputs, aliases = parts, None
        if i > 0:
            inputs = parts + [self.staged[k] for k in keys]
            aliases = {n_keys + a: a for a in range(n_keys)}
        return _Exchange(inputs, shapes, plan, 3 * n_keys, aliases, finish)

    def scatter_behind(self, keys, work, tag):
        n = len(keys)
        parts = [self.partial[k] for k in keys]
        lands = []
        for k, p in zip(keys, parts):
            _, r, c = p.shape
            lands.append(jax.ShapeDtypeStruct((N_CHIPS, r, c // N_CHIPS if k[0] in BIG_IN else c), p.dtype))
        sem_spec = pl.BlockSpec(memory_space=pltpu.SEMAPHORE)
        effect = pltpu.CompilerParams(has_side_effects=pltpu.SideEffectType.DATAFLOW_SIDE_EFFECTING)

        def copies(part_refs, land_refs, send_sems, recv_sems):
            x, y, c = _my_place()
            me = 2 * x + y
            out = []
            for a, k in enumerate(keys):
                cols = lands[a].shape[2]
                for j, (px, py) in enumerate(_other_chips(x, y)):
                    q = 2 * px + py
                    src = part_refs[a].at[0, :, pl.ds(q * cols, cols)] if k[0] in BIG_IN else part_refs[a].at[q]
                    sems = dict(send_sem=send_sems.at[3 * a + j], recv_sem=recv_sems.at[3 * a + j], device_id=(px, py, c), device_id_type=MESH)
                    out.append((pltpu.make_async_remote_copy(src_ref=src, dst_ref=land_refs[a].at[me], **sems),
                                pltpu.make_async_remote_copy(src_ref=src, dst_ref=land_refs[a].at[q], **sems)))
            return out

        def start_body(*refs):
            part_refs, land_refs, send_sems, recv_sems, token = refs[:n], refs[n:2 * n], refs[2 * n], refs[2 * n + 1], refs[-1]
            for send, _ in copies(part_refs, land_refs, send_sems, recv_sems):
                send.start()
            token[...] = jnp.zeros_like(token)

        def wait_body(*refs):
            part_refs, land_refs, send_sems, recv_sems = refs[:n], refs[n:2 * n], refs[2 * n], refs[2 * n + 1]
            for send, arrive in copies(part_refs, land_refs, send_sems, recv_sems):
                send.wait_send()
                arrive.wait_recv()

        in_hbm = [pltpu.with_memory_space_constraint(p, pltpu.HBM) for p in parts]
        in_hbm += [pltpu.with_memory_space_constraint(lax.empty(s.shape, s.dtype), pltpu.HBM) for s in lands]
        thru_shapes = [pltpu.HBM(p.shape, p.dtype) for p in parts] + [pltpu.HBM(s.shape, s.dtype) for s in lands]
        started = pl.pallas_call(
            start_body, name=f"scatter_{tag}_start", in_specs=[HBM_SPEC] * (2 * n),
            out_shape=(pltpu.SemaphoreType.DMA((3 * n,)), pltpu.SemaphoreType.DMA((3 * n,)), *thru_shapes, jax.ShapeDtypeStruct((8, 128), F32)),
            out_specs=(sem_spec, sem_spec, *[HBM_SPEC] * (2 * n), pl.BlockSpec(memory_space=pltpu.VMEM)),
            input_output_aliases={i: 2 + i for i in range(2 * n)}, compiler_params=effect)(*in_hbm)
        send_sems, recv_sems, thru, token = started[0], started[1], started[2:2 + 2 * n], started[-1]
        after = work(token)
        done = pl.pallas_call(
            wait_body, name=f"scatter_{tag}_wait", in_specs=[HBM_SPEC] * (2 * n) + [sem_spec, sem_spec] + [pl.BlockSpec(memory_space=pl.ANY)] * len(after),
            out_shape=tuple(thru_shapes), out_specs=tuple([HBM_SPEC] * (2 * n)), input_output_aliases={i: i for i in range(2 * n)},
            compiler_params=effect)(*thru, send_sems, recv_sems, *after)
        for a, k in enumerate(keys):
            self.staged[k] = done[n + a]
            self.mine[k] = _reduce_piece(done[a], done[n + a], self.me, k[0] in BIG_IN, f"reduce_{k[0]}_{k[1]}")

    def exchange(self, keys):
        mine = [self.mine[k] for k in keys]

        def plan(ins, outs, place):
            x, y, c = place
            return [(ins[a], outs[a], (x, y, 1 - c), outs[a]) for a in range(len(keys))]

        def finish(outs):
            for k, o in zip(keys, outs):
                self.theirs[k] = o

        return _Exchange(mine, [_sds(a) for a in mine], plan, len(keys), None, finish)


def _half(ref, q, core):
    rows = ref.shape[1] // 2
    return ref.at[q, pl.ds(core * rows, rows), :]


def _all_gather_full(gat, keys, work):
    n = len(keys)
    arrays = [gat.slots[k] for k in keys]
    sem_spec = pl.BlockSpec(memory_space=pltpu.SEMAPHORE)
    effect = pltpu.CompilerParams(has_side_effects=pltpu.SideEffectType.DATAFLOW_SIDE_EFFECTING)

    def places():
        x, y, c = _my_place()
        return c, (x, y, 1 - c), (1 - x, y, c), (x, 1 - y, c), 2 * x + y, 2 * (1 - x) + y, 2 * x + (1 - y), 2 * (1 - x) + (1 - y)

    def first_hop(refs, send_sems, recv_sems):
        c, _, x_nbr, y_nbr, me, qx, qy, _ = places()
        out = []
        for a in range(n):
            for k, (to, q) in enumerate(((x_nbr, qx), (y_nbr, qy))):
                sems = dict(send_sem=send_sems.at[2 * a + k], recv_sem=recv_sems.at[2 * a + k], device_id=to, device_id_type=MESH)
                out.append((pltpu.make_async_remote_copy(src_ref=_half(refs[a], me, c), dst_ref=_half(refs[a], me, c), **sems),
                            pltpu.make_async_remote_copy(src_ref=_half(refs[a], me, c), dst_ref=_half(refs[a], q, c), **sems)))
        return out

    def start_body(*refs):
        for send, _ in first_hop(refs[:n], refs[n], refs[n + 1]):
            send.start()
        refs[-1][...] = jnp.zeros_like(refs[-1])

    def wait_body(*refs):
        for send, arrive in first_hop(refs[:n], refs[n], refs[n + 1]):
            send.wait_send()
            arrive.wait_recv()

    thru_shapes = [pltpu.HBM(a.shape, a.dtype) for a in arrays]
    started = pl.pallas_call(
        start_body, name="gather_first_hop_start", in_specs=[HBM_SPEC] * n,
        out_shape=(pltpu.SemaphoreType.DMA((2 * n,)), pltpu.SemaphoreType.DMA((2 * n,)), *thru_shapes, jax.ShapeDtypeStruct((8, 128), F32)),
        out_specs=(sem_spec, sem_spec, *[HBM_SPEC] * n, pl.BlockSpec(memory_space=pltpu.VMEM)),
        input_output_aliases={i: 2 + i for i in range(n)}, compiler_params=effect,
    )(*[pltpu.with_memory_space_constraint(a, pltpu.HBM) for a in arrays])
    after = work(started[-1])
    landed_first = pl.pallas_call(
        wait_body, name="gather_first_hop_wait", in_specs=[HBM_SPEC] * n + [sem_spec, sem_spec] + [pl.BlockSpec(memory_space=pl.ANY)] * len(after),
        out_shape=tuple(thru_shapes), out_specs=tuple([HBM_SPEC] * n), input_output_aliases={i: i for i in range(n)},
        compiler_params=effect)(*started[2:2 + n], started[0], started[1], *after)
    per = 5

    def body(*refs):
        outs = refs[n:2 * n]
        send_sems, recv_sems = refs[2 * n:]
        c, sibling, x_nbr, y_nbr, _, qx, qy, qd = places()

        def quarter(ref, q, core, k):
            rows = ref.shape[1] // 4
            return ref.at[q, pl.ds((2 * core + k) * rows, rows), :]

        def copy(a, k, part, to):
            return pltpu.make_async_remote_copy(src_ref=part, dst_ref=part, send_sem=send_sems.at[per * a + k],
                                                recv_sem=recv_sems.at[per * a + k], device_id=to, device_id_type=MESH)

        sent = []

        def send(a, k, part, to):
            cp = copy(a, k, part, to)
            cp.start()
            sent.append(cp)

        for a in range(n):
            send(a, 0, quarter(outs[a], qy, c, 0), x_nbr)
            send(a, 1, quarter(outs[a], qx, c, 1), y_nbr)
            send(a, 2, _half(outs[a], qx, c), sibling)
            send(a, 3, _half(outs[a], qy, c), sibling)
        for a in range(n):
            copy(a, 0, quarter(outs[a], qd, c, 0), sibling).wait_recv()
            copy(a, 1, quarter(outs[a], qd, c, 1), sibling).wait_recv()
            send(a, 4, _half(outs[a], qd, c), sibling)
        for a in range(n):
            for k, q in ((2, qx), (3, qy), (4, qd)):
                copy(a, k, _half(outs[a], q, 1 - c), sibling).wait_recv()
        for cp in sent:
            cp.wait_send()

    outs = pl.pallas_call(
        body, name="gather_first_rest", in_specs=[HBM_SPEC] * n, out_specs=[HBM_SPEC] * n,
        out_shape=[_sds(a) for a in arrays], input_output_aliases={a: a for a in range(n)},
        scratch_shapes=[pltpu.SemaphoreType.DMA((per * n,)), pltpu.SemaphoreType.DMA((per * n,))])(*landed_first)
    for k, o in zip(keys, outs):
        gat.slots[k] = o


def _small_all_gather(buf, done):
    state = {}

    def index(x, y, c):
        return 4 * x + 2 * y + c

    def plan_ici(ins, outs, place):
        x, y, c = place
        return [(ins[0], outs[0].at[index(x, y, c)], (px, py, c), outs[0].at[index(px, py, c)]) for (px, py) in _other_chips(x, y)]

    def local(ins, outs, place):
        return [(ins[0], outs[0].at[index(*place)])]

    def plan_d2d(ins, outs, place):
        x, y, c = place
        return [(ins[0].at[index(px, py, c)], outs[0].at[index(px, py, c)], (x, y, 1 - c), outs[0].at[index(px, py, 1 - c)])
                for (px, py) in [(x, y)] + _other_chips(x, y)]

    def second():
        return _Exchange([state["blocks"]], [_sds(state["blocks"])], plan_d2d, N_CHIPS, {0: 0}, lambda outs: done(outs[0]))

    first = _Exchange([buf], [jax.ShapeDtypeStruct((2 * N_CHIPS,) + buf.shape, buf.dtype)], plan_ici, 3, None,
                      lambda outs: state.update(blocks=outs[0]), local, 1)
    return first, second


WEIGHT_NAMES = ("ffn1_norm", "ffn1_w_in", "ffn1_w_out", "mix_norm", "ffn2_norm", "ffn2_w_in", "ffn2_w_out", "ab_w_in", "pool_w", "pool_b",
                "pool_scale", "conv_w", "conv_b", "conv_ln_g", "conv_ln_b", "ab_w_out", "sgu_w_in", "sgu_ln_g", "sgu_ln_b", "sgu_w", "sgu_b",
                "sgu_w_out", "final_norm")
SMALL = tuple(n for n in WEIGHT_NAMES if n not in BIG)
SHARDED_SMALL = ("conv_w", "sgu_ln_g", "sgu_ln_b")
PACK_ROWS = 64
PACK = ("pack", 0)


def _pair(prefix, layer):
    return [(prefix + "_w_in", layer), (prefix + "_w_out", layer)]


def kernel(x, ffn1_norm, ffn1_w_in, ffn1_w_out, mix_norm, ffn2_norm, ffn2_w_in, ffn2_w_out, ab_w_in, pool_w, pool_b, pool_scale, conv_w, conv_b, conv_ln_g, conv_ln_b, ab_w_out, sgu_w_in, sgu_ln_g, sgu_ln_b, sgu_w, sgu_b, sgu_w_out, final_norm, loss_target, m_ffn1_norm, m_ffn1_w_in, m_ffn1_w_out, m_mix_norm, m_ffn2_norm, m_ffn2_w_in, m_ffn2_w_out, m_ab_w_in, m_pool_w, m_pool_b, m_pool_scale, m_conv_w, m_conv_b, m_conv_ln_g, m_conv_ln_b, m_ab_w_out, m_sgu_w_in, m_sgu_ln_g, m_sgu_ln_b, m_sgu_w, m_sgu_b, m_sgu_w_out, m_final_norm, v_ffn1_norm, v_ffn1_w_in, v_ffn1_w_out, v_mix_norm, v_ffn2_norm, v_ffn2_w_in, v_ffn2_w_out, v_ab_w_in, v_pool_w, v_pool_b, v_pool_scale, v_conv_w, v_conv_b, v_conv_ln_g, v_conv_ln_b, v_ab_w_out, v_sgu_w_in, v_sgu_ln_g, v_sgu_ln_b, v_sgu_w, v_sgu_b, v_sgu_w_out, v_final_norm):
    given = dict(locals())
    w = {n: given[n] for n in WEIGHT_NAMES}
    chip = 2 * lax.axis_index("x") + lax.axis_index("y")
    me = chip.astype(jnp.int32).reshape(1)
    core = lax.axis_index("c").astype(jnp.int32).reshape(1)
    row = lambda v: v.reshape(1, -1)
    xin, tgt = x[0], loss_target[0]

    pack = jnp.concatenate([
        w["conv_w"][0], jnp.zeros((1, 128), F32), w["sgu_ln_g"].reshape(2, 128), w["sgu_ln_b"].reshape(2, 128),
        jnp.zeros((PACK_ROWS - 36, 128), F32)], axis=0)
    first = _pair("ffn1", 0)
    slots = {PACK: lax.dynamic_update_slice(jnp.zeros((N_CHIPS, PACK_ROWS, 128), F32), pack[None], (me[0], 0, 0))}
    for n, layer in first:
        slots[(n, layer)] = _cast_into_slot(w[n], layer, me, f"cast_{n}_{layer}")
    gat = _Gatherer(slots)

    def other_casts(token):
        for n in BIG:
            for layer in range(w[n].shape[0]):
                if (n, layer) not in first:
                    token = gat.slots[(n, layer)] = _cast_into_slot(w[n], layer, me, f"cast_{n}_{layer}", after=[token])
        return [token]

    _all_gather_full(gat, first + [PACK], other_casts)
    gp = gat.slots[PACK]
    conv_w_full = jnp.transpose(gp[:, 0:CONV_WIDTH], (1, 0, 2)).reshape(CONV_WIDTH, N_CHIPS * 128)
    sgu_ln_g_full = gp[:, 32:34].reshape(1, -1)
    sgu_ln_b_full = gp[:, 34:36].reshape(1, -1)
    gw = lambda n, layer: gat.slots[(n, layer)]

    st = [dict(), dict()]
    st[0]["xa"] = xin
    later = _pair("sgu", 0) + _pair("ffn2", 1)
    cur, st[0]["h1"], st[0]["xn1"] = _ffn_fwd(xin, row(w["ffn1_norm"][0]), gw("ffn1_w_in", 0), gw("ffn1_w_out", 0), "ffn1_fwd_0",
                                              comms=[gat.direct(_pair("ab", 0)), gat.ici(_pair("ffn2", 0))])
    st[0]["xb"] = cur
    st[0]["h0"], st[0]["xnm"] = _norm_matmul(cur, row(w["mix_norm"][0]), gw("ab_w_in", 0), "mix0_proj_in",
                                             comms=[gat.d2d(_pair("ffn2", 0)), gat.ici([("ffn1_w_out", 1)])])
    pool_args = (w["pool_w"][0], row(w["pool_b"][0]), row(w["pool_scale"][0]), conv_w_full, row(w["conv_b"][0]), row(w["conv_ln_g"][0]),
                 row(w["conv_ln_b"][0]), gw("ab_w_out", 0))
    cur, st[0]["ycat"], st[0]["yconv"] = _mix0_fwd(cur, st[0]["h0"], *pool_args, "mix0_fwd", comms=[gat.ici([("ffn1_w_in", 1)])])
    st[0]["xc"] = cur
    cur, st[0]["h2"], st[0]["xn2"] = _ffn_fwd(cur, row(w["ffn2_norm"][0]), gw("ffn2_w_in", 0), gw("ffn2_w_out", 0), "ffn2_fwd_0",
                                              comms=[gat.d2d(_pair("ffn1", 1)), gat.ici(later)])
    st[1]["xa"] = cur
    cur, st[1]["h1"], st[1]["xn1"] = _ffn_fwd(cur, row(w["ffn1_norm"][1]), gw("ffn1_w_in", 1), gw("ffn1_w_out", 1), "ffn1_fwd_1",
                                              comms=[gat.d2d(later)])
    st[1]["xb"] = cur
    st[1]["pre"], st[1]["xnm"] = _norm_matmul(cur, row(w["mix_norm"][1]), gw("sgu_w_in", 0), "sgu_proj_in")
    sgu_args = (sgu_ln_g_full, sgu_ln_b_full, w["sgu_w"][0], w["sgu_b"][0].T)
    cur, st[1]["p"] = _sgu_fwd(cur, st[1]["pre"], *sgu_args, gw("sgu_w_out", 0), "sgu_fwd")
    st[1]["xc"] = cur
    dy, st[1]["h2"], st[1]["xn2"], loss, d_final = _ffn_fwd(cur, row(w["ffn2_norm"][1]), gw("ffn2_w_in", 1), gw("ffn2_w_out", 1),
                                                            "ffn2_fwd_1_loss", loss_head=(tgt, row(w["final_norm"])))

    red = _Reducer(me, core)
    small = {"final_norm": d_final.reshape(-1)}
    norm_grads = {"ffn1_norm": [None] * DEPTH, "mix_norm": [None] * DEPTH, "ffn2_norm": [None] * DEPTH}
    ga, gb, gc, gd, ge, gf = _pair("ffn2", 1), _pair("sgu", 0), _pair("ffn1", 1), _pair("ffn2", 0), _pair("ab", 0), _pair("ffn1", 0)

    def ffn_backward(prefix, layer, xs, hs, xns, dy_in, bwd_comms=(), dwin_comms=(), dwout_comms=()):
        dx, dh, act, norm_grads[prefix + "_norm"][layer] = _ffn_bwd(
            xs, dy_in, hs, row(w[prefix + "_norm"][layer]), gw(prefix + "_w_in", layer), gw(prefix + "_w_out", layer),
            f"{prefix}_bwd_{layer}", comms=bwd_comms)
        red.add((prefix + "_w_in", layer), _tn_matmul(xns, dh, 1.0, 1024, 1408, f"{prefix}_dwin_{layer}", comms=dwin_comms))
        red.add((prefix + "_w_out", layer), _tn_matmul(act, dy_in, 0.5, 1408, 1024, f"{prefix}_dwout_{layer}", comms=dwout_comms))
        return dx

    s1, s0 = st[1], st[0]
    dy = ffn_backward("ffn2", 1, s1["xc"], s1["h2"], s1["xn2"], dy)
    dy_in = dy
    dy, dpre, norm_grads["mix_norm"][1], dlg, dlb, dw, dbt = _sgu_bwd(
        s1["xb"], dy_in, s1["pre"], row(w["mix_norm"][1]), *sgu_args, gw("sgu_w_in", 0), gw("sgu_w_out", 0), "sgu_bwd", comms=[red.swap(ga)])
    red.add(("sgu_w_in", 0), _tn_matmul(s1["xnm"], dpre, 1.0, 1024, 2048, "sgu_dwin"))
    red.add(("sgu_w_out", 0), _tn_matmul(s1["p"], dy_in, 1.0, 1024, 1024, "sgu_dwout"))
    small.update(sgu_ln_g=dlg, sgu_ln_b=dlb, sgu_w=dw[None], sgu_b=dbt.T[None])
    dy = ffn_backward("ffn1", 1, s1["xa"], s1["h1"], s1["xn1"], dy, bwd_comms=[lambda: red.scatter(ga), lambda: red.swap(gb)],
                      dwin_comms=[lambda: red.scatter(gb), lambda: red.exchange(ga)])
    dy = ffn_backward("ffn2", 0, s0["xc"], s0["h2"], s0["xn2"], dy, bwd_comms=[lambda: red.swap(gc), lambda: red.exchange(gb)],
                      dwin_comms=[lambda: red.scatter(gc)])
    dy_in = dy
    dconv, dpc, dpw, rows = _mix0_bwd_a(dy_in, s0["h0"], s0["yconv"], *pool_args, "mix0_bwd_a")
    dy, dh0, norm_grads["mix_norm"][0] = _mix0_bwd_b(s0["xb"], dy_in, s0["h0"], dconv, dpc, row(w["mix_norm"][0]), conv_w_full,
                                                      gw("ab_w_in", 0), "mix0_bwd_b")
    red.add(("ab_w_in", 0), _tn_matmul(s0["xnm"], dh0, 1.0, 1024, 1536, "ab_dwin", comms=[red.swap(gd), red.exchange(gc)]))
    small.update(pool_w=dpw[None], conv_w=rows[None, 0:CONV_WIDTH], conv_b=rows[32:33], conv_ln_g=rows[33:34], conv_ln_b=rows[34:35],
                 pool_scale=rows[35:36], pool_b=rows[36:37].reshape(1, len(POOL_WINDOWS), POOL_GC))

    last = {}

    def behind_ffn2_scatter(token):
        red.add(("ab_w_out", 0), _tn_matmul(s0["ycat"], dy_in, 1.0, 1024, 1024, "ab_dwout", after=[token]))
        last["dx"], last["dh"], last["act"], norm_grads["ffn1_norm"][0] = _ffn_bwd(
            s0["xa"], dy, s0["h1"], row(w["ffn1_norm"][0]), gw("ffn1_w_in", 0), gw("ffn1_w_out", 0), "ffn1_bwd_0", after=[token])
        return [norm_grads["ffn1_norm"][0]]

    red.scatter_behind(gd, behind_ffn2_scatter, "ffn2_0")

    small_sum = {}

    def small_ready():
        for k, v in norm_grads.items():
            small[k] = jnp.concatenate(v, axis=0)
        flat = [small[n].reshape(-1, 128) for n in SMALL]
        rows = sum(f.shape[0] for f in flat)
        loss_block = jnp.pad(loss, ((0, 8 + (-rows) % 8 - 1), (0, 127)))
        buf = jnp.concatenate(flat + [loss_block], axis=0)

        def done(gathered):
            total, at = _sum_leading(gathered, "reduce_small"), 0
            for n, f in zip(SMALL, flat):
                small_sum[n] = total[at:at + f.shape[0]].reshape(small[n].shape)
                at += f.shape[0]
            small_sum["loss"] = total[at:at + 1, 0:1]

        return _small_all_gather(buf, done)

    small_first, small_second = small_ready()
    red.add(("ffn1_w_in", 0), _tn_matmul(s0["xn1"], last["dh"], 1.0, 1024, 1408, "ffn1_dwin_0", comms=[red.swap(ge), small_first]))
    red.add(("ffn1_w_out", 0), _tn_matmul(last["act"], dy, 0.5, 1408, 1024, "ffn1_dwout_0",
                                          comms=[red.scatter(ge), red.exchange(gd), small_second, red.swap(gf[:1])]))
    grad_x = last["dx"]

    big_out = {}

    def adamw_big(n, layer, after=()):
        big_out[n] = _adamw_sharded(w[n], red.mine[(n, layer)], red.theirs[(n, layer)], given["m_" + n], given["v_" + n], core, layer,
                                    big_out.get(n), f"adamw_{n}_{layer}", after=after)

    _exchange_alone("swap_last_grads", [red.swap(gf[1:]), red.exchange(ge)])

    def other_updates(token):
        for n in BIG:
            for layer in reversed(range(w[n].shape[0])):
                if (n, layer) not in gf:
                    adamw_big(n, layer, after=[token])
        return [big_out[n][0] for n in BIG]

    red.scatter_behind(gf, other_updates, "last")
    _exchange_alone("exchange_last_grads", [red.exchange(gf)])
    for key in gf:
        adamw_big(*key)

    loss = small_sum["loss"][0, 0]
    grads, delta, new_m, new_v = {}, {}, {}, {}
    for n in WEIGHT_NAMES:
        mom, var = given["m_" + n], given["v_" + n]
        if n in BIG:
            grads[n], delta[n], new_m[n], new_v[n] = big_out[n]
            continue
        g = small_sum[n]
        if n in SHARDED_SMALL:
            width = w[n].shape[-1]
            g = lax.dynamic_slice_in_dim(g, chip * width, width, axis=g.ndim - 1)
        grads[n] = g
        delta[n], new_m[n], new_v[n] = _adamw(w[n], g, mom, var, f"adamw_{n}")
    return (loss, grad_x[None], *[grads[n] for n in WEIGHT_NAMES], *[delta[n] for n in WEIGHT_NAMES],
            *[new_m[n] for n in WEIGHT_NAMES], *[new_v[n] for n in WEIGHT_NAMES])
```

```python
import jax
import jax.numpy as jnp
from jax import lax
from jax.experimental import pallas as pl
from jax.experimental.pallas import tpu as pltpu

F32, BF16 = jnp.float32, jnp.bfloat16
EPS = 1e-6
N_CHIPS = 4
POOL_WINDOWS = (2, 4, 8, 16)
POOL_GC = 128
POOL_CH = 512
CONV_CH = 512
CONV_WIDTH = 31
HALO = 32
SGU_HEADS = 8
CHUNK = 128
DEPTH = 2
ADAM_LR, ADAM_B1, ADAM_B2, ADAM_EPS, ADAM_WD, ADAM_STEP = 0.001, 0.9, 0.999, 1e-08, 0.01, 10
VMEM_LIMIT_BYTES = 60 * 1024 * 1024
MESH_AXES = ("x", "y", "c")
MESH = pl.DeviceIdType.MESH
HBM_SPEC = pl.BlockSpec(memory_space=pltpu.HBM)


def _sds(a):
    return jax.ShapeDtypeStruct(a.shape, a.dtype)


def _cparams_nd(n):
    return pltpu.CompilerParams(dimension_semantics=("arbitrary",) * n, vmem_limit_bytes=VMEM_LIMIT_BYTES)


def _cparams():
    return _cparams_nd(1)


def _dot(a, b):
    return jnp.dot(a, b, preferred_element_type=F32)


def _dot_nt(a, b):
    return lax.dot_general(a, b, (((1,), (1,)), ((), ())), preferred_element_type=F32)


def _dot_tn(a, b):
    return lax.dot_general(a, b, (((0,), (0,)), ((), ())), preferred_element_type=F32)


def _rms_fwd(x):
    r = lax.rsqrt(jnp.mean(x * x, axis=-1, keepdims=True) + EPS)
    return x * r, r


def _rms_bwd(dxn, xh, r, g):
    dxh = dxn * g
    return r * (dxh - xh * jnp.mean(dxh * xh, axis=-1, keepdims=True))


def _ln_fwd(y):
    mu = jnp.mean(y, axis=-1, keepdims=True)
    yc = y - mu
    rs = lax.rsqrt(jnp.mean(yc * yc, axis=-1, keepdims=True) + EPS)
    return yc * rs, rs


def _ln_bwd(dyhat, yhat, rs):
    return rs * (dyhat - jnp.mean(dyhat, axis=-1, keepdims=True) - yhat * jnp.mean(dyhat * yhat, axis=-1, keepdims=True))


def _sigmoid(x):
    return 0.5 * jnp.tanh(0.5 * x) + 0.5


def _const_spec(shape):
    n = len(shape)
    return pl.BlockSpec(shape, lambda i: (0,) * n)


def _row_spec(tm, cols):
    return pl.BlockSpec((tm, cols), lambda i: (i, 0))


def _my_place():
    return lax.axis_index("x"), lax.axis_index("y"), lax.axis_index("c")


def _other_chips(x, y):
    return [(1 - x, y), (x, 1 - y), (1 - x, 1 - y)]


class _Exchange:
    def __init__(self, inputs, out_shapes, plan, count, aliases=None, finish=None, local=None, n_local=0):
        self.inputs, self.out_shapes, self.plan, self.count = list(inputs), list(out_shapes), plan, count
        self.aliases, self.finish, self.local, self.n_local = dict(aliases or {}), finish, local, n_local


def _call(body, *, name, grid, in_specs, out_specs, out_shape, args, scratch_shapes=(), comms=(), scalar=None, aliases=None, after=()):
    comms = [cm if isinstance(cm, _Exchange) else cm() for cm in comms]
    in_specs, out_specs, out_shape, scratch_shapes = list(in_specs), list(out_specs), list(out_shape), list(scratch_shapes)
    n_body_in = len(in_specs)
    in_specs += [pl.BlockSpec(memory_space=pl.ANY)] * len(after)
    args = list(args) + list(after)
    n_in, n_out, n_scr = len(in_specs), len(out_specs), len(scratch_shapes)
    c_in = [a for cm in comms for a in cm.inputs]
    c_out = [s for cm in comms for s in cm.out_shapes]
    n_remote = sum(cm.count for cm in comms)
    n_local = sum(cm.n_local for cm in comms)
    n_scalar = 0 if scalar is None else 1
    all_aliases = {n_scalar + i: o for i, o in (aliases or {}).items()}
    at_in, at_out = n_scalar + n_in, n_out
    for cm in comms:
        for i, o in cm.aliases.items():
            all_aliases[at_in + i] = at_out + o
        at_in += len(cm.inputs)
        at_out += len(cm.out_shapes)

    def wrapped(*all_refs):
        scalar_ref, refs = all_refs[:n_scalar], all_refs[n_scalar:]
        ins, ci = refs[:n_in], refs[n_in:n_in + len(c_in)]
        at = n_in + len(c_in)
        outs, co = refs[at:at + n_out], refs[at + n_out:at + n_out + len(c_out)]
        at += n_out + len(c_out)
        scr = refs[at:at + n_scr]

        def run_body():
            body(*scalar_ref, *ins[:n_body_in], *outs, *scr)

        if not comms:
            run_body()
            return
        send_sems, recv_sems, local_sems = refs[at + n_scr:]
        place = _my_place()
        sends, arrivals, locals_ = [], [], []
        i0 = o0 = 0
        for cm in comms:
            cm_in, cm_out = ci[i0:i0 + len(cm.inputs)], co[o0:o0 + len(cm.out_shapes)]
            i0 += len(cm.inputs)
            o0 += len(cm.out_shapes)
            for src, dst, dev, incoming in cm.plan(cm_in, cm_out, place):
                k = len(sends)
                sends.append(pltpu.make_async_remote_copy(src_ref=src, dst_ref=dst, send_sem=send_sems.at[k], recv_sem=recv_sems.at[k],
                                                          device_id=dev, device_id_type=MESH))
                arrivals.append(pltpu.make_async_remote_copy(src_ref=src, dst_ref=incoming, send_sem=send_sems.at[k],
                                                             recv_sem=recv_sems.at[k], device_id=dev, device_id_type=MESH))
            if cm.local is not None:
                for src, dst in cm.local(cm_in, cm_out, place):
                    locals_.append(pltpu.make_async_copy(src, dst, local_sems.at[len(locals_)]))

        def start():
            for cp in locals_ + sends:
                cp.start()

        def finish():
            for cp in arrivals:
                cp.wait_recv()
            for cp in sends:
                cp.wait_send()
            for cp in locals_:
                cp.wait()

        if not grid:
            start()
            run_body()
            finish()
            return
        ids = [pl.program_id(a) for a in range(len(grid))]
        first, last = ids[0] == 0, ids[0] == grid[0] - 1
        for a in range(1, len(grid)):
            first = jnp.logical_and(first, ids[a] == 0)
            last = jnp.logical_and(last, ids[a] == grid[a] - 1)
        pl.when(first)(start)
        run_body()
        pl.when(last)(finish)

    sems = []
    if comms:
        sems = [pltpu.SemaphoreType.DMA((max(n_remote, 1),)), pltpu.SemaphoreType.DMA((max(n_remote, 1),)),
                pltpu.SemaphoreType.DMA((max(n_local, 1),))]
    all_in, all_out = in_specs + [HBM_SPEC] * len(c_in), out_specs + [HBM_SPEC] * len(c_out)
    if scalar is None:
        kwargs = dict(grid=grid, compiler_params=_cparams_nd(len(grid))) if grid else {}
        res = pl.pallas_call(
            wrapped, name=name, in_specs=all_in, out_specs=all_out, out_shape=out_shape + c_out, scratch_shapes=scratch_shapes + sems,
            input_output_aliases=all_aliases, **kwargs)(*args, *c_in)
    else:
        spec = pltpu.PrefetchScalarGridSpec(num_scalar_prefetch=1, grid=grid, in_specs=all_in, out_specs=all_out,
                                            scratch_shapes=scratch_shapes + sems)
        res = pl.pallas_call(
            wrapped, name=name, grid_spec=spec, out_shape=out_shape + c_out, input_output_aliases=all_aliases,
            compiler_params=_cparams_nd(len(grid)))(scalar, *args, *c_in)
    at = n_out
    for cm in comms:
        got = res[at:at + len(cm.out_shapes)]
        at += len(cm.out_shapes)
        if cm.finish is not None:
            cm.finish(got)
    return list(res[:n_out])


def _exchange_alone(name, comms):
    _call(lambda: None, name=name, grid=(), in_specs=[], out_specs=[], out_shape=[], args=[], comms=comms)


def _in_weight_copies(w_hbm, w_v, sem, base=0):
    return [pltpu.make_async_copy(w_hbm.at[q], w_v.at[q], sem.at[base + q]) for q in range(N_CHIPS)]


def _out_weight_copies(w_hbm, w_v, sem, base=0):
    rows = w_hbm.shape[1]
    return [pltpu.make_async_copy(w_hbm.at[q], w_v.at[pl.ds(q * rows, rows)], sem.at[base + q]) for q in range(N_CHIPS)]


def _load_at_first_step(copies):
    @pl.when(pl.program_id(0) == 0)
    def _():
        for cp in copies:
            cp.start()
        for cp in copies:
            cp.wait()


def _loss_head(xo, tgt, gv, loss_ref, dg_ref):
    d = xo.shape[1]
    xh, r = _rms_fwd(xo)
    diff = xh * gv - tgt
    loss_ref[...] += 0.5 * jnp.sum(jnp.sum(diff * diff, axis=1, keepdims=True), axis=0, keepdims=True) / d
    dout = diff / d
    dg_ref[...] += jnp.sum(dout * xh, axis=0, keepdims=True)
    return _rms_bwd(dout, xh, r, gv)


def _ffn_fwd(x, g, win_g, wout_g, name, comms=(), loss_head=None):
    t, d = x.shape
    c = win_g.shape[-1]
    ff = 2 * c
    tm = min(512, t)
    n_head = 0 if loss_head is None else 2

    def body(x_ref, g_ref, win_hbm, wout_hbm, *rest):
        head_in, (xo_ref, h_ref, xn_ref), rest = rest[:n_head], rest[n_head:n_head + 3], rest[n_head + 3:]
        head_out, (win_v, wout_v, sem) = rest[:n_head], rest[n_head:]
        _load_at_first_step(_in_weight_copies(win_hbm, win_v, sem) + _out_weight_copies(wout_hbm, wout_v, sem, N_CHIPS))
        xv = x_ref[...]
        xh, _ = _rms_fwd(xv)
        xn = (xh * g_ref[...]).astype(BF16)
        xn_ref[...] = xn
        acc = jnp.zeros((tm, d), F32)
        for j in range(2):
            gate = _dot(xn, win_v[j])
            up = _dot(xn, win_v[j + 2])
            h_ref[:, j * c:(j + 1) * c] = gate.astype(BF16)
            h_ref[:, ff + j * c:ff + (j + 1) * c] = up.astype(BF16)
            act = (gate * _sigmoid(gate) * up).astype(BF16)
            acc = acc + _dot(act, wout_v[j * c:(j + 1) * c, :])
        xo = xv + 0.5 * acc
        if loss_head is None:
            xo_ref[...] = xo
            return

        @pl.when(pl.program_id(0) == 0)
        def _():
            for ref in head_out:
                ref[...] = jnp.zeros_like(ref)

        xo_ref[...] = _loss_head(xo, head_in[0][...], head_in[1][...], *head_out)

    head_specs = [] if loss_head is None else [_row_spec(tm, d), _const_spec((1, d))]
    head_out_specs = [] if loss_head is None else [_const_spec((1, 1)), _const_spec((1, d))]
    head_out_shape = [] if loss_head is None else [jax.ShapeDtypeStruct((1, 1), F32), jax.ShapeDtypeStruct((1, d), F32)]
    return _call(
        body, name=name, grid=(t // tm,),
        in_specs=[_row_spec(tm, d), _const_spec((1, d)), HBM_SPEC, HBM_SPEC] + head_specs,
        out_specs=[_row_spec(tm, d), _row_spec(tm, 2 * ff), _row_spec(tm, d)] + head_out_specs,
        out_shape=[jax.ShapeDtypeStruct((t, d), F32), jax.ShapeDtypeStruct((t, 2 * ff), BF16), jax.ShapeDtypeStruct((t, d), BF16)]
        + head_out_shape,
        scratch_shapes=[pltpu.VMEM((N_CHIPS, d, c), BF16), pltpu.VMEM((ff, d), BF16), pltpu.SemaphoreType.DMA((2 * N_CHIPS,))],
        args=(x, g, win_g, wout_g) + tuple(loss_head or ()), comms=comms)


def _ffn_bwd(x, dy, h, g, win_g, wout_g, name, comms=(), after=()):
    t, d = x.shape
    c = win_g.shape[-1]
    ff = 2 * c
    tm = min(256, t)

    def body(x_ref, dy_ref, h_ref, g_ref, win_hbm, wout_hbm, dx_ref, dh_ref, act_ref, dg_ref, win_v, wout_v, sem):
        _load_at_first_step(_in_weight_copies(win_hbm, win_v, sem) + _out_weight_copies(wout_hbm, wout_v, sem, N_CHIPS))

        @pl.when(pl.program_id(0) == 0)
        def _():
            dg_ref[...] = jnp.zeros_like(dg_ref)

        xv, dyv, gv = x_ref[...], dy_ref[...], g_ref[...]
        xh, r = _rms_fwd(xv)
        dyh = (0.5 * dyv).astype(BF16)
        dxn = jnp.zeros((tm, d), F32)
        for j in range(2):
            gate = h_ref[:, j * c:(j + 1) * c].astype(F32)
            up = h_ref[:, ff + j * c:ff + (j + 1) * c].astype(F32)
            dact = _dot_nt(dyh, wout_v[j * c:(j + 1) * c, :])
            s = _sigmoid(gate)
            sl = gate * s
            act_ref[:, j * c:(j + 1) * c] = (sl * up).astype(BF16)
            dgate = (dact * up * (s + sl * (1.0 - s))).astype(BF16)
            dup = (dact * sl).astype(BF16)
            dh_ref[:, j * c:(j + 1) * c] = dgate
            dh_ref[:, ff + j * c:ff + (j + 1) * c] = dup
            dxn = dxn + _dot_nt(dgate, win_v[j]) + _dot_nt(dup, win_v[j + 2])
        dg_ref[...] += jnp.sum(dxn * xh, axis=0, keepdims=True)
        dx_ref[...] = dyv + _rms_bwd(dxn, xh, r, gv)

    return _call(
        body, name=name, grid=(t // tm,),
        in_specs=[_row_spec(tm, d), _row_spec(tm, d), _row_spec(tm, 2 * ff), _const_spec((1, d)), HBM_SPEC, HBM_SPEC],
        out_specs=[_row_spec(tm, d), _row_spec(tm, 2 * ff), _row_spec(tm, ff), _const_spec((1, d))],
        out_shape=[jax.ShapeDtypeStruct((t, d), F32), jax.ShapeDtypeStruct((t, 2 * ff), BF16), jax.ShapeDtypeStruct((t, ff), BF16),
                   jax.ShapeDtypeStruct((1, d), F32)],
        scratch_shapes=[pltpu.VMEM((N_CHIPS, d, c), BF16), pltpu.VMEM((ff, d), BF16), pltpu.SemaphoreType.DMA((2 * N_CHIPS,))],
        args=(x, dy, h, g, win_g, wout_g), comms=comms, after=after)


def _norm_matmul(x, g, win_g, name, comms=()):
    t, d = x.shape
    c = win_g.shape[-1]
    tm = min(1024, t)

    def body(x_ref, g_ref, win_hbm, o_ref, xn_ref, win_v, sem):
        _load_at_first_step(_in_weight_copies(win_hbm, win_v, sem))
        xh, _ = _rms_fwd(x_ref[...])
        xn = (xh * g_ref[...]).astype(BF16)
        xn_ref[...] = xn
        for q in range(N_CHIPS):
            o_ref[:, q * c:(q + 1) * c] = _dot(xn, win_v[q])

    return _call(
        body, name=name, grid=(t // tm,),
        in_specs=[_row_spec(tm, d), _const_spec((1, d)), HBM_SPEC],
        out_specs=[_row_spec(tm, N_CHIPS * c), _row_spec(tm, d)],
        out_shape=[jax.ShapeDtypeStruct((t, N_CHIPS * c), F32), jax.ShapeDtypeStruct((t, d), BF16)],
        scratch_shapes=[pltpu.VMEM((N_CHIPS, d, c), BF16), pltpu.SemaphoreType.DMA((N_CHIPS,))],
        args=(x, g, win_g), comms=comms)


def _proj_in_bwd_tail(dh, win_v, c):
    dxn = _dot_nt(dh[:, 0:c], win_v[0])
    for q in range(1, N_CHIPS):
        dxn = dxn + _dot_nt(dh[:, q * c:(q + 1) * c], win_v[q])
    return dxn


def _prev_halo_spec(tm, cols):
    return pl.BlockSpec((HALO, cols), lambda i: (jnp.maximum(i * (tm // HALO) - 1, 0), 0))


def _next_halo_spec(tm, cols, t):
    last = t // HALO - 1
    return pl.BlockSpec((HALO, cols), lambda i: (jnp.minimum((i + 1) * (tm // HALO), last), 0))


def _shift_down(w, k):
    return w if k == 0 else pltpu.roll(w, k, 0)


def _shift_up(w, k):
    return w if k == 0 else pltpu.roll(w, w.shape[0] - k, 0)


def _pool_counts(i, tm):
    pos = (i * tm + lax.broadcasted_iota(jnp.int32, (tm, POOL_CH), 0) + 1).astype(F32)
    lane = lax.broadcasted_iota(jnp.int32, (tm, POOL_CH), 1)
    win = jnp.where(lane < POOL_GC, 2.0, jnp.where(lane < 2 * POOL_GC, 4.0, jnp.where(lane < 3 * POOL_GC, 8.0, 16.0)))
    return jnp.minimum(pos, win)


def _group_select(parts):
    return jnp.concatenate([p[:, k * POOL_GC:(k + 1) * POOL_GC] for k, p in enumerate(parts)], axis=1)


TAP_ROWS = 128


def _tap_blocks(tm, block_fn):
    cols = []
    for ch in range(CONV_CH // POOL_GC):
        lanes = slice(ch * POOL_GC, (ch + 1) * POOL_GC)
        cols.append(jnp.concatenate([block_fn(r, lanes) for r in range(tm // TAP_ROWS)], axis=0))
    return jnp.concatenate(cols, axis=1)


def _conv_block(win, taps):
    acc = jnp.zeros((TAP_ROWS, win.shape[1]), F32)
    for k in range(CONV_WIDTH):
        acc = acc + taps[k:k + 1, :] * _shift_down(win, CONV_WIDTH - 1 - k)[HALO:, :]
    return acc


def _conv_block_transposed(win, taps):
    acc = jnp.zeros((TAP_ROWS, win.shape[1]), F32)
    for k in range(CONV_WIDTH):
        acc = acc + taps[k:k + 1, :] * _shift_up(win, CONV_WIDTH - 1 - k)[0:TAP_ROWS, :]
    return acc


def _mix0_recompute(i, tm, h_cur, h_prev, conv_w, conv_b, y=None):
    prev = jnp.where(i > 0, h_prev, 0.0)
    win = jnp.concatenate([prev, h_cur], axis=0)
    u_w = win[:, 0:POOL_CH]
    a_w = win[:, POOL_CH:POOL_CH + CONV_CH]
    gt_w = win[:, POOL_CH + CONV_CH:]
    g_w = a_w * _sigmoid(gt_w)
    if y is None:
        y = _tap_blocks(tm, lambda r, lanes: _conv_block(g_w[r * TAP_ROWS:(r + 1) * TAP_ROWS + HALO, lanes], conv_w[:, lanes])) + conv_b
    s2 = u_w + _shift_down(u_w, 1)
    s4 = s2 + _shift_down(s2, 2)
    s8 = s4 + _shift_down(s4, 4)
    s16 = s8 + _shift_down(s8, 8)
    sums = _group_select([s2[HALO:], s4[HALO:], s8[HALO:], s16[HALO:]])
    cnt = _pool_counts(i, tm)
    pooled = sums / cnt - h_cur[:, 0:POOL_CH]
    return g_w, y, pooled, cnt


def _pool_linear(pooled, pw_ref, pb):
    return jnp.concatenate(
        [_dot(pooled[:, k * POOL_GC:(k + 1) * POOL_GC].astype(BF16), pw_ref[k].astype(BF16)) for k in range(len(POOL_WINDOWS))], axis=1) + pb


def _mix0_fwd(x, h0, pool_w, pool_b, pool_scale, conv_w, conv_b, ln_g, ln_b, wout_g, name, comms=()):
    t, d = x.shape
    tm = min(512, t)
    hc = h0.shape[1]

    def body(x_ref, h_ref, hp_ref, pw_ref, pb_ref, ps_ref, cw_ref, cb_ref, lg_ref, lb_ref, wout_hbm, xo_ref, ycat_ref, y_ref, wout_v, sem):
        _load_at_first_step(_out_weight_copies(wout_hbm, wout_v, sem))
        i = pl.program_id(0)
        _, y, pooled, _ = _mix0_recompute(i, tm, h_ref[...], hp_ref[...], cw_ref[...], cb_ref[...])
        y_ref[...] = y
        yhat, _ = _ln_fwd(y)
        yn = yhat * lg_ref[...] + lb_ref[...]
        yb = yn * _sigmoid(yn)
        ya = _pool_linear(pooled, pw_ref, pb_ref[...]) * ps_ref[...]
        ycat = jnp.concatenate([ya, yb], axis=1).astype(BF16)
        ycat_ref[...] = ycat
        xo_ref[...] = x_ref[...] + _dot(ycat, wout_v[...])

    return _call(
        body, name=name, grid=(t // tm,),
        in_specs=[_row_spec(tm, d), _row_spec(tm, hc), _prev_halo_spec(tm, hc), _const_spec(pool_w.shape), _const_spec((1, POOL_CH)),
                  _const_spec((1, POOL_CH)), _const_spec(conv_w.shape), _const_spec((1, CONV_CH)), _const_spec((1, CONV_CH)),
                  _const_spec((1, CONV_CH)), HBM_SPEC],
        out_specs=[_row_spec(tm, d), _row_spec(tm, d), _row_spec(tm, CONV_CH)],
        out_shape=[jax.ShapeDtypeStruct((t, d), F32), jax.ShapeDtypeStruct((t, d), BF16), jax.ShapeDtypeStruct((t, CONV_CH), F32)],
        scratch_shapes=[pltpu.VMEM((d, d), BF16), pltpu.SemaphoreType.DMA((N_CHIPS,))],
        args=(x, h0, h0, pool_w, pool_b, pool_scale, conv_w, conv_b, ln_g, ln_b, wout_g), comms=comms)


def _mix0_bwd_a(dy, h0, y_conv, pool_w, pool_b, pool_scale, conv_w, conv_b, ln_g, ln_b, wout_g, name):
    t, d = dy.shape
    tm = min(512, t)
    hc = h0.shape[1]
    n_small = 40

    def body(dy_ref, h_ref, hp_ref, y_ref, pw_ref, pb_ref, ps_ref, cw_ref, cb_ref, lg_ref, lb_ref, wout_hbm,
             dconv_ref, dpc_ref, dpw_ref, small_ref, wout_v, sem):
        _load_at_first_step(_out_weight_copies(wout_hbm, wout_v, sem))
        i = pl.program_id(0)

        @pl.when(i == 0)
        def _():
            dpw_ref[...] = jnp.zeros_like(dpw_ref)
            small_ref[...] = jnp.zeros_like(small_ref)

        g_w, y, pooled, cnt = _mix0_recompute(i, tm, h_ref[...], hp_ref[...], cw_ref[...], cb_ref[...], y_ref[...])
        yhat, rs = _ln_fwd(y)
        lg = lg_ref[...]
        yn = yhat * lg + lb_ref[...]
        mixed = _pool_linear(pooled, pw_ref, pb_ref[...])
        dycat = _dot_nt(dy_ref[...].astype(BF16), wout_v[...])
        dya, dyb = dycat[:, 0:POOL_CH], dycat[:, POOL_CH:]
        sg = _sigmoid(yn)
        dyn = dyb * (sg * (1.0 + yn * (1.0 - sg)))
        dyc = _ln_bwd(dyn * lg, yhat, rs)
        dconv_ref[...] = dyc

        def add_row(k, value):
            small_ref[k:k + 1, :] += jnp.sum(value, axis=0, keepdims=True)

        for ch in range(CONV_CH // POOL_GC):
            lanes = slice(ch * POOL_GC, (ch + 1) * POOL_GC)
            taps = [jnp.zeros((1, POOL_GC), F32)] * CONV_WIDTH
            for r in range(tm // TAP_ROWS):
                win = g_w[r * TAP_ROWS:(r + 1) * TAP_ROWS + HALO, lanes]
                d = dyc[r * TAP_ROWS:(r + 1) * TAP_ROWS, lanes]
                for k in range(CONV_WIDTH):
                    taps[k] = taps[k] + jnp.sum(d * _shift_down(win, CONV_WIDTH - 1 - k)[HALO:, :], axis=0, keepdims=True)
            for k in range(CONV_WIDTH):
                small_ref[k:k + 1, lanes] += taps[k]
        add_row(32, dyc)
        add_row(33, dyn * yhat)
        add_row(34, dyn)
        scale = ps_ref[...]
        dmixed = dya * scale
        add_row(35, dya * mixed)
        add_row(36, dmixed)
        dmb = dmixed.astype(BF16)
        dpooled = []
        for k in range(len(POOL_WINDOWS)):
            sl = slice(k * POOL_GC, (k + 1) * POOL_GC)
            dpw_ref[k] += _dot_tn(pooled[:, sl].astype(BF16), dmb[:, sl])
            dpooled.append(_dot_nt(dmb[:, sl], pw_ref[k].astype(BF16)))
        dpc_ref[...] = jnp.concatenate(dpooled, axis=1) / cnt

    return _call(
        body, name=name, grid=(t // tm,),
        in_specs=[_row_spec(tm, d), _row_spec(tm, hc), _prev_halo_spec(tm, hc), _row_spec(tm, CONV_CH), _const_spec(pool_w.shape),
                  _const_spec((1, POOL_CH)), _const_spec((1, POOL_CH)), _const_spec(conv_w.shape), _const_spec((1, CONV_CH)),
                  _const_spec((1, CONV_CH)), _const_spec((1, CONV_CH)), HBM_SPEC],
        out_specs=[_row_spec(tm, CONV_CH), _row_spec(tm, POOL_CH), _const_spec(pool_w.shape), _const_spec((n_small, CONV_CH))],
        out_shape=[jax.ShapeDtypeStruct((t, CONV_CH), F32), jax.ShapeDtypeStruct((t, POOL_CH), F32),
                   jax.ShapeDtypeStruct(pool_w.shape, F32), jax.ShapeDtypeStruct((n_small, CONV_CH), F32)],
        scratch_shapes=[pltpu.VMEM((d, d), BF16), pltpu.SemaphoreType.DMA((N_CHIPS,))],
        args=(dy, h0, h0, y_conv, pool_w, pool_b, pool_scale, conv_w, conv_b, ln_g, ln_b, wout_g))


def _mix0_bwd_b(x, dy, h0, dconv, dpc, g, conv_w, win_g, name):
    t, d = x.shape
    tm = min(512, t)
    hc = h0.shape[1]
    c = win_g.shape[-1]
    n_tiles = t // tm

    def body(x_ref, dy_ref, h_ref, dc_ref, dcn_ref, dp_ref, dpn_ref, g_ref, cw_ref, win_hbm, dx_ref, dh_ref, dg_ref, win_v, sem):
        _load_at_first_step(_in_weight_copies(win_hbm, win_v, sem))
        i = pl.program_id(0)

        @pl.when(i == 0)
        def _():
            dg_ref[...] = jnp.zeros_like(dg_ref)

        not_last = i < n_tiles - 1
        dc_w = jnp.concatenate([dc_ref[...], jnp.where(not_last, dcn_ref[...], 0.0)], axis=0)
        dp_w = jnp.concatenate([dp_ref[...], jnp.where(not_last, dpn_ref[...], 0.0)], axis=0)
        cw = cw_ref[...]
        dg = _tap_blocks(tm, lambda r, lanes: _conv_block_transposed(dc_w[r * TAP_ROWS:(r + 1) * TAP_ROWS + HALO, lanes], cw[:, lanes]))
        a2 = dp_w + _shift_up(dp_w, 1)
        a4 = a2 + _shift_up(a2, 2)
        a8 = a4 + _shift_up(a4, 4)
        a16 = a8 + _shift_up(a8, 8)
        back = _group_select([a2[0:tm], a4[0:tm], a8[0:tm], a16[0:tm]])
        du = back - dp_ref[...] * _pool_counts(i, tm)
        hv = h_ref[...]
        a = hv[:, POOL_CH:POOL_CH + CONV_CH]
        sig = _sigmoid(hv[:, POOL_CH + CONV_CH:])
        dh = jnp.concatenate([du, dg * sig, dg * a * sig * (1.0 - sig)], axis=1).astype(BF16)
        dh_ref[...] = dh
        dxn = _proj_in_bwd_tail(dh, win_v, c)
        xh, r = _rms_fwd(x_ref[...])
        dg_ref[...] += jnp.sum(dxn * xh, axis=0, keepdims=True)
        dx_ref[...] = dy_ref[...] + _rms_bwd(dxn, xh, r, g_ref[...])

    return _call(
        body, name=name, grid=(n_tiles,),
        in_specs=[_row_spec(tm, d), _row_spec(tm, d), _row_spec(tm, hc), _row_spec(tm, CONV_CH), _next_halo_spec(tm, CONV_CH, t),
                  _row_spec(tm, POOL_CH), _next_halo_spec(tm, POOL_CH, t), _const_spec((1, d)), _const_spec(conv_w.shape), HBM_SPEC],
        out_specs=[_row_spec(tm, d), _row_spec(tm, hc), _const_spec((1, d))],
        out_shape=[jax.ShapeDtypeStruct((t, d), F32), jax.ShapeDtypeStruct((t, hc), BF16), jax.ShapeDtypeStruct((1, d), F32)],
        scratch_shapes=[pltpu.VMEM((N_CHIPS, d, c), BF16), pltpu.SemaphoreType.DMA((N_CHIPS,))],
        args=(x, dy, h0, dconv, dconv, dpc, dpc, g, conv_w, win_g))


SQRT_HALF = 0.7071067811865476
INV_SQRT_2PI = 0.3989422804014327


def _causal_mask():
    return (lax.broadcasted_iota(jnp.int32, (CHUNK, CHUNK), 1) <= lax.broadcasted_iota(jnp.int32, (CHUNK, CHUNK), 0)).astype(F32)


def _sgu_recompute(pre, lg, lb):
    half = pre.shape[1] // 2
    phi = 0.5 * (1.0 + lax.erf(pre * SQRT_HALF))
    z = pre * phi
    u, v = z[:, 0:half], z[:, half:]
    vhat, rs = _ln_fwd(v)
    return u, vhat, rs, vhat * lg + lb, phi


def _sgu_spatial(vln, w_ref, bt, tm):
    mask = _causal_mask()
    wm = [(w_ref[hd] * mask).astype(BF16) for hd in range(SGU_HEADS)]
    vb = vln.astype(BF16)
    rows = []
    for ch in range(tm // CHUNK):
        blocks = [_dot(wm[hd], vb[ch * CHUNK:(ch + 1) * CHUNK, hd * CHUNK:(hd + 1) * CHUNK]) + bt[:, hd:hd + 1] for hd in range(SGU_HEADS)]
        rows.append(jnp.concatenate(blocks, axis=1))
    return jnp.concatenate(rows, axis=0), wm


def _sgu_fwd(x, pre, ln_g, ln_b, w, bt, wout_g, name):
    t, d = x.shape
    tm = min(512, t)
    pc = pre.shape[1]

    def body(x_ref, pre_ref, lg_ref, lb_ref, w_ref, bt_ref, wout_hbm, xo_ref, p_ref, wout_v, sem):
        _load_at_first_step(_out_weight_copies(wout_hbm, wout_v, sem))
        u, _, _, vln, _ = _sgu_recompute(pre_ref[...], lg_ref[...], lb_ref[...])
        vo, _ = _sgu_spatial(vln, w_ref, bt_ref[...], tm)
        p = (u * vo).astype(BF16)
        p_ref[...] = p
        xo_ref[...] = x_ref[...] + _dot(p, wout_v[...])

    return _call(
        body, name=name, grid=(t // tm,),
        in_specs=[_row_spec(tm, d), _row_spec(tm, pc), _const_spec((1, d)), _const_spec((1, d)), _const_spec(w.shape),
                  _const_spec(bt.shape), HBM_SPEC],
        out_specs=[_row_spec(tm, d), _row_spec(tm, d)],
        out_shape=[jax.ShapeDtypeStruct((t, d), F32), jax.ShapeDtypeStruct((t, d), BF16)],
        scratch_shapes=[pltpu.VMEM((d, d), BF16), pltpu.SemaphoreType.DMA((N_CHIPS,))],
        args=(x, pre, ln_g, ln_b, w, bt, wout_g))


def _sgu_bwd(x, dy, pre, g, ln_g, ln_b, w, bt, win_g, wout_g, name, comms=()):
    t, d = x.shape
    tm = min(512, t)
    pc = pre.shape[1]
    c = win_g.shape[-1]

    def body(x_ref, dy_ref, pre_ref, g_ref, lg_ref, lb_ref, w_ref, bt_ref, win_hbm, wout_hbm,
             dx_ref, dpre_ref, dg_ref, dlg_ref, dlb_ref, dw_ref, dbt_ref, win_v, wout_v, sem):
        _load_at_first_step(_in_weight_copies(win_hbm, win_v, sem) + _out_weight_copies(wout_hbm, wout_v, sem, N_CHIPS))
        i = pl.program_id(0)

        @pl.when(i == 0)
        def _():
            for ref in (dg_ref, dlg_ref, dlb_ref, dw_ref, dbt_ref):
                ref[...] = jnp.zeros_like(ref)

        prev = pre_ref[...]
        lg = lg_ref[...]
        u, vhat, rs, vln, phi = _sgu_recompute(prev, lg, lb_ref[...])
        vo, wm = _sgu_spatial(vln, w_ref, bt_ref[...], tm)
        dp = _dot_nt(dy_ref[...].astype(BF16), wout_v[...])
        du = dp * vo
        dvo = dp * u
        dvob = dvo.astype(BF16)
        vb = vln.astype(BF16)
        head_lane = lax.broadcasted_iota(jnp.int32, (CHUNK, SGU_HEADS), 1)
        dbt = jnp.zeros((CHUNK, SGU_HEADS), F32)
        dw = [jnp.zeros((CHUNK, CHUNK), F32) for _ in range(SGU_HEADS)]
        rows = []
        for ch in range(tm // CHUNK):
            rs_ = slice(ch * CHUNK, (ch + 1) * CHUNK)
            blocks = []
            for hd in range(SGU_HEADS):
                cs = slice(hd * CHUNK, (hd + 1) * CHUNK)
                dbt = dbt + jnp.where(head_lane == hd, jnp.sum(dvo[rs_, cs], axis=1, keepdims=True), 0.0)
                dw[hd] = dw[hd] + _dot_nt(dvob[rs_, cs], vb[rs_, cs])
                blocks.append(_dot_tn(wm[hd], dvob[rs_, cs]))
            rows.append(jnp.concatenate(blocks, axis=1))
        dvln = jnp.concatenate(rows, axis=0)
        mask = _causal_mask()
        for hd in range(SGU_HEADS):
            dw_ref[hd] += dw[hd] * mask
        dbt_ref[...] += dbt
        dlg_ref[...] += jnp.sum(dvln * vhat, axis=0, keepdims=True)
        dlb_ref[...] += jnp.sum(dvln, axis=0, keepdims=True)
        dv = _ln_bwd(dvln * lg, vhat, rs)
        gelu_grad = phi + prev * jnp.exp(-0.5 * prev * prev) * INV_SQRT_2PI
        dpre = (jnp.concatenate([du, dv], axis=1) * gelu_grad).astype(BF16)
        dpre_ref[...] = dpre
        dxn = _proj_in_bwd_tail(dpre, win_v, c)
        xh, r = _rms_fwd(x_ref[...])
        dg_ref[...] += jnp.sum(dxn * xh, axis=0, keepdims=True)
        dx_ref[...] = dy_ref[...] + _rms_bwd(dxn, xh, r, g_ref[...])

    return _call(
        body, name=name, grid=(t // tm,),
        in_specs=[_row_spec(tm, d), _row_spec(tm, d), _row_spec(tm, pc), _const_spec((1, d)), _const_spec((1, d)), _const_spec((1, d)),
                  _const_spec(w.shape), _const_spec(bt.shape), HBM_SPEC, HBM_SPEC],
        out_specs=[_row_spec(tm, d), _row_spec(tm, pc), _const_spec((1, d)), _const_spec((1, d)), _const_spec((1, d)),
                   _const_spec(w.shape), _const_spec(bt.shape)],
        out_shape=[jax.ShapeDtypeStruct((t, d), F32), jax.ShapeDtypeStruct((t, pc), BF16), jax.ShapeDtypeStruct((1, d), F32),
                   jax.ShapeDtypeStruct((1, d), F32), jax.ShapeDtypeStruct((1, d), F32), jax.ShapeDtypeStruct(w.shape, F32),
                   jax.ShapeDtypeStruct(bt.shape, F32)],
        scratch_shapes=[pltpu.VMEM((N_CHIPS, d, c), BF16), pltpu.VMEM((d, d), BF16), pltpu.SemaphoreType.DMA((2 * N_CHIPS,))],
        args=(x, dy, pre, g, ln_g, ln_b, w, bt, win_g, wout_g), comms=comms)


def _tn_matmul(a, b, scale, bm, bn, name, comms=(), after=()):
    t, m = a.shape
    n = b.shape[1]
    tk = min(2048, t)
    bm, bn = min(bm, m), min(bn, n)
    nk = t // tk

    def body(a_ref, b_ref, o_ref, acc_ref):
        k = pl.program_id(2)

        @pl.when(k == 0)
        def _():
            acc_ref[...] = jnp.zeros_like(acc_ref)

        bv = b_ref[...]
        if bv.dtype != BF16:
            bv = (scale * bv).astype(BF16)
        acc_ref[...] += _dot_tn(a_ref[...], bv)

        @pl.when(k == nk - 1)
        def _():
            o_ref[...] = acc_ref[...].astype(BF16)

    return _call(
        body, name=name, grid=(m // bm, n // bn, nk),
        in_specs=[pl.BlockSpec((tk, bm), lambda i, j, k: (k, i)), pl.BlockSpec((tk, bn), lambda i, j, k: (k, j))],
        out_specs=[pl.BlockSpec((bm, bn), lambda i, j, k: (i, j))],
        out_shape=[jax.ShapeDtypeStruct((m, n), BF16)],
        scratch_shapes=[pltpu.VMEM((bm, bn), F32)],
        args=(a, b), comms=comms, after=after)[0]


def _row_tile(rows, cols, budget_bytes=2 * 1024 * 1024):
    best = None
    for cand in range(16, rows + 1, 16):
        if rows % cand == 0 and cand * cols * 4 <= budget_bytes:
            best = cand
    return best or rows


def _scalar_grid(grid, in_specs, out_specs):
    return pltpu.PrefetchScalarGridSpec(num_scalar_prefetch=1, grid=grid, in_specs=in_specs, out_specs=out_specs)


def _cast_into_slot(w, layer, me, name, after=()):
    _, rows, cols = w.shape
    tr = _row_tile(rows, cols)

    def body(me_ref, w_ref, *rest):
        rest[-1][...] = w_ref[...].astype(BF16)

    return pl.pallas_call(
        body, name=name,
        grid_spec=_scalar_grid((rows // tr,), [pl.BlockSpec((None, tr, cols), lambda i, me: (layer, i, 0))]
                               + [pl.BlockSpec(memory_space=pl.ANY)] * len(after),
                               pl.BlockSpec((None, tr, cols), lambda i, me: (me[0], i, 0))),
        out_shape=jax.ShapeDtypeStruct((N_CHIPS, rows, cols), BF16), compiler_params=_cparams())(me, w, *after)


def _add_half(view, other, core, name):
    q, _, r, c = view.shape
    tr = _row_tile(r, c)

    def body(core_ref, a_ref, b_ref, o_ref):
        o_ref[...] = (a_ref[...].astype(F32) + b_ref[...].astype(F32)).astype(BF16)

    return pl.pallas_call(
        body, name=name,
        grid_spec=_scalar_grid((q, r // tr), [pl.BlockSpec((None, None, tr, c), lambda k, i, core: (k, core[0], i, 0)),
                                             pl.BlockSpec((None, tr, c), lambda k, i, core: (k, i, 0))],
                               pl.BlockSpec((None, tr, c), lambda k, i, core: (k, i, 0))),
        out_shape=jax.ShapeDtypeStruct((q, r, c), BF16), compiler_params=_cparams_nd(2))(core, view, other)


def _reduce_piece(partial, staged, me, column_sharded, name):
    _, r, c = staged.shape
    tr = _row_tile(r, c, budget_bytes=1024 * 1024)
    nt = r // tr
    if column_sharded:
        own2d = partial.reshape(r, N_CHIPS * c)
        own_spec = pl.BlockSpec((tr, c), lambda i, me: (i, me[0]))
    else:
        own2d = partial.reshape(N_CHIPS * r, c)
        own_spec = pl.BlockSpec((tr, c), lambda i, me: (me[0] * nt + i, 0))
    ring = [pl.BlockSpec((None, tr, c), lambda i, me, k=k: ((me[0] + k) % N_CHIPS, i, 0)) for k in (1, 2, 3)]

    def body(me_ref, own_ref, s1_ref, s2_ref, s3_ref, o_ref):
        o_ref[...] = ((own_ref[...].astype(F32) + s1_ref[...].astype(F32)) + s2_ref[...].astype(F32)) + s3_ref[...].astype(F32)

    return pl.pallas_call(
        body, name=name, grid_spec=_scalar_grid((nt,), [own_spec] + ring, pl.BlockSpec((tr, c), lambda i, me: (i, 0))),
        out_shape=jax.ShapeDtypeStruct((r, c), F32), compiler_params=_cparams())(me, own2d, staged, staged, staged)


def _sum_leading(s, name):
    n, rows, cols = s.shape
    tr = _row_tile(rows, cols, budget_bytes=1024 * 1024)

    def body(s_ref, o_ref):
        acc = s_ref[0].astype(F32)
        for k in range(1, n):
            acc = acc + s_ref[k].astype(F32)
        o_ref[...] = acc

    return pl.pallas_call(
        body, name=name, grid=(rows // tr,), in_specs=[pl.BlockSpec((n, tr, cols), lambda i: (0, i, 0))], out_specs=_row_spec(tr, cols),
        out_shape=jax.ShapeDtypeStruct((rows, cols), F32), compiler_params=_cparams())(s)


ADAM_C1 = 1.0 / (1.0 - ADAM_B1 ** ADAM_STEP)
ADAM_C2 = 1.0 / (1.0 - ADAM_B2 ** ADAM_STEP)


def _adamw_math(w, g, m, v):
    mn = ADAM_B1 * m + (1.0 - ADAM_B1) * g
    vn = ADAM_B2 * v + (1.0 - ADAM_B2) * (g * g)
    return -ADAM_LR * ((mn * ADAM_C1) / (jnp.sqrt(vn * ADAM_C2) + ADAM_EPS) + ADAM_WD * w), mn, vn


def _adamw(w, g, m, v, name):
    shape = w.shape
    cols = shape[-1] if w.ndim > 1 else 128
    w2, g2, m2, v2 = (a.reshape(-1, cols) for a in (w, g, m, v))
    rows = w2.shape[0]
    tr = _row_tile(rows, cols, budget_bytes=1024 * 1024)

    def body(w_ref, g_ref, m_ref, v_ref, d_ref, mo_ref, vo_ref):
        d_ref[...], mo_ref[...], vo_ref[...] = _adamw_math(w_ref[...], g_ref[...], m_ref[...], v_ref[...])

    spec = _row_spec(tr, cols)
    outs = pl.pallas_call(
        body, name=name, grid=(rows // tr,), in_specs=[spec] * 4, out_specs=[spec] * 3,
        out_shape=[jax.ShapeDtypeStruct((rows, cols), F32)] * 3, compiler_params=_cparams())(w2, g2, m2, v2)
    return tuple(o.reshape(shape) for o in outs)


def _adamw_sharded(w, g_mine, g_sibling, m, v, core, layer, prev, name, comms=(), after=()):
    n_layers, r, c = w.shape
    half = r // 2
    tr = _row_tile(half, c)
    nt = half // tr

    def body(core_ref, w_ref, gm_ref, gs_ref, m_ref, v_ref, *rest):
        g_ref, d_ref, mo_ref, vo_ref = rest[-4:]
        gv = jnp.where(pl.program_id(0) == core_ref[0], gm_ref[...], gs_ref[...])
        g_ref[...] = gv
        d_ref[...], mo_ref[...], vo_ref[...] = _adamw_math(w_ref[...], gv, m_ref[...], v_ref[...])

    full = pl.BlockSpec((None, tr, c), lambda h, i, core: (layer, h * nt + i, 0))
    part = pl.BlockSpec((tr, c), lambda h, i, core: (i, 0))
    args = [w, g_mine, g_sibling, m, v]
    in_specs = [full, part, part, full, full]
    aliases = {}
    if prev is not None:
        aliases = {len(args) + k: k for k in range(4)}
        args += list(prev)
        in_specs += [pl.BlockSpec(memory_space=pl.ANY)] * 4
    args += list(after)
    in_specs += [pl.BlockSpec(memory_space=pl.ANY)] * len(after)
    return _call(body, name=name, grid=(2, nt), in_specs=in_specs, out_specs=[full] * 4, out_shape=[jax.ShapeDtypeStruct(w.shape, F32)] * 4,
                 args=args, comms=comms, scalar=core, aliases=aliases)


BIG_IN = ("ffn1_w_in", "ffn2_w_in", "ab_w_in", "sgu_w_in")
BIG_OUT = ("ffn1_w_out", "ffn2_w_out", "ab_w_out", "sgu_w_out")
BIG = BIG_IN + BIG_OUT


class _Gatherer:
    def __init__(self, slots):
        self.slots = dict(slots)

    def _stage(self, keys, d2d):
        n = len(keys)

        def plan(ins, outs, place):
            x, y, c = place
            me = 2 * x + y
            remote = []
            for a in range(n):
                rows = ins[a].shape[1] // 2

                def half(ref, q, core, rows=rows):
                    return ref.at[q, pl.ds(core * rows, rows), :]

                for (px, py) in _other_chips(x, y):
                    q = 2 * px + py
                    if d2d:
                        remote.append((half(ins[a], q, c), half(outs[a], q, c), (x, y, 1 - c), half(outs[a], q, 1 - c)))
                    else:
                        remote.append((half(ins[a], me, c), half(outs[a], me, c), (px, py, c), half(outs[a], q, c)))
            return remote

        def finish(outs):
            for k, o in zip(keys, outs):
                self.slots[k] = o

        arrays = [self.slots[k] for k in keys]
        return _Exchange(arrays, [_sds(a) for a in arrays], plan, 3 * n, {a: a for a in range(n)}, finish)

    def direct(self, keys):
        n = len(keys)

        def plan(ins, outs, place):
            x, y, c = place
            me = 2 * x + y
            return [(ins[a].at[me], outs[a].at[me], (px, py, c), outs[a].at[2 * px + py]) for a in range(n) for (px, py) in _other_chips(x, y)]

        def finish(outs):
            for k, o in zip(keys, outs):
                self.slots[k] = o

        arrays = [self.slots[k] for k in keys]
        return _Exchange(arrays, [_sds(a) for a in arrays], plan, 3 * n, {a: a for a in range(n)}, finish)

    def ici(self, keys):
        return self._stage(keys, False)

    def d2d(self, keys):
        return self._stage(keys, True)


class _Reducer:
    def __init__(self, me, core):
        self.me, self.core = me, core
        self.views, self.partial, self.staged, self.mine, self.theirs = {}, {}, {}, {}, {}

    def add(self, key, g):
        m, n = g.shape
        if key[0] in BIG_IN:
            self.views[key] = g.reshape(1, 2, m // 2, n)
        else:
            self.views[key] = g.reshape(N_CHIPS, 2, m // (2 * N_CHIPS), n)

    def swap(self, keys):
        views = [self.views[k] for k in keys]

        def plan(ins, outs, place):
            x, y, c = place
            return [(ins[a].at[:, 1 - c], outs[a], (x, y, 1 - c), outs[a]) for a in range(len(keys))]

        def finish(outs):
            for k, v, o in zip(keys, views, outs):
                self.partial[k] = _add_half(v, o, self.core, f"chip_partial_{k[0]}_{k[1]}")

        shapes = [jax.ShapeDtypeStruct((v.shape[0],) + v.shape[2:], v.dtype) for v in views]
        return _Exchange(views, shapes, plan, len(keys), None, finish)

    def scatter(self, keys, part=(0, 1)):
        i, n = part
        n_keys = len(keys)
        parts = [self.partial[k] for k in keys]
        shapes = []
        for k, p in zip(keys, parts):
            q, r, c = p.shape
            shapes.append(jax.ShapeDtypeStruct((N_CHIPS, r, c // N_CHIPS if k[0] in BIG_IN else c), p.dtype))

        def piece(ref, key, q, rows, cols):
            return ref.at[0, rows, pl.ds(q * cols, cols)] if key[0] in BIG_IN else ref.at[q, rows, :]

        def plan(ins, outs, place):
            x, y, c = place
            me = 2 * x + y
            remote = []
            for a, k in enumerate(keys):
                _, r, cols = shapes[a].shape
                rows = pl.ds(i * (r // n), r // n)
                for (px, py) in _other_chips(x, y):
                    q = 2 * px + py
                    remote.append((piece(ins[a], k, q, rows, cols), outs[a].at[me, rows, :], (px, py, c), outs[a].at[q, rows, :]))
            return remote

        def finish(outs):
            for k, p, o in zip(keys, parts, outs):
                self.staged[k] = o
                if i == n - 1:
                    self.mine[k] = _reduce_piece(p, o, self.me, k[0] in BIG_IN, f"reduce_{k[0]}_{k[1]}")

        inputs, aliases = parts, None
        if i > 0:
            inputs = parts + [self.staged[k] for k in keys]
            aliases = {n_keys + a: a for a in range(n_keys)}
        return _Exchange(inputs, shapes, plan, 3 * n_keys, aliases, finish)

    def scatter_behind(self, keys, work, tag):
        n = len(keys)
        parts = [self.partial[k] for k in keys]
        lands = []
        for k, p in zip(keys, parts):
            _, r, c = p.shape
            lands.append(jax.ShapeDtypeStruct((N_CHIPS, r, c // N_CHIPS if k[0] in BIG_IN else c), p.dtype))
        sem_spec = pl.BlockSpec(memory_space=pltpu.SEMAPHORE)
        effect = pltpu.CompilerParams(has_side_effects=pltpu.SideEffectType.DATAFLOW_SIDE_EFFECTING)

        def copies(part_refs, land_refs, send_sems, recv_sems):
            x, y, c = _my_place()
            me = 2 * x + y
            out = []
            for a, k in enumerate(keys):
                cols = lands[a].shape[2]
                for j, (px, py) in enumerate(_other_chips(x, y)):
                    q = 2 * px + py
                    src = part_refs[a].at[0, :, pl.ds(q * cols, cols)] if k[0] in BIG_IN else part_refs[a].at[q]
                    sems = dict(send_sem=send_sems.at[3 * a + j], recv_sem=recv_sems.at[3 * a + j], device_id=(px, py, c), device_id_type=MESH)
                    out.append((pltpu.make_async_remote_copy(src_ref=src, dst_ref=land_refs[a].at[me], **sems),
                                pltpu.make_async_remote_copy(src_ref=src, dst_ref=land_refs[a].at[q], **sems)))
            return out

        def start_body(*refs):
            part_refs, land_refs, send_sems, recv_sems, token = refs[:n], refs[n:2 * n], refs[2 * n], refs[2 * n + 1], refs[-1]
            for send, _ in copies(part_refs, land_refs, send_sems, recv_sems):
                send.start()
            token[...] = jnp.zeros_like(token)

        def wait_body(*refs):
            part_refs, land_refs, send_sems, recv_sems = refs[:n], refs[n:2 * n], refs[2 * n], refs[2 * n + 1]
            for send, arrive in copies(part_refs, land_refs, send_sems, recv_sems):
                send.wait_send()
                arrive.wait_recv()

        in_hbm = [pltpu.with_memory_space_constraint(p, pltpu.HBM) for p in parts]
        in_hbm += [pltpu.with_memory_space_constraint(lax.empty(s.shape, s.dtype), pltpu.HBM) for s in lands]
        thru_shapes = [pltpu.HBM(p.shape, p.dtype) for p in parts] + [pltpu.HBM(s.shape, s.dtype) for s in lands]
        started = pl.pallas_call(
            start_body, name=f"scatter_{tag}_start", in_specs=[HBM_SPEC] * (2 * n),
            out_shape=(pltpu.SemaphoreType.DMA((3 * n,)), pltpu.SemaphoreType.DMA((3 * n,)), *thru_shapes, jax.ShapeDtypeStruct((8, 128), F32)),
            out_specs=(sem_spec, sem_spec, *[HBM_SPEC] * (2 * n), pl.BlockSpec(memory_space=pltpu.VMEM)),
            input_output_aliases={i: 2 + i for i in range(2 * n)}, compiler_params=effect)(*in_hbm)
        send_sems, recv_sems, thru, token = started[0], started[1], started[2:2 + 2 * n], started[-1]
        after = work(token)
        done = pl.pallas_call(
            wait_body, name=f"scatter_{tag}_wait", in_specs=[HBM_SPEC] * (2 * n) + [sem_spec, sem_spec] + [pl.BlockSpec(memory_space=pl.ANY)] * len(after),
            out_shape=tuple(thru_shapes), out_specs=tuple([HBM_SPEC] * (2 * n)), input_output_aliases={i: i for i in range(2 * n)},
            compiler_params=effect)(*thru, send_sems, recv_sems, *after)
        for a, k in enumerate(keys):
            self.staged[k] = done[n + a]
            self.mine[k] = _reduce_piece(done[a], done[n + a], self.me, k[0] in BIG_IN, f"reduce_{k[0]}_{k[1]}")

    def exchange(self, keys):
        mine = [self.mine[k] for k in keys]

        def plan(ins, outs, place):
            x, y, c = place
            return [(ins[a], outs[a], (x, y, 1 - c), outs[a]) for a in range(len(keys))]

        def finish(outs):
            for k, o in zip(keys, outs):
                self.theirs[k] = o

        return _Exchange(mine, [_sds(a) for a in mine], plan, len(keys), None, finish)


def _half(ref, q, core):
    rows = ref.shape[1] // 2
    return ref.at[q, pl.ds(core * rows, rows), :]


def _all_gather_full(gat, keys, work):
    n = len(keys)
    arrays = [gat.slots[k] for k in keys]
    sem_spec = pl.BlockSpec(memory_space=pltpu.SEMAPHORE)
    effect = pltpu.CompilerParams(has_side_effects=pltpu.SideEffectType.DATAFLOW_SIDE_EFFECTING)

    def places():
        x, y, c = _my_place()
        return c, (x, y, 1 - c), (1 - x, y, c), (x, 1 - y, c), 2 * x + y, 2 * (1 - x) + y, 2 * x + (1 - y), 2 * (1 - x) + (1 - y)

    def first_hop(refs, send_sems, recv_sems):
        c, _, x_nbr, y_nbr, me, qx, qy, _ = places()
        out = []
        for a in range(n):
            for k, (to, q) in enumerate(((x_nbr, qx), (y_nbr, qy))):
                sems = dict(send_sem=send_sems.at[2 * a + k], recv_sem=recv_sems.at[2 * a + k], device_id=to, device_id_type=MESH)
                out.append((pltpu.make_async_remote_copy(src_ref=_half(refs[a], me, c), dst_ref=_half(refs[a], me, c), **sems),
                            pltpu.make_async_remote_copy(src_ref=_half(refs[a], me, c), dst_ref=_half(refs[a], q, c), **sems)))
        return out

    def start_body(*refs):
        for send, _ in first_hop(refs[:n], refs[n], refs[n + 1]):
            send.start()
        refs[-1][...] = jnp.zeros_like(refs[-1])

    def wait_body(*refs):
        for send, arrive in first_hop(refs[:n], refs[n], refs[n + 1]):
            send.wait_send()
            arrive.wait_recv()

    thru_shapes = [pltpu.HBM(a.shape, a.dtype) for a in arrays]
    started = pl.pallas_call(
        start_body, name="gather_first_hop_start", in_specs=[HBM_SPEC] * n,
        out_shape=(pltpu.SemaphoreType.DMA((2 * n,)), pltpu.SemaphoreType.DMA((2 * n,)), *thru_shapes, jax.ShapeDtypeStruct((8, 128), F32)),
        out_specs=(sem_spec, sem_spec, *[HBM_SPEC] * n, pl.BlockSpec(memory_space=pltpu.VMEM)),
        input_output_aliases={i: 2 + i for i in range(n)}, compiler_params=effect,
    )(*[pltpu.with_memory_space_constraint(a, pltpu.HBM) for a in arrays])
    after = work(started[-1])
    landed_first = pl.pallas_call(
        wait_body, name="gather_first_hop_wait", in_specs=[HBM_SPEC] * n + [sem_spec, sem_spec] + [pl.BlockSpec(memory_space=pl.ANY)] * len(after),
        out_shape=tuple(thru_shapes), out_specs=tuple([HBM_SPEC] * n), input_output_aliases={i: i for i in range(n)},
        compiler_params=effect)(*started[2:2 + n], started[0], started[1], *after)
    per = 5

    def body(*refs):
        outs = refs[n:2 * n]
        send_sems, recv_sems = refs[2 * n:]
        c, sibling, x_nbr, y_nbr, _, qx, qy, qd = places()

        def quarter(ref, q, core, k):
            rows = ref.shape[1] // 4
            return ref.at[q, pl.ds((2 * core + k) * rows, rows), :]

        def copy(a, k, part, to):
            return pltpu.make_async_remote_copy(src_ref=part, dst_ref=part, send_sem=send_sems.at[per * a + k],
                                                recv_sem=recv_sems.at[per * a + k], device_id=to, device_id_type=MESH)

        sent = []

        def send(a, k, part, to):
            cp = copy(a, k, part, to)
            cp.start()
            sent.append(cp)

        for a in range(n):
            send(a, 0, quarter(outs[a], qy, c, 0), x_nbr)
            send(a, 1, quarter(outs[a], qx, c, 1), y_nbr)
            send(a, 2, _half(outs[a], qx, c), sibling)
            send(a, 3, _half(outs[a], qy, c), sibling)
        for a in range(n):
            copy(a, 0, quarter(outs[a], qd, c, 0), sibling).wait_recv()
            copy(a, 1, quarter(outs[a], qd, c, 1), sibling).wait_recv()
            send(a, 4, _half(outs[a], qd, c), sibling)
        for a in range(n):
            for k, q in ((2, qx), (3, qy), (4, qd)):
                copy(a, k, _half(outs[a], q, 1 - c), sibling).wait_recv()
        for cp in sent:
            cp.wait_send()

    outs = pl.pallas_call(
        body, name="gather_first_rest", in_specs=[HBM_SPEC] * n, out_specs=[HBM_SPEC] * n,
        out_shape=[_sds(a) for a in arrays], input_output_aliases={a: a for a in range(n)},
        scratch_shapes=[pltpu.SemaphoreType.DMA((per * n,)), pltpu.SemaphoreType.DMA((per * n,))])(*landed_first)
    for k, o in zip(keys, outs):
        gat.slots[k] = o


def _small_all_gather(buf, done):
    state = {}

    def index(x, y, c):
        return 4 * x + 2 * y + c

    def plan_ici(ins, outs, place):
        x, y, c = place
        return [(ins[0], outs[0].at[index(x, y, c)], (px, py, c), outs[0].at[index(px, py, c)]) for (px, py) in _other_chips(x, y)]

    def local(ins, outs, place):
        return [(ins[0], outs[0].at[index(*place)])]

    def plan_d2d(ins, outs, place):
        x, y, c = place
        return [(ins[0].at[index(px, py, c)], outs[0].at[index(px, py, c)], (x, y, 1 - c), outs[0].at[index(px, py, 1 - c)])
                for (px, py) in [(x, y)] + _other_chips(x, y)]

    def second():
        return _Exchange([state["blocks"]], [_sds(state["blocks"])], plan_d2d, N_CHIPS, {0: 0}, lambda outs: done(outs[0]))

    first = _Exchange([buf], [jax.ShapeDtypeStruct((2 * N_CHIPS,) + buf.shape, buf.dtype)], plan_ici, 3, None,
                      lambda outs: state.update(blocks=outs[0]), local, 1)
    return first, second


WEIGHT_NAMES = ("ffn1_norm", "ffn1_w_in", "ffn1_w_out", "mix_norm", "ffn2_norm", "ffn2_w_in", "ffn2_w_out", "ab_w_in", "pool_w", "pool_b",
                "pool_scale", "conv_w", "conv_b", "conv_ln_g", "conv_ln_b", "ab_w_out", "sgu_w_in", "sgu_ln_g", "sgu_ln_b", "sgu_w", "sgu_b",
                "sgu_w_out", "final_norm")
SMALL = tuple(n for n in WEIGHT_NAMES if n not in BIG)
SHARDED_SMALL = ("conv_w", "sgu_ln_g", "sgu_ln_b")
PACK_ROWS = 64
PACK = ("pack", 0)


def _pair(prefix, layer):
    return [(prefix + "_w_in", layer), (prefix + "_w_out", layer)]


def kernel(x, ffn1_norm, ffn1_w_in, ffn1_w_out, mix_norm, ffn2_norm, ffn2_w_in, ffn2_w_out, ab_w_in, pool_w, pool_b, pool_scale, conv_w, conv_b, conv_ln_g, conv_ln_b, ab_w_out, sgu_w_in, sgu_ln_g, sgu_ln_b, sgu_w, sgu_b, sgu_w_out, final_norm, loss_target, m_ffn1_norm, m_ffn1_w_in, m_ffn1_w_out, m_mix_norm, m_ffn2_norm, m_ffn2_w_in, m_ffn2_w_out, m_ab_w_in, m_pool_w, m_pool_b, m_pool_scale, m_conv_w, m_conv_b, m_conv_ln_g, m_conv_ln_b, m_ab_w_out, m_sgu_w_in, m_sgu_ln_g, m_sgu_ln_b, m_sgu_w, m_sgu_b, m_sgu_w_out, m_final_norm, v_ffn1_norm, v_ffn1_w_in, v_ffn1_w_out, v_mix_norm, v_ffn2_norm, v_ffn2_w_in, v_ffn2_w_out, v_ab_w_in, v_pool_w, v_pool_b, v_pool_scale, v_conv_w, v_conv_b, v_conv_ln_g, v_conv_ln_b, v_ab_w_out, v_sgu_w_in, v_sgu_ln_g, v_sgu_ln_b, v_sgu_w, v_sgu_b, v_sgu_w_out, v_final_norm):
    given = dict(locals())
    w = {n: given[n] for n in WEIGHT_NAMES}
    chip = 2 * lax.axis_index("x") + lax.axis_index("y")
    me = chip.astype(jnp.int32).reshape(1)
    core = lax.axis_index("c").astype(jnp.int32).reshape(1)
    row = lambda v: v.reshape(1, -1)
    xin, tgt = x[0], loss_target[0]

    pack = jnp.concatenate([
        w["conv_w"][0], jnp.zeros((1, 128), F32), w["sgu_ln_g"].reshape(2, 128), w["sgu_ln_b"].reshape(2, 128),
        jnp.zeros((PACK_ROWS - 36, 128), F32)], axis=0)
    first = _pair("ffn1", 0)
    slots = {PACK: lax.dynamic_update_slice(jnp.zeros((N_CHIPS, PACK_ROWS, 128), F32), pack[None], (me[0], 0, 0))}
    for n, layer in first:
        slots[(n, layer)] = _cast_into_slot(w[n], layer, me, f"cast_{n}_{layer}")
    gat = _Gatherer(slots)

    def other_casts(token):
        for n in BIG:
            for layer in range(w[n].shape[0]):
                if (n, layer) not in first:
                    token = gat.slots[(n, layer)] = _cast_into_slot(w[n], layer, me, f"cast_{n}_{layer}", after=[token])
        return [token]

    _all_gather_full(gat, first + [PACK], other_casts)
    gp = gat.slots[PACK]
    conv_w_full = jnp.transpose(gp[:, 0:CONV_WIDTH], (1, 0, 2)).reshape(CONV_WIDTH, N_CHIPS * 128)
    sgu_ln_g_full = gp[:, 32:34].reshape(1, -1)
    sgu_ln_b_full = gp[:, 34:36].reshape(1, -1)
    gw = lambda n, layer: gat.slots[(n, layer)]

    st = [dict(), dict()]
    st[0]["xa"] = xin
    later = _pair("sgu", 0) + _pair("ffn2", 1)
    cur, st[0]["h1"], st[0]["xn1"] = _ffn_fwd(xin, row(w["ffn1_norm"][0]), gw("ffn1_w_in", 0), gw("ffn1_w_out", 0), "ffn1_fwd_0",
                                              comms=[gat.direct(_pair("ab", 0)), gat.ici(_pair("ffn2", 0))])
    st[0]["xb"] = cur
    st[0]["h0"], st[0]["xnm"] = _norm_matmul(cur, row(w["mix_norm"][0]), gw("ab_w_in", 0), "mix0_proj_in",
                                             comms=[gat.d2d(_pair("ffn2", 0)), gat.ici([("ffn1_w_out", 1)])])
    pool_args = (w["pool_w"][0], row(w["pool_b"][0]), row(w["pool_scale"][0]), conv_w_full, row(w["conv_b"][0]), row(w["conv_ln_g"][0]),
                 row(w["conv_ln_b"][0]), gw("ab_w_out", 0))
    cur, st[0]["ycat"], st[0]["yconv"] = _mix0_fwd(cur, st[0]["h0"], *pool_args, "mix0_fwd", comms=[gat.ici([("ffn1_w_in", 1)])])
    st[0]["xc"] = cur
    cur, st[0]["h2"], st[0]["xn2"] = _ffn_fwd(cur, row(w["ffn2_norm"][0]), gw("ffn2_w_in", 0), gw("ffn2_w_out", 0), "ffn2_fwd_0",
                                              comms=[gat.d2d(_pair("ffn1", 1)), gat.ici(later)])
    st[1]["xa"] = cur
    cur, st[1]["h1"], st[1]["xn1"] = _ffn_fwd(cur, row(w["ffn1_norm"][1]), gw("ffn1_w_in", 1), gw("ffn1_w_out", 1), "ffn1_fwd_1",
                                              comms=[gat.d2d(later)])
    st[1]["xb"] = cur
    st[1]["pre"], st[1]["xnm"] = _norm_matmul(cur, row(w["mix_norm"][1]), gw("sgu_w_in", 0), "sgu_proj_in")
    sgu_args = (sgu_ln_g_full, sgu_ln_b_full, w["sgu_w"][0], w["sgu_b"][0].T)
    cur, st[1]["p"] = _sgu_fwd(cur, st[1]["pre"], *sgu_args, gw("sgu_w_out", 0), "sgu_fwd")
    st[1]["xc"] = cur
    dy, st[1]["h2"], st[1]["xn2"], loss, d_final = _ffn_fwd(cur, row(w["ffn2_norm"][1]), gw("ffn2_w_in", 1), gw("ffn2_w_out", 1),
                                                            "ffn2_fwd_1_loss", loss_head=(tgt, row(w["final_norm"])))

    red = _Reducer(me, core)
    small = {"final_norm": d_final.reshape(-1)}
    norm_grads = {"ffn1_norm": [None] * DEPTH, "mix_norm": [None] * DEPTH, "ffn2_norm": [None] * DEPTH}
    ga, gb, gc, gd, ge, gf = _pair("ffn2", 1), _pair("sgu", 0), _pair("ffn1", 1), _pair("ffn2", 0), _pair("ab", 0), _pair("ffn1", 0)

    def ffn_backward(prefix, layer, xs, hs, xns, dy_in, bwd_comms=(), dwin_comms=(), dwout_comms=()):
        dx, dh, act, norm_grads[prefix + "_norm"][layer] = _ffn_bwd(
            xs, dy_in, hs, row(w[prefix + "_norm"][layer]), gw(prefix + "_w_in", layer), gw(prefix + "_w_out", layer),
            f"{prefix}_bwd_{layer}", comms=bwd_comms)
        red.add((prefix + "_w_in", layer), _tn_matmul(xns, dh, 1.0, 1024, 1408, f"{prefix}_dwin_{layer}", comms=dwin_comms))
        red.add((prefix + "_w_out", layer), _tn_matmul(act, dy_in, 0.5, 1408, 1024, f"{prefix}_dwout_{layer}", comms=dwout_comms))
        return dx

    s1, s0 = st[1], st[0]
    dy = ffn_backward("ffn2", 1, s1["xc"], s1["h2"], s1["xn2"], dy)
    dy_in = dy
    dy, dpre, norm_grads["mix_norm"][1], dlg, dlb, dw, dbt = _sgu_bwd(
        s1["xb"], dy_in, s1["pre"], row(w["mix_norm"][1]), *sgu_args, gw("sgu_w_in", 0), gw("sgu_w_out", 0), "sgu_bwd", comms=[red.swap(ga)])
    red.add(("sgu_w_in", 0), _tn_matmul(s1["xnm"], dpre, 1.0, 1024, 2048, "sgu_dwin"))
    red.add(("sgu_w_out", 0), _tn_matmul(s1["p"], dy_in, 1.0, 1024, 1024, "sgu_dwout"))
    small.update(sgu_ln_g=dlg, sgu_ln_b=dlb, sgu_w=dw[None], sgu_b=dbt.T[None])
    dy = ffn_backward("ffn1", 1, s1["xa"], s1["h1"], s1["xn1"], dy, bwd_comms=[lambda: red.scatter(ga), lambda: red.swap(gb)],
                      dwin_comms=[lambda: red.scatter(gb), lambda: red.exchange(ga)])
    dy = ffn_backward("ffn2", 0, s0["xc"], s0["h2"], s0["xn2"], dy, bwd_comms=[lambda: red.swap(gc), lambda: red.exchange(gb)],
                      dwin_comms=[lambda: red.scatter(gc)])
    dy_in = dy
    dconv, dpc, dpw, rows = _mix0_bwd_a(dy_in, s0["h0"], s0["yconv"], *pool_args, "mix0_bwd_a")
    dy, dh0, norm_grads["mix_norm"][0] = _mix0_bwd_b(s0["xb"], dy_in, s0["h0"], dconv, dpc, row(w["mix_norm"][0]), conv_w_full,
                                                      gw("ab_w_in", 0), "mix0_bwd_b")
    red.add(("ab_w_in", 0), _tn_matmul(s0["xnm"], dh0, 1.0, 1024, 1536, "ab_dwin", comms=[red.swap(gd), red.exchange(gc)]))
    small.update(pool_w=dpw[None], conv_w=rows[None, 0:CONV_WIDTH], conv_b=rows[32:33], conv_ln_g=rows[33:34], conv_ln_b=rows[34:35],
                 pool_scale=rows[35:36], pool_b=rows[36:37].reshape(1, len(POOL_WINDOWS), POOL_GC))

    last = {}

    def behind_ffn2_scatter(token):
        red.add(("ab_w_out", 0), _tn_matmul(s0["ycat"], dy_in, 1.0, 1024, 1024, "ab_dwout", after=[token]))
        last["dx"], last["dh"], last["act"], norm_grads["ffn1_norm"][0] = _ffn_bwd(
            s0["xa"], dy, s0["h1"], row(w["ffn1_norm"][0]), gw("ffn1_w_in", 0), gw("ffn1_w_out", 0), "ffn1_bwd_0", after=[token])
        return [norm_grads["ffn1_norm"][0]]

    red.scatter_behind(gd, behind_ffn2_scatter, "ffn2_0")

    small_sum = {}

    def small_ready():
        for k, v in norm_grads.items():
            small[k] = jnp.concatenate(v, axis=0)
        flat = [small[n].reshape(-1, 128) for n in SMALL]
        rows = sum(f.shape[0] for f in flat)
        loss_block = jnp.pad(loss, ((0, 8 + (-rows) % 8 - 1), (0, 127)))
        buf = jnp.concatenate(flat + [loss_block], axis=0)

        def done(gathered):
            total, at = _sum_leading(gathered, "reduce_small"), 0
            for n, f in zip(SMALL, flat):
                small_sum[n] = total[at:at + f.shape[0]].reshape(small[n].shape)
                at += f.shape[0]
            small_sum["loss"] = total[at:at + 1, 0:1]

        return _small_all_gather(buf, done)

    small_first, small_second = small_ready()
    red.add(("ffn1_w_in", 0), _tn_matmul(s0["xn1"], last["dh"], 1.0, 1024, 1408, "ffn1_dwin_0", comms=[red.swap(ge), small_first]))
    red.add(("ffn1_w_out", 0), _tn_matmul(last["act"], dy, 0.5, 1408, 1024, "ffn1_dwout_0",
                                          comms=[red.scatter(ge), red.exchange(gd), small_second, red.swap(gf[:1])]))
    grad_x = last["dx"]

    big_out = {}

    def adamw_big(n, layer, after=()):
        big_out[n] = _adamw_sharded(w[n], red.mine[(n, layer)], red.theirs[(n, layer)], given["m_" + n], given["v_" + n], core, layer,
                                    big_out.get(n), f"adamw_{n}_{layer}", after=after)

    _exchange_alone("swap_last_grads", [red.swap(gf[1:]), red.exchange(ge)])

    def other_updates(token):
        for n in BIG:
            for layer in reversed(range(w[n].shape[0])):
                if (n, layer) not in gf:
                    adamw_big(n, layer, after=[token])
        return [big_out[n][0] for n in BIG]

    red.scatter_behind(gf, other_updates, "last")
    _exchange_alone("exchange_last_grads", [red.exchange(gf)])
    for key in gf:
        adamw_big(*key)

    loss = small_sum["loss"][0, 0]
    grads, delta, new_m, new_v = {}, {}, {}, {}
    for n in WEIGHT_NAMES:
        mom, var = given["m_" + n], given["v_" + n]
        if n in BIG:
            grads[n], delta[n], new_m[n], new_v[n] = big_out[n]
            continue
        g = small_sum[n]
        if n in SHARDED_SMALL:
            width = w[n].shape[-1]
            g = lax.dynamic_slice_in_dim(g, chip * width, width, axis=g.ndim - 1)
        grads[n] = g
        delta[n], new_m[n], new_v[n] = _adamw(w[n], g, mom, var, f"adamw_{n}")
    return (loss, grad_x[None], *[grads[n] for n in WEIGHT_NAMES], *[delta[n] for n in WEIGHT_NAMES],
            *[new_m[n] for n in WEIGHT_NAMES], *[new_v[n] for n in WEIGHT_NAMES])
```

```python
import jax
import jax.numpy as jnp
from jax import lax
from jax.experimental import pallas as pl
from jax.experimental.pallas import tpu as pltpu

F32, BF16 = jnp.float32, jnp.bfloat16
EPS = 1e-6
N_CHIPS = 4
POOL_WINDOWS = (2, 4, 8, 16)
POOL_GC = 128
POOL_CH = 512
CONV_CH = 512
CONV_WIDTH = 31
HALO = 32
SGU_HEADS = 8
CHUNK = 128
DEPTH = 2
ADAM_LR, ADAM_B1, ADAM_B2, ADAM_EPS, ADAM_WD, ADAM_STEP = 0.001, 0.9, 0.999, 1e-08, 0.01, 10
VMEM_LIMIT_BYTES = 60 * 1024 * 1024
MESH_AXES = ("x", "y", "c")
MESH = pl.DeviceIdType.MESH
HBM_SPEC = pl.BlockSpec(memory_space=pltpu.HBM)


def _sds(a):
    return jax.ShapeDtypeStruct(a.shape, a.dtype)


def _cparams_nd(n):
    return pltpu.CompilerParams(dimension_semantics=("arbitrary",) * n, vmem_limit_bytes=VMEM_LIMIT_BYTES)


def _cparams():
    return _cparams_nd(1)


def _dot(a, b):
    return jnp.dot(a, b, preferred_element_type=F32)


def _dot_nt(a, b):
    return lax.dot_general(a, b, (((1,), (1,)), ((), ())), preferred_element_type=F32)


def _dot_tn(a, b):
    return lax.dot_general(a, b, (((0,), (0,)), ((), ())), preferred_element_type=F32)


def _rms_fwd(x):
    r = lax.rsqrt(jnp.mean(x * x, axis=-1, keepdims=True) + EPS)
    return x * r, r


def _rms_bwd(dxn, xh, r, g):
    dxh = dxn * g
    return r * (dxh - xh * jnp.mean(dxh * xh, axis=-1, keepdims=True))


def _ln_fwd(y):
    mu = jnp.mean(y, axis=-1, keepdims=True)
    yc = y - mu
    rs = lax.rsqrt(jnp.mean(yc * yc, axis=-1, keepdims=True) + EPS)
    return yc * rs, rs


def _ln_bwd(dyhat, yhat, rs):
    return rs * (dyhat - jnp.mean(dyhat, axis=-1, keepdims=True) - yhat * jnp.mean(dyhat * yhat, axis=-1, keepdims=True))


def _sigmoid(x):
    return 0.5 * jnp.tanh(0.5 * x) + 0.5


def _const_spec(shape):
    n = len(shape)
    return pl.BlockSpec(shape, lambda i: (0,) * n)


def _row_spec(tm, cols):
    return pl.BlockSpec((tm, cols), lambda i: (i, 0))


def _my_place():
    return lax.axis_index("x"), lax.axis_index("y"), lax.axis_index("c")


def _other_chips(x, y):
    return [(1 - x, y), (x, 1 - y), (1 - x, 1 - y)]


class _Exchange:
    def __init__(self, inputs, out_shapes, plan, count, aliases=None, finish=None, local=None, n_local=0):
        self.inputs, self.out_shapes, self.plan, self.count = list(inputs), list(out_shapes), plan, count
        self.aliases, self.finish, self.local, self.n_local = dict(aliases or {}), finish, local, n_local


def _call(body, *, name, grid, in_specs, out_specs, out_shape, args, scratch_shapes=(), comms=(), scalar=None, aliases=None, after=()):
    comms = [cm if isinstance(cm, _Exchange) else cm() for cm in comms]
    in_specs, out_specs, out_shape, scratch_shapes = list(in_specs), list(out_specs), list(out_shape), list(scratch_shapes)
    n_body_in = len(in_specs)
    in_specs += [pl.BlockSpec(memory_space=pl.ANY)] * len(after)
    args = list(args) + list(after)
    n_in, n_out, n_scr = len(in_specs), len(out_specs), len(scratch_shapes)
    c_in = [a for cm in comms for a in cm.inputs]
    c_out = [s for cm in comms for s in cm.out_shapes]
    n_remote = sum(cm.count for cm in comms)
    n_local = sum(cm.n_local for cm in comms)
    n_scalar = 0 if scalar is None else 1
    all_aliases = {n_scalar + i: o for i, o in (aliases or {}).items()}
    at_in, at_out = n_scalar + n_in, n_out
    for cm in comms:
        for i, o in cm.aliases.items():
            all_aliases[at_in + i] = at_out + o
        at_in += len(cm.inputs)
        at_out += len(cm.out_shapes)

    def wrapped(*all_refs):
        scalar_ref, refs = all_refs[:n_scalar], all_refs[n_scalar:]
        ins, ci = refs[:n_in], refs[n_in:n_in + len(c_in)]
        at = n_in + len(c_in)
        outs, co = refs[at:at + n_out], refs[at + n_out:at + n_out + len(c_out)]
        at += n_out + len(c_out)
        scr = refs[at:at + n_scr]

        def run_body():
            body(*scalar_ref, *ins[:n_body_in], *outs, *scr)

        if not comms:
            run_body()
            return
        send_sems, recv_sems, local_sems = refs[at + n_scr:]
        place = _my_place()
        sends, arrivals, locals_ = [], [], []
        i0 = o0 = 0
        for cm in comms:
            cm_in, cm_out = ci[i0:i0 + len(cm.inputs)], co[o0:o0 + len(cm.out_shapes)]
            i0 += len(cm.inputs)
            o0 += len(cm.out_shapes)
            for src, dst, dev, incoming in cm.plan(cm_in, cm_out, place):
                k = len(sends)
                sends.append(pltpu.make_async_remote_copy(src_ref=src, dst_ref=dst, send_sem=send_sems.at[k], recv_sem=recv_sems.at[k],
                                                          device_id=dev, device_id_type=MESH))
                arrivals.append(pltpu.make_async_remote_copy(src_ref=src, dst_ref=incoming, send_sem=send_sems.at[k],
                                                             recv_sem=recv_sems.at[k], device_id=dev, device_id_type=MESH))
            if cm.local is not None:
                for src, dst in cm.local(cm_in, cm_out, place):
                    locals_.append(pltpu.make_async_copy(src, dst, local_sems.at[len(locals_)]))

        def start():
            for cp in locals_ + sends:
                cp.start()

        def finish():
            for cp in arrivals:
                cp.wait_recv()
            for cp in sends:
                cp.wait_send()
            for cp in locals_:
                cp.wait()

        if not grid:
            start()
            run_body()
            finish()
            return
        ids = [pl.program_id(a) for a in range(len(grid))]
        first, last = ids[0] == 0, ids[0] == grid[0] - 1
        for a in range(1, len(grid)):
            first = jnp.logical_and(first, ids[a] == 0)
            last = jnp.logical_and(last, ids[a] == grid[a] - 1)
        pl.when(first)(start)
        run_body()
        pl.when(last)(finish)

    sems = []
    if comms:
        sems = [pltpu.SemaphoreType.DMA((max(n_remote, 1),)), pltpu.SemaphoreType.DMA((max(n_remote, 1),)),
                pltpu.SemaphoreType.DMA((max(n_local, 1),))]
    all_in, all_out = in_specs + [HBM_SPEC] * len(c_in), out_specs + [HBM_SPEC] * len(c_out)
    if scalar is None:
        kwargs = dict(grid=grid, compiler_params=_cparams_nd(len(grid))) if grid else {}
        res = pl.pallas_call(
            wrapped, name=name, in_specs=all_in, out_specs=all_out, out_shape=out_shape + c_out, scratch_shapes=scratch_shapes + sems,
            input_output_aliases=all_aliases, **kwargs)(*args, *c_in)
    else:
        spec = pltpu.PrefetchScalarGridSpec(num_scalar_prefetch=1, grid=grid, in_specs=all_in, out_specs=all_out,
                                            scratch_shapes=scratch_shapes + sems)
        res = pl.pallas_call(
            wrapped, name=name, grid_spec=spec, out_shape=out_shape + c_out, input_output_aliases=all_aliases,
            compiler_params=_cparams_nd(len(grid)))(scalar, *args, *c_in)
    at = n_out
    for cm in comms:
        got = res[at:at + len(cm.out_shapes)]
        at += len(cm.out_shapes)
        if cm.finish is not None:
            cm.finish(got)
    return list(res[:n_out])


def _exchange_alone(name, comms):
    _call(lambda: None, name=name, grid=(), in_specs=[], out_specs=[], out_shape=[], args=[], comms=comms)


def _in_weight_copies(w_hbm, w_v, sem, base=0):
    return [pltpu.make_async_copy(w_hbm.at[q], w_v.at[q], sem.at[base + q]) for q in range(N_CHIPS)]


def _out_weight_copies(w_hbm, w_v, sem, base=0):
    rows = w_hbm.shape[1]
    return [pltpu.make_async_copy(w_hbm.at[q], w_v.at[pl.ds(q * rows, rows)], sem.at[base + q]) for q in range(N_CHIPS)]


def _load_at_first_step(copies):
    @pl.when(pl.program_id(0) == 0)
    def _():
        for cp in copies:
            cp.start()
        for cp in copies:
            cp.wait()


def _loss_head(xo, tgt, gv, loss_ref, dg_ref):
    d = xo.shape[1]
    xh, r = _rms_fwd(xo)
    diff = xh * gv - tgt
    loss_ref[...] += 0.5 * jnp.sum(jnp.sum(diff * diff, axis=1, keepdims=True), axis=0, keepdims=True) / d
    dout = diff / d
    dg_ref[...] += jnp.sum(dout * xh, axis=0, keepdims=True)
    return _rms_bwd(dout, xh, r, gv)


def _ffn_fwd(x, g, win_g, wout_g, name, comms=(), loss_head=None):
    t, d = x.shape
    c = win_g.shape[-1]
    ff = 2 * c
    tm = min(512, t)
    n_head = 0 if loss_head is None else 2

    def body(x_ref, g_ref, win_hbm, wout_hbm, *rest):
        head_in, (xo_ref, h_ref, xn_ref), rest = rest[:n_head], rest[n_head:n_head + 3], rest[n_head + 3:]
        head_out, (win_v, wout_v, sem) = rest[:n_head], rest[n_head:]
        _load_at_first_step(_in_weight_copies(win_hbm, win_v, sem) + _out_weight_copies(wout_hbm, wout_v, sem, N_CHIPS))
        xv = x_ref[...]
        xh, _ = _rms_fwd(xv)
        xn = (xh * g_ref[...]).astype(BF16)
        xn_ref[...] = xn
        acc = jnp.zeros((tm, d), F32)
        for j in range(2):
            gate = _dot(xn, win_v[j])
            up = _dot(xn, win_v[j + 2])
            h_ref[:, j * c:(j + 1) * c] = gate.astype(BF16)
            h_ref[:, ff + j * c:ff + (j + 1) * c] = up.astype(BF16)
            act = (gate * _sigmoid(gate) * up).astype(BF16)
            acc = acc + _dot(act, wout_v[j * c:(j + 1) * c, :])
        xo = xv + 0.5 * acc
        if loss_head is None:
            xo_ref[...] = xo
            return

        @pl.when(pl.program_id(0) == 0)
        def _():
            for ref in head_out:
                ref[...] = jnp.zeros_like(ref)

        xo_ref[...] = _loss_head(xo, head_in[0][...], head_in[1][...], *head_out)

    head_specs = [] if loss_head is None else [_row_spec(tm, d), _const_spec((1, d))]
    head_out_specs = [] if loss_head is None else [_const_spec((1, 1)), _const_spec((1, d))]
    head_out_shape = [] if loss_head is None else [jax.ShapeDtypeStruct((1, 1), F32), jax.ShapeDtypeStruct((1, d), F32)]
    return _call(
        body, name=name, grid=(t // tm,),
        in_specs=[_row_spec(tm, d), _const_spec((1, d)), HBM_SPEC, HBM_SPEC] + head_specs,
        out_specs=[_row_spec(tm, d), _row_spec(tm, 2 * ff), _row_spec(tm, d)] + head_out_specs,
        out_shape=[jax.ShapeDtypeStruct((t, d), F32), jax.ShapeDtypeStruct((t, 2 * ff), BF16), jax.ShapeDtypeStruct((t, d), BF16)]
        + head_out_shape,
        scratch_shapes=[pltpu.VMEM((N_CHIPS, d, c), BF16), pltpu.VMEM((ff, d), BF16), pltpu.SemaphoreType.DMA((2 * N_CHIPS,))],
        args=(x, g, win_g, wout_g) + tuple(loss_head or ()), comms=comms)


def _ffn_bwd(x, dy, h, g, win_g, wout_g, name, comms=(), after=()):
    t, d = x.shape
    c = win_g.shape[-1]
    ff = 2 * c
    tm = min(256, t)

    def body(x_ref, dy_ref, h_ref, g_ref, win_hbm, wout_hbm, dx_ref, dh_ref, act_ref, dg_ref, win_v, wout_v, sem):
        _load_at_first_step(_in_weight_copies(win_hbm, win_v, sem) + _out_weight_copies(wout_hbm, wout_v, sem, N_CHIPS))

        @pl.when(pl.program_id(0) == 0)
        def _():
            dg_ref[...] = jnp.zeros_like(dg_ref)

        xv, dyv, gv = x_ref[...], dy_ref[...], g_ref[...]
        xh, r = _rms_fwd(xv)
        dyh = (0.5 * dyv).astype(BF16)
        dxn = jnp.zeros((tm, d), F32)
        for j in range(2):
            gate = h_ref[:, j * c:(j + 1) * c].astype(F32)
            up = h_ref[:, ff + j * c:ff + (j + 1) * c].astype(F32)
            dact = _dot_nt(dyh, wout_v[j * c:(j + 1) * c, :])
            s = _sigmoid(gate)
            sl = gate * s
            act_ref[:, j * c:(j + 1) * c] = (sl * up).astype(BF16)
            dgate = (dact * up * (s + sl * (1.0 - s))).astype(BF16)
            dup = (dact * sl).astype(BF16)
            dh_ref[:, j * c:(j + 1) * c] = dgate
            dh_ref[:, ff + j * c:ff + (j + 1) * c] = dup
            dxn = dxn + _dot_nt(dgate, win_v[j]) + _dot_nt(dup, win_v[j + 2])
        dg_ref[...] += jnp.sum(dxn * xh, axis=0, keepdims=True)
        dx_ref[...] = dyv + _rms_bwd(dxn, xh, r, gv)

    return _call(
        body, name=name, grid=(t // tm,),
        in_specs=[_row_spec(tm, d), _row_spec(tm, d), _row_spec(tm, 2 * ff), _const_spec((1, d)), HBM_SPEC, HBM_SPEC],
        out_specs=[_row_spec(tm, d), _row_spec(tm, 2 * ff), _row_spec(tm, ff), _const_spec((1, d))],
        out_shape=[jax.ShapeDtypeStruct((t, d), F32), jax.ShapeDtypeStruct((t, 2 * ff), BF16), jax.ShapeDtypeStruct((t, ff), BF16),
                   jax.ShapeDtypeStruct((1, d), F32)],
        scratch_shapes=[pltpu.VMEM((N_CHIPS, d, c), BF16), pltpu.VMEM((ff, d), BF16), pltpu.SemaphoreType.DMA((2 * N_CHIPS,))],
        args=(x, dy, h, g, win_g, wout_g), comms=comms, after=after)


def _norm_matmul(x, g, win_g, name, comms=()):
    t, d = x.shape
    c = win_g.shape[-1]
    tm = min(1024, t)

    def body(x_ref, g_ref, win_hbm, o_ref, xn_ref, win_v, sem):
        _load_at_first_step(_in_weight_copies(win_hbm, win_v, sem))
        xh, _ = _rms_fwd(x_ref[...])
        xn = (xh * g_ref[...]).astype(BF16)
        xn_ref[...] = xn
        for q in range(N_CHIPS):
            o_ref[:, q * c:(q + 1) * c] = _dot(xn, win_v[q])

    return _call(
        body, name=name, grid=(t // tm,),
        in_specs=[_row_spec(tm, d), _const_spec((1, d)), HBM_SPEC],
        out_specs=[_row_spec(tm, N_CHIPS * c), _row_spec(tm, d)],
        out_shape=[jax.ShapeDtypeStruct((t, N_CHIPS * c), F32), jax.ShapeDtypeStruct((t, d), BF16)],
        scratch_shapes=[pltpu.VMEM((N_CHIPS, d, c), BF16), pltpu.SemaphoreType.DMA((N_CHIPS,))],
        args=(x, g, win_g), comms=comms)


def _in_weight_copies_flat(w_hbm, w_v, sem, base=0):
    c = w_hbm.shape[-1]
    return [pltpu.make_async_copy(w_hbm.at[q], w_v.at[:, pl.ds(q * c, c)], sem.at[base + q]) for q in range(N_CHIPS)]


def _proj_in_bwd_tail(dh, win_v, c):
    if len(win_v.shape) == 2:
        return _dot_nt(dh, win_v[...])
    dxn = _dot_nt(dh[:, 0:c], win_v[0])
    for q in range(1, N_CHIPS):
        dxn = dxn + _dot_nt(dh[:, q * c:(q + 1) * c], win_v[q])
    return dxn


def _prev_halo_spec(tm, cols):
    return pl.BlockSpec((HALO, cols), lambda i: (jnp.maximum(i * (tm // HALO) - 1, 0), 0))


def _next_halo_spec(tm, cols, t):
    last = t // HALO - 1
    return pl.BlockSpec((HALO, cols), lambda i: (jnp.minimum((i + 1) * (tm // HALO), last), 0))


def _shift_down(w, k):
    return w if k == 0 else pltpu.roll(w, k, 0)


def _shift_up(w, k):
    return w if k == 0 else pltpu.roll(w, w.shape[0] - k, 0)


def _pool_counts(i, tm):
    pos = (i * tm + lax.broadcasted_iota(jnp.int32, (tm, POOL_CH), 0) + 1).astype(F32)
    lane = lax.broadcasted_iota(jnp.int32, (tm, POOL_CH), 1)
    win = jnp.where(lane < POOL_GC, 2.0, jnp.where(lane < 2 * POOL_GC, 4.0, jnp.where(lane < 3 * POOL_GC, 8.0, 16.0)))
    return jnp.minimum(pos, win)


def _group_select(parts):
    return jnp.concatenate([p[:, k * POOL_GC:(k + 1) * POOL_GC] for k, p in enumerate(parts)], axis=1)


TAP_ROWS = 128


def _tap_blocks(tm, block_fn):
    cols = []
    for ch in range(CONV_CH // POOL_GC):
        lanes = slice(ch * POOL_GC, (ch + 1) * POOL_GC)
        cols.append(jnp.concatenate([block_fn(r, lanes) for r in range(tm // TAP_ROWS)], axis=0))
    return jnp.concatenate(cols, axis=1)


def _conv_block(win, taps):
    acc = jnp.zeros((TAP_ROWS, win.shape[1]), F32)
    for k in range(CONV_WIDTH):
        acc = acc + taps[k:k + 1, :] * _shift_down(win, CONV_WIDTH - 1 - k)[HALO:, :]
    return acc


def _conv_block_transposed(win, taps):
    acc = jnp.zeros((TAP_ROWS, win.shape[1]), F32)
    for k in range(CONV_WIDTH):
        acc = acc + taps[k:k + 1, :] * _shift_up(win, CONV_WIDTH - 1 - k)[0:TAP_ROWS, :]
    return acc


def _mix0_recompute(i, tm, h_cur, h_prev, conv_w, conv_b, y=None):
    prev = jnp.where(i > 0, h_prev, 0.0)
    win = jnp.concatenate([prev, h_cur], axis=0)
    u_w = win[:, 0:POOL_CH]
    a_w = win[:, POOL_CH:POOL_CH + CONV_CH]
    gt_w = win[:, POOL_CH + CONV_CH:]
    g_w = a_w * _sigmoid(gt_w)
    if y is None:
        y = _tap_blocks(tm, lambda r, lanes: _conv_block(g_w[r * TAP_ROWS:(r + 1) * TAP_ROWS + HALO, lanes], conv_w[:, lanes])) + conv_b
    s2 = u_w + _shift_down(u_w, 1)
    s4 = s2 + _shift_down(s2, 2)
    s8 = s4 + _shift_down(s4, 4)
    s16 = s8 + _shift_down(s8, 8)
    sums = _group_select([s2[HALO:], s4[HALO:], s8[HALO:], s16[HALO:]])
    cnt = _pool_counts(i, tm)
    pooled = sums / cnt - h_cur[:, 0:POOL_CH]
    return g_w, y, pooled, cnt


def _pool_linear(pooled, pw_ref, pb):
    return jnp.concatenate(
        [_dot(pooled[:, k * POOL_GC:(k + 1) * POOL_GC].astype(BF16), pw_ref[k].astype(BF16)) for k in range(len(POOL_WINDOWS))], axis=1) + pb


def _mix0_fwd(x, h0, pool_w, pool_b, pool_scale, conv_w, conv_b, ln_g, ln_b, wout_g, name, comms=()):
    t, d = x.shape
    tm = min(512, t)
    hc = h0.shape[1]

    def body(x_ref, h_ref, hp_ref, pw_ref, pb_ref, ps_ref, cw_ref, cb_ref, lg_ref, lb_ref, wout_hbm, xo_ref, ycat_ref, y_ref, wout_v, sem):
        _load_at_first_step(_out_weight_copies(wout_hbm, wout_v, sem))
        i = pl.program_id(0)
        _, y, pooled, _ = _mix0_recompute(i, tm, h_ref[...], hp_ref[...], cw_ref[...], cb_ref[...])
        y_ref[...] = y
        yhat, _ = _ln_fwd(y)
        yn = yhat * lg_ref[...] + lb_ref[...]
        yb = yn * _sigmoid(yn)
        ya = _pool_linear(pooled, pw_ref, pb_ref[...]) * ps_ref[...]
        ycat = jnp.concatenate([ya, yb], axis=1).astype(BF16)
        ycat_ref[...] = ycat
        xo_ref[...] = x_ref[...] + _dot(ycat, wout_v[...])

    return _call(
        body, name=name, grid=(t // tm,),
        in_specs=[_row_spec(tm, d), _row_spec(tm, hc), _prev_halo_spec(tm, hc), _const_spec(pool_w.shape), _const_spec((1, POOL_CH)),
                  _const_spec((1, POOL_CH)), _const_spec(conv_w.shape), _const_spec((1, CONV_CH)), _const_spec((1, CONV_CH)),
                  _const_spec((1, CONV_CH)), HBM_SPEC],
        out_specs=[_row_spec(tm, d), _row_spec(tm, d), _row_spec(tm, CONV_CH)],
        out_shape=[jax.ShapeDtypeStruct((t, d), F32), jax.ShapeDtypeStruct((t, d), BF16), jax.ShapeDtypeStruct((t, CONV_CH), F32)],
        scratch_shapes=[pltpu.VMEM((d, d), BF16), pltpu.SemaphoreType.DMA((N_CHIPS,))],
        args=(x, h0, h0, pool_w, pool_b, pool_scale, conv_w, conv_b, ln_g, ln_b, wout_g), comms=comms)


def _mix0_bwd_a(dy, h0, y_conv, pool_w, pool_b, pool_scale, conv_w, conv_b, ln_g, ln_b, wout_g, name):
    t, d = dy.shape
    tm = min(512, t)
    hc = h0.shape[1]
    n_small = 40

    def body(dy_ref, h_ref, hp_ref, y_ref, pw_ref, pb_ref, ps_ref, cw_ref, cb_ref, lg_ref, lb_ref, wout_hbm,
             dconv_ref, dpc_ref, dpw_ref, small_ref, wout_v, sem):
        _load_at_first_step(_out_weight_copies(wout_hbm, wout_v, sem))
        i = pl.program_id(0)

        @pl.when(i == 0)
        def _():
            dpw_ref[...] = jnp.zeros_like(dpw_ref)
            small_ref[...] = jnp.zeros_like(small_ref)

        g_w, y, pooled, cnt = _mix0_recompute(i, tm, h_ref[...], hp_ref[...], cw_ref[...], cb_ref[...], y_ref[...])
        yhat, rs = _ln_fwd(y)
        lg = lg_ref[...]
        yn = yhat * lg + lb_ref[...]
        mixed = _pool_linear(pooled, pw_ref, pb_ref[...])
        dycat = _dot_nt(dy_ref[...].astype(BF16), wout_v[...])
        dya, dyb = dycat[:, 0:POOL_CH], dycat[:, POOL_CH:]
        sg = _sigmoid(yn)
        dyn = dyb * (sg * (1.0 + yn * (1.0 - sg)))
        dyc = _ln_bwd(dyn * lg, yhat, rs)
        dconv_ref[...] = dyc

        def add_row(k, value):
            small_ref[k:k + 1, :] += jnp.sum(value, axis=0, keepdims=True)

        for ch in range(CONV_CH // POOL_GC):
            lanes = slice(ch * POOL_GC, (ch + 1) * POOL_GC)
            taps = [jnp.zeros((1, POOL_GC), F32)] * CONV_WIDTH
            for r in range(tm // TAP_ROWS):
                win = g_w[r * TAP_ROWS:(r + 1) * TAP_ROWS + HALO, lanes]
                d = dyc[r * TAP_ROWS:(r + 1) * TAP_ROWS, lanes]
                for k in range(CONV_WIDTH):
                    taps[k] = taps[k] + jnp.sum(d * _shift_down(win, CONV_WIDTH - 1 - k)[HALO:, :], axis=0, keepdims=True)
            for k in range(CONV_WIDTH):
                small_ref[k:k + 1, lanes] += taps[k]
        add_row(32, dyc)
        add_row(33, dyn * yhat)
        add_row(34, dyn)
        scale = ps_ref[...]
        dmixed = dya * scale
        add_row(35, dya * mixed)
        add_row(36, dmixed)
        dmb = dmixed.astype(BF16)
        dpooled = []
        for k in range(len(POOL_WINDOWS)):
            sl = slice(k * POOL_GC, (k + 1) * POOL_GC)
            dpw_ref[k] += _dot_tn(pooled[:, sl].astype(BF16), dmb[:, sl])
            dpooled.append(_dot_nt(dmb[:, sl], pw_ref[k].astype(BF16)))
        dpc_ref[...] = jnp.concatenate(dpooled, axis=1) / cnt

    return _call(
        body, name=name, grid=(t // tm,),
        in_specs=[_row_spec(tm, d), _row_spec(tm, hc), _prev_halo_spec(tm, hc), _row_spec(tm, CONV_CH), _const_spec(pool_w.shape),
                  _const_spec((1, POOL_CH)), _const_spec((1, POOL_CH)), _const_spec(conv_w.shape), _const_spec((1, CONV_CH)),
                  _const_spec((1, CONV_CH)), _const_spec((1, CONV_CH)), HBM_SPEC],
        out_specs=[_row_spec(tm, CONV_CH), _row_spec(tm, POOL_CH), _const_spec(pool_w.shape), _const_spec((n_small, CONV_CH))],
        out_shape=[jax.ShapeDtypeStruct((t, CONV_CH), F32), jax.ShapeDtypeStruct((t, POOL_CH), F32),
                   jax.ShapeDtypeStruct(pool_w.shape, F32), jax.ShapeDtypeStruct((n_small, CONV_CH), F32)],
        scratch_shapes=[pltpu.VMEM((d, d), BF16), pltpu.SemaphoreType.DMA((N_CHIPS,))],
        args=(dy, h0, h0, y_conv, pool_w, pool_b, pool_scale, conv_w, conv_b, ln_g, ln_b, wout_g))


def _mix0_bwd_b(x, dy, h0, dconv, dpc, g, conv_w, win_g, name):
    t, d = x.shape
    tm = min(512, t)
    hc = h0.shape[1]
    c = win_g.shape[-1]
    n_tiles = t // tm

    def body(x_ref, dy_ref, h_ref, dc_ref, dcn_ref, dp_ref, dpn_ref, g_ref, cw_ref, win_hbm, dx_ref, dh_ref, dg_ref, win_v, sem):
        _load_at_first_step(_in_weight_copies_flat(win_hbm, win_v, sem))
        i = pl.program_id(0)

        @pl.when(i == 0)
        def _():
            dg_ref[...] = jnp.zeros_like(dg_ref)

        not_last = i < n_tiles - 1
        dc_w = jnp.concatenate([dc_ref[...], jnp.where(not_last, dcn_ref[...], 0.0)], axis=0)
        dp_w = jnp.concatenate([dp_ref[...], jnp.where(not_last, dpn_ref[...], 0.0)], axis=0)
        cw = cw_ref[...]
        dg = _tap_blocks(tm, lambda r, lanes: _conv_block_transposed(dc_w[r * TAP_ROWS:(r + 1) * TAP_ROWS + HALO, lanes], cw[:, lanes]))
        a2 = dp_w + _shift_up(dp_w, 1)
        a4 = a2 + _shift_up(a2, 2)
        a8 = a4 + _shift_up(a4, 4)
        a16 = a8 + _shift_up(a8, 8)
        back = _group_select([a2[0:tm], a4[0:tm], a8[0:tm], a16[0:tm]])
        du = back - dp_ref[...] * _pool_counts(i, tm)
        hv = h_ref[...]
        a = hv[:, POOL_CH:POOL_CH + CONV_CH]
        sig = _sigmoid(hv[:, POOL_CH + CONV_CH:])
        dh = jnp.concatenate([du, dg * sig, dg * a * sig * (1.0 - sig)], axis=1).astype(BF16)
        dh_ref[...] = dh
        dxn = _proj_in_bwd_tail(dh, win_v, c)
        xh, r = _rms_fwd(x_ref[...])
        dg_ref[...] += jnp.sum(dxn * xh, axis=0, keepdims=True)
        dx_ref[...] = dy_ref[...] + _rms_bwd(dxn, xh, r, g_ref[...])

    return _call(
        body, name=name, grid=(n_tiles,),
        in_specs=[_row_spec(tm, d), _row_spec(tm, d), _row_spec(tm, hc), _row_spec(tm, CONV_CH), _next_halo_spec(tm, CONV_CH, t),
                  _row_spec(tm, POOL_CH), _next_halo_spec(tm, POOL_CH, t), _const_spec((1, d)), _const_spec(conv_w.shape), HBM_SPEC],
        out_specs=[_row_spec(tm, d), _row_spec(tm, hc), _const_spec((1, d))],
        out_shape=[jax.ShapeDtypeStruct((t, d), F32), jax.ShapeDtypeStruct((t, hc), BF16), jax.ShapeDtypeStruct((1, d), F32)],
        scratch_shapes=[pltpu.VMEM((d, N_CHIPS * c), BF16), pltpu.SemaphoreType.DMA((N_CHIPS,))],
        args=(x, dy, h0, dconv, dconv, dpc, dpc, g, conv_w, win_g))


SQRT_HALF = 0.7071067811865476
INV_SQRT_2PI = 0.3989422804014327


def _causal_mask():
    return (lax.broadcasted_iota(jnp.int32, (CHUNK, CHUNK), 1) <= lax.broadcasted_iota(jnp.int32, (CHUNK, CHUNK), 0)).astype(F32)


def _sgu_recompute(pre, lg, lb):
    half = pre.shape[1] // 2
    phi = 0.5 * (1.0 + lax.erf(pre * SQRT_HALF))
    z = pre * phi
    u, v = z[:, 0:half], z[:, half:]
    vhat, rs = _ln_fwd(v)
    return u, vhat, rs, vhat * lg + lb, phi


def _sgu_spatial(vln, w_ref, bt, tm):
    mask = _causal_mask()
    wm = [(w_ref[hd] * mask).astype(BF16) for hd in range(SGU_HEADS)]
    vb = vln.astype(BF16)
    rows = []
    for ch in range(tm // CHUNK):
        blocks = [_dot(wm[hd], vb[ch * CHUNK:(ch + 1) * CHUNK, hd * CHUNK:(hd + 1) * CHUNK]) + bt[:, hd:hd + 1] for hd in range(SGU_HEADS)]
        rows.append(jnp.concatenate(blocks, axis=1))
    return jnp.concatenate(rows, axis=0), wm


def _sgu_fwd(x, pre, ln_g, ln_b, w, bt, wout_g, name):
    t, d = x.shape
    tm = min(512, t)
    pc = pre.shape[1]

    def body(x_ref, pre_ref, lg_ref, lb_ref, w_ref, bt_ref, wout_hbm, xo_ref, p_ref, wout_v, sem):
        _load_at_first_step(_out_weight_copies(wout_hbm, wout_v, sem))
        u, _, _, vln, _ = _sgu_recompute(pre_ref[...], lg_ref[...], lb_ref[...])
        vo, _ = _sgu_spatial(vln, w_ref, bt_ref[...], tm)
        p = (u * vo).astype(BF16)
        p_ref[...] = p
        xo_ref[...] = x_ref[...] + _dot(p, wout_v[...])

    return _call(
        body, name=name, grid=(t // tm,),
        in_specs=[_row_spec(tm, d), _row_spec(tm, pc), _const_spec((1, d)), _const_spec((1, d)), _const_spec(w.shape),
                  _const_spec(bt.shape), HBM_SPEC],
        out_specs=[_row_spec(tm, d), _row_spec(tm, d)],
        out_shape=[jax.ShapeDtypeStruct((t, d), F32), jax.ShapeDtypeStruct((t, d), BF16)],
        scratch_shapes=[pltpu.VMEM((d, d), BF16), pltpu.SemaphoreType.DMA((N_CHIPS,))],
        args=(x, pre, ln_g, ln_b, w, bt, wout_g))


def _sgu_bwd(x, dy, pre, g, ln_g, ln_b, w, bt, win_g, wout_g, name, comms=()):
    t, d = x.shape
    tm = min(512, t)
    pc = pre.shape[1]
    c = win_g.shape[-1]

    def body(x_ref, dy_ref, pre_ref, g_ref, lg_ref, lb_ref, w_ref, bt_ref, win_hbm, wout_hbm,
             dx_ref, dpre_ref, dg_ref, dlg_ref, dlb_ref, dw_ref, dbt_ref, win_v, wout_v, sem):
        _load_at_first_step(_in_weight_copies(win_hbm, win_v, sem) + _out_weight_copies(wout_hbm, wout_v, sem, N_CHIPS))
        i = pl.program_id(0)

        @pl.when(i == 0)
        def _():
            for ref in (dg_ref, dlg_ref, dlb_ref, dw_ref, dbt_ref):
                ref[...] = jnp.zeros_like(ref)

        prev = pre_ref[...]
        lg = lg_ref[...]
        u, vhat, rs, vln, phi = _sgu_recompute(prev, lg, lb_ref[...])
        vo, wm = _sgu_spatial(vln, w_ref, bt_ref[...], tm)
        dp = _dot_nt(dy_ref[...].astype(BF16), wout_v[...])
        du = dp * vo
        dvo = dp * u
        dvob = dvo.astype(BF16)
        vb = vln.astype(BF16)
        head_lane = lax.broadcasted_iota(jnp.int32, (CHUNK, SGU_HEADS), 1)
        dbt = jnp.zeros((CHUNK, SGU_HEADS), F32)
        dw = [jnp.zeros((CHUNK, CHUNK), F32) for _ in range(SGU_HEADS)]
        rows = []
        for ch in range(tm // CHUNK):
            rs_ = slice(ch * CHUNK, (ch + 1) * CHUNK)
            blocks = []
            for hd in range(SGU_HEADS):
                cs = slice(hd * CHUNK, (hd + 1) * CHUNK)
                dbt = dbt + jnp.where(head_lane == hd, jnp.sum(dvo[rs_, cs], axis=1, keepdims=True), 0.0)
                dw[hd] = dw[hd] + _dot_nt(dvob[rs_, cs], vb[rs_, cs])
                blocks.append(_dot_tn(wm[hd], dvob[rs_, cs]))
            rows.append(jnp.concatenate(blocks, axis=1))
        dvln = jnp.concatenate(rows, axis=0)
        mask = _causal_mask()
        for hd in range(SGU_HEADS):
            dw_ref[hd] += dw[hd] * mask
        dbt_ref[...] += dbt
        dlg_ref[...] += jnp.sum(dvln * vhat, axis=0, keepdims=True)
        dlb_ref[...] += jnp.sum(dvln, axis=0, keepdims=True)
        dv = _ln_bwd(dvln * lg, vhat, rs)
        gelu_grad = phi + prev * jnp.exp(-0.5 * prev * prev) * INV_SQRT_2PI
        dpre = (jnp.concatenate([du, dv], axis=1) * gelu_grad).astype(BF16)
        dpre_ref[...] = dpre
        dxn = _proj_in_bwd_tail(dpre, win_v, c)
        xh, r = _rms_fwd(x_ref[...])
        dg_ref[...] += jnp.sum(dxn * xh, axis=0, keepdims=True)
        dx_ref[...] = dy_ref[...] + _rms_bwd(dxn, xh, r, g_ref[...])

    return _call(
        body, name=name, grid=(t // tm,),
        in_specs=[_row_spec(tm, d), _row_spec(tm, d), _row_spec(tm, pc), _const_spec((1, d)), _const_spec((1, d)), _const_spec((1, d)),
                  _const_spec(w.shape), _const_spec(bt.shape), HBM_SPEC, HBM_SPEC],
        out_specs=[_row_spec(tm, d), _row_spec(tm, pc), _const_spec((1, d)), _const_spec((1, d)), _const_spec((1, d)),
                   _const_spec(w.shape), _const_spec(bt.shape)],
        out_shape=[jax.ShapeDtypeStruct((t, d), F32), jax.ShapeDtypeStruct((t, pc), BF16), jax.ShapeDtypeStruct((1, d), F32),
                   jax.ShapeDtypeStruct((1, d), F32), jax.ShapeDtypeStruct((1, d), F32), jax.ShapeDtypeStruct(w.shape, F32),
                   jax.ShapeDtypeStruct(bt.shape, F32)],
        scratch_shapes=[pltpu.VMEM((N_CHIPS, d, c), BF16), pltpu.VMEM((d, d), BF16), pltpu.SemaphoreType.DMA((2 * N_CHIPS,))],
        args=(x, dy, pre, g, ln_g, ln_b, w, bt, win_g, wout_g), comms=comms)


def _tn_matmul(a, b, scale, bm, bn, name, comms=(), after=()):
    t, m = a.shape
    n = b.shape[1]
    tk = min(2048, t)
    bm, bn = min(bm, m), min(bn, n)
    nk = t // tk

    def body(a_ref, b_ref, o_ref, acc_ref):
        k = pl.program_id(2)

        @pl.when(k == 0)
        def _():
            acc_ref[...] = jnp.zeros_like(acc_ref)

        bv = b_ref[...]
        if bv.dtype != BF16:
            bv = (scale * bv).astype(BF16)
        acc_ref[...] += _dot_tn(a_ref[...], bv)

        @pl.when(k == nk - 1)
        def _():
            o_ref[...] = acc_ref[...].astype(BF16)

    return _call(
        body, name=name, grid=(m // bm, n // bn, nk),
        in_specs=[pl.BlockSpec((tk, bm), lambda i, j, k: (k, i)), pl.BlockSpec((tk, bn), lambda i, j, k: (k, j))],
        out_specs=[pl.BlockSpec((bm, bn), lambda i, j, k: (i, j))],
        out_shape=[jax.ShapeDtypeStruct((m, n), BF16)],
        scratch_shapes=[pltpu.VMEM((bm, bn), F32)],
        args=(a, b), comms=comms, after=after)[0]


def _row_tile(rows, cols, budget_bytes=2 * 1024 * 1024):
    best = None
    for cand in range(16, rows + 1, 16):
        if rows % cand == 0 and cand * cols * 4 <= budget_bytes:
            best = cand
    return best or rows


def _scalar_grid(grid, in_specs, out_specs):
    return pltpu.PrefetchScalarGridSpec(num_scalar_prefetch=1, grid=grid, in_specs=in_specs, out_specs=out_specs)


def _cast_into_slot(w, layer, me, name, after=()):
    _, rows, cols = w.shape
    tr = _row_tile(rows, cols)

    def body(me_ref, w_ref, *rest):
        rest[-1][...] = w_ref[...].astype(BF16)

    return pl.pallas_call(
        body, name=name,
        grid_spec=_scalar_grid((rows // tr,), [pl.BlockSpec((None, tr, cols), lambda i, me: (layer, i, 0))]
                               + [pl.BlockSpec(memory_space=pl.ANY)] * len(after),
                               pl.BlockSpec((None, tr, cols), lambda i, me: (me[0], i, 0))),
        out_shape=jax.ShapeDtypeStruct((N_CHIPS, rows, cols), BF16), compiler_params=_cparams())(me, w, *after)


def _add_half(view, other, core, name):
    q, _, r, c = view.shape
    tr = _row_tile(r, c)

    def body(core_ref, a_ref, b_ref, o_ref):
        o_ref[...] = (a_ref[...].astype(F32) + b_ref[...].astype(F32)).astype(BF16)

    return pl.pallas_call(
        body, name=name,
        grid_spec=_scalar_grid((q, r // tr), [pl.BlockSpec((None, None, tr, c), lambda k, i, core: (k, core[0], i, 0)),
                                             pl.BlockSpec((None, tr, c), lambda k, i, core: (k, i, 0))],
                               pl.BlockSpec((None, tr, c), lambda k, i, core: (k, i, 0))),
        out_shape=jax.ShapeDtypeStruct((q, r, c), BF16), compiler_params=_cparams_nd(2))(core, view, other)


def _reduce_piece(partial, staged, me, column_sharded, name):
    _, r, c = staged.shape
    tr = _row_tile(r, c, budget_bytes=1024 * 1024)
    nt = r // tr
    if column_sharded:
        own2d = partial.reshape(r, N_CHIPS * c)
        own_spec = pl.BlockSpec((tr, c), lambda i, me: (i, me[0]))
    else:
        own2d = partial.reshape(N_CHIPS * r, c)
        own_spec = pl.BlockSpec((tr, c), lambda i, me: (me[0] * nt + i, 0))
    ring = [pl.BlockSpec((None, tr, c), lambda i, me, k=k: ((me[0] + k) % N_CHIPS, i, 0)) for k in (1, 2, 3)]

    def body(me_ref, own_ref, s1_ref, s2_ref, s3_ref, o_ref):
        o_ref[...] = ((own_ref[...].astype(F32) + s1_ref[...].astype(F32)) + s2_ref[...].astype(F32)) + s3_ref[...].astype(F32)

    return pl.pallas_call(
        body, name=name, grid_spec=_scalar_grid((nt,), [own_spec] + ring, pl.BlockSpec((tr, c), lambda i, me: (i, 0))),
        out_shape=jax.ShapeDtypeStruct((r, c), F32), compiler_params=_cparams())(me, own2d, staged, staged, staged)


def _sum_leading(s, name):
    n, rows, cols = s.shape
    tr = _row_tile(rows, cols, budget_bytes=1024 * 1024)

    def body(s_ref, o_ref):
        acc = s_ref[0].astype(F32)
        for k in range(1, n):
            acc = acc + s_ref[k].astype(F32)
        o_ref[...] = acc

    return pl.pallas_call(
        body, name=name, grid=(rows // tr,), in_specs=[pl.BlockSpec((n, tr, cols), lambda i: (0, i, 0))], out_specs=_row_spec(tr, cols),
        out_shape=jax.ShapeDtypeStruct((rows, cols), F32), compiler_params=_cparams())(s)


ADAM_C1 = 1.0 / (1.0 - ADAM_B1 ** ADAM_STEP)
ADAM_C2 = 1.0 / (1.0 - ADAM_B2 ** ADAM_STEP)


def _adamw_math(w, g, m, v):
    mn = ADAM_B1 * m + (1.0 - ADAM_B1) * g
    vn = ADAM_B2 * v + (1.0 - ADAM_B2) * (g * g)
    return -ADAM_LR * ((mn * ADAM_C1) / (jnp.sqrt(vn * ADAM_C2) + ADAM_EPS) + ADAM_WD * w), mn, vn


def _adamw(w, g, m, v, name):
    shape = w.shape
    cols = shape[-1] if w.ndim > 1 else 128
    w2, g2, m2, v2 = (a.reshape(-1, cols) for a in (w, g, m, v))
    rows = w2.shape[0]
    tr = _row_tile(rows, cols, budget_bytes=1024 * 1024)

    def body(w_ref, g_ref, m_ref, v_ref, d_ref, mo_ref, vo_ref):
        d_ref[...], mo_ref[...], vo_ref[...] = _adamw_math(w_ref[...], g_ref[...], m_ref[...], v_ref[...])

    spec = _row_spec(tr, cols)
    outs = pl.pallas_call(
        body, name=name, grid=(rows // tr,), in_specs=[spec] * 4, out_specs=[spec] * 3,
        out_shape=[jax.ShapeDtypeStruct((rows, cols), F32)] * 3, compiler_params=_cparams())(w2, g2, m2, v2)
    return tuple(o.reshape(shape) for o in outs)


def _adamw_sharded(w, g_mine, g_sibling, m, v, core, layer, prev, name, comms=(), after=()):
    n_layers, r, c = w.shape
    half = r // 2
    tr = _row_tile(half, c)
    nt = half // tr

    def body(core_ref, w_ref, gm_ref, gs_ref, m_ref, v_ref, *rest):
        g_ref, d_ref, mo_ref, vo_ref = rest[-4:]
        gv = jnp.where(pl.program_id(0) == core_ref[0], gm_ref[...], gs_ref[...])
        g_ref[...] = gv
        d_ref[...], mo_ref[...], vo_ref[...] = _adamw_math(w_ref[...], gv, m_ref[...], v_ref[...])

    full = pl.BlockSpec((None, tr, c), lambda h, i, core: (layer, h * nt + i, 0))
    part = pl.BlockSpec((tr, c), lambda h, i, core: (i, 0))
    args = [w, g_mine, g_sibling, m, v]
    in_specs = [full, part, part, full, full]
    aliases = {}
    if prev is not None:
        aliases = {len(args) + k: k for k in range(4)}
        args += list(prev)
        in_specs += [pl.BlockSpec(memory_space=pl.ANY)] * 4
    args += list(after)
    in_specs += [pl.BlockSpec(memory_space=pl.ANY)] * len(after)
    return _call(body, name=name, grid=(2, nt), in_specs=in_specs, out_specs=[full] * 4, out_shape=[jax.ShapeDtypeStruct(w.shape, F32)] * 4,
                 args=args, comms=comms, scalar=core, aliases=aliases)


BIG_IN = ("ffn1_w_in", "ffn2_w_in", "ab_w_in", "sgu_w_in")
BIG_OUT = ("ffn1_w_out", "ffn2_w_out", "ab_w_out", "sgu_w_out")
BIG = BIG_IN + BIG_OUT


class _Gatherer:
    def __init__(self, slots):
        self.slots = dict(slots)

    def _stage(self, keys, d2d):
        n = len(keys)

        def plan(ins, outs, place):
            x, y, c = place
            me = 2 * x + y
            remote = []
            for a in range(n):
                rows = ins[a].shape[1] // 2

                def half(ref, q, core, rows=rows):
                    return ref.at[q, pl.ds(core * rows, rows), :]

                for (px, py) in _other_chips(x, y):
                    q = 2 * px + py
                    if d2d:
                        remote.append((half(ins[a], q, c), half(outs[a], q, c), (x, y, 1 - c), half(outs[a], q, 1 - c)))
                    else:
                        remote.append((half(ins[a], me, c), half(outs[a], me, c), (px, py, c), half(outs[a], q, c)))
            return remote

        def finish(outs):
            for k, o in zip(keys, outs):
                self.slots[k] = o

        arrays = [self.slots[k] for k in keys]
        return _Exchange(arrays, [_sds(a) for a in arrays], plan, 3 * n, {a: a for a in range(n)}, finish)

    def direct(self, keys):
        n = len(keys)

        def plan(ins, outs, place):
            x, y, c = place
            me = 2 * x + y
            return [(ins[a].at[me], outs[a].at[me], (px, py, c), outs[a].at[2 * px + py]) for a in range(n) for (px, py) in _other_chips(x, y)]

        def finish(outs):
            for k, o in zip(keys, outs):
                self.slots[k] = o

        arrays = [self.slots[k] for k in keys]
        return _Exchange(arrays, [_sds(a) for a in arrays], plan, 3 * n, {a: a for a in range(n)}, finish)

    def ici(self, keys):
        return self._stage(keys, False)

    def d2d(self, keys):
        return self._stage(keys, True)


class _Reducer:
    def __init__(self, me, core):
        self.me, self.core = me, core
        self.views, self.partial, self.staged, self.mine, self.theirs = {}, {}, {}, {}, {}

    def add(self, key, g):
        m, n = g.shape
        if key[0] in BIG_IN:
            self.views[key] = g.reshape(1, 2, m // 2, n)
        else:
            self.views[key] = g.reshape(N_CHIPS, 2, m // (2 * N_CHIPS), n)

    def swap(self, keys):
        views = [self.views[k] for k in keys]

        def plan(ins, outs, place):
            x, y, c = place
            return [(ins[a].at[:, 1 - c], outs[a], (x, y, 1 - c), outs[a]) for a in range(len(keys))]

        def finish(outs):
            for k, v, o in zip(keys, views, outs):
                self.partial[k] = _add_half(v, o, self.core, f"chip_partial_{k[0]}_{k[1]}")

        shapes = [jax.ShapeDtypeStruct((v.shape[0],) + v.shape[2:], v.dtype) for v in views]
        return _Exchange(views, shapes, plan, len(keys), None, finish)

    def scatter(self, keys, part=(0, 1)):
        i, n = part
        n_keys = len(keys)
        parts = [self.partial[k] for k in keys]
        shapes = []
        for k, p in zip(keys, parts):
            q, r, c = p.shape
            shapes.append(jax.ShapeDtypeStruct((N_CHIPS, r, c // N_CHIPS if k[0] in BIG_IN else c), p.dtype))

        def piece(ref, key, q, rows, cols):
            return ref.at[0, rows, pl.ds(q * cols, cols)] if key[0] in BIG_IN else ref.at[q, rows, :]

        def plan(ins, outs, place):
            x, y, c = place
            me = 2 * x + y
            remote = []
            for a, k in enumerate(keys):
                _, r, cols = shapes[a].shape
                rows = pl.ds(i * (r // n), r // n)
                for (px, py) in _other_chips(x, y):
                    q = 2 * px + py
                    remote.append((piece(ins[a], k, q, rows, cols), outs[a].at[me, rows, :], (px, py, c), outs[a].at[q, rows, :]))
            return remote

        def finish(outs):
            for k, p, o in zip(keys, parts, outs):
                self.staged[k] = o
                if i == n - 1:
                    self.mine[k] = _reduce_piece(p, o, self.me, k[0] in BIG_IN, f"reduce_{k[0]}_{k[1]}")

        inputs, aliases = parts, None
        if i > 0:
            inputs = parts + [self.staged[k] for k in keys]
            aliases = {n_keys + a: a for a in range(n_keys)}
        return _Exchange(inputs, shapes, plan, 3 * n_keys, aliases, finish)

    def scatter_behind(self, keys, work, tag):
        n = len(keys)
        parts = [self.partial[k] for k in keys]
        lands = []
        for k, p in zip(keys, parts):
            _, r, c = p.shape
            lands.append(jax.ShapeDtypeStruct((N_CHIPS, r, c // N_CHIPS if k[0] in BIG_IN else c), p.dtype))
        sem_spec = pl.BlockSpec(memory_space=pltpu.SEMAPHORE)
        effect = pltpu.CompilerParams(has_side_effects=pltpu.SideEffectType.DATAFLOW_SIDE_EFFECTING)

        def copies(part_refs, land_refs, send_sems, recv_sems):
            x, y, c = _my_place()
            me = 2 * x + y
            out = []
            for a, k in enumerate(keys):
                cols = lands[a].shape[2]
                for j, (px, py) in enumerate(_other_chips(x, y)):
                    q = 2 * px + py
                    src = part_refs[a].at[0, :, pl.ds(q * cols, cols)] if k[0] in BIG_IN else part_refs[a].at[q]
                    sems = dict(send_sem=send_sems.at[3 * a + j], recv_sem=recv_sems.at[3 * a + j], device_id=(px, py, c), device_id_type=MESH)
                    out.append((pltpu.make_async_remote_copy(src_ref=src, dst_ref=land_refs[a].at[me], **sems),
                                pltpu.make_async_remote_copy(src_ref=src, dst_ref=land_refs[a].at[q], **sems)))
            return out

        def start_body(*refs):
            part_refs, land_refs, send_sems, recv_sems, token = refs[:n], refs[n:2 * n], refs[2 * n], refs[2 * n + 1], refs[-1]
            for send, _ in copies(part_refs, land_refs, send_sems, recv_sems):
                send.start()
            token[...] = jnp.zeros_like(token)

        def wait_body(*refs):
            part_refs, land_refs, send_sems, recv_sems = refs[:n], refs[n:2 * n], refs[2 * n], refs[2 * n + 1]
            for send, arrive in copies(part_refs, land_refs, send_sems, recv_sems):
                send.wait_send()
                arrive.wait_recv()

        in_hbm = [pltpu.with_memory_space_constraint(p, pltpu.HBM) for p in parts]
        in_hbm += [pltpu.with_memory_space_constraint(lax.empty(s.shape, s.dtype), pltpu.HBM) for s in lands]
        thru_shapes = [pltpu.HBM(p.shape, p.dtype) for p in parts] + [pltpu.HBM(s.shape, s.dtype) for s in lands]
        started = pl.pallas_call(
            start_body, name=f"scatter_{tag}_start", in_specs=[HBM_SPEC] * (2 * n),
            out_shape=(pltpu.SemaphoreType.DMA((3 * n,)), pltpu.SemaphoreType.DMA((3 * n,)), *thru_shapes, jax.ShapeDtypeStruct((8, 128), F32)),
            out_specs=(sem_spec, sem_spec, *[HBM_SPEC] * (2 * n), pl.BlockSpec(memory_space=pltpu.VMEM)),
            input_output_aliases={i: 2 + i for i in range(2 * n)}, compiler_params=effect)(*in_hbm)
        send_sems, recv_sems, thru, token = started[0], started[1], started[2:2 + 2 * n], started[-1]
        after = work(token)
        done = pl.pallas_call(
            wait_body, name=f"scatter_{tag}_wait", in_specs=[HBM_SPEC] * (2 * n) + [sem_spec, sem_spec] + [pl.BlockSpec(memory_space=pl.ANY)] * len(after),
            out_shape=tuple(thru_shapes), out_specs=tuple([HBM_SPEC] * (2 * n)), input_output_aliases={i: i for i in range(2 * n)},
            compiler_params=effect)(*thru, send_sems, recv_sems, *after)
        for a, k in enumerate(keys):
            self.staged[k] = done[n + a]
            self.mine[k] = _reduce_piece(done[a], done[n + a], self.me, k[0] in BIG_IN, f"reduce_{k[0]}_{k[1]}")

    def exchange(self, keys):
        mine = [self.mine[k] for k in keys]

        def plan(ins, outs, place):
            x, y, c = place
            return [(ins[a], outs[a], (x, y, 1 - c), outs[a]) for a in range(len(keys))]

        def finish(outs):
            for k, o in zip(keys, outs):
                self.theirs[k] = o

        return _Exchange(mine, [_sds(a) for a in mine], plan, len(keys), None, finish)


def _half(ref, q, core):
    rows = ref.shape[1] // 2
    return ref.at[q, pl.ds(core * rows, rows), :]


def _all_gather_full(gat, keys, work):
    n = len(keys)
    arrays = [gat.slots[k] for k in keys]
    sem_spec = pl.BlockSpec(memory_space=pltpu.SEMAPHORE)
    effect = pltpu.CompilerParams(has_side_effects=pltpu.SideEffectType.DATAFLOW_SIDE_EFFECTING)

    def places():
        x, y, c = _my_place()
        return c, (x, y, 1 - c), (1 - x, y, c), (x, 1 - y, c), 2 * x + y, 2 * (1 - x) + y, 2 * x + (1 - y), 2 * (1 - x) + (1 - y)

    def first_hop(refs, send_sems, recv_sems):
        c, _, x_nbr, y_nbr, me, qx, qy, _ = places()
        out = []
        for a in range(n):
            for k, (to, q) in enumerate(((x_nbr, qx), (y_nbr, qy))):
                sems = dict(send_sem=send_sems.at[2 * a + k], recv_sem=recv_sems.at[2 * a + k], device_id=to, device_id_type=MESH)
                out.append((pltpu.make_async_remote_copy(src_ref=_half(refs[a], me, c), dst_ref=_half(refs[a], me, c), **sems),
                            pltpu.make_async_remote_copy(src_ref=_half(refs[a], me, c), dst_ref=_half(refs[a], q, c), **sems)))
        return out

    def start_body(*refs):
        for send, _ in first_hop(refs[:n], refs[n], refs[n + 1]):
            send.start()
        refs[-1][...] = jnp.zeros_like(refs[-1])

    def wait_body(*refs):
        for send, arrive in first_hop(refs[:n], refs[n], refs[n + 1]):
            send.wait_send()
            arrive.wait_recv()

    thru_shapes = [pltpu.HBM(a.shape, a.dtype) for a in arrays]
    started = pl.pallas_call(
        start_body, name="gather_first_hop_start", in_specs=[HBM_SPEC] * n,
        out_shape=(pltpu.SemaphoreType.DMA((2 * n,)), pltpu.SemaphoreType.DMA((2 * n,)), *thru_shapes, jax.ShapeDtypeStruct((8, 128), F32)),
        out_specs=(sem_spec, sem_spec, *[HBM_SPEC] * n, pl.BlockSpec(memory_space=pltpu.VMEM)),
        input_output_aliases={i: 2 + i for i in range(n)}, compiler_params=effect,
    )(*[pltpu.with_memory_space_constraint(a, pltpu.HBM) for a in arrays])
    after = work(started[-1])
    landed_first = pl.pallas_call(
        wait_body, name="gather_first_hop_wait", in_specs=[HBM_SPEC] * n + [sem_spec, sem_spec] + [pl.BlockSpec(memory_space=pl.ANY)] * len(after),
        out_shape=tuple(thru_shapes), out_specs=tuple([HBM_SPEC] * n), input_output_aliases={i: i for i in range(n)},
        compiler_params=effect)(*started[2:2 + n], started[0], started[1], *after)
    per = 5

    def body(*refs):
        outs = refs[n:2 * n]
        send_sems, recv_sems = refs[2 * n:]
        c, sibling, x_nbr, y_nbr, _, qx, qy, qd = places()

        def quarter(ref, q, core, k):
            rows = ref.shape[1] // 4
            return ref.at[q, pl.ds((2 * core + k) * rows, rows), :]

        def copy(a, k, part, to):
            return pltpu.make_async_remote_copy(src_ref=part, dst_ref=part, send_sem=send_sems.at[per * a + k],
                                                recv_sem=recv_sems.at[per * a + k], device_id=to, device_id_type=MESH)

        sent = []

        def send(a, k, part, to):
            cp = copy(a, k, part, to)
            cp.start()
            sent.append(cp)

        for a in range(n):
            send(a, 0, quarter(outs[a], qy, c, 0), x_nbr)
            send(a, 1, quarter(outs[a], qx, c, 1), y_nbr)
            send(a, 2, _half(outs[a], qx, c), sibling)
            send(a, 3, _half(outs[a], qy, c), sibling)
        for a in range(n):
            copy(a, 0, quarter(outs[a], qd, c, 0), sibling).wait_recv()
            copy(a, 1, quarter(outs[a], qd, c, 1), sibling).wait_recv()
            send(a, 4, _half(outs[a], qd, c), sibling)
        for a in range(n):
            for k, q in ((2, qx), (3, qy), (4, qd)):
                copy(a, k, _half(outs[a], q, 1 - c), sibling).wait_recv()
        for cp in sent:
            cp.wait_send()

    outs = pl.pallas_call(
        body, name="gather_first_rest", in_specs=[HBM_SPEC] * n, out_specs=[HBM_SPEC] * n,
        out_shape=[_sds(a) for a in arrays], input_output_aliases={a: a for a in range(n)},
        scratch_shapes=[pltpu.SemaphoreType.DMA((per * n,)), pltpu.SemaphoreType.DMA((per * n,))])(*landed_first)
    for k, o in zip(keys, outs):
        gat.slots[k] = o


def _small_all_gather(buf, done):
    state = {}

    def index(x, y, c):
        return 4 * x + 2 * y + c

    def plan_ici(ins, outs, place):
        x, y, c = place
        return [(ins[0], outs[0].at[index(x, y, c)], (px, py, c), outs[0].at[index(px, py, c)]) for (px, py) in _other_chips(x, y)]

    def local(ins, outs, place):
        return [(ins[0], outs[0].at[index(*place)])]

    def plan_d2d(ins, outs, place):
        x, y, c = place
        return [(ins[0].at[index(px, py, c)], outs[0].at[index(px, py, c)], (x, y, 1 - c), outs[0].at[index(px, py, 1 - c)])
                for (px, py) in [(x, y)] + _other_chips(x, y)]

    def second():
        return _Exchange([state["blocks"]], [_sds(state["blocks"])], plan_d2d, N_CHIPS, {0: 0}, lambda outs: done(outs[0]))

    first = _Exchange([buf], [jax.ShapeDtypeStruct((2 * N_CHIPS,) + buf.shape, buf.dtype)], plan_ici, 3, None,
                      lambda outs: state.update(blocks=outs[0]), local, 1)
    return first, second


WEIGHT_NAMES = ("ffn1_norm", "ffn1_w_in", "ffn1_w_out", "mix_norm", "ffn2_norm", "ffn2_w_in", "ffn2_w_out", "ab_w_in", "pool_w", "pool_b",
                "pool_scale", "conv_w", "conv_b", "conv_ln_g", "conv_ln_b", "ab_w_out", "sgu_w_in", "sgu_ln_g", "sgu_ln_b", "sgu_w", "sgu_b",
                "sgu_w_out", "final_norm")
SMALL = tuple(n for n in WEIGHT_NAMES if n not in BIG)
SHARDED_SMALL = ("conv_w", "sgu_ln_g", "sgu_ln_b")
PACK_ROWS = 64
PACK = ("pack", 0)


def _pair(prefix, layer):
    return [(prefix + "_w_in", layer), (prefix + "_w_out", layer)]


def kernel(x, ffn1_norm, ffn1_w_in, ffn1_w_out, mix_norm, ffn2_norm, ffn2_w_in, ffn2_w_out, ab_w_in, pool_w, pool_b, pool_scale, conv_w, conv_b, conv_ln_g, conv_ln_b, ab_w_out, sgu_w_in, sgu_ln_g, sgu_ln_b, sgu_w, sgu_b, sgu_w_out, final_norm, loss_target, m_ffn1_norm, m_ffn1_w_in, m_ffn1_w_out, m_mix_norm, m_ffn2_norm, m_ffn2_w_in, m_ffn2_w_out, m_ab_w_in, m_pool_w, m_pool_b, m_pool_scale, m_conv_w, m_conv_b, m_conv_ln_g, m_conv_ln_b, m_ab_w_out, m_sgu_w_in, m_sgu_ln_g, m_sgu_ln_b, m_sgu_w, m_sgu_b, m_sgu_w_out, m_final_norm, v_ffn1_norm, v_ffn1_w_in, v_ffn1_w_out, v_mix_norm, v_ffn2_norm, v_ffn2_w_in, v_ffn2_w_out, v_ab_w_in, v_pool_w, v_pool_b, v_pool_scale, v_conv_w, v_conv_b, v_conv_ln_g, v_conv_ln_b, v_ab_w_out, v_sgu_w_in, v_sgu_ln_g, v_sgu_ln_b, v_sgu_w, v_sgu_b, v_sgu_w_out, v_final_norm):
    given = dict(locals())
    w = {n: given[n] for n in WEIGHT_NAMES}
    chip = 2 * lax.axis_index("x") + lax.axis_index("y")
    me = chip.astype(jnp.int32).reshape(1)
    core = lax.axis_index("c").astype(jnp.int32).reshape(1)
    row = lambda v: v.reshape(1, -1)
    xin, tgt = x[0], loss_target[0]

    pack = jnp.concatenate([
        w["conv_w"][0], jnp.zeros((1, 128), F32), w["sgu_ln_g"].reshape(2, 128), w["sgu_ln_b"].reshape(2, 128),
        jnp.zeros((PACK_ROWS - 36, 128), F32)], axis=0)
    first = _pair("ffn1", 0)
    slots = {PACK: lax.dynamic_update_slice(jnp.zeros((N_CHIPS, PACK_ROWS, 128), F32), pack[None], (me[0], 0, 0))}
    for n, layer in first:
        slots[(n, layer)] = _cast_into_slot(w[n], layer, me, f"cast_{n}_{layer}")
    gat = _Gatherer(slots)

    def other_casts(token):
        for n in BIG:
            for layer in range(w[n].shape[0]):
                if (n, layer) not in first:
                    token = gat.slots[(n, layer)] = _cast_into_slot(w[n], layer, me, f"cast_{n}_{layer}", after=[token])
        return [token]

    _all_gather_full(gat, first + [PACK], other_casts)
    gp = gat.slots[PACK]
    conv_w_full = jnp.transpose(gp[:, 0:CONV_WIDTH], (1, 0, 2)).reshape(CONV_WIDTH, N_CHIPS * 128)
    sgu_ln_g_full = gp[:, 32:34].reshape(1, -1)
    sgu_ln_b_full = gp[:, 34:36].reshape(1, -1)
    gw = lambda n, layer: gat.slots[(n, layer)]

    st = [dict(), dict()]
    st[0]["xa"] = xin
    later = _pair("sgu", 0) + _pair("ffn2", 1)
    cur, st[0]["h1"], st[0]["xn1"] = _ffn_fwd(xin, row(w["ffn1_norm"][0]), gw("ffn1_w_in", 0), gw("ffn1_w_out", 0), "ffn1_fwd_0",
                                              comms=[gat.direct(_pair("ab", 0)), gat.ici(_pair("ffn2", 0))])
    st[0]["xb"] = cur
    st[0]["h0"], st[0]["xnm"] = _norm_matmul(cur, row(w["mix_norm"][0]), gw("ab_w_in", 0), "mix0_proj_in",
                                             comms=[gat.d2d(_pair("ffn2", 0)), gat.ici([("ffn1_w_out", 1)])])
    pool_args = (w["pool_w"][0], row(w["pool_b"][0]), row(w["pool_scale"][0]), conv_w_full, row(w["conv_b"][0]), row(w["conv_ln_g"][0]),
                 row(w["conv_ln_b"][0]), gw("ab_w_out", 0))
    cur, st[0]["ycat"], st[0]["yconv"] = _mix0_fwd(cur, st[0]["h0"], *pool_args, "mix0_fwd", comms=[gat.ici([("ffn1_w_in", 1)])])
    st[0]["xc"] = cur
    cur, st[0]["h2"], st[0]["xn2"] = _ffn_fwd(cur, row(w["ffn2_norm"][0]), gw("ffn2_w_in", 0), gw("ffn2_w_out", 0), "ffn2_fwd_0",
                                              comms=[gat.d2d(_pair("ffn1", 1)), gat.ici(later)])
    st[1]["xa"] = cur
    cur, st[1]["h1"], st[1]["xn1"] = _ffn_fwd(cur, row(w["ffn1_norm"][1]), gw("ffn1_w_in", 1), gw("ffn1_w_out", 1), "ffn1_fwd_1",
                                              comms=[gat.d2d(later)])
    st[1]["xb"] = cur
    st[1]["pre"], st[1]["xnm"] = _norm_matmul(cur, row(w["mix_norm"][1]), gw("sgu_w_in", 0), "sgu_proj_in")
    sgu_args = (sgu_ln_g_full, sgu_ln_b_full, w["sgu_w"][0], w["sgu_b"][0].T)
    cur, st[1]["p"] = _sgu_fwd(cur, st[1]["pre"], *sgu_args, gw("sgu_w_out", 0), "sgu_fwd")
    st[1]["xc"] = cur
    dy, st[1]["h2"], st[1]["xn2"], loss, d_final = _ffn_fwd(cur, row(w["ffn2_norm"][1]), gw("ffn2_w_in", 1), gw("ffn2_w_out", 1),
                                                            "ffn2_fwd_1_loss", loss_head=(tgt, row(w["final_norm"])))

    red = _Reducer(me, core)
    small = {"final_norm": d_final.reshape(-1)}
    norm_grads = {"ffn1_norm": [None] * DEPTH, "mix_norm": [None] * DEPTH, "ffn2_norm": [None] * DEPTH}
    ga, gb, gc, gd, ge, gf = _pair("ffn2", 1), _pair("sgu", 0), _pair("ffn1", 1), _pair("ffn2", 0), _pair("ab", 0), _pair("ffn1", 0)

    def ffn_backward(prefix, layer, xs, hs, xns, dy_in, bwd_comms=(), dwin_comms=(), dwout_comms=()):
        dx, dh, act, norm_grads[prefix + "_norm"][layer] = _ffn_bwd(
            xs, dy_in, hs, row(w[prefix + "_norm"][layer]), gw(prefix + "_w_in", layer), gw(prefix + "_w_out", layer),
            f"{prefix}_bwd_{layer}", comms=bwd_comms)
        red.add((prefix + "_w_in", layer), _tn_matmul(xns, dh, 1.0, 1024, 1408, f"{prefix}_dwin_{layer}", comms=dwin_comms))
        red.add((prefix + "_w_out", layer), _tn_matmul(act, dy_in, 0.5, 1408, 1024, f"{prefix}_dwout_{layer}", comms=dwout_comms))
        return dx

    s1, s0 = st[1], st[0]
    dy = ffn_backward("ffn2", 1, s1["xc"], s1["h2"], s1["xn2"], dy)
    dy_in = dy
    dy, dpre, norm_grads["mix_norm"][1], dlg, dlb, dw, dbt = _sgu_bwd(
        s1["xb"], dy_in, s1["pre"], row(w["mix_norm"][1]), *sgu_args, gw("sgu_w_in", 0), gw("sgu_w_out", 0), "sgu_bwd", comms=[red.swap(ga)])
    red.add(("sgu_w_in", 0), _tn_matmul(s1["xnm"], dpre, 1.0, 1024, 2048, "sgu_dwin"))
    red.add(("sgu_w_out", 0), _tn_matmul(s1["p"], dy_in, 1.0, 1024, 1024, "sgu_dwout"))
    small.update(sgu_ln_g=dlg, sgu_ln_b=dlb, sgu_w=dw[None], sgu_b=dbt.T[None])
    dy = ffn_backward("ffn1", 1, s1["xa"], s1["h1"], s1["xn1"], dy, bwd_comms=[lambda: red.scatter(ga), lambda: red.swap(gb)],
                      dwin_comms=[lambda: red.scatter(gb), lambda: red.exchange(ga)])
    dy = ffn_backward("ffn2", 0, s0["xc"], s0["h2"], s0["xn2"], dy, bwd_comms=[lambda: red.swap(gc), lambda: red.exchange(gb)],
                      dwin_comms=[lambda: red.scatter(gc)])
    dy_in = dy
    dconv, dpc, dpw, rows = _mix0_bwd_a(dy_in, s0["h0"], s0["yconv"], *pool_args, "mix0_bwd_a")
    dy, dh0, norm_grads["mix_norm"][0] = _mix0_bwd_b(s0["xb"], dy_in, s0["h0"], dconv, dpc, row(w["mix_norm"][0]), conv_w_full,
                                                      gw("ab_w_in", 0), "mix0_bwd_b")
    red.add(("ab_w_in", 0), _tn_matmul(s0["xnm"], dh0, 1.0, 1024, 1536, "ab_dwin", comms=[red.swap(gd), red.exchange(gc)]))
    small.update(pool_w=dpw[None], conv_w=rows[None, 0:CONV_WIDTH], conv_b=rows[32:33], conv_ln_g=rows[33:34], conv_ln_b=rows[34:35],
                 pool_scale=rows[35:36], pool_b=rows[36:37].reshape(1, len(POOL_WINDOWS), POOL_GC))

    last = {}

    def behind_ffn2_scatter(token):
        red.add(("ab_w_out", 0), _tn_matmul(s0["ycat"], dy_in, 1.0, 1024, 1024, "ab_dwout", after=[token]))
        last["dx"], last["dh"], last["act"], norm_grads["ffn1_norm"][0] = _ffn_bwd(
            s0["xa"], dy, s0["h1"], row(w["ffn1_norm"][0]), gw("ffn1_w_in", 0), gw("ffn1_w_out", 0), "ffn1_bwd_0", after=[token])
        return [norm_grads["ffn1_norm"][0]]

    red.scatter_behind(gd, behind_ffn2_scatter, "ffn2_0")

    small_sum = {}

    def small_ready():
        for k, v in norm_grads.items():
            small[k] = jnp.concatenate(v, axis=0)
        flat = [small[n].reshape(-1, 128) for n in SMALL]
        rows = sum(f.shape[0] for f in flat)
        loss_block = jnp.pad(loss, ((0, 8 + (-rows) % 8 - 1), (0, 127)))
        buf = jnp.concatenate(flat + [loss_block], axis=0)

        def done(gathered):
            total, at = _sum_leading(gathered, "reduce_small"), 0
            for n, f in zip(SMALL, flat):
                small_sum[n] = total[at:at + f.shape[0]].reshape(small[n].shape)
                at += f.shape[0]
            small_sum["loss"] = total[at:at + 1, 0:1]

        return _small_all_gather(buf, done)

    small_first, small_second = small_ready()
    red.add(("ffn1_w_in", 0), _tn_matmul(s0["xn1"], last["dh"], 1.0, 1024, 1408, "ffn1_dwin_0", comms=[red.swap(ge), small_first]))
    red.add(("ffn1_w_out", 0), _tn_matmul(last["act"], dy, 0.5, 1408, 1024, "ffn1_dwout_0",
                                          comms=[red.scatter(ge), red.exchange(gd), small_second, red.swap(gf[:1])]))
    grad_x = last["dx"]

    big_out = {}

    def adamw_big(n, layer, after=()):
        big_out[n] = _adamw_sharded(w[n], red.mine[(n, layer)], red.theirs[(n, layer)], given["m_" + n], given["v_" + n], core, layer,
                                    big_out.get(n), f"adamw_{n}_{layer}", after=after)

    _exchange_alone("swap_last_grads", [red.swap(gf[1:]), red.exchange(ge)])

    def other_updates(token):
        for n in BIG:
            for layer in reversed(range(w[n].shape[0])):
                if (n, layer) not in gf:
                    adamw_big(n, layer, after=[token])
        return [big_out[n][0] for n in BIG]

    red.scatter_behind(gf, other_updates, "last")
    _exchange_alone("exchange_last_grads", [red.exchange(gf)])
    for key in gf:
        adamw_big(*key)

    loss = small_sum["loss"][0, 0]
    grads, delta, new_m, new_v = {}, {}, {}, {}
    for n in WEIGHT_NAMES:
        mom, var = given["m_" + n], given["v_" + n]
        if n in BIG:
            grads[n], delta[n], new_m[n], new_v[n] = big_out[n]
            continue
        g = small_sum[n]
        if n in SHARDED_SMALL:
            width = w[n].shape[-1]
            g = lax.dynamic_slice_in_dim(g, chip * width, width, axis=g.ndim - 1)
        grads[n] = g
        delta[n], new_m[n], new_v[n] = _adamw(w[n], g, mom, var, f"adamw_{n}")
    return (loss, grad_x[None], *[grads[n] for n in WEIGHT_NAMES], *[delta[n] for n in WEIGHT_NAMES],
            *[new_m[n] for n in WEIGHT_NAMES], *[new_v[n] for n in WEIGHT_NAMES])
```
